```python
import jax, jax.numpy as jnp
from jax import lax
import numpy as np

D_MODEL = 1024
BATCH = 8
SEQ = 2048
DEPTH = 1
DEC_BATCH = 128
DEC_SEQ = 4
PAST_LEN = 16384
PAGE_SIZE = 128

POOL_WINDOWS = (2, 4, 8, 16)
POOL_GROUPS = len(POOL_WINDOWS)
POOL_GROUP_DIM = 128
POOL_DIM = POOL_GROUPS * POOL_GROUP_DIM
POOL_BUF = max(POOL_WINDOWS) - 1
N_HEADS = 8
HEAD_DK = 128
HEAD_DV = 128
QK_DIM = N_HEADS * HEAD_DK
V_DIM = N_HEADS * HEAD_DV
CONV_WIDTH = 4
CONV_DIM = 2 * QK_DIM + V_DIM
DELTA_CHUNK = 64
N_GROUPS = 4
EXPERTS_PER_GROUP = 8
N_EXPERTS = N_GROUPS * EXPERTS_PER_GROUP
TOP_K = 2
D_EXPERT = 256
IN_DIM = POOL_DIM + CONV_DIM + 2 * N_HEADS + V_DIM + 2 * D_MODEL
SPLITS = [int(s) for s in np.cumsum([POOL_DIM, CONV_DIM, N_HEADS, N_HEADS, V_DIM, D_MODEL])]
EPS = 1e-6

kernel_name = 'hybrid_pool_gdn_hmoe_step'


def rmsnorm(x, w):
    xf = x.astype(jnp.float32)
    y = xf * lax.rsqrt(jnp.mean(xf * xf, axis=-1, keepdims=True) + EPS)
    return (y * w.astype(jnp.float32)).astype(x.dtype)


def l2norm(x):
    xf = x.astype(jnp.float32)
    return xf * lax.rsqrt(jnp.sum(xf * xf, axis=-1, keepdims=True) + EPS)


def pool_mix(u, buf, pos0, w_pool, pool_scale):
    B, T, _ = u.shape
    ext = jnp.concatenate([buf.astype(u.dtype), u], axis=1)
    cs = jnp.cumsum(ext.astype(jnp.float32), axis=1)
    cs = jnp.pad(cs, ((0, 0), (1, 0), (0, 0)))
    pos = pos0 + jnp.arange(T)
    means = []
    for gi, win in enumerate(POOL_WINDOWS):
        sl = slice(gi * POOL_GROUP_DIM, (gi + 1) * POOL_GROUP_DIM)
        s = cs[:, POOL_BUF + 1:POOL_BUF + 1 + T, sl] - cs[:, POOL_BUF + 1 - win:POOL_BUF + 1 - win + T, sl]
        cnt = jnp.minimum(win, pos + 1).astype(jnp.float32)
        means.append(s / cnt[None, :, None])
    pooled = jnp.stack(means, axis=2) - u.astype(jnp.float32).reshape(B, T, POOL_GROUPS, POOL_GROUP_DIM)
    y = jnp.einsum('btgc,gcd->btgd', pooled.astype(u.dtype), w_pool)
    y = y * pool_scale.reshape(POOL_GROUPS, POOL_GROUP_DIM)
    return y.reshape(B, T, POOL_DIM), ext[:, -POOL_BUF:]


def causal_conv(u, buf, w_conv):
    T = u.shape[1]
    ext = jnp.concatenate([buf.astype(u.dtype), u], axis=1)
    out = ext[:, 0:T] * w_conv[0]
    for j in range(1, CONV_WIDTH):
        out = out + ext[:, j:j + T] * w_conv[j]
    return jax.nn.silu(out), ext[:, -(CONV_WIDTH - 1):]


def gated_delta_chunked(q, k, v, g, beta, s0, chunk):
    B, T, H, DK = q.shape
    DV = v.shape[-1]
    n = T // chunk
    f32 = jnp.float32

    def blocks(a):
        a = a.astype(f32).reshape((B, n, chunk) + a.shape[2:])
        return jnp.moveaxis(a, 2, 3)

    q = blocks(q) * DK ** -0.5
    k = blocks(k)
    v = blocks(v)
    g = blocks(g)
    beta = blocks(beta)
    G = jnp.cumsum(g, axis=-1)
    idx = jnp.arange(chunk)
    causal = idx[:, None] >= idx[None, :]
    strict = idx[:, None] > idx[None, :]
    decay = jnp.exp(jnp.where(causal, G[..., :, None] - G[..., None, :], -jnp.inf))
    kb = k * beta[..., None]
    A = jnp.einsum('bnhid,bnhjd->bnhij', kb, k) * decay * strict
    eye = jnp.eye(chunk, dtype=f32)
    t_inv = lax.linalg.triangular_solve(eye + A, jnp.broadcast_to(eye, A.shape),
                                        left_side=True, lower=True, unit_diagonal=True)
    w = jnp.einsum('bnhij,bnhjd->bnhid', t_inv, kb * jnp.exp(G)[..., None])
    u = jnp.einsum('bnhij,bnhjd->bnhid', t_inv, v * beta[..., None])
    q_dec = q * jnp.exp(G)[..., None]
    a_qk = jnp.einsum('bnhid,bnhjd->bnhij', q, k) * decay
    g_last = G[..., -1]
    k_dec = k * jnp.exp(g_last[..., None] - G)[..., None]

    def step(S, xs):
        w_c, u_c, qd_c, aqk_c, kd_c, gl_c = xs
        v_new = u_c - jnp.einsum('bhcd,bhde->bhce', w_c, S)
        o_c = jnp.einsum('bhcd,bhde->bhce', qd_c, S) + jnp.einsum('bhij,bhje->bhie', aqk_c, v_new)
        S = S * jnp.exp(gl_c)[..., None, None] + jnp.einsum('bhcd,bhce->bhde', kd_c, v_new)
        return S, o_c

    xs = (jnp.moveaxis(w, 1, 0), jnp.moveaxis(u, 1, 0), jnp.moveaxis(q_dec, 1, 0),
          jnp.moveaxis(a_qk, 1, 0), jnp.moveaxis(k_dec, 1, 0), jnp.moveaxis(g_last, 1, 0))
    S, o = lax.scan(step, s0.astype(f32), xs)
    o = jnp.moveaxis(jnp.moveaxis(o, 0, 1), 3, 2).reshape(B, T, H, DV)
    return o, S


def hier_moe(h, w_router_group, b_router_group, w_router_expert, b_router_expert, w_gate, w_up, w_down):
    hf = h.astype(jnp.float32)
    lg = hf @ w_router_group.astype(jnp.float32) + b_router_group.astype(jnp.float32)
    p_group = jax.nn.softmax(lg, axis=-1)
    gsel = jnp.argmax(lg, axis=-1)
    pg = jnp.take_along_axis(p_group, gsel[..., None], axis=-1)
    le = jnp.einsum('btd,dge->btge', hf, w_router_expert.astype(jnp.float32)) + b_router_expert.astype(jnp.float32)
    le_sel = jnp.take_along_axis(le, gsel[..., None, None], axis=2)[..., 0, :]
    top_v, top_i = lax.top_k(le_sel, TOP_K)
    w_k = jax.nn.softmax(top_v, axis=-1) * pg
    expert_id = gsel[..., None] * EXPERTS_PER_GROUP + top_i
    combine = jnp.sum(jax.nn.one_hot(expert_id, N_EXPERTS, dtype=jnp.float32) * w_k[..., None], axis=-2)
    a = jnp.einsum('btd,edf->btef', h, w_gate)
    b = jnp.einsum('btd,edf->btef', h, w_up)
    act = jax.nn.silu(a) * b * combine[..., None].astype(h.dtype)
    return jnp.einsum('btef,efd->btd', act, w_down)


def trunk_layer(x, pool_buf, conv_buf, s0, pos0, chunk, norm1_w, w_in, w_pool, pool_scale, w_conv, a_log,
                dt_bias, o_norm_w, w_branch_pool, w_branch_delta, w_out, norm2_w, w_router_group,
                b_router_group, w_router_expert, b_router_expert, w_gate, w_up, w_down):
    B, T, _ = x.shape
    f32 = jnp.float32
    h = rmsnorm(x, norm1_w)
    proj = jnp.einsum('btd,de->bte', h, w_in)
    u_pool, u_qkv, b_raw, a_raw, z, gate_pool, gate_delta = jnp.split(proj, SPLITS, axis=-1)
    y_pool, pool_buf_new = pool_mix(u_pool, pool_buf, pos0, w_pool, pool_scale)
    qkv, conv_buf_new = causal_conv(u_qkv, conv_buf, w_conv)
    q, k, v = jnp.split(qkv, [QK_DIM, 2 * QK_DIM], axis=-1)
    q = l2norm(q.reshape(B, T, N_HEADS, HEAD_DK))
    k = l2norm(k.reshape(B, T, N_HEADS, HEAD_DK))
    v = v.reshape(B, T, N_HEADS, HEAD_DV)
    beta = jax.nn.sigmoid(b_raw.astype(f32))
    g = -jnp.exp(a_log.astype(f32)) * jax.nn.softplus(a_raw.astype(f32) + dt_bias.astype(f32))
    o, s_new = gated_delta_chunked(q, k, v, g, beta, s0, chunk)
    o = rmsnorm(o, o_norm_w) * jax.nn.silu(z.reshape(B, T, N_HEADS, HEAD_DV).astype(f32))
    y_delta = o.reshape(B, T, V_DIM).astype(x.dtype)
    branch_pool = jnp.einsum('btc,cd->btd', y_pool, w_branch_pool)
    branch_delta = jnp.einsum('btc,cd->btd', y_delta, w_branch_delta)
    merged = jax.nn.sigmoid(gate_pool) * branch_pool + jax.nn.sigmoid(gate_delta) * branch_delta
    x = x + jnp.einsum('btd,de->bte', merged, w_out)
    x = x + hier_moe(rmsnorm(x, norm2_w), w_router_group, b_router_group, w_router_expert,
                     b_router_expert, w_gate, w_up, w_down)
    return x, pool_buf_new, conv_buf_new, s_new.astype(x.dtype)


def setup_inputs(seed: int = 0) -> dict:
    key = jax.random.key(seed)
    ks = jax.random.split(key, 28)
    f32 = jnp.float32
    L = DEPTH

    def nrm(k, shape, scale):
        return jax.random.normal(k, shape, f32) * scale

    dt = jax.random.uniform(ks[11], (L, N_HEADS), f32, 0.001, 0.1)
    return {
        'x_prompt': nrm(ks[0], (BATCH, SEQ, D_MODEL), 1.0),
        'x_sample': nrm(ks[1], (DEC_BATCH, DEC_SEQ, D_MODEL), 1.0),
        'cache_pool': nrm(ks[2], (L, DEC_BATCH, POOL_BUF, POOL_DIM), 1.0),
        'cache_conv': nrm(ks[3], (L, DEC_BATCH, CONV_WIDTH - 1, CONV_DIM), 1.0),
        'state_delta': nrm(ks[4], (L, DEC_BATCH, N_HEADS, HEAD_DK, HEAD_DV), 0.3),
        'norm1_w': 1.0 + nrm(ks[5], (L, D_MODEL), 0.02),
        'w_in': nrm(ks[6], (L, D_MODEL, IN_DIM), D_MODEL ** -0.5),
        'w_pool': nrm(ks[7], (L, POOL_GROUPS, POOL_GROUP_DIM, POOL_GROUP_DIM), POOL_GROUP_DIM ** -0.5),
        'pool_scale': 1.0 + nrm(ks[8], (L, POOL_DIM), 0.1),
        'w_conv': nrm(ks[9], (L, CONV_WIDTH, CONV_DIM), CONV_WIDTH ** -0.5),
        'a_log': jnp.log(jax.random.uniform(ks[10], (L, N_HEADS), f32, 1.0, 16.0)),
        'dt_bias': jnp.log(jnp.expm1(dt)),
        'o_norm_w': 1.0 + nrm(ks[12], (L, HEAD_DV), 0.02),
        'w_branch_pool': nrm(ks[13], (L, POOL_DIM, D_MODEL), POOL_DIM ** -0.5),
        'w_branch_delta': nrm(ks[14], (L, V_DIM, D_MODEL), V_DIM ** -0.5),
        'w_out': nrm(ks[15], (L, D_MODEL, D_MODEL), D_MODEL ** -0.5),
        'norm2_w': 1.0 + nrm(ks[16], (L, D_MODEL), 0.02),
        'w_router_group': nrm(ks[17], (L, D_MODEL, N_GROUPS), D_MODEL ** -0.5),
        'b_router_group': nrm(ks[18], (L, N_GROUPS), 0.01),
        'w_router_expert': nrm(ks[19], (L, D_MODEL, N_GROUPS, EXPERTS_PER_GROUP), D_MODEL ** -0.5),
        'b_router_expert': nrm(ks[20], (L, N_GROUPS, EXPERTS_PER_GROUP), 0.01),
        'w_gate': nrm(ks[21], (L, N_EXPERTS, D_MODEL, D_EXPERT), D_MODEL ** -0.5),
        'w_up': nrm(ks[22], (L, N_EXPERTS, D_MODEL, D_EXPERT), D_MODEL ** -0.5),
        'w_down': nrm(ks[23], (L, N_EXPERTS, D_EXPERT, D_MODEL), D_EXPERT ** -0.5),
        'final_norm_w': 1.0 + nrm(ks[24], (D_MODEL,), 0.02),
    }


def reference(x_prompt, x_sample, cache_pool, cache_conv, state_delta, norm1_w, w_in, w_pool, pool_scale,
              w_conv, a_log, dt_bias, o_norm_w, w_branch_pool, w_branch_delta, w_out, norm2_w,
              w_router_group, b_router_group, w_router_expert, b_router_expert, w_gate, w_up, w_down,
              final_norm_w):
    B, T = x_prompt.shape[0], x_prompt.shape[1]
    dt = x_prompt.dtype
    xp, xs = x_prompt, x_sample
    pool_p, conv_p, delta_p, pool_s, conv_s, delta_s = [], [], [], [], [], []
    for layer in range(DEPTH):
        lp = (norm1_w[layer], w_in[layer], w_pool[layer], pool_scale[layer], w_conv[layer], a_log[layer],
              dt_bias[layer], o_norm_w[layer], w_branch_pool[layer], w_branch_delta[layer], w_out[layer],
              norm2_w[layer], w_router_group[layer], b_router_group[layer], w_router_expert[layer],
              b_router_expert[layer], w_gate[layer], w_up[layer], w_down[layer])
        xp, pb, cb, sb = trunk_layer(
            xp, jnp.zeros((B, POOL_BUF, POOL_DIM), dt), jnp.zeros((B, CONV_WIDTH - 1, CONV_DIM), dt),
            jnp.zeros((B, N_HEADS, HEAD_DK, HEAD_DV), jnp.float32), 0, min(DELTA_CHUNK, T), *lp)
        pool_p.append(pb)
        conv_p.append(cb)
        delta_p.append(sb)
        xs, pb, cb, sb = trunk_layer(
            xs, cache_pool[layer], cache_conv[layer], state_delta[layer], PAST_LEN, xs.shape[1], *lp)
        pool_s.append(pb)
        conv_s.append(cb)
        delta_s.append(sb)
    y_prompt = rmsnorm(xp, final_norm_w)
    y_sample = rmsnorm(xs, final_norm_w)
    return (y_prompt, y_sample, jnp.stack(pool_p), jnp.stack(conv_p), jnp.stack(delta_p),
            jnp.stack(pool_s), jnp.stack(conv_s), jnp.stack(delta_s))
```

```python
import functools

import jax
import jax.numpy as jnp
from jax import lax
from jax.experimental import pallas as pl
from jax.experimental.pallas import tpu as pltpu

F32 = jnp.float32
BF16 = jnp.bfloat16
EPS = 1e-6
HIGHEST = lax.Precision.HIGHEST

POOL_WINDOWS = (2, 4, 8, 16)
POOL_GROUP_DIM = 128
POOL_DIM = len(POOL_WINDOWS) * POOL_GROUP_DIM
POOL_BUF = max(POOL_WINDOWS) - 1
N_HEADS = 8
HEAD_DIM = 128
QK_DIM = N_HEADS * HEAD_DIM
CONV_WIDTH = 4
CONV_DIM = 3 * QK_DIM
N_GROUPS = 4
EXPERTS_PER_GROUP = 8
N_EXPERTS = N_GROUPS * EXPERTS_PER_GROUP
LANES = 128
SUBLANES = 8

OFF_GP = 0
OFF_GD = OFF_GP + 1024
OFF_Z = OFF_GD + 1024
OFF_QKV = OFF_Z + 1024
OFF_POOL = OFF_QKV + CONV_DIM
OFF_B = OFF_POOL + POOL_DIM
OFF_A = OFF_B + LANES
IN_COLS = OFF_A + LANES

POOL_ROWS = 16
CONV_ROWS = 8
VMEM_LIMIT = 56 * 1024 * 1024


def _sigmoid(x):
    return 1.0 / (1.0 + jnp.exp(-x))


def _softplus(x):
    return jnp.maximum(x, 0.0) + jnp.log1p(jnp.exp(-jnp.abs(x)))


def _rms(x, w):
    return x * lax.rsqrt(jnp.mean(x * x, axis=-1, keepdims=True) + EPS) * w


def _dot(a, b):
    return jnp.dot(a.astype(BF16), b.astype(BF16), preferred_element_type=F32)


def _dot_nt(a, b):
    return lax.dot_general(a.astype(BF16), b.astype(BF16), (((1,), (1,)), ((), ())), preferred_element_type=F32)


def _dot_tn(a, b):
    return lax.dot_general(a.astype(BF16), b.astype(BF16), (((0,), (0,)), ((), ())), preferred_element_type=F32)


def _inproj_kernel(x_ref, nw_ref, w_ref, o_ref, *, col_chunk):
    h = _rms(x_ref[...], nw_ref[...]).astype(BF16)
    for c0 in range(0, IN_COLS, col_chunk):
        o_ref[:, c0:c0 + col_chunk] = jnp.dot(h, w_ref[:, c0:c0 + col_chunk], preferred_element_type=F32)


def _inproj(x, norm_w, w_in_r, tm):
    n, d = x.shape
    return pl.pallas_call(
        functools.partial(_inproj_kernel, col_chunk=768),
        grid=(n // tm,),
        in_specs=[
            pl.BlockSpec((tm, d), lambda i: (i, 0)),
            pl.BlockSpec((1, d), lambda i: (0, 0)),
            pl.BlockSpec((d, IN_COLS), lambda i: (0, 0)),
        ],
        out_specs=pl.BlockSpec((tm, IN_COLS), lambda i: (i, 0)),
        out_shape=jax.ShapeDtypeStruct((n, IN_COLS), F32),
        compiler_params=pltpu.CompilerParams(dimension_semantics=("arbitrary",), vmem_limit_bytes=VMEM_LIMIT),
        name="inproj",
    )(x, norm_w, w_in_r)


def _mixer_kernel(proj_ref, pool0_ref, conv0_ref, s0_ref, wpool_ref, pscale_ref, wconv_ref, alog_ref, dtb_ref, onw_ref,
                  ypool_ref, ydelta_ref, npool_ref, nconv_ref, ns_ref,
                  ext_p, ext_c, q_s, k_s, v_s, g_s, b_s, state,
                  *, tt, tv, chunk, pos0, n_tiles):
    t = pl.program_id(1)

    @pl.when(t == 0)
    def _():
        ext_p[0:POOL_ROWS, :] = pool0_ref[0]
        ext_c[0:CONV_ROWS, :] = conv0_ref[0]
        state[...] = s0_ref[0]

    ext_p[POOL_ROWS:POOL_ROWS + tt, :] = proj_ref[:, OFF_POOL:OFF_POOL + POOL_DIM]
    ext_c[CONV_ROWS:CONV_ROWS + tt, :] = proj_ref[:, OFF_QKV:OFF_QKV + CONV_DIM]

    row = lax.broadcasted_iota(jnp.int32, (tt, 1), 0)

    pos = pos0 + t * tt + row
    for gi, win in enumerate(POOL_WINDOWS):
        cs = slice(gi * POOL_GROUP_DIM, (gi + 1) * POOL_GROUP_DIM)
        u = ext_p[POOL_ROWS:POOL_ROWS + tt, cs]
        acc = u
        for j in range(1, win):
            acc = acc + ext_p[POOL_ROWS - j:POOL_ROWS - j + tt, cs]
        cnt = jnp.minimum(win, pos + 1).astype(F32)
        pooled = acc / cnt - u
        ypool_ref[:, cs] = _dot(pooled, wpool_ref[gi]) * pscale_ref[:, cs]

    base = CONV_ROWS - (CONV_WIDTH - 1)
    for ci in range(CONV_DIM // LANES):
        cs = slice(ci * LANES, (ci + 1) * LANES)
        acc = ext_c[base:base + tt, cs] * wconv_ref[0:1, cs]
        for j in range(1, CONV_WIDTH):
            acc = acc + ext_c[base + j:base + j + tt, cs] * wconv_ref[j:j + 1, cs]
        y = acc * _sigmoid(acc)
        hs = slice((ci % N_HEADS) * HEAD_DIM, (ci % N_HEADS + 1) * HEAD_DIM)
        if ci < 2 * N_HEADS:
            y = y * lax.rsqrt(jnp.sum(y * y, axis=-1, keepdims=True) + EPS)
        if ci < N_HEADS:
            q_s[:, hs] = y * (HEAD_DIM ** -0.5)
        elif ci < 2 * N_HEADS:
            k_s[:, hs] = y
        else:
            v_s[:, hs] = y

    beta = _sigmoid(proj_ref[:, OFF_B:OFF_B + LANES])
    g = -jnp.exp(alog_ref[...]) * _softplus(proj_ref[:, OFF_A:OFF_A + LANES] + dtb_ref[...])
    if tv < tt:
        beta = jnp.where(row < tv, beta, 0.0)
        g = jnp.where(row < tv, g, 0.0)
    b_s[...] = beta
    g_s[...] = g

    @pl.when(t == n_tiles - 1)
    def _():
        npool_ref[0] = ext_p[tv:tv + POOL_ROWS, :]
        nconv_ref[0] = ext_c[tv:tv + CONV_ROWS, :]

    if n_tiles > 1:
        ext_p[0:POOL_ROWS, :] = ext_p[tt:tt + POOL_ROWS, :]
        ext_c[0:CONV_ROWS, :] = ext_c[tt:tt + CONV_ROWS, :]

    ri = lax.broadcasted_iota(jnp.int32, (chunk, chunk), 0)
    cj = lax.broadcasted_iota(jnp.int32, (chunk, chunk), 1)
    causal = ri >= cj
    strict = ri > cj
    ltri = causal.astype(F32)
    eye = (ri == cj).astype(F32)
    n_doublings = chunk.bit_length() - 2

    def chunk_body(ci, carry):
        r0 = pl.multiple_of(ci * chunk, chunk)
        rows = pl.ds(r0, chunk)
        g_all = jnp.dot(ltri, g_s[rows, :], precision=HIGHEST, preferred_element_type=F32)
        if chunk < LANES:
            g_sq = jnp.concatenate([g_all, jnp.zeros((LANES - chunk, LANES), F32)], axis=0)
        else:
            g_sq = g_all
        g_t = g_sq.T
        b_all = b_s[rows, :]
        for h in range(N_HEADS):
            hs = slice(h * HEAD_DIM, (h + 1) * HEAD_DIM)
            g_col = g_all[:, h:h + 1]
            g_row = g_t[h:h + 1, 0:chunk]
            dec = jnp.exp(jnp.minimum(g_col - g_row, 0.0))
            b_col = b_all[:, h:h + 1]
            q_h = q_s[rows, hs]
            k_h = k_s[rows, hs]
            v_h = v_s[rows, hs]
            kb = k_h * b_col
            a_mat = jnp.where(strict, _dot_nt(kb, k_h) * dec, 0.0)
            t_inv = eye - a_mat
            p_mat = a_mat
            for _ in range(n_doublings):
                p_mat = _dot(p_mat, p_mat)
                t_inv = t_inv + _dot(t_inv, p_mat)
            g_exp = jnp.exp(g_col)
            wu = _dot(t_inv, jnp.concatenate([kb * g_exp, v_h * b_col], axis=1))
            w_c = wu[:, 0:HEAD_DIM]
            u_c = wu[:, HEAD_DIM:2 * HEAD_DIM]
            a_qk = jnp.where(causal, _dot_nt(q_h, k_h) * dec, 0.0)
            g_last = g_all[chunk - 1:chunk, h:h + 1]
            k_dec = k_h * jnp.exp(g_last - g_col)
            s_h = state[h]
            res = _dot(jnp.concatenate([w_c, q_h * g_exp], axis=0), s_h)
            v_new = u_c - res[0:chunk]
            o_c = res[chunk:2 * chunk] + _dot(a_qk, v_new)
            state[h] = s_h * jnp.exp(g_last) + _dot_tn(k_dec, v_new)
            z_c = proj_ref[rows, OFF_Z + h * HEAD_DIM:OFF_Z + (h + 1) * HEAD_DIM]
            ydelta_ref[rows, hs] = _rms(o_c, onw_ref[...]) * (z_c * _sigmoid(z_c))
        return carry

    lax.fori_loop(0, tt // chunk, chunk_body, 0)

    @pl.when(t == n_tiles - 1)
    def _():
        ns_ref[0] = state[...]


def _mixers(proj, pool0, conv0, s0, wpool, pscale, wconv, alog, dtb, onw, *, n_seq, tt, tv, chunk, pos0, n_tiles):
    kern = functools.partial(_mixer_kernel, tt=tt, tv=tv, chunk=chunk, pos0=pos0, n_tiles=n_tiles)
    n = n_seq * n_tiles * tt
    tile = lambda b, t: (b * n_tiles + t, 0)
    seq3 = lambda b, t: (b, 0, 0)
    const2 = lambda b, t: (0, 0)
    return pl.pallas_call(
        kern,
        grid=(n_seq, n_tiles),
        in_specs=[
            pl.BlockSpec((tt, IN_COLS), tile),
            pl.BlockSpec((1, POOL_ROWS, POOL_DIM), seq3),
            pl.BlockSpec((1, CONV_ROWS, CONV_DIM), seq3),
            pl.BlockSpec((1, N_HEADS, HEAD_DIM, HEAD_DIM), lambda b, t: (b, 0, 0, 0)),
            pl.BlockSpec((len(POOL_WINDOWS), POOL_GROUP_DIM, POOL_GROUP_DIM), lambda b, t: (0, 0, 0)),
            pl.BlockSpec((1, POOL_DIM), const2),
            pl.BlockSpec((SUBLANES, CONV_DIM), const2),
            pl.BlockSpec((1, LANES), const2),
            pl.BlockSpec((1, LANES), const2),
            pl.BlockSpec((1, HEAD_DIM), const2),
        ],
        out_specs=[
            pl.BlockSpec((tt, POOL_DIM), tile),
            pl.BlockSpec((tt, QK_DIM), tile),
            pl.BlockSpec((1, POOL_ROWS, POOL_DIM), seq3),
            pl.BlockSpec((1, CONV_ROWS, CONV_DIM), seq3),
            pl.BlockSpec((1, N_HEADS, HEAD_DIM, HEAD_DIM), lambda b, t: (b, 0, 0, 0)),
        ],
        out_shape=[
            jax.ShapeDtypeStruct((n, POOL_DIM), F32),
            jax.ShapeDtypeStruct((n, QK_DIM), F32),
            jax.ShapeDtypeStruct((n_seq, POOL_ROWS, POOL_DIM), F32),
            jax.ShapeDtypeStruct((n_seq, CONV_ROWS, CONV_DIM), F32),
            jax.ShapeDtypeStruct((n_seq, N_HEADS, HEAD_DIM, HEAD_DIM), F32),
        ],
        scratch_shapes=[
            pltpu.VMEM((POOL_ROWS + tt, POOL_DIM), F32),
            pltpu.VMEM((CONV_ROWS + tt, CONV_DIM), F32),
            pltpu.VMEM((tt, QK_DIM), F32),
            pltpu.VMEM((tt, QK_DIM), F32),
            pltpu.VMEM((tt, QK_DIM), F32),
            pltpu.VMEM((tt, LANES), F32),
            pltpu.VMEM((tt, LANES), F32),
            pltpu.VMEM((N_HEADS, HEAD_DIM, HEAD_DIM), F32),
        ],
        compiler_params=pltpu.CompilerParams(dimension_semantics=("arbitrary", "arbitrary"),
                                             vmem_limit_bytes=VMEM_LIMIT),
        name="mixers",
    )(proj, pool0, conv0, s0, wpool, pscale, wconv, alog, dtb, onw)


def _merge_kernel(x_ref, gates_ref, ypool_ref, ydelta_ref, wbp_ref, wbd_ref, wout_ref, n2w_ref, wr_ref, br_ref,
                  xm_ref, h2_ref, eid_ref, wk_ref):
    tm = x_ref.shape[0]
    bp = _dot(ypool_ref[...], wbp_ref[...])
    bd = _dot(ydelta_ref[...], wbd_ref[...])
    merged = _sigmoid(gates_ref[:, 0:1024]) * bp + _sigmoid(gates_ref[:, 1024:2048]) * bd
    xm = x_ref[...] + _dot(merged, wout_ref[...])
    xm_ref[...] = xm
    h2 = _rms(xm, n2w_ref[...])
    h2_ref[...] = h2

    logits = jnp.dot(h2, wr_ref[...], precision=HIGHEST, preferred_element_type=F32) + br_ref[...]
    lane = lax.broadcasted_iota(jnp.int32, logits.shape, 1)
    neg = -jnp.inf
    far = LANES - 1
    is_g = lane < N_GROUPS
    g_max = jnp.max(jnp.where(is_g, logits, neg), axis=-1, keepdims=True)
    g_sel = jnp.min(jnp.where(is_g & (logits == g_max), lane, far), axis=-1, keepdims=True)
    p_g = 1.0 / jnp.sum(jnp.where(is_g, jnp.exp(logits - g_max), 0.0), axis=-1, keepdims=True)
    e_lane = lane - N_GROUPS
    is_e = (e_lane >= 0) & (e_lane < N_EXPERTS) & ((e_lane // EXPERTS_PER_GROUP) == g_sel)
    ev = jnp.where(is_e, logits, neg)
    v1 = jnp.max(ev, axis=-1, keepdims=True)
    i1 = jnp.min(jnp.where(is_e & (ev == v1), lane, far), axis=-1, keepdims=True)
    is_e2 = is_e & (lane != i1)
    ev2 = jnp.where(is_e2, logits, neg)
    v2 = jnp.max(ev2, axis=-1, keepdims=True)
    i2 = jnp.min(jnp.where(is_e2 & (ev2 == v2), lane, far), axis=-1, keepdims=True)
    e21 = jnp.exp(v2 - v1)
    w1 = p_g / (1.0 + e21)
    w2 = p_g * e21 / (1.0 + e21)
    lane8 = lax.broadcasted_iota(jnp.int32, (tm, SUBLANES), 1)
    eid_ref[...] = jnp.where(lane8 == 0, i1 - N_GROUPS, jnp.where(lane8 == 1, i2 - N_GROUPS, 0))
    wk_ref[...] = jnp.where(lane8 == 0, w1, jnp.where(lane8 == 1, w2, 0.0))


def _merge(x, proj, ypool, ydelta, wbp, wbd, wout, n2w, wr, br, tm):
    n, d = x.shape
    row = lambda i: (i, 0)
    const = lambda i: (0, 0)
    return pl.pallas_call(
        _merge_kernel,
        grid=(n // tm,),
        in_specs=[
            pl.BlockSpec((tm, d), row),
            pl.BlockSpec((tm, 2048), row),
            pl.BlockSpec((tm, POOL_DIM), row),
            pl.BlockSpec((tm, QK_DIM), row),
            pl.BlockSpec((POOL_DIM, d), const),
            pl.BlockSpec((QK_DIM, d), const),
            pl.BlockSpec((d, d), const),
            pl.BlockSpec((1, d), const),
            pl.BlockSpec((d, LANES), const),
            pl.BlockSpec((1, LANES), const),
        ],
        out_specs=[
            pl.BlockSpec((tm, d), row),
            pl.BlockSpec((tm, d), row),
            pl.BlockSpec((tm, SUBLANES), row),
            pl.BlockSpec((tm, SUBLANES), row),
        ],
        out_shape=[
            jax.ShapeDtypeStruct((n, d), F32),
            jax.ShapeDtypeStruct((n, d), F32),
            jax.ShapeDtypeStruct((n, SUBLANES), jnp.int32),
            jax.ShapeDtypeStruct((n, SUBLANES), F32),
        ],
        compiler_params=pltpu.CompilerParams(dimension_semantics=("arbitrary",), vmem_limit_bytes=VMEM_LIMIT),
        name="merge",
    )(x, proj, ypool, ydelta, wbp, wbd, wout, n2w, wr, br)


def _expert_kernel(te_ref, na_ref, idx_ref, idx_next_ref, ws_ref, h2_hbm, wg_ref, wu_ref, wd_ref, y_ref, buf, sem, *, tm):
    i = pl.program_id(0)
    n_active = na_ref[0]
    slot = lax.rem(i, 2)

    def gather(src_idx_ref, s):
        def body(r, carry):
            tok = src_idx_ref[0, 0, r]
            pltpu.make_async_copy(h2_hbm.at[pl.ds(tok, 1), :], buf.at[s, pl.ds(r, 1), :], sem.at[s]).start()
            return carry
        lax.fori_loop(0, tm, body, 0, unroll=8)

    @pl.when(i == 0)
    def _():
        gather(idx_ref, 0)

    @pl.when(i + 1 < n_active)
    def _():
        gather(idx_next_ref, 1 - slot)

    @pl.when(i < n_active)
    def _():
        pltpu.make_async_copy(h2_hbm.at[pl.ds(0, tm), :], buf.at[slot], sem.at[slot]).wait()
        x = buf[slot]
        a = _dot(x, wg_ref[0])
        b = _dot(x, wu_ref[0])
        act = (a * _sigmoid(a)) * b * ws_ref[...]
        y_ref[...] = _dot(act, wd_ref[0])

    @pl.when(i >= n_active)
    def _():
        y_ref[...] = jnp.zeros_like(y_ref)


def _experts(tile_expert, n_active, src_tok, w_sorted, h2, wg, wu, wd, tm):
    n_tiles = src_tok.shape[0]
    d = h2.shape[1]
    f = wg.shape[2]
    grid_spec = pltpu.PrefetchScalarGridSpec(
        num_scalar_prefetch=2,
        grid=(n_tiles,),
        in_specs=[
            pl.BlockSpec((1, 1, tm), lambda i, te, na: (i, 0, 0), memory_space=pltpu.SMEM),
            pl.BlockSpec((1, 1, tm), lambda i, te, na: (jnp.minimum(i + 1, n_tiles - 1), 0, 0),
                         memory_space=pltpu.SMEM),
            pl.BlockSpec((tm, 1), lambda i, te, na: (i, 0)),
            pl.BlockSpec(memory_space=pl.ANY),
            pl.BlockSpec((1, d, f), lambda i, te, na: (te[i], 0, 0)),
            pl.BlockSpec((1, d, f), lambda i, te, na: (te[i], 0, 0)),
            pl.BlockSpec((1, f, d), lambda i, te, na: (te[i], 0, 0)),
        ],
        out_specs=pl.BlockSpec((tm, d), lambda i, te, na: (i, 0)),
        scratch_shapes=[pltpu.VMEM((2, tm, d), F32), pltpu.SemaphoreType.DMA((2,))],
    )
    return pl.pallas_call(
        functools.partial(_expert_kernel, tm=tm),
        grid_spec=grid_spec,
        out_shape=jax.ShapeDtypeStruct((n_tiles * tm, d), F32),
        compiler_params=pltpu.CompilerParams(dimension_semantics=("arbitrary",), vmem_limit_bytes=VMEM_LIMIT),
        name="experts",
    )(tile_expert, n_active, src_tok, src_tok, w_sorted, h2, wg, wu, wd)


def _combine_kernel(pos_ref, xm_ref, fnw_ref, ys_hbm, o_ref, buf, sem, *, tm):
    def body(r, carry):
        for k in range(2):
            p = pos_ref[0, 0, 2 * r + k]
            pltpu.make_async_copy(ys_hbm.at[pl.ds(p, 1), :], buf.at[k, pl.ds(r, 1), :], sem.at[k]).start()
        return carry
    lax.fori_loop(0, tm, body, 0, unroll=8)
    for k in range(2):
        pltpu.make_async_copy(ys_hbm.at[pl.ds(0, tm), :], buf.at[k], sem.at[k]).wait()
    o_ref[...] = _rms(xm_ref[...] + buf[0] + buf[1], fnw_ref[...])


def _combine(pos, xm, fnw, y_sorted, tm):
    n, d = xm.shape
    return pl.pallas_call(
        functools.partial(_combine_kernel, tm=tm),
        grid=(n // tm,),
        in_specs=[
            pl.BlockSpec((1, 1, 2 * tm), lambda i: (i, 0, 0), memory_space=pltpu.SMEM),
            pl.BlockSpec((tm, d), lambda i: (i, 0)),
            pl.BlockSpec((1, d), lambda i: (0, 0)),
            pl.BlockSpec(memory_space=pl.ANY),
        ],
        out_specs=pl.BlockSpec((tm, d), lambda i: (i, 0)),
        out_shape=jax.ShapeDtypeStruct((n, d), F32),
        scratch_shapes=[pltpu.VMEM((2, tm, d), F32), pltpu.SemaphoreType.DMA((2,))],
        compiler_params=pltpu.CompilerParams(dimension_semantics=("arbitrary",), vmem_limit_bytes=VMEM_LIMIT),
        name="combine",
    )(pos, xm, fnw, y_sorted)


def _moe(xm, h2, eid, wk, wg, wu, wd, fnw, *, tm_e, tm_c):
    n = xm.shape[0]
    flat_e = eid.reshape(-1)
    onehot = (flat_e[:, None] == jnp.arange(N_EXPERTS, dtype=jnp.int32)[None, :]).astype(jnp.int32)
    csum = jnp.cumsum(onehot, axis=0)
    rank = jnp.take_along_axis(csum, flat_e[:, None], axis=1)[:, 0] - 1
    counts = csum[-1]
    padded = ((counts + tm_e - 1) // tm_e) * tm_e
    ends = jnp.cumsum(padded)
    pos = (ends - padded)[flat_e] + rank
    n_rows = 2 * n + N_EXPERTS * tm_e
    n_tiles = n_rows // tm_e
    tile_expert = jnp.minimum(
        jnp.searchsorted(ends, jnp.arange(n_tiles, dtype=jnp.int32) * tm_e, side="right"), N_EXPERTS - 1
    ).astype(jnp.int32)
    n_active = (ends[-1] // tm_e).astype(jnp.int32).reshape(1)
    tok = jnp.arange(2 * n, dtype=jnp.int32) // 2
    src_tok = jnp.zeros((n_rows,), jnp.int32).at[pos].set(tok).reshape(n_tiles, 1, tm_e)
    w_sorted = jnp.zeros((n_rows,), F32).at[pos].set(wk.reshape(-1)).reshape(n_rows, 1)
    y_sorted = _experts(tile_expert, n_active, src_tok, w_sorted, h2, wg, wu, wd, tm_e)
    return _combine(pos.astype(jnp.int32).reshape(n // tm_c, 1, 2 * tm_c), xm, fnw, y_sorted, tm_c)


def _layer(x_tokens, pool0, conv0, s0, prm, *, n_seq, tt, tv, chunk, pos0, n_tiles, tm):
    proj = _inproj(x_tokens, prm["norm1_w"], prm["w_in"], tm)
    ypool, ydelta, npool, nconv, ns = _mixers(
        proj, pool0, conv0, s0, prm["w_pool"], prm["pool_scale"], prm["w_conv"], prm["a_log"], prm["dt_bias"],
        prm["o_norm_w"], n_seq=n_seq, tt=tt, tv=tv, chunk=chunk, pos0=pos0, n_tiles=n_tiles)
    xm, h2, eid, wk = _merge(x_tokens, proj, ypool, ydelta, prm["w_branch_pool"], prm["w_branch_delta"],
                             prm["w_out"], prm["norm2_w"], prm["w_router"], prm["b_router"], tm)
    return xm, h2, eid, wk, npool, nconv, ns


def _pad_lanes(v, width=LANES):
    v = v.reshape(1, -1).astype(F32)
    return jnp.pad(v, ((0, 0), (0, width - v.shape[1])))


def kernel(x_prompt, x_sample, cache_pool, cache_conv, state_delta, norm1_w, w_in, w_pool, pool_scale, w_conv, a_log, dt_bias, o_norm_w, w_branch_pool, w_branch_delta, w_out, norm2_w, w_router_group, b_router_group, w_router_expert, b_router_expert, w_gate, w_up, w_down, final_norm_w):
    n_layers = norm1_w.shape[0]
    assert n_layers == 1, "single-layer step"
    bsz, seq, d = x_prompt.shape
    dbs, dseq, _ = x_sample.shape
    past_len = 16384
    lyr = 0

    wi = w_in[lyr]
    c = [POOL_DIM, POOL_DIM + CONV_DIM]
    c += [c[1] + N_HEADS, c[1] + 2 * N_HEADS]
    c += [c[3] + QK_DIM, c[3] + QK_DIM + d]
    pad8 = lambda m: jnp.pad(m, ((0, 0), (0, LANES - N_HEADS)))
    w_in_r = jnp.concatenate(
        [wi[:, c[4]:c[5]], wi[:, c[5]:], wi[:, c[3]:c[4]], wi[:, c[0]:c[1]], wi[:, :c[0]],
         pad8(wi[:, c[1]:c[2]]), pad8(wi[:, c[2]:c[3]])], axis=1).astype(BF16)
    w_router = jnp.concatenate(
        [w_router_group[lyr], w_router_expert[lyr].reshape(d, N_EXPERTS),
         jnp.zeros((d, LANES - N_GROUPS - N_EXPERTS), F32)], axis=1).astype(F32)
    b_router = _pad_lanes(jnp.concatenate([b_router_group[lyr], b_router_expert[lyr].reshape(-1)]))
    prm = dict(
        norm1_w=norm1_w[lyr].reshape(1, d), w_in=w_in_r,
        w_pool=w_pool[lyr].astype(BF16), pool_scale=pool_scale[lyr].reshape(1, POOL_DIM),
        w_conv=jnp.pad(w_conv[lyr], ((0, SUBLANES - CONV_WIDTH), (0, 0))),
        a_log=_pad_lanes(a_log[lyr]), dt_bias=_pad_lanes(dt_bias[lyr]), o_norm_w=o_norm_w[lyr].reshape(1, HEAD_DIM),
        w_branch_pool=w_branch_pool[lyr].astype(BF16), w_branch_delta=w_branch_delta[lyr].astype(BF16),
        w_out=w_out[lyr].astype(BF16), norm2_w=norm2_w[lyr].reshape(1, d),
        w_router=w_router, b_router=b_router,
    )
    wg = w_gate[lyr].astype(BF16)
    wu = w_up[lyr].astype(BF16)
    wd = w_down[lyr].astype(BF16)
    fnw = final_norm_w.reshape(1, d)

    tt_p = 256
    xm, h2, eid, wk, npool, nconv, ns = _layer(
        x_prompt.reshape(bsz * seq, d),
        jnp.zeros((bsz, POOL_ROWS, POOL_DIM), F32), jnp.zeros((bsz, CONV_ROWS, CONV_DIM), F32),
        jnp.zeros((bsz, N_HEADS, HEAD_DIM, HEAD_DIM), F32), prm,
        n_seq=bsz, tt=tt_p, tv=tt_p, chunk=64, pos0=0, n_tiles=seq // tt_p, tm=256)
    y_prompt = _moe(xm, h2, eid[:, :2], wk[:, :2], wg, wu, wd, fnw, tm_e=256, tm_c=256).reshape(bsz, seq, d)
    pool_p = npool[:, 1:, :][None]
    conv_p = nconv[:, CONV_ROWS - (CONV_WIDTH - 1):, :][None]
    delta_p = ns[None]

    tt_s = SUBLANES
    xs_pad = jnp.pad(x_sample, ((0, 0), (0, tt_s - dseq), (0, 0))).reshape(dbs * tt_s, d)
    pool0 = jnp.pad(cache_pool[lyr], ((0, 0), (POOL_ROWS - POOL_BUF, 0), (0, 0)))
    conv0 = jnp.pad(cache_conv[lyr], ((0, 0), (CONV_ROWS - (CONV_WIDTH - 1), 0), (0, 0)))
    xm, h2, eid, wk, npool, nconv, ns = _layer(
        xs_pad, pool0, conv0, state_delta[lyr], prm,
        n_seq=dbs, tt=tt_s, tv=dseq, chunk=tt_s, pos0=past_len, n_tiles=1, tm=256)
    real = lambda a: a.reshape(dbs, tt_s, -1)[:, :dseq].reshape(dbs * dseq, -1)
    y_sample = _moe(real(xm), real(h2), real(eid)[:, :2], real(wk)[:, :2], wg, wu, wd, fnw,
                    tm_e=128, tm_c=256).reshape(dbs, dseq, d)
    pool_s = npool[:, 1:, :][None]
    conv_s = nconv[:, CONV_ROWS - (CONV_WIDTH - 1):, :][None]
    delta_s = ns[None]
    return (y_prompt, y_sample, pool_p, conv_p, delta_p, pool_s, conv_s, delta_s)
```

```python
import functools

import jax
import jax.numpy as jnp
from jax import lax
from jax.experimental import pallas as pl
from jax.experimental.pallas import tpu as pltpu

F32 = jnp.float32
BF16 = jnp.bfloat16
EPS = 1e-6
HIGHEST = lax.Precision.HIGHEST

POOL_WINDOWS = (2, 4, 8, 16)
POOL_GROUP_DIM = 128
POOL_DIM = len(POOL_WINDOWS) * POOL_GROUP_DIM
POOL_BUF = max(POOL_WINDOWS) - 1
N_HEADS = 8
HEAD_DIM = 128
QK_DIM = N_HEADS * HEAD_DIM
CONV_WIDTH = 4
CONV_DIM = 3 * QK_DIM
N_GROUPS = 4
EXPERTS_PER_GROUP = 8
N_EXPERTS = N_GROUPS * EXPERTS_PER_GROUP
LANES = 128
SUBLANES = 8

OFF_Z = 0
OFF_QKV = OFF_Z + QK_DIM
OFF_POOL = OFF_QKV + CONV_DIM
OFF_B = OFF_POOL + POOL_DIM
OFF_A = OFF_B + LANES
MIX_COLS = OFF_A + LANES
GATE_COLS = 2048
IN_COLS = MIX_COLS + GATE_COLS

POOL_ROWS = 16
CONV_ROWS = 8
VMEM_LIMIT = 56 * 1024 * 1024


def _sigmoid(x):
    return 1.0 / (1.0 + jnp.exp(-x))


def _softplus(x):
    return jnp.maximum(x, 0.0) + jnp.log1p(jnp.exp(-jnp.abs(x)))


def _rms(x, w):
    return x * lax.rsqrt(jnp.mean(x * x, axis=-1, keepdims=True) + EPS) * w


def _dot(a, b):
    return jnp.dot(a.astype(BF16), b.astype(BF16), preferred_element_type=F32)


def _dot_nt(a, b):
    return lax.dot_general(a.astype(BF16), b.astype(BF16), (((1,), (1,)), ((), ())), preferred_element_type=F32)


def _dot_tn(a, b):
    return lax.dot_general(a.astype(BF16), b.astype(BF16), (((0,), (0,)), ((), ())), preferred_element_type=F32)


def _inproj_kernel(x_ref, nw_ref, w_ref, mix_ref, gate_ref):
    h = _rms(x_ref[...], nw_ref[...]).astype(BF16)
    half = MIX_COLS // 2
    for c0 in (0, half):
        mix_ref[:, c0:c0 + half] = jnp.dot(h, w_ref[:, c0:c0 + half], preferred_element_type=F32)
    half = GATE_COLS // 2
    for c0 in (0, half):
        gate_ref[:, c0:c0 + half] = jnp.dot(h, w_ref[:, MIX_COLS + c0:MIX_COLS + c0 + half],
                                            preferred_element_type=F32)


def _inproj(x, norm_w, w_in_r, tm):
    n, d = x.shape
    return pl.pallas_call(
        _inproj_kernel,
        grid=(n // tm,),
        in_specs=[
            pl.BlockSpec((tm, d), lambda i: (i, 0)),
            pl.BlockSpec((1, d), lambda i: (0, 0)),
            pl.BlockSpec((d, IN_COLS), lambda i: (0, 0)),
        ],
        out_specs=[pl.BlockSpec((tm, MIX_COLS), lambda i: (i, 0)), pl.BlockSpec((tm, GATE_COLS), lambda i: (i, 0))],
        out_shape=[jax.ShapeDtypeStruct((n, MIX_COLS), F32), jax.ShapeDtypeStruct((n, GATE_COLS), F32)],
        compiler_params=pltpu.CompilerParams(dimension_semantics=("arbitrary",), vmem_limit_bytes=VMEM_LIMIT),
        name="inproj",
    )(x, norm_w, w_in_r)


def _mixer_kernel(proj_ref, pool0_ref, conv0_ref, s0_ref, wpool_ref, pscale_ref, wconv_ref, alog_ref, dtb_ref, onw_ref,
                  ypool_ref, ydelta_ref, npool_ref, nconv_ref, ns_ref,
                  ext_p, ext_c, q_s, k_s, v_s, g_s, b_s, state,
                  *, sb, tt, tv, chunk, pos0, n_tiles):
    t = pl.program_id(1)

    @pl.when(t == 0)
    def _():
        ext_p[:, 0:POOL_ROWS, :] = pool0_ref[...]
        ext_c[:, 0:CONV_ROWS, :] = conv0_ref[...]
        state[...] = s0_ref[...]

    row = lax.broadcasted_iota(jnp.int32, (tt, 1), 0)
    pos = pos0 + t * tt + row
    base = CONV_ROWS - (CONV_WIDTH - 1)
    for s in range(sb):
        ext_p[s, POOL_ROWS:POOL_ROWS + tt, :] = proj_ref[s, :, OFF_POOL:OFF_POOL + POOL_DIM]
        ext_c[s, CONV_ROWS:CONV_ROWS + tt, :] = proj_ref[s, :, OFF_QKV:OFF_QKV + CONV_DIM]

        for gi, win in enumerate(POOL_WINDOWS):
            cs = slice(gi * POOL_GROUP_DIM, (gi + 1) * POOL_GROUP_DIM)
            u = ext_p[s, POOL_ROWS:POOL_ROWS + tt, cs]
            acc = u
            for j in range(1, win):
                acc = acc + ext_p[s, POOL_ROWS - j:POOL_ROWS - j + tt, cs]
            cnt = jnp.minimum(win, pos + 1).astype(F32)
            pooled = acc / cnt - u
            ypool_ref[s, :, cs] = _dot(pooled, wpool_ref[gi]) * pscale_ref[:, cs]

        for ci in range(CONV_DIM // LANES):
            cs = slice(ci * LANES, (ci + 1) * LANES)
            acc = ext_c[s, base:base + tt, cs] * wconv_ref[0:1, cs]
            for j in range(1, CONV_WIDTH):
                acc = acc + ext_c[s, base + j:base + j + tt, cs] * wconv_ref[j:j + 1, cs]
            y = acc * _sigmoid(acc)
            hs = slice((ci % N_HEADS) * HEAD_DIM, (ci % N_HEADS + 1) * HEAD_DIM)
            if ci < 2 * N_HEADS:
                y = y * lax.rsqrt(jnp.sum(y * y, axis=-1, keepdims=True) + EPS)
            if ci < N_HEADS:
                q_s[s, :, hs] = y * (HEAD_DIM ** -0.5)
            elif ci < 2 * N_HEADS:
                k_s[s, :, hs] = y
            else:
                v_s[s, :, hs] = y

        beta = _sigmoid(proj_ref[s, :, OFF_B:OFF_B + LANES])
        g = -jnp.exp(alog_ref[...]) * _softplus(proj_ref[s, :, OFF_A:OFF_A + LANES] + dtb_ref[...])
        if tv < tt:
            beta = jnp.where(row < tv, beta, 0.0)
            g = jnp.where(row < tv, g, 0.0)
        b_s[s] = beta
        g_s[s] = g

    @pl.when(t == n_tiles - 1)
    def _():
        npool_ref[...] = ext_p[:, tv:tv + POOL_ROWS, :]
        nconv_ref[...] = ext_c[:, tv:tv + CONV_ROWS, :]

    if n_tiles > 1:
        ext_p[:, 0:POOL_ROWS, :] = ext_p[:, tt:tt + POOL_ROWS, :]
        ext_c[:, 0:CONV_ROWS, :] = ext_c[:, tt:tt + CONV_ROWS, :]

    ri = lax.broadcasted_iota(jnp.int32, (chunk, chunk), 0)
    cj = lax.broadcasted_iota(jnp.int32, (chunk, chunk), 1)
    causal = ri >= cj
    strict = ri > cj
    ltri = causal.astype(F32)
    eye = (ri == cj).astype(F32)
    n_doublings = chunk.bit_length() - 2
    units = [(s, h) for s in range(sb) for h in range(N_HEADS)]
    hsl = [slice(h * HEAD_DIM, (h + 1) * HEAD_DIM) for h in range(N_HEADS)]

    def chunk_body(ci, carry):
        r0 = pl.multiple_of(ci * chunk, chunk)
        rows = pl.ds(r0, chunk)
        g_all, g_t, b_all = [], [], []
        for s in range(sb):
            ga = jnp.dot(ltri, g_s[s, rows, :], precision=HIGHEST, preferred_element_type=F32)
            g_all.append(ga)
            if chunk < LANES:
                ga = jnp.concatenate([ga, jnp.zeros((LANES - chunk, LANES), F32)], axis=0)
            g_t.append(ga.T)
            b_all.append(b_s[s, rows, :])
        g_col = [g_all[s][:, h:h + 1] for s, h in units]
        dec = [jnp.exp(jnp.minimum(g_col[i] - g_t[s][h:h + 1, 0:chunk], 0.0)) for i, (s, h) in enumerate(units)]
        b_col = [b_all[s][:, h:h + 1] for s, h in units]
        q = [q_s[s, rows, hsl[h]] for s, h in units]
        k = [k_s[s, rows, hsl[h]] for s, h in units]
        idx = range(len(units))
        kb = [k[i] * b_col[i] for i in idx]
        kbk = [_dot_nt(kb[i], k[i]) for i in idx]
        qk = [_dot_nt(q[i], k[i]) for i in idx]
        p_mat = [jnp.where(strict, kbk[i] * dec[i], 0.0) for i in idx]
        t_inv = [eye - p_mat[i] for i in idx]
        for _ in range(n_doublings):
            p_mat = [_dot(p_mat[i], p_mat[i]) for i in idx]
            xp = [_dot(t_inv[i], p_mat[i]) for i in idx]
            t_inv = [t_inv[i] + xp[i] for i in idx]
        g_exp = [jnp.exp(g_col[i]) for i in idx]
        wu = [_dot(t_inv[i], jnp.concatenate([kb[i] * g_exp[i], v_s[s, rows, hsl[h]] * b_col[i]], axis=1))
              for i, (s, h) in enumerate(units)]
        s_old = [state[s, h] for s, h in units]
        res = [_dot(jnp.concatenate([wu[i][:, 0:HEAD_DIM], q[i] * g_exp[i]], axis=0), s_old[i]) for i in idx]
        v_new = [wu[i][:, HEAD_DIM:2 * HEAD_DIM] - res[i][0:chunk] for i in idx]
        a_qk = [jnp.where(causal, qk[i] * dec[i], 0.0) for i in idx]
        g_last = [g_all[s][chunk - 1:chunk, h:h + 1] for s, h in units]
        k_dec = [k[i] * jnp.exp(g_last[i] - g_col[i]) for i in idx]
        intra = [_dot(a_qk[i], v_new[i]) for i in idx]
        upd = [_dot_tn(k_dec[i], v_new[i]) for i in idx]
        for i, (s, h) in enumerate(units):
            state[s, h] = s_old[i] * jnp.exp(g_last[i]) + upd[i]
            o_c = res[i][chunk:2 * chunk] + intra[i]
            z_c = proj_ref[s, rows, OFF_Z + h * HEAD_DIM:OFF_Z + (h + 1) * HEAD_DIM]
            ydelta_ref[s, rows, hsl[h]] = _rms(o_c, onw_ref[...]) * (z_c * _sigmoid(z_c))
        return carry

    lax.fori_loop(0, tt // chunk, chunk_body, 0)

    @pl.when(t == n_tiles - 1)
    def _():
        ns_ref[...] = state[...]


def _mixers(proj, pool0, conv0, s0, wpool, pscale, wconv, alog, dtb, onw, *, n_seq, sb, tt, tv, chunk, pos0, n_tiles):
    kern = functools.partial(_mixer_kernel, sb=sb, tt=tt, tv=tv, chunk=chunk, pos0=pos0, n_tiles=n_tiles)
    seq_len = n_tiles * tt
    tile = lambda b, t: (b, t, 0)
    seq3 = lambda b, t: (b, 0, 0)
    seq4 = lambda b, t: (b, 0, 0, 0)
    const2 = lambda b, t: (0, 0)
    ypool, ydelta, npool, nconv, ns = pl.pallas_call(
        kern,
        grid=(n_seq // sb, n_tiles),
        in_specs=[
            pl.BlockSpec((sb, tt, MIX_COLS), tile),
            pl.BlockSpec((sb, POOL_ROWS, POOL_DIM), seq3),
            pl.BlockSpec((sb, CONV_ROWS, CONV_DIM), seq3),
            pl.BlockSpec((sb, N_HEADS, HEAD_DIM, HEAD_DIM), seq4),
            pl.BlockSpec((len(POOL_WINDOWS), POOL_GROUP_DIM, POOL_GROUP_DIM), lambda b, t: (0, 0, 0)),
            pl.BlockSpec((1, POOL_DIM), const2),
            pl.BlockSpec((SUBLANES, CONV_DIM), const2),
            pl.BlockSpec((1, LANES), const2),
            pl.BlockSpec((1, LANES), const2),
            pl.BlockSpec((1, HEAD_DIM), const2),
        ],
        out_specs=[
            pl.BlockSpec((sb, tt, POOL_DIM), tile),
            pl.BlockSpec((sb, tt, QK_DIM), tile),
            pl.BlockSpec((sb, POOL_ROWS, POOL_DIM), seq3),
            pl.BlockSpec((sb, CONV_ROWS, CONV_DIM), seq3),
            pl.BlockSpec((sb, N_HEADS, HEAD_DIM, HEAD_DIM), seq4),
        ],
        out_shape=[
            jax.ShapeDtypeStruct((n_seq, seq_len, POOL_DIM), F32),
            jax.ShapeDtypeStruct((n_seq, seq_len, QK_DIM), F32),
            jax.ShapeDtypeStruct((n_seq, POOL_ROWS, POOL_DIM), F32),
            jax.ShapeDtypeStruct((n_seq, CONV_ROWS, CONV_DIM), F32),
            jax.ShapeDtypeStruct((n_seq, N_HEADS, HEAD_DIM, HEAD_DIM), F32),
        ],
        scratch_shapes=[
            pltpu.VMEM((sb, POOL_ROWS + tt, POOL_DIM), F32),
            pltpu.VMEM((sb, CONV_ROWS + tt, CONV_DIM), F32),
            pltpu.VMEM((sb, tt, QK_DIM), F32),
            pltpu.VMEM((sb, tt, QK_DIM), F32),
            pltpu.VMEM((sb, tt, QK_DIM), F32),
            pltpu.VMEM((sb, tt, LANES), F32),
            pltpu.VMEM((sb, tt, LANES), F32),
            pltpu.VMEM((sb, N_HEADS, HEAD_DIM, HEAD_DIM), F32),
        ],
        compiler_params=pltpu.CompilerParams(dimension_semantics=("arbitrary", "arbitrary"),
                                             vmem_limit_bytes=VMEM_LIMIT),
        name="mixers",
    )(proj.reshape(n_seq, seq_len, MIX_COLS), pool0, conv0, s0, wpool, pscale, wconv, alog, dtb, onw)
    n = n_seq * seq_len
    return ypool.reshape(n, POOL_DIM), ydelta.reshape(n, QK_DIM), npool, nconv, ns


def _merge_kernel(x_ref, gates_ref, ypool_ref, ydelta_ref, wbp_ref, wbd_ref, wout_ref, n2w_ref, wr_ref, br_ref,
                  xm_ref, h2_ref, eid_ref, wk_ref):
    tm = x_ref.shape[0]
    bp = _dot(ypool_ref[...], wbp_ref[...])
    bd = _dot(ydelta_ref[...], wbd_ref[...])
    merged = _sigmoid(gates_ref[:, 0:1024]) * bp + _sigmoid(gates_ref[:, 1024:2048]) * bd
    xm = x_ref[...] + _dot(merged, wout_ref[...])
    xm_ref[...] = xm
    h2 = _rms(xm, n2w_ref[...])
    h2_ref[...] = h2

    logits = jnp.dot(h2, wr_ref[...], precision=HIGHEST, preferred_element_type=F32) + br_ref[...]
    lane = lax.broadcasted_iota(jnp.int32, logits.shape, 1)
    neg = -jnp.inf
    far = LANES - 1
    is_g = lane < N_GROUPS
    g_max = jnp.max(jnp.where(is_g, logits, neg), axis=-1, keepdims=True)
    g_sel = jnp.min(jnp.where(is_g & (logits == g_max), lane, far), axis=-1, keepdims=True)
    p_g = 1.0 / jnp.sum(jnp.where(is_g, jnp.exp(logits - g_max), 0.0), axis=-1, keepdims=True)
    e_lane = lane - N_GROUPS
    is_e = (e_lane >= 0) & (e_lane < N_EXPERTS) & ((e_lane // EXPERTS_PER_GROUP) == g_sel)
    ev = jnp.where(is_e, logits, neg)
    v1 = jnp.max(ev, axis=-1, keepdims=True)
    i1 = jnp.min(jnp.where(is_e & (ev == v1), lane, far), axis=-1, keepdims=True)
    is_e2 = is_e & (lane != i1)
    ev2 = jnp.where(is_e2, logits, neg)
    v2 = jnp.max(ev2, axis=-1, keepdims=True)
    i2 = jnp.min(jnp.where(is_e2 & (ev2 == v2), lane, far), axis=-1, keepdims=True)
    e21 = jnp.exp(v2 - v1)
    w1 = p_g / (1.0 + e21)
    w2 = p_g * e21 / (1.0 + e21)
    lane8 = lax.broadcasted_iota(jnp.int32, (tm, SUBLANES), 1)
    eid_ref[...] = jnp.where(lane8 == 0, i1 - N_GROUPS, jnp.where(lane8 == 1, i2 - N_GROUPS, 0))
    wk_ref[...] = jnp.where(lane8 == 0, w1, jnp.where(lane8 == 1, w2, 0.0))


def _merge(x, proj, ypool, ydelta, wbp, wbd, wout, n2w, wr, br, tm):
    n, d = x.shape
    row = lambda i: (i, 0)
    const = lambda i: (0, 0)
    return pl.pallas_call(
        _merge_kernel,
        grid=(n // tm,),
        in_specs=[
            pl.BlockSpec((tm, d), row),
            pl.BlockSpec((tm, GATE_COLS), row),
            pl.BlockSpec((tm, POOL_DIM), row),
            pl.BlockSpec((tm, QK_DIM), row),
            pl.BlockSpec((POOL_DIM, d), const),
            pl.BlockSpec((QK_DIM, d), const),
            pl.BlockSpec((d, d), const),
            pl.BlockSpec((1, d), const),
            pl.BlockSpec((d, LANES), const),
            pl.BlockSpec((1, LANES), const),
        ],
        out_specs=[
            pl.BlockSpec((tm, d), row),
            pl.BlockSpec((tm, d), row),
            pl.BlockSpec((tm, SUBLANES), row),
            pl.BlockSpec((tm, SUBLANES), row),
        ],
        out_shape=[
            jax.ShapeDtypeStruct((n, d), F32),
            jax.ShapeDtypeStruct((n, d), F32),
            jax.ShapeDtypeStruct((n, SUBLANES), jnp.int32),
            jax.ShapeDtypeStruct((n, SUBLANES), F32),
        ],
        compiler_params=pltpu.CompilerParams(dimension_semantics=("arbitrary",), vmem_limit_bytes=VMEM_LIMIT),
        name="merge",
    )(x, proj, ypool, ydelta, wbp, wbd, wout, n2w, wr, br)


def _dispatch_kernel(fill_start_ref, fill_len_ref, pos_ref, h2_ref, xs_hbm, stage, zbuf, sem, fill_sem,
                     *, tm, tm_e, n_steps):
    i = pl.program_id(0)
    slot = lax.rem(i, 2)

    def drain(s):
        for _ in range(2):
            pltpu.make_async_copy(stage.at[s], xs_hbm.at[pl.ds(0, tm), :], sem.at[s]).wait()

    @pl.when(i >= 2)
    def _():
        drain(slot)

    stage[slot] = h2_ref[...]

    def body(r, carry):
        for k in range(2):
            p = pos_ref[0, 0, 2 * r + k]
            pltpu.make_async_copy(stage.at[slot, pl.ds(r, 1), :], xs_hbm.at[pl.ds(p, 1), :], sem.at[slot]).start()
        return carry
    lax.fori_loop(0, tm, body, 0, unroll=8)

    @pl.when(i == n_steps - 1)
    def _():
        drain(slot)
        if n_steps >= 2:
            drain(1 - slot)
        zbuf[...] = jnp.zeros_like(zbuf)

        def fill_copy(start, rows):
            return pltpu.make_async_copy(zbuf.at[pl.ds(0, rows), :], xs_hbm.at[pl.ds(start, rows), :], fill_sem)

        def for_each_fill(op):
            for e in range(N_EXPERTS):
                start = fill_start_ref[e]
                n_fill = fill_len_ref[e]
                head = n_fill & (SUBLANES - 1)
                for j in range(SUBLANES - 1):
                    @pl.when(j < head)
                    def _(start=start, j=j):
                        op(fill_copy(start + j, 1))
                start = pl.multiple_of(start + head, SUBLANES)
                bit = tm_e // 2
                while bit >= SUBLANES:
                    take = (n_fill & bit) != 0

                    @pl.when(take)
                    def _(start=start, bit=bit):
                        op(fill_copy(start, bit))
                    start = pl.multiple_of(start + jnp.where(take, bit, 0), SUBLANES)
                    bit //= 2

        for_each_fill(lambda cp: cp.start())
        for_each_fill(lambda cp: cp.wait())

        blk = tm_e // 2
        tail_start = fill_start_ref[N_EXPERTS]
        n_blk = (xs_hbm.shape[0] - tail_start) // blk

        def tail_copy(j):
            row0 = pl.multiple_of(tail_start + j * blk, SUBLANES)
            return pltpu.make_async_copy(zbuf, xs_hbm.at[pl.ds(row0, blk), :], fill_sem)

        def tail_start_body(j, carry):
            tail_copy(j).start()
            return carry

        def tail_wait_body(j, carry):
            tail_copy(j).wait()
            return carry
        lax.fori_loop(0, n_blk, tail_start_body, 0)
        lax.fori_loop(0, n_blk, tail_wait_body, 0)


def _dispatch(fill_start, fill_len, pos, h2, n_rows, tm, tm_e):
    n, d = h2.shape
    n_steps = n // tm
    grid_spec = pltpu.PrefetchScalarGridSpec(
        num_scalar_prefetch=2,
        grid=(n_steps,),
        in_specs=[
            pl.BlockSpec((1, 1, 2 * tm), lambda i, fs, fl: (i, 0, 0), memory_space=pltpu.SMEM),
            pl.BlockSpec((tm, d), lambda i, fs, fl: (i, 0)),
        ],
        out_specs=pl.BlockSpec(memory_space=pl.ANY),
        scratch_shapes=[pltpu.VMEM((2, tm, d), F32), pltpu.VMEM((tm_e // 2, d), F32),
                        pltpu.SemaphoreType.DMA((2,)), pltpu.SemaphoreType.DMA(())],
    )
    return pl.pallas_call(
        functools.partial(_dispatch_kernel, tm=tm, tm_e=tm_e, n_steps=n_steps),
        grid_spec=grid_spec,
        out_shape=jax.ShapeDtypeStruct((n_rows, d), F32),
        compiler_params=pltpu.CompilerParams(dimension_semantics=("arbitrary",), vmem_limit_bytes=VMEM_LIMIT),
        name="dispatch",
    )(fill_start, fill_len, pos, h2)


def _expert_kernel(te_ref, na_ref, x_ref, wg_ref, wu_ref, wd_ref, y_ref):
    i = pl.program_id(0)

    @pl.when(i < na_ref[0])
    def _():
        x = x_ref[...]
        a = _dot(x, wg_ref[0])
        b = _dot(x, wu_ref[0])
        y_ref[...] = _dot((a * _sigmoid(a)) * b, wd_ref[0])

    @pl.when(i >= na_ref[0])
    def _():
        y_ref[...] = jnp.zeros_like(y_ref)


def _experts(tile_expert, n_active, xs, wg, wu, wd, tm):
    n_rows, d = xs.shape
    f = wg.shape[2]
    grid_spec = pltpu.PrefetchScalarGridSpec(
        num_scalar_prefetch=2,
        grid=(n_rows // tm,),
        in_specs=[
            pl.BlockSpec((tm, d), lambda i, te, na: (jnp.where(i < na[0], i, 0), 0)),
            pl.BlockSpec((1, d, f), lambda i, te, na: (te[i], 0, 0)),
            pl.BlockSpec((1, d, f), lambda i, te, na: (te[i], 0, 0)),
            pl.BlockSpec((1, f, d), lambda i, te, na: (te[i], 0, 0)),
        ],
        out_specs=pl.BlockSpec((tm, d), lambda i, te, na: (i, 0)),
    )
    return pl.pallas_call(
        _expert_kernel,
        grid_spec=grid_spec,
        out_shape=jax.ShapeDtypeStruct((n_rows, d), F32),
        compiler_params=pltpu.CompilerParams(dimension_semantics=("arbitrary",), vmem_limit_bytes=VMEM_LIMIT),
        name="experts",
    )(tile_expert, n_active, xs, wg, wu, wd)


def _combine_kernel(pos_ref, pos_next_ref, xm_ref, wk_ref, fnw_ref, ys_hbm, o_ref, buf, sem, *, tm, n_steps):
    i = pl.program_id(0)
    slot = lax.rem(i, 2)

    def gather(p_ref, s):
        def body(r, carry):
            for k in range(2):
                p = p_ref[0, 0, 2 * r + k]
                pltpu.make_async_copy(ys_hbm.at[pl.ds(p, 1), :], buf.at[s, k, pl.ds(r, 1), :], sem.at[s]).start()
            return carry
        lax.fori_loop(0, tm, body, 0, unroll=8)

    @pl.when(i == 0)
    def _():
        gather(pos_ref, 0)

    @pl.when(i + 1 < n_steps)
    def _():
        gather(pos_next_ref, 1 - slot)

    for k in range(2):
        pltpu.make_async_copy(ys_hbm.at[pl.ds(0, tm), :], buf.at[slot, k], sem.at[slot]).wait()
    w = wk_ref[...]
    moe = w[:, 0:1] * buf[slot, 0] + w[:, 1:2] * buf[slot, 1]
    o_ref[...] = _rms(xm_ref[...] + moe, fnw_ref[...])


def _combine(pos, xm, wk, fnw, y_sorted, tm):
    n, d = xm.shape
    n_steps = n // tm
    return pl.pallas_call(
        functools.partial(_combine_kernel, tm=tm, n_steps=n_steps),
        grid=(n_steps,),
        in_specs=[
            pl.BlockSpec((1, 1, 2 * tm), lambda i: (i, 0, 0), memory_space=pltpu.SMEM),
            pl.BlockSpec((1, 1, 2 * tm), lambda i: (jnp.minimum(i + 1, n_steps - 1), 0, 0), memory_space=pltpu.SMEM),
            pl.BlockSpec((tm, d), lambda i: (i, 0)),
            pl.BlockSpec((tm, SUBLANES), lambda i: (i, 0)),
            pl.BlockSpec((1, d), lambda i: (0, 0)),
            pl.BlockSpec(memory_space=pl.ANY),
        ],
        out_specs=pl.BlockSpec((tm, d), lambda i: (i, 0)),
        out_shape=jax.ShapeDtypeStruct((n, d), F32),
        scratch_shapes=[pltpu.VMEM((2, 2, tm, d), F32), pltpu.SemaphoreType.DMA((2,))],
        compiler_params=pltpu.CompilerParams(dimension_semantics=("arbitrary",), vmem_limit_bytes=VMEM_LIMIT),
        name="combine",
    )(pos, pos, xm, wk, fnw, y_sorted)


def _moe(xm, h2, eid, wk, wg, wu, wd, fnw, *, tm_e, tm_c):
    n = xm.shape[0]
    flat_e = eid[:, :2].reshape(-1)
    onehot = (flat_e[:, None] == jnp.arange(N_EXPERTS, dtype=jnp.int32)[None, :]).astype(jnp.int32)
    csum = jnp.cumsum(onehot, axis=0)
    rank = jnp.take_along_axis(csum, flat_e[:, None], axis=1)[:, 0] - 1
    counts = csum[-1]
    padded = ((counts + tm_e - 1) // tm_e) * tm_e
    ends = jnp.cumsum(padded)
    starts = ends - padded
    pos = (starts[flat_e] + rank).astype(jnp.int32).reshape(n // tm_c, 1, 2 * tm_c)
    n_rows = 2 * n + N_EXPERTS * tm_e
    tile_start = jnp.arange(n_rows // tm_e, dtype=jnp.int32) * tm_e
    tile_expert = jnp.minimum(
        jnp.sum((ends[None, :] <= tile_start[:, None]).astype(jnp.int32), axis=1), N_EXPERTS - 1)
    n_active = (ends[-1] // tm_e).astype(jnp.int32).reshape(1)
    fill_start = jnp.concatenate([starts + counts, ends[-1:]]).astype(jnp.int32)
    xs = _dispatch(fill_start, (padded - counts).astype(jnp.int32), pos, h2, n_rows, tm_c, tm_e)
    y_sorted = _experts(tile_expert, n_active, xs, wg, wu, wd, tm_e)
    return _combine(pos, xm, wk, fnw, y_sorted, tm_c)


def _layer(x_tokens, pool0, conv0, s0, prm, *, n_seq, sb, tt, tv, chunk, pos0, n_tiles, tm):
    proj, gates = _inproj(x_tokens, prm["norm1_w"], prm["w_in"], tm)
    ypool, ydelta, npool, nconv, ns = _mixers(
        proj, pool0, conv0, s0, prm["w_pool"], prm["pool_scale"], prm["w_conv"], prm["a_log"], prm["dt_bias"],
        prm["o_norm_w"], n_seq=n_seq, sb=sb, tt=tt, tv=tv, chunk=chunk, pos0=pos0, n_tiles=n_tiles)
    xm, h2, eid, wk = _merge(x_tokens, gates, ypool, ydelta, prm["w_branch_pool"], prm["w_branch_delta"],
                             prm["w_out"], prm["norm2_w"], prm["w_router"], prm["b_router"], tm)
    return xm, h2, eid, wk, npool, nconv, ns


def _pad_lanes(v, width=LANES):
    v = v.reshape(1, -1).astype(F32)
    return jnp.pad(v, ((0, 0), (0, width - v.shape[1])))


def kernel(x_prompt, x_sample, cache_pool, cache_conv, state_delta, norm1_w, w_in, w_pool, pool_scale, w_conv, a_log, dt_bias, o_norm_w, w_branch_pool, w_branch_delta, w_out, norm2_w, w_router_group, b_router_group, w_router_expert, b_router_expert, w_gate, w_up, w_down, final_norm_w):
    n_layers = norm1_w.shape[0]
    assert n_layers == 1, "single-layer step"
    bsz, seq, d = x_prompt.shape
    dbs, dseq, _ = x_sample.shape
    past_len = 16384
    lyr = 0

    wi = w_in[lyr]
    c = [POOL_DIM, POOL_DIM + CONV_DIM]
    c += [c[1] + N_HEADS, c[1] + 2 * N_HEADS]
    c += [c[3] + QK_DIM, c[3] + QK_DIM + d]
    pad8 = lambda m: jnp.pad(m, ((0, 0), (0, LANES - N_HEADS)))
    w_in_r = jnp.concatenate(
        [wi[:, c[3]:c[4]], wi[:, c[0]:c[1]], wi[:, :c[0]], pad8(wi[:, c[1]:c[2]]), pad8(wi[:, c[2]:c[3]]),
         wi[:, c[4]:c[5]], wi[:, c[5]:]], axis=1).astype(BF16)
    w_router = jnp.concatenate(
        [w_router_group[lyr], w_router_expert[lyr].reshape(d, N_EXPERTS),
         jnp.zeros((d, LANES - N_GROUPS - N_EXPERTS), F32)], axis=1).astype(F32)
    b_router = _pad_lanes(jnp.concatenate([b_router_group[lyr], b_router_expert[lyr].reshape(-1)]))
    prm = dict(
        norm1_w=norm1_w[lyr].reshape(1, d), w_in=w_in_r,
        w_pool=w_pool[lyr].astype(BF16), pool_scale=pool_scale[lyr].reshape(1, POOL_DIM),
        w_conv=jnp.pad(w_conv[lyr], ((0, SUBLANES - CONV_WIDTH), (0, 0))),
        a_log=_pad_lanes(a_log[lyr]), dt_bias=_pad_lanes(dt_bias[lyr]), o_norm_w=o_norm_w[lyr].reshape(1, HEAD_DIM),
        w_branch_pool=w_branch_pool[lyr].astype(BF16), w_branch_delta=w_branch_delta[lyr].astype(BF16),
        w_out=w_out[lyr].astype(BF16), norm2_w=norm2_w[lyr].reshape(1, d),
        w_router=w_router, b_router=b_router,
    )
    wg = w_gate[lyr].astype(BF16)
    wu = w_up[lyr].astype(BF16)
    wd = w_down[lyr].astype(BF16)
    fnw = final_norm_w.reshape(1, d)

    tt_p = 256
    xm, h2, eid, wk, npool, nconv, ns = _layer(
        x_prompt.reshape(bsz * seq, d),
        jnp.zeros((bsz, POOL_ROWS, POOL_DIM), F32), jnp.zeros((bsz, CONV_ROWS, CONV_DIM), F32),
        jnp.zeros((bsz, N_HEADS, HEAD_DIM, HEAD_DIM), F32), prm,
        n_seq=bsz, sb=2, tt=tt_p, tv=tt_p, chunk=64, pos0=0, n_tiles=seq // tt_p, tm=256)
    y_prompt = _moe(xm, h2, eid, wk, wg, wu, wd, fnw, tm_e=256, tm_c=256).reshape(bsz, seq, d)
    pool_p = npool[:, 1:, :][None]
    conv_p = nconv[:, CONV_ROWS - (CONV_WIDTH - 1):, :][None]
    delta_p = ns[None]

    tt_s = SUBLANES
    xs_pad = jnp.pad(x_sample, ((0, 0), (0, tt_s - dseq), (0, 0))).reshape(dbs * tt_s, d)
    pool0 = jnp.pad(cache_pool[lyr], ((0, 0), (POOL_ROWS - POOL_BUF, 0), (0, 0)))
    conv0 = jnp.pad(cache_conv[lyr], ((0, 0), (CONV_ROWS - (CONV_WIDTH - 1), 0), (0, 0)))
    xm, h2, eid, wk, npool, nconv, ns = _layer(
        xs_pad, pool0, conv0, state_delta[lyr], prm,
        n_seq=dbs, sb=4, tt=tt_s, tv=dseq, chunk=tt_s, pos0=past_len, n_tiles=1, tm=256)
    real = lambda a: a.reshape(dbs, tt_s, -1)[:, :dseq].reshape(dbs * dseq, -1)
    y_sample = _moe(real(xm), real(h2), real(eid), real(wk), wg, wu, wd, fnw, tm_e=128, tm_c=256).reshape(dbs, dseq, d)
    pool_s = npool[:, 1:, :][None]
    conv_s = nconv[:, CONV_ROWS - (CONV_WIDTH - 1):, :][None]
    delta_s = ns[None]
    return (y_prompt, y_sample, pool_p, conv_p, delta_p, pool_s, conv_s, delta_s)
```

```python
import functools

import jax
import jax.numpy as jnp
from jax import lax
from jax.experimental import pallas as pl
from jax.experimental.pallas import tpu as pltpu
from jax.experimental.pallas import tpu_sc as plsc

F32 = jnp.float32
BF16 = jnp.bfloat16
EPS = 1e-6
HIGHEST = lax.Precision.HIGHEST

POOL_WINDOWS = (2, 4, 8, 16)
POOL_GROUP_DIM = 128
POOL_DIM = len(POOL_WINDOWS) * POOL_GROUP_DIM
POOL_BUF = max(POOL_WINDOWS) - 1
N_HEADS = 8
HEAD_DIM = 128
QK_DIM = N_HEADS * HEAD_DIM
CONV_WIDTH = 4
CONV_DIM = 3 * QK_DIM
N_GROUPS = 4
EXPERTS_PER_GROUP = 8
N_EXPERTS = N_GROUPS * EXPERTS_PER_GROUP
LANES = 128
SUBLANES = 8

OFF_Z = 0
OFF_QKV = OFF_Z + QK_DIM
OFF_POOL = OFF_QKV + CONV_DIM
OFF_B = OFF_POOL + POOL_DIM
OFF_A = OFF_B + LANES
MIX_COLS = OFF_A + LANES
GATE_COLS = 2048
IN_COLS = MIX_COLS + GATE_COLS

POOL_ROWS = 16
CONV_ROWS = 8
VMEM_LIMIT = 56 * 1024 * 1024
SC_GATHER_ROWS = 96


def _sigmoid(x):
    return 1.0 / (1.0 + jnp.exp(-x))


def _softplus(x):
    return jnp.maximum(x, 0.0) + jnp.log1p(jnp.exp(-jnp.abs(x)))


def _rms(x, w):
    return x * lax.rsqrt(jnp.mean(x * x, axis=-1, keepdims=True) + EPS) * w


def _dot(a, b):
    return jnp.dot(a.astype(BF16), b.astype(BF16), preferred_element_type=F32)


def _dot_nt(a, b):
    return lax.dot_general(a.astype(BF16), b.astype(BF16), (((1,), (1,)), ((), ())), preferred_element_type=F32)


def _dot_tn(a, b):
    return lax.dot_general(a.astype(BF16), b.astype(BF16), (((0,), (0,)), ((), ())), preferred_element_type=F32)


def _inproj_kernel(x_ref, nw_ref, w_ref, mix_ref, gate_ref):
    h = _rms(x_ref[...], nw_ref[...]).astype(BF16)
    half = MIX_COLS // 2
    for c0 in (0, half):
        mix_ref[:, c0:c0 + half] = jnp.dot(h, w_ref[:, c0:c0 + half], preferred_element_type=F32)
    half = GATE_COLS // 2
    for c0 in (0, half):
        gate_ref[:, c0:c0 + half] = jnp.dot(h, w_ref[:, MIX_COLS + c0:MIX_COLS + c0 + half],
                                            preferred_element_type=F32)


def _inproj(x, norm_w, w_in_r, tm):
    n, d = x.shape
    return pl.pallas_call(
        _inproj_kernel,
        grid=(n // tm,),
        in_specs=[
            pl.BlockSpec((tm, d), lambda i: (i, 0)),
            pl.BlockSpec((1, d), lambda i: (0, 0)),
            pl.BlockSpec((d, IN_COLS), lambda i: (0, 0)),
        ],
        out_specs=[pl.BlockSpec((tm, MIX_COLS), lambda i: (i, 0)), pl.BlockSpec((tm, GATE_COLS), lambda i: (i, 0))],
        out_shape=[jax.ShapeDtypeStruct((n, MIX_COLS), F32), jax.ShapeDtypeStruct((n, GATE_COLS), F32)],
        compiler_params=pltpu.CompilerParams(dimension_semantics=("arbitrary",), vmem_limit_bytes=VMEM_LIMIT),
        name="inproj",
    )(x, norm_w, w_in_r)


def _mixer_kernel(proj_ref, pool0_ref, conv0_ref, s0_ref, wpool_ref, pscale_ref, wconv_ref, alog_ref, dtb_ref, onw_ref,
                  ypool_ref, ydelta_ref, npool_ref, nconv_ref, ns_ref,
                  ext_p, ext_c, q_s, k_s, v_s, g_s, b_s, state,
                  *, sb, tt, tv, chunk, pos0, n_tiles):
    t = pl.program_id(1)
    base = CONV_ROWS - (CONV_WIDTH - 1)

    @pl.when(t == 0)
    def _():
        ext_p[:, POOL_ROWS - POOL_BUF:POOL_ROWS, :] = pool0_ref[...]
        ext_c[:, base:CONV_ROWS, :] = conv0_ref[...]
        state[...] = s0_ref[...]

    row = lax.broadcasted_iota(jnp.int32, (tt, 1), 0)
    pos = pos0 + t * tt + row
    for s in range(sb):
        ext_p[s, POOL_ROWS:POOL_ROWS + tt, :] = proj_ref[s, :, OFF_POOL:OFF_POOL + POOL_DIM]
        ext_c[s, CONV_ROWS:CONV_ROWS + tt, :] = proj_ref[s, :, OFF_QKV:OFF_QKV + CONV_DIM]

        for gi, win in enumerate(POOL_WINDOWS):
            cs = slice(gi * POOL_GROUP_DIM, (gi + 1) * POOL_GROUP_DIM)
            u = ext_p[s, POOL_ROWS:POOL_ROWS + tt, cs]
            acc = u
            for j in range(1, win):
                acc = acc + ext_p[s, POOL_ROWS - j:POOL_ROWS - j + tt, cs]
            cnt = jnp.minimum(win, pos + 1).astype(F32)
            pooled = acc / cnt - u
            ypool_ref[s, :, cs] = _dot(pooled, wpool_ref[gi]) * pscale_ref[:, cs]

        for ci in range(CONV_DIM // LANES):
            cs = slice(ci * LANES, (ci + 1) * LANES)
            acc = ext_c[s, base:base + tt, cs] * wconv_ref[0:1, cs]
            for j in range(1, CONV_WIDTH):
                acc = acc + ext_c[s, base + j:base + j + tt, cs] * wconv_ref[j:j + 1, cs]
            y = acc * _sigmoid(acc)
            hs = slice((ci % N_HEADS) * HEAD_DIM, (ci % N_HEADS + 1) * HEAD_DIM)
            if ci < 2 * N_HEADS:
                y = y * lax.rsqrt(jnp.sum(y * y, axis=-1, keepdims=True) + EPS)
            if ci < N_HEADS:
                q_s[s, :, hs] = y * (HEAD_DIM ** -0.5)
            elif ci < 2 * N_HEADS:
                k_s[s, :, hs] = y
            else:
                v_s[s, :, hs] = y

        beta = _sigmoid(proj_ref[s, :, OFF_B:OFF_B + LANES])
        g = -jnp.exp(alog_ref[...]) * _softplus(proj_ref[s, :, OFF_A:OFF_A + LANES] + dtb_ref[...])
        if tv < tt:
            beta = jnp.where(row < tv, beta, 0.0)
            g = jnp.where(row < tv, g, 0.0)
        b_s[s] = beta
        g_s[s] = g

    @pl.when(t == n_tiles - 1)
    def _():
        npool_ref[...] = ext_p[:, tv + POOL_ROWS - POOL_BUF:tv + POOL_ROWS, :]
        nconv_ref[...] = ext_c[:, tv + base:tv + CONV_ROWS, :]

    if n_tiles > 1:
        ext_p[:, 0:POOL_ROWS, :] = ext_p[:, tt:tt + POOL_ROWS, :]
        ext_c[:, 0:CONV_ROWS, :] = ext_c[:, tt:tt + CONV_ROWS, :]

    ri = lax.broadcasted_iota(jnp.int32, (chunk, chunk), 0)
    cj = lax.broadcasted_iota(jnp.int32, (chunk, chunk), 1)
    causal = ri >= cj
    strict = ri > cj
    ltri = causal.astype(F32)
    eye = (ri == cj).astype(F32)
    n_doublings = chunk.bit_length() - 2
    units = [(s, h) for s in range(sb) for h in range(N_HEADS)]
    hsl = [slice(h * HEAD_DIM, (h + 1) * HEAD_DIM) for h in range(N_HEADS)]

    def chunk_body(ci, carry):
        r0 = pl.multiple_of(ci * chunk, chunk)
        rows = pl.ds(r0, chunk)
        g_all, g_t, b_all = [], [], []
        for s in range(sb):
            ga = jnp.dot(ltri, g_s[s, rows, :], precision=HIGHEST, preferred_element_type=F32)
            g_all.append(ga)
            if chunk < LANES:
                ga = jnp.concatenate([ga, jnp.zeros((LANES - chunk, LANES), F32)], axis=0)
            g_t.append(ga.T)
            b_all.append(b_s[s, rows, :])
        g_col = [g_all[s][:, h:h + 1] for s, h in units]
        dec = [jnp.exp(jnp.minimum(g_col[i] - g_t[s][h:h + 1, 0:chunk], 0.0)) for i, (s, h) in enumerate(units)]
        b_col = [b_all[s][:, h:h + 1] for s, h in units]
        q = [q_s[s, rows, hsl[h]] for s, h in units]
        k = [k_s[s, rows, hsl[h]] for s, h in units]
        idx = range(len(units))
        kb = [k[i] * b_col[i] for i in idx]
        kbk = [_dot_nt(kb[i], k[i]) for i in idx]
        qk = [_dot_nt(q[i], k[i]) for i in idx]
        p_mat = [jnp.where(strict, kbk[i] * dec[i], 0.0) for i in idx]
        t_inv = [eye - p_mat[i] for i in idx]
        for _ in range(n_doublings):
            p_mat = [_dot(p_mat[i], p_mat[i]) for i in idx]
            xp = [_dot(t_inv[i], p_mat[i]) for i in idx]
            t_inv = [t_inv[i] + xp[i] for i in idx]
        g_exp = [jnp.exp(g_col[i]) for i in idx]
        wu = [_dot(t_inv[i], jnp.concatenate([kb[i] * g_exp[i], v_s[s, rows, hsl[h]] * b_col[i]], axis=1))
              for i, (s, h) in enumerate(units)]
        s_old = [state[s, h] for s, h in units]
        res = [_dot(jnp.concatenate([wu[i][:, 0:HEAD_DIM], q[i] * g_exp[i]], axis=0), s_old[i]) for i in idx]
        v_new = [wu[i][:, HEAD_DIM:2 * HEAD_DIM] - res[i][0:chunk] for i in idx]
        a_qk = [jnp.where(causal, qk[i] * dec[i], 0.0) for i in idx]
        g_last = [g_all[s][chunk - 1:chunk, h:h + 1] for s, h in units]
        k_dec = [k[i] * jnp.exp(g_last[i] - g_col[i]) for i in idx]
        intra = [_dot(a_qk[i], v_new[i]) for i in idx]
        upd = [_dot_tn(k_dec[i], v_new[i]) for i in idx]
        for i, (s, h) in enumerate(units):
            state[s, h] = s_old[i] * jnp.exp(g_last[i]) + upd[i]
            o_c = res[i][chunk:2 * chunk] + intra[i]
            z_c = proj_ref[s, rows, OFF_Z + h * HEAD_DIM:OFF_Z + (h + 1) * HEAD_DIM]
            ydelta_ref[s, rows, hsl[h]] = _rms(o_c, onw_ref[...]) * (z_c * _sigmoid(z_c))
        return carry

    lax.fori_loop(0, tt // chunk, chunk_body, 0)

    @pl.when(t == n_tiles - 1)
    def _():
        ns_ref[...] = state[...]


def _mixers(proj, pool0, conv0, s0, wpool, pscale, wconv, alog, dtb, onw, *, n_seq, sb, tt, tv, chunk, pos0, n_tiles):
    kern = functools.partial(_mixer_kernel, sb=sb, tt=tt, tv=tv, chunk=chunk, pos0=pos0, n_tiles=n_tiles)
    seq_len = n_tiles * tt
    tile = lambda b, t: (b, t, 0)
    seq3 = lambda b, t: (b, 0, 0)
    seq4 = lambda b, t: (b, 0, 0, 0)
    const2 = lambda b, t: (0, 0)
    ypool, ydelta, npool, nconv, ns = pl.pallas_call(
        kern,
        grid=(n_seq // sb, n_tiles),
        in_specs=[
            pl.BlockSpec((sb, tt, MIX_COLS), tile),
            pl.BlockSpec((sb, POOL_BUF, POOL_DIM), seq3),
            pl.BlockSpec((sb, CONV_WIDTH - 1, CONV_DIM), seq3),
            pl.BlockSpec((sb, N_HEADS, HEAD_DIM, HEAD_DIM), seq4),
            pl.BlockSpec((len(POOL_WINDOWS), POOL_GROUP_DIM, POOL_GROUP_DIM), lambda b, t: (0, 0, 0)),
            pl.BlockSpec((1, POOL_DIM), const2),
            pl.BlockSpec((SUBLANES, CONV_DIM), const2),
            pl.BlockSpec((1, LANES), const2),
            pl.BlockSpec((1, LANES), const2),
            pl.BlockSpec((1, HEAD_DIM), const2),
        ],
        out_specs=[
            pl.BlockSpec((sb, tt, POOL_DIM), tile),
            pl.BlockSpec((sb, tt, QK_DIM), tile),
            pl.BlockSpec((sb, POOL_BUF, POOL_DIM), seq3),
            pl.BlockSpec((sb, CONV_WIDTH - 1, CONV_DIM), seq3),
            pl.BlockSpec((sb, N_HEADS, HEAD_DIM, HEAD_DIM), seq4),
        ],
        out_shape=[
            jax.ShapeDtypeStruct((n_seq, seq_len, POOL_DIM), F32),
            jax.ShapeDtypeStruct((n_seq, seq_len, QK_DIM), F32),
            jax.ShapeDtypeStruct((n_seq, POOL_BUF, POOL_DIM), F32),
            jax.ShapeDtypeStruct((n_seq, CONV_WIDTH - 1, CONV_DIM), F32),
            jax.ShapeDtypeStruct((n_seq, N_HEADS, HEAD_DIM, HEAD_DIM), F32),
        ],
        scratch_shapes=[
            pltpu.VMEM((sb, POOL_ROWS + tt, POOL_DIM), F32),
            pltpu.VMEM((sb, CONV_ROWS + tt, CONV_DIM), F32),
            pltpu.VMEM((sb, tt, QK_DIM), F32),
            pltpu.VMEM((sb, tt, QK_DIM), F32),
            pltpu.VMEM((sb, tt, QK_DIM), F32),
            pltpu.VMEM((sb, tt, LANES), F32),
            pltpu.VMEM((sb, tt, LANES), F32),
            pltpu.VMEM((sb, N_HEADS, HEAD_DIM, HEAD_DIM), F32),
        ],
        compiler_params=pltpu.CompilerParams(dimension_semantics=("arbitrary", "arbitrary"),
                                             vmem_limit_bytes=VMEM_LIMIT),
        name="mixers",
    )(proj.reshape(n_seq, seq_len, MIX_COLS), pool0, conv0, s0, wpool, pscale, wconv, alog, dtb, onw)
    n = n_seq * seq_len
    return ypool.reshape(n, POOL_DIM), ydelta.reshape(n, QK_DIM), npool, nconv, ns


def _merge_kernel(x_ref, gates_ref, ypool_ref, ydelta_ref, wbp_ref, wbd_ref, wout_ref, n2w_ref, wr_hi_ref, wr_lo_ref,
                  br_ref, xm_ref, h2_ref, route_ref, wk_ref, cnt_ref, cnt_s, *, period, valid):
    tm = x_ref.shape[0]

    @pl.when(pl.program_id(0) == 0)
    def _():
        cnt_s[...] = jnp.zeros_like(cnt_s)

    bp = _dot(ypool_ref[...], wbp_ref[...])
    bd = _dot(ydelta_ref[...], wbd_ref[...])
    merged = _sigmoid(gates_ref[:, 0:1024]) * bp + _sigmoid(gates_ref[:, 1024:2048]) * bd
    xm = x_ref[...] + _dot(merged, wout_ref[...])
    xm_ref[...] = xm
    h2 = _rms(xm, n2w_ref[...])
    h2_ref[...] = h2

    h2_hi = h2.astype(BF16)
    h2_lo = (h2 - h2_hi.astype(F32)).astype(BF16)
    logits = (jnp.dot(h2_hi, wr_hi_ref[...], preferred_element_type=F32)
              + (jnp.dot(h2_hi, wr_lo_ref[...], preferred_element_type=F32)
                 + jnp.dot(h2_lo, wr_hi_ref[...], preferred_element_type=F32))) + br_ref[...]
    lane = lax.broadcasted_iota(jnp.int32, logits.shape, 1)
    neg = -jnp.inf
    far = LANES - 1
    is_g = lane < N_GROUPS
    g_max = jnp.max(jnp.where(is_g, logits, neg), axis=-1, keepdims=True)
    g_sel = jnp.min(jnp.where(is_g & (logits == g_max), lane, far), axis=-1, keepdims=True)
    p_g = 1.0 / jnp.sum(jnp.where(is_g, jnp.exp(logits - g_max), 0.0), axis=-1, keepdims=True)
    e_lane = lane - N_GROUPS
    is_e = (e_lane >= 0) & (e_lane < N_EXPERTS) & ((e_lane // EXPERTS_PER_GROUP) == g_sel)
    ev = jnp.where(is_e, logits, neg)
    v1 = jnp.max(ev, axis=-1, keepdims=True)
    i1 = jnp.min(jnp.where(is_e & (ev == v1), lane, far), axis=-1, keepdims=True)
    is_e2 = is_e & (lane != i1)
    ev2 = jnp.where(is_e2, logits, neg)
    v2 = jnp.max(ev2, axis=-1, keepdims=True)
    i2 = jnp.min(jnp.where(is_e2 & (ev2 == v2), lane, far), axis=-1, keepdims=True)
    e21 = jnp.exp(v2 - v1)
    w1 = p_g / (1.0 + e21)
    w2 = p_g * e21 / (1.0 + e21)
    e1 = i1 - N_GROUPS
    e2 = i2 - N_GROUPS
    chosen = (lane == e1) | (lane == e2)
    if valid < period:
        row = lax.broadcasted_iota(jnp.int32, (tm, 1), 0)
        chosen = chosen & (lax.rem(row, period) < valid)
    onehot = jnp.where(chosen, 1.0, 0.0)
    rr = lax.broadcasted_iota(jnp.int32, (tm, tm), 0)
    cc = lax.broadcasted_iota(jnp.int32, (tm, tm), 1)
    earlier = jnp.where(rr > cc, 1.0, 0.0).astype(BF16)
    before = jnp.dot(earlier, onehot.astype(BF16), preferred_element_type=F32) + cnt_s[...]
    r1 = jnp.sum(jnp.where(lane == e1, before, 0.0), axis=-1, keepdims=True).astype(jnp.int32)
    r2 = jnp.sum(jnp.where(lane == e2, before, 0.0), axis=-1, keepdims=True).astype(jnp.int32)
    cnt_s[...] = cnt_s[...] + jnp.sum(onehot, axis=0, keepdims=True)
    cnt_ref[...] = cnt_s[...].astype(jnp.int32)
    lane8 = lax.broadcasted_iota(jnp.int32, (tm, SUBLANES), 1)
    route_ref[...] = jnp.where(lane8 == 0, e1, jnp.where(lane8 == 1, e2, jnp.where(lane8 == 2, r1,
                                                                                    jnp.where(lane8 == 3, r2, 0))))
    wk_ref[...] = jnp.where(lane8 == 0, w1, jnp.where(lane8 == 1, w2, 0.0))


def _merge(x, proj, ypool, ydelta, wbp, wbd, wout, n2w, wr_hi, wr_lo, br, tm, period, valid):
    n, d = x.shape
    row = lambda i: (i, 0)
    const = lambda i: (0, 0)
    return pl.pallas_call(
        functools.partial(_merge_kernel, period=period, valid=valid),
        grid=(n // tm,),
        in_specs=[
            pl.BlockSpec((tm, d), row),
            pl.BlockSpec((tm, GATE_COLS), row),
            pl.BlockSpec((tm, POOL_DIM), row),
            pl.BlockSpec((tm, QK_DIM), row),
            pl.BlockSpec((POOL_DIM, d), const),
            pl.BlockSpec((QK_DIM, d), const),
            pl.BlockSpec((d, d), const),
            pl.BlockSpec((1, d), const),
            pl.BlockSpec((d, LANES), const),
            pl.BlockSpec((d, LANES), const),
            pl.BlockSpec((1, LANES), const),
        ],
        out_specs=[
            pl.BlockSpec((tm, d), row),
            pl.BlockSpec((tm, d), row),
            pl.BlockSpec((tm, SUBLANES), row),
            pl.BlockSpec((tm, SUBLANES), row),
            pl.BlockSpec((1, LANES), const),
        ],
        out_shape=[
            jax.ShapeDtypeStruct((n, d), F32),
            jax.ShapeDtypeStruct((n, d), F32),
            jax.ShapeDtypeStruct((n, SUBLANES), jnp.int32),
            jax.ShapeDtypeStruct((n, SUBLANES), F32),
            jax.ShapeDtypeStruct((1, LANES), jnp.int32),
        ],
        scratch_shapes=[pltpu.VMEM((1, LANES), F32)],
        compiler_params=pltpu.CompilerParams(dimension_semantics=("arbitrary",), vmem_limit_bytes=VMEM_LIMIT),
        name="merge",
    )(x, proj, ypool, ydelta, wbp, wbd, wout, n2w, wr_hi, wr_lo, br)


def _dispatch_kernel(fill_start_ref, fill_len_ref, pos_ref, h2_ref, xs_hbm, stage, zbuf, sem, fill_sem,
                     *, tm, tm_e, n_steps):
    i = pl.program_id(0)
    slot = lax.rem(i, 2)

    def drain(s):
        for _ in range(2):
            pltpu.make_async_copy(stage.at[s], xs_hbm.at[pl.ds(0, tm), :], sem.at[s]).wait()

    @pl.when(i >= 2)
    def _():
        drain(slot)

    stage[slot] = h2_ref[...]

    def body(r, carry):
        for k in range(2):
            p = pos_ref[0, 0, 2 * r + k]
            pltpu.make_async_copy(stage.at[slot, pl.ds(r, 1), :], xs_hbm.at[pl.ds(p, 1), :], sem.at[slot]).start()
        return carry
    lax.fori_loop(0, tm, body, 0, unroll=8)

    @pl.when(i == n_steps - 1)
    def _():
        drain(slot)
        if n_steps >= 2:
            drain(1 - slot)
        zbuf[...] = jnp.zeros_like(zbuf)

        def fill_copy(start, rows):
            return pltpu.make_async_copy(zbuf.at[pl.ds(0, rows), :], xs_hbm.at[pl.ds(start, rows), :], fill_sem)

        def for_each_fill(op):
            for e in range(N_EXPERTS):
                start = fill_start_ref[e]
                n_fill = fill_len_ref[e]
                head = n_fill & (SUBLANES - 1)
                for j in range(SUBLANES - 1):
                    @pl.when(j < head)
                    def _(start=start, j=j):
                        op(fill_copy(start + j, 1))
                start = pl.multiple_of(start + head, SUBLANES)
                bit = tm_e // 2
                while bit >= SUBLANES:
                    take = (n_fill & bit) != 0

                    @pl.when(take)
                    def _(start=start, bit=bit):
                        op(fill_copy(start, bit))
                    start = pl.multiple_of(start + jnp.where(take, bit, 0), SUBLANES)
                    bit //= 2

        for_each_fill(lambda cp: cp.start())
        for_each_fill(lambda cp: cp.wait())

        blk = tm_e // 2
        tail_start = fill_start_ref[N_EXPERTS]
        n_blk = (xs_hbm.shape[0] - tail_start) // blk

        def tail_copy(j):
            row0 = pl.multiple_of(tail_start + j * blk, SUBLANES)
            return pltpu.make_async_copy(zbuf, xs_hbm.at[pl.ds(row0, blk), :], fill_sem)

        def tail_start_body(j, carry):
            tail_copy(j).start()
            return carry

        def tail_wait_body(j, carry):
            tail_copy(j).wait()
            return carry
        lax.fori_loop(0, n_blk, tail_start_body, 0)
        lax.fori_loop(0, n_blk, tail_wait_body, 0)


def _dispatch(fill_start, fill_len, pos, h2, n_rows, tm, tm_e):
    n, d = h2.shape
    n_steps = n // tm
    grid_spec = pltpu.PrefetchScalarGridSpec(
        num_scalar_prefetch=2,
        grid=(n_steps,),
        in_specs=[
            pl.BlockSpec((1, 1, 2 * tm), lambda i, fs, fl: (i, 0, 0), memory_space=pltpu.SMEM),
            pl.BlockSpec((tm, d), lambda i, fs, fl: (i, 0)),
        ],
        out_specs=pl.BlockSpec(memory_space=pl.ANY),
        scratch_shapes=[pltpu.VMEM((2, tm, d), F32), pltpu.VMEM((tm_e // 2, d), F32),
                        pltpu.SemaphoreType.DMA((2,)), pltpu.SemaphoreType.DMA(())],
    )
    return pl.pallas_call(
        functools.partial(_dispatch_kernel, tm=tm, tm_e=tm_e, n_steps=n_steps),
        grid_spec=grid_spec,
        out_shape=jax.ShapeDtypeStruct((n_rows, d), F32),
        compiler_params=pltpu.CompilerParams(dimension_semantics=("arbitrary",), vmem_limit_bytes=VMEM_LIMIT),
        name="dispatch",
    )(fill_start, fill_len, pos, h2)


def _expert_kernel(te_ref, na_ref, x_ref, wg_ref, wu_ref, wd_ref, y_ref):
    i = pl.program_id(0)

    @pl.when(i < na_ref[0])
    def _():
        x = x_ref[...]
        a = _dot(x, wg_ref[0])
        b = _dot(x, wu_ref[0])
        y_ref[...] = _dot((a * _sigmoid(a)) * b, wd_ref[0])

    @pl.when(i >= na_ref[0])
    def _():
        y_ref[...] = jnp.zeros_like(y_ref)


def _experts(tile_expert, n_active, xs, wg, wu, wd, tm):
    n_rows, d = xs.shape
    f = wg.shape[2]
    grid_spec = pltpu.PrefetchScalarGridSpec(
        num_scalar_prefetch=2,
        grid=(n_rows // tm,),
        in_specs=[
            pl.BlockSpec((tm, d), lambda i, te, na: (jnp.where(i < na[0], i, 0), 0)),
            pl.BlockSpec((1, d, f), lambda i, te, na: (te[i], 0, 0)),
            pl.BlockSpec((1, d, f), lambda i, te, na: (te[i], 0, 0)),
            pl.BlockSpec((1, f, d), lambda i, te, na: (te[i], 0, 0)),
        ],
        out_specs=pl.BlockSpec((tm, d), lambda i, te, na: (i, 0)),
    )
    return pl.pallas_call(
        _expert_kernel,
        grid_spec=grid_spec,
        out_shape=jax.ShapeDtypeStruct((n_rows, d), F32),
        compiler_params=pltpu.CompilerParams(dimension_semantics=("arbitrary",), vmem_limit_bytes=VMEM_LIMIT),
        name="experts",
    )(tile_expert, n_active, xs, wg, wu, wd)


def _sc_gather(table, idx):
    info = plsc.get_sparse_core_info()
    n_cores, n_sub = info.num_cores, info.num_subcores
    n_idx = idx.shape[0]
    d = table.shape[1]
    per_worker = n_idx // (n_cores * n_sub)
    assert per_worker * n_cores * n_sub == n_idx and per_worker % SUBLANES == 0
    chunk = max(c for c in range(SUBLANES, SC_GATHER_ROWS + 1, SUBLANES) if per_worker % c == 0)
    mesh = plsc.VectorSubcoreMesh(core_axis_name="c", subcore_axis_name="s")

    @functools.partial(
        pl.kernel, mesh=mesh, out_type=jax.ShapeDtypeStruct((n_idx, d), table.dtype),
        scratch_types=[pltpu.VMEM((chunk,), jnp.int32), pltpu.VMEM((chunk, d), table.dtype), pltpu.SemaphoreType.DMA])
    def gather_rows(table_hbm, idx_hbm, out_hbm, idx_v, rows_v, sem):
        worker = lax.axis_index("s") * n_cores + lax.axis_index("c")
        base = worker * per_worker

        @pl.loop(0, per_worker // chunk)
        def _(j):
            off = pl.multiple_of(base + j * chunk, SUBLANES)
            pltpu.sync_copy(idx_hbm.at[pl.ds(off, chunk)], idx_v)
            pltpu.async_copy(table_hbm.at[idx_v], rows_v, sem).wait()
            pltpu.sync_copy(rows_v, out_hbm.at[pl.ds(off, chunk)])

    return gather_rows(table, idx)


def _finalize_kernel(xm_ref, yk_ref, wk_ref, fnw_ref, o_ref):
    d = xm_ref.shape[1]
    w = wk_ref[...]
    moe = w[:, 0:1] * yk_ref[:, 0:d] + w[:, 1:2] * yk_ref[:, d:2 * d]
    o_ref[...] = _rms(xm_ref[...] + moe, fnw_ref[...])


def _finalize(xm, yk, wk, fnw, tm):
    n, d = xm.shape
    return pl.pallas_call(
        _finalize_kernel,
        grid=(n // tm,),
        in_specs=[
            pl.BlockSpec((tm, d), lambda i: (i, 0)),
            pl.BlockSpec((tm, 2 * d), lambda i: (i, 0)),
            pl.BlockSpec((tm, SUBLANES), lambda i: (i, 0)),
            pl.BlockSpec((1, d), lambda i: (0, 0)),
        ],
        out_specs=pl.BlockSpec((tm, d), lambda i: (i, 0)),
        out_shape=jax.ShapeDtypeStruct((n, d), F32),
        compiler_params=pltpu.CompilerParams(dimension_semantics=("arbitrary",), vmem_limit_bytes=VMEM_LIMIT),
        name="finalize",
    )(xm, yk, wk, fnw)


def _moe(xm, h2, route, wk, counts, wg, wu, wd, fnw, *, tm_e, tm_c):
    n = xm.shape[0]
    counts = counts[0, :N_EXPERTS]
    padded = ((counts + tm_e - 1) // tm_e) * tm_e
    ends = jnp.cumsum(padded)
    starts = ends - padded
    pos = (starts[route[:, 0:2]] + route[:, 2:4]).astype(jnp.int32).reshape(n // tm_c, 1, 2 * tm_c)
    n_rows = 2 * n + N_EXPERTS * tm_e
    tile_start = jnp.arange(n_rows // tm_e, dtype=jnp.int32) * tm_e
    tile_expert = jnp.minimum(
        jnp.sum((ends[None, :] <= tile_start[:, None]).astype(jnp.int32), axis=1), N_EXPERTS - 1)
    n_active = (ends[-1] // tm_e).astype(jnp.int32).reshape(1)
    fill_start = jnp.concatenate([starts + counts, ends[-1:]]).astype(jnp.int32)
    xs = _dispatch(fill_start, (padded - counts).astype(jnp.int32), pos, h2, n_rows, tm_c, tm_e)
    y_sorted = _experts(tile_expert, n_active, xs, wg, wu, wd, tm_e)
    yk = _sc_gather(y_sorted, pos.reshape(-1))
    return _finalize(xm, yk.reshape(n, 2 * yk.shape[1]), wk, fnw, tm_c)


def _layer(x_tokens, pool0, conv0, s0, prm, *, n_seq, sb, tt, tv, chunk, pos0, n_tiles, tm):
    proj, gates = _inproj(x_tokens, prm["norm1_w"], prm["w_in"], tm)
    ypool, ydelta, npool, nconv, ns = _mixers(
        proj, pool0, conv0, s0, prm["w_pool"], prm["pool_scale"], prm["w_conv"], prm["a_log"], prm["dt_bias"],
        prm["o_norm_w"], n_seq=n_seq, sb=sb, tt=tt, tv=tv, chunk=chunk, pos0=pos0, n_tiles=n_tiles)
    xm, h2, route, wk, counts = _merge(
        x_tokens, gates, ypool, ydelta, prm["w_branch_pool"], prm["w_branch_delta"], prm["w_out"], prm["norm2_w"],
        prm["w_router_hi"], prm["w_router_lo"], prm["b_router"], tm, tt, tv)
    return xm, h2, route, wk, counts, npool, nconv, ns


def _pad_lanes(v, width=LANES):
    v = v.reshape(1, -1).astype(F32)
    return jnp.pad(v, ((0, 0), (0, width - v.shape[1])))


def kernel(x_prompt, x_sample, cache_pool, cache_conv, state_delta, norm1_w, w_in, w_pool, pool_scale, w_conv, a_log, dt_bias, o_norm_w, w_branch_pool, w_branch_delta, w_out, norm2_w, w_router_group, b_router_group, w_router_expert, b_router_expert, w_gate, w_up, w_down, final_norm_w):
    n_layers = norm1_w.shape[0]
    assert n_layers == 1, "single-layer step"
    bsz, seq, d = x_prompt.shape
    dbs, dseq, _ = x_sample.shape
    past_len = 16384
    lyr = 0

    wi = w_in[lyr]
    c = [POOL_DIM, POOL_DIM + CONV_DIM]
    c += [c[1] + N_HEADS, c[1] + 2 * N_HEADS]
    c += [c[3] + QK_DIM, c[3] + QK_DIM + d]
    pad8 = lambda m: jnp.pad(m, ((0, 0), (0, LANES - N_HEADS)))
    w_in_r = jnp.concatenate(
        [wi[:, c[3]:c[4]], wi[:, c[0]:c[1]], wi[:, :c[0]], pad8(wi[:, c[1]:c[2]]), pad8(wi[:, c[2]:c[3]]),
         wi[:, c[4]:c[5]], wi[:, c[5]:]], axis=1).astype(BF16)
    w_router = jnp.concatenate(
        [w_router_group[lyr], w_router_expert[lyr].reshape(d, N_EXPERTS),
         jnp.zeros((d, LANES - N_GROUPS - N_EXPERTS), F32)], axis=1).astype(F32)
    w_router_hi = w_router.astype(BF16)
    b_router = _pad_lanes(jnp.concatenate([b_router_group[lyr], b_router_expert[lyr].reshape(-1)]))
    prm = dict(
        norm1_w=norm1_w[lyr].reshape(1, d), w_in=w_in_r,
        w_pool=w_pool[lyr].astype(BF16), pool_scale=pool_scale[lyr].reshape(1, POOL_DIM),
        w_conv=jnp.pad(w_conv[lyr], ((0, SUBLANES - CONV_WIDTH), (0, 0))),
        a_log=_pad_lanes(a_log[lyr]), dt_bias=_pad_lanes(dt_bias[lyr]), o_norm_w=o_norm_w[lyr].reshape(1, HEAD_DIM),
        w_branch_pool=w_branch_pool[lyr].astype(BF16), w_branch_delta=w_branch_delta[lyr].astype(BF16),
        w_out=w_out[lyr].astype(BF16), norm2_w=norm2_w[lyr].reshape(1, d),
        w_router_hi=w_router_hi, w_router_lo=(w_router - w_router_hi.astype(F32)).astype(BF16), b_router=b_router,
    )
    wg = w_gate[lyr].astype(BF16)
    wu = w_up[lyr].astype(BF16)
    wd = w_down[lyr].astype(BF16)
    fnw = final_norm_w.reshape(1, d)

    tt_p = 256
    xm, h2, route, wk, counts, npool, nconv, ns = _layer(
        x_prompt.reshape(bsz * seq, d),
        jnp.zeros((bsz, POOL_BUF, POOL_DIM), F32), jnp.zeros((bsz, CONV_WIDTH - 1, CONV_DIM), F32),
        jnp.zeros((bsz, N_HEADS, HEAD_DIM, HEAD_DIM), F32), prm,
        n_seq=bsz, sb=2, tt=tt_p, tv=tt_p, chunk=64, pos0=0, n_tiles=seq // tt_p, tm=256)
    y_prompt = _moe(xm, h2, route, wk, counts, wg, wu, wd, fnw, tm_e=256, tm_c=256).reshape(bsz, seq, d)
    pool_p = npool[None]
    conv_p = nconv[None]
    delta_p = ns[None]

    tt_s = SUBLANES
    xs_pad = jnp.pad(x_sample, ((0, 0), (0, tt_s - dseq), (0, 0))).reshape(dbs * tt_s, d)
    xm, h2, route, wk, counts, npool, nconv, ns = _layer(
        xs_pad, cache_pool[lyr], cache_conv[lyr], state_delta[lyr], prm,
        n_seq=dbs, sb=4, tt=tt_s, tv=dseq, chunk=tt_s, pos0=past_len, n_tiles=1, tm=256)
    real = lambda a: a.reshape(dbs, tt_s, -1)[:, :dseq].reshape(dbs * dseq, -1)
    y_sample = _moe(real(xm), real(h2), real(route), real(wk), counts, wg, wu, wd, fnw,
                    tm_e=128, tm_c=256).reshape(dbs, dseq, d)
    pool_s = npool[None]
    conv_s = nconv[None]
    delta_s = ns[None]
    return (y_prompt, y_sample, pool_p, conv_p, delta_p, pool_s, conv_s, delta_s)
```

```python
import functools

import jax
import jax.numpy as jnp
from jax import lax
from jax.experimental import pallas as pl
from jax.experimental.pallas import tpu as pltpu
from jax.experimental.pallas import tpu_sc as plsc

F32 = jnp.float32
BF16 = jnp.bfloat16
EPS = 1e-6
HIGHEST = lax.Precision.HIGHEST

POOL_WINDOWS = (2, 4, 8, 16)
POOL_GROUP_DIM = 128
POOL_DIM = len(POOL_WINDOWS) * POOL_GROUP_DIM
POOL_BUF = max(POOL_WINDOWS) - 1
N_HEADS = 8
HEAD_DIM = 128
QK_DIM = N_HEADS * HEAD_DIM
CONV_WIDTH = 4
CONV_DIM = 3 * QK_DIM
N_GROUPS = 4
EXPERTS_PER_GROUP = 8
N_EXPERTS = N_GROUPS * EXPERTS_PER_GROUP
LANES = 128
SUBLANES = 8

OFF_Z = 0
OFF_QKV = OFF_Z + QK_DIM
OFF_POOL = OFF_QKV + CONV_DIM
OFF_B = OFF_POOL + POOL_DIM
OFF_A = OFF_B + LANES
MIX_COLS = OFF_A + LANES
GATE_COLS = 2048
IN_COLS = MIX_COLS + GATE_COLS

POOL_ROWS = 16
CONV_ROWS = 8
VMEM_LIMIT = 56 * 1024 * 1024
SC_ROWS = 96


def _sigmoid(x):
    return 1.0 / (1.0 + jnp.exp(-x))


def _softplus(x):
    return jnp.maximum(x, 0.0) + jnp.log1p(jnp.exp(-jnp.abs(x)))


def _rms(x, w):
    return x * lax.rsqrt(jnp.mean(x * x, axis=-1, keepdims=True) + EPS) * w


def _dot(a, b):
    return jnp.dot(a.astype(BF16), b.astype(BF16), preferred_element_type=F32)


def _dot_nt(a, b):
    return lax.dot_general(a.astype(BF16), b.astype(BF16), (((1,), (1,)), ((), ())), preferred_element_type=F32)


def _dot_tn(a, b):
    return lax.dot_general(a.astype(BF16), b.astype(BF16), (((0,), (0,)), ((), ())), preferred_element_type=F32)


def _inproj_kernel(x_ref, nw_ref, w_ref, mix_ref, gate_ref):
    h = _rms(x_ref[...], nw_ref[...]).astype(BF16)
    half = MIX_COLS // 2
    for c0 in (0, half):
        mix_ref[:, c0:c0 + half] = jnp.dot(h, w_ref[:, c0:c0 + half], preferred_element_type=F32)
    half = GATE_COLS // 2
    for c0 in (0, half):
        gate_ref[:, c0:c0 + half] = jnp.dot(h, w_ref[:, MIX_COLS + c0:MIX_COLS + c0 + half],
                                            preferred_element_type=F32)


def _inproj(x, norm_w, w_in_r, tm):
    n, d = x.shape
    return pl.pallas_call(
        _inproj_kernel,
        grid=(n // tm,),
        in_specs=[
            pl.BlockSpec((tm, d), lambda i: (i, 0)),
            pl.BlockSpec((1, d), lambda i: (0, 0)),
            pl.BlockSpec((d, IN_COLS), lambda i: (0, 0)),
        ],
        out_specs=[pl.BlockSpec((tm, MIX_COLS), lambda i: (i, 0)), pl.BlockSpec((tm, GATE_COLS), lambda i: (i, 0))],
        out_shape=[jax.ShapeDtypeStruct((n, MIX_COLS), F32), jax.ShapeDtypeStruct((n, GATE_COLS), F32)],
        compiler_params=pltpu.CompilerParams(dimension_semantics=("arbitrary",), vmem_limit_bytes=VMEM_LIMIT),
        name="inproj",
    )(x, norm_w, w_in_r)


def _mixer_kernel(proj_ref, pool0_ref, conv0_ref, s0_ref, wpool_ref, pscale_ref, wconv_ref, alog_ref, dtb_ref, onw_ref,
                  ypool_ref, ydelta_ref, npool_ref, nconv_ref, ns_ref,
                  ext_p, ext_c, q_s, k_s, v_s, g_s, b_s, state,
                  *, sb, tt, tv, chunk, pos0, n_tiles):
    t = pl.program_id(1)
    base = CONV_ROWS - (CONV_WIDTH - 1)

    @pl.when(t == 0)
    def _():
        ext_p[:, POOL_ROWS - POOL_BUF:POOL_ROWS, :] = pool0_ref[...]
        ext_c[:, base:CONV_ROWS, :] = conv0_ref[...]
        state[...] = s0_ref[...]

    row = lax.broadcasted_iota(jnp.int32, (tt, 1), 0)
    pos = pos0 + t * tt + row
    for s in range(sb):
        ext_p[s, POOL_ROWS:POOL_ROWS + tt, :] = proj_ref[s, :, OFF_POOL:OFF_POOL + POOL_DIM]
        ext_c[s, CONV_ROWS:CONV_ROWS + tt, :] = proj_ref[s, :, OFF_QKV:OFF_QKV + CONV_DIM]

        for gi, win in enumerate(POOL_WINDOWS):
            cs = slice(gi * POOL_GROUP_DIM, (gi + 1) * POOL_GROUP_DIM)
            u = ext_p[s, POOL_ROWS:POOL_ROWS + tt, cs]
            acc = u
            for j in range(1, win):
                acc = acc + ext_p[s, POOL_ROWS - j:POOL_ROWS - j + tt, cs]
            cnt = jnp.minimum(win, pos + 1).astype(F32)
            pooled = acc / cnt - u
            ypool_ref[s, :, cs] = _dot(pooled, wpool_ref[gi]) * pscale_ref[:, cs]

        for ci in range(CONV_DIM // LANES):
            cs = slice(ci * LANES, (ci + 1) * LANES)
            acc = ext_c[s, base:base + tt, cs] * wconv_ref[0:1, cs]
            for j in range(1, CONV_WIDTH):
                acc = acc + ext_c[s, base + j:base + j + tt, cs] * wconv_ref[j:j + 1, cs]
            y = acc * _sigmoid(acc)
            hs = slice((ci % N_HEADS) * HEAD_DIM, (ci % N_HEADS + 1) * HEAD_DIM)
            if ci < 2 * N_HEADS:
                y = y * lax.rsqrt(jnp.sum(y * y, axis=-1, keepdims=True) + EPS)
            if ci < N_HEADS:
                q_s[s, :, hs] = y * (HEAD_DIM ** -0.5)
            elif ci < 2 * N_HEADS:
                k_s[s, :, hs] = y
            else:
                v_s[s, :, hs] = y

        beta = _sigmoid(proj_ref[s, :, OFF_B:OFF_B + LANES])
        g = -jnp.exp(alog_ref[...]) * _softplus(proj_ref[s, :, OFF_A:OFF_A + LANES] + dtb_ref[...])
        if tv < tt:
            beta = jnp.where(row < tv, beta, 0.0)
            g = jnp.where(row < tv, g, 0.0)
        b_s[s] = beta
        g_s[s] = g

    @pl.when(t == n_tiles - 1)
    def _():
        npool_ref[...] = ext_p[:, tv + POOL_ROWS - POOL_BUF:tv + POOL_ROWS, :]
        nconv_ref[...] = ext_c[:, tv + base:tv + CONV_ROWS, :]

    if n_tiles > 1:
        ext_p[:, 0:POOL_ROWS, :] = ext_p[:, tt:tt + POOL_ROWS, :]
        ext_c[:, 0:CONV_ROWS, :] = ext_c[:, tt:tt + CONV_ROWS, :]

    ri = lax.broadcasted_iota(jnp.int32, (chunk, chunk), 0)
    cj = lax.broadcasted_iota(jnp.int32, (chunk, chunk), 1)
    causal = ri >= cj
    strict = ri > cj
    ltri = causal.astype(F32)
    eye = (ri == cj).astype(F32)
    n_doublings = chunk.bit_length() - 2
    units = [(s, h) for s in range(sb) for h in range(N_HEADS)]
    hsl = [slice(h * HEAD_DIM, (h + 1) * HEAD_DIM) for h in range(N_HEADS)]

    def chunk_body(ci, carry):
        r0 = pl.multiple_of(ci * chunk, chunk)
        rows = pl.ds(r0, chunk)
        g_all, g_t, b_all = [], [], []
        for s in range(sb):
            ga = jnp.dot(ltri, g_s[s, rows, :], precision=HIGHEST, preferred_element_type=F32)
            g_all.append(ga)
            if chunk < LANES:
                ga = jnp.concatenate([ga, jnp.zeros((LANES - chunk, LANES), F32)], axis=0)
            g_t.append(ga.T)
            b_all.append(b_s[s, rows, :])
        g_col = [g_all[s][:, h:h + 1] for s, h in units]
        dec = [jnp.exp(jnp.minimum(g_col[i] - g_t[s][h:h + 1, 0:chunk], 0.0)) for i, (s, h) in enumerate(units)]
        b_col = [b_all[s][:, h:h + 1] for s, h in units]
        q = [q_s[s, rows, hsl[h]] for s, h in units]
        k = [k_s[s, rows, hsl[h]] for s, h in units]
        idx = range(len(units))
        kb = [k[i] * b_col[i] for i in idx]
        kbk = [_dot_nt(kb[i], k[i]) for i in idx]
        qk = [_dot_nt(q[i], k[i]) for i in idx]
        p_mat = [jnp.where(strict, kbk[i] * dec[i], 0.0) for i in idx]
        t_inv = [eye - p_mat[i] for i in idx]
        for _ in range(n_doublings):
            p_mat = [_dot(p_mat[i], p_mat[i]) for i in idx]
            xp = [_dot(t_inv[i], p_mat[i]) for i in idx]
            t_inv = [t_inv[i] + xp[i] for i in idx]
        g_exp = [jnp.exp(g_col[i]) for i in idx]
        wu = [_dot(t_inv[i], jnp.concatenate([kb[i] * g_exp[i], v_s[s, rows, hsl[h]] * b_col[i]], axis=1))
              for i, (s, h) in enumerate(units)]
        s_old = [state[s, h] for s, h in units]
        res = [_dot(jnp.concatenate([wu[i][:, 0:HEAD_DIM], q[i] * g_exp[i]], axis=0), s_old[i]) for i in idx]
        v_new = [wu[i][:, HEAD_DIM:2 * HEAD_DIM] - res[i][0:chunk] for i in idx]
        a_qk = [jnp.where(causal, qk[i] * dec[i], 0.0) for i in idx]
        g_last = [g_all[s][chunk - 1:chunk, h:h + 1] for s, h in units]
        k_dec = [k[i] * jnp.exp(g_last[i] - g_col[i]) for i in idx]
        intra = [_dot(a_qk[i], v_new[i]) for i in idx]
        upd = [_dot_tn(k_dec[i], v_new[i]) for i in idx]
        for i, (s, h) in enumerate(units):
            state[s, h] = s_old[i] * jnp.exp(g_last[i]) + upd[i]
            o_c = res[i][chunk:2 * chunk] + intra[i]
            z_c = proj_ref[s, rows, OFF_Z + h * HEAD_DIM:OFF_Z + (h + 1) * HEAD_DIM]
            ydelta_ref[s, rows, hsl[h]] = _rms(o_c, onw_ref[...]) * (z_c * _sigmoid(z_c))
        return carry

    lax.fori_loop(0, tt // chunk, chunk_body, 0)

    @pl.when(t == n_tiles - 1)
    def _():
        ns_ref[...] = state[...]


def _mixers(proj, pool0, conv0, s0, wpool, pscale, wconv, alog, dtb, onw, *, n_seq, sb, tt, tv, chunk, pos0, n_tiles):
    kern = functools.partial(_mixer_kernel, sb=sb, tt=tt, tv=tv, chunk=chunk, pos0=pos0, n_tiles=n_tiles)
    seq_len = n_tiles * tt
    tile = lambda b, t: (b, t, 0)
    seq3 = lambda b, t: (b, 0, 0)
    seq4 = lambda b, t: (b, 0, 0, 0)
    const2 = lambda b, t: (0, 0)
    ypool, ydelta, npool, nconv, ns = pl.pallas_call(
        kern,
        grid=(n_seq // sb, n_tiles),
        in_specs=[
            pl.BlockSpec((sb, tt, MIX_COLS), tile),
            pl.BlockSpec((sb, POOL_BUF, POOL_DIM), seq3),
            pl.BlockSpec((sb, CONV_WIDTH - 1, CONV_DIM), seq3),
            pl.BlockSpec((sb, N_HEADS, HEAD_DIM, HEAD_DIM), seq4),
            pl.BlockSpec((len(POOL_WINDOWS), POOL_GROUP_DIM, POOL_GROUP_DIM), lambda b, t: (0, 0, 0)),
            pl.BlockSpec((1, POOL_DIM), const2),
            pl.BlockSpec((SUBLANES, CONV_DIM), const2),
            pl.BlockSpec((1, LANES), const2),
            pl.BlockSpec((1, LANES), const2),
            pl.BlockSpec((1, HEAD_DIM), const2),
        ],
        out_specs=[
            pl.BlockSpec((sb, tt, POOL_DIM), tile),
            pl.BlockSpec((sb, tt, QK_DIM), tile),
            pl.BlockSpec((sb, POOL_BUF, POOL_DIM), seq3),
            pl.BlockSpec((sb, CONV_WIDTH - 1, CONV_DIM), seq3),
            pl.BlockSpec((sb, N_HEADS, HEAD_DIM, HEAD_DIM), seq4),
        ],
        out_shape=[
            jax.ShapeDtypeStruct((n_seq, seq_len, POOL_DIM), F32),
            jax.ShapeDtypeStruct((n_seq, seq_len, QK_DIM), F32),
            jax.ShapeDtypeStruct((n_seq, POOL_BUF, POOL_DIM), F32),
            jax.ShapeDtypeStruct((n_seq, CONV_WIDTH - 1, CONV_DIM), F32),
            jax.ShapeDtypeStruct((n_seq, N_HEADS, HEAD_DIM, HEAD_DIM), F32),
        ],
        scratch_shapes=[
            pltpu.VMEM((sb, POOL_ROWS + tt, POOL_DIM), F32),
            pltpu.VMEM((sb, CONV_ROWS + tt, CONV_DIM), F32),
            pltpu.VMEM((sb, tt, QK_DIM), F32),
            pltpu.VMEM((sb, tt, QK_DIM), F32),
            pltpu.VMEM((sb, tt, QK_DIM), F32),
            pltpu.VMEM((sb, tt, LANES), F32),
            pltpu.VMEM((sb, tt, LANES), F32),
            pltpu.VMEM((sb, N_HEADS, HEAD_DIM, HEAD_DIM), F32),
        ],
        compiler_params=pltpu.CompilerParams(dimension_semantics=("arbitrary", "arbitrary"),
                                             vmem_limit_bytes=VMEM_LIMIT),
        name="mixers",
    )(proj.reshape(n_seq, seq_len, MIX_COLS), pool0, conv0, s0, wpool, pscale, wconv, alog, dtb, onw)
    n = n_seq * seq_len
    return ypool.reshape(n, POOL_DIM), ydelta.reshape(n, QK_DIM), npool, nconv, ns


def _merge_kernel(x_ref, gates_ref, ypool_ref, ydelta_ref, wbp_ref, wbd_ref, wout_ref, n2w_ref, wr_hi_ref, wr_lo_ref,
                  br_ref, xm_ref, h2_ref, route_ref, wk_ref, cnt_ref, cnt_s, *, period, valid):
    tm = x_ref.shape[0]

    @pl.when(pl.program_id(0) == 0)
    def _():
        cnt_s[...] = jnp.zeros_like(cnt_s)

    bp = _dot(ypool_ref[...], wbp_ref[...])
    bd = _dot(ydelta_ref[...], wbd_ref[...])
    merged = _sigmoid(gates_ref[:, 0:1024]) * bp + _sigmoid(gates_ref[:, 1024:2048]) * bd
    xm = x_ref[...] + _dot(merged, wout_ref[...])
    xm_ref[...] = xm
    h2 = _rms(xm, n2w_ref[...])
    h2_ref[...] = h2

    h2_hi = h2.astype(BF16)
    h2_lo = (h2 - h2_hi.astype(F32)).astype(BF16)
    logits = (jnp.dot(h2_hi, wr_hi_ref[...], preferred_element_type=F32)
              + (jnp.dot(h2_hi, wr_lo_ref[...], preferred_element_type=F32)
                 + jnp.dot(h2_lo, wr_hi_ref[...], preferred_element_type=F32))) + br_ref[...]
    lane = lax.broadcasted_iota(jnp.int32, logits.shape, 1)
    neg = -jnp.inf
    far = LANES - 1
    is_g = lane < N_GROUPS
    g_max = jnp.max(jnp.where(is_g, logits, neg), axis=-1, keepdims=True)
    g_sel = jnp.min(jnp.where(is_g & (logits == g_max), lane, far), axis=-1, keepdims=True)
    p_g = 1.0 / jnp.sum(jnp.where(is_g, jnp.exp(logits - g_max), 0.0), axis=-1, keepdims=True)
    e_lane = lane - N_GROUPS
    is_e = (e_lane >= 0) & (e_lane < N_EXPERTS) & ((e_lane // EXPERTS_PER_GROUP) == g_sel)
    ev = jnp.where(is_e, logits, neg)
    v1 = jnp.max(ev, axis=-1, keepdims=True)
    i1 = jnp.min(jnp.where(is_e & (ev == v1), lane, far), axis=-1, keepdims=True)
    is_e2 = is_e & (lane != i1)
    ev2 = jnp.where(is_e2, logits, neg)
    v2 = jnp.max(ev2, axis=-1, keepdims=True)
    i2 = jnp.min(jnp.where(is_e2 & (ev2 == v2), lane, far), axis=-1, keepdims=True)
    e21 = jnp.exp(v2 - v1)
    w1 = p_g / (1.0 + e21)
    w2 = p_g * e21 / (1.0 + e21)
    e1 = i1 - N_GROUPS
    e2 = i2 - N_GROUPS
    chosen = (lane == e1) | (lane == e2)
    if valid < period:
        row = lax.broadcasted_iota(jnp.int32, (tm, 1), 0)
        chosen = chosen & (lax.rem(row, period) < valid)
    onehot = jnp.where(chosen, 1.0, 0.0)
    rr = lax.broadcasted_iota(jnp.int32, (tm, tm), 0)
    cc = lax.broadcasted_iota(jnp.int32, (tm, tm), 1)
    earlier = jnp.where(rr > cc, 1.0, 0.0).astype(BF16)
    before = jnp.dot(earlier, onehot.astype(BF16), preferred_element_type=F32) + cnt_s[...]
    r1 = jnp.sum(jnp.where(lane == e1, before, 0.0), axis=-1, keepdims=True).astype(jnp.int32)
    r2 = jnp.sum(jnp.where(lane == e2, before, 0.0), axis=-1, keepdims=True).astype(jnp.int32)
    cnt_s[...] = cnt_s[...] + jnp.sum(onehot, axis=0, keepdims=True)
    cnt_ref[...] = cnt_s[...].astype(jnp.int32)
    lane8 = lax.broadcasted_iota(jnp.int32, (tm, SUBLANES), 1)
    route_ref[...] = jnp.where(lane8 == 0, e1, jnp.where(lane8 == 1, e2, jnp.where(lane8 == 2, r1,
                                                                                    jnp.where(lane8 == 3, r2, 0))))
    wk_ref[...] = jnp.where(lane8 == 0, w1, jnp.where(lane8 == 1, w2, 0.0))


def _merge(x, proj, ypool, ydelta, wbp, wbd, wout, n2w, wr_hi, wr_lo, br, tm, period, valid):
    n, d = x.shape
    row = lambda i: (i, 0)
    const = lambda i: (0, 0)
    return pl.pallas_call(
        functools.partial(_merge_kernel, period=period, valid=valid),
        grid=(n // tm,),
        in_specs=[
            pl.BlockSpec((tm, d), row),
            pl.BlockSpec((tm, GATE_COLS), row),
            pl.BlockSpec((tm, POOL_DIM), row),
            pl.BlockSpec((tm, QK_DIM), row),
            pl.BlockSpec((POOL_DIM, d), const),
            pl.BlockSpec((QK_DIM, d), const),
            pl.BlockSpec((d, d), const),
            pl.BlockSpec((1, d), const),
            pl.BlockSpec((d, LANES), const),
            pl.BlockSpec((d, LANES), const),
            pl.BlockSpec((1, LANES), const),
        ],
        out_specs=[
            pl.BlockSpec((tm, d), row),
            pl.BlockSpec((tm, d), row),
            pl.BlockSpec((tm, SUBLANES), row),
            pl.BlockSpec((tm, SUBLANES), row),
            pl.BlockSpec((1, LANES), const),
        ],
        out_shape=[
            jax.ShapeDtypeStruct((n, d), F32),
            jax.ShapeDtypeStruct((n, d), F32),
            jax.ShapeDtypeStruct((n, SUBLANES), jnp.int32),
            jax.ShapeDtypeStruct((n, SUBLANES), F32),
            jax.ShapeDtypeStruct((1, LANES), jnp.int32),
        ],
        scratch_shapes=[pltpu.VMEM((1, LANES), F32)],
        compiler_params=pltpu.CompilerParams(dimension_semantics=("arbitrary",), vmem_limit_bytes=VMEM_LIMIT),
        name="merge",
    )(x, proj, ypool, ydelta, wbp, wbd, wout, n2w, wr_hi, wr_lo, br)


def _expert_kernel(te_ref, nv_ref, x_ref, wg_ref, wu_ref, wd_ref, y_ref):
    i = pl.program_id(0)
    n_valid = nv_ref[i]

    @pl.when(n_valid > 0)
    def _():
        row = lax.broadcasted_iota(jnp.int32, (x_ref.shape[0], 1), 0)
        x = jnp.where(row < n_valid, x_ref[...], 0.0)
        a = _dot(x, wg_ref[0])
        b = _dot(x, wu_ref[0])
        y_ref[...] = _dot((a * _sigmoid(a)) * b, wd_ref[0])

    @pl.when(n_valid == 0)
    def _():
        y_ref[...] = jnp.zeros_like(y_ref)


def _experts(tile_expert, tile_valid, xs, wg, wu, wd, tm):
    n_rows, d = xs.shape
    f = wg.shape[2]
    grid_spec = pltpu.PrefetchScalarGridSpec(
        num_scalar_prefetch=2,
        grid=(n_rows // tm,),
        in_specs=[
            pl.BlockSpec((tm, d), lambda i, te, nv: (jnp.where(nv[i] > 0, i, 0), 0)),
            pl.BlockSpec((1, d, f), lambda i, te, nv: (te[i], 0, 0)),
            pl.BlockSpec((1, d, f), lambda i, te, nv: (te[i], 0, 0)),
            pl.BlockSpec((1, f, d), lambda i, te, nv: (te[i], 0, 0)),
        ],
        out_specs=pl.BlockSpec((tm, d), lambda i, te, nv: (i, 0)),
    )
    return pl.pallas_call(
        _expert_kernel,
        grid_spec=grid_spec,
        out_shape=jax.ShapeDtypeStruct((n_rows, d), F32),
        compiler_params=pltpu.CompilerParams(dimension_semantics=("arbitrary",), vmem_limit_bytes=VMEM_LIMIT),
        name="experts",
    )(tile_expert, tile_valid, xs, wg, wu, wd)


def _sc_workers():
    info = plsc.get_sparse_core_info()
    return info.num_cores, info.num_subcores


def _sc_chunk(per_worker):
    assert per_worker % SUBLANES == 0
    return max(c for c in range(SUBLANES, SC_ROWS + 1, SUBLANES) if per_worker % c == 0)


def _sc_scatter2(rows, idx_a, idx_b, n_out):
    n_cores, n_sub = _sc_workers()
    n, d = rows.shape
    per_worker = n // (n_cores * n_sub)
    assert per_worker * n_cores * n_sub == n
    chunk = _sc_chunk(per_worker)
    mesh = plsc.VectorSubcoreMesh(core_axis_name="c", subcore_axis_name="s")

    @functools.partial(
        pl.kernel, mesh=mesh, out_type=jax.ShapeDtypeStruct((n_out, d), rows.dtype),
        scratch_types=[pltpu.VMEM((chunk,), jnp.int32), pltpu.VMEM((chunk,), jnp.int32),
                       pltpu.VMEM((chunk, d), rows.dtype), pltpu.SemaphoreType.DMA])
    def scatter_rows(rows_hbm, ia_hbm, ib_hbm, out_hbm, ia_v, ib_v, rows_v, sem):
        worker = lax.axis_index("s") * n_cores + lax.axis_index("c")
        base = worker * per_worker

        @pl.loop(0, per_worker // chunk)
        def _(j):
            off = pl.multiple_of(base + j * chunk, SUBLANES)
            pltpu.sync_copy(ia_hbm.at[pl.ds(off, chunk)], ia_v)
            pltpu.sync_copy(ib_hbm.at[pl.ds(off, chunk)], ib_v)
            pltpu.sync_copy(rows_hbm.at[pl.ds(off, chunk)], rows_v)
            pltpu.async_copy(rows_v, out_hbm.at[ia_v], sem).wait()
            pltpu.async_copy(rows_v, out_hbm.at[ib_v], sem).wait()

    return scatter_rows(rows, idx_a, idx_b)


def _sc_gather(table, idx):
    n_cores, n_sub = _sc_workers()
    n_idx = idx.shape[0]
    d = table.shape[1]
    per_worker = n_idx // (n_cores * n_sub)
    assert per_worker * n_cores * n_sub == n_idx
    chunk = _sc_chunk(per_worker)
    mesh = plsc.VectorSubcoreMesh(core_axis_name="c", subcore_axis_name="s")

    @functools.partial(
        pl.kernel, mesh=mesh, out_type=jax.ShapeDtypeStruct((n_idx, d), table.dtype),
        scratch_types=[pltpu.VMEM((chunk,), jnp.int32), pltpu.VMEM((chunk, d), table.dtype), pltpu.SemaphoreType.DMA])
    def gather_rows(table_hbm, idx_hbm, out_hbm, idx_v, rows_v, sem):
        worker = lax.axis_index("s") * n_cores + lax.axis_index("c")
        base = worker * per_worker

        @pl.loop(0, per_worker // chunk)
        def _(j):
            off = pl.multiple_of(base + j * chunk, SUBLANES)
            pltpu.sync_copy(idx_hbm.at[pl.ds(off, chunk)], idx_v)
            pltpu.async_copy(table_hbm.at[idx_v], rows_v, sem).wait()
            pltpu.sync_copy(rows_v, out_hbm.at[pl.ds(off, chunk)])

    return gather_rows(table, idx)


def _finalize_kernel(xm_ref, ya_ref, yb_ref, wk_ref, fnw_ref, o_ref):
    w = wk_ref[...]
    moe = w[:, 0:1] * ya_ref[...] + w[:, 1:2] * yb_ref[...]
    o_ref[...] = _rms(xm_ref[...] + moe, fnw_ref[...])


def _finalize(xm, yk, wk, fnw, tm):
    n, d = xm.shape
    steps = n // tm
    return pl.pallas_call(
        _finalize_kernel,
        grid=(steps,),
        in_specs=[
            pl.BlockSpec((tm, d), lambda i: (i, 0)),
            pl.BlockSpec((tm, d), lambda i: (i, 0)),
            pl.BlockSpec((tm, d), lambda i: (i + steps, 0)),
            pl.BlockSpec((tm, SUBLANES), lambda i: (i, 0)),
            pl.BlockSpec((1, d), lambda i: (0, 0)),
        ],
        out_specs=pl.BlockSpec((tm, d), lambda i: (i, 0)),
        out_shape=jax.ShapeDtypeStruct((n, d), F32),
        compiler_params=pltpu.CompilerParams(dimension_semantics=("arbitrary",), vmem_limit_bytes=VMEM_LIMIT),
        name="finalize",
    )(xm, yk, yk, wk, fnw)


def _moe(xm, h2, route, wk, counts, wg, wu, wd, fnw, *, tm_e, tm_c):
    n = xm.shape[0]
    counts = counts[0, :N_EXPERTS]
    padded = ((counts + tm_e - 1) // tm_e) * tm_e
    ends = jnp.cumsum(padded)
    starts = ends - padded
    pos_a = (starts[route[:, 0]] + route[:, 2]).astype(jnp.int32)
    pos_b = (starts[route[:, 1]] + route[:, 3]).astype(jnp.int32)
    n_rows = 2 * n + N_EXPERTS * tm_e
    tile_start = jnp.arange(n_rows // tm_e, dtype=jnp.int32) * tm_e
    tile_expert = jnp.minimum(
        jnp.sum((ends[None, :] <= tile_start[:, None]).astype(jnp.int32), axis=1), N_EXPERTS - 1)
    tile_valid = jnp.clip(counts[tile_expert] - (tile_start - starts[tile_expert]), 0, tm_e).astype(jnp.int32)
    xs = _sc_scatter2(h2, pos_a, pos_b, n_rows)
    y_sorted = _experts(tile_expert, tile_valid, xs, wg, wu, wd, tm_e)
    yk = _sc_gather(y_sorted, jnp.concatenate([pos_a, pos_b]))
    return _finalize(xm, yk, wk, fnw, tm_c)


def _layer(x_tokens, pool0, conv0, s0, prm, *, n_seq, sb, tt, tv, chunk, pos0, n_tiles, tm):
    proj, gates = _inproj(x_tokens, prm["norm1_w"], prm["w_in"], tm)
    ypool, ydelta, npool, nconv, ns = _mixers(
        proj, pool0, conv0, s0, prm["w_pool"], prm["pool_scale"], prm["w_conv"], prm["a_log"], prm["dt_bias"],
        prm["o_norm_w"], n_seq=n_seq, sb=sb, tt=tt, tv=tv, chunk=chunk, pos0=pos0, n_tiles=n_tiles)
    xm, h2, route, wk, counts = _merge(
        x_tokens, gates, ypool, ydelta, prm["w_branch_pool"], prm["w_branch_delta"], prm["w_out"], prm["norm2_w"],
        prm["w_router_hi"], prm["w_router_lo"], prm["b_router"], tm, tt, tv)
    return xm, h2, route, wk, counts, npool, nconv, ns


def _pad_lanes(v, width=LANES):
    v = v.reshape(1, -1).astype(F32)
    return jnp.pad(v, ((0, 0), (0, width - v.shape[1])))


def kernel(x_prompt, x_sample, cache_pool, cache_conv, state_delta, norm1_w, w_in, w_pool, pool_scale, w_conv, a_log, dt_bias, o_norm_w, w_branch_pool, w_branch_delta, w_out, norm2_w, w_router_group, b_router_group, w_router_expert, b_router_expert, w_gate, w_up, w_down, final_norm_w):
    n_layers = norm1_w.shape[0]
    assert n_layers == 1, "single-layer step"
    bsz, seq, d = x_prompt.shape
    dbs, dseq, _ = x_sample.shape
    past_len = 16384
    lyr = 0

    wi = w_in[lyr]
    c = [POOL_DIM, POOL_DIM + CONV_DIM]
    c += [c[1] + N_HEADS, c[1] + 2 * N_HEADS]
    c += [c[3] + QK_DIM, c[3] + QK_DIM + d]
    pad8 = lambda m: jnp.pad(m, ((0, 0), (0, LANES - N_HEADS)))
    w_in_r = jnp.concatenate(
        [wi[:, c[3]:c[4]], wi[:, c[0]:c[1]], wi[:, :c[0]], pad8(wi[:, c[1]:c[2]]), pad8(wi[:, c[2]:c[3]]),
         wi[:, c[4]:c[5]], wi[:, c[5]:]], axis=1).astype(BF16)
    w_router = jnp.concatenate(
        [w_router_group[lyr], w_router_expert[lyr].reshape(d, N_EXPERTS),
         jnp.zeros((d, LANES - N_GROUPS - N_EXPERTS), F32)], axis=1).astype(F32)
    w_router_hi = w_router.astype(BF16)
    b_router = _pad_lanes(jnp.concatenate([b_router_group[lyr], b_router_expert[lyr].reshape(-1)]))
    prm = dict(
        norm1_w=norm1_w[lyr].reshape(1, d), w_in=w_in_r,
        w_pool=w_pool[lyr].astype(BF16), pool_scale=pool_scale[lyr].reshape(1, POOL_DIM),
        w_conv=jnp.pad(w_conv[lyr], ((0, SUBLANES - CONV_WIDTH), (0, 0))),
        a_log=_pad_lanes(a_log[lyr]), dt_bias=_pad_lanes(dt_bias[lyr]), o_norm_w=o_norm_w[lyr].reshape(1, HEAD_DIM),
        w_branch_pool=w_branch_pool[lyr].astype(BF16), w_branch_delta=w_branch_delta[lyr].astype(BF16),
        w_out=w_out[lyr].astype(BF16), norm2_w=norm2_w[lyr].reshape(1, d),
        w_router_hi=w_router_hi, w_router_lo=(w_router - w_router_hi.astype(F32)).astype(BF16), b_router=b_router,
    )
    wg = w_gate[lyr].astype(BF16)
    wu = w_up[lyr].astype(BF16)
    wd = w_down[lyr].astype(BF16)
    fnw = final_norm_w.reshape(1, d)

    tt_p = 256
    xm, h2, route, wk, counts, npool, nconv, ns = _layer(
        x_prompt.reshape(bsz * seq, d),
        jnp.zeros((bsz, POOL_BUF, POOL_DIM), F32), jnp.zeros((bsz, CONV_WIDTH - 1, CONV_DIM), F32),
        jnp.zeros((bsz, N_HEADS, HEAD_DIM, HEAD_DIM), F32), prm,
        n_seq=bsz, sb=2, tt=tt_p, tv=tt_p, chunk=64, pos0=0, n_tiles=seq // tt_p, tm=256)
    y_prompt = _moe(xm, h2, route, wk, counts, wg, wu, wd, fnw, tm_e=256, tm_c=256).reshape(bsz, seq, d)
    pool_p = npool[None]
    conv_p = nconv[None]
    delta_p = ns[None]

    tt_s = SUBLANES
    xs_pad = jnp.pad(x_sample, ((0, 0), (0, tt_s - dseq), (0, 0))).reshape(dbs * tt_s, d)
    xm, h2, route, wk, counts, npool, nconv, ns = _layer(
        xs_pad, cache_pool[lyr], cache_conv[lyr], state_delta[lyr], prm,
        n_seq=dbs, sb=4, tt=tt_s, tv=dseq, chunk=tt_s, pos0=past_len, n_tiles=1, tm=256)
    real = lambda a: a.reshape(dbs, tt_s, -1)[:, :dseq].reshape(dbs * dseq, -1)
    y_sample = _moe(real(xm), real(h2), real(route), real(wk), counts, wg, wu, wd, fnw,
                    tm_e=128, tm_c=256).reshape(dbs, dseq, d)
    pool_s = npool[None]
    conv_s = nconv[None]
    delta_s = ns[None]
    return (y_prompt, y_sample, pool_p, conv_p, delta_p, pool_s, conv_s, delta_s)
```

```python
import functools

import jax
import jax.numpy as jnp
from jax import lax
from jax.experimental import pallas as pl
from jax.experimental.pallas import tpu as pltpu
from jax.experimental.pallas import tpu_sc as plsc

F32 = jnp.float32
BF16 = jnp.bfloat16
U32 = jnp.uint32
EPS = 1e-6
HIGHEST = lax.Precision.HIGHEST

POOL_WINDOWS = (2, 4, 8, 16)
POOL_GROUP_DIM = 128
POOL_DIM = len(POOL_WINDOWS) * POOL_GROUP_DIM
POOL_BUF = max(POOL_WINDOWS) - 1
N_HEADS = 8
HEAD_DIM = 128
QK_DIM = N_HEADS * HEAD_DIM
CONV_WIDTH = 4
CONV_DIM = 3 * QK_DIM
N_GROUPS = 4
EXPERTS_PER_GROUP = 8
N_EXPERTS = N_GROUPS * EXPERTS_PER_GROUP
LANES = 128
SUBLANES = 8

OFF_Z = 0
OFF_QKV = OFF_Z + QK_DIM
OFF_POOL = OFF_QKV + CONV_DIM
OFF_B = OFF_POOL + POOL_DIM
OFF_A = OFF_B + LANES
MIX_COLS = OFF_A + LANES
GATE_COLS = 2048
IN_COLS = MIX_COLS + GATE_COLS

POOL_ROWS = 16
CONV_ROWS = 8
VMEM_LIMIT = 56 * 1024 * 1024
SC_ROWS = 96


def _sigmoid(x):
    return 1.0 / (1.0 + jnp.exp(-x))


def _softplus(x):
    return jnp.maximum(x, 0.0) + jnp.log1p(jnp.exp(-jnp.abs(x)))


def _rms(x, w):
    return x * lax.rsqrt(jnp.mean(x * x, axis=-1, keepdims=True) + EPS) * w


def _dot(a, b):
    return jnp.dot(a.astype(BF16), b.astype(BF16), preferred_element_type=F32)


def _dot_nt(a, b):
    return lax.dot_general(a.astype(BF16), b.astype(BF16), (((1,), (1,)), ((), ())), preferred_element_type=F32)


def _dot_tn(a, b):
    return lax.dot_general(a.astype(BF16), b.astype(BF16), (((0,), (0,)), ((), ())), preferred_element_type=F32)


def _pack_pairs(x):
    c = x.shape[1] // 2
    hi = pltpu.bitcast(x[:, :c].astype(BF16).astype(F32), U32)
    lo = pltpu.bitcast(x[:, c:].astype(BF16).astype(F32), U32)
    return hi | lax.shift_right_logical(lo, jnp.uint32(16))


def _unpack_pairs(w):
    left = pltpu.bitcast(w & jnp.uint32(0xFFFF0000), F32)
    right = pltpu.bitcast(lax.shift_left(w, jnp.uint32(16)), F32)
    return left, right


def _inproj_kernel(x_ref, nw_ref, w_ref, mix_ref, gate_ref):
    h = _rms(x_ref[...], nw_ref[...]).astype(BF16)
    half = MIX_COLS // 2
    for c0 in (0, half):
        mix_ref[:, c0:c0 + half] = jnp.dot(h, w_ref[:, c0:c0 + half], preferred_element_type=F32)
    half = GATE_COLS // 2
    for c0 in (0, half):
        gate_ref[:, c0:c0 + half] = jnp.dot(h, w_ref[:, MIX_COLS + c0:MIX_COLS + c0 + half],
                                            preferred_element_type=F32)


def _inproj(x, norm_w, w_in_r, tm):
    n, d = x.shape
    return pl.pallas_call(
        _inproj_kernel,
        grid=(n // tm,),
        in_specs=[
            pl.BlockSpec((tm, d), lambda i: (i, 0)),
            pl.BlockSpec((1, d), lambda i: (0, 0)),
            pl.BlockSpec((d, IN_COLS), lambda i: (0, 0)),
        ],
        out_specs=[pl.BlockSpec((tm, MIX_COLS), lambda i: (i, 0)), pl.BlockSpec((tm, GATE_COLS), lambda i: (i, 0))],
        out_shape=[jax.ShapeDtypeStruct((n, MIX_COLS), F32), jax.ShapeDtypeStruct((n, GATE_COLS), F32)],
        compiler_params=pltpu.CompilerParams(dimension_semantics=("arbitrary",), vmem_limit_bytes=VMEM_LIMIT),
        name="inproj",
    )(x, norm_w, w_in_r)


def _mixer_kernel(proj_ref, pool0_ref, conv0_ref, s0_ref, wpool_ref, pscale_ref, wconv_ref, alog_ref, dtb_ref, onw_ref,
                  ypool_ref, ydelta_ref, npool_ref, nconv_ref, ns_ref,
                  ext_p, ext_c, q_s, k_s, v_s, g_s, b_s, state,
                  *, sb, tt, tv, chunk, pos0, n_tiles):
    t = pl.program_id(1)
    base = CONV_ROWS - (CONV_WIDTH - 1)

    @pl.when(t == 0)
    def _():
        ext_p[:, POOL_ROWS - POOL_BUF:POOL_ROWS, :] = pool0_ref[...]
        ext_c[:, base:CONV_ROWS, :] = conv0_ref[...]
        state[...] = s0_ref[...]

    row = lax.broadcasted_iota(jnp.int32, (tt, 1), 0)
    pos = pos0 + t * tt + row
    for s in range(sb):
        ext_p[s, POOL_ROWS:POOL_ROWS + tt, :] = proj_ref[s, :, OFF_POOL:OFF_POOL + POOL_DIM]
        ext_c[s, CONV_ROWS:CONV_ROWS + tt, :] = proj_ref[s, :, OFF_QKV:OFF_QKV + CONV_DIM]

        for gi, win in enumerate(POOL_WINDOWS):
            cs = slice(gi * POOL_GROUP_DIM, (gi + 1) * POOL_GROUP_DIM)
            u = ext_p[s, POOL_ROWS:POOL_ROWS + tt, cs]
            acc = u
            for j in range(1, win):
                acc = acc + ext_p[s, POOL_ROWS - j:POOL_ROWS - j + tt, cs]
            cnt = jnp.minimum(win, pos + 1).astype(F32)
            pooled = acc / cnt - u
            ypool_ref[s, :, cs] = _dot(pooled, wpool_ref[gi]) * pscale_ref[:, cs]

        for ci in range(CONV_DIM // LANES):
            cs = slice(ci * LANES, (ci + 1) * LANES)
            acc = ext_c[s, base:base + tt, cs] * wconv_ref[0:1, cs]
            for j in range(1, CONV_WIDTH):
                acc = acc + ext_c[s, base + j:base + j + tt, cs] * wconv_ref[j:j + 1, cs]
            y = acc * _sigmoid(acc)
            hs = slice((ci % N_HEADS) * HEAD_DIM, (ci % N_HEADS + 1) * HEAD_DIM)
            if ci < 2 * N_HEADS:
                y = y * lax.rsqrt(jnp.sum(y * y, axis=-1, keepdims=True) + EPS)
            if ci < N_HEADS:
                q_s[s, :, hs] = y * (HEAD_DIM ** -0.5)
            elif ci < 2 * N_HEADS:
                k_s[s, :, hs] = y
            else:
                v_s[s, :, hs] = y

        beta = _sigmoid(proj_ref[s, :, OFF_B:OFF_B + LANES])
        g = -jnp.exp(alog_ref[...]) * _softplus(proj_ref[s, :, OFF_A:OFF_A + LANES] + dtb_ref[...])
        if tv < tt:
            beta = jnp.where(row < tv, beta, 0.0)
            g = jnp.where(row < tv, g, 0.0)
        b_s[s] = beta
        g_s[s] = g

    @pl.when(t == n_tiles - 1)
    def _():
        npool_ref[...] = ext_p[:, tv + POOL_ROWS - POOL_BUF:tv + POOL_ROWS, :]
        nconv_ref[...] = ext_c[:, tv + base:tv + CONV_ROWS, :]

    if n_tiles > 1:
        ext_p[:, 0:POOL_ROWS, :] = ext_p[:, tt:tt + POOL_ROWS, :]
        ext_c[:, 0:CONV_ROWS, :] = ext_c[:, tt:tt + CONV_ROWS, :]

    ri = lax.broadcasted_iota(jnp.int32, (chunk, chunk), 0)
    cj = lax.broadcasted_iota(jnp.int32, (chunk, chunk), 1)
    causal = ri >= cj
    strict = ri > cj
    ltri = causal.astype(F32)
    eye = (ri == cj).astype(F32)
    n_doublings = chunk.bit_length() - 2
    units = [(s, h) for s in range(sb) for h in range(N_HEADS)]
    hsl = [slice(h * HEAD_DIM, (h + 1) * HEAD_DIM) for h in range(N_HEADS)]

    def chunk_body(ci, carry):
        r0 = pl.multiple_of(ci * chunk, chunk)
        rows = pl.ds(r0, chunk)
        g_all, g_t, b_all = [], [], []
        for s in range(sb):
            ga = jnp.dot(ltri, g_s[s, rows, :], precision=HIGHEST, preferred_element_type=F32)
            g_all.append(ga)
            if chunk < LANES:
                ga = jnp.concatenate([ga, jnp.zeros((LANES - chunk, LANES), F32)], axis=0)
            g_t.append(ga.T)
            b_all.append(b_s[s, rows, :])
        g_col = [g_all[s][:, h:h + 1] for s, h in units]
        dec = [jnp.exp(jnp.minimum(g_col[i] - g_t[s][h:h + 1, 0:chunk], 0.0)) for i, (s, h) in enumerate(units)]
        b_col = [b_all[s][:, h:h + 1] for s, h in units]
        q = [q_s[s, rows, hsl[h]] for s, h in units]
        k = [k_s[s, rows, hsl[h]] for s, h in units]
        idx = range(len(units))
        kb = [k[i] * b_col[i] for i in idx]
        kbk = [_dot_nt(kb[i], k[i]) for i in idx]
        qk = [_dot_nt(q[i], k[i]) for i in idx]
        p_mat = [jnp.where(strict, kbk[i] * dec[i], 0.0) for i in idx]
        t_inv = [eye - p_mat[i] for i in idx]
        for _ in range(n_doublings):
            p_mat = [_dot(p_mat[i], p_mat[i]) for i in idx]
            xp = [_dot(t_inv[i], p_mat[i]) for i in idx]
            t_inv = [t_inv[i] + xp[i] for i in idx]
        g_exp = [jnp.exp(g_col[i]) for i in idx]
        wu = [_dot(t_inv[i], jnp.concatenate([kb[i] * g_exp[i], v_s[s, rows, hsl[h]] * b_col[i]], axis=1))
              for i, (s, h) in enumerate(units)]
        s_old = [state[s, h] for s, h in units]
        res = [_dot(jnp.concatenate([wu[i][:, 0:HEAD_DIM], q[i] * g_exp[i]], axis=0), s_old[i]) for i in idx]
        v_new = [wu[i][:, HEAD_DIM:2 * HEAD_DIM] - res[i][0:chunk] for i in idx]
        a_qk = [jnp.where(causal, qk[i] * dec[i], 0.0) for i in idx]
        g_last = [g_all[s][chunk - 1:chunk, h:h + 1] for s, h in units]
        k_dec = [k[i] * jnp.exp(g_last[i] - g_col[i]) for i in idx]
        intra = [_dot(a_qk[i], v_new[i]) for i in idx]
        upd = [_dot_tn(k_dec[i], v_new[i]) for i in idx]
        for i, (s, h) in enumerate(units):
            state[s, h] = s_old[i] * jnp.exp(g_last[i]) + upd[i]
            o_c = res[i][chunk:2 * chunk] + intra[i]
            z_c = proj_ref[s, rows, OFF_Z + h * HEAD_DIM:OFF_Z + (h + 1) * HEAD_DIM]
            ydelta_ref[s, rows, hsl[h]] = _rms(o_c, onw_ref[...]) * (z_c * _sigmoid(z_c))
        return carry

    lax.fori_loop(0, tt // chunk, chunk_body, 0)

    @pl.when(t == n_tiles - 1)
    def _():
        ns_ref[...] = state[...]


def _mixers(proj, pool0, conv0, s0, wpool, pscale, wconv, alog, dtb, onw, *, n_seq, sb, tt, tv, chunk, pos0, n_tiles):
    kern = functools.partial(_mixer_kernel, sb=sb, tt=tt, tv=tv, chunk=chunk, pos0=pos0, n_tiles=n_tiles)
    seq_len = n_tiles * tt
    tile = lambda b, t: (b, t, 0)
    seq3 = lambda b, t: (b, 0, 0)
    seq4 = lambda b, t: (b, 0, 0, 0)
    const2 = lambda b, t: (0, 0)
    ypool, ydelta, npool, nconv, ns = pl.pallas_call(
        kern,
        grid=(n_seq // sb, n_tiles),
        in_specs=[
            pl.BlockSpec((sb, tt, MIX_COLS), tile),
            pl.BlockSpec((sb, POOL_BUF, POOL_DIM), seq3),
            pl.BlockSpec((sb, CONV_WIDTH - 1, CONV_DIM), seq3),
            pl.BlockSpec((sb, N_HEADS, HEAD_DIM, HEAD_DIM), seq4),
            pl.BlockSpec((len(POOL_WINDOWS), POOL_GROUP_DIM, POOL_GROUP_DIM), lambda b, t: (0, 0, 0)),
            pl.BlockSpec((1, POOL_DIM), const2),
            pl.BlockSpec((SUBLANES, CONV_DIM), const2),
            pl.BlockSpec((1, LANES), const2),
            pl.BlockSpec((1, LANES), const2),
            pl.BlockSpec((1, HEAD_DIM), const2),
        ],
        out_specs=[
            pl.BlockSpec((sb, tt, POOL_DIM), tile),
            pl.BlockSpec((sb, tt, QK_DIM), tile),
            pl.BlockSpec((sb, POOL_BUF, POOL_DIM), seq3),
            pl.BlockSpec((sb, CONV_WIDTH - 1, CONV_DIM), seq3),
            pl.BlockSpec((sb, N_HEADS, HEAD_DIM, HEAD_DIM), seq4),
        ],
        out_shape=[
            jax.ShapeDtypeStruct((n_seq, seq_len, POOL_DIM), F32),
            jax.ShapeDtypeStruct((n_seq, seq_len, QK_DIM), F32),
            jax.ShapeDtypeStruct((n_seq, POOL_BUF, POOL_DIM), F32),
            jax.ShapeDtypeStruct((n_seq, CONV_WIDTH - 1, CONV_DIM), F32),
            jax.ShapeDtypeStruct((n_seq, N_HEADS, HEAD_DIM, HEAD_DIM), F32),
        ],
        scratch_shapes=[
            pltpu.VMEM((sb, POOL_ROWS + tt, POOL_DIM), F32),
            pltpu.VMEM((sb, CONV_ROWS + tt, CONV_DIM), F32),
            pltpu.VMEM((sb, tt, QK_DIM), F32),
            pltpu.VMEM((sb, tt, QK_DIM), F32),
            pltpu.VMEM((sb, tt, QK_DIM), F32),
            pltpu.VMEM((sb, tt, LANES), F32),
            pltpu.VMEM((sb, tt, LANES), F32),
            pltpu.VMEM((sb, N_HEADS, HEAD_DIM, HEAD_DIM), F32),
        ],
        compiler_params=pltpu.CompilerParams(dimension_semantics=("arbitrary", "arbitrary"),
                                             vmem_limit_bytes=VMEM_LIMIT),
        name="mixers",
    )(proj.reshape(n_seq, seq_len, MIX_COLS), pool0, conv0, s0, wpool, pscale, wconv, alog, dtb, onw)
    n = n_seq * seq_len
    return ypool.reshape(n, POOL_DIM), ydelta.reshape(n, QK_DIM), npool, nconv, ns


def _merge_kernel(x_ref, gates_ref, ypool_ref, ydelta_ref, wbp_ref, wbd_ref, wout_ref, n2w_ref, wr_hi_ref, wr_lo_ref,
                  br_ref, xm_ref, h2_ref, route_ref, wk_ref, cnt_ref, cnt_s, *, period, valid):
    tm = x_ref.shape[0]

    @pl.when(pl.program_id(0) == 0)
    def _():
        cnt_s[...] = jnp.zeros_like(cnt_s)

    bp = _dot(ypool_ref[...], wbp_ref[...])
    bd = _dot(ydelta_ref[...], wbd_ref[...])
    merged = _sigmoid(gates_ref[:, 0:1024]) * bp + _sigmoid(gates_ref[:, 1024:2048]) * bd
    xm = x_ref[...] + _dot(merged, wout_ref[...])
    xm_ref[...] = xm
    h2 = _rms(xm, n2w_ref[...])
    h2_ref[...] = _pack_pairs(h2)

    h2_hi = h2.astype(BF16)
    h2_lo = (h2 - h2_hi.astype(F32)).astype(BF16)
    logits = (jnp.dot(h2_hi, wr_hi_ref[...], preferred_element_type=F32)
              + (jnp.dot(h2_hi, wr_lo_ref[...], preferred_element_type=F32)
                 + jnp.dot(h2_lo, wr_hi_ref[...], preferred_element_type=F32))) + br_ref[...]
    lane = lax.broadcasted_iota(jnp.int32, logits.shape, 1)
    neg = -jnp.inf
    far = LANES - 1
    is_g = lane < N_GROUPS
    g_max = jnp.max(jnp.where(is_g, logits, neg), axis=-1, keepdims=True)
    g_sel = jnp.min(jnp.where(is_g & (logits == g_max), lane, far), axis=-1, keepdims=True)
    p_g = 1.0 / jnp.sum(jnp.where(is_g, jnp.exp(logits - g_max), 0.0), axis=-1, keepdims=True)
    e_lane = lane - N_GROUPS
    is_e = (e_lane >= 0) & (e_lane < N_EXPERTS) & ((e_lane // EXPERTS_PER_GROUP) == g_sel)
    ev = jnp.where(is_e, logits, neg)
    v1 = jnp.max(ev, axis=-1, keepdims=True)
    i1 = jnp.min(jnp.where(is_e & (ev == v1), lane, far), axis=-1, keepdims=True)
    is_e2 = is_e & (lane != i1)
    ev2 = jnp.where(is_e2, logits, neg)
    v2 = jnp.max(ev2, axis=-1, keepdims=True)
    i2 = jnp.min(jnp.where(is_e2 & (ev2 == v2), lane, far), axis=-1, keepdims=True)
    e21 = jnp.exp(v2 - v1)
    w1 = p_g / (1.0 + e21)
    w2 = p_g * e21 / (1.0 + e21)
    e1 = i1 - N_GROUPS
    e2 = i2 - N_GROUPS
    chosen = (lane == e1) | (lane == e2)
    if valid < period:
        row = lax.broadcasted_iota(jnp.int32, (tm, 1), 0)
        chosen = chosen & (lax.rem(row, period) < valid)
    onehot = jnp.where(chosen, 1.0, 0.0)
    rr = lax.broadcasted_iota(jnp.int32, (tm, tm), 0)
    cc = lax.broadcasted_iota(jnp.int32, (tm, tm), 1)
    earlier = jnp.where(rr > cc, 1.0, 0.0).astype(BF16)
    before = jnp.dot(earlier, onehot.astype(BF16), preferred_element_type=F32) + cnt_s[...]
    r1 = jnp.sum(jnp.where(lane == e1, before, 0.0), axis=-1, keepdims=True).astype(jnp.int32)
    r2 = jnp.sum(jnp.where(lane == e2, before, 0.0), axis=-1, keepdims=True).astype(jnp.int32)
    cnt_s[...] = cnt_s[...] + jnp.sum(onehot, axis=0, keepdims=True)
    cnt_ref[...] = cnt_s[...].astype(jnp.int32)
    lane8 = lax.broadcasted_iota(jnp.int32, (tm, SUBLANES), 1)
    route_ref[...] = jnp.where(lane8 == 0, e1, jnp.where(lane8 == 1, e2, jnp.where(lane8 == 2, r1,
                                                                                    jnp.where(lane8 == 3, r2, 0))))
    wk_ref[...] = jnp.where(lane8 == 0, w1, jnp.where(lane8 == 1, w2, 0.0))


def _merge(x, proj, ypool, ydelta, wbp, wbd, wout, n2w, wr_hi, wr_lo, br, tm, period, valid):
    n, d = x.shape
    row = lambda i: (i, 0)
    const = lambda i: (0, 0)
    return pl.pallas_call(
        functools.partial(_merge_kernel, period=period, valid=valid),
        grid=(n // tm,),
        in_specs=[
            pl.BlockSpec((tm, d), row),
            pl.BlockSpec((tm, GATE_COLS), row),
            pl.BlockSpec((tm, POOL_DIM), row),
            pl.BlockSpec((tm, QK_DIM), row),
            pl.BlockSpec((POOL_DIM, d), const),
            pl.BlockSpec((QK_DIM, d), const),
            pl.BlockSpec((d, d), const),
            pl.BlockSpec((1, d), const),
            pl.BlockSpec((d, LANES), const),
            pl.BlockSpec((d, LANES), const),
            pl.BlockSpec((1, LANES), const),
        ],
        out_specs=[
            pl.BlockSpec((tm, d), row),
            pl.BlockSpec((tm, d // 2), row),
            pl.BlockSpec((tm, SUBLANES), row),
            pl.BlockSpec((tm, SUBLANES), row),
            pl.BlockSpec((1, LANES), const),
        ],
        out_shape=[
            jax.ShapeDtypeStruct((n, d), F32),
            jax.ShapeDtypeStruct((n, d // 2), U32),
            jax.ShapeDtypeStruct((n, SUBLANES), jnp.int32),
            jax.ShapeDtypeStruct((n, SUBLANES), F32),
            jax.ShapeDtypeStruct((1, LANES), jnp.int32),
        ],
        scratch_shapes=[pltpu.VMEM((1, LANES), F32)],
        compiler_params=pltpu.CompilerParams(dimension_semantics=("arbitrary",), vmem_limit_bytes=VMEM_LIMIT),
        name="merge",
    )(x, proj, ypool, ydelta, wbp, wbd, wout, n2w, wr_hi, wr_lo, br)


def _expert_kernel(te_ref, nv_ref, x_ref, wg_ref, wu_ref, wd_ref, y_ref):
    i = pl.program_id(0)
    n_valid = nv_ref[i]

    @pl.when(n_valid > 0)
    def _():
        row = lax.broadcasted_iota(jnp.int32, (x_ref.shape[0], 1), 0)
        left, right = _unpack_pairs(jnp.where(row < n_valid, x_ref[...], jnp.uint32(0)))
        x = jnp.concatenate([left.astype(BF16), right.astype(BF16)], axis=1)
        a = _dot(x, wg_ref[0])
        b = _dot(x, wu_ref[0])
        y_ref[...] = _pack_pairs(_dot((a * _sigmoid(a)) * b, wd_ref[0]))

    @pl.when(n_valid == 0)
    def _():
        y_ref[...] = jnp.zeros_like(y_ref)


def _experts(tile_expert, tile_valid, xs, wg, wu, wd, tm):
    n_rows, dp = xs.shape
    d, f = wg.shape[1], wg.shape[2]
    grid_spec = pltpu.PrefetchScalarGridSpec(
        num_scalar_prefetch=2,
        grid=(n_rows // tm,),
        in_specs=[
            pl.BlockSpec((tm, dp), lambda i, te, nv: (jnp.where(nv[i] > 0, i, 0), 0)),
            pl.BlockSpec((1, d, f), lambda i, te, nv: (te[i], 0, 0)),
            pl.BlockSpec((1, d, f), lambda i, te, nv: (te[i], 0, 0)),
            pl.BlockSpec((1, f, d), lambda i, te, nv: (te[i], 0, 0)),
        ],
        out_specs=pl.BlockSpec((tm, dp), lambda i, te, nv: (i, 0)),
    )
    return pl.pallas_call(
        _expert_kernel,
        grid_spec=grid_spec,
        out_shape=jax.ShapeDtypeStruct((n_rows, dp), U32),
        compiler_params=pltpu.CompilerParams(dimension_semantics=("arbitrary",), vmem_limit_bytes=VMEM_LIMIT),
        name="experts",
    )(tile_expert, tile_valid, xs, wg, wu, wd)


def _sc_workers():
    info = plsc.get_sparse_core_info()
    return info.num_cores, info.num_subcores


def _sc_chunk(per_worker):
    assert per_worker % SUBLANES == 0
    return max(c for c in range(SUBLANES, SC_ROWS + 1, SUBLANES) if per_worker % c == 0)


def _sc_scatter2(rows, idx_a, idx_b, n_out):
    n_cores, n_sub = _sc_workers()
    n, d = rows.shape
    per_worker = n // (n_cores * n_sub)
    assert per_worker * n_cores * n_sub == n
    chunk = _sc_chunk(per_worker)
    mesh = plsc.VectorSubcoreMesh(core_axis_name="c", subcore_axis_name="s")

    @functools.partial(
        pl.kernel, mesh=mesh, out_type=jax.ShapeDtypeStruct((n_out, d), rows.dtype),
        scratch_types=[pltpu.VMEM((chunk,), jnp.int32), pltpu.VMEM((chunk,), jnp.int32),
                       pltpu.VMEM((chunk, d), rows.dtype), pltpu.SemaphoreType.DMA])
    def scatter_rows(rows_hbm, ia_hbm, ib_hbm, out_hbm, ia_v, ib_v, rows_v, sem):
        worker = lax.axis_index("s") * n_cores + lax.axis_index("c")
        base = worker * per_worker

        @pl.loop(0, per_worker // chunk)
        def _(j):
            off = pl.multiple_of(base + j * chunk, SUBLANES)
            pltpu.sync_copy(ia_hbm.at[pl.ds(off, chunk)], ia_v)
            pltpu.sync_copy(ib_hbm.at[pl.ds(off, chunk)], ib_v)
            pltpu.sync_copy(rows_hbm.at[pl.ds(off, chunk)], rows_v)
            pltpu.async_copy(rows_v, out_hbm.at[ia_v], sem).wait()
            pltpu.async_copy(rows_v, out_hbm.at[ib_v], sem).wait()

    return scatter_rows(rows, idx_a, idx_b)


def _sc_gather(table, idx):
    n_cores, n_sub = _sc_workers()
    n_idx = idx.shape[0]
    d = table.shape[1]
    per_worker = n_idx // (n_cores * n_sub)
    assert per_worker * n_cores * n_sub == n_idx
    chunk = _sc_chunk(per_worker)
    mesh = plsc.VectorSubcoreMesh(core_axis_name="c", subcore_axis_name="s")

    @functools.partial(
        pl.kernel, mesh=mesh, out_type=jax.ShapeDtypeStruct((n_idx, d), table.dtype),
        scratch_types=[pltpu.VMEM((chunk,), jnp.int32), pltpu.VMEM((chunk, d), table.dtype), pltpu.SemaphoreType.DMA])
    def gather_rows(table_hbm, idx_hbm, out_hbm, idx_v, rows_v, sem):
        worker = lax.axis_index("s") * n_cores + lax.axis_index("c")
        base = worker * per_worker

        @pl.loop(0, per_worker // chunk)
        def _(j):
            off = pl.multiple_of(base + j * chunk, SUBLANES)
            pltpu.sync_copy(idx_hbm.at[pl.ds(off, chunk)], idx_v)
            pltpu.async_copy(table_hbm.at[idx_v], rows_v, sem).wait()
            pltpu.sync_copy(rows_v, out_hbm.at[pl.ds(off, chunk)])

    return gather_rows(table, idx)


def _finalize_kernel(xm_ref, ya_ref, yb_ref, wk_ref, fnw_ref, o_ref):
    d = xm_ref.shape[1]
    c = d // 2
    w = wk_ref[...]
    a_left, a_right = _unpack_pairs(ya_ref[...])
    b_left, b_right = _unpack_pairs(yb_ref[...])
    x_left = xm_ref[:, 0:c] + (w[:, 0:1] * a_left + w[:, 1:2] * b_left)
    x_right = xm_ref[:, c:d] + (w[:, 0:1] * a_right + w[:, 1:2] * b_right)
    ms = (jnp.sum(x_left * x_left, axis=-1, keepdims=True) + jnp.sum(x_right * x_right, axis=-1, keepdims=True)) / d
    scale = lax.rsqrt(ms + EPS)
    o_ref[:, 0:c] = x_left * scale * fnw_ref[:, 0:c]
    o_ref[:, c:d] = x_right * scale * fnw_ref[:, c:d]


def _finalize(xm, yk, wk, fnw, tm):
    n, d = xm.shape
    steps = n // tm
    return pl.pallas_call(
        _finalize_kernel,
        grid=(steps,),
        in_specs=[
            pl.BlockSpec((tm, d), lambda i: (i, 0)),
            pl.BlockSpec((tm, d // 2), lambda i: (i, 0)),
            pl.BlockSpec((tm, d // 2), lambda i: (i + steps, 0)),
            pl.BlockSpec((tm, SUBLANES), lambda i: (i, 0)),
            pl.BlockSpec((1, d), lambda i: (0, 0)),
        ],
        out_specs=pl.BlockSpec((tm, d), lambda i: (i, 0)),
        out_shape=jax.ShapeDtypeStruct((n, d), F32),
        compiler_params=pltpu.CompilerParams(dimension_semantics=("arbitrary",), vmem_limit_bytes=VMEM_LIMIT),
        name="finalize",
    )(xm, yk, yk, wk, fnw)


def _moe(xm, h2, route, wk, counts, wg, wu, wd, fnw, *, tm_e, tm_c):
    n = xm.shape[0]
    counts = counts[0, :N_EXPERTS]
    padded = ((counts + tm_e - 1) // tm_e) * tm_e
    ends = jnp.cumsum(padded)
    starts = ends - padded
    pos_a = (starts[route[:, 0]] + route[:, 2]).astype(jnp.int32)
    pos_b = (starts[route[:, 1]] + route[:, 3]).astype(jnp.int32)
    n_rows = 2 * n + N_EXPERTS * tm_e
    tile_start = jnp.arange(n_rows // tm_e, dtype=jnp.int32) * tm_e
    tile_expert = jnp.minimum(
        jnp.sum((ends[None, :] <= tile_start[:, None]).astype(jnp.int32), axis=1), N_EXPERTS - 1)
    tile_valid = jnp.clip(counts[tile_expert] - (tile_start - starts[tile_expert]), 0, tm_e).astype(jnp.int32)
    xs = _sc_scatter2(h2, pos_a, pos_b, n_rows)
    y_sorted = _experts(tile_expert, tile_valid, xs, wg, wu, wd, tm_e)
    yk = _sc_gather(y_sorted, jnp.concatenate([pos_a, pos_b]))
    return _finalize(xm, yk, wk, fnw, tm_c)


def _layer(x_tokens, pool0, conv0, s0, prm, *, n_seq, sb, tt, tv, chunk, pos0, n_tiles, tm):
    proj, gates = _inproj(x_tokens, prm["norm1_w"], prm["w_in"], tm)
    ypool, ydelta, npool, nconv, ns = _mixers(
        proj, pool0, conv0, s0, prm["w_pool"], prm["pool_scale"], prm["w_conv"], prm["a_log"], prm["dt_bias"],
        prm["o_norm_w"], n_seq=n_seq, sb=sb, tt=tt, tv=tv, chunk=chunk, pos0=pos0, n_tiles=n_tiles)
    xm, h2, route, wk, counts = _merge(
        x_tokens, gates, ypool, ydelta, prm["w_branch_pool"], prm["w_branch_delta"], prm["w_out"], prm["norm2_w"],
        prm["w_router_hi"], prm["w_router_lo"], prm["b_router"], tm, tt, tv)
    return xm, h2, route, wk, counts, npool, nconv, ns


def _pad_lanes(v, width=LANES):
    v = v.reshape(1, -1).astype(F32)
    return jnp.pad(v, ((0, 0), (0, width - v.shape[1])))


def kernel(x_prompt, x_sample, cache_pool, cache_conv, state_delta, norm1_w, w_in, w_pool, pool_scale, w_conv, a_log, dt_bias, o_norm_w, w_branch_pool, w_branch_delta, w_out, norm2_w, w_router_group, b_router_group, w_router_expert, b_router_expert, w_gate, w_up, w_down, final_norm_w):
    n_layers = norm1_w.shape[0]
    assert n_layers == 1, "single-layer step"
    bsz, seq, d = x_prompt.shape
    dbs, dseq, _ = x_sample.shape
    past_len = 16384
    lyr = 0

    wi = w_in[lyr]
    c = [POOL_DIM, POOL_DIM + CONV_DIM]
    c += [c[1] + N_HEADS, c[1] + 2 * N_HEADS]
    c += [c[3] + QK_DIM, c[3] + QK_DIM + d]
    pad8 = lambda m: jnp.pad(m, ((0, 0), (0, LANES - N_HEADS)))
    w_in_r = jnp.concatenate(
        [wi[:, c[3]:c[4]], wi[:, c[0]:c[1]], wi[:, :c[0]], pad8(wi[:, c[1]:c[2]]), pad8(wi[:, c[2]:c[3]]),
         wi[:, c[4]:c[5]], wi[:, c[5]:]], axis=1).astype(BF16)
    w_router = jnp.concatenate(
        [w_router_group[lyr], w_router_expert[lyr].reshape(d, N_EXPERTS),
         jnp.zeros((d, LANES - N_GROUPS - N_EXPERTS), F32)], axis=1).astype(F32)
    w_router_hi = w_router.astype(BF16)
    b_router = _pad_lanes(jnp.concatenate([b_router_group[lyr], b_router_expert[lyr].reshape(-1)]))
    prm = dict(
        norm1_w=norm1_w[lyr].reshape(1, d), w_in=w_in_r,
        w_pool=w_pool[lyr].astype(BF16), pool_scale=pool_scale[lyr].reshape(1, POOL_DIM),
        w_conv=jnp.pad(w_conv[lyr], ((0, SUBLANES - CONV_WIDTH), (0, 0))),
        a_log=_pad_lanes(a_log[lyr]), dt_bias=_pad_lanes(dt_bias[lyr]), o_norm_w=o_norm_w[lyr].reshape(1, HEAD_DIM),
        w_branch_pool=w_branch_pool[lyr].astype(BF16), w_branch_delta=w_branch_delta[lyr].astype(BF16),
        w_out=w_out[lyr].astype(BF16), norm2_w=norm2_w[lyr].reshape(1, d),
        w_router_hi=w_router_hi, w_router_lo=(w_router - w_router_hi.astype(F32)).astype(BF16), b_router=b_router,
    )
    wg = w_gate[lyr].astype(BF16)
    wu = w_up[lyr].astype(BF16)
    wd = w_down[lyr].astype(BF16)
    fnw = final_norm_w.reshape(1, d)

    tt_p = 256
    xm, h2, route, wk, counts, npool, nconv, ns = _layer(
        x_prompt.reshape(bsz * seq, d),
        jnp.zeros((bsz, POOL_BUF, POOL_DIM), F32), jnp.zeros((bsz, CONV_WIDTH - 1, CONV_DIM), F32),
        jnp.zeros((bsz, N_HEADS, HEAD_DIM, HEAD_DIM), F32), prm,
        n_seq=bsz, sb=2, tt=tt_p, tv=tt_p, chunk=64, pos0=0, n_tiles=seq // tt_p, tm=256)
    y_prompt = _moe(xm, h2, route, wk, counts, wg, wu, wd, fnw, tm_e=256, tm_c=256).reshape(bsz, seq, d)
    pool_p = npool[None]
    conv_p = nconv[None]
    delta_p = ns[None]

    tt_s = SUBLANES
    xs_pad = jnp.pad(x_sample, ((0, 0), (0, tt_s - dseq), (0, 0))).reshape(dbs * tt_s, d)
    xm, h2, route, wk, counts, npool, nconv, ns = _layer(
        xs_pad, cache_pool[lyr], cache_conv[lyr], state_delta[lyr], prm,
        n_seq=dbs, sb=4, tt=tt_s, tv=dseq, chunk=tt_s, pos0=past_len, n_tiles=1, tm=256)
    real = lambda a: a.reshape(dbs, tt_s, -1)[:, :dseq].reshape(dbs * dseq, -1)
    y_sample = _moe(real(xm), real(h2), real(route), real(wk), counts, wg, wu, wd, fnw,
                    tm_e=128, tm_c=256).reshape(dbs, dseq, d)
    pool_s = npool[None]
    conv_s = nconv[None]
    delta_s = ns[None]
    return (y_prompt, y_sample, pool_p, conv_p, delta_p, pool_s, conv_s, delta_s)
```

```python
import functools

import jax
import jax.numpy as jnp
from jax import lax
from jax.experimental import pallas as pl
from jax.experimental.pallas import tpu as pltpu
from jax.experimental.pallas import tpu_sc as plsc

F32 = jnp.float32
BF16 = jnp.bfloat16
U32 = jnp.uint32
EPS = 1e-6
HIGHEST = lax.Precision.HIGHEST

POOL_WINDOWS = (2, 4, 8, 16)
POOL_GROUP_DIM = 128
POOL_DIM = len(POOL_WINDOWS) * POOL_GROUP_DIM
POOL_BUF = max(POOL_WINDOWS) - 1
N_HEADS = 8
HEAD_DIM = 128
QK_DIM = N_HEADS * HEAD_DIM
CONV_WIDTH = 4
CONV_DIM = 3 * QK_DIM
N_GROUPS = 4
EXPERTS_PER_GROUP = 8
N_EXPERTS = N_GROUPS * EXPERTS_PER_GROUP
LANES = 128
SUBLANES = 8

OFF_Z = 0
OFF_QKV = OFF_Z + QK_DIM
OFF_POOL = OFF_QKV + CONV_DIM
OFF_B = OFF_POOL + POOL_DIM
OFF_A = OFF_B + LANES
MIX_COLS = OFF_A + LANES
GATE_COLS = 2048
IN_COLS = MIX_COLS + GATE_COLS

POOL_ROWS = 16
CONV_ROWS = 8
VMEM_LIMIT = 56 * 1024 * 1024
SC_ROWS = 96


def _sigmoid(x):
    return 1.0 / (1.0 + jnp.exp(-x))


def _softplus(x):
    return jnp.maximum(x, 0.0) + jnp.log1p(jnp.exp(-jnp.abs(x)))


def _rms(x, w):
    return x * lax.rsqrt(jnp.mean(x * x, axis=-1, keepdims=True) + EPS) * w


def _dot(a, b):
    return jnp.dot(a.astype(BF16), b.astype(BF16), preferred_element_type=F32)


def _dot_nt(a, b):
    return lax.dot_general(a.astype(BF16), b.astype(BF16), (((1,), (1,)), ((), ())), preferred_element_type=F32)


def _dot_tn(a, b):
    return lax.dot_general(a.astype(BF16), b.astype(BF16), (((0,), (0,)), ((), ())), preferred_element_type=F32)


def _pack_pairs(x):
    c = x.shape[1] // 2
    hi = pltpu.bitcast(x[:, :c].astype(BF16).astype(F32), U32)
    lo = pltpu.bitcast(x[:, c:].astype(BF16).astype(F32), U32)
    return hi | lax.shift_right_logical(lo, jnp.uint32(16))


def _unpack_pairs(w):
    left = pltpu.bitcast(w & jnp.uint32(0xFFFF0000), F32)
    right = pltpu.bitcast(lax.shift_left(w, jnp.uint32(16)), F32)
    return left, right


def _inproj_kernel(x_ref, nw_ref, w_ref, mix_ref, gate_ref):
    h = _rms(x_ref[...], nw_ref[...]).astype(BF16)
    half = MIX_COLS // 2
    for c0 in (0, half):
        mix_ref[:, c0:c0 + half] = jnp.dot(h, w_ref[:, c0:c0 + half], preferred_element_type=F32)
    half = GATE_COLS // 2
    for c0 in (0, half):
        gate_ref[:, c0:c0 + half] = jnp.dot(h, w_ref[:, MIX_COLS + c0:MIX_COLS + c0 + half],
                                            preferred_element_type=F32)


def _inproj(x, norm_w, w_in_r, tm):
    n, d = x.shape
    return pl.pallas_call(
        _inproj_kernel,
        grid=(n // tm,),
        in_specs=[
            pl.BlockSpec((tm, d), lambda i: (i, 0)),
            pl.BlockSpec((1, d), lambda i: (0, 0)),
            pl.BlockSpec((d, IN_COLS), lambda i: (0, 0)),
        ],
        out_specs=[pl.BlockSpec((tm, MIX_COLS), lambda i: (i, 0)), pl.BlockSpec((tm, GATE_COLS), lambda i: (i, 0))],
        out_shape=[jax.ShapeDtypeStruct((n, MIX_COLS), F32), jax.ShapeDtypeStruct((n, GATE_COLS), F32)],
        compiler_params=pltpu.CompilerParams(dimension_semantics=("arbitrary",), vmem_limit_bytes=VMEM_LIMIT),
        name="inproj",
    )(x, norm_w, w_in_r)


def _mixer_kernel(proj_ref, pool0_ref, conv0_ref, s0_ref, wpool_ref, pscale_ref, wconv_ref, alog_ref, dtb_ref, onw_ref,
                  ypool_ref, ydelta_ref, npool_ref, nconv_ref, ns_ref,
                  ext_p, ext_c, q_s, k_s, v_s, g_s, b_s, state,
                  *, sb, tt, tv, chunk, pos0, n_tiles):
    t = pl.program_id(1)
    base = CONV_ROWS - (CONV_WIDTH - 1)

    @pl.when(t == 0)
    def _():
        ext_p[:, POOL_ROWS - POOL_BUF:POOL_ROWS, :] = pool0_ref[...]
        ext_c[:, base:CONV_ROWS, :] = conv0_ref[...]
        state[...] = s0_ref[...]

    row = lax.broadcasted_iota(jnp.int32, (tt, 1), 0)
    pos = pos0 + t * tt + row
    for s in range(sb):
        ext_p[s, POOL_ROWS:POOL_ROWS + tt, :] = proj_ref[s, :, OFF_POOL:OFF_POOL + POOL_DIM]
        ext_c[s, CONV_ROWS:CONV_ROWS + tt, :] = proj_ref[s, :, OFF_QKV:OFF_QKV + CONV_DIM]

        for gi, win in enumerate(POOL_WINDOWS):
            cs = slice(gi * POOL_GROUP_DIM, (gi + 1) * POOL_GROUP_DIM)
            u = ext_p[s, POOL_ROWS:POOL_ROWS + tt, cs]
            acc = u
            for j in range(1, win):
                acc = acc + ext_p[s, POOL_ROWS - j:POOL_ROWS - j + tt, cs]
            cnt = jnp.minimum(win, pos + 1).astype(F32)
            pooled = acc / cnt - u
            ypool_ref[s, :, cs] = _dot(pooled, wpool_ref[gi]) * pscale_ref[:, cs]

        for ci in range(CONV_DIM // LANES):
            cs = slice(ci * LANES, (ci + 1) * LANES)
            acc = ext_c[s, base:base + tt, cs] * wconv_ref[0:1, cs]
            for j in range(1, CONV_WIDTH):
                acc = acc + ext_c[s, base + j:base + j + tt, cs] * wconv_ref[j:j + 1, cs]
            y = acc * _sigmoid(acc)
            hs = slice((ci % N_HEADS) * HEAD_DIM, (ci % N_HEADS + 1) * HEAD_DIM)
            if ci < 2 * N_HEADS:
                y = y * lax.rsqrt(jnp.sum(y * y, axis=-1, keepdims=True) + EPS)
            if ci < N_HEADS:
                q_s[s, :, hs] = y * (HEAD_DIM ** -0.5)
            elif ci < 2 * N_HEADS:
                k_s[s, :, hs] = y
            else:
                v_s[s, :, hs] = y

        beta = _sigmoid(proj_ref[s, :, OFF_B:OFF_B + LANES])
        g = -jnp.exp(alog_ref[...]) * _softplus(proj_ref[s, :, OFF_A:OFF_A + LANES] + dtb_ref[...])
        if tv < tt:
            beta = jnp.where(row < tv, beta, 0.0)
            g = jnp.where(row < tv, g, 0.0)
        b_s[s] = beta
        g_s[s] = g

    @pl.when(t == n_tiles - 1)
    def _():
        npool_ref[...] = ext_p[:, tv + POOL_ROWS - POOL_BUF:tv + POOL_ROWS, :]
        nconv_ref[...] = ext_c[:, tv + base:tv + CONV_ROWS, :]

    if n_tiles > 1:
        ext_p[:, 0:POOL_ROWS, :] = ext_p[:, tt:tt + POOL_ROWS, :]
        ext_c[:, 0:CONV_ROWS, :] = ext_c[:, tt:tt + CONV_ROWS, :]

    ri = lax.broadcasted_iota(jnp.int32, (chunk, chunk), 0)
    cj = lax.broadcasted_iota(jnp.int32, (chunk, chunk), 1)
    causal = ri >= cj
    strict = ri > cj
    ltri = causal.astype(F32)
    eye = (ri == cj).astype(F32)
    n_doublings = chunk.bit_length() - 2
    units = [(s, h) for s in range(sb) for h in range(N_HEADS)]
    hsl = [slice(h * HEAD_DIM, (h + 1) * HEAD_DIM) for h in range(N_HEADS)]

    def chunk_body(ci, carry):
        r0 = pl.multiple_of(ci * chunk, chunk)
        rows = pl.ds(r0, chunk)
        g_all, g_t, b_all = [], [], []
        for s in range(sb):
            ga = jnp.dot(ltri, g_s[s, rows, :], precision=HIGHEST, preferred_element_type=F32)
            g_all.append(ga)
            if chunk < LANES:
                ga = jnp.concatenate([ga, jnp.zeros((LANES - chunk, LANES), F32)], axis=0)
            g_t.append(ga.T)
            b_all.append(b_s[s, rows, :])
        g_col = [g_all[s][:, h:h + 1] for s, h in units]
        dec = [jnp.exp(jnp.minimum(g_col[i] - g_t[s][h:h + 1, 0:chunk], 0.0)) for i, (s, h) in enumerate(units)]
        b_col = [b_all[s][:, h:h + 1] for s, h in units]
        q = [q_s[s, rows, hsl[h]] for s, h in units]
        k = [k_s[s, rows, hsl[h]] for s, h in units]
        idx = range(len(units))
        kb = [k[i] * b_col[i] for i in idx]
        kbk = [_dot_nt(kb[i], k[i]) for i in idx]
        qk = [_dot_nt(q[i], k[i]) for i in idx]
        p_mat = [jnp.where(strict, kbk[i] * dec[i], 0.0) for i in idx]
        t_inv = [eye - p_mat[i] for i in idx]
        for _ in range(n_doublings):
            p_mat = [_dot(p_mat[i], p_mat[i]) for i in idx]
            xp = [_dot(t_inv[i], p_mat[i]) for i in idx]
            t_inv = [t_inv[i] + xp[i] for i in idx]
        g_exp = [jnp.exp(g_col[i]) for i in idx]
        wu = [_dot(t_inv[i], jnp.concatenate([kb[i] * g_exp[i], v_s[s, rows, hsl[h]] * b_col[i]], axis=1))
              for i, (s, h) in enumerate(units)]
        s_old = [state[s, h] for s, h in units]
        res = [_dot(jnp.concatenate([wu[i][:, 0:HEAD_DIM], q[i] * g_exp[i]], axis=0), s_old[i]) for i in idx]
        v_new = [wu[i][:, HEAD_DIM:2 * HEAD_DIM] - res[i][0:chunk] for i in idx]
        a_qk = [jnp.where(causal, qk[i] * dec[i], 0.0) for i in idx]
        g_last = [g_all[s][chunk - 1:chunk, h:h + 1] for s, h in units]
        k_dec = [k[i] * jnp.exp(g_last[i] - g_col[i]) for i in idx]
        intra = [_dot(a_qk[i], v_new[i]) for i in idx]
        upd = [_dot_tn(k_dec[i], v_new[i]) for i in idx]
        for i, (s, h) in enumerate(units):
            state[s, h] = s_old[i] * jnp.exp(g_last[i]) + upd[i]
            o_c = res[i][chunk:2 * chunk] + intra[i]
            z_c = proj_ref[s, rows, OFF_Z + h * HEAD_DIM:OFF_Z + (h + 1) * HEAD_DIM]
            ydelta_ref[s, rows, hsl[h]] = _rms(o_c, onw_ref[...]) * (z_c * _sigmoid(z_c))
        return carry

    lax.fori_loop(0, tt // chunk, chunk_body, 0)

    @pl.when(t == n_tiles - 1)
    def _():
        ns_ref[...] = state[...]


def _mixers(proj, pool0, conv0, s0, wpool, pscale, wconv, alog, dtb, onw, *, n_seq, sb, tt, tv, chunk, pos0, n_tiles):
    kern = functools.partial(_mixer_kernel, sb=sb, tt=tt, tv=tv, chunk=chunk, pos0=pos0, n_tiles=n_tiles)
    seq_len = n_tiles * tt
    tile = lambda b, t: (b, t, 0)
    seq3 = lambda b, t: (b, 0, 0)
    seq4 = lambda b, t: (b, 0, 0, 0)
    const2 = lambda b, t: (0, 0)
    ypool, ydelta, npool, nconv, ns = pl.pallas_call(
        kern,
        grid=(n_seq // sb, n_tiles),
        in_specs=[
            pl.BlockSpec((sb, tt, MIX_COLS), tile),
            pl.BlockSpec((sb, POOL_BUF, POOL_DIM), seq3),
            pl.BlockSpec((sb, CONV_WIDTH - 1, CONV_DIM), seq3),
            pl.BlockSpec((sb, N_HEADS, HEAD_DIM, HEAD_DIM), seq4),
            pl.BlockSpec((len(POOL_WINDOWS), POOL_GROUP_DIM, POOL_GROUP_DIM), lambda b, t: (0, 0, 0)),
            pl.BlockSpec((1, POOL_DIM), const2),
            pl.BlockSpec((SUBLANES, CONV_DIM), const2),
            pl.BlockSpec((1, LANES), const2),
            pl.BlockSpec((1, LANES), const2),
            pl.BlockSpec((1, HEAD_DIM), const2),
        ],
        out_specs=[
            pl.BlockSpec((sb, tt, POOL_DIM), tile),
            pl.BlockSpec((sb, tt, QK_DIM), tile),
            pl.BlockSpec((sb, POOL_BUF, POOL_DIM), seq3),
            pl.BlockSpec((sb, CONV_WIDTH - 1, CONV_DIM), seq3),
            pl.BlockSpec((sb, N_HEADS, HEAD_DIM, HEAD_DIM), seq4),
        ],
        out_shape=[
            jax.ShapeDtypeStruct((n_seq, seq_len, POOL_DIM), F32),
            jax.ShapeDtypeStruct((n_seq, seq_len, QK_DIM), F32),
            jax.ShapeDtypeStruct((n_seq, POOL_BUF, POOL_DIM), F32),
            jax.ShapeDtypeStruct((n_seq, CONV_WIDTH - 1, CONV_DIM), F32),
            jax.ShapeDtypeStruct((n_seq, N_HEADS, HEAD_DIM, HEAD_DIM), F32),
        ],
        scratch_shapes=[
            pltpu.VMEM((sb, POOL_ROWS + tt, POOL_DIM), F32),
            pltpu.VMEM((sb, CONV_ROWS + tt, CONV_DIM), F32),
            pltpu.VMEM((sb, tt, QK_DIM), F32),
            pltpu.VMEM((sb, tt, QK_DIM), F32),
            pltpu.VMEM((sb, tt, QK_DIM), F32),
            pltpu.VMEM((sb, tt, LANES), F32),
            pltpu.VMEM((sb, tt, LANES), F32),
            pltpu.VMEM((sb, N_HEADS, HEAD_DIM, HEAD_DIM), F32),
        ],
        compiler_params=pltpu.CompilerParams(dimension_semantics=("arbitrary", "arbitrary"),
                                             vmem_limit_bytes=VMEM_LIMIT),
        name="mixers",
    )(proj.reshape(n_seq, seq_len, MIX_COLS), pool0, conv0, s0, wpool, pscale, wconv, alog, dtb, onw)
    n = n_seq * seq_len
    return ypool.reshape(n, POOL_DIM), ydelta.reshape(n, QK_DIM), npool, nconv, ns


def _merge_kernel(x_ref, gates_ref, ypool_ref, ydelta_ref, wbp_ref, wbd_ref, wout_ref, n2w_ref, wr_hi_ref, wr_lo_ref,
                  br_ref, xm_ref, h2_ref, route_ref, wk_ref, cnt_ref, cnt_s, *, period, valid):
    tm = x_ref.shape[0]

    @pl.when(pl.program_id(0) == 0)
    def _():
        cnt_s[...] = jnp.zeros_like(cnt_s)

    bp = _dot(ypool_ref[...], wbp_ref[...])
    bd = _dot(ydelta_ref[...], wbd_ref[...])
    merged = _sigmoid(gates_ref[:, 0:1024]) * bp + _sigmoid(gates_ref[:, 1024:2048]) * bd
    xm = x_ref[...] + _dot(merged, wout_ref[...])
    xm_ref[...] = xm
    h2 = _rms(xm, n2w_ref[...])
    h2_ref[...] = _pack_pairs(h2)

    h2_hi = h2.astype(BF16)
    h2_lo = (h2 - h2_hi.astype(F32)).astype(BF16)
    logits = (jnp.dot(h2_hi, wr_hi_ref[...], preferred_element_type=F32)
              + (jnp.dot(h2_hi, wr_lo_ref[...], preferred_element_type=F32)
                 + jnp.dot(h2_lo, wr_hi_ref[...], preferred_element_type=F32))) + br_ref[...]
    lane = lax.broadcasted_iota(jnp.int32, logits.shape, 1)
    neg = -jnp.inf
    far = LANES - 1
    is_g = lane < N_GROUPS
    g_max = jnp.max(jnp.where(is_g, logits, neg), axis=-1, keepdims=True)
    g_sel = jnp.min(jnp.where(is_g & (logits == g_max), lane, far), axis=-1, keepdims=True)
    p_g = 1.0 / jnp.sum(jnp.where(is_g, jnp.exp(logits - g_max), 0.0), axis=-1, keepdims=True)
    e_lane = lane - N_GROUPS
    is_e = (e_lane >= 0) & (e_lane < N_EXPERTS) & ((e_lane // EXPERTS_PER_GROUP) == g_sel)
    ev = jnp.where(is_e, logits, neg)
    v1 = jnp.max(ev, axis=-1, keepdims=True)
    i1 = jnp.min(jnp.where(is_e & (ev == v1), lane, far), axis=-1, keepdims=True)
    is_e2 = is_e & (lane != i1)
    ev2 = jnp.where(is_e2, logits, neg)
    v2 = jnp.max(ev2, axis=-1, keepdims=True)
    i2 = jnp.min(jnp.where(is_e2 & (ev2 == v2), lane, far), axis=-1, keepdims=True)
    e21 = jnp.exp(v2 - v1)
    w1 = p_g / (1.0 + e21)
    w2 = p_g * e21 / (1.0 + e21)
    e1 = i1 - N_GROUPS
    e2 = i2 - N_GROUPS
    chosen = (lane == e1) | (lane == e2)
    if valid < period:
        row = lax.broadcasted_iota(jnp.int32, (tm, 1), 0)
        chosen = chosen & (lax.rem(row, period) < valid)
    onehot = jnp.where(chosen, 1.0, 0.0)
    rr = lax.broadcasted_iota(jnp.int32, (tm, tm), 0)
    cc = lax.broadcasted_iota(jnp.int32, (tm, tm), 1)
    earlier = jnp.where(rr > cc, 1.0, 0.0).astype(BF16)
    before = jnp.dot(earlier, onehot.astype(BF16), preferred_element_type=F32) + cnt_s[...]
    r1 = jnp.sum(jnp.where(lane == e1, before, 0.0), axis=-1, keepdims=True)
    r2 = jnp.sum(jnp.where(lane == e2, before, 0.0), axis=-1, keepdims=True)
    cnt_s[...] = cnt_s[...] + jnp.sum(onehot, axis=0, keepdims=True)
    cnt_ref[...] = cnt_s[...]
    record = jnp.where(lane == 0, e1.astype(F32), jnp.where(lane == 1, e2.astype(F32),
                                                           jnp.where(lane == 2, r1, jnp.where(lane == 3, r2, 0.0))))
    route_ref[...] = record.T[0:SUBLANES, :]
    lane8 = lax.broadcasted_iota(jnp.int32, (tm, SUBLANES), 1)
    wk_ref[...] = jnp.where(lane8 == 0, w1, jnp.where(lane8 == 1, w2, 0.0))


def _merge(x, proj, ypool, ydelta, wbp, wbd, wout, n2w, wr_hi, wr_lo, br, tm, period, valid):
    n, d = x.shape
    row = lambda i: (i, 0)
    const = lambda i: (0, 0)
    return pl.pallas_call(
        functools.partial(_merge_kernel, period=period, valid=valid),
        grid=(n // tm,),
        in_specs=[
            pl.BlockSpec((tm, d), row),
            pl.BlockSpec((tm, GATE_COLS), row),
            pl.BlockSpec((tm, POOL_DIM), row),
            pl.BlockSpec((tm, QK_DIM), row),
            pl.BlockSpec((POOL_DIM, d), const),
            pl.BlockSpec((QK_DIM, d), const),
            pl.BlockSpec((d, d), const),
            pl.BlockSpec((1, d), const),
            pl.BlockSpec((d, LANES), const),
            pl.BlockSpec((d, LANES), const),
            pl.BlockSpec((1, LANES), const),
        ],
        out_specs=[
            pl.BlockSpec((tm, d), row),
            pl.BlockSpec((tm, d // 2), row),
            pl.BlockSpec((SUBLANES, tm), lambda i: (0, i)),
            pl.BlockSpec((tm, SUBLANES), row),
            pl.BlockSpec((1, LANES), const),
        ],
        out_shape=[
            jax.ShapeDtypeStruct((n, d), F32),
            jax.ShapeDtypeStruct((n, d // 2), U32),
            jax.ShapeDtypeStruct((SUBLANES, n), F32),
            jax.ShapeDtypeStruct((n, SUBLANES), F32),
            jax.ShapeDtypeStruct((1, LANES), F32),
        ],
        scratch_shapes=[pltpu.VMEM((1, LANES), F32)],
        compiler_params=pltpu.CompilerParams(dimension_semantics=("arbitrary",), vmem_limit_bytes=VMEM_LIMIT),
        name="merge",
    )(x, proj, ypool, ydelta, wbp, wbd, wout, n2w, wr_hi, wr_lo, br)


def _expert_kernel(te_ref, nv_ref, x_ref, wg_ref, wu_ref, wd_ref, y_ref):
    i = pl.program_id(0)
    n_valid = nv_ref[i]

    @pl.when(n_valid > 0)
    def _():
        row = lax.broadcasted_iota(jnp.int32, (x_ref.shape[0], 1), 0)
        left, right = _unpack_pairs(jnp.where(row < n_valid, x_ref[...], jnp.uint32(0)))
        x = jnp.concatenate([left.astype(BF16), right.astype(BF16)], axis=1)
        a = _dot(x, wg_ref[0])
        b = _dot(x, wu_ref[0])
        y_ref[...] = _pack_pairs(_dot((a * _sigmoid(a)) * b, wd_ref[0]))

    @pl.when(n_valid == 0)
    def _():
        y_ref[...] = jnp.zeros_like(y_ref)


def _experts(tile_expert, tile_valid, xs, wg, wu, wd, tm):
    n_rows, dp = xs.shape
    d, f = wg.shape[1], wg.shape[2]
    grid_spec = pltpu.PrefetchScalarGridSpec(
        num_scalar_prefetch=2,
        grid=(n_rows // tm,),
        in_specs=[
            pl.BlockSpec((tm, dp), lambda i, te, nv: (jnp.where(nv[i] > 0, i, 0), 0)),
            pl.BlockSpec((1, d, f), lambda i, te, nv: (te[i], 0, 0)),
            pl.BlockSpec((1, d, f), lambda i, te, nv: (te[i], 0, 0)),
            pl.BlockSpec((1, f, d), lambda i, te, nv: (te[i], 0, 0)),
        ],
        out_specs=pl.BlockSpec((tm, dp), lambda i, te, nv: (i, 0)),
    )
    return pl.pallas_call(
        _expert_kernel,
        grid_spec=grid_spec,
        out_shape=jax.ShapeDtypeStruct((n_rows, dp), U32),
        compiler_params=pltpu.CompilerParams(dimension_semantics=("arbitrary",), vmem_limit_bytes=VMEM_LIMIT),
        name="experts",
    )(tile_expert, tile_valid, xs, wg, wu, wd)


def _plan_kernel(route_ref, cnt_ref, pos_ref, tiles_ref, *, tm_e):
    counts = cnt_ref[...]
    padded = jnp.floor((counts + (tm_e - 1)) / tm_e) * tm_e
    lane = lax.broadcasted_iota(jnp.int32, counts.shape, 1)
    ends = padded
    shift = 1
    while shift < N_EXPERTS:
        ends = ends + jnp.where(lane >= shift, pltpu.roll(ends, shift, axis=1), 0.0)
        shift *= 2
    starts = ends - padded

    def lookup(table, key):
        out = jnp.zeros_like(key)
        for e in range(N_EXPERTS):
            out = out + jnp.where(key == e, table[:, e:e + 1], 0.0)
        return out

    for k in range(2):
        pos = lookup(starts, route_ref[k:k + 1, :]) + route_ref[2 + k:3 + k, :]
        pos_ref[k:k + 1, :] = pos.astype(jnp.int32)

    tile_start = lax.broadcasted_iota(jnp.int32, (1, tiles_ref.shape[1]), 1).astype(F32) * tm_e
    tile_expert = jnp.zeros_like(tile_start)
    for e in range(N_EXPERTS):
        tile_expert = tile_expert + jnp.where(ends[:, e:e + 1] <= tile_start, 1.0, 0.0)
    tile_expert = jnp.minimum(tile_expert, N_EXPERTS - 1.0)
    valid = lookup(counts, tile_expert) - (tile_start - lookup(starts, tile_expert))
    tiles_ref[0:1, :] = tile_expert.astype(jnp.int32)
    tiles_ref[1:2, :] = jnp.clip(valid, 0.0, tm_e).astype(jnp.int32)


def _plan(route_t, counts, tm_e, n_tiles):
    n = route_t.shape[1]
    tiles_pad = -(-n_tiles // LANES) * LANES
    return pl.pallas_call(
        functools.partial(_plan_kernel, tm_e=tm_e),
        out_shape=[jax.ShapeDtypeStruct((2, n), jnp.int32), jax.ShapeDtypeStruct((2, tiles_pad), jnp.int32)],
        name="plan",
    )(route_t, counts)


def _sc_workers():
    info = plsc.get_sparse_core_info()
    return info.num_cores, info.num_subcores


def _sc_chunk(per_worker):
    assert per_worker % SUBLANES == 0
    return max(c for c in range(SUBLANES, SC_ROWS + 1, SUBLANES) if per_worker % c == 0)


def _sc_scatter2(rows, idx_a, idx_b, n_out):
    n_cores, n_sub = _sc_workers()
    n, d = rows.shape
    per_worker = n // (n_cores * n_sub)
    assert per_worker * n_cores * n_sub == n
    chunk = _sc_chunk(per_worker)
    mesh = plsc.VectorSubcoreMesh(core_axis_name="c", subcore_axis_name="s")

    @functools.partial(
        pl.kernel, mesh=mesh, out_type=jax.ShapeDtypeStruct((n_out, d), rows.dtype),
        scratch_types=[pltpu.VMEM((chunk,), jnp.int32), pltpu.VMEM((chunk,), jnp.int32),
                       pltpu.VMEM((chunk, d), rows.dtype), pltpu.SemaphoreType.DMA])
    def scatter_rows(rows_hbm, ia_hbm, ib_hbm, out_hbm, ia_v, ib_v, rows_v, sem):
        worker = lax.axis_index("s") * n_cores + lax.axis_index("c")
        base = worker * per_worker

        @pl.loop(0, per_worker // chunk)
        def _(j):
            off = pl.multiple_of(base + j * chunk, SUBLANES)
            pltpu.sync_copy(ia_hbm.at[pl.ds(off, chunk)], ia_v)
            pltpu.sync_copy(ib_hbm.at[pl.ds(off, chunk)], ib_v)
            pltpu.sync_copy(rows_hbm.at[pl.ds(off, chunk)], rows_v)
            pltpu.async_copy(rows_v, out_hbm.at[ia_v], sem).wait()
            pltpu.async_copy(rows_v, out_hbm.at[ib_v], sem).wait()

    return scatter_rows(rows, idx_a, idx_b)


def _sc_gather(table, idx):
    n_cores, n_sub = _sc_workers()
    n_idx = idx.shape[0]
    d = table.shape[1]
    per_worker = n_idx // (n_cores * n_sub)
    assert per_worker * n_cores * n_sub == n_idx
    chunk = _sc_chunk(per_worker)
    mesh = plsc.VectorSubcoreMesh(core_axis_name="c", subcore_axis_name="s")

    @functools.partial(
        pl.kernel, mesh=mesh, out_type=jax.ShapeDtypeStruct((n_idx, d), table.dtype),
        scratch_types=[pltpu.VMEM((chunk,), jnp.int32), pltpu.VMEM((chunk, d), table.dtype), pltpu.SemaphoreType.DMA])
    def gather_rows(table_hbm, idx_hbm, out_hbm, idx_v, rows_v, sem):
        worker = lax.axis_index("s") * n_cores + lax.axis_index("c")
        base = worker * per_worker

        @pl.loop(0, per_worker // chunk)
        def _(j):
            off = pl.multiple_of(base + j * chunk, SUBLANES)
            pltpu.sync_copy(idx_hbm.at[pl.ds(off, chunk)], idx_v)
            pltpu.async_copy(table_hbm.at[idx_v], rows_v, sem).wait()
            pltpu.sync_copy(rows_v, out_hbm.at[pl.ds(off, chunk)])

    return gather_rows(table, idx)


def _finalize_kernel(xm_ref, ya_ref, yb_ref, wk_ref, fnw_ref, o_ref):
    d = xm_ref.shape[1]
    c = d // 2
    w = wk_ref[...]
    a_left, a_right = _unpack_pairs(ya_ref[...])
    b_left, b_right = _unpack_pairs(yb_ref[...])
    x_left = xm_ref[:, 0:c] + (w[:, 0:1] * a_left + w[:, 1:2] * b_left)
    x_right = xm_ref[:, c:d] + (w[:, 0:1] * a_right + w[:, 1:2] * b_right)
    ms = (jnp.sum(x_left * x_left, axis=-1, keepdims=True) + jnp.sum(x_right * x_right, axis=-1, keepdims=True)) / d
    scale = lax.rsqrt(ms + EPS)
    o_ref[:, 0:c] = x_left * scale * fnw_ref[:, 0:c]
    o_ref[:, c:d] = x_right * scale * fnw_ref[:, c:d]


def _finalize(xm, yk, wk, fnw, tm):
    n, d = xm.shape
    steps = n // tm
    return pl.pallas_call(
        _finalize_kernel,
        grid=(steps,),
        in_specs=[
            pl.BlockSpec((tm, d), lambda i: (i, 0)),
            pl.BlockSpec((tm, d // 2), lambda i: (i, 0)),
            pl.BlockSpec((tm, d // 2), lambda i: (i + steps, 0)),
            pl.BlockSpec((tm, SUBLANES), lambda i: (i, 0)),
            pl.BlockSpec((1, d), lambda i: (0, 0)),
        ],
        out_specs=pl.BlockSpec((tm, d), lambda i: (i, 0)),
        out_shape=jax.ShapeDtypeStruct((n, d), F32),
        compiler_params=pltpu.CompilerParams(dimension_semantics=("arbitrary",), vmem_limit_bytes=VMEM_LIMIT),
        name="finalize",
    )(xm, yk, yk, wk, fnw)


def _moe(xm, h2, route, wk, counts, wg, wu, wd, fnw, *, tm_e, tm_c):
    n = xm.shape[0]
    n_rows = 2 * n + N_EXPERTS * tm_e
    n_tiles = n_rows // tm_e
    pos, tiles = _plan(route, counts, tm_e, n_tiles)
    xs = _sc_scatter2(h2, pos[0], pos[1], n_rows)
    y_sorted = _experts(tiles[0, :n_tiles], tiles[1, :n_tiles], xs, wg, wu, wd, tm_e)
    yk = _sc_gather(y_sorted, pos.reshape(-1))
    return _finalize(xm, yk, wk, fnw, tm_c)


def _layer(x_tokens, pool0, conv0, s0, prm, *, n_seq, sb, tt, tv, chunk, pos0, n_tiles, tm):
    proj, gates = _inproj(x_tokens, prm["norm1_w"], prm["w_in"], tm)
    ypool, ydelta, npool, nconv, ns = _mixers(
        proj, pool0, conv0, s0, prm["w_pool"], prm["pool_scale"], prm["w_conv"], prm["a_log"], prm["dt_bias"],
        prm["o_norm_w"], n_seq=n_seq, sb=sb, tt=tt, tv=tv, chunk=chunk, pos0=pos0, n_tiles=n_tiles)
    xm, h2, route, wk, counts = _merge(
        x_tokens, gates, ypool, ydelta, prm["w_branch_pool"], prm["w_branch_delta"], prm["w_out"], prm["norm2_w"],
        prm["w_router_hi"], prm["w_router_lo"], prm["b_router"], tm, tt, tv)
    return xm, h2, route, wk, counts, npool, nconv, ns


def _pad_lanes(v, width=LANES):
    v = v.reshape(1, -1).astype(F32)
    return jnp.pad(v, ((0, 0), (0, width - v.shape[1])))


def kernel(x_prompt, x_sample, cache_pool, cache_conv, state_delta, norm1_w, w_in, w_pool, pool_scale, w_conv, a_log, dt_bias, o_norm_w, w_branch_pool, w_branch_delta, w_out, norm2_w, w_router_group, b_router_group, w_router_expert, b_router_expert, w_gate, w_up, w_down, final_norm_w):
    n_layers = norm1_w.shape[0]
    assert n_layers == 1, "single-layer step"
    bsz, seq, d = x_prompt.shape
    dbs, dseq, _ = x_sample.shape
    past_len = 16384
    lyr = 0

    wi = w_in[lyr]
    c = [POOL_DIM, POOL_DIM + CONV_DIM]
    c += [c[1] + N_HEADS, c[1] + 2 * N_HEADS]
    c += [c[3] + QK_DIM, c[3] + QK_DIM + d]
    pad8 = lambda m: jnp.pad(m, ((0, 0), (0, LANES - N_HEADS)))
    w_in_r = jnp.concatenate(
        [wi[:, c[3]:c[4]], wi[:, c[0]:c[1]], wi[:, :c[0]], pad8(wi[:, c[1]:c[2]]), pad8(wi[:, c[2]:c[3]]),
         wi[:, c[4]:c[5]], wi[:, c[5]:]], axis=1).astype(BF16)
    w_router = jnp.concatenate(
        [w_router_group[lyr], w_router_expert[lyr].reshape(d, N_EXPERTS),
         jnp.zeros((d, LANES - N_GROUPS - N_EXPERTS), F32)], axis=1).astype(F32)
    w_router_hi = w_router.astype(BF16)
    b_router = _pad_lanes(jnp.concatenate([b_router_group[lyr], b_router_expert[lyr].reshape(-1)]))
    prm = dict(
        norm1_w=norm1_w[lyr].reshape(1, d), w_in=w_in_r,
        w_pool=w_pool[lyr].astype(BF16), pool_scale=pool_scale[lyr].reshape(1, POOL_DIM),
        w_conv=jnp.pad(w_conv[lyr], ((0, SUBLANES - CONV_WIDTH), (0, 0))),
        a_log=_pad_lanes(a_log[lyr]), dt_bias=_pad_lanes(dt_bias[lyr]), o_norm_w=o_norm_w[lyr].reshape(1, HEAD_DIM),
        w_branch_pool=w_branch_pool[lyr].astype(BF16), w_branch_delta=w_branch_delta[lyr].astype(BF16),
        w_out=w_out[lyr].astype(BF16), norm2_w=norm2_w[lyr].reshape(1, d),
        w_router_hi=w_router_hi, w_router_lo=(w_router - w_router_hi.astype(F32)).astype(BF16), b_router=b_router,
    )
    wg = w_gate[lyr].astype(BF16)
    wu = w_up[lyr].astype(BF16)
    wd = w_down[lyr].astype(BF16)
    fnw = final_norm_w.reshape(1, d)

    tt_p = 256
    xm, h2, route, wk, counts, npool, nconv, ns = _layer(
        x_prompt.reshape(bsz * seq, d),
        jnp.zeros((bsz, POOL_BUF, POOL_DIM), F32), jnp.zeros((bsz, CONV_WIDTH - 1, CONV_DIM), F32),
        jnp.zeros((bsz, N_HEADS, HEAD_DIM, HEAD_DIM), F32), prm,
        n_seq=bsz, sb=2, tt=tt_p, tv=tt_p, chunk=64, pos0=0, n_tiles=seq // tt_p, tm=256)
    y_prompt = _moe(xm, h2, route, wk, counts, wg, wu, wd, fnw, tm_e=256, tm_c=256).reshape(bsz, seq, d)
    pool_p = npool[None]
    conv_p = nconv[None]
    delta_p = ns[None]

    tt_s = SUBLANES
    xs_pad = jnp.pad(x_sample, ((0, 0), (0, tt_s - dseq), (0, 0))).reshape(dbs * tt_s, d)
    xm, h2, route, wk, counts, npool, nconv, ns = _layer(
        xs_pad, cache_pool[lyr], cache_conv[lyr], state_delta[lyr], prm,
        n_seq=dbs, sb=4, tt=tt_s, tv=dseq, chunk=tt_s, pos0=past_len, n_tiles=1, tm=256)
    real = lambda a: a.reshape(dbs, tt_s, -1)[:, :dseq].reshape(dbs * dseq, -1)
    route = route.reshape(SUBLANES, dbs, tt_s)[:, :, :dseq].reshape(SUBLANES, dbs * dseq)
    y_sample = _moe(real(xm), real(h2), route, real(wk), counts, wg, wu, wd, fnw,
                    tm_e=128, tm_c=256).reshape(dbs, dseq, d)
    pool_s = npool[None]
    conv_s = nconv[None]
    delta_s = ns[None]
    return (y_prompt, y_sample, pool_p, conv_p, delta_p, pool_s, conv_s, delta_s)
```

```python
import functools

import jax
import jax.numpy as jnp
from jax import lax
from jax.experimental import pallas as pl
from jax.experimental.pallas import tpu as pltpu
from jax.experimental.pallas import tpu_sc as plsc

F32 = jnp.float32
BF16 = jnp.bfloat16
U32 = jnp.uint32
EPS = 1e-6
HIGHEST = lax.Precision.HIGHEST

POOL_WINDOWS = (2, 4, 8, 16)
POOL_GROUP_DIM = 128
POOL_DIM = len(POOL_WINDOWS) * POOL_GROUP_DIM
POOL_BUF = max(POOL_WINDOWS) - 1
N_HEADS = 8
HEAD_DIM = 128
QK_DIM = N_HEADS * HEAD_DIM
CONV_WIDTH = 4
CONV_DIM = 3 * QK_DIM
N_GROUPS = 4
EXPERTS_PER_GROUP = 8
N_EXPERTS = N_GROUPS * EXPERTS_PER_GROUP
LANES = 128
SUBLANES = 8

OFF_Z = 0
OFF_QKV = OFF_Z + QK_DIM
OFF_POOL = OFF_QKV + CONV_DIM
OFF_B = OFF_POOL + POOL_DIM
OFF_A = OFF_B + LANES
MIX_COLS = OFF_A + LANES
GATE_COLS = 2048
IN_COLS = MIX_COLS + GATE_COLS

POOL_ROWS = 16
CONV_ROWS = 8
VMEM_LIMIT = 60 * 1024 * 1024
SC_ROWS = 96


def _sigmoid(x):
    return 1.0 / (1.0 + jnp.exp(-x))


def _softplus(x):
    return jnp.maximum(x, 0.0) + jnp.log1p(jnp.exp(-jnp.abs(x)))


def _rms(x, w):
    return x * lax.rsqrt(jnp.mean(x * x, axis=-1, keepdims=True) + EPS) * w


def _dot(a, b):
    return jnp.dot(a.astype(BF16), b.astype(BF16), preferred_element_type=F32)


def _dot_nt(a, b):
    return lax.dot_general(a.astype(BF16), b.astype(BF16), (((1,), (1,)), ((), ())), preferred_element_type=F32)


def _dot_tn(a, b):
    return lax.dot_general(a.astype(BF16), b.astype(BF16), (((0,), (0,)), ((), ())), preferred_element_type=F32)


def _pack_pairs(x):
    c = x.shape[1] // 2
    hi = pltpu.bitcast(x[:, :c].astype(BF16).astype(F32), U32)
    lo = pltpu.bitcast(x[:, c:].astype(BF16).astype(F32), U32)
    return hi | lax.shift_right_logical(lo, jnp.uint32(16))


def _unpack_pairs(w):
    left = pltpu.bitcast(w & jnp.uint32(0xFFFF0000), F32)
    right = pltpu.bitcast(lax.shift_left(w, jnp.uint32(16)), F32)
    return left, right


def _inproj_kernel(x_ref, nw_ref, w_ref, mix_ref, gate_ref):
    h = _rms(x_ref[...], nw_ref[...]).astype(BF16)
    half = MIX_COLS // 2
    for c0 in (0, half):
        mix_ref[:, c0:c0 + half] = jnp.dot(h, w_ref[:, c0:c0 + half], preferred_element_type=F32)
    half = GATE_COLS // 2
    for c0 in (0, half):
        gate_ref[:, c0:c0 + half] = jnp.dot(h, w_ref[:, MIX_COLS + c0:MIX_COLS + c0 + half],
                                            preferred_element_type=F32)


def _inproj(x, norm_w, w_in_r, tm):
    n, d = x.shape
    return pl.pallas_call(
        _inproj_kernel,
        grid=(n // tm,),
        in_specs=[
            pl.BlockSpec((tm, d), lambda i: (i, 0)),
            pl.BlockSpec((1, d), lambda i: (0, 0)),
            pl.BlockSpec((d, IN_COLS), lambda i: (0, 0)),
        ],
        out_specs=[pl.BlockSpec((tm, MIX_COLS), lambda i: (i, 0)), pl.BlockSpec((tm, GATE_COLS), lambda i: (i, 0))],
        out_shape=[jax.ShapeDtypeStruct((n, MIX_COLS), F32), jax.ShapeDtypeStruct((n, GATE_COLS), F32)],
        compiler_params=pltpu.CompilerParams(dimension_semantics=("arbitrary",), vmem_limit_bytes=VMEM_LIMIT),
        name="inproj",
    )(x, norm_w, w_in_r)


def _mixer_kernel(proj_ref, pool0_ref, conv0_ref, s0_ref, wpool_ref, pscale_ref, wconv_ref, alog_ref, dtb_ref, onw_ref,
                  ypool_ref, ydelta_ref, npool_ref, nconv_ref, ns_ref,
                  ext_p, ext_c, q_s, k_s, v_s, g_s, b_s, state,
                  *, sb, tt, tv, chunk, pos0, n_tiles):
    t = pl.program_id(1)
    base = CONV_ROWS - (CONV_WIDTH - 1)

    @pl.when(t == 0)
    def _():
        ext_p[:, POOL_ROWS - POOL_BUF:POOL_ROWS, :] = pool0_ref[...]
        ext_c[:, base:CONV_ROWS, :] = conv0_ref[...]
        state[...] = s0_ref[...]

    row = lax.broadcasted_iota(jnp.int32, (tt, 1), 0)
    pos = pos0 + t * tt + row
    for s in range(sb):
        ext_p[s, POOL_ROWS:POOL_ROWS + tt, :] = proj_ref[s, :, OFF_POOL:OFF_POOL + POOL_DIM]
        ext_c[s, CONV_ROWS:CONV_ROWS + tt, :] = proj_ref[s, :, OFF_QKV:OFF_QKV + CONV_DIM]

        for gi, win in enumerate(POOL_WINDOWS):
            cs = slice(gi * POOL_GROUP_DIM, (gi + 1) * POOL_GROUP_DIM)
            u = ext_p[s, POOL_ROWS:POOL_ROWS + tt, cs]
            acc = u
            for j in range(1, win):
                acc = acc + ext_p[s, POOL_ROWS - j:POOL_ROWS - j + tt, cs]
            cnt = jnp.minimum(win, pos + 1).astype(F32)
            pooled = acc / cnt - u
            ypool_ref[s, :, cs] = _dot(pooled, wpool_ref[gi]) * pscale_ref[:, cs]

        for ci in range(CONV_DIM // LANES):
            cs = slice(ci * LANES, (ci + 1) * LANES)
            acc = ext_c[s, base:base + tt, cs] * wconv_ref[0:1, cs]
            for j in range(1, CONV_WIDTH):
                acc = acc + ext_c[s, base + j:base + j + tt, cs] * wconv_ref[j:j + 1, cs]
            y = acc * _sigmoid(acc)
            hs = slice((ci % N_HEADS) * HEAD_DIM, (ci % N_HEADS + 1) * HEAD_DIM)
            if ci < 2 * N_HEADS:
                y = y * lax.rsqrt(jnp.sum(y * y, axis=-1, keepdims=True) + EPS)
            if ci < N_HEADS:
                q_s[s, :, hs] = y * (HEAD_DIM ** -0.5)
            elif ci < 2 * N_HEADS:
                k_s[s, :, hs] = y
            else:
                v_s[s, :, hs] = y

        beta = _sigmoid(proj_ref[s, :, OFF_B:OFF_B + LANES])
        g = -jnp.exp(alog_ref[...]) * _softplus(proj_ref[s, :, OFF_A:OFF_A + LANES] + dtb_ref[...])
        if tv < tt:
            beta = jnp.where(row < tv, beta, 0.0)
            g = jnp.where(row < tv, g, 0.0)
        b_s[s] = beta
        g_s[s] = g

    @pl.when(t == n_tiles - 1)
    def _():
        npool_ref[...] = ext_p[:, tv + POOL_ROWS - POOL_BUF:tv + POOL_ROWS, :]
        nconv_ref[...] = ext_c[:, tv + base:tv + CONV_ROWS, :]

    if n_tiles > 1:
        ext_p[:, 0:POOL_ROWS, :] = ext_p[:, tt:tt + POOL_ROWS, :]
        ext_c[:, 0:CONV_ROWS, :] = ext_c[:, tt:tt + CONV_ROWS, :]

    ri = lax.broadcasted_iota(jnp.int32, (chunk, chunk), 0)
    cj = lax.broadcasted_iota(jnp.int32, (chunk, chunk), 1)
    causal = ri >= cj
    strict = ri > cj
    ltri = causal.astype(F32)
    eye = (ri == cj).astype(F32)
    n_doublings = chunk.bit_length() - 2
    units = [(s, h) for s in range(sb) for h in range(N_HEADS)]
    hsl = [slice(h * HEAD_DIM, (h + 1) * HEAD_DIM) for h in range(N_HEADS)]

    def chunk_body(ci, carry):
        r0 = pl.multiple_of(ci * chunk, chunk)
        rows = pl.ds(r0, chunk)
        g_all, g_t, b_all = [], [], []
        for s in range(sb):
            ga = jnp.dot(ltri, g_s[s, rows, :], precision=HIGHEST, preferred_element_type=F32)
            g_all.append(ga)
            if chunk < LANES:
                ga = jnp.concatenate([ga, jnp.zeros((LANES - chunk, LANES), F32)], axis=0)
            g_t.append(ga.T)
            b_all.append(b_s[s, rows, :])
        g_col = [g_all[s][:, h:h + 1] for s, h in units]
        dec = [jnp.exp(jnp.minimum(g_col[i] - g_t[s][h:h + 1, 0:chunk], 0.0)) for i, (s, h) in enumerate(units)]
        b_col = [b_all[s][:, h:h + 1] for s, h in units]
        q = [q_s[s, rows, hsl[h]] for s, h in units]
        k = [k_s[s, rows, hsl[h]] for s, h in units]
        idx = range(len(units))
        kb = [k[i] * b_col[i] for i in idx]
        kbk = [_dot_nt(kb[i], k[i]) for i in idx]
        qk = [_dot_nt(q[i], k[i]) for i in idx]
        p_mat = [jnp.where(strict, kbk[i] * dec[i], 0.0) for i in idx]
        t_inv = [eye - p_mat[i] for i in idx]
        for _ in range(n_doublings):
            p_mat = [_dot(p_mat[i], p_mat[i]) for i in idx]
            xp = [_dot(t_inv[i], p_mat[i]) for i in idx]
            t_inv = [t_inv[i] + xp[i] for i in idx]
        g_exp = [jnp.exp(g_col[i]) for i in idx]
        wu = [_dot(t_inv[i], jnp.concatenate([kb[i] * g_exp[i], v_s[s, rows, hsl[h]] * b_col[i]], axis=1))
              for i, (s, h) in enumerate(units)]
        s_old = [state[s, h] for s, h in units]
        res = [_dot(jnp.concatenate([wu[i][:, 0:HEAD_DIM], q[i] * g_exp[i]], axis=0), s_old[i]) for i in idx]
        v_new = [wu[i][:, HEAD_DIM:2 * HEAD_DIM] - res[i][0:chunk] for i in idx]
        a_qk = [jnp.where(causal, qk[i] * dec[i], 0.0) for i in idx]
        g_last = [g_all[s][chunk - 1:chunk, h:h + 1] for s, h in units]
        k_dec = [k[i] * jnp.exp(g_last[i] - g_col[i]) for i in idx]
        intra = [_dot(a_qk[i], v_new[i]) for i in idx]
        upd = [_dot_tn(k_dec[i], v_new[i]) for i in idx]
        for i, (s, h) in enumerate(units):
            state[s, h] = s_old[i] * jnp.exp(g_last[i]) + upd[i]
            o_c = res[i][chunk:2 * chunk] + intra[i]
            z_c = proj_ref[s, rows, OFF_Z + h * HEAD_DIM:OFF_Z + (h + 1) * HEAD_DIM]
            ydelta_ref[s, rows, hsl[h]] = _rms(o_c, onw_ref[...]) * (z_c * _sigmoid(z_c))
        return carry

    lax.fori_loop(0, tt // chunk, chunk_body, 0)

    @pl.when(t == n_tiles - 1)
    def _():
        ns_ref[...] = state[...]


def _mixers(proj, pool0, conv0, s0, wpool, pscale, wconv, alog, dtb, onw, *, n_seq, sb, tt, tv, chunk, pos0, n_tiles):
    kern = functools.partial(_mixer_kernel, sb=sb, tt=tt, tv=tv, chunk=chunk, pos0=pos0, n_tiles=n_tiles)
    seq_len = n_tiles * tt
    tile = lambda b, t: (b, t, 0)
    seq3 = lambda b, t: (b, 0, 0)
    seq4 = lambda b, t: (b, 0, 0, 0)
    const2 = lambda b, t: (0, 0)
    ypool, ydelta, npool, nconv, ns = pl.pallas_call(
        kern,
        grid=(n_seq // sb, n_tiles),
        in_specs=[
            pl.BlockSpec((sb, tt, MIX_COLS), tile),
            pl.BlockSpec((sb, POOL_BUF, POOL_DIM), seq3),
            pl.BlockSpec((sb, CONV_WIDTH - 1, CONV_DIM), seq3),
            pl.BlockSpec((sb, N_HEADS, HEAD_DIM, HEAD_DIM), seq4),
            pl.BlockSpec((len(POOL_WINDOWS), POOL_GROUP_DIM, POOL_GROUP_DIM), lambda b, t: (0, 0, 0)),
            pl.BlockSpec((1, POOL_DIM), const2),
            pl.BlockSpec((SUBLANES, CONV_DIM), const2),
            pl.BlockSpec((1, LANES), const2),
            pl.BlockSpec((1, LANES), const2),
            pl.BlockSpec((1, HEAD_DIM), const2),
        ],
        out_specs=[
            pl.BlockSpec((sb, tt, POOL_DIM), tile),
            pl.BlockSpec((sb, tt, QK_DIM), tile),
            pl.BlockSpec((sb, POOL_BUF, POOL_DIM), seq3),
            pl.BlockSpec((sb, CONV_WIDTH - 1, CONV_DIM), seq3),
            pl.BlockSpec((sb, N_HEADS, HEAD_DIM, HEAD_DIM), seq4),
        ],
        out_shape=[
            jax.ShapeDtypeStruct((n_seq, seq_len, POOL_DIM), F32),
            jax.ShapeDtypeStruct((n_seq, seq_len, QK_DIM), F32),
            jax.ShapeDtypeStruct((n_seq, POOL_BUF, POOL_DIM), F32),
            jax.ShapeDtypeStruct((n_seq, CONV_WIDTH - 1, CONV_DIM), F32),
            jax.ShapeDtypeStruct((n_seq, N_HEADS, HEAD_DIM, HEAD_DIM), F32),
        ],
        scratch_shapes=[
            pltpu.VMEM((sb, POOL_ROWS + tt, POOL_DIM), F32),
            pltpu.VMEM((sb, CONV_ROWS + tt, CONV_DIM), F32),
            pltpu.VMEM((sb, tt, QK_DIM), F32),
            pltpu.VMEM((sb, tt, QK_DIM), F32),
            pltpu.VMEM((sb, tt, QK_DIM), F32),
            pltpu.VMEM((sb, tt, LANES), F32),
            pltpu.VMEM((sb, tt, LANES), F32),
            pltpu.VMEM((sb, N_HEADS, HEAD_DIM, HEAD_DIM), F32),
        ],
        compiler_params=pltpu.CompilerParams(dimension_semantics=("arbitrary", "arbitrary"),
                                             vmem_limit_bytes=VMEM_LIMIT),
        name="mixers",
    )(proj.reshape(n_seq, seq_len, MIX_COLS), pool0, conv0, s0, wpool, pscale, wconv, alog, dtb, onw)
    n = n_seq * seq_len
    return ypool.reshape(n, POOL_DIM), ydelta.reshape(n, QK_DIM), npool, nconv, ns


def _merge_kernel(x_ref, gates_ref, ypool_ref, ydelta_ref, wbp_ref, wbd_ref, wout_ref, n2w_ref, wr_hi_ref, wr_lo_ref,
                  br_ref, xm_ref, h2_ref, route_ref, wk_ref, cnt_ref, cnt_s, *, period, valid):
    tm = x_ref.shape[0]

    @pl.when(pl.program_id(0) == 0)
    def _():
        cnt_s[...] = jnp.zeros_like(cnt_s)

    bp = _dot(ypool_ref[...], wbp_ref[...])
    bd = _dot(ydelta_ref[...], wbd_ref[...])
    merged = _sigmoid(gates_ref[:, 0:1024]) * bp + _sigmoid(gates_ref[:, 1024:2048]) * bd
    xm = x_ref[...] + _dot(merged, wout_ref[...])
    xm_ref[...] = xm
    h2 = _rms(xm, n2w_ref[...])
    h2_ref[...] = _pack_pairs(h2)

    h2_hi = h2.astype(BF16)
    h2_lo = (h2 - h2_hi.astype(F32)).astype(BF16)
    logits = (jnp.dot(h2_hi, wr_hi_ref[...], preferred_element_type=F32)
              + (jnp.dot(h2_hi, wr_lo_ref[...], preferred_element_type=F32)
                 + jnp.dot(h2_lo, wr_hi_ref[...], preferred_element_type=F32))) + br_ref[...]
    lane = lax.broadcasted_iota(jnp.int32, logits.shape, 1)
    neg = -jnp.inf
    far = LANES - 1
    is_g = lane < N_GROUPS
    g_max = jnp.max(jnp.where(is_g, logits, neg), axis=-1, keepdims=True)
    g_sel = jnp.min(jnp.where(is_g & (logits == g_max), lane, far), axis=-1, keepdims=True)
    p_g = 1.0 / jnp.sum(jnp.where(is_g, jnp.exp(logits - g_max), 0.0), axis=-1, keepdims=True)
    e_lane = lane - N_GROUPS
    is_e = (e_lane >= 0) & (e_lane < N_EXPERTS) & ((e_lane // EXPERTS_PER_GROUP) == g_sel)
    ev = jnp.where(is_e, logits, neg)
    v1 = jnp.max(ev, axis=-1, keepdims=True)
    i1 = jnp.min(jnp.where(is_e & (ev == v1), lane, far), axis=-1, keepdims=True)
    is_e2 = is_e & (lane != i1)
    ev2 = jnp.where(is_e2, logits, neg)
    v2 = jnp.max(ev2, axis=-1, keepdims=True)
    i2 = jnp.min(jnp.where(is_e2 & (ev2 == v2), lane, far), axis=-1, keepdims=True)
    e21 = jnp.exp(v2 - v1)
    w1 = p_g / (1.0 + e21)
    w2 = p_g * e21 / (1.0 + e21)
    e1 = i1 - N_GROUPS
    e2 = i2 - N_GROUPS
    chosen = (lane == e1) | (lane == e2)
    if valid < period:
        row = lax.broadcasted_iota(jnp.int32, (tm, 1), 0)
        chosen = chosen & (lax.rem(row, period) < valid)
    onehot = jnp.where(chosen, 1.0, 0.0)
    rr = lax.broadcasted_iota(jnp.int32, (tm, tm), 0)
    cc = lax.broadcasted_iota(jnp.int32, (tm, tm), 1)
    earlier = jnp.where(rr > cc, 1.0, 0.0).astype(BF16)
    before = jnp.dot(earlier, onehot.astype(BF16), preferred_element_type=F32) + cnt_s[...]
    r1 = jnp.sum(jnp.where(lane == e1, before, 0.0), axis=-1, keepdims=True)
    r2 = jnp.sum(jnp.where(lane == e2, before, 0.0), axis=-1, keepdims=True)
    cnt_s[...] = cnt_s[...] + jnp.sum(onehot, axis=0, keepdims=True)
    cnt_ref[...] = cnt_s[...]
    record = jnp.where(lane == 0, e1.astype(F32), jnp.where(lane == 1, e2.astype(F32),
                                                           jnp.where(lane == 2, r1, jnp.where(lane == 3, r2, 0.0))))
    route_ref[...] = record.T[0:SUBLANES, :]
    lane8 = lax.broadcasted_iota(jnp.int32, (tm, SUBLANES), 1)
    wk_ref[...] = jnp.where(lane8 == 0, w1, jnp.where(lane8 == 1, w2, 0.0))


def _merge(x, proj, ypool, ydelta, wbp, wbd, wout, n2w, wr_hi, wr_lo, br, tm, period, valid):
    n, d = x.shape
    row = lambda i: (i, 0)
    const = lambda i: (0, 0)
    return pl.pallas_call(
        functools.partial(_merge_kernel, period=period, valid=valid),
        grid=(n // tm,),
        in_specs=[
            pl.BlockSpec((tm, d), row),
            pl.BlockSpec((tm, GATE_COLS), row),
            pl.BlockSpec((tm, POOL_DIM), row),
            pl.BlockSpec((tm, QK_DIM), row),
            pl.BlockSpec((POOL_DIM, d), const),
            pl.BlockSpec((QK_DIM, d), const),
            pl.BlockSpec((d, d), const),
            pl.BlockSpec((1, d), const),
            pl.BlockSpec((d, LANES), const),
            pl.BlockSpec((d, LANES), const),
            pl.BlockSpec((1, LANES), const),
        ],
        out_specs=[
            pl.BlockSpec((tm, d), row),
            pl.BlockSpec((tm, d // 2), row),
            pl.BlockSpec((SUBLANES, tm), lambda i: (0, i)),
            pl.BlockSpec((tm, SUBLANES), row),
            pl.BlockSpec((1, LANES), const),
        ],
        out_shape=[
            jax.ShapeDtypeStruct((n, d), F32),
            jax.ShapeDtypeStruct((n, d // 2), U32),
            jax.ShapeDtypeStruct((SUBLANES, n), F32),
            jax.ShapeDtypeStruct((n, SUBLANES), F32),
            jax.ShapeDtypeStruct((1, LANES), F32),
        ],
        scratch_shapes=[pltpu.VMEM((1, LANES), F32)],
        compiler_params=pltpu.CompilerParams(dimension_semantics=("arbitrary",), vmem_limit_bytes=VMEM_LIMIT),
        name="merge",
    )(x, proj, ypool, ydelta, wbp, wbd, wout, n2w, wr_hi, wr_lo, br)


def _expert_kernel(te_ref, nv_ref, x_ref, wg_ref, wu_ref, wd_ref, y_ref):
    i = pl.program_id(0)
    n_valid = nv_ref[i]

    @pl.when(n_valid > 0)
    def _():
        row = lax.broadcasted_iota(jnp.int32, (x_ref.shape[0], 1), 0)
        left, right = _unpack_pairs(jnp.where(row < n_valid, x_ref[...], jnp.uint32(0)))
        x = jnp.concatenate([left.astype(BF16), right.astype(BF16)], axis=1)
        a = _dot(x, wg_ref[0])
        b = _dot(x, wu_ref[0])
        y_ref[...] = _pack_pairs(_dot((a * _sigmoid(a)) * b, wd_ref[0]))

    @pl.when(n_valid == 0)
    def _():
        y_ref[...] = jnp.zeros_like(y_ref)


def _experts(tile_expert, tile_valid, xs, wg, wu, wd, tm):
    n_rows, dp = xs.shape
    d, f = wg.shape[1], wg.shape[2]
    grid_spec = pltpu.PrefetchScalarGridSpec(
        num_scalar_prefetch=2,
        grid=(n_rows // tm,),
        in_specs=[
            pl.BlockSpec((tm, dp), lambda i, te, nv: (jnp.where(nv[i] > 0, i, 0), 0)),
            pl.BlockSpec((1, d, f), lambda i, te, nv: (te[i], 0, 0)),
            pl.BlockSpec((1, d, f), lambda i, te, nv: (te[i], 0, 0)),
            pl.BlockSpec((1, f, d), lambda i, te, nv: (te[i], 0, 0)),
        ],
        out_specs=pl.BlockSpec((tm, dp), lambda i, te, nv: (i, 0)),
    )
    return pl.pallas_call(
        _expert_kernel,
        grid_spec=grid_spec,
        out_shape=jax.ShapeDtypeStruct((n_rows, dp), U32),
        compiler_params=pltpu.CompilerParams(dimension_semantics=("arbitrary",), vmem_limit_bytes=VMEM_LIMIT),
        name="experts",
    )(tile_expert, tile_valid, xs, wg, wu, wd)


def _plan_kernel(route_ref, cnt_ref, pos_ref, tiles_ref, *, tm_e):
    counts = cnt_ref[...]
    padded = jnp.floor((counts + (tm_e - 1)) / tm_e) * tm_e
    lane = lax.broadcasted_iota(jnp.int32, counts.shape, 1)
    ends = padded
    shift = 1
    while shift < N_EXPERTS:
        ends = ends + jnp.where(lane >= shift, pltpu.roll(ends, shift, axis=1), 0.0)
        shift *= 2
    starts = ends - padded

    def lookup(table, key):
        out = jnp.zeros_like(key)
        for e in range(N_EXPERTS):
            out = out + jnp.where(key == e, table[:, e:e + 1], 0.0)
        return out

    for k in range(2):
        pos = lookup(starts, route_ref[k:k + 1, :]) + route_ref[2 + k:3 + k, :]
        pos_ref[k:k + 1, :] = pos.astype(jnp.int32)

    tile_start = lax.broadcasted_iota(jnp.int32, (1, tiles_ref.shape[1]), 1).astype(F32) * tm_e
    tile_expert = jnp.zeros_like(tile_start)
    for e in range(N_EXPERTS):
        tile_expert = tile_expert + jnp.where(ends[:, e:e + 1] <= tile_start, 1.0, 0.0)
    tile_expert = jnp.minimum(tile_expert, N_EXPERTS - 1.0)
    valid = lookup(counts, tile_expert) - (tile_start - lookup(starts, tile_expert))
    tiles_ref[0:1, :] = tile_expert.astype(jnp.int32)
    tiles_ref[1:2, :] = jnp.clip(valid, 0.0, tm_e).astype(jnp.int32)


def _plan(route_t, counts, tm_e, n_tiles):
    n = route_t.shape[1]
    tiles_pad = -(-n_tiles // LANES) * LANES
    return pl.pallas_call(
        functools.partial(_plan_kernel, tm_e=tm_e),
        out_shape=[jax.ShapeDtypeStruct((2, n), jnp.int32), jax.ShapeDtypeStruct((2, tiles_pad), jnp.int32)],
        name="plan",
    )(route_t, counts)


def _sc_workers():
    info = plsc.get_sparse_core_info()
    return info.num_cores, info.num_subcores


def _sc_chunk(per_worker):
    assert per_worker % SUBLANES == 0
    return max(c for c in range(SUBLANES, SC_ROWS + 1, SUBLANES) if per_worker % c == 0)


def _sc_scatter2(rows, idx_a, idx_b, n_out):
    n_cores, n_sub = _sc_workers()
    n, d = rows.shape
    per_worker = n // (n_cores * n_sub)
    assert per_worker * n_cores * n_sub == n
    chunk = _sc_chunk(per_worker)
    mesh = plsc.VectorSubcoreMesh(core_axis_name="c", subcore_axis_name="s")

    @functools.partial(
        pl.kernel, mesh=mesh, out_type=jax.ShapeDtypeStruct((n_out, d), rows.dtype),
        scratch_types=[pltpu.VMEM((chunk,), jnp.int32), pltpu.VMEM((chunk,), jnp.int32),
                       pltpu.VMEM((chunk, d), rows.dtype), pltpu.SemaphoreType.DMA])
    def scatter_rows(rows_hbm, ia_hbm, ib_hbm, out_hbm, ia_v, ib_v, rows_v, sem):
        worker = lax.axis_index("s") * n_cores + lax.axis_index("c")
        base = worker * per_worker

        @pl.loop(0, per_worker // chunk)
        def _(j):
            off = pl.multiple_of(base + j * chunk, SUBLANES)
            pltpu.sync_copy(ia_hbm.at[pl.ds(off, chunk)], ia_v)
            pltpu.sync_copy(ib_hbm.at[pl.ds(off, chunk)], ib_v)
            pltpu.sync_copy(rows_hbm.at[pl.ds(off, chunk)], rows_v)
            pltpu.async_copy(rows_v, out_hbm.at[ia_v], sem).wait()
            pltpu.async_copy(rows_v, out_hbm.at[ib_v], sem).wait()

    return scatter_rows(rows, idx_a, idx_b)


def _sc_gather(table, idx):
    n_cores, n_sub = _sc_workers()
    n_idx = idx.shape[0]
    d = table.shape[1]
    per_worker = n_idx // (n_cores * n_sub)
    assert per_worker * n_cores * n_sub == n_idx
    chunk = _sc_chunk(per_worker)
    mesh = plsc.VectorSubcoreMesh(core_axis_name="c", subcore_axis_name="s")

    @functools.partial(
        pl.kernel, mesh=mesh, out_type=jax.ShapeDtypeStruct((n_idx, d), table.dtype),
        scratch_types=[pltpu.VMEM((chunk,), jnp.int32), pltpu.VMEM((chunk, d), table.dtype), pltpu.SemaphoreType.DMA])
    def gather_rows(table_hbm, idx_hbm, out_hbm, idx_v, rows_v, sem):
        worker = lax.axis_index("s") * n_cores + lax.axis_index("c")
        base = worker * per_worker

        @pl.loop(0, per_worker // chunk)
        def _(j):
            off = pl.multiple_of(base + j * chunk, SUBLANES)
            pltpu.sync_copy(idx_hbm.at[pl.ds(off, chunk)], idx_v)
            pltpu.async_copy(table_hbm.at[idx_v], rows_v, sem).wait()
            pltpu.sync_copy(rows_v, out_hbm.at[pl.ds(off, chunk)])

    return gather_rows(table, idx)


def _finalize_kernel(xm_ref, ya_ref, yb_ref, wk_ref, fnw_ref, o_ref):
    d = xm_ref.shape[1]
    c = d // 2
    w = wk_ref[...]
    a_left, a_right = _unpack_pairs(ya_ref[...])
    b_left, b_right = _unpack_pairs(yb_ref[...])
    x_left = xm_ref[:, 0:c] + (w[:, 0:1] * a_left + w[:, 1:2] * b_left)
    x_right = xm_ref[:, c:d] + (w[:, 0:1] * a_right + w[:, 1:2] * b_right)
    ms = (jnp.sum(x_left * x_left, axis=-1, keepdims=True) + jnp.sum(x_right * x_right, axis=-1, keepdims=True)) / d
    scale = lax.rsqrt(ms + EPS)
    o_ref[:, 0:c] = x_left * scale * fnw_ref[:, 0:c]
    o_ref[:, c:d] = x_right * scale * fnw_ref[:, c:d]


def _finalize(xm, yk, wk, fnw, tm):
    n, d = xm.shape
    steps = n // tm
    return pl.pallas_call(
        _finalize_kernel,
        grid=(steps,),
        in_specs=[
            pl.BlockSpec((tm, d), lambda i: (i, 0)),
            pl.BlockSpec((tm, d // 2), lambda i: (i, 0)),
            pl.BlockSpec((tm, d // 2), lambda i: (i + steps, 0)),
            pl.BlockSpec((tm, SUBLANES), lambda i: (i, 0)),
            pl.BlockSpec((1, d), lambda i: (0, 0)),
        ],
        out_specs=pl.BlockSpec((tm, d), lambda i: (i, 0)),
        out_shape=jax.ShapeDtypeStruct((n, d), F32),
        compiler_params=pltpu.CompilerParams(dimension_semantics=("arbitrary",), vmem_limit_bytes=VMEM_LIMIT),
        name="finalize",
    )(xm, yk, yk, wk, fnw)


def _moe(xm, h2, route, wk, counts, wg, wu, wd, fnw, *, tm_e, tm_c):
    n = xm.shape[0]
    n_rows = 2 * n + N_EXPERTS * tm_e
    n_tiles = n_rows // tm_e
    pos, tiles = _plan(route, counts, tm_e, n_tiles)
    xs = _sc_scatter2(h2, pos[0], pos[1], n_rows)
    y_sorted = _experts(tiles[0, :n_tiles], tiles[1, :n_tiles], xs, wg, wu, wd, tm_e)
    yk = _sc_gather(y_sorted, pos.reshape(-1))
    return _finalize(xm, yk, wk, fnw, tm_c)


def _layer(x_tokens, pool0, conv0, s0, prm, *, n_seq, sb, tt, tv, chunk, pos0, n_tiles, tm):
    proj, gates = _inproj(x_tokens, prm["norm1_w"], prm["w_in"], tm)
    ypool, ydelta, npool, nconv, ns = _mixers(
        proj, pool0, conv0, s0, prm["w_pool"], prm["pool_scale"], prm["w_conv"], prm["a_log"], prm["dt_bias"],
        prm["o_norm_w"], n_seq=n_seq, sb=sb, tt=tt, tv=tv, chunk=chunk, pos0=pos0, n_tiles=n_tiles)
    xm, h2, route, wk, counts = _merge(
        x_tokens, gates, ypool, ydelta, prm["w_branch_pool"], prm["w_branch_delta"], prm["w_out"], prm["norm2_w"],
        prm["w_router_hi"], prm["w_router_lo"], prm["b_router"], tm, tt, tv)
    return xm, h2, route, wk, counts, npool, nconv, ns


def _pad_lanes(v, width=LANES):
    v = v.reshape(1, -1).astype(F32)
    return jnp.pad(v, ((0, 0), (0, width - v.shape[1])))


def kernel(x_prompt, x_sample, cache_pool, cache_conv, state_delta, norm1_w, w_in, w_pool, pool_scale, w_conv, a_log, dt_bias, o_norm_w, w_branch_pool, w_branch_delta, w_out, norm2_w, w_router_group, b_router_group, w_router_expert, b_router_expert, w_gate, w_up, w_down, final_norm_w):
    n_layers = norm1_w.shape[0]
    assert n_layers == 1, "single-layer step"
    bsz, seq, d = x_prompt.shape
    dbs, dseq, _ = x_sample.shape
    past_len = 16384
    lyr = 0

    wi = w_in[lyr]
    c = [POOL_DIM, POOL_DIM + CONV_DIM]
    c += [c[1] + N_HEADS, c[1] + 2 * N_HEADS]
    c += [c[3] + QK_DIM, c[3] + QK_DIM + d]
    pad8 = lambda m: jnp.pad(m, ((0, 0), (0, LANES - N_HEADS)))
    w_in_r = jnp.concatenate(
        [wi[:, c[3]:c[4]], wi[:, c[0]:c[1]], wi[:, :c[0]], pad8(wi[:, c[1]:c[2]]), pad8(wi[:, c[2]:c[3]]),
         wi[:, c[4]:c[5]], wi[:, c[5]:]], axis=1).astype(BF16)
    w_router = jnp.concatenate(
        [w_router_group[lyr], w_router_expert[lyr].reshape(d, N_EXPERTS),
         jnp.zeros((d, LANES - N_GROUPS - N_EXPERTS), F32)], axis=1).astype(F32)
    w_router_hi = w_router.astype(BF16)
    b_router = _pad_lanes(jnp.concatenate([b_router_group[lyr], b_router_expert[lyr].reshape(-1)]))
    prm = dict(
        norm1_w=norm1_w[lyr].reshape(1, d), w_in=w_in_r,
        w_pool=w_pool[lyr].astype(BF16), pool_scale=pool_scale[lyr].reshape(1, POOL_DIM),
        w_conv=jnp.pad(w_conv[lyr], ((0, SUBLANES - CONV_WIDTH), (0, 0))),
        a_log=_pad_lanes(a_log[lyr]), dt_bias=_pad_lanes(dt_bias[lyr]), o_norm_w=o_norm_w[lyr].reshape(1, HEAD_DIM),
        w_branch_pool=w_branch_pool[lyr].astype(BF16), w_branch_delta=w_branch_delta[lyr].astype(BF16),
        w_out=w_out[lyr].astype(BF16), norm2_w=norm2_w[lyr].reshape(1, d),
        w_router_hi=w_router_hi, w_router_lo=(w_router - w_router_hi.astype(F32)).astype(BF16), b_router=b_router,
    )
    wg = w_gate[lyr].astype(BF16)
    wu = w_up[lyr].astype(BF16)
    wd = w_down[lyr].astype(BF16)
    fnw = final_norm_w.reshape(1, d)

    tt_p = 128
    xm, h2, route, wk, counts, npool, nconv, ns = _layer(
        x_prompt.reshape(bsz * seq, d),
        jnp.zeros((bsz, POOL_BUF, POOL_DIM), F32), jnp.zeros((bsz, CONV_WIDTH - 1, CONV_DIM), F32),
        jnp.zeros((bsz, N_HEADS, HEAD_DIM, HEAD_DIM), F32), prm,
        n_seq=bsz, sb=4, tt=tt_p, tv=tt_p, chunk=64, pos0=0, n_tiles=seq // tt_p, tm=256)
    y_prompt = _moe(xm, h2, route, wk, counts, wg, wu, wd, fnw, tm_e=256, tm_c=256).reshape(bsz, seq, d)
    pool_p = npool[None]
    conv_p = nconv[None]
    delta_p = ns[None]

    tt_s = SUBLANES
    xs_pad = jnp.pad(x_sample, ((0, 0), (0, tt_s - dseq), (0, 0))).reshape(dbs * tt_s, d)
    xm, h2, route, wk, counts, npool, nconv, ns = _layer(
        xs_pad, cache_pool[lyr], cache_conv[lyr], state_delta[lyr], prm,
        n_seq=dbs, sb=8, tt=tt_s, tv=dseq, chunk=tt_s, pos0=past_len, n_tiles=1, tm=256)
    real = lambda a: a.reshape(dbs, tt_s, -1)[:, :dseq].reshape(dbs * dseq, -1)
    route = route.reshape(SUBLANES, dbs, tt_s)[:, :, :dseq].reshape(SUBLANES, dbs * dseq)
    y_sample = _moe(real(xm), real(h2), route, real(wk), counts, wg, wu, wd, fnw,
                    tm_e=128, tm_c=256).reshape(dbs, dseq, d)
    pool_s = npool[None]
    conv_s = nconv[None]
    delta_s = ns[None]
    return (y_prompt, y_sample, pool_p, conv_p, delta_p, pool_s, conv_s, delta_s)
```

```python
import functools

import jax
import jax.numpy as jnp
from jax import lax
from jax.experimental import pallas as pl
from jax.experimental.pallas import tpu as pltpu
from jax.experimental.pallas import tpu_sc as plsc

F32 = jnp.float32
BF16 = jnp.bfloat16
U32 = jnp.uint32
EPS = 1e-6
HIGHEST = lax.Precision.HIGHEST

POOL_WINDOWS = (2, 4, 8, 16)
POOL_GROUP_DIM = 128
POOL_DIM = len(POOL_WINDOWS) * POOL_GROUP_DIM
POOL_BUF = max(POOL_WINDOWS) - 1
N_HEADS = 8
HEAD_DIM = 128
QK_DIM = N_HEADS * HEAD_DIM
CONV_WIDTH = 4
CONV_DIM = 3 * QK_DIM
N_GROUPS = 4
EXPERTS_PER_GROUP = 8
N_EXPERTS = N_GROUPS * EXPERTS_PER_GROUP
LANES = 128
SUBLANES = 8

OFF_Z = 0
OFF_QKV = OFF_Z + QK_DIM
OFF_POOL = OFF_QKV + CONV_DIM
OFF_B = OFF_POOL + POOL_DIM
OFF_A = OFF_B + LANES
MIX_COLS = OFF_A + LANES
GATE_COLS = 2048
IN_COLS = MIX_COLS + GATE_COLS

POOL_ROWS = 16
CONV_ROWS = 8
VMEM_LIMIT = 60 * 1024 * 1024
SC_ROWS = 96


def _sigmoid(x):
    return 1.0 / (1.0 + jnp.exp(-x))


def _softplus(x):
    return jnp.maximum(x, 0.0) + jnp.log1p(jnp.exp(-jnp.abs(x)))


def _rms(x, w):
    return x * lax.rsqrt(jnp.mean(x * x, axis=-1, keepdims=True) + EPS) * w


def _dot(a, b):
    return jnp.dot(a.astype(BF16), b.astype(BF16), preferred_element_type=F32)


def _dot_nt(a, b):
    return lax.dot_general(a.astype(BF16), b.astype(BF16), (((1,), (1,)), ((), ())), preferred_element_type=F32)


def _dot_tn(a, b):
    return lax.dot_general(a.astype(BF16), b.astype(BF16), (((0,), (0,)), ((), ())), preferred_element_type=F32)


def _pack_pairs(x):
    c = x.shape[1] // 2
    hi = pltpu.bitcast(x[:, :c].astype(BF16).astype(F32), U32)
    lo = pltpu.bitcast(x[:, c:].astype(BF16).astype(F32), U32)
    return hi | lax.shift_right_logical(lo, jnp.uint32(16))


def _unpack_pairs(w):
    left = pltpu.bitcast(w & jnp.uint32(0xFFFF0000), F32)
    right = pltpu.bitcast(lax.shift_left(w, jnp.uint32(16)), F32)
    return left, right


def _inproj_kernel(x_ref, nw_ref, w_ref, mix_ref, gate_ref):
    h = _rms(x_ref[...], nw_ref[...]).astype(BF16)
    half = MIX_COLS // 2
    for c0 in (0, half):
        mix_ref[:, c0:c0 + half] = jnp.dot(h, w_ref[:, c0:c0 + half], preferred_element_type=F32)
    half = GATE_COLS // 2
    for c0 in (0, half):
        gate_ref[:, c0:c0 + half] = jnp.dot(h, w_ref[:, MIX_COLS + c0:MIX_COLS + c0 + half],
                                            preferred_element_type=F32)


def _inproj(x, norm_w, w_in_r, tm):
    n, d = x.shape
    return pl.pallas_call(
        _inproj_kernel,
        grid=(n // tm,),
        in_specs=[
            pl.BlockSpec((tm, d), lambda i: (i, 0)),
            pl.BlockSpec((1, d), lambda i: (0, 0)),
            pl.BlockSpec((d, IN_COLS), lambda i: (0, 0)),
        ],
        out_specs=[pl.BlockSpec((tm, MIX_COLS), lambda i: (i, 0)), pl.BlockSpec((tm, GATE_COLS), lambda i: (i, 0))],
        out_shape=[jax.ShapeDtypeStruct((n, MIX_COLS), F32), jax.ShapeDtypeStruct((n, GATE_COLS), F32)],
        compiler_params=pltpu.CompilerParams(dimension_semantics=("arbitrary",), vmem_limit_bytes=VMEM_LIMIT),
        name="inproj",
    )(x, norm_w, w_in_r)


def _conv_silu_qkv(ext_c, seq, tt, wconv_ref, ci):
    base = CONV_ROWS - (CONV_WIDTH - 1)
    cs = slice(ci * LANES, (ci + 1) * LANES)
    idx = (lambda r: (r, cs)) if seq is None else (lambda r: (seq, r, cs))
    acc = ext_c[idx(slice(base, base + tt))] * wconv_ref[0:1, cs]
    for j in range(1, CONV_WIDTH):
        acc = acc + ext_c[idx(slice(base + j, base + j + tt))] * wconv_ref[j:j + 1, cs]
    y = acc * _sigmoid(acc)
    if ci < 2 * N_HEADS:
        y = y * lax.rsqrt(jnp.sum(y * y, axis=-1, keepdims=True) + EPS)
    if ci < N_HEADS:
        y = y * (HEAD_DIM ** -0.5)
    return y


def _inproj_conv_kernel(x_ref, nw_ref, w_ref, wconv_ref, conv0_ref, mix_ref, gate_ref, nconv_ref, ext_c, *, n_tiles):
    t = pl.program_id(1)
    tm = x_ref.shape[1]
    base = CONV_ROWS - (CONV_WIDTH - 1)

    @pl.when(t == 0)
    def _():
        ext_c[base:CONV_ROWS, :] = conv0_ref[0]

    h = _rms(x_ref[0], nw_ref[...]).astype(BF16)

    def proj(c0, c1):
        return jnp.dot(h, w_ref[:, c0:c1], preferred_element_type=F32)

    mix_ref[0, :, OFF_Z:OFF_QKV] = proj(OFF_Z, OFF_QKV)
    step = 4 * LANES
    for c0 in range(0, CONV_DIM, step):
        ext_c[CONV_ROWS:CONV_ROWS + tm, c0:c0 + step] = proj(OFF_QKV + c0, OFF_QKV + c0 + step)
        for ci in range(c0 // LANES, (c0 + step) // LANES):
            mix_ref[0, :, OFF_QKV + ci * LANES:OFF_QKV + (ci + 1) * LANES] = _conv_silu_qkv(ext_c, None, tm, wconv_ref, ci)
    mix_ref[0, :, OFF_POOL:MIX_COLS] = proj(OFF_POOL, MIX_COLS)
    half = GATE_COLS // 2
    for c0 in (0, half):
        gate_ref[0, :, c0:c0 + half] = proj(MIX_COLS + c0, MIX_COLS + c0 + half)

    @pl.when(t == n_tiles - 1)
    def _():
        nconv_ref[0] = ext_c[tm + base:tm + CONV_ROWS, :]

    ext_c[0:CONV_ROWS, :] = ext_c[tm:tm + CONV_ROWS, :]


def _inproj_conv(x, conv0, norm_w, w_in_r, wconv, tm):
    n_seq, seq_len, d = x.shape
    n_tiles = seq_len // tm
    tile = lambda b, t: (b, t, 0)
    const = lambda b, t: (0, 0)
    mix, gates, nconv = pl.pallas_call(
        functools.partial(_inproj_conv_kernel, n_tiles=n_tiles),
        grid=(n_seq, n_tiles),
        in_specs=[
            pl.BlockSpec((1, tm, d), tile),
            pl.BlockSpec((1, d), const),
            pl.BlockSpec((d, IN_COLS), const),
            pl.BlockSpec((SUBLANES, CONV_DIM), const),
            pl.BlockSpec((1, CONV_WIDTH - 1, CONV_DIM), lambda b, t: (b, 0, 0)),
        ],
        out_specs=[
            pl.BlockSpec((1, tm, MIX_COLS), tile),
            pl.BlockSpec((1, tm, GATE_COLS), tile),
            pl.BlockSpec((1, CONV_WIDTH - 1, CONV_DIM), lambda b, t: (b, 0, 0)),
        ],
        out_shape=[
            jax.ShapeDtypeStruct((n_seq, seq_len, MIX_COLS), F32),
            jax.ShapeDtypeStruct((n_seq, seq_len, GATE_COLS), F32),
            jax.ShapeDtypeStruct((n_seq, CONV_WIDTH - 1, CONV_DIM), F32),
        ],
        scratch_shapes=[pltpu.VMEM((CONV_ROWS + tm, CONV_DIM), F32)],
        compiler_params=pltpu.CompilerParams(dimension_semantics=("arbitrary", "arbitrary"),
                                             vmem_limit_bytes=VMEM_LIMIT),
        name="inproj_conv",
    )(x, norm_w, w_in_r, wconv, conv0)
    n = n_seq * seq_len
    return mix.reshape(n, MIX_COLS), gates.reshape(n, GATE_COLS), nconv


def _mixer_kernel(proj_ref, pool0_ref, conv0_ref, s0_ref, wpool_ref, pscale_ref, wconv_ref, alog_ref, dtb_ref, onw_ref,
                  ypool_ref, ydelta_ref, npool_ref, nconv_ref, ns_ref,
                  ext_p, ext_c, q_s, k_s, v_s, g_s, b_s, state,
                  *, sb, tt, tv, chunk, pos0, n_tiles, qkv_done):
    t = pl.program_id(1)
    base = CONV_ROWS - (CONV_WIDTH - 1)

    @pl.when(t == 0)
    def _():
        ext_p[:, POOL_ROWS - POOL_BUF:POOL_ROWS, :] = pool0_ref[...]
        if not qkv_done:
            ext_c[:, base:CONV_ROWS, :] = conv0_ref[...]
        state[...] = s0_ref[...]

    row = lax.broadcasted_iota(jnp.int32, (tt, 1), 0)
    pos = pos0 + t * tt + row
    for s in range(sb):
        ext_p[s, POOL_ROWS:POOL_ROWS + tt, :] = proj_ref[s, :, OFF_POOL:OFF_POOL + POOL_DIM]
        if not qkv_done:
            ext_c[s, CONV_ROWS:CONV_ROWS + tt, :] = proj_ref[s, :, OFF_QKV:OFF_QKV + CONV_DIM]

        for gi, win in enumerate(POOL_WINDOWS):
            cs = slice(gi * POOL_GROUP_DIM, (gi + 1) * POOL_GROUP_DIM)
            u = ext_p[s, POOL_ROWS:POOL_ROWS + tt, cs]
            acc = u
            for j in range(1, win):
                acc = acc + ext_p[s, POOL_ROWS - j:POOL_ROWS - j + tt, cs]
            cnt = jnp.minimum(win, pos + 1).astype(F32)
            pooled = acc / cnt - u
            ypool_ref[s, :, cs] = _dot(pooled, wpool_ref[gi]) * pscale_ref[:, cs]

        for ci in range(0 if qkv_done else CONV_DIM // LANES):
            hs = slice((ci % N_HEADS) * HEAD_DIM, (ci % N_HEADS + 1) * HEAD_DIM)
            (q_s, k_s, v_s)[ci // N_HEADS][s, :, hs] = _conv_silu_qkv(ext_c, s, tt, wconv_ref, ci)

        beta = _sigmoid(proj_ref[s, :, OFF_B:OFF_B + LANES])
        g = -jnp.exp(alog_ref[...]) * _softplus(proj_ref[s, :, OFF_A:OFF_A + LANES] + dtb_ref[...])
        if tv < tt:
            beta = jnp.where(row < tv, beta, 0.0)
            g = jnp.where(row < tv, g, 0.0)
        b_s[s] = beta
        g_s[s] = g

    @pl.when(t == n_tiles - 1)
    def _():
        npool_ref[...] = ext_p[:, tv + POOL_ROWS - POOL_BUF:tv + POOL_ROWS, :]
        if qkv_done:
            nconv_ref[...] = jnp.zeros_like(nconv_ref)
        else:
            nconv_ref[...] = ext_c[:, tv + base:tv + CONV_ROWS, :]

    if n_tiles > 1:
        ext_p[:, 0:POOL_ROWS, :] = ext_p[:, tt:tt + POOL_ROWS, :]
        if not qkv_done:
            ext_c[:, 0:CONV_ROWS, :] = ext_c[:, tt:tt + CONV_ROWS, :]

    ri = lax.broadcasted_iota(jnp.int32, (chunk, chunk), 0)
    cj = lax.broadcasted_iota(jnp.int32, (chunk, chunk), 1)
    causal = ri >= cj
    strict = ri > cj
    ltri = causal.astype(F32)
    eye = (ri == cj).astype(F32)
    n_doublings = chunk.bit_length() - 2
    units = [(s, h) for s in range(sb) for h in range(N_HEADS)]
    hsl = [slice(h * HEAD_DIM, (h + 1) * HEAD_DIM) for h in range(N_HEADS)]

    def qkv(which, s, rows, h):
        if qkv_done:
            c0 = OFF_QKV + which * QK_DIM + h * HEAD_DIM
            return proj_ref[s, rows, c0:c0 + HEAD_DIM]
        return (q_s, k_s, v_s)[which][s, rows, hsl[h]]

    def chunk_body(ci, carry):
        r0 = pl.multiple_of(ci * chunk, chunk)
        rows = pl.ds(r0, chunk)
        g_all, g_t, b_all = [], [], []
        for s in range(sb):
            ga = jnp.dot(ltri, g_s[s, rows, :], precision=HIGHEST, preferred_element_type=F32)
            g_all.append(ga)
            if chunk < LANES:
                ga = jnp.concatenate([ga, jnp.zeros((LANES - chunk, LANES), F32)], axis=0)
            g_t.append(ga.T)
            b_all.append(b_s[s, rows, :])
        g_col = [g_all[s][:, h:h + 1] for s, h in units]
        dec = [jnp.exp(jnp.minimum(g_col[i] - g_t[s][h:h + 1, 0:chunk], 0.0)) for i, (s, h) in enumerate(units)]
        b_col = [b_all[s][:, h:h + 1] for s, h in units]
        q = [qkv(0, s, rows, h) for s, h in units]
        k = [qkv(1, s, rows, h) for s, h in units]
        idx = range(len(units))
        kb = [k[i] * b_col[i] for i in idx]
        kbk = [_dot_nt(kb[i], k[i]) for i in idx]
        qk = [_dot_nt(q[i], k[i]) for i in idx]
        p_mat = [jnp.where(strict, kbk[i] * dec[i], 0.0) for i in idx]
        t_inv = [eye - p_mat[i] for i in idx]
        for _ in range(n_doublings):
            p_mat = [_dot(p_mat[i], p_mat[i]) for i in idx]
            xp = [_dot(t_inv[i], p_mat[i]) for i in idx]
            t_inv = [t_inv[i] + xp[i] for i in idx]
        g_exp = [jnp.exp(g_col[i]) for i in idx]
        wu = [_dot(t_inv[i], jnp.concatenate([kb[i] * g_exp[i], qkv(2, s, rows, h) * b_col[i]], axis=1))
              for i, (s, h) in enumerate(units)]
        s_old = [state[s, h] for s, h in units]
        res = [_dot(jnp.concatenate([wu[i][:, 0:HEAD_DIM], q[i] * g_exp[i]], axis=0), s_old[i]) for i in idx]
        v_new = [wu[i][:, HEAD_DIM:2 * HEAD_DIM] - res[i][0:chunk] for i in idx]
        a_qk = [jnp.where(causal, qk[i] * dec[i], 0.0) for i in idx]
        g_last = [g_all[s][chunk - 1:chunk, h:h + 1] for s, h in units]
        k_dec = [k[i] * jnp.exp(g_last[i] - g_col[i]) for i in idx]
        intra = [_dot(a_qk[i], v_new[i]) for i in idx]
        upd = [_dot_tn(k_dec[i], v_new[i]) for i in idx]
        for i, (s, h) in enumerate(units):
            state[s, h] = s_old[i] * jnp.exp(g_last[i]) + upd[i]
            o_c = res[i][chunk:2 * chunk] + intra[i]
            z_c = proj_ref[s, rows, OFF_Z + h * HEAD_DIM:OFF_Z + (h + 1) * HEAD_DIM]
            ydelta_ref[s, rows, hsl[h]] = _rms(o_c, onw_ref[...]) * (z_c * _sigmoid(z_c))
        return carry

    lax.fori_loop(0, tt // chunk, chunk_body, 0)

    @pl.when(t == n_tiles - 1)
    def _():
        ns_ref[...] = state[...]


def _mixers(proj, pool0, conv0, s0, wpool, pscale, wconv, alog, dtb, onw, *, n_seq, sb, tt, tv, chunk, pos0, n_tiles,
            qkv_done):
    kern = functools.partial(_mixer_kernel, sb=sb, tt=tt, tv=tv, chunk=chunk, pos0=pos0, n_tiles=n_tiles,
                             qkv_done=qkv_done)
    seq_len = n_tiles * tt
    tile = lambda b, t: (b, t, 0)
    seq3 = lambda b, t: (b, 0, 0)
    seq4 = lambda b, t: (b, 0, 0, 0)
    const2 = lambda b, t: (0, 0)
    ypool, ydelta, npool, nconv, ns = pl.pallas_call(
        kern,
        grid=(n_seq // sb, n_tiles),
        in_specs=[
            pl.BlockSpec((sb, tt, MIX_COLS), tile),
            pl.BlockSpec((sb, POOL_BUF, POOL_DIM), seq3),
            pl.BlockSpec((sb, CONV_WIDTH - 1, CONV_DIM), seq3),
            pl.BlockSpec((sb, N_HEADS, HEAD_DIM, HEAD_DIM), seq4),
            pl.BlockSpec((len(POOL_WINDOWS), POOL_GROUP_DIM, POOL_GROUP_DIM), lambda b, t: (0, 0, 0)),
            pl.BlockSpec((1, POOL_DIM), const2),
            pl.BlockSpec((SUBLANES, CONV_DIM), const2),
            pl.BlockSpec((1, LANES), const2),
            pl.BlockSpec((1, LANES), const2),
            pl.BlockSpec((1, HEAD_DIM), const2),
        ],
        out_specs=[
            pl.BlockSpec((sb, tt, POOL_DIM), tile),
            pl.BlockSpec((sb, tt, QK_DIM), tile),
            pl.BlockSpec((sb, POOL_BUF, POOL_DIM), seq3),
            pl.BlockSpec((sb, CONV_WIDTH - 1, CONV_DIM), seq3),
            pl.BlockSpec((sb, N_HEADS, HEAD_DIM, HEAD_DIM), seq4),
        ],
        out_shape=[
            jax.ShapeDtypeStruct((n_seq, seq_len, POOL_DIM), F32),
            jax.ShapeDtypeStruct((n_seq, seq_len, QK_DIM), F32),
            jax.ShapeDtypeStruct((n_seq, POOL_BUF, POOL_DIM), F32),
            jax.ShapeDtypeStruct((n_seq, CONV_WIDTH - 1, CONV_DIM), F32),
            jax.ShapeDtypeStruct((n_seq, N_HEADS, HEAD_DIM, HEAD_DIM), F32),
        ],
        scratch_shapes=[
            pltpu.VMEM((sb, POOL_ROWS + tt, POOL_DIM), F32),
            pltpu.VMEM((sb, CONV_ROWS + tt, CONV_DIM), F32),
            pltpu.VMEM((sb, tt, QK_DIM), F32),
            pltpu.VMEM((sb, tt, QK_DIM), F32),
            pltpu.VMEM((sb, tt, QK_DIM), F32),
            pltpu.VMEM((sb, tt, LANES), F32),
            pltpu.VMEM((sb, tt, LANES), F32),
            pltpu.VMEM((sb, N_HEADS, HEAD_DIM, HEAD_DIM), F32),
        ],
        compiler_params=pltpu.CompilerParams(dimension_semantics=("arbitrary", "arbitrary"),
                                             vmem_limit_bytes=VMEM_LIMIT),
        name="mixers",
    )(proj.reshape(n_seq, seq_len, MIX_COLS), pool0, conv0, s0, wpool, pscale, wconv, alog, dtb, onw)
    n = n_seq * seq_len
    return ypool.reshape(n, POOL_DIM), ydelta.reshape(n, QK_DIM), npool, nconv, ns


def _merge_kernel(x_ref, gates_ref, ypool_ref, ydelta_ref, wbp_ref, wbd_ref, wout_ref, n2w_ref, wr_hi_ref, wr_lo_ref,
                  br_ref, xm_ref, h2_ref, route_ref, wk_ref, cnt_ref, cnt_s, *, period, valid):
    tm = x_ref.shape[0]

    @pl.when(pl.program_id(0) == 0)
    def _():
        cnt_s[...] = jnp.zeros_like(cnt_s)

    bp = _dot(ypool_ref[...], wbp_ref[...])
    bd = _dot(ydelta_ref[...], wbd_ref[...])
    merged = _sigmoid(gates_ref[:, 0:1024]) * bp + _sigmoid(gates_ref[:, 1024:2048]) * bd
    xm = x_ref[...] + _dot(merged, wout_ref[...])
    xm_ref[...] = xm
    h2 = _rms(xm, n2w_ref[...])
    h2_ref[...] = _pack_pairs(h2)

    h2_hi = h2.astype(BF16)
    h2_lo = (h2 - h2_hi.astype(F32)).astype(BF16)
    logits = (jnp.dot(h2_hi, wr_hi_ref[...], preferred_element_type=F32)
              + (jnp.dot(h2_hi, wr_lo_ref[...], preferred_element_type=F32)
                 + jnp.dot(h2_lo, wr_hi_ref[...], preferred_element_type=F32))) + br_ref[...]
    lane = lax.broadcasted_iota(jnp.int32, logits.shape, 1)
    neg = -jnp.inf
    far = LANES - 1
    is_g = lane < N_GROUPS
    g_max = jnp.max(jnp.where(is_g, logits, neg), axis=-1, keepdims=True)
    g_sel = jnp.min(jnp.where(is_g & (logits == g_max), lane, far), axis=-1, keepdims=True)
    p_g = 1.0 / jnp.sum(jnp.where(is_g, jnp.exp(logits - g_max), 0.0), axis=-1, keepdims=True)
    e_lane = lane - N_GROUPS
    is_e = (e_lane >= 0) & (e_lane < N_EXPERTS) & ((e_lane // EXPERTS_PER_GROUP) == g_sel)
    ev = jnp.where(is_e, logits, neg)
    v1 = jnp.max(ev, axis=-1, keepdims=True)
    i1 = jnp.min(jnp.where(is_e & (ev == v1), lane, far), axis=-1, keepdims=True)
    is_e2 = is_e & (lane != i1)
    ev2 = jnp.where(is_e2, logits, neg)
    v2 = jnp.max(ev2, axis=-1, keepdims=True)
    i2 = jnp.min(jnp.where(is_e2 & (ev2 == v2), lane, far), axis=-1, keepdims=True)
    e21 = jnp.exp(v2 - v1)
    w1 = p_g / (1.0 + e21)
    w2 = p_g * e21 / (1.0 + e21)
    e1 = i1 - N_GROUPS
    e2 = i2 - N_GROUPS
    chosen = (lane == e1) | (lane == e2)
    if valid < period:
        row = lax.broadcasted_iota(jnp.int32, (tm, 1), 0)
        chosen = chosen & (lax.rem(row, period) < valid)
    onehot = jnp.where(chosen, 1.0, 0.0)
    rr = lax.broadcasted_iota(jnp.int32, (tm, tm), 0)
    cc = lax.broadcasted_iota(jnp.int32, (tm, tm), 1)
    earlier = jnp.where(rr > cc, 1.0, 0.0).astype(BF16)
    before = jnp.dot(earlier, onehot.astype(BF16), preferred_element_type=F32) + cnt_s[...]
    r1 = jnp.sum(jnp.where(lane == e1, before, 0.0), axis=-1, keepdims=True)
    r2 = jnp.sum(jnp.where(lane == e2, before, 0.0), axis=-1, keepdims=True)
    cnt_s[...] = cnt_s[...] + jnp.sum(onehot, axis=0, keepdims=True)
    cnt_ref[...] = cnt_s[...]
    record = jnp.where(lane == 0, e1.astype(F32), jnp.where(lane == 1, e2.astype(F32),
                                                           jnp.where(lane == 2, r1, jnp.where(lane == 3, r2, 0.0))))
    route_ref[...] = record.T[0:SUBLANES, :]
    lane8 = lax.broadcasted_iota(jnp.int32, (tm, SUBLANES), 1)
    wk_ref[...] = jnp.where(lane8 == 0, w1, jnp.where(lane8 == 1, w2, 0.0))


def _merge(x, proj, ypool, ydelta, wbp, wbd, wout, n2w, wr_hi, wr_lo, br, tm, period, valid):
    n, d = x.shape
    row = lambda i: (i, 0)
    const = lambda i: (0, 0)
    return pl.pallas_call(
        functools.partial(_merge_kernel, period=period, valid=valid),
        grid=(n // tm,),
        in_specs=[
            pl.BlockSpec((tm, d), row),
            pl.BlockSpec((tm, GATE_COLS), row),
            pl.BlockSpec((tm, POOL_DIM), row),
            pl.BlockSpec((tm, QK_DIM), row),
            pl.BlockSpec((POOL_DIM, d), const),
            pl.BlockSpec((QK_DIM, d), const),
            pl.BlockSpec((d, d), const),
            pl.BlockSpec((1, d), const),
            pl.BlockSpec((d, LANES), const),
            pl.BlockSpec((d, LANES), const),
            pl.BlockSpec((1, LANES), const),
        ],
        out_specs=[
            pl.BlockSpec((tm, d), row),
            pl.BlockSpec((tm, d // 2), row),
            pl.BlockSpec((SUBLANES, tm), lambda i: (0, i)),
            pl.BlockSpec((tm, SUBLANES), row),
            pl.BlockSpec((1, LANES), const),
        ],
        out_shape=[
            jax.ShapeDtypeStruct((n, d), F32),
            jax.ShapeDtypeStruct((n, d // 2), U32),
            jax.ShapeDtypeStruct((SUBLANES, n), F32),
            jax.ShapeDtypeStruct((n, SUBLANES), F32),
            jax.ShapeDtypeStruct((1, LANES), F32),
        ],
        scratch_shapes=[pltpu.VMEM((1, LANES), F32)],
        compiler_params=pltpu.CompilerParams(dimension_semantics=("arbitrary",), vmem_limit_bytes=VMEM_LIMIT),
        name="merge",
    )(x, proj, ypool, ydelta, wbp, wbd, wout, n2w, wr_hi, wr_lo, br)


def _expert_kernel(te_ref, nv_ref, x_ref, wg_ref, wu_ref, wd_ref, y_ref):
    i = pl.program_id(0)
    n_valid = nv_ref[i]

    @pl.when(n_valid > 0)
    def _():
        row = lax.broadcasted_iota(jnp.int32, (x_ref.shape[0], 1), 0)
        left, right = _unpack_pairs(jnp.where(row < n_valid, x_ref[...], jnp.uint32(0)))
        x = jnp.concatenate([left.astype(BF16), right.astype(BF16)], axis=1)
        a = _dot(x, wg_ref[0])
        b = _dot(x, wu_ref[0])
        y_ref[...] = _pack_pairs(_dot((a * _sigmoid(a)) * b, wd_ref[0]))

    @pl.when(n_valid == 0)
    def _():
        y_ref[...] = jnp.zeros_like(y_ref)


def _experts(tile_expert, tile_valid, xs, wg, wu, wd, tm):
    n_rows, dp = xs.shape
    d, f = wg.shape[1], wg.shape[2]
    grid_spec = pltpu.PrefetchScalarGridSpec(
        num_scalar_prefetch=2,
        grid=(n_rows // tm,),
        in_specs=[
            pl.BlockSpec((tm, dp), lambda i, te, nv: (jnp.where(nv[i] > 0, i, 0), 0)),
            pl.BlockSpec((1, d, f), lambda i, te, nv: (te[i], 0, 0)),
            pl.BlockSpec((1, d, f), lambda i, te, nv: (te[i], 0, 0)),
            pl.BlockSpec((1, f, d), lambda i, te, nv: (te[i], 0, 0)),
        ],
        out_specs=pl.BlockSpec((tm, dp), lambda i, te, nv: (i, 0)),
    )
    return pl.pallas_call(
        _expert_kernel,
        grid_spec=grid_spec,
        out_shape=jax.ShapeDtypeStruct((n_rows, dp), U32),
        compiler_params=pltpu.CompilerParams(dimension_semantics=("arbitrary",), vmem_limit_bytes=VMEM_LIMIT),
        name="experts",
    )(tile_expert, tile_valid, xs, wg, wu, wd)


def _plan_kernel(route_ref, cnt_ref, pos_ref, tiles_ref, *, tm_e):
    counts = cnt_ref[...]
    padded = jnp.floor((counts + (tm_e - 1)) / tm_e) * tm_e
    lane = lax.broadcasted_iota(jnp.int32, counts.shape, 1)
    ends = padded
    shift = 1
    while shift < N_EXPERTS:
        ends = ends + jnp.where(lane >= shift, pltpu.roll(ends, shift, axis=1), 0.0)
        shift *= 2
    starts = ends - padded

    def lookup(table, key):
        out = jnp.zeros_like(key)
        for e in range(N_EXPERTS):
            out = out + jnp.where(key == e, table[:, e:e + 1], 0.0)
        return out

    for k in range(2):
        pos = lookup(starts, route_ref[k:k + 1, :]) + route_ref[2 + k:3 + k, :]
        pos_ref[k:k + 1, :] = pos.astype(jnp.int32)

    tile_start = lax.broadcasted_iota(jnp.int32, (1, tiles_ref.shape[1]), 1).astype(F32) * tm_e
    tile_expert = jnp.zeros_like(tile_start)
    for e in range(N_EXPERTS):
        tile_expert = tile_expert + jnp.where(ends[:, e:e + 1] <= tile_start, 1.0, 0.0)
    tile_expert = jnp.minimum(tile_expert, N_EXPERTS - 1.0)
    valid = lookup(counts, tile_expert) - (tile_start - lookup(starts, tile_expert))
    tiles_ref[0:1, :] = tile_expert.astype(jnp.int32)
    tiles_ref[1:2, :] = jnp.clip(valid, 0.0, tm_e).astype(jnp.int32)


def _plan(route_t, counts, tm_e, n_tiles):
    n = route_t.shape[1]
    tiles_pad = -(-n_tiles // LANES) * LANES
    return pl.pallas_call(
        functools.partial(_plan_kernel, tm_e=tm_e),
        out_shape=[jax.ShapeDtypeStruct((2, n), jnp.int32), jax.ShapeDtypeStruct((2, tiles_pad), jnp.int32)],
        name="plan",
    )(route_t, counts)


def _sc_workers():
    info = plsc.get_sparse_core_info()
    return info.num_cores, info.num_subcores


def _sc_chunk(per_worker):
    assert per_worker % SUBLANES == 0
    return max(c for c in range(SUBLANES, SC_ROWS + 1, SUBLANES) if per_worker % c == 0)


def _sc_scatter2(rows, idx_a, idx_b, n_out):
    n_cores, n_sub = _sc_workers()
    n, d = rows.shape
    per_worker = n // (n_cores * n_sub)
    assert per_worker * n_cores * n_sub == n
    chunk = _sc_chunk(per_worker)
    mesh = plsc.VectorSubcoreMesh(core_axis_name="c", subcore_axis_name="s")

    @functools.partial(
        pl.kernel, mesh=mesh, out_type=jax.ShapeDtypeStruct((n_out, d), rows.dtype),
        scratch_types=[pltpu.VMEM((chunk,), jnp.int32), pltpu.VMEM((chunk,), jnp.int32),
                       pltpu.VMEM((chunk, d), rows.dtype), pltpu.SemaphoreType.DMA])
    def scatter_rows(rows_hbm, ia_hbm, ib_hbm, out_hbm, ia_v, ib_v, rows_v, sem):
        worker = lax.axis_index("s") * n_cores + lax.axis_index("c")
        base = worker * per_worker

        @pl.loop(0, per_worker // chunk)
        def _(j):
            off = pl.multiple_of(base + j * chunk, SUBLANES)
            pltpu.sync_copy(ia_hbm.at[pl.ds(off, chunk)], ia_v)
            pltpu.sync_copy(ib_hbm.at[pl.ds(off, chunk)], ib_v)
            pltpu.sync_copy(rows_hbm.at[pl.ds(off, chunk)], rows_v)
            pltpu.async_copy(rows_v, out_hbm.at[ia_v], sem).wait()
            pltpu.async_copy(rows_v, out_hbm.at[ib_v], sem).wait()

    return scatter_rows(rows, idx_a, idx_b)


def _sc_gather(table, idx):
    n_cores, n_sub = _sc_workers()
    n_idx = idx.shape[0]
    d = table.shape[1]
    per_worker = n_idx // (n_cores * n_sub)
    assert per_worker * n_cores * n_sub == n_idx
    chunk = _sc_chunk(per_worker)
    mesh = plsc.VectorSubcoreMesh(core_axis_name="c", subcore_axis_name="s")

    @functools.partial(
        pl.kernel, mesh=mesh, out_type=jax.ShapeDtypeStruct((n_idx, d), table.dtype),
        scratch_types=[pltpu.VMEM((chunk,), jnp.int32), pltpu.VMEM((chunk, d), table.dtype), pltpu.SemaphoreType.DMA])
    def gather_rows(table_hbm, idx_hbm, out_hbm, idx_v, rows_v, sem):
        worker = lax.axis_index("s") * n_cores + lax.axis_index("c")
        base = worker * per_worker

        @pl.loop(0, per_worker // chunk)
        def _(j):
            off = pl.multiple_of(base + j * chunk, SUBLANES)
            pltpu.sync_copy(idx_hbm.at[pl.ds(off, chunk)], idx_v)
            pltpu.async_copy(table_hbm.at[idx_v], rows_v, sem).wait()
            pltpu.sync_copy(rows_v, out_hbm.at[pl.ds(off, chunk)])

    return gather_rows(table, idx)


def _finalize_kernel(xm_ref, ya_ref, yb_ref, wk_ref, fnw_ref, o_ref):
    d = xm_ref.shape[1]
    c = d // 2
    w = wk_ref[...]
    a_left, a_right = _unpack_pairs(ya_ref[...])
    b_left, b_right = _unpack_pairs(yb_ref[...])
    x_left = xm_ref[:, 0:c] + (w[:, 0:1] * a_left + w[:, 1:2] * b_left)
    x_right = xm_ref[:, c:d] + (w[:, 0:1] * a_right + w[:, 1:2] * b_right)
    ms = (jnp.sum(x_left * x_left, axis=-1, keepdims=True) + jnp.sum(x_right * x_right, axis=-1, keepdims=True)) / d
    scale = lax.rsqrt(ms + EPS)
    o_ref[:, 0:c] = x_left * scale * fnw_ref[:, 0:c]
    o_ref[:, c:d] = x_right * scale * fnw_ref[:, c:d]


def _finalize(xm, yk, wk, fnw, tm):
    n, d = xm.shape
    steps = n // tm
    return pl.pallas_call(
        _finalize_kernel,
        grid=(steps,),
        in_specs=[
            pl.BlockSpec((tm, d), lambda i: (i, 0)),
            pl.BlockSpec((tm, d // 2), lambda i: (i, 0)),
            pl.BlockSpec((tm, d // 2), lambda i: (i + steps, 0)),
            pl.BlockSpec((tm, SUBLANES), lambda i: (i, 0)),
            pl.BlockSpec((1, d), lambda i: (0, 0)),
        ],
        out_specs=pl.BlockSpec((tm, d), lambda i: (i, 0)),
        out_shape=jax.ShapeDtypeStruct((n, d), F32),
        compiler_params=pltpu.CompilerParams(dimension_semantics=("arbitrary",), vmem_limit_bytes=VMEM_LIMIT),
        name="finalize",
    )(xm, yk, yk, wk, fnw)


def _moe(xm, h2, route, wk, counts, wg, wu, wd, fnw, *, tm_e, tm_c):
    n = xm.shape[0]
    n_rows = 2 * n + N_EXPERTS * tm_e
    n_tiles = n_rows // tm_e
    pos, tiles = _plan(route, counts, tm_e, n_tiles)
    xs = _sc_scatter2(h2, pos[0], pos[1], n_rows)
    y_sorted = _experts(tiles[0, :n_tiles], tiles[1, :n_tiles], xs, wg, wu, wd, tm_e)
    yk = _sc_gather(y_sorted, pos.reshape(-1))
    return _finalize(xm, yk, wk, fnw, tm_c)


def _layer(x_tokens, pool0, conv0, s0, prm, *, n_seq, sb, tt, tv, chunk, pos0, n_tiles, tm):
    conv_in_proj = tv == tt
    if conv_in_proj:
        proj, gates, nconv_proj = _inproj_conv(x_tokens.reshape(n_seq, n_tiles * tt, -1), conv0, prm["norm1_w"],
                                               prm["w_in"], prm["w_conv"], tm)
    else:
        proj, gates = _inproj(x_tokens, prm["norm1_w"], prm["w_in"], tm)
    ypool, ydelta, npool, nconv, ns = _mixers(
        proj, pool0, conv0, s0, prm["w_pool"], prm["pool_scale"], prm["w_conv"], prm["a_log"], prm["dt_bias"],
        prm["o_norm_w"], n_seq=n_seq, sb=sb, tt=tt, tv=tv, chunk=chunk, pos0=pos0, n_tiles=n_tiles,
        qkv_done=conv_in_proj)
    if conv_in_proj:
        nconv = nconv_proj
    xm, h2, route, wk, counts = _merge(
        x_tokens, gates, ypool, ydelta, prm["w_branch_pool"], prm["w_branch_delta"], prm["w_out"], prm["norm2_w"],
        prm["w_router_hi"], prm["w_router_lo"], prm["b_router"], tm, tt, tv)
    return xm, h2, route, wk, counts, npool, nconv, ns


def _pad_lanes(v, width=LANES):
    v = v.reshape(1, -1).astype(F32)
    return jnp.pad(v, ((0, 0), (0, width - v.shape[1])))


def kernel(x_prompt, x_sample, cache_pool, cache_conv, state_delta, norm1_w, w_in, w_pool, pool_scale, w_conv, a_log, dt_bias, o_norm_w, w_branch_pool, w_branch_delta, w_out, norm2_w, w_router_group, b_router_group, w_router_expert, b_router_expert, w_gate, w_up, w_down, final_norm_w):
    n_layers = norm1_w.shape[0]
    assert n_layers == 1, "single-layer step"
    bsz, seq, d = x_prompt.shape
    dbs, dseq, _ = x_sample.shape
    past_len = 16384
    lyr = 0

    wi = w_in[lyr]
    c = [POOL_DIM, POOL_DIM + CONV_DIM]
    c += [c[1] + N_HEADS, c[1] + 2 * N_HEADS]
    c += [c[3] + QK_DIM, c[3] + QK_DIM + d]
    pad8 = lambda m: jnp.pad(m, ((0, 0), (0, LANES - N_HEADS)))
    w_in_r = jnp.concatenate(
        [wi[:, c[3]:c[4]], wi[:, c[0]:c[1]], wi[:, :c[0]], pad8(wi[:, c[1]:c[2]]), pad8(wi[:, c[2]:c[3]]),
         wi[:, c[4]:c[5]], wi[:, c[5]:]], axis=1).astype(BF16)
    w_router = jnp.concatenate(
        [w_router_group[lyr], w_router_expert[lyr].reshape(d, N_EXPERTS),
         jnp.zeros((d, LANES - N_GROUPS - N_EXPERTS), F32)], axis=1).astype(F32)
    w_router_hi = w_router.astype(BF16)
    b_router = _pad_lanes(jnp.concatenate([b_router_group[lyr], b_router_expert[lyr].reshape(-1)]))
    prm = dict(
        norm1_w=norm1_w[lyr].reshape(1, d), w_in=w_in_r,
        w_pool=w_pool[lyr].astype(BF16), pool_scale=pool_scale[lyr].reshape(1, POOL_DIM),
        w_conv=jnp.pad(w_conv[lyr], ((0, SUBLANES - CONV_WIDTH), (0, 0))),
        a_log=_pad_lanes(a_log[lyr]), dt_bias=_pad_lanes(dt_bias[lyr]), o_norm_w=o_norm_w[lyr].reshape(1, HEAD_DIM),
        w_branch_pool=w_branch_pool[lyr].astype(BF16), w_branch_delta=w_branch_delta[lyr].astype(BF16),
        w_out=w_out[lyr].astype(BF16), norm2_w=norm2_w[lyr].reshape(1, d),
        w_router_hi=w_router_hi, w_router_lo=(w_router - w_router_hi.astype(F32)).astype(BF16), b_router=b_router,
    )
    wg = w_gate[lyr].astype(BF16)
    wu = w_up[lyr].astype(BF16)
    wd = w_down[lyr].astype(BF16)
    fnw = final_norm_w.reshape(1, d)

    tt_p = 128
    xm, h2, route, wk, counts, npool, nconv, ns = _layer(
        x_prompt.reshape(bsz * seq, d),
        jnp.zeros((bsz, POOL_BUF, POOL_DIM), F32), jnp.zeros((bsz, CONV_WIDTH - 1, CONV_DIM), F32),
        jnp.zeros((bsz, N_HEADS, HEAD_DIM, HEAD_DIM), F32), prm,
        n_seq=bsz, sb=4, tt=tt_p, tv=tt_p, chunk=64, pos0=0, n_tiles=seq // tt_p, tm=256)
    y_prompt = _moe(xm, h2, route, wk, counts, wg, wu, wd, fnw, tm_e=256, tm_c=256).reshape(bsz, seq, d)
    pool_p = npool[None]
    conv_p = nconv[None]
    delta_p = ns[None]

    tt_s = SUBLANES
    xs_pad = jnp.pad(x_sample, ((0, 0), (0, tt_s - dseq), (0, 0))).reshape(dbs * tt_s, d)
    xm, h2, route, wk, counts, npool, nconv, ns = _layer(
        xs_pad, cache_pool[lyr], cache_conv[lyr], state_delta[lyr], prm,
        n_seq=dbs, sb=8, tt=tt_s, tv=dseq, chunk=tt_s, pos0=past_len, n_tiles=1, tm=256)
    real = lambda a: a.reshape(dbs, tt_s, -1)[:, :dseq].reshape(dbs * dseq, -1)
    route = route.reshape(SUBLANES, dbs, tt_s)[:, :, :dseq].reshape(SUBLANES, dbs * dseq)
    y_sample = _moe(real(xm), real(h2), route, real(wk), counts, wg, wu, wd, fnw,
                    tm_e=128, tm_c=256).reshape(dbs, dseq, d)
    pool_s = npool[None]
    conv_s = nconv[None]
    delta_s = ns[None]
    return (y_prompt, y_sample, pool_p, conv_p, delta_p, pool_s, conv_s, delta_s)
```

```python
import functools

import jax
import jax.numpy as jnp
from jax import lax
from jax.experimental import pallas as pl
from jax.experimental.pallas import tpu as pltpu
from jax.experimental.pallas import tpu_sc as plsc

F32 = jnp.float32
BF16 = jnp.bfloat16
U32 = jnp.uint32
EPS = 1e-6
HIGHEST = lax.Precision.HIGHEST

POOL_WINDOWS = (2, 4, 8, 16)
POOL_GROUP_DIM = 128
POOL_DIM = len(POOL_WINDOWS) * POOL_GROUP_DIM
POOL_BUF = max(POOL_WINDOWS) - 1
N_HEADS = 8
HEAD_DIM = 128
QK_DIM = N_HEADS * HEAD_DIM
CONV_WIDTH = 4
CONV_DIM = 3 * QK_DIM
N_GROUPS = 4
EXPERTS_PER_GROUP = 8
N_EXPERTS = N_GROUPS * EXPERTS_PER_GROUP
LANES = 128
SUBLANES = 8

OFF_Z = 0
OFF_QKV = OFF_Z + QK_DIM
OFF_POOL = OFF_QKV + CONV_DIM
OFF_B = OFF_POOL + POOL_DIM
OFF_A = OFF_B + LANES
MIX_COLS = OFF_A + LANES
GATE_COLS = 2048
IN_COLS = MIX_COLS + GATE_COLS

POOL_ROWS = 16
CONV_ROWS = 8
VMEM_LIMIT = 60 * 1024 * 1024
CONV_GROUPS = 6
SC_ROWS = 96


def _sigmoid(x):
    return 1.0 / (1.0 + jnp.exp(-x))


def _softplus(x):
    return jnp.maximum(x, 0.0) + jnp.log1p(jnp.exp(-jnp.abs(x)))


def _rms(x, w):
    return x * lax.rsqrt(jnp.mean(x * x, axis=-1, keepdims=True) + EPS) * w


def _dot(a, b):
    return jnp.dot(a.astype(BF16), b.astype(BF16), preferred_element_type=F32)


def _dot_nt(a, b):
    return lax.dot_general(a.astype(BF16), b.astype(BF16), (((1,), (1,)), ((), ())), preferred_element_type=F32)


def _dot_tn(a, b):
    return lax.dot_general(a.astype(BF16), b.astype(BF16), (((0,), (0,)), ((), ())), preferred_element_type=F32)


def _pack_pairs(x):
    c = x.shape[1] // 2
    hi = pltpu.bitcast(x[:, :c].astype(BF16).astype(F32), U32)
    lo = pltpu.bitcast(x[:, c:].astype(BF16).astype(F32), U32)
    return hi | lax.shift_right_logical(lo, jnp.uint32(16))


def _unpack_pairs(w):
    left = pltpu.bitcast(w & jnp.uint32(0xFFFF0000), F32)
    right = pltpu.bitcast(lax.shift_left(w, jnp.uint32(16)), F32)
    return left, right


def _inproj_kernel(x_ref, nw_ref, w_ref, mix_ref, gate_ref):
    h = _rms(x_ref[...], nw_ref[...]).astype(BF16)
    half = MIX_COLS // 2
    for c0 in (0, half):
        mix_ref[:, c0:c0 + half] = jnp.dot(h, w_ref[:, c0:c0 + half], preferred_element_type=F32)
    half = GATE_COLS // 2
    for c0 in (0, half):
        gate_ref[:, c0:c0 + half] = jnp.dot(h, w_ref[:, MIX_COLS + c0:MIX_COLS + c0 + half],
                                            preferred_element_type=F32)


def _inproj(x, norm_w, w_in_r, tm):
    n, d = x.shape
    return pl.pallas_call(
        _inproj_kernel,
        grid=(n // tm,),
        in_specs=[
            pl.BlockSpec((tm, d), lambda i: (i, 0)),
            pl.BlockSpec((1, d), lambda i: (0, 0)),
            pl.BlockSpec((d, IN_COLS), lambda i: (0, 0)),
        ],
        out_specs=[pl.BlockSpec((tm, MIX_COLS), lambda i: (i, 0)), pl.BlockSpec((tm, GATE_COLS), lambda i: (i, 0))],
        out_shape=[jax.ShapeDtypeStruct((n, MIX_COLS), F32), jax.ShapeDtypeStruct((n, GATE_COLS), F32)],
        compiler_params=pltpu.CompilerParams(dimension_semantics=("arbitrary",), vmem_limit_bytes=VMEM_LIMIT),
        name="inproj",
    )(x, norm_w, w_in_r)


def _conv_silu_qkv(ext_c, seq, tt, wconv_ref, ci, col0=0):
    base = CONV_ROWS - (CONV_WIDTH - 1)
    cs = slice(ci * LANES, (ci + 1) * LANES)
    cl = slice(ci * LANES - col0, (ci + 1) * LANES - col0)
    idx = (lambda r: (r, cl)) if seq is None else (lambda r: (seq, r, cl))
    acc = ext_c[idx(slice(base, base + tt))] * wconv_ref[0:1, cs]
    for j in range(1, CONV_WIDTH):
        acc = acc + ext_c[idx(slice(base + j, base + j + tt))] * wconv_ref[j:j + 1, cs]
    y = acc * _sigmoid(acc)
    if ci < 2 * N_HEADS:
        y = y * lax.rsqrt(jnp.sum(y * y, axis=-1, keepdims=True) + EPS)
    if ci < N_HEADS:
        y = y * (HEAD_DIM ** -0.5)
    return y


def _inproj_conv_kernel(x_ref, nw_ref, w_ref, wconv_ref, conv0_ref, mix_ref, gate_ref, nconv_ref, h_s, *ext, n_tiles):
    t = pl.program_id(1)
    tm = x_ref.shape[1]
    base = CONV_ROWS - (CONV_WIDTH - 1)
    step = CONV_DIM // len(ext)

    @pl.when(t == 0)
    def _():
        for g, buf in enumerate(ext):
            buf[base:CONV_ROWS, :] = conv0_ref[0, :, g * step:(g + 1) * step]

    h_s[...] = _rms(x_ref[0], nw_ref[...]).astype(BF16)

    def proj(c0, c1):
        return jnp.dot(h_s[...], w_ref[:, c0:c1], preferred_element_type=F32)

    def fill_z0():
        mix_ref[0, :, OFF_Z:OFF_Z + QK_DIM // 2] = proj(OFF_Z, OFF_Z + QK_DIM // 2)

    def fill_z1():
        mix_ref[0, :, OFF_Z + QK_DIM // 2:OFF_QKV] = proj(OFF_Z + QK_DIM // 2, OFF_QKV)

    def fill_pool():
        mix_ref[0, :, OFF_POOL:MIX_COLS] = proj(OFF_POOL, MIX_COLS)

    def fill_gate(i):
        q = GATE_COLS // 4
        gate_ref[0, :, i * q:(i + 1) * q] = proj(MIX_COLS + i * q, MIX_COLS + (i + 1) * q)

    fillers = [fill_z0, fill_z1, fill_pool] + [functools.partial(fill_gate, i) for i in range(4)]
    for g, buf in enumerate(ext):
        c0 = g * step
        buf[CONV_ROWS:CONV_ROWS + tm, :] = proj(OFF_QKV + c0, OFF_QKV + c0 + step)
        if fillers:
            fillers.pop(0)()
        for ci in range(c0 // LANES, (c0 + step) // LANES):
            mix_ref[0, :, OFF_QKV + ci * LANES:OFF_QKV + (ci + 1) * LANES] = _conv_silu_qkv(
                buf, None, tm, wconv_ref, ci, c0)
    for f in fillers:
        f()

    @pl.when(t == n_tiles - 1)
    def _():
        for g, buf in enumerate(ext):
            nconv_ref[0, :, g * step:(g + 1) * step] = buf[tm + base:tm + CONV_ROWS, :]

    for buf in ext:
        buf[0:CONV_ROWS, :] = buf[tm:tm + CONV_ROWS, :]


def _inproj_conv(x, conv0, norm_w, w_in_r, wconv, tm):
    n_seq, seq_len, d = x.shape
    n_tiles = seq_len // tm
    tile = lambda b, t: (b, t, 0)
    const = lambda b, t: (0, 0)
    mix, gates, nconv = pl.pallas_call(
        functools.partial(_inproj_conv_kernel, n_tiles=n_tiles),
        grid=(n_seq, n_tiles),
        in_specs=[
            pl.BlockSpec((1, tm, d), tile),
            pl.BlockSpec((1, d), const),
            pl.BlockSpec((d, IN_COLS), const),
            pl.BlockSpec((SUBLANES, CONV_DIM), const),
            pl.BlockSpec((1, CONV_WIDTH - 1, CONV_DIM), lambda b, t: (b, 0, 0)),
        ],
        out_specs=[
            pl.BlockSpec((1, tm, MIX_COLS), tile),
            pl.BlockSpec((1, tm, GATE_COLS), tile),
            pl.BlockSpec((1, CONV_WIDTH - 1, CONV_DIM), lambda b, t: (b, 0, 0)),
        ],
        out_shape=[
            jax.ShapeDtypeStruct((n_seq, seq_len, MIX_COLS), F32),
            jax.ShapeDtypeStruct((n_seq, seq_len, GATE_COLS), F32),
            jax.ShapeDtypeStruct((n_seq, CONV_WIDTH - 1, CONV_DIM), F32),
        ],
        scratch_shapes=[pltpu.VMEM((tm, d), BF16)] + [
            pltpu.VMEM((CONV_ROWS + tm, CONV_DIM // CONV_GROUPS), F32) for _ in range(CONV_GROUPS)],
        compiler_params=pltpu.CompilerParams(dimension_semantics=("arbitrary", "arbitrary"),
                                             vmem_limit_bytes=VMEM_LIMIT),
        name="inproj_conv",
    )(x, norm_w, w_in_r, wconv, conv0)
    n = n_seq * seq_len
    return mix.reshape(n, MIX_COLS), gates.reshape(n, GATE_COLS), nconv


def _mixer_kernel(proj_ref, pool0_ref, conv0_ref, s0_ref, wpool_ref, pscale_ref, wconv_ref, alog_ref, dtb_ref, onw_ref,
                  ypool_ref, ydelta_ref, npool_ref, nconv_ref, ns_ref,
                  ext_p, ext_c, q_s, k_s, v_s, g_s, b_s, state,
                  *, sb, tt, tv, chunk, pos0, n_tiles, qkv_done):
    t = pl.program_id(1)
    base = CONV_ROWS - (CONV_WIDTH - 1)

    @pl.when(t == 0)
    def _():
        ext_p[:, POOL_ROWS - POOL_BUF:POOL_ROWS, :] = pool0_ref[...]
        if not qkv_done:
            ext_c[:, base:CONV_ROWS, :] = conv0_ref[...]
        state[...] = s0_ref[...]

    row = lax.broadcasted_iota(jnp.int32, (tt, 1), 0)
    pos = pos0 + t * tt + row
    for s in range(sb):
        ext_p[s, POOL_ROWS:POOL_ROWS + tt, :] = proj_ref[s, :, OFF_POOL:OFF_POOL + POOL_DIM]
        if not qkv_done:
            ext_c[s, CONV_ROWS:CONV_ROWS + tt, :] = proj_ref[s, :, OFF_QKV:OFF_QKV + CONV_DIM]

        for gi, win in enumerate(POOL_WINDOWS):
            cs = slice(gi * POOL_GROUP_DIM, (gi + 1) * POOL_GROUP_DIM)
            u = ext_p[s, POOL_ROWS:POOL_ROWS + tt, cs]
            acc = u
            for j in range(1, win):
                acc = acc + ext_p[s, POOL_ROWS - j:POOL_ROWS - j + tt, cs]
            cnt = jnp.minimum(win, pos + 1).astype(F32)
            pooled = acc / cnt - u
            ypool_ref[s, :, cs] = _dot(pooled, wpool_ref[gi]) * pscale_ref[:, cs]

        for ci in range(0 if qkv_done else CONV_DIM // LANES):
            hs = slice((ci % N_HEADS) * HEAD_DIM, (ci % N_HEADS + 1) * HEAD_DIM)
            (q_s, k_s, v_s)[ci // N_HEADS][s, :, hs] = _conv_silu_qkv(ext_c, s, tt, wconv_ref, ci)

        beta = _sigmoid(proj_ref[s, :, OFF_B:OFF_B + LANES])
        g = -jnp.exp(alog_ref[...]) * _softplus(proj_ref[s, :, OFF_A:OFF_A + LANES] + dtb_ref[...])
        if tv < tt:
            beta = jnp.where(row < tv, beta, 0.0)
            g = jnp.where(row < tv, g, 0.0)
        b_s[s] = beta
        g_s[s] = g

    @pl.when(t == n_tiles - 1)
    def _():
        npool_ref[...] = ext_p[:, tv + POOL_ROWS - POOL_BUF:tv + POOL_ROWS, :]
        if qkv_done:
            nconv_ref[...] = jnp.zeros_like(nconv_ref)
        else:
            nconv_ref[...] = ext_c[:, tv + base:tv + CONV_ROWS, :]

    if n_tiles > 1:
        ext_p[:, 0:POOL_ROWS, :] = ext_p[:, tt:tt + POOL_ROWS, :]
        if not qkv_done:
            ext_c[:, 0:CONV_ROWS, :] = ext_c[:, tt:tt + CONV_ROWS, :]

    ri = lax.broadcasted_iota(jnp.int32, (chunk, chunk), 0)
    cj = lax.broadcasted_iota(jnp.int32, (chunk, chunk), 1)
    causal = ri >= cj
    strict = ri > cj
    ltri = causal.astype(F32)
    eye = (ri == cj).astype(F32)
    n_doublings = chunk.bit_length() - 2
    units = [(s, h) for s in range(sb) for h in range(N_HEADS)]
    hsl = [slice(h * HEAD_DIM, (h + 1) * HEAD_DIM) for h in range(N_HEADS)]

    def qkv(which, s, rows, h):
        if qkv_done:
            c0 = OFF_QKV + which * QK_DIM + h * HEAD_DIM
            return proj_ref[s, rows, c0:c0 + HEAD_DIM]
        return (q_s, k_s, v_s)[which][s, rows, hsl[h]]

    def chunk_body(ci, carry):
        r0 = pl.multiple_of(ci * chunk, chunk)
        rows = pl.ds(r0, chunk)
        g_all, g_t, b_all = [], [], []
        for s in range(sb):
            ga = jnp.dot(ltri, g_s[s, rows, :], precision=HIGHEST, preferred_element_type=F32)
            g_all.append(ga)
            if chunk < LANES:
                ga = jnp.concatenate([ga, jnp.zeros((LANES - chunk, LANES), F32)], axis=0)
            g_t.append(ga.T)
            b_all.append(b_s[s, rows, :])
        g_col = [g_all[s][:, h:h + 1] for s, h in units]
        dec = [jnp.exp(jnp.minimum(g_col[i] - g_t[s][h:h + 1, 0:chunk], 0.0)) for i, (s, h) in enumerate(units)]
        b_col = [b_all[s][:, h:h + 1] for s, h in units]
        q = [qkv(0, s, rows, h) for s, h in units]
        k = [qkv(1, s, rows, h) for s, h in units]
        idx = range(len(units))
        kb = [k[i] * b_col[i] for i in idx]
        kbk = [_dot_nt(kb[i], k[i]) for i in idx]
        qk = [_dot_nt(q[i], k[i]) for i in idx]
        p_mat = [jnp.where(strict, kbk[i] * dec[i], 0.0) for i in idx]
        t_inv = [eye - p_mat[i] for i in idx]
        for _ in range(n_doublings):
            p_mat = [_dot(p_mat[i], p_mat[i]) for i in idx]
            xp = [_dot(t_inv[i], p_mat[i]) for i in idx]
            t_inv = [t_inv[i] + xp[i] for i in idx]
        g_exp = [jnp.exp(g_col[i]) for i in idx]
        wu = [_dot(t_inv[i], jnp.concatenate([kb[i] * g_exp[i], qkv(2, s, rows, h) * b_col[i]], axis=1))
              for i, (s, h) in enumerate(units)]
        s_old = [state[s, h] for s, h in units]
        res = [_dot(jnp.concatenate([wu[i][:, 0:HEAD_DIM], q[i] * g_exp[i]], axis=0), s_old[i]) for i in idx]
        v_new = [wu[i][:, HEAD_DIM:2 * HEAD_DIM] - res[i][0:chunk] for i in idx]
        a_qk = [jnp.where(causal, qk[i] * dec[i], 0.0) for i in idx]
        g_last = [g_all[s][chunk - 1:chunk, h:h + 1] for s, h in units]
        k_dec = [k[i] * jnp.exp(g_last[i] - g_col[i]) for i in idx]
        intra = [_dot(a_qk[i], v_new[i]) for i in idx]
        upd = [_dot_tn(k_dec[i], v_new[i]) for i in idx]
        for i, (s, h) in enumerate(units):
            state[s, h] = s_old[i] * jnp.exp(g_last[i]) + upd[i]
            o_c = res[i][chunk:2 * chunk] + intra[i]
            z_c = proj_ref[s, rows, OFF_Z + h * HEAD_DIM:OFF_Z + (h + 1) * HEAD_DIM]
            ydelta_ref[s, rows, hsl[h]] = _rms(o_c, onw_ref[...]) * (z_c * _sigmoid(z_c))
        return carry

    lax.fori_loop(0, tt // chunk, chunk_body, 0)

    @pl.when(t == n_tiles - 1)
    def _():
        ns_ref[...] = state[...]


def _mixers(proj, pool0, conv0, s0, wpool, pscale, wconv, alog, dtb, onw, *, n_seq, sb, tt, tv, chunk, pos0, n_tiles,
            qkv_done):
    kern = functools.partial(_mixer_kernel, sb=sb, tt=tt, tv=tv, chunk=chunk, pos0=pos0, n_tiles=n_tiles,
                             qkv_done=qkv_done)
    seq_len = n_tiles * tt
    tile = lambda b, t: (b, t, 0)
    seq3 = lambda b, t: (b, 0, 0)
    seq4 = lambda b, t: (b, 0, 0, 0)
    const2 = lambda b, t: (0, 0)
    ypool, ydelta, npool, nconv, ns = pl.pallas_call(
        kern,
        grid=(n_seq // sb, n_tiles),
        in_specs=[
            pl.BlockSpec((sb, tt, MIX_COLS), tile),
            pl.BlockSpec((sb, POOL_BUF, POOL_DIM), seq3),
            pl.BlockSpec((sb, CONV_WIDTH - 1, CONV_DIM), seq3),
            pl.BlockSpec((sb, N_HEADS, HEAD_DIM, HEAD_DIM), seq4),
            pl.BlockSpec((len(POOL_WINDOWS), POOL_GROUP_DIM, POOL_GROUP_DIM), lambda b, t: (0, 0, 0)),
            pl.BlockSpec((1, POOL_DIM), const2),
            pl.BlockSpec((SUBLANES, CONV_DIM), const2),
            pl.BlockSpec((1, LANES), const2),
            pl.BlockSpec((1, LANES), const2),
            pl.BlockSpec((1, HEAD_DIM), const2),
        ],
        out_specs=[
            pl.BlockSpec((sb, tt, POOL_DIM), tile),
            pl.BlockSpec((sb, tt, QK_DIM), tile),
            pl.BlockSpec((sb, POOL_BUF, POOL_DIM), seq3),
            pl.BlockSpec((sb, CONV_WIDTH - 1, CONV_DIM), seq3),
            pl.BlockSpec((sb, N_HEADS, HEAD_DIM, HEAD_DIM), seq4),
        ],
        out_shape=[
            jax.ShapeDtypeStruct((n_seq, seq_len, POOL_DIM), F32),
            jax.ShapeDtypeStruct((n_seq, seq_len, QK_DIM), F32),
            jax.ShapeDtypeStruct((n_seq, POOL_BUF, POOL_DIM), F32),
            jax.ShapeDtypeStruct((n_seq, CONV_WIDTH - 1, CONV_DIM), F32),
            jax.ShapeDtypeStruct((n_seq, N_HEADS, HEAD_DIM, HEAD_DIM), F32),
        ],
        scratch_shapes=[
            pltpu.VMEM((sb, POOL_ROWS + tt, POOL_DIM), F32),
            pltpu.VMEM((sb, CONV_ROWS + tt, CONV_DIM), F32),
            pltpu.VMEM((sb, tt, QK_DIM), F32),
            pltpu.VMEM((sb, tt, QK_DIM), F32),
            pltpu.VMEM((sb, tt, QK_DIM), F32),
            pltpu.VMEM((sb, tt, LANES), F32),
            pltpu.VMEM((sb, tt, LANES), F32),
            pltpu.VMEM((sb, N_HEADS, HEAD_DIM, HEAD_DIM), F32),
        ],
        compiler_params=pltpu.CompilerParams(dimension_semantics=("arbitrary", "arbitrary"),
                                             vmem_limit_bytes=VMEM_LIMIT),
        name="mixers",
    )(proj.reshape(n_seq, seq_len, MIX_COLS), pool0, conv0, s0, wpool, pscale, wconv, alog, dtb, onw)
    n = n_seq * seq_len
    return ypool.reshape(n, POOL_DIM), ydelta.reshape(n, QK_DIM), npool, nconv, ns


def _merge_kernel(x_ref, gates_ref, ypool_ref, ydelta_ref, wbp_ref, wbd_ref, wout_ref, n2w_ref, wr_hi_ref, wr_lo_ref,
                  br_ref, xm_ref, h2_ref, route_ref, wk_ref, cnt_ref, cnt_s, *, period, valid):
    tm = x_ref.shape[0]

    @pl.when(pl.program_id(0) == 0)
    def _():
        cnt_s[...] = jnp.zeros_like(cnt_s)

    bp = _dot(ypool_ref[...], wbp_ref[...])
    bd = _dot(ydelta_ref[...], wbd_ref[...])
    merged = _sigmoid(gates_ref[:, 0:1024]) * bp + _sigmoid(gates_ref[:, 1024:2048]) * bd
    xm = x_ref[...] + _dot(merged, wout_ref[...])
    xm_ref[...] = xm
    h2 = _rms(xm, n2w_ref[...])
    h2_ref[...] = _pack_pairs(h2)

    h2_hi = h2.astype(BF16)
    h2_lo = (h2 - h2_hi.astype(F32)).astype(BF16)
    logits = (jnp.dot(h2_hi, wr_hi_ref[...], preferred_element_type=F32)
              + (jnp.dot(h2_hi, wr_lo_ref[...], preferred_element_type=F32)
                 + jnp.dot(h2_lo, wr_hi_ref[...], preferred_element_type=F32))) + br_ref[...]
    lane = lax.broadcasted_iota(jnp.int32, logits.shape, 1)
    neg = -jnp.inf
    far = LANES - 1
    is_g = lane < N_GROUPS
    g_max = jnp.max(jnp.where(is_g, logits, neg), axis=-1, keepdims=True)
    g_sel = jnp.min(jnp.where(is_g & (logits == g_max), lane, far), axis=-1, keepdims=True)
    p_g = 1.0 / jnp.sum(jnp.where(is_g, jnp.exp(logits - g_max), 0.0), axis=-1, keepdims=True)
    e_lane = lane - N_GROUPS
    is_e = (e_lane >= 0) & (e_lane < N_EXPERTS) & ((e_lane // EXPERTS_PER_GROUP) == g_sel)
    ev = jnp.where(is_e, logits, neg)
    v1 = jnp.max(ev, axis=-1, keepdims=True)
    i1 = jnp.min(jnp.where(is_e & (ev == v1), lane, far), axis=-1, keepdims=True)
    is_e2 = is_e & (lane != i1)
    ev2 = jnp.where(is_e2, logits, neg)
    v2 = jnp.max(ev2, axis=-1, keepdims=True)
    i2 = jnp.min(jnp.where(is_e2 & (ev2 == v2), lane, far), axis=-1, keepdims=True)
    e21 = jnp.exp(v2 - v1)
    w1 = p_g / (1.0 + e21)
    w2 = p_g * e21 / (1.0 + e21)
    e1 = i1 - N_GROUPS
    e2 = i2 - N_GROUPS
    chosen = (lane == e1) | (lane == e2)
    if valid < period:
        row = lax.broadcasted_iota(jnp.int32, (tm, 1), 0)
        chosen = chosen & (lax.rem(row, period) < valid)
    onehot = jnp.where(chosen, 1.0, 0.0)
    rr = lax.broadcasted_iota(jnp.int32, (tm, tm), 0)
    cc = lax.broadcasted_iota(jnp.int32, (tm, tm), 1)
    earlier = jnp.where(rr > cc, 1.0, 0.0).astype(BF16)
    before = jnp.dot(earlier, onehot.astype(BF16), preferred_element_type=F32) + cnt_s[...]
    r1 = jnp.sum(jnp.where(lane == e1, before, 0.0), axis=-1, keepdims=True)
    r2 = jnp.sum(jnp.where(lane == e2, before, 0.0), axis=-1, keepdims=True)
    cnt_s[...] = cnt_s[...] + jnp.sum(onehot, axis=0, keepdims=True)
    cnt_ref[...] = cnt_s[...]
    record = jnp.where(lane == 0, e1.astype(F32), jnp.where(lane == 1, e2.astype(F32),
                                                           jnp.where(lane == 2, r1, jnp.where(lane == 3, r2, 0.0))))
    route_ref[...] = record.T[0:SUBLANES, :]
    lane8 = lax.broadcasted_iota(jnp.int32, (tm, SUBLANES), 1)
    wk_ref[...] = jnp.where(lane8 == 0, w1, jnp.where(lane8 == 1, w2, 0.0))


def _merge(x, proj, ypool, ydelta, wbp, wbd, wout, n2w, wr_hi, wr_lo, br, tm, period, valid):
    n, d = x.shape
    row = lambda i: (i, 0)
    const = lambda i: (0, 0)
    return pl.pallas_call(
        functools.partial(_merge_kernel, period=period, valid=valid),
        grid=(n // tm,),
        in_specs=[
            pl.BlockSpec((tm, d), row),
            pl.BlockSpec((tm, GATE_COLS), row),
            pl.BlockSpec((tm, POOL_DIM), row),
            pl.BlockSpec((tm, QK_DIM), row),
            pl.BlockSpec((POOL_DIM, d), const),
            pl.BlockSpec((QK_DIM, d), const),
            pl.BlockSpec((d, d), const),
            pl.BlockSpec((1, d), const),
            pl.BlockSpec((d, LANES), const),
            pl.BlockSpec((d, LANES), const),
            pl.BlockSpec((1, LANES), const),
        ],
        out_specs=[
            pl.BlockSpec((tm, d), row),
            pl.BlockSpec((tm, d // 2), row),
            pl.BlockSpec((SUBLANES, tm), lambda i: (0, i)),
            pl.BlockSpec((tm, SUBLANES), row),
            pl.BlockSpec((1, LANES), const),
        ],
        out_shape=[
            jax.ShapeDtypeStruct((n, d), F32),
            jax.ShapeDtypeStruct((n, d // 2), U32),
            jax.ShapeDtypeStruct((SUBLANES, n), F32),
            jax.ShapeDtypeStruct((n, SUBLANES), F32),
            jax.ShapeDtypeStruct((1, LANES), F32),
        ],
        scratch_shapes=[pltpu.VMEM((1, LANES), F32)],
        compiler_params=pltpu.CompilerParams(dimension_semantics=("arbitrary",), vmem_limit_bytes=VMEM_LIMIT),
        name="merge",
    )(x, proj, ypool, ydelta, wbp, wbd, wout, n2w, wr_hi, wr_lo, br)


def _expert_kernel(te_ref, nv_ref, x_ref, wg_ref, wu_ref, wd_ref, y_ref):
    i = pl.program_id(0)
    n_valid = nv_ref[i]

    @pl.when(n_valid > 0)
    def _():
        row = lax.broadcasted_iota(jnp.int32, (x_ref.shape[0], 1), 0)
        left, right = _unpack_pairs(jnp.where(row < n_valid, x_ref[...], jnp.uint32(0)))
        x = jnp.concatenate([left.astype(BF16), right.astype(BF16)], axis=1)
        a = _dot(x, wg_ref[0])
        b = _dot(x, wu_ref[0])
        y_ref[...] = _pack_pairs(_dot((a * _sigmoid(a)) * b, wd_ref[0]))

    @pl.when(n_valid == 0)
    def _():
        y_ref[...] = jnp.zeros_like(y_ref)


def _experts(tile_expert, tile_valid, xs, wg, wu, wd, tm):
    n_rows, dp = xs.shape
    d, f = wg.shape[1], wg.shape[2]
    grid_spec = pltpu.PrefetchScalarGridSpec(
        num_scalar_prefetch=2,
        grid=(n_rows // tm,),
        in_specs=[
            pl.BlockSpec((tm, dp), lambda i, te, nv: (jnp.where(nv[i] > 0, i, 0), 0)),
            pl.BlockSpec((1, d, f), lambda i, te, nv: (te[i], 0, 0)),
            pl.BlockSpec((1, d, f), lambda i, te, nv: (te[i], 0, 0)),
            pl.BlockSpec((1, f, d), lambda i, te, nv: (te[i], 0, 0)),
        ],
        out_specs=pl.BlockSpec((tm, dp), lambda i, te, nv: (i, 0)),
    )
    return pl.pallas_call(
        _expert_kernel,
        grid_spec=grid_spec,
        out_shape=jax.ShapeDtypeStruct((n_rows, dp), U32),
        compiler_params=pltpu.CompilerParams(dimension_semantics=("arbitrary",), vmem_limit_bytes=VMEM_LIMIT),
        name="experts",
    )(tile_expert, tile_valid, xs, wg, wu, wd)


def _plan_kernel(route_ref, cnt_ref, pos_ref, tiles_ref, *, tm_e):
    counts = cnt_ref[...]
    padded = jnp.floor((counts + (tm_e - 1)) / tm_e) * tm_e
    lane = lax.broadcasted_iota(jnp.int32, counts.shape, 1)
    ends = padded
    shift = 1
    while shift < N_EXPERTS:
        ends = ends + jnp.where(lane >= shift, pltpu.roll(ends, shift, axis=1), 0.0)
        shift *= 2
    starts = ends - padded

    def lookup(table, key):
        out = jnp.zeros_like(key)
        for e in range(N_EXPERTS):
            out = out + jnp.where(key == e, table[:, e:e + 1], 0.0)
        return out

    for k in range(2):
        pos = lookup(starts, route_ref[k:k + 1, :]) + route_ref[2 + k:3 + k, :]
        pos_ref[k:k + 1, :] = pos.astype(jnp.int32)

    tile_start = lax.broadcasted_iota(jnp.int32, (1, tiles_ref.shape[1]), 1).astype(F32) * tm_e
    tile_expert = jnp.zeros_like(tile_start)
    for e in range(N_EXPERTS):
        tile_expert = tile_expert + jnp.where(ends[:, e:e + 1] <= tile_start, 1.0, 0.0)
    tile_expert = jnp.minimum(tile_expert, N_EXPERTS - 1.0)
    valid = lookup(counts, tile_expert) - (tile_start - lookup(starts, tile_expert))
    tiles_ref[0:1, :] = tile_expert.astype(jnp.int32)
    tiles_ref[1:2, :] = jnp.clip(valid, 0.0, tm_e).astype(jnp.int32)


def _plan(route_t, counts, tm_e, n_tiles):
    n = route_t.shape[1]
    tiles_pad = -(-n_tiles // LANES) * LANES
    return pl.pallas_call(
        functools.partial(_plan_kernel, tm_e=tm_e),
        out_shape=[jax.ShapeDtypeStruct((2, n), jnp.int32), jax.ShapeDtypeStruct((2, tiles_pad), jnp.int32)],
        name="plan",
    )(route_t, counts)


def _sc_workers():
    info = plsc.get_sparse_core_info()
    return info.num_cores, info.num_subcores


def _sc_chunk(per_worker):
    assert per_worker % SUBLANES == 0
    return max(c for c in range(SUBLANES, SC_ROWS + 1, SUBLANES) if per_worker % c == 0)


def _sc_scatter2(rows, idx_a, idx_b, n_out):
    n_cores, n_sub = _sc_workers()
    n, d = rows.shape
    per_worker = n // (n_cores * n_sub)
    assert per_worker * n_cores * n_sub == n
    chunk = _sc_chunk(per_worker)
    mesh = plsc.VectorSubcoreMesh(core_axis_name="c", subcore_axis_name="s")

    @functools.partial(
        pl.kernel, mesh=mesh, out_type=jax.ShapeDtypeStruct((n_out, d), rows.dtype),
        scratch_types=[pltpu.VMEM((chunk,), jnp.int32), pltpu.VMEM((chunk,), jnp.int32),
                       pltpu.VMEM((chunk, d), rows.dtype), pltpu.SemaphoreType.DMA])
    def scatter_rows(rows_hbm, ia_hbm, ib_hbm, out_hbm, ia_v, ib_v, rows_v, sem):
        worker = lax.axis_index("s") * n_cores + lax.axis_index("c")
        base = worker * per_worker

        @pl.loop(0, per_worker // chunk)
        def _(j):
            off = pl.multiple_of(base + j * chunk, SUBLANES)
            pltpu.sync_copy(ia_hbm.at[pl.ds(off, chunk)], ia_v)
            pltpu.sync_copy(ib_hbm.at[pl.ds(off, chunk)], ib_v)
            pltpu.sync_copy(rows_hbm.at[pl.ds(off, chunk)], rows_v)
            pltpu.async_copy(rows_v, out_hbm.at[ia_v], sem).wait()
            pltpu.async_copy(rows_v, out_hbm.at[ib_v], sem).wait()

    return scatter_rows(rows, idx_a, idx_b)


def _sc_gather(table, idx):
    n_cores, n_sub = _sc_workers()
    n_idx = idx.shape[0]
    d = table.shape[1]
    per_worker = n_idx // (n_cores * n_sub)
    assert per_worker * n_cores * n_sub == n_idx
    chunk = _sc_chunk(per_worker)
    mesh = plsc.VectorSubcoreMesh(core_axis_name="c", subcore_axis_name="s")

    @functools.partial(
        pl.kernel, mesh=mesh, out_type=jax.ShapeDtypeStruct((n_idx, d), table.dtype),
        scratch_types=[pltpu.VMEM((chunk,), jnp.int32), pltpu.VMEM((chunk, d), table.dtype), pltpu.SemaphoreType.DMA])
    def gather_rows(table_hbm, idx_hbm, out_hbm, idx_v, rows_v, sem):
        worker = lax.axis_index("s") * n_cores + lax.axis_index("c")
        base = worker * per_worker

        @pl.loop(0, per_worker // chunk)
        def _(j):
            off = pl.multiple_of(base + j * chunk, SUBLANES)
            pltpu.sync_copy(idx_hbm.at[pl.ds(off, chunk)], idx_v)
            pltpu.async_copy(table_hbm.at[idx_v], rows_v, sem).wait()
            pltpu.sync_copy(rows_v, out_hbm.at[pl.ds(off, chunk)])

    return gather_rows(table, idx)


def _finalize_kernel(xm_ref, ya_ref, yb_ref, wk_ref, fnw_ref, o_ref):
    d = xm_ref.shape[1]
    c = d // 2
    w = wk_ref[...]
    a_left, a_right = _unpack_pairs(ya_ref[...])
    b_left, b_right = _unpack_pairs(yb_ref[...])
    x_left = xm_ref[:, 0:c] + (w[:, 0:1] * a_left + w[:, 1:2] * b_left)
    x_right = xm_ref[:, c:d] + (w[:, 0:1] * a_right + w[:, 1:2] * b_right)
    ms = (jnp.sum(x_left * x_left, axis=-1, keepdims=True) + jnp.sum(x_right * x_right, axis=-1, keepdims=True)) / d
    scale = lax.rsqrt(ms + EPS)
    o_ref[:, 0:c] = x_left * scale * fnw_ref[:, 0:c]
    o_ref[:, c:d] = x_right * scale * fnw_ref[:, c:d]


def _finalize(xm, yk, wk, fnw, tm):
    n, d = xm.shape
    steps = n // tm
    return pl.pallas_call(
        _finalize_kernel,
        grid=(steps,),
        in_specs=[
            pl.BlockSpec((tm, d), lambda i: (i, 0)),
            pl.BlockSpec((tm, d // 2), lambda i: (i, 0)),
            pl.BlockSpec((tm, d // 2), lambda i: (i + steps, 0)),
            pl.BlockSpec((tm, SUBLANES), lambda i: (i, 0)),
            pl.BlockSpec((1, d), lambda i: (0, 0)),
        ],
        out_specs=pl.BlockSpec((tm, d), lambda i: (i, 0)),
        out_shape=jax.ShapeDtypeStruct((n, d), F32),
        compiler_params=pltpu.CompilerParams(dimension_semantics=("arbitrary",), vmem_limit_bytes=VMEM_LIMIT),
        name="finalize",
    )(xm, yk, yk, wk, fnw)


def _moe(xm, h2, route, wk, counts, wg, wu, wd, fnw, *, tm_e, tm_c):
    n = xm.shape[0]
    n_rows = 2 * n + N_EXPERTS * tm_e
    n_tiles = n_rows // tm_e
    pos, tiles = _plan(route, counts, tm_e, n_tiles)
    xs = _sc_scatter2(h2, pos[0], pos[1], n_rows)
    y_sorted = _experts(tiles[0, :n_tiles], tiles[1, :n_tiles], xs, wg, wu, wd, tm_e)
    yk = _sc_gather(y_sorted, pos.reshape(-1))
    return _finalize(xm, yk, wk, fnw, tm_c)


def _layer(x_tokens, pool0, conv0, s0, prm, *, n_seq, sb, tt, tv, chunk, pos0, n_tiles, tm):
    conv_in_proj = tv == tt
    if conv_in_proj:
        proj, gates, nconv_proj = _inproj_conv(x_tokens.reshape(n_seq, n_tiles * tt, -1), conv0, prm["norm1_w"],
                                               prm["w_in"], prm["w_conv"], tm)
    else:
        proj, gates = _inproj(x_tokens, prm["norm1_w"], prm["w_in"], tm)
    ypool, ydelta, npool, nconv, ns = _mixers(
        proj, pool0, conv0, s0, prm["w_pool"], prm["pool_scale"], prm["w_conv"], prm["a_log"], prm["dt_bias"],
        prm["o_norm_w"], n_seq=n_seq, sb=sb, tt=tt, tv=tv, chunk=chunk, pos0=pos0, n_tiles=n_tiles,
        qkv_done=conv_in_proj)
    if conv_in_proj:
        nconv = nconv_proj
    xm, h2, route, wk, counts = _merge(
        x_tokens, gates, ypool, ydelta, prm["w_branch_pool"], prm["w_branch_delta"], prm["w_out"], prm["norm2_w"],
        prm["w_router_hi"], prm["w_router_lo"], prm["b_router"], 2 * tm, tt, tv)
    return xm, h2, route, wk, counts, npool, nconv, ns


def _pad_lanes(v, width=LANES):
    v = v.reshape(1, -1).astype(F32)
    return jnp.pad(v, ((0, 0), (0, width - v.shape[1])))


def kernel(x_prompt, x_sample, cache_pool, cache_conv, state_delta, norm1_w, w_in, w_pool, pool_scale, w_conv, a_log, dt_bias, o_norm_w, w_branch_pool, w_branch_delta, w_out, norm2_w, w_router_group, b_router_group, w_router_expert, b_router_expert, w_gate, w_up, w_down, final_norm_w):
    n_layers = norm1_w.shape[0]
    assert n_layers == 1, "single-layer step"
    bsz, seq, d = x_prompt.shape
    dbs, dseq, _ = x_sample.shape
    past_len = 16384
    lyr = 0

    wi = w_in[lyr]
    c = [POOL_DIM, POOL_DIM + CONV_DIM]
    c += [c[1] + N_HEADS, c[1] + 2 * N_HEADS]
    c += [c[3] + QK_DIM, c[3] + QK_DIM + d]
    pad8 = lambda m: jnp.pad(m, ((0, 0), (0, LANES - N_HEADS)))
    w_in_r = jnp.concatenate(
        [wi[:, c[3]:c[4]], wi[:, c[0]:c[1]], wi[:, :c[0]], pad8(wi[:, c[1]:c[2]]), pad8(wi[:, c[2]:c[3]]),
         wi[:, c[4]:c[5]], wi[:, c[5]:]], axis=1).astype(BF16)
    w_router = jnp.concatenate(
        [w_router_group[lyr], w_router_expert[lyr].reshape(d, N_EXPERTS),
         jnp.zeros((d, LANES - N_GROUPS - N_EXPERTS), F32)], axis=1).astype(F32)
    w_router_hi = w_router.astype(BF16)
    b_router = _pad_lanes(jnp.concatenate([b_router_group[lyr], b_router_expert[lyr].reshape(-1)]))
    prm = dict(
        norm1_w=norm1_w[lyr].reshape(1, d), w_in=w_in_r,
        w_pool=w_pool[lyr].astype(BF16), pool_scale=pool_scale[lyr].reshape(1, POOL_DIM),
        w_conv=jnp.pad(w_conv[lyr], ((0, SUBLANES - CONV_WIDTH), (0, 0))),
        a_log=_pad_lanes(a_log[lyr]), dt_bias=_pad_lanes(dt_bias[lyr]), o_norm_w=o_norm_w[lyr].reshape(1, HEAD_DIM),
        w_branch_pool=w_branch_pool[lyr].astype(BF16), w_branch_delta=w_branch_delta[lyr].astype(BF16),
        w_out=w_out[lyr].astype(BF16), norm2_w=norm2_w[lyr].reshape(1, d),
        w_router_hi=w_router_hi, w_router_lo=(w_router - w_router_hi.astype(F32)).astype(BF16), b_router=b_router,
    )
    wg = w_gate[lyr].astype(BF16)
    wu = w_up[lyr].astype(BF16)
    wd = w_down[lyr].astype(BF16)
    fnw = final_norm_w.reshape(1, d)

    tt_p = 128
    xm, h2, route, wk, counts, npool, nconv, ns = _layer(
        x_prompt.reshape(bsz * seq, d),
        jnp.zeros((bsz, POOL_BUF, POOL_DIM), F32), jnp.zeros((bsz, CONV_WIDTH - 1, CONV_DIM), F32),
        jnp.zeros((bsz, N_HEADS, HEAD_DIM, HEAD_DIM), F32), prm,
        n_seq=bsz, sb=4, tt=tt_p, tv=tt_p, chunk=64, pos0=0, n_tiles=seq // tt_p, tm=256)
    y_prompt = _moe(xm, h2, route, wk, counts, wg, wu, wd, fnw, tm_e=256, tm_c=256).reshape(bsz, seq, d)
    pool_p = npool[None]
    conv_p = nconv[None]
    delta_p = ns[None]

    tt_s = SUBLANES
    xs_pad = jnp.pad(x_sample, ((0, 0), (0, tt_s - dseq), (0, 0))).reshape(dbs * tt_s, d)
    xm, h2, route, wk, counts, npool, nconv, ns = _layer(
        xs_pad, cache_pool[lyr], cache_conv[lyr], state_delta[lyr], prm,
        n_seq=dbs, sb=8, tt=tt_s, tv=dseq, chunk=tt_s, pos0=past_len, n_tiles=1, tm=256)
    real = lambda a: a.reshape(dbs, tt_s, -1)[:, :dseq].reshape(dbs * dseq, -1)
    route = route.reshape(SUBLANES, dbs, tt_s)[:, :, :dseq].reshape(SUBLANES, dbs * dseq)
    y_sample = _moe(real(xm), real(h2), route, real(wk), counts, wg, wu, wd, fnw,
                    tm_e=128, tm_c=256).reshape(dbs, dseq, d)
    pool_s = npool[None]
    conv_s = nconv[None]
    delta_s = ns[None]
    return (y_prompt, y_sample, pool_p, conv_p, delta_p, pool_s, conv_s, delta_s)
```

```python
import functools

import jax
import jax.numpy as jnp
from jax import lax
from jax.experimental import pallas as pl
from jax.experimental.pallas import tpu as pltpu
from jax.experimental.pallas import tpu_sc as plsc

F32 = jnp.float32
BF16 = jnp.bfloat16
U32 = jnp.uint32
EPS = 1e-6
HIGHEST = lax.Precision.HIGHEST

POOL_WINDOWS = (2, 4, 8, 16)
POOL_GROUP_DIM = 128
POOL_DIM = len(POOL_WINDOWS) * POOL_GROUP_DIM
POOL_BUF = max(POOL_WINDOWS) - 1
N_HEADS = 8
HEAD_DIM = 128
QK_DIM = N_HEADS * HEAD_DIM
CONV_WIDTH = 4
CONV_DIM = 3 * QK_DIM
N_GROUPS = 4
EXPERTS_PER_GROUP = 8
N_EXPERTS = N_GROUPS * EXPERTS_PER_GROUP
LANES = 128
SUBLANES = 8

OFF_Z = 0
OFF_QKV = OFF_Z + QK_DIM
OFF_POOL = OFF_QKV + CONV_DIM
OFF_B = OFF_POOL + POOL_DIM
OFF_A = OFF_B + LANES
MIX_COLS = OFF_A + LANES
GATE_COLS = 2048
IN_COLS = MIX_COLS + GATE_COLS

POOL_ROWS = 16
CONV_ROWS = 8
VMEM_LIMIT = 60 * 1024 * 1024
CONV_GROUPS = 6
SC_ROWS = 96


def _sigmoid(x):
    return 1.0 / (1.0 + jnp.exp(-x))


def _softplus(x):
    return jnp.maximum(x, 0.0) + jnp.log1p(jnp.exp(-jnp.abs(x)))


def _rms(x, w):
    return x * lax.rsqrt(jnp.mean(x * x, axis=-1, keepdims=True) + EPS) * w


def _dot(a, b):
    return jnp.dot(a.astype(BF16), b.astype(BF16), preferred_element_type=F32)


def _dot_nt(a, b):
    return lax.dot_general(a.astype(BF16), b.astype(BF16), (((1,), (1,)), ((), ())), preferred_element_type=F32)


def _dot_tn(a, b):
    return lax.dot_general(a.astype(BF16), b.astype(BF16), (((0,), (0,)), ((), ())), preferred_element_type=F32)


def _pack_pairs(x):
    c = x.shape[1] // 2
    hi = pltpu.bitcast(x[:, :c].astype(BF16).astype(F32), U32)
    lo = pltpu.bitcast(x[:, c:].astype(BF16).astype(F32), U32)
    return hi | lax.shift_right_logical(lo, jnp.uint32(16))


def _unpack_pairs(w):
    left = pltpu.bitcast(w & jnp.uint32(0xFFFF0000), F32)
    right = pltpu.bitcast(lax.shift_left(w, jnp.uint32(16)), F32)
    return left, right


def _inproj_kernel(x_ref, nw_ref, w_ref, mix_ref, gate_ref):
    h = _rms(x_ref[...], nw_ref[...]).astype(BF16)
    half = MIX_COLS // 2
    for c0 in (0, half):
        mix_ref[:, c0:c0 + half] = jnp.dot(h, w_ref[:, c0:c0 + half], preferred_element_type=F32)
    half = GATE_COLS // 2
    for c0 in (0, half):
        gate_ref[:, c0:c0 + half] = jnp.dot(h, w_ref[:, MIX_COLS + c0:MIX_COLS + c0 + half],
                                            preferred_element_type=F32)


def _inproj(x, norm_w, w_in_r, tm):
    n, d = x.shape
    return pl.pallas_call(
        _inproj_kernel,
        grid=(n // tm,),
        in_specs=[
            pl.BlockSpec((tm, d), lambda i: (i, 0)),
            pl.BlockSpec((1, d), lambda i: (0, 0)),
            pl.BlockSpec((d, IN_COLS), lambda i: (0, 0)),
        ],
        out_specs=[pl.BlockSpec((tm, MIX_COLS), lambda i: (i, 0)), pl.BlockSpec((tm, GATE_COLS), lambda i: (i, 0))],
        out_shape=[jax.ShapeDtypeStruct((n, MIX_COLS), F32), jax.ShapeDtypeStruct((n, GATE_COLS), F32)],
        compiler_params=pltpu.CompilerParams(dimension_semantics=("arbitrary",), vmem_limit_bytes=VMEM_LIMIT),
        name="inproj",
    )(x, norm_w, w_in_r)


def _conv_silu_qkv(ext_c, seq, tt, wconv_ref, ci, col0=0):
    base = CONV_ROWS - (CONV_WIDTH - 1)
    cs = slice(ci * LANES, (ci + 1) * LANES)
    cl = slice(ci * LANES - col0, (ci + 1) * LANES - col0)
    idx = (lambda r: (r, cl)) if seq is None else (lambda r: (seq, r, cl))
    acc = ext_c[idx(slice(base, base + tt))] * wconv_ref[0:1, cs]
    for j in range(1, CONV_WIDTH):
        acc = acc + ext_c[idx(slice(base + j, base + j + tt))] * wconv_ref[j:j + 1, cs]
    y = acc * _sigmoid(acc)
    if ci < 2 * N_HEADS:
        y = y * lax.rsqrt(jnp.sum(y * y, axis=-1, keepdims=True) + EPS)
    if ci < N_HEADS:
        y = y * (HEAD_DIM ** -0.5)
    return y


def _inproj_conv_kernel(x_ref, nw_ref, w_ref, wconv_ref, conv0_ref, mix_ref, gate_ref, nconv_ref, h_s, *ext, n_tiles):
    t = pl.program_id(1)
    tm = x_ref.shape[1]
    base = CONV_ROWS - (CONV_WIDTH - 1)
    step = CONV_DIM // len(ext)

    @pl.when(t == 0)
    def _():
        for g, buf in enumerate(ext):
            buf[base:CONV_ROWS, :] = conv0_ref[0, :, g * step:(g + 1) * step]

    h_s[...] = _rms(x_ref[0], nw_ref[...]).astype(BF16)

    def proj(c0, c1):
        return jnp.dot(h_s[...], w_ref[:, c0:c1], preferred_element_type=F32)

    def fill_z0():
        mix_ref[0, :, OFF_Z:OFF_Z + QK_DIM // 2] = proj(OFF_Z, OFF_Z + QK_DIM // 2)

    def fill_z1():
        mix_ref[0, :, OFF_Z + QK_DIM // 2:OFF_QKV] = proj(OFF_Z + QK_DIM // 2, OFF_QKV)

    def fill_pool():
        mix_ref[0, :, OFF_POOL:MIX_COLS] = proj(OFF_POOL, MIX_COLS)

    def fill_gate(i):
        q = GATE_COLS // 4
        gate_ref[0, :, i * q:(i + 1) * q] = proj(MIX_COLS + i * q, MIX_COLS + (i + 1) * q)

    fillers = [fill_z0, fill_z1, fill_pool] + [functools.partial(fill_gate, i) for i in range(4)]
    for g, buf in enumerate(ext):
        c0 = g * step
        buf[CONV_ROWS:CONV_ROWS + tm, :] = proj(OFF_QKV + c0, OFF_QKV + c0 + step)
        if fillers:
            fillers.pop(0)()
        for ci in range(c0 // LANES, (c0 + step) // LANES):
            mix_ref[0, :, OFF_QKV + ci * LANES:OFF_QKV + (ci + 1) * LANES] = _conv_silu_qkv(
                buf, None, tm, wconv_ref, ci, c0)
    for f in fillers:
        f()

    @pl.when(t == n_tiles - 1)
    def _():
        for g, buf in enumerate(ext):
            nconv_ref[0, :, g * step:(g + 1) * step] = buf[tm + base:tm + CONV_ROWS, :]

    for buf in ext:
        buf[0:CONV_ROWS, :] = buf[tm:tm + CONV_ROWS, :]


def _inproj_conv(x, conv0, norm_w, w_in_r, wconv, tm):
    n_seq, seq_len, d = x.shape
    n_tiles = seq_len // tm
    tile = lambda b, t: (b, t, 0)
    const = lambda b, t: (0, 0)
    mix, gates, nconv = pl.pallas_call(
        functools.partial(_inproj_conv_kernel, n_tiles=n_tiles),
        grid=(n_seq, n_tiles),
        in_specs=[
            pl.BlockSpec((1, tm, d), tile),
            pl.BlockSpec((1, d), const),
            pl.BlockSpec((d, IN_COLS), const),
            pl.BlockSpec((SUBLANES, CONV_DIM), const),
            pl.BlockSpec((1, CONV_WIDTH - 1, CONV_DIM), lambda b, t: (b, 0, 0)),
        ],
        out_specs=[
            pl.BlockSpec((1, tm, MIX_COLS), tile),
            pl.BlockSpec((1, tm, GATE_COLS), tile),
            pl.BlockSpec((1, CONV_WIDTH - 1, CONV_DIM), lambda b, t: (b, 0, 0)),
        ],
        out_shape=[
            jax.ShapeDtypeStruct((n_seq, seq_len, MIX_COLS), F32),
            jax.ShapeDtypeStruct((n_seq, seq_len, GATE_COLS), F32),
            jax.ShapeDtypeStruct((n_seq, CONV_WIDTH - 1, CONV_DIM), F32),
        ],
        scratch_shapes=[pltpu.VMEM((tm, d), BF16)] + [
            pltpu.VMEM((CONV_ROWS + tm, CONV_DIM // CONV_GROUPS), F32) for _ in range(CONV_GROUPS)],
        compiler_params=pltpu.CompilerParams(dimension_semantics=("arbitrary", "arbitrary"),
                                             vmem_limit_bytes=VMEM_LIMIT),
        name="inproj_conv",
    )(x, norm_w, w_in_r, wconv, conv0)
    n = n_seq * seq_len
    return mix.reshape(n, MIX_COLS), gates.reshape(n, GATE_COLS), nconv


def _mixer_kernel(proj_ref, pool0_ref, conv0_ref, s0_ref, wpool_ref, pscale_ref, wconv_ref, alog_ref, dtb_ref, onw_ref,
                  ypool_ref, ydelta_ref, npool_ref, nconv_ref, ns_ref,
                  ext_p, ext_c, q_s, k_s, v_s, g_s, b_s, state,
                  *, sb, tt, tv, chunk, pos0, n_tiles, qkv_done):
    t = pl.program_id(1)
    base = CONV_ROWS - (CONV_WIDTH - 1)

    @pl.when(t == 0)
    def _():
        ext_p[:, POOL_ROWS - POOL_BUF:POOL_ROWS, :] = pool0_ref[...]
        if not qkv_done:
            ext_c[:, base:CONV_ROWS, :] = conv0_ref[...]
        state[...] = s0_ref[...]

    row = lax.broadcasted_iota(jnp.int32, (tt, 1), 0)
    pos = pos0 + t * tt + row
    for s in range(sb):
        ext_p[s, POOL_ROWS:POOL_ROWS + tt, :] = proj_ref[s, :, OFF_POOL:OFF_POOL + POOL_DIM]
        if not qkv_done:
            ext_c[s, CONV_ROWS:CONV_ROWS + tt, :] = proj_ref[s, :, OFF_QKV:OFF_QKV + CONV_DIM]

        for gi, win in enumerate(POOL_WINDOWS):
            cs = slice(gi * POOL_GROUP_DIM, (gi + 1) * POOL_GROUP_DIM)
            u = ext_p[s, POOL_ROWS:POOL_ROWS + tt, cs]
            acc = u
            for j in range(1, win):
                acc = acc + ext_p[s, POOL_ROWS - j:POOL_ROWS - j + tt, cs]
            cnt = jnp.minimum(win, pos + 1).astype(F32)
            pooled = acc / cnt - u
            ypool_ref[s, :, cs] = _dot(pooled, wpool_ref[gi]) * pscale_ref[:, cs]

        for ci in range(0 if qkv_done else CONV_DIM // LANES):
            hs = slice((ci % N_HEADS) * HEAD_DIM, (ci % N_HEADS + 1) * HEAD_DIM)
            (q_s, k_s, v_s)[ci // N_HEADS][s, :, hs] = _conv_silu_qkv(ext_c, s, tt, wconv_ref, ci)

        beta = _sigmoid(proj_ref[s, :, OFF_B:OFF_B + LANES])
        g = -jnp.exp(alog_ref[...]) * _softplus(proj_ref[s, :, OFF_A:OFF_A + LANES] + dtb_ref[...])
        if tv < tt:
            beta = jnp.where(row < tv, beta, 0.0)
            g = jnp.where(row < tv, g, 0.0)
        b_s[s] = beta
        g_s[s] = g

    @pl.when(t == n_tiles - 1)
    def _():
        npool_ref[...] = ext_p[:, tv + POOL_ROWS - POOL_BUF:tv + POOL_ROWS, :]
        if qkv_done:
            nconv_ref[...] = jnp.zeros_like(nconv_ref)
        else:
            nconv_ref[...] = ext_c[:, tv + base:tv + CONV_ROWS, :]

    if n_tiles > 1:
        ext_p[:, 0:POOL_ROWS, :] = ext_p[:, tt:tt + POOL_ROWS, :]
        if not qkv_done:
            ext_c[:, 0:CONV_ROWS, :] = ext_c[:, tt:tt + CONV_ROWS, :]

    ri = lax.broadcasted_iota(jnp.int32, (chunk, chunk), 0)
    cj = lax.broadcasted_iota(jnp.int32, (chunk, chunk), 1)
    causal = ri >= cj
    strict = ri > cj
    ltri = causal.astype(F32)
    eye = (ri == cj).astype(F32)
    n_doublings = chunk.bit_length() - 2
    units = [(s, h) for s in range(sb) for h in range(N_HEADS)]
    hsl = [slice(h * HEAD_DIM, (h + 1) * HEAD_DIM) for h in range(N_HEADS)]

    def qkv(which, s, rows, h):
        if qkv_done:
            c0 = OFF_QKV + which * QK_DIM + h * HEAD_DIM
            return proj_ref[s, rows, c0:c0 + HEAD_DIM]
        return (q_s, k_s, v_s)[which][s, rows, hsl[h]]

    def chunk_body(ci, carry):
        r0 = pl.multiple_of(ci * chunk, chunk)
        rows = pl.ds(r0, chunk)
        g_all, g_t, b_all = [], [], []
        for s in range(sb):
            ga = jnp.dot(ltri, g_s[s, rows, :], precision=HIGHEST, preferred_element_type=F32)
            g_all.append(ga)
            if chunk < LANES:
                ga = jnp.concatenate([ga, jnp.zeros((LANES - chunk, LANES), F32)], axis=0)
            g_t.append(ga.T)
            b_all.append(b_s[s, rows, :])
        g_col = [g_all[s][:, h:h + 1] for s, h in units]
        dec = [jnp.exp(jnp.minimum(g_col[i] - g_t[s][h:h + 1, 0:chunk], 0.0)) for i, (s, h) in enumerate(units)]
        b_col = [b_all[s][:, h:h + 1] for s, h in units]
        q = [qkv(0, s, rows, h) for s, h in units]
        k = [qkv(1, s, rows, h) for s, h in units]
        idx = range(len(units))
        kb = [k[i] * b_col[i] for i in idx]
        kbk = [_dot_nt(kb[i], k[i]) for i in idx]
        qk = [_dot_nt(q[i], k[i]) for i in idx]
        p_mat = [jnp.where(strict, kbk[i] * dec[i], 0.0) for i in idx]
        t_inv = [eye - p_mat[i] for i in idx]
        for _ in range(n_doublings):
            p_mat = [_dot(p_mat[i], p_mat[i]) for i in idx]
            xp = [_dot(t_inv[i], p_mat[i]) for i in idx]
            t_inv = [t_inv[i] + xp[i] for i in idx]
        g_exp = [jnp.exp(g_col[i]) for i in idx]
        wu = [_dot(t_inv[i], jnp.concatenate([kb[i] * g_exp[i], qkv(2, s, rows, h) * b_col[i]], axis=1))
              for i, (s, h) in enumerate(units)]
        s_old = [state[s, h] for s, h in units]
        res = [_dot(jnp.concatenate([wu[i][:, 0:HEAD_DIM], q[i] * g_exp[i]], axis=0), s_old[i]) for i in idx]
        v_new = [wu[i][:, HEAD_DIM:2 * HEAD_DIM] - res[i][0:chunk] for i in idx]
        a_qk = [jnp.where(causal, qk[i] * dec[i], 0.0) for i in idx]
        g_last = [g_all[s][chunk - 1:chunk, h:h + 1] for s, h in units]
        k_dec = [k[i] * jnp.exp(g_last[i] - g_col[i]) for i in idx]
        intra = [_dot(a_qk[i], v_new[i]) for i in idx]
        upd = [_dot_tn(k_dec[i], v_new[i]) for i in idx]
        for i, (s, h) in enumerate(units):
            state[s, h] = s_old[i] * jnp.exp(g_last[i]) + upd[i]
            o_c = res[i][chunk:2 * chunk] + intra[i]
            z_c = proj_ref[s, rows, OFF_Z + h * HEAD_DIM:OFF_Z + (h + 1) * HEAD_DIM]
            ydelta_ref[s, rows, hsl[h]] = _rms(o_c, onw_ref[...]) * (z_c * _sigmoid(z_c))
        return carry

    lax.fori_loop(0, tt // chunk, chunk_body, 0)

    @pl.when(t == n_tiles - 1)
    def _():
        ns_ref[...] = state[...]


def _mixers(proj, pool0, conv0, s0, wpool, pscale, wconv, alog, dtb, onw, *, n_seq, sb, tt, tv, chunk, pos0, n_tiles,
            qkv_done):
    kern = functools.partial(_mixer_kernel, sb=sb, tt=tt, tv=tv, chunk=chunk, pos0=pos0, n_tiles=n_tiles,
                             qkv_done=qkv_done)
    seq_len = n_tiles * tt
    tile = lambda b, t: (b, t, 0)
    seq3 = lambda b, t: (b, 0, 0)
    seq4 = lambda b, t: (b, 0, 0, 0)
    const2 = lambda b, t: (0, 0)
    ypool, ydelta, npool, nconv, ns = pl.pallas_call(
        kern,
        grid=(n_seq // sb, n_tiles),
        in_specs=[
            pl.BlockSpec((sb, tt, MIX_COLS), tile),
            pl.BlockSpec((sb, POOL_BUF, POOL_DIM), seq3),
            pl.BlockSpec((sb, CONV_WIDTH - 1, CONV_DIM), seq3),
            pl.BlockSpec((sb, N_HEADS, HEAD_DIM, HEAD_DIM), seq4),
            pl.BlockSpec((len(POOL_WINDOWS), POOL_GROUP_DIM, POOL_GROUP_DIM), lambda b, t: (0, 0, 0)),
            pl.BlockSpec((1, POOL_DIM), const2),
            pl.BlockSpec((SUBLANES, CONV_DIM), const2),
            pl.BlockSpec((1, LANES), const2),
            pl.BlockSpec((1, LANES), const2),
            pl.BlockSpec((1, HEAD_DIM), const2),
        ],
        out_specs=[
            pl.BlockSpec((sb, tt, POOL_DIM), tile),
            pl.BlockSpec((sb, tt, QK_DIM), tile),
            pl.BlockSpec((sb, POOL_BUF, POOL_DIM), seq3),
            pl.BlockSpec((sb, CONV_WIDTH - 1, CONV_DIM), seq3),
            pl.BlockSpec((sb, N_HEADS, HEAD_DIM, HEAD_DIM), seq4),
        ],
        out_shape=[
            jax.ShapeDtypeStruct((n_seq, seq_len, POOL_DIM), F32),
            jax.ShapeDtypeStruct((n_seq, seq_len, QK_DIM), F32),
            jax.ShapeDtypeStruct((n_seq, POOL_BUF, POOL_DIM), F32),
            jax.ShapeDtypeStruct((n_seq, CONV_WIDTH - 1, CONV_DIM), F32),
            jax.ShapeDtypeStruct((n_seq, N_HEADS, HEAD_DIM, HEAD_DIM), F32),
        ],
        scratch_shapes=[
            pltpu.VMEM((sb, POOL_ROWS + tt, POOL_DIM), F32),
            pltpu.VMEM((sb, CONV_ROWS + tt, CONV_DIM), F32),
            pltpu.VMEM((sb, tt, QK_DIM), F32),
            pltpu.VMEM((sb, tt, QK_DIM), F32),
            pltpu.VMEM((sb, tt, QK_DIM), F32),
            pltpu.VMEM((sb, tt, LANES), F32),
            pltpu.VMEM((sb, tt, LANES), F32),
            pltpu.VMEM((sb, N_HEADS, HEAD_DIM, HEAD_DIM), F32),
        ],
        compiler_params=pltpu.CompilerParams(dimension_semantics=("arbitrary", "arbitrary"),
                                             vmem_limit_bytes=VMEM_LIMIT),
        name="mixers",
    )(proj.reshape(n_seq, seq_len, MIX_COLS), pool0, conv0, s0, wpool, pscale, wconv, alog, dtb, onw)
    n = n_seq * seq_len
    return ypool.reshape(n, POOL_DIM), ydelta.reshape(n, QK_DIM), npool, nconv, ns


def _merge_kernel(x_ref, gates_ref, ypool_ref, ydelta_ref, wbp_ref, wbd_ref, wout_ref, n2w_ref, wr_hi_ref, wr_lo_ref,
                  br_ref, xm_ref, h2_ref, route_ref, wk_ref, cnt_ref, cnt_s, *, period, valid):
    tm = x_ref.shape[0]

    @pl.when(pl.program_id(0) == 0)
    def _():
        cnt_s[...] = jnp.zeros_like(cnt_s)

    bp = _dot(ypool_ref[...], wbp_ref[...])
    bd = _dot(ydelta_ref[...], wbd_ref[...])
    merged = _sigmoid(gates_ref[:, 0:1024]) * bp + _sigmoid(gates_ref[:, 1024:2048]) * bd
    xm = x_ref[...] + _dot(merged, wout_ref[...])
    xm_ref[...] = xm
    h2 = _rms(xm, n2w_ref[...])
    h2_ref[...] = _pack_pairs(h2)

    h2_hi = h2.astype(BF16)
    h2_lo = (h2 - h2_hi.astype(F32)).astype(BF16)
    logits = (jnp.dot(h2_hi, wr_hi_ref[...], preferred_element_type=F32)
              + (jnp.dot(h2_hi, wr_lo_ref[...], preferred_element_type=F32)
                 + jnp.dot(h2_lo, wr_hi_ref[...], preferred_element_type=F32))) + br_ref[...]
    lane = lax.broadcasted_iota(jnp.int32, logits.shape, 1)
    neg = -jnp.inf
    far = LANES - 1
    is_g = lane < N_GROUPS
    g_max = jnp.max(jnp.where(is_g, logits, neg), axis=-1, keepdims=True)
    g_sel = jnp.min(jnp.where(is_g & (logits == g_max), lane, far), axis=-1, keepdims=True)
    p_g = 1.0 / jnp.sum(jnp.where(is_g, jnp.exp(logits - g_max), 0.0), axis=-1, keepdims=True)
    e_lane = lane - N_GROUPS
    is_e = (e_lane >= 0) & (e_lane < N_EXPERTS) & ((e_lane // EXPERTS_PER_GROUP) == g_sel)
    ev = jnp.where(is_e, logits, neg)
    v1 = jnp.max(ev, axis=-1, keepdims=True)
    i1 = jnp.min(jnp.where(is_e & (ev == v1), lane, far), axis=-1, keepdims=True)
    is_e2 = is_e & (lane != i1)
    ev2 = jnp.where(is_e2, logits, neg)
    v2 = jnp.max(ev2, axis=-1, keepdims=True)
    i2 = jnp.min(jnp.where(is_e2 & (ev2 == v2), lane, far), axis=-1, keepdims=True)
    e21 = jnp.exp(v2 - v1)
    w1 = p_g / (1.0 + e21)
    w2 = p_g * e21 / (1.0 + e21)
    e1 = i1 - N_GROUPS
    e2 = i2 - N_GROUPS
    chosen = (lane == e1) | (lane == e2)
    if valid < period:
        row = lax.broadcasted_iota(jnp.int32, (tm, 1), 0)
        chosen = chosen & (lax.rem(row, period) < valid)
    onehot = jnp.where(chosen, 1.0, 0.0)
    rr = lax.broadcasted_iota(jnp.int32, (tm, tm), 0)
    cc = lax.broadcasted_iota(jnp.int32, (tm, tm), 1)
    earlier = jnp.where(rr > cc, 1.0, 0.0).astype(BF16)
    before = jnp.dot(earlier, onehot.astype(BF16), preferred_element_type=F32) + cnt_s[...]
    r1 = jnp.sum(jnp.where(lane == e1, before, 0.0), axis=-1, keepdims=True)
    r2 = jnp.sum(jnp.where(lane == e2, before, 0.0), axis=-1, keepdims=True)
    cnt_s[...] = cnt_s[...] + jnp.sum(onehot, axis=0, keepdims=True)
    cnt_ref[...] = cnt_s[...]
    record = jnp.where(lane == 0, e1.astype(F32), jnp.where(lane == 1, e2.astype(F32),
                                                           jnp.where(lane == 2, r1, jnp.where(lane == 3, r2, 0.0))))
    route_ref[...] = record.T[0:SUBLANES, :]
    lane8 = lax.broadcasted_iota(jnp.int32, (tm, SUBLANES), 1)
    wk_ref[...] = jnp.where(lane8 == 0, w1, jnp.where(lane8 == 1, w2, 0.0))


def _merge(x, proj, ypool, ydelta, wbp, wbd, wout, n2w, wr_hi, wr_lo, br, tm, period, valid):
    n, d = x.shape
    row = lambda i: (i, 0)
    const = lambda i: (0, 0)
    return pl.pallas_call(
        functools.partial(_merge_kernel, period=period, valid=valid),
        grid=(n // tm,),
        in_specs=[
            pl.BlockSpec((tm, d), row),
            pl.BlockSpec((tm, GATE_COLS), row),
            pl.BlockSpec((tm, POOL_DIM), row),
            pl.BlockSpec((tm, QK_DIM), row),
            pl.BlockSpec((POOL_DIM, d), const),
            pl.BlockSpec((QK_DIM, d), const),
            pl.BlockSpec((d, d), const),
            pl.BlockSpec((1, d), const),
            pl.BlockSpec((d, LANES), const),
            pl.BlockSpec((d, LANES), const),
            pl.BlockSpec((1, LANES), const),
        ],
        out_specs=[
            pl.BlockSpec((tm, d), row),
            pl.BlockSpec((tm, d // 2), row),
            pl.BlockSpec((SUBLANES, tm), lambda i: (0, i)),
            pl.BlockSpec((tm, SUBLANES), row),
            pl.BlockSpec((1, LANES), const),
        ],
        out_shape=[
            jax.ShapeDtypeStruct((n, d), F32),
            jax.ShapeDtypeStruct((n, d // 2), U32),
            jax.ShapeDtypeStruct((SUBLANES, n), F32),
            jax.ShapeDtypeStruct((n, SUBLANES), F32),
            jax.ShapeDtypeStruct((1, LANES), F32),
        ],
        scratch_shapes=[pltpu.VMEM((1, LANES), F32)],
        compiler_params=pltpu.CompilerParams(dimension_semantics=("arbitrary",), vmem_limit_bytes=VMEM_LIMIT),
        name="merge",
    )(x, proj, ypool, ydelta, wbp, wbd, wout, n2w, wr_hi, wr_lo, br)


def _expert_kernel(te_ref, nv_ref, x_ref, wg_ref, wu_ref, wd_ref, y_ref, wg_s, wu_s, wd_s):
    i = pl.program_id(0)
    n_valid = nv_ref[i]

    @pl.when((i == 0) | (te_ref[i] != te_ref[jnp.maximum(i - 1, 0)]))
    def _():
        wg_s[...] = wg_ref[0].astype(BF16)
        wu_s[...] = wu_ref[0].astype(BF16)
        wd_s[...] = wd_ref[0].astype(BF16)

    @pl.when(n_valid > 0)
    def _():
        row = lax.broadcasted_iota(jnp.int32, (x_ref.shape[0], 1), 0)
        left, right = _unpack_pairs(jnp.where(row < n_valid, x_ref[...], jnp.uint32(0)))
        x = jnp.concatenate([left.astype(BF16), right.astype(BF16)], axis=1)
        a = _dot(x, wg_s[...])
        b = _dot(x, wu_s[...])
        y_ref[...] = _pack_pairs(_dot((a * _sigmoid(a)) * b, wd_s[...]))

    @pl.when(n_valid == 0)
    def _():
        y_ref[...] = jnp.zeros_like(y_ref)


def _experts(tile_expert, tile_valid, xs, wg, wu, wd, tm):
    n_rows, dp = xs.shape
    d, f = wg.shape[1], wg.shape[2]
    grid_spec = pltpu.PrefetchScalarGridSpec(
        num_scalar_prefetch=2,
        grid=(n_rows // tm,),
        in_specs=[
            pl.BlockSpec((tm, dp), lambda i, te, nv: (jnp.where(nv[i] > 0, i, 0), 0)),
            pl.BlockSpec((1, d, f), lambda i, te, nv: (te[i], 0, 0)),
            pl.BlockSpec((1, d, f), lambda i, te, nv: (te[i], 0, 0)),
            pl.BlockSpec((1, f, d), lambda i, te, nv: (te[i], 0, 0)),
        ],
        out_specs=pl.BlockSpec((tm, dp), lambda i, te, nv: (i, 0)),
        scratch_shapes=[pltpu.VMEM((d, f), BF16), pltpu.VMEM((d, f), BF16), pltpu.VMEM((f, d), BF16)],
    )
    return pl.pallas_call(
        _expert_kernel,
        grid_spec=grid_spec,
        out_shape=jax.ShapeDtypeStruct((n_rows, dp), U32),
        compiler_params=pltpu.CompilerParams(dimension_semantics=("arbitrary",), vmem_limit_bytes=VMEM_LIMIT),
        name="experts",
    )(tile_expert, tile_valid, xs, wg, wu, wd)


def _plan_kernel(route_ref, cnt_ref, pos_ref, tiles_ref, *, tm_e):
    counts = cnt_ref[...]
    padded = jnp.floor((counts + (tm_e - 1)) / tm_e) * tm_e
    lane = lax.broadcasted_iota(jnp.int32, counts.shape, 1)
    ends = padded
    shift = 1
    while shift < N_EXPERTS:
        ends = ends + jnp.where(lane >= shift, pltpu.roll(ends, shift, axis=1), 0.0)
        shift *= 2
    starts = ends - padded

    def lookup(table, key):
        out = jnp.zeros_like(key)
        for e in range(N_EXPERTS):
            out = out + jnp.where(key == e, table[:, e:e + 1], 0.0)
        return out

    for k in range(2):
        pos = lookup(starts, route_ref[k:k + 1, :]) + route_ref[2 + k:3 + k, :]
        pos_ref[k:k + 1, :] = pos.astype(jnp.int32)

    tile_start = lax.broadcasted_iota(jnp.int32, (1, tiles_ref.shape[1]), 1).astype(F32) * tm_e
    tile_expert = jnp.zeros_like(tile_start)
    for e in range(N_EXPERTS):
        tile_expert = tile_expert + jnp.where(ends[:, e:e + 1] <= tile_start, 1.0, 0.0)
    tile_expert = jnp.minimum(tile_expert, N_EXPERTS - 1.0)
    valid = lookup(counts, tile_expert) - (tile_start - lookup(starts, tile_expert))
    tiles_ref[0:1, :] = tile_expert.astype(jnp.int32)
    tiles_ref[1:2, :] = jnp.clip(valid, 0.0, tm_e).astype(jnp.int32)


def _plan(route_t, counts, tm_e, n_tiles):
    n = route_t.shape[1]
    tiles_pad = -(-n_tiles // LANES) * LANES
    return pl.pallas_call(
        functools.partial(_plan_kernel, tm_e=tm_e),
        out_shape=[jax.ShapeDtypeStruct((2, n), jnp.int32), jax.ShapeDtypeStruct((2, tiles_pad), jnp.int32)],
        name="plan",
    )(route_t, counts)


def _sc_workers():
    info = plsc.get_sparse_core_info()
    return info.num_cores, info.num_subcores


def _sc_chunk(per_worker):
    assert per_worker % SUBLANES == 0
    return max(c for c in range(SUBLANES, SC_ROWS + 1, SUBLANES) if per_worker % c == 0)


def _sc_scatter2(rows, idx_a, idx_b, n_out):
    n_cores, n_sub = _sc_workers()
    n, d = rows.shape
    per_worker = n // (n_cores * n_sub)
    assert per_worker * n_cores * n_sub == n
    chunk = _sc_chunk(per_worker)
    mesh = plsc.VectorSubcoreMesh(core_axis_name="c", subcore_axis_name="s")

    n_chunks = per_worker // chunk
    buf = lambda shape, dtype: [pltpu.VMEM(shape, dtype), pltpu.VMEM(shape, dtype)]

    @functools.partial(
        pl.kernel, mesh=mesh, out_type=jax.ShapeDtypeStruct((n_out, d), rows.dtype),
        scratch_types=buf((chunk,), jnp.int32) + buf((chunk,), jnp.int32) + buf((chunk, d), rows.dtype)
        + [pltpu.SemaphoreType.DMA, pltpu.SemaphoreType.DMA])
    def scatter_rows(rows_hbm, ia_hbm, ib_hbm, out_hbm, ia0, ia1, ib0, ib1, r0, r1, sem0, sem1):
        worker = lax.axis_index("s") * n_cores + lax.axis_index("c")
        base = worker * per_worker
        slots = ((ia0, ib0, r0, sem0), (ia1, ib1, r1, sem1))
        pending = {}
        for j in range(n_chunks):
            ia_v, ib_v, rows_v, sem = slots[j % 2]
            if j >= 2:
                for cp in pending.pop(j - 2):
                    cp.wait()
            off = pl.multiple_of(base + j * chunk, SUBLANES)
            pltpu.sync_copy(ia_hbm.at[pl.ds(off, chunk)], ia_v)
            pltpu.sync_copy(ib_hbm.at[pl.ds(off, chunk)], ib_v)
            pltpu.sync_copy(rows_hbm.at[pl.ds(off, chunk)], rows_v)
            pending[j] = (pltpu.async_copy(rows_v, out_hbm.at[ia_v], sem),
                          pltpu.async_copy(rows_v, out_hbm.at[ib_v], sem))
        for j in sorted(pending):
            for cp in pending[j]:
                cp.wait()

    return scatter_rows(rows, idx_a, idx_b)


def _sc_gather(table, idx):
    n_cores, n_sub = _sc_workers()
    n_idx = idx.shape[0]
    d = table.shape[1]
    per_worker = n_idx // (n_cores * n_sub)
    assert per_worker * n_cores * n_sub == n_idx
    chunk = _sc_chunk(per_worker)
    mesh = plsc.VectorSubcoreMesh(core_axis_name="c", subcore_axis_name="s")

    n_chunks = per_worker // chunk

    @functools.partial(
        pl.kernel, mesh=mesh, out_type=jax.ShapeDtypeStruct((n_idx, d), table.dtype),
        scratch_types=[pltpu.VMEM((chunk,), jnp.int32), pltpu.VMEM((chunk,), jnp.int32),
                       pltpu.VMEM((chunk, d), table.dtype), pltpu.VMEM((chunk, d), table.dtype),
                       pltpu.SemaphoreType.DMA, pltpu.SemaphoreType.DMA])
    def gather_rows(table_hbm, idx_hbm, out_hbm, i0, i1, r0, r1, sem0, sem1):
        worker = lax.axis_index("s") * n_cores + lax.axis_index("c")
        base = worker * per_worker
        slots = ((i0, r0, sem0), (i1, r1, sem1))

        def start(j):
            idx_v, rows_v, sem = slots[j % 2]
            off = pl.multiple_of(base + j * chunk, SUBLANES)
            pltpu.sync_copy(idx_hbm.at[pl.ds(off, chunk)], idx_v)
            return pltpu.async_copy(table_hbm.at[idx_v], rows_v, sem)

        nxt = start(0)
        for j in range(n_chunks):
            cur = nxt
            if j + 1 < n_chunks:
                nxt = start(j + 1)
            cur.wait()
            off = pl.multiple_of(base + j * chunk, SUBLANES)
            pltpu.sync_copy(slots[j % 2][1], out_hbm.at[pl.ds(off, chunk)])

    return gather_rows(table, idx)


def _finalize_kernel(xm_ref, ya_ref, yb_ref, wk_ref, fnw_ref, o_ref):
    d = xm_ref.shape[1]
    c = d // 2
    w = wk_ref[...]
    a_left, a_right = _unpack_pairs(ya_ref[...])
    b_left, b_right = _unpack_pairs(yb_ref[...])
    x_left = xm_ref[:, 0:c] + (w[:, 0:1] * a_left + w[:, 1:2] * b_left)
    x_right = xm_ref[:, c:d] + (w[:, 0:1] * a_right + w[:, 1:2] * b_right)
    ms = (jnp.sum(x_left * x_left, axis=-1, keepdims=True) + jnp.sum(x_right * x_right, axis=-1, keepdims=True)) / d
    scale = lax.rsqrt(ms + EPS)
    o_ref[:, 0:c] = x_left * scale * fnw_ref[:, 0:c]
    o_ref[:, c:d] = x_right * scale * fnw_ref[:, c:d]


def _finalize(xm, yk, wk, fnw, tm):
    n, d = xm.shape
    steps = n // tm
    return pl.pallas_call(
        _finalize_kernel,
        grid=(steps,),
        in_specs=[
            pl.BlockSpec((tm, d), lambda i: (i, 0)),
            pl.BlockSpec((tm, d // 2), lambda i: (i, 0)),
            pl.BlockSpec((tm, d // 2), lambda i: (i + steps, 0)),
            pl.BlockSpec((tm, SUBLANES), lambda i: (i, 0)),
            pl.BlockSpec((1, d), lambda i: (0, 0)),
        ],
        out_specs=pl.BlockSpec((tm, d), lambda i: (i, 0)),
        out_shape=jax.ShapeDtypeStruct((n, d), F32),
        compiler_params=pltpu.CompilerParams(dimension_semantics=("arbitrary",), vmem_limit_bytes=VMEM_LIMIT),
        name="finalize",
    )(xm, yk, yk, wk, fnw)


def _moe(xm, h2, route, wk, counts, wg, wu, wd, fnw, *, tm_e, tm_c):
    n = xm.shape[0]
    n_rows = 2 * n + N_EXPERTS * tm_e
    n_tiles = n_rows // tm_e
    pos, tiles = _plan(route, counts, tm_e, n_tiles)
    xs = _sc_scatter2(h2, pos[0], pos[1], n_rows)
    y_sorted = _experts(tiles[0, :n_tiles], tiles[1, :n_tiles], xs, wg, wu, wd, tm_e)
    yk = _sc_gather(y_sorted, pos.reshape(-1))
    return _finalize(xm, yk, wk, fnw, tm_c)


def _layer(x_tokens, pool0, conv0, s0, prm, *, n_seq, sb, tt, tv, chunk, pos0, n_tiles, tm):
    conv_in_proj = tv == tt
    if conv_in_proj:
        proj, gates, nconv_proj = _inproj_conv(x_tokens.reshape(n_seq, n_tiles * tt, -1), conv0, prm["norm1_w"],
                                               prm["w_in"], prm["w_conv"], tm)
    else:
        proj, gates = _inproj(x_tokens, prm["norm1_w"], prm["w_in"], tm)
    ypool, ydelta, npool, nconv, ns = _mixers(
        proj, pool0, conv0, s0, prm["w_pool"], prm["pool_scale"], prm["w_conv"], prm["a_log"], prm["dt_bias"],
        prm["o_norm_w"], n_seq=n_seq, sb=sb, tt=tt, tv=tv, chunk=chunk, pos0=pos0, n_tiles=n_tiles,
        qkv_done=conv_in_proj)
    if conv_in_proj:
        nconv = nconv_proj
    xm, h2, route, wk, counts = _merge(
        x_tokens, gates, ypool, ydelta, prm["w_branch_pool"], prm["w_branch_delta"], prm["w_out"], prm["norm2_w"],
        prm["w_router_hi"], prm["w_router_lo"], prm["b_router"], 2 * tm, tt, tv)
    return xm, h2, route, wk, counts, npool, nconv, ns


def _pad_lanes(v, width=LANES):
    v = v.reshape(1, -1).astype(F32)
    return jnp.pad(v, ((0, 0), (0, width - v.shape[1])))


def kernel(x_prompt, x_sample, cache_pool, cache_conv, state_delta, norm1_w, w_in, w_pool, pool_scale, w_conv, a_log, dt_bias, o_norm_w, w_branch_pool, w_branch_delta, w_out, norm2_w, w_router_group, b_router_group, w_router_expert, b_router_expert, w_gate, w_up, w_down, final_norm_w):
    n_layers = norm1_w.shape[0]
    assert n_layers == 1, "single-layer step"
    bsz, seq, d = x_prompt.shape
    dbs, dseq, _ = x_sample.shape
    past_len = 16384
    lyr = 0

    wi = w_in[lyr]
    c = [POOL_DIM, POOL_DIM + CONV_DIM]
    c += [c[1] + N_HEADS, c[1] + 2 * N_HEADS]
    c += [c[3] + QK_DIM, c[3] + QK_DIM + d]
    pad8 = lambda m: jnp.pad(m, ((0, 0), (0, LANES - N_HEADS)))
    w_in_r = jnp.concatenate(
        [wi[:, c[3]:c[4]], wi[:, c[0]:c[1]], wi[:, :c[0]], pad8(wi[:, c[1]:c[2]]), pad8(wi[:, c[2]:c[3]]),
         wi[:, c[4]:c[5]], wi[:, c[5]:]], axis=1).astype(BF16)
    w_router = jnp.concatenate(
        [w_router_group[lyr], w_router_expert[lyr].reshape(d, N_EXPERTS),
         jnp.zeros((d, LANES - N_GROUPS - N_EXPERTS), F32)], axis=1).astype(F32)
    w_router_hi = w_router.astype(BF16)
    b_router = _pad_lanes(jnp.concatenate([b_router_group[lyr], b_router_expert[lyr].reshape(-1)]))
    prm = dict(
        norm1_w=norm1_w[lyr].reshape(1, d), w_in=w_in_r,
        w_pool=w_pool[lyr].astype(BF16), pool_scale=pool_scale[lyr].reshape(1, POOL_DIM),
        w_conv=jnp.pad(w_conv[lyr], ((0, SUBLANES - CONV_WIDTH), (0, 0))),
        a_log=_pad_lanes(a_log[lyr]), dt_bias=_pad_lanes(dt_bias[lyr]), o_norm_w=o_norm_w[lyr].reshape(1, HEAD_DIM),
        w_branch_pool=w_branch_pool[lyr].astype(BF16), w_branch_delta=w_branch_delta[lyr].astype(BF16),
        w_out=w_out[lyr].astype(BF16), norm2_w=norm2_w[lyr].reshape(1, d),
        w_router_hi=w_router_hi, w_router_lo=(w_router - w_router_hi.astype(F32)).astype(BF16), b_router=b_router,
    )
    wg, wu, wd = w_gate[lyr], w_up[lyr], w_down[lyr]
    fnw = final_norm_w.reshape(1, d)

    tt_p = 128
    xm, h2, route, wk, counts, npool, nconv, ns = _layer(
        x_prompt.reshape(bsz * seq, d),
        jnp.zeros((bsz, POOL_BUF, POOL_DIM), F32), jnp.zeros((bsz, CONV_WIDTH - 1, CONV_DIM), F32),
        jnp.zeros((bsz, N_HEADS, HEAD_DIM, HEAD_DIM), F32), prm,
        n_seq=bsz, sb=4, tt=tt_p, tv=tt_p, chunk=64, pos0=0, n_tiles=seq // tt_p, tm=256)
    y_prompt = _moe(xm, h2, route, wk, counts, wg, wu, wd, fnw, tm_e=256, tm_c=256).reshape(bsz, seq, d)
    pool_p = npool[None]
    conv_p = nconv[None]
    delta_p = ns[None]

    tt_s = SUBLANES
    xs_pad = jnp.pad(x_sample, ((0, 0), (0, tt_s - dseq), (0, 0))).reshape(dbs * tt_s, d)
    xm, h2, route, wk, counts, npool, nconv, ns = _layer(
        xs_pad, cache_pool[lyr], cache_conv[lyr], state_delta[lyr], prm,
        n_seq=dbs, sb=8, tt=tt_s, tv=dseq, chunk=tt_s, pos0=past_len, n_tiles=1, tm=256)
    real = lambda a: a.reshape(dbs, tt_s, -1)[:, :dseq].reshape(dbs * dseq, -1)
    route = route.reshape(SUBLANES, dbs, tt_s)[:, :, :dseq].reshape(SUBLANES, dbs * dseq)
    y_sample = _moe(real(xm), real(h2), route, real(wk), counts, wg, wu, wd, fnw,
                    tm_e=128, tm_c=256).reshape(dbs, dseq, d)
    pool_s = npool[None]
    conv_s = nconv[None]
    delta_s = ns[None]
    return (y_prompt, y_sample, pool_p, conv_p, delta_p, pool_s, conv_s, delta_s)
```

```python
import functools

import jax
import jax.numpy as jnp
from jax import lax
from jax.experimental import pallas as pl
from jax.experimental.pallas import tpu as pltpu
from jax.experimental.pallas import tpu_sc as plsc

F32 = jnp.float32
BF16 = jnp.bfloat16
U32 = jnp.uint32
EPS = 1e-6
HIGHEST = lax.Precision.HIGHEST

POOL_WINDOWS = (2, 4, 8, 16)
POOL_GROUP_DIM = 128
POOL_DIM = len(POOL_WINDOWS) * POOL_GROUP_DIM
POOL_BUF = max(POOL_WINDOWS) - 1
N_HEADS = 8
HEAD_DIM = 128
QK_DIM = N_HEADS * HEAD_DIM
CONV_WIDTH = 4
CONV_DIM = 3 * QK_DIM
N_GROUPS = 4
EXPERTS_PER_GROUP = 8
N_EXPERTS = N_GROUPS * EXPERTS_PER_GROUP
LANES = 128
SUBLANES = 8

OFF_Z = 0
OFF_QKV = OFF_Z + QK_DIM
OFF_POOL = OFF_QKV + CONV_DIM
OFF_B = OFF_POOL + POOL_DIM
OFF_A = OFF_B + LANES
MIX_COLS = OFF_A + LANES
GATE_COLS = 2048
IN_COLS = MIX_COLS + GATE_COLS

POOL_ROWS = 16
CONV_ROWS = 8
VMEM_LIMIT = 60 * 1024 * 1024
CONV_GROUPS = 6
SC_ROWS = 96


def _sigmoid(x):
    return 1.0 / (1.0 + jnp.exp(-x))


def _softplus(x):
    return jnp.maximum(x, 0.0) + jnp.log1p(jnp.exp(-jnp.abs(x)))


def _rms(x, w):
    return x * lax.rsqrt(jnp.mean(x * x, axis=-1, keepdims=True) + EPS) * w


def _dot(a, b):
    return jnp.dot(a.astype(BF16), b.astype(BF16), preferred_element_type=F32)


def _dot_nt(a, b):
    return lax.dot_general(a.astype(BF16), b.astype(BF16), (((1,), (1,)), ((), ())), preferred_element_type=F32)


def _dot_tn(a, b):
    return lax.dot_general(a.astype(BF16), b.astype(BF16), (((0,), (0,)), ((), ())), preferred_element_type=F32)


def _pack_pairs(x):
    c = x.shape[1] // 2
    hi = pltpu.bitcast(x[:, :c].astype(BF16).astype(F32), U32)
    lo = pltpu.bitcast(x[:, c:].astype(BF16).astype(F32), U32)
    return hi | lax.shift_right_logical(lo, jnp.uint32(16))


def _unpack_pairs(w):
    left = pltpu.bitcast(w & jnp.uint32(0xFFFF0000), F32)
    right = pltpu.bitcast(lax.shift_left(w, jnp.uint32(16)), F32)
    return left, right


def _inproj_kernel(x_ref, nw_ref, w_ref, mix_ref, gate_ref):
    h = _rms(x_ref[...], nw_ref[...]).astype(BF16)
    half = MIX_COLS // 2
    for c0 in (0, half):
        mix_ref[:, c0:c0 + half] = jnp.dot(h, w_ref[:, c0:c0 + half], preferred_element_type=F32)
    half = GATE_COLS // 2
    for c0 in (0, half):
        gate_ref[:, c0:c0 + half] = jnp.dot(h, w_ref[:, MIX_COLS + c0:MIX_COLS + c0 + half],
                                            preferred_element_type=F32)


def _inproj(x, norm_w, w_in_r, tm):
    n, d = x.shape
    return pl.pallas_call(
        _inproj_kernel,
        grid=(n // tm,),
        in_specs=[
            pl.BlockSpec((tm, d), lambda i: (i, 0)),
            pl.BlockSpec((1, d), lambda i: (0, 0)),
            pl.BlockSpec((d, IN_COLS), lambda i: (0, 0)),
        ],
        out_specs=[pl.BlockSpec((tm, MIX_COLS), lambda i: (i, 0)), pl.BlockSpec((tm, GATE_COLS), lambda i: (i, 0))],
        out_shape=[jax.ShapeDtypeStruct((n, MIX_COLS), F32), jax.ShapeDtypeStruct((n, GATE_COLS), F32)],
        compiler_params=pltpu.CompilerParams(dimension_semantics=("arbitrary",), vmem_limit_bytes=VMEM_LIMIT),
        name="inproj",
    )(x, norm_w, w_in_r)


def _conv_silu_qkv(ext_c, seq, tt, wconv_ref, ci, col0=0):
    base = CONV_ROWS - (CONV_WIDTH - 1)
    cs = slice(ci * LANES, (ci + 1) * LANES)
    cl = slice(ci * LANES - col0, (ci + 1) * LANES - col0)
    idx = (lambda r: (r, cl)) if seq is None else (lambda r: (seq, r, cl))
    acc = ext_c[idx(slice(base, base + tt))] * wconv_ref[0:1, cs]
    for j in range(1, CONV_WIDTH):
        acc = acc + ext_c[idx(slice(base + j, base + j + tt))] * wconv_ref[j:j + 1, cs]
    y = acc * _sigmoid(acc)
    if ci < 2 * N_HEADS:
        y = y * lax.rsqrt(jnp.sum(y * y, axis=-1, keepdims=True) + EPS)
    if ci < N_HEADS:
        y = y * (HEAD_DIM ** -0.5)
    return y


def _inproj_conv_kernel(x_ref, nw_ref, w_ref, wconv_ref, conv0_ref, mix_ref, gate_ref, nconv_ref, h_s, *ext, n_tiles):
    t = pl.program_id(1)
    tm = x_ref.shape[1]
    base = CONV_ROWS - (CONV_WIDTH - 1)
    step = CONV_DIM // len(ext)

    @pl.when(t == 0)
    def _():
        for g, buf in enumerate(ext):
            buf[base:CONV_ROWS, :] = conv0_ref[0, :, g * step:(g + 1) * step]

    h_s[...] = _rms(x_ref[0], nw_ref[...]).astype(BF16)

    def proj(c0, c1):
        return jnp.dot(h_s[...], w_ref[:, c0:c1], preferred_element_type=F32)

    def fill_z0():
        mix_ref[0, :, OFF_Z:OFF_Z + QK_DIM // 2] = proj(OFF_Z, OFF_Z + QK_DIM // 2)

    def fill_z1():
        mix_ref[0, :, OFF_Z + QK_DIM // 2:OFF_QKV] = proj(OFF_Z + QK_DIM // 2, OFF_QKV)

    def fill_pool():
        mix_ref[0, :, OFF_POOL:MIX_COLS] = proj(OFF_POOL, MIX_COLS)

    def fill_gate(i):
        q = GATE_COLS // 4
        gate_ref[0, :, i * q:(i + 1) * q] = proj(MIX_COLS + i * q, MIX_COLS + (i + 1) * q)

    fillers = [fill_z0, fill_z1, fill_pool] + [functools.partial(fill_gate, i) for i in range(4)]
    for g, buf in enumerate(ext):
        c0 = g * step
        buf[CONV_ROWS:CONV_ROWS + tm, :] = proj(OFF_QKV + c0, OFF_QKV + c0 + step)
        if fillers:
            fillers.pop(0)()
        for ci in range(c0 // LANES, (c0 + step) // LANES):
            mix_ref[0, :, OFF_QKV + ci * LANES:OFF_QKV + (ci + 1) * LANES] = _conv_silu_qkv(
                buf, None, tm, wconv_ref, ci, c0)
    for f in fillers:
        f()

    @pl.when(t == n_tiles - 1)
    def _():
        for g, buf in enumerate(ext):
            nconv_ref[0, :, g * step:(g + 1) * step] = buf[tm + base:tm + CONV_ROWS, :]

    for buf in ext:
        buf[0:CONV_ROWS, :] = buf[tm:tm + CONV_ROWS, :]


def _inproj_conv(x, conv0, norm_w, w_in_r, wconv, tm):
    n_seq, seq_len, d = x.shape
    n_tiles = seq_len // tm
    tile = lambda b, t: (b, t, 0)
    const = lambda b, t: (0, 0)
    mix, gates, nconv = pl.pallas_call(
        functools.partial(_inproj_conv_kernel, n_tiles=n_tiles),
        grid=(n_seq, n_tiles),
        in_specs=[
            pl.BlockSpec((1, tm, d), tile),
            pl.BlockSpec((1, d), const),
            pl.BlockSpec((d, IN_COLS), const),
            pl.BlockSpec((SUBLANES, CONV_DIM), const),
            pl.BlockSpec((1, CONV_WIDTH - 1, CONV_DIM), lambda b, t: (b, 0, 0)),
        ],
        out_specs=[
            pl.BlockSpec((1, tm, MIX_COLS), tile),
            pl.BlockSpec((1, tm, GATE_COLS), tile),
            pl.BlockSpec((1, CONV_WIDTH - 1, CONV_DIM), lambda b, t: (b, 0, 0)),
        ],
        out_shape=[
            jax.ShapeDtypeStruct((n_seq, seq_len, MIX_COLS), F32),
            jax.ShapeDtypeStruct((n_seq, seq_len, GATE_COLS), F32),
            jax.ShapeDtypeStruct((n_seq, CONV_WIDTH - 1, CONV_DIM), F32),
        ],
        scratch_shapes=[pltpu.VMEM((tm, d), BF16)] + [
            pltpu.VMEM((CONV_ROWS + tm, CONV_DIM // CONV_GROUPS), F32) for _ in range(CONV_GROUPS)],
        compiler_params=pltpu.CompilerParams(dimension_semantics=("arbitrary", "arbitrary"),
                                             vmem_limit_bytes=VMEM_LIMIT),
        name="inproj_conv",
    )(x, norm_w, w_in_r, wconv, conv0)
    n = n_seq * seq_len
    return mix.reshape(n, MIX_COLS), gates.reshape(n, GATE_COLS), nconv


def _mixer_kernel(proj_ref, pool0_ref, conv0_ref, s0_ref, wpool_ref, pscale_ref, wconv_ref, alog_ref, dtb_ref, onw_ref,
                  ypool_ref, ydelta_ref, npool_ref, nconv_ref, ns_ref,
                  ext_p, ext_c, q_s, k_s, v_s, g_s, b_s, state,
                  *, sb, tt, tv, chunk, pos0, n_tiles, qkv_done):
    t = pl.program_id(1)
    base = CONV_ROWS - (CONV_WIDTH - 1)

    @pl.when(t == 0)
    def _():
        ext_p[:, POOL_ROWS - POOL_BUF:POOL_ROWS, :] = pool0_ref[...]
        if not qkv_done:
            ext_c[:, base:CONV_ROWS, :] = conv0_ref[...]
        state[...] = s0_ref[...]

    row = lax.broadcasted_iota(jnp.int32, (tt, 1), 0)
    pos = pos0 + t * tt + row
    for s in range(sb):
        ext_p[s, POOL_ROWS:POOL_ROWS + tt, :] = proj_ref[s, :, OFF_POOL:OFF_POOL + POOL_DIM]
        if not qkv_done:
            ext_c[s, CONV_ROWS:CONV_ROWS + tt, :] = proj_ref[s, :, OFF_QKV:OFF_QKV + CONV_DIM]

        for gi, win in enumerate(POOL_WINDOWS):
            cs = slice(gi * POOL_GROUP_DIM, (gi + 1) * POOL_GROUP_DIM)
            u = ext_p[s, POOL_ROWS:POOL_ROWS + tt, cs]
            acc = u
            for j in range(1, win):
                acc = acc + ext_p[s, POOL_ROWS - j:POOL_ROWS - j + tt, cs]
            cnt = jnp.minimum(win, pos + 1).astype(F32)
            pooled = acc / cnt - u
            ypool_ref[s, :, cs] = _dot(pooled, wpool_ref[gi]) * pscale_ref[:, cs]

        for ci in range(0 if qkv_done else CONV_DIM // LANES):
            hs = slice((ci % N_HEADS) * HEAD_DIM, (ci % N_HEADS + 1) * HEAD_DIM)
            (q_s, k_s, v_s)[ci // N_HEADS][s, :, hs] = _conv_silu_qkv(ext_c, s, tt, wconv_ref, ci)

        beta = _sigmoid(proj_ref[s, :, OFF_B:OFF_B + LANES])
        g = -jnp.exp(alog_ref[...]) * _softplus(proj_ref[s, :, OFF_A:OFF_A + LANES] + dtb_ref[...])
        if tv < tt:
            beta = jnp.where(row < tv, beta, 0.0)
            g = jnp.where(row < tv, g, 0.0)
        b_s[s] = beta
        g_s[s] = g

    @pl.when(t == n_tiles - 1)
    def _():
        npool_ref[...] = ext_p[:, tv + POOL_ROWS - POOL_BUF:tv + POOL_ROWS, :]
        if qkv_done:
            nconv_ref[...] = jnp.zeros_like(nconv_ref)
        else:
            nconv_ref[...] = ext_c[:, tv + base:tv + CONV_ROWS, :]

    if n_tiles > 1:
        ext_p[:, 0:POOL_ROWS, :] = ext_p[:, tt:tt + POOL_ROWS, :]
        if not qkv_done:
            ext_c[:, 0:CONV_ROWS, :] = ext_c[:, tt:tt + CONV_ROWS, :]

    ri = lax.broadcasted_iota(jnp.int32, (chunk, chunk), 0)
    cj = lax.broadcasted_iota(jnp.int32, (chunk, chunk), 1)
    causal = ri >= cj
    strict = ri > cj
    ltri = causal.astype(F32)
    eye = (ri == cj).astype(F32)
    n_doublings = chunk.bit_length() - 2
    units = [(s, h) for s in range(sb) for h in range(N_HEADS)]
    hsl = [slice(h * HEAD_DIM, (h + 1) * HEAD_DIM) for h in range(N_HEADS)]

    def qkv(which, s, rows, h):
        if qkv_done:
            c0 = OFF_QKV + which * QK_DIM + h * HEAD_DIM
            return proj_ref[s, rows, c0:c0 + HEAD_DIM]
        return (q_s, k_s, v_s)[which][s, rows, hsl[h]]

    def chunk_body(ci, carry):
        r0 = pl.multiple_of(ci * chunk, chunk)
        rows = pl.ds(r0, chunk)
        g_all, g_t, b_all = [], [], []
        for s in range(sb):
            ga = jnp.dot(ltri, g_s[s, rows, :], precision=HIGHEST, preferred_element_type=F32)
            g_all.append(ga)
            if chunk < LANES:
                ga = jnp.concatenate([ga, jnp.zeros((LANES - chunk, LANES), F32)], axis=0)
            g_t.append(ga.T)
            b_all.append(b_s[s, rows, :])
        g_col = [g_all[s][:, h:h + 1] for s, h in units]
        dec = [jnp.exp(jnp.minimum(g_col[i] - g_t[s][h:h + 1, 0:chunk], 0.0)) for i, (s, h) in enumerate(units)]
        b_col = [b_all[s][:, h:h + 1] for s, h in units]
        q = [qkv(0, s, rows, h) for s, h in units]
        k = [qkv(1, s, rows, h) for s, h in units]
        idx = range(len(units))
        kb = [k[i] * b_col[i] for i in idx]
        kbk = [_dot_nt(kb[i], k[i]) for i in idx]
        qk = [_dot_nt(q[i], k[i]) for i in idx]
        p_mat = [jnp.where(strict, kbk[i] * dec[i], 0.0) for i in idx]
        t_inv = [eye - p_mat[i] for i in idx]
        for _ in range(n_doublings):
            p_mat = [_dot(p_mat[i], p_mat[i]) for i in idx]
            xp = [_dot(t_inv[i], p_mat[i]) for i in idx]
            t_inv = [t_inv[i] + xp[i] for i in idx]
        g_exp = [jnp.exp(g_col[i]) for i in idx]
        wu = [_dot(t_inv[i], jnp.concatenate([kb[i] * g_exp[i], qkv(2, s, rows, h) * b_col[i]], axis=1))
              for i, (s, h) in enumerate(units)]
        s_old = [state[s, h] for s, h in units]
        res = [_dot(jnp.concatenate([wu[i][:, 0:HEAD_DIM], q[i] * g_exp[i]], axis=0), s_old[i]) for i in idx]
        v_new = [wu[i][:, HEAD_DIM:2 * HEAD_DIM] - res[i][0:chunk] for i in idx]
        a_qk = [jnp.where(causal, qk[i] * dec[i], 0.0) for i in idx]
        g_last = [g_all[s][chunk - 1:chunk, h:h + 1] for s, h in units]
        k_dec = [k[i] * jnp.exp(g_last[i] - g_col[i]) for i in idx]
        intra = [_dot(a_qk[i], v_new[i]) for i in idx]
        upd = [_dot_tn(k_dec[i], v_new[i]) for i in idx]
        for i, (s, h) in enumerate(units):
            state[s, h] = s_old[i] * jnp.exp(g_last[i]) + upd[i]
            o_c = res[i][chunk:2 * chunk] + intra[i]
            z_c = proj_ref[s, rows, OFF_Z + h * HEAD_DIM:OFF_Z + (h + 1) * HEAD_DIM]
            ydelta_ref[s, rows, hsl[h]] = _rms(o_c, onw_ref[...]) * (z_c * _sigmoid(z_c))
        return carry

    lax.fori_loop(0, tt // chunk, chunk_body, 0)

    @pl.when(t == n_tiles - 1)
    def _():
        ns_ref[...] = state[...]


def _mixers(proj, pool0, conv0, s0, wpool, pscale, wconv, alog, dtb, onw, *, n_seq, sb, tt, tv, chunk, pos0, n_tiles,
            qkv_done):
    kern = functools.partial(_mixer_kernel, sb=sb, tt=tt, tv=tv, chunk=chunk, pos0=pos0, n_tiles=n_tiles,
                             qkv_done=qkv_done)
    seq_len = n_tiles * tt
    tile = lambda b, t: (b, t, 0)
    seq3 = lambda b, t: (b, 0, 0)
    seq4 = lambda b, t: (b, 0, 0, 0)
    const2 = lambda b, t: (0, 0)
    ypool, ydelta, npool, nconv, ns = pl.pallas_call(
        kern,
        grid=(n_seq // sb, n_tiles),
        in_specs=[
            pl.BlockSpec((sb, tt, MIX_COLS), tile),
            pl.BlockSpec((sb, POOL_BUF, POOL_DIM), seq3),
            pl.BlockSpec((sb, CONV_WIDTH - 1, CONV_DIM), seq3),
            pl.BlockSpec((sb, N_HEADS, HEAD_DIM, HEAD_DIM), seq4),
            pl.BlockSpec((len(POOL_WINDOWS), POOL_GROUP_DIM, POOL_GROUP_DIM), lambda b, t: (0, 0, 0)),
            pl.BlockSpec((1, POOL_DIM), const2),
            pl.BlockSpec((SUBLANES, CONV_DIM), const2),
            pl.BlockSpec((1, LANES), const2),
            pl.BlockSpec((1, LANES), const2),
            pl.BlockSpec((1, HEAD_DIM), const2),
        ],
        out_specs=[
            pl.BlockSpec((sb, tt, POOL_DIM), tile),
            pl.BlockSpec((sb, tt, QK_DIM), tile),
            pl.BlockSpec((sb, POOL_BUF, POOL_DIM), seq3),
            pl.BlockSpec((sb, CONV_WIDTH - 1, CONV_DIM), seq3),
            pl.BlockSpec((sb, N_HEADS, HEAD_DIM, HEAD_DIM), seq4),
        ],
        out_shape=[
            jax.ShapeDtypeStruct((n_seq, seq_len, POOL_DIM), F32),
            jax.ShapeDtypeStruct((n_seq, seq_len, QK_DIM), F32),
            jax.ShapeDtypeStruct((n_seq, POOL_BUF, POOL_DIM), F32),
            jax.ShapeDtypeStruct((n_seq, CONV_WIDTH - 1, CONV_DIM), F32),
            jax.ShapeDtypeStruct((n_seq, N_HEADS, HEAD_DIM, HEAD_DIM), F32),
        ],
        scratch_shapes=[
            pltpu.VMEM((sb, POOL_ROWS + tt, POOL_DIM), F32),
            pltpu.VMEM((sb, CONV_ROWS + tt, CONV_DIM), F32),
            pltpu.VMEM((sb, tt, QK_DIM), F32),
            pltpu.VMEM((sb, tt, QK_DIM), F32),
            pltpu.VMEM((sb, tt, QK_DIM), F32),
            pltpu.VMEM((sb, tt, LANES), F32),
            pltpu.VMEM((sb, tt, LANES), F32),
            pltpu.VMEM((sb, N_HEADS, HEAD_DIM, HEAD_DIM), F32),
        ],
        compiler_params=pltpu.CompilerParams(dimension_semantics=("arbitrary", "arbitrary"),
                                             vmem_limit_bytes=VMEM_LIMIT),
        name="mixers",
    )(proj.reshape(n_seq, seq_len, MIX_COLS), pool0, conv0, s0, wpool, pscale, wconv, alog, dtb, onw)
    n = n_seq * seq_len
    return ypool.reshape(n, POOL_DIM), ydelta.reshape(n, QK_DIM), npool, nconv, ns


def _merge_kernel(x_ref, gates_ref, ypool_ref, ydelta_ref, wbp_ref, wbd_ref, wout_ref, n2w_ref, wr_hi_ref, wr_lo_ref,
                  br_ref, cnt0_ref, xm_ref, h2_ref, route_ref, wk_ref, cnt_ref, cnt_s, *, period, valid):
    tm = x_ref.shape[0]

    @pl.when(pl.program_id(0) == 0)
    def _():
        cnt_s[...] = cnt0_ref[...]

    bp = _dot(ypool_ref[...], wbp_ref[...])
    bd = _dot(ydelta_ref[...], wbd_ref[...])
    merged = _sigmoid(gates_ref[:, 0:1024]) * bp + _sigmoid(gates_ref[:, 1024:2048]) * bd
    xm = x_ref[...] + _dot(merged, wout_ref[...])
    xm_ref[...] = xm
    h2 = _rms(xm, n2w_ref[...])
    h2_ref[...] = _pack_pairs(h2)

    h2_hi = h2.astype(BF16)
    h2_lo = (h2 - h2_hi.astype(F32)).astype(BF16)
    logits = (jnp.dot(h2_hi, wr_hi_ref[...], preferred_element_type=F32)
              + (jnp.dot(h2_hi, wr_lo_ref[...], preferred_element_type=F32)
                 + jnp.dot(h2_lo, wr_hi_ref[...], preferred_element_type=F32))) + br_ref[...]
    lane = lax.broadcasted_iota(jnp.int32, logits.shape, 1)
    neg = -jnp.inf
    far = LANES - 1
    is_g = lane < N_GROUPS
    g_max = jnp.max(jnp.where(is_g, logits, neg), axis=-1, keepdims=True)
    g_sel = jnp.min(jnp.where(is_g & (logits == g_max), lane, far), axis=-1, keepdims=True)
    p_g = 1.0 / jnp.sum(jnp.where(is_g, jnp.exp(logits - g_max), 0.0), axis=-1, keepdims=True)
    e_lane = lane - N_GROUPS
    is_e = (e_lane >= 0) & (e_lane < N_EXPERTS) & ((e_lane // EXPERTS_PER_GROUP) == g_sel)
    ev = jnp.where(is_e, logits, neg)
    v1 = jnp.max(ev, axis=-1, keepdims=True)
    i1 = jnp.min(jnp.where(is_e & (ev == v1), lane, far), axis=-1, keepdims=True)
    is_e2 = is_e & (lane != i1)
    ev2 = jnp.where(is_e2, logits, neg)
    v2 = jnp.max(ev2, axis=-1, keepdims=True)
    i2 = jnp.min(jnp.where(is_e2 & (ev2 == v2), lane, far), axis=-1, keepdims=True)
    e21 = jnp.exp(v2 - v1)
    w1 = p_g / (1.0 + e21)
    w2 = p_g * e21 / (1.0 + e21)
    e1 = i1 - N_GROUPS
    e2 = i2 - N_GROUPS
    chosen = (lane == e1) | (lane == e2)
    routed = jnp.ones((tm, 1), F32)
    if valid < period:
        row = lax.broadcasted_iota(jnp.int32, (tm, 1), 0)
        is_real = lax.rem(row, period) < valid
        chosen = chosen & is_real
        routed = jnp.where(is_real, 1.0, 0.0)
    onehot = jnp.where(chosen, 1.0, 0.0)
    rr = lax.broadcasted_iota(jnp.int32, (tm, tm), 0)
    cc = lax.broadcasted_iota(jnp.int32, (tm, tm), 1)
    earlier = jnp.where(rr > cc, 1.0, 0.0).astype(BF16)
    before = jnp.dot(earlier, onehot.astype(BF16), preferred_element_type=F32) + cnt_s[...]
    r1 = jnp.sum(jnp.where(lane == e1, before, 0.0), axis=-1, keepdims=True)
    r2 = jnp.sum(jnp.where(lane == e2, before, 0.0), axis=-1, keepdims=True)
    cnt_s[...] = cnt_s[...] + jnp.sum(onehot, axis=0, keepdims=True)
    cnt_ref[...] = cnt_s[...]
    record = jnp.where(lane == 0, e1.astype(F32), jnp.where(lane == 1, e2.astype(F32), jnp.where(
        lane == 2, r1, jnp.where(lane == 3, r2, jnp.where(lane == 4, routed, 0.0)))))
    route_ref[...] = record.T[0:SUBLANES, :]
    lane8 = lax.broadcasted_iota(jnp.int32, (tm, SUBLANES), 1)
    wk_ref[...] = jnp.where(lane8 == 0, w1, jnp.where(lane8 == 1, w2, 0.0))


def _merge(x, proj, ypool, ydelta, wbp, wbd, wout, n2w, wr_hi, wr_lo, br, cnt0, tm, period, valid):
    n, d = x.shape
    row = lambda i: (i, 0)
    const = lambda i: (0, 0)
    return pl.pallas_call(
        functools.partial(_merge_kernel, period=period, valid=valid),
        grid=(n // tm,),
        in_specs=[
            pl.BlockSpec((tm, d), row),
            pl.BlockSpec((tm, GATE_COLS), row),
            pl.BlockSpec((tm, POOL_DIM), row),
            pl.BlockSpec((tm, QK_DIM), row),
            pl.BlockSpec((POOL_DIM, d), const),
            pl.BlockSpec((QK_DIM, d), const),
            pl.BlockSpec((d, d), const),
            pl.BlockSpec((1, d), const),
            pl.BlockSpec((d, LANES), const),
            pl.BlockSpec((d, LANES), const),
            pl.BlockSpec((1, LANES), const),
            pl.BlockSpec((1, LANES), const),
        ],
        out_specs=[
            pl.BlockSpec((tm, d), row),
            pl.BlockSpec((tm, d // 2), row),
            pl.BlockSpec((SUBLANES, tm), lambda i: (0, i)),
            pl.BlockSpec((tm, SUBLANES), row),
            pl.BlockSpec((1, LANES), const),
        ],
        out_shape=[
            jax.ShapeDtypeStruct((n, d), F32),
            jax.ShapeDtypeStruct((n, d // 2), U32),
            jax.ShapeDtypeStruct((SUBLANES, n), F32),
            jax.ShapeDtypeStruct((n, SUBLANES), F32),
            jax.ShapeDtypeStruct((1, LANES), F32),
        ],
        scratch_shapes=[pltpu.VMEM((1, LANES), F32)],
        compiler_params=pltpu.CompilerParams(dimension_semantics=("arbitrary",), vmem_limit_bytes=VMEM_LIMIT),
        name="merge",
    )(x, proj, ypool, ydelta, wbp, wbd, wout, n2w, wr_hi, wr_lo, br, cnt0)


def _expert_kernel(te_ref, nv_ref, x_ref, wg_ref, wu_ref, wd_ref, y_ref, wg_s, wu_s, wd_s):
    i = pl.program_id(0)
    n_valid = nv_ref[i]

    @pl.when((i == 0) | (te_ref[i] != te_ref[jnp.maximum(i - 1, 0)]))
    def _():
        wg_s[...] = wg_ref[0].astype(BF16)
        wu_s[...] = wu_ref[0].astype(BF16)
        wd_s[...] = wd_ref[0].astype(BF16)

    @pl.when(n_valid > 0)
    def _():
        row = lax.broadcasted_iota(jnp.int32, (x_ref.shape[0], 1), 0)
        left, right = _unpack_pairs(jnp.where(row < n_valid, x_ref[...], jnp.uint32(0)))
        x = jnp.concatenate([left.astype(BF16), right.astype(BF16)], axis=1)
        a = _dot(x, wg_s[...])
        b = _dot(x, wu_s[...])
        y_ref[...] = _pack_pairs(_dot((a * _sigmoid(a)) * b, wd_s[...]))

    @pl.when(n_valid == 0)
    def _():
        y_ref[...] = jnp.zeros_like(y_ref)


def _experts(tile_expert, tile_valid, xs, wg, wu, wd, tm):
    n_rows, dp = xs.shape
    d, f = wg.shape[1], wg.shape[2]
    grid_spec = pltpu.PrefetchScalarGridSpec(
        num_scalar_prefetch=2,
        grid=(n_rows // tm,),
        in_specs=[
            pl.BlockSpec((tm, dp), lambda i, te, nv: (jnp.where(nv[i] > 0, i, 0), 0)),
            pl.BlockSpec((1, d, f), lambda i, te, nv: (te[i], 0, 0)),
            pl.BlockSpec((1, d, f), lambda i, te, nv: (te[i], 0, 0)),
            pl.BlockSpec((1, f, d), lambda i, te, nv: (te[i], 0, 0)),
        ],
        out_specs=pl.BlockSpec((tm, dp), lambda i, te, nv: (i, 0)),
        scratch_shapes=[pltpu.VMEM((d, f), BF16), pltpu.VMEM((d, f), BF16), pltpu.VMEM((f, d), BF16)],
    )
    return pl.pallas_call(
        _expert_kernel,
        grid_spec=grid_spec,
        out_shape=jax.ShapeDtypeStruct((n_rows, dp), U32),
        compiler_params=pltpu.CompilerParams(dimension_semantics=("arbitrary",), vmem_limit_bytes=VMEM_LIMIT),
        name="experts",
    )(tile_expert, tile_valid, xs, wg, wu, wd)


def _plan_kernel(route_ref, cnt_ref, pos_ref, tiles_ref, *, tm_e, n_rows):
    counts = cnt_ref[...]
    padded = jnp.floor((counts + (tm_e - 1)) / tm_e) * tm_e
    lane = lax.broadcasted_iota(jnp.int32, counts.shape, 1)
    ends = padded
    shift = 1
    while shift < N_EXPERTS:
        ends = ends + jnp.where(lane >= shift, pltpu.roll(ends, shift, axis=1), 0.0)
        shift *= 2
    starts = ends - padded

    def lookup(table, key):
        out = jnp.zeros_like(key)
        for e in range(N_EXPERTS):
            out = out + jnp.where(key == e, table[:, e:e + 1], 0.0)
        return out

    routed = route_ref[4:5, :] > 0.0
    for k in range(2):
        pos = lookup(starts, route_ref[k:k + 1, :]) + route_ref[2 + k:3 + k, :]
        pos_ref[k:k + 1, :] = jnp.where(routed, pos, n_rows - 1.0).astype(jnp.int32)

    tile_start = lax.broadcasted_iota(jnp.int32, (1, tiles_ref.shape[1]), 1).astype(F32) * tm_e
    tile_expert = jnp.zeros_like(tile_start)
    for e in range(N_EXPERTS):
        tile_expert = tile_expert + jnp.where(ends[:, e:e + 1] <= tile_start, 1.0, 0.0)
    tile_expert = jnp.minimum(tile_expert, N_EXPERTS - 1.0)
    valid = lookup(counts, tile_expert) - (tile_start - lookup(starts, tile_expert))
    tiles_ref[0:1, :] = tile_expert.astype(jnp.int32)
    tiles_ref[1:2, :] = jnp.clip(valid, 0.0, tm_e).astype(jnp.int32)


def _plan(route_t, counts, tm_e, n_tiles):
    n = route_t.shape[1]
    tiles_pad = -(-n_tiles // LANES) * LANES
    return pl.pallas_call(
        functools.partial(_plan_kernel, tm_e=tm_e, n_rows=n_tiles * tm_e),
        out_shape=[jax.ShapeDtypeStruct((2, n), jnp.int32), jax.ShapeDtypeStruct((2, tiles_pad), jnp.int32)],
        name="plan",
    )(route_t, counts)


def _sc_workers():
    info = plsc.get_sparse_core_info()
    return info.num_cores, info.num_subcores


def _sc_chunk(per_worker):
    assert per_worker % SUBLANES == 0
    return max(c for c in range(SUBLANES, SC_ROWS + 1, SUBLANES) if per_worker % c == 0)


def _sc_scatter2(rows, idx_a, idx_b, n_out):
    n_cores, n_sub = _sc_workers()
    n, d = rows.shape
    per_worker = n // (n_cores * n_sub)
    assert per_worker * n_cores * n_sub == n
    chunk = _sc_chunk(per_worker)
    mesh = plsc.VectorSubcoreMesh(core_axis_name="c", subcore_axis_name="s")

    n_chunks = per_worker // chunk
    buf = lambda shape, dtype: [pltpu.VMEM(shape, dtype), pltpu.VMEM(shape, dtype)]

    @functools.partial(
        pl.kernel, mesh=mesh, out_type=jax.ShapeDtypeStruct((n_out, d), rows.dtype),
        scratch_types=buf((chunk,), jnp.int32) + buf((chunk,), jnp.int32) + buf((chunk, d), rows.dtype)
        + [pltpu.SemaphoreType.DMA, pltpu.SemaphoreType.DMA])
    def scatter_rows(rows_hbm, ia_hbm, ib_hbm, out_hbm, ia0, ia1, ib0, ib1, r0, r1, sem0, sem1):
        worker = lax.axis_index("s") * n_cores + lax.axis_index("c")
        base = worker * per_worker
        slots = ((ia0, ib0, r0, sem0), (ia1, ib1, r1, sem1))
        pending = {}
        for j in range(n_chunks):
            ia_v, ib_v, rows_v, sem = slots[j % 2]
            if j >= 2:
                for cp in pending.pop(j - 2):
                    cp.wait()
            off = pl.multiple_of(base + j * chunk, SUBLANES)
            pltpu.sync_copy(ia_hbm.at[pl.ds(off, chunk)], ia_v)
            pltpu.sync_copy(ib_hbm.at[pl.ds(off, chunk)], ib_v)
            pltpu.sync_copy(rows_hbm.at[pl.ds(off, chunk)], rows_v)
            pending[j] = (pltpu.async_copy(rows_v, out_hbm.at[ia_v], sem),
                          pltpu.async_copy(rows_v, out_hbm.at[ib_v], sem))
        for j in sorted(pending):
            for cp in pending[j]:
                cp.wait()

    return scatter_rows(rows, idx_a, idx_b)


def _sc_gather(table, idx):
    n_cores, n_sub = _sc_workers()
    n_idx = idx.shape[0]
    d = table.shape[1]
    per_worker = n_idx // (n_cores * n_sub)
    assert per_worker * n_cores * n_sub == n_idx
    chunk = _sc_chunk(per_worker)
    mesh = plsc.VectorSubcoreMesh(core_axis_name="c", subcore_axis_name="s")

    n_chunks = per_worker // chunk

    @functools.partial(
        pl.kernel, mesh=mesh, out_type=jax.ShapeDtypeStruct((n_idx, d), table.dtype),
        scratch_types=[pltpu.VMEM((chunk,), jnp.int32), pltpu.VMEM((chunk,), jnp.int32),
                       pltpu.VMEM((chunk, d), table.dtype), pltpu.VMEM((chunk, d), table.dtype),
                       pltpu.SemaphoreType.DMA, pltpu.SemaphoreType.DMA])
    def gather_rows(table_hbm, idx_hbm, out_hbm, i0, i1, r0, r1, sem0, sem1):
        worker = lax.axis_index("s") * n_cores + lax.axis_index("c")
        base = worker * per_worker
        slots = ((i0, r0, sem0), (i1, r1, sem1))

        def start(j):
            idx_v, rows_v, sem = slots[j % 2]
            off = pl.multiple_of(base + j * chunk, SUBLANES)
            pltpu.sync_copy(idx_hbm.at[pl.ds(off, chunk)], idx_v)
            return pltpu.async_copy(table_hbm.at[idx_v], rows_v, sem)

        nxt = start(0)
        for j in range(n_chunks):
            cur = nxt
            if j + 1 < n_chunks:
                nxt = start(j + 1)
            cur.wait()
            off = pl.multiple_of(base + j * chunk, SUBLANES)
            pltpu.sync_copy(slots[j % 2][1], out_hbm.at[pl.ds(off, chunk)])

    return gather_rows(table, idx)


def _finalize_kernel(xm_ref, ya_ref, yb_ref, wk_ref, fnw_ref, o_ref):
    d = xm_ref.shape[1]
    c = d // 2
    w = wk_ref[...]
    a_left, a_right = _unpack_pairs(ya_ref[...])
    b_left, b_right = _unpack_pairs(yb_ref[...])
    x_left = xm_ref[:, 0:c] + (w[:, 0:1] * a_left + w[:, 1:2] * b_left)
    x_right = xm_ref[:, c:d] + (w[:, 0:1] * a_right + w[:, 1:2] * b_right)
    ms = (jnp.sum(x_left * x_left, axis=-1, keepdims=True) + jnp.sum(x_right * x_right, axis=-1, keepdims=True)) / d
    scale = lax.rsqrt(ms + EPS)
    o_ref[:, 0:c] = x_left * scale * fnw_ref[:, 0:c]
    o_ref[:, c:d] = x_right * scale * fnw_ref[:, c:d]


def _finalize(xm, yk, wk, fnw, tm, row0):
    n, d = xm.shape
    steps = n // tm
    first = row0 // tm
    second = (yk.shape[0] // 2 + row0) // tm
    return pl.pallas_call(
        _finalize_kernel,
        grid=(steps,),
        in_specs=[
            pl.BlockSpec((tm, d), lambda i: (i, 0)),
            pl.BlockSpec((tm, d // 2), lambda i: (i + first, 0)),
            pl.BlockSpec((tm, d // 2), lambda i: (i + second, 0)),
            pl.BlockSpec((tm, SUBLANES), lambda i: (i, 0)),
            pl.BlockSpec((1, d), lambda i: (0, 0)),
        ],
        out_specs=pl.BlockSpec((tm, d), lambda i: (i, 0)),
        out_shape=jax.ShapeDtypeStruct((n, d), F32),
        compiler_params=pltpu.CompilerParams(dimension_semantics=("arbitrary",), vmem_limit_bytes=VMEM_LIMIT),
        name="finalize",
    )(xm, yk, yk, wk, fnw)


def _moe(h2, route, counts, wg, wu, wd, *, tm_e):
    n = h2.shape[0]
    n_rows = 2 * n + N_EXPERTS * tm_e
    n_tiles = n_rows // tm_e
    pos, tiles = _plan(route, counts, tm_e, n_tiles)
    xs = _sc_scatter2(h2, pos[0], pos[1], n_rows)
    y_sorted = _experts(tiles[0, :n_tiles], tiles[1, :n_tiles], xs, wg, wu, wd, tm_e)
    return _sc_gather(y_sorted, pos.reshape(-1))


def _layer(x_tokens, pool0, conv0, s0, cnt0, prm, *, n_seq, sb, tt, tv, chunk, pos0, n_tiles, tm):
    conv_in_proj = tv == tt
    if conv_in_proj:
        proj, gates, nconv_proj = _inproj_conv(x_tokens.reshape(n_seq, n_tiles * tt, -1), conv0, prm["norm1_w"],
                                               prm["w_in"], prm["w_conv"], tm)
    else:
        proj, gates = _inproj(x_tokens, prm["norm1_w"], prm["w_in"], tm)
    ypool, ydelta, npool, nconv, ns = _mixers(
        proj, pool0, conv0, s0, prm["w_pool"], prm["pool_scale"], prm["w_conv"], prm["a_log"], prm["dt_bias"],
        prm["o_norm_w"], n_seq=n_seq, sb=sb, tt=tt, tv=tv, chunk=chunk, pos0=pos0, n_tiles=n_tiles,
        qkv_done=conv_in_proj)
    if conv_in_proj:
        nconv = nconv_proj
    xm, h2, route, wk, counts = _merge(
        x_tokens, gates, ypool, ydelta, prm["w_branch_pool"], prm["w_branch_delta"], prm["w_out"], prm["norm2_w"],
        prm["w_router_hi"], prm["w_router_lo"], prm["b_router"], cnt0, 2 * tm, tt, tv)
    return xm, h2, route, wk, counts, npool, nconv, ns


def _pad_lanes(v, width=LANES):
    v = v.reshape(1, -1).astype(F32)
    return jnp.pad(v, ((0, 0), (0, width - v.shape[1])))


def kernel(x_prompt, x_sample, cache_pool, cache_conv, state_delta, norm1_w, w_in, w_pool, pool_scale, w_conv, a_log, dt_bias, o_norm_w, w_branch_pool, w_branch_delta, w_out, norm2_w, w_router_group, b_router_group, w_router_expert, b_router_expert, w_gate, w_up, w_down, final_norm_w):
    n_layers = norm1_w.shape[0]
    assert n_layers == 1, "single-layer step"
    bsz, seq, d = x_prompt.shape
    dbs, dseq, _ = x_sample.shape
    past_len = 16384
    lyr = 0

    wi = w_in[lyr]
    c = [POOL_DIM, POOL_DIM + CONV_DIM]
    c += [c[1] + N_HEADS, c[1] + 2 * N_HEADS]
    c += [c[3] + QK_DIM, c[3] + QK_DIM + d]
    pad8 = lambda m: jnp.pad(m, ((0, 0), (0, LANES - N_HEADS)))
    w_in_r = jnp.concatenate(
        [wi[:, c[3]:c[4]], wi[:, c[0]:c[1]], wi[:, :c[0]], pad8(wi[:, c[1]:c[2]]), pad8(wi[:, c[2]:c[3]]),
         wi[:, c[4]:c[5]], wi[:, c[5]:]], axis=1).astype(BF16)
    w_router = jnp.concatenate(
        [w_router_group[lyr], w_router_expert[lyr].reshape(d, N_EXPERTS),
         jnp.zeros((d, LANES - N_GROUPS - N_EXPERTS), F32)], axis=1).astype(F32)
    w_router_hi = w_router.astype(BF16)
    b_router = _pad_lanes(jnp.concatenate([b_router_group[lyr], b_router_expert[lyr].reshape(-1)]))
    prm = dict(
        norm1_w=norm1_w[lyr].reshape(1, d), w_in=w_in_r,
        w_pool=w_pool[lyr].astype(BF16), pool_scale=pool_scale[lyr].reshape(1, POOL_DIM),
        w_conv=jnp.pad(w_conv[lyr], ((0, SUBLANES - CONV_WIDTH), (0, 0))),
        a_log=_pad_lanes(a_log[lyr]), dt_bias=_pad_lanes(dt_bias[lyr]), o_norm_w=o_norm_w[lyr].reshape(1, HEAD_DIM),
        w_branch_pool=w_branch_pool[lyr].astype(BF16), w_branch_delta=w_branch_delta[lyr].astype(BF16),
        w_out=w_out[lyr].astype(BF16), norm2_w=norm2_w[lyr].reshape(1, d),
        w_router_hi=w_router_hi, w_router_lo=(w_router - w_router_hi.astype(F32)).astype(BF16), b_router=b_router,
    )
    wg, wu, wd = w_gate[lyr], w_up[lyr], w_down[lyr]
    fnw = final_norm_w.reshape(1, d)

    tt_p = 128
    xm, h2, route, wk, counts, npool, nconv, ns = _layer(
        x_prompt.reshape(bsz * seq, d),
        jnp.zeros((bsz, POOL_BUF, POOL_DIM), F32), jnp.zeros((bsz, CONV_WIDTH - 1, CONV_DIM), F32),
        jnp.zeros((bsz, N_HEADS, HEAD_DIM, HEAD_DIM), F32), jnp.zeros((1, LANES), F32), prm,
        n_seq=bsz, sb=4, tt=tt_p, tv=tt_p, chunk=64, pos0=0, n_tiles=seq // tt_p, tm=256)
    xm_p, h2_p, route_p, wk_p, counts_p = xm, h2, route, wk, counts
    pool_p = npool[None]
    conv_p = nconv[None]
    delta_p = ns[None]

    tt_s = SUBLANES
    xs_pad = jnp.pad(x_sample, ((0, 0), (0, tt_s - dseq), (0, 0))).reshape(dbs * tt_s, d)
    xm, h2, route, wk, counts, npool, nconv, ns = _layer(
        xs_pad, cache_pool[lyr], cache_conv[lyr], state_delta[lyr], counts_p, prm,
        n_seq=dbs, sb=8, tt=tt_s, tv=dseq, chunk=tt_s, pos0=past_len, n_tiles=1, tm=256)
    yk = _moe(jnp.concatenate([h2_p, h2]), jnp.concatenate([route_p, route], axis=1), counts, wg, wu, wd, tm_e=256)
    y_prompt = _finalize(xm_p, yk, wk_p, fnw, 256, 0).reshape(bsz, seq, d)
    y_sample = _finalize(xm, yk, wk, fnw, 256, bsz * seq).reshape(dbs, tt_s, d)[:, :dseq]
    pool_s = npool[None]
    conv_s = nconv[None]
    delta_s = ns[None]
    return (y_prompt, y_sample, pool_p, conv_p, delta_p, pool_s, conv_s, delta_s)
```

```python
import functools

import jax
import jax.numpy as jnp
from jax import lax
from jax.experimental import pallas as pl
from jax.experimental.pallas import tpu as pltpu
from jax.experimental.pallas import tpu_sc as plsc

F32 = jnp.float32
BF16 = jnp.bfloat16
U32 = jnp.uint32
EPS = 1e-6
HIGHEST = lax.Precision.HIGHEST

POOL_WINDOWS = (2, 4, 8, 16)
POOL_GROUP_DIM = 128
POOL_DIM = len(POOL_WINDOWS) * POOL_GROUP_DIM
POOL_BUF = max(POOL_WINDOWS) - 1
N_HEADS = 8
HEAD_DIM = 128
QK_DIM = N_HEADS * HEAD_DIM
CONV_WIDTH = 4
CONV_DIM = 3 * QK_DIM
N_GROUPS = 4
EXPERTS_PER_GROUP = 8
N_EXPERTS = N_GROUPS * EXPERTS_PER_GROUP
LANES = 128
SUBLANES = 8

OFF_Z = 0
OFF_QKV = OFF_Z + QK_DIM
OFF_POOL = OFF_QKV + CONV_DIM
OFF_B = OFF_POOL + POOL_DIM
OFF_A = OFF_B + LANES
MIX_COLS = OFF_A + LANES
GATE_COLS = 2048
IN_COLS = MIX_COLS + GATE_COLS

POOL_ROWS = 16
CONV_ROWS = 8
VMEM_LIMIT = 60 * 1024 * 1024
EXPERT_PART_ROWS = 256
CONV_GROUPS = 6
SC_ROWS = 96


def _sigmoid(x):
    return 1.0 / (1.0 + jnp.exp(-x))


def _softplus(x):
    return jnp.maximum(x, 0.0) + jnp.log1p(jnp.exp(-jnp.abs(x)))


def _rms(x, w):
    return x * lax.rsqrt(jnp.mean(x * x, axis=-1, keepdims=True) + EPS) * w


def _dot(a, b):
    return jnp.dot(a.astype(BF16), b.astype(BF16), preferred_element_type=F32)


def _dot_nt(a, b):
    return lax.dot_general(a.astype(BF16), b.astype(BF16), (((1,), (1,)), ((), ())), preferred_element_type=F32)


def _dot_tn(a, b):
    return lax.dot_general(a.astype(BF16), b.astype(BF16), (((0,), (0,)), ((), ())), preferred_element_type=F32)


def _pack_pairs(x):
    c = x.shape[1] // 2
    hi = pltpu.bitcast(x[:, :c].astype(BF16).astype(F32), U32)
    lo = pltpu.bitcast(x[:, c:].astype(BF16).astype(F32), U32)
    return hi | lax.shift_right_logical(lo, jnp.uint32(16))


def _unpack_pairs(w):
    left = pltpu.bitcast(w & jnp.uint32(0xFFFF0000), F32)
    right = pltpu.bitcast(lax.shift_left(w, jnp.uint32(16)), F32)
    return left, right


def _inproj_kernel(x_ref, nw_ref, w_ref, mix_ref, gate_ref):
    h = _rms(x_ref[...], nw_ref[...]).astype(BF16)
    half = MIX_COLS // 2
    for c0 in (0, half):
        mix_ref[:, c0:c0 + half] = jnp.dot(h, w_ref[:, c0:c0 + half], preferred_element_type=F32)
    half = GATE_COLS // 2
    for c0 in (0, half):
        gate_ref[:, c0:c0 + half] = jnp.dot(h, w_ref[:, MIX_COLS + c0:MIX_COLS + c0 + half],
                                            preferred_element_type=F32)


def _inproj(x, norm_w, w_in_r, tm):
    n, d = x.shape
    return pl.pallas_call(
        _inproj_kernel,
        grid=(n // tm,),
        in_specs=[
            pl.BlockSpec((tm, d), lambda i: (i, 0)),
            pl.BlockSpec((1, d), lambda i: (0, 0)),
            pl.BlockSpec((d, IN_COLS), lambda i: (0, 0)),
        ],
        out_specs=[pl.BlockSpec((tm, MIX_COLS), lambda i: (i, 0)), pl.BlockSpec((tm, GATE_COLS), lambda i: (i, 0))],
        out_shape=[jax.ShapeDtypeStruct((n, MIX_COLS), F32), jax.ShapeDtypeStruct((n, GATE_COLS), F32)],
        compiler_params=pltpu.CompilerParams(dimension_semantics=("arbitrary",), vmem_limit_bytes=VMEM_LIMIT),
        name="inproj",
    )(x, norm_w, w_in_r)


def _conv_silu_qkv(ext_c, seq, tt, wconv_ref, ci, col0=0):
    base = CONV_ROWS - (CONV_WIDTH - 1)
    cs = slice(ci * LANES, (ci + 1) * LANES)
    cl = slice(ci * LANES - col0, (ci + 1) * LANES - col0)
    idx = (lambda r: (r, cl)) if seq is None else (lambda r: (seq, r, cl))
    acc = ext_c[idx(slice(base, base + tt))] * wconv_ref[0:1, cs]
    for j in range(1, CONV_WIDTH):
        acc = acc + ext_c[idx(slice(base + j, base + j + tt))] * wconv_ref[j:j + 1, cs]
    y = acc * _sigmoid(acc)
    if ci < 2 * N_HEADS:
        y = y * lax.rsqrt(jnp.sum(y * y, axis=-1, keepdims=True) + EPS)
    if ci < N_HEADS:
        y = y * (HEAD_DIM ** -0.5)
    return y


def _inproj_conv_kernel(x_ref, nw_ref, w_ref, wconv_ref, conv0_ref, mix_ref, gate_ref, nconv_ref, h_s, *ext, n_tiles):
    t = pl.program_id(1)
    tm = x_ref.shape[1]
    base = CONV_ROWS - (CONV_WIDTH - 1)
    step = CONV_DIM // len(ext)

    @pl.when(t == 0)
    def _():
        for g, buf in enumerate(ext):
            buf[base:CONV_ROWS, :] = conv0_ref[0, :, g * step:(g + 1) * step]

    h_s[...] = _rms(x_ref[0], nw_ref[...]).astype(BF16)

    def proj(c0, c1):
        return jnp.dot(h_s[...], w_ref[:, c0:c1], preferred_element_type=F32)

    def fill_z0():
        mix_ref[0, :, OFF_Z:OFF_Z + QK_DIM // 2] = proj(OFF_Z, OFF_Z + QK_DIM // 2)

    def fill_z1():
        mix_ref[0, :, OFF_Z + QK_DIM // 2:OFF_QKV] = proj(OFF_Z + QK_DIM // 2, OFF_QKV)

    def fill_pool():
        mix_ref[0, :, OFF_POOL:MIX_COLS] = proj(OFF_POOL, MIX_COLS)

    def fill_gate(i):
        q = GATE_COLS // 4
        gate_ref[0, :, i * q:(i + 1) * q] = proj(MIX_COLS + i * q, MIX_COLS + (i + 1) * q)

    fillers = [fill_z0, fill_z1, fill_pool] + [functools.partial(fill_gate, i) for i in range(4)]
    for g, buf in enumerate(ext):
        c0 = g * step
        buf[CONV_ROWS:CONV_ROWS + tm, :] = proj(OFF_QKV + c0, OFF_QKV + c0 + step)
        if fillers:
            fillers.pop(0)()
        for ci in range(c0 // LANES, (c0 + step) // LANES):
            mix_ref[0, :, OFF_QKV + ci * LANES:OFF_QKV + (ci + 1) * LANES] = _conv_silu_qkv(
                buf, None, tm, wconv_ref, ci, c0)
    for f in fillers:
        f()

    @pl.when(t == n_tiles - 1)
    def _():
        for g, buf in enumerate(ext):
            nconv_ref[0, :, g * step:(g + 1) * step] = buf[tm + base:tm + CONV_ROWS, :]

    for buf in ext:
        buf[0:CONV_ROWS, :] = buf[tm:tm + CONV_ROWS, :]


def _inproj_conv(x, conv0, norm_w, w_in_r, wconv, tm):
    n_seq, seq_len, d = x.shape
    n_tiles = seq_len // tm
    tile = lambda b, t: (b, t, 0)
    const = lambda b, t: (0, 0)
    mix, gates, nconv = pl.pallas_call(
        functools.partial(_inproj_conv_kernel, n_tiles=n_tiles),
        grid=(n_seq, n_tiles),
        in_specs=[
            pl.BlockSpec((1, tm, d), tile),
            pl.BlockSpec((1, d), const),
            pl.BlockSpec((d, IN_COLS), const),
            pl.BlockSpec((SUBLANES, CONV_DIM), const),
            pl.BlockSpec((1, CONV_WIDTH - 1, CONV_DIM), lambda b, t: (b, 0, 0)),
        ],
        out_specs=[
            pl.BlockSpec((1, tm, MIX_COLS), tile),
            pl.BlockSpec((1, tm, GATE_COLS), tile),
            pl.BlockSpec((1, CONV_WIDTH - 1, CONV_DIM), lambda b, t: (b, 0, 0)),
        ],
        out_shape=[
            jax.ShapeDtypeStruct((n_seq, seq_len, MIX_COLS), F32),
            jax.ShapeDtypeStruct((n_seq, seq_len, GATE_COLS), F32),
            jax.ShapeDtypeStruct((n_seq, CONV_WIDTH - 1, CONV_DIM), F32),
        ],
        scratch_shapes=[pltpu.VMEM((tm, d), BF16)] + [
            pltpu.VMEM((CONV_ROWS + tm, CONV_DIM // CONV_GROUPS), F32) for _ in range(CONV_GROUPS)],
        compiler_params=pltpu.CompilerParams(dimension_semantics=("arbitrary", "arbitrary"),
                                             vmem_limit_bytes=VMEM_LIMIT),
        name="inproj_conv",
    )(x, norm_w, w_in_r, wconv, conv0)
    n = n_seq * seq_len
    return mix.reshape(n, MIX_COLS), gates.reshape(n, GATE_COLS), nconv


def _mixer_kernel(proj_ref, pool0_ref, conv0_ref, s0_ref, wpool_ref, pscale_ref, wconv_ref, alog_ref, dtb_ref, onw_ref,
                  ypool_ref, ydelta_ref, npool_ref, nconv_ref, ns_ref,
                  ext_p, ext_c, q_s, k_s, v_s, g_s, b_s, state,
                  *, sb, tt, tv, chunk, pos0, n_tiles, qkv_done):
    t = pl.program_id(1)
    base = CONV_ROWS - (CONV_WIDTH - 1)

    @pl.when(t == 0)
    def _():
        ext_p[:, POOL_ROWS - POOL_BUF:POOL_ROWS, :] = pool0_ref[...]
        if not qkv_done:
            ext_c[:, base:CONV_ROWS, :] = conv0_ref[...]
        state[...] = s0_ref[...]

    row = lax.broadcasted_iota(jnp.int32, (tt, 1), 0)
    pos = pos0 + t * tt + row
    for s in range(sb):
        ext_p[s, POOL_ROWS:POOL_ROWS + tt, :] = proj_ref[s, :, OFF_POOL:OFF_POOL + POOL_DIM]
        if not qkv_done:
            ext_c[s, CONV_ROWS:CONV_ROWS + tt, :] = proj_ref[s, :, OFF_QKV:OFF_QKV + CONV_DIM]

        for gi, win in enumerate(POOL_WINDOWS):
            cs = slice(gi * POOL_GROUP_DIM, (gi + 1) * POOL_GROUP_DIM)
            u = ext_p[s, POOL_ROWS:POOL_ROWS + tt, cs]
            acc = u
            for j in range(1, win):
                acc = acc + ext_p[s, POOL_ROWS - j:POOL_ROWS - j + tt, cs]
            cnt = jnp.minimum(win, pos + 1).astype(F32)
            pooled = acc / cnt - u
            ypool_ref[s, :, cs] = _dot(pooled, wpool_ref[gi]) * pscale_ref[:, cs]

        for ci in range(0 if qkv_done else CONV_DIM // LANES):
            hs = slice((ci % N_HEADS) * HEAD_DIM, (ci % N_HEADS + 1) * HEAD_DIM)
            (q_s, k_s, v_s)[ci // N_HEADS][s, :, hs] = _conv_silu_qkv(ext_c, s, tt, wconv_ref, ci)

        beta = _sigmoid(proj_ref[s, :, OFF_B:OFF_B + LANES])
        g = -jnp.exp(alog_ref[...]) * _softplus(proj_ref[s, :, OFF_A:OFF_A + LANES] + dtb_ref[...])
        if tv < tt:
            beta = jnp.where(row < tv, beta, 0.0)
            g = jnp.where(row < tv, g, 0.0)
        b_s[s] = beta
        g_s[s] = g

    @pl.when(t == n_tiles - 1)
    def _():
        npool_ref[...] = ext_p[:, tv + POOL_ROWS - POOL_BUF:tv + POOL_ROWS, :]
        if qkv_done:
            nconv_ref[...] = jnp.zeros_like(nconv_ref)
        else:
            nconv_ref[...] = ext_c[:, tv + base:tv + CONV_ROWS, :]

    if n_tiles > 1:
        ext_p[:, 0:POOL_ROWS, :] = ext_p[:, tt:tt + POOL_ROWS, :]
        if not qkv_done:
            ext_c[:, 0:CONV_ROWS, :] = ext_c[:, tt:tt + CONV_ROWS, :]

    ri = lax.broadcasted_iota(jnp.int32, (chunk, chunk), 0)
    cj = lax.broadcasted_iota(jnp.int32, (chunk, chunk), 1)
    causal = ri >= cj
    strict = ri > cj
    ltri = causal.astype(F32)
    eye = (ri == cj).astype(F32)
    n_doublings = chunk.bit_length() - 2
    units = [(s, h) for s in range(sb) for h in range(N_HEADS)]
    hsl = [slice(h * HEAD_DIM, (h + 1) * HEAD_DIM) for h in range(N_HEADS)]

    def qkv(which, s, rows, h):
        if qkv_done:
            c0 = OFF_QKV + which * QK_DIM + h * HEAD_DIM
            return proj_ref[s, rows, c0:c0 + HEAD_DIM]
        return (q_s, k_s, v_s)[which][s, rows, hsl[h]]

    def chunk_body(ci, carry):
        r0 = pl.multiple_of(ci * chunk, chunk)
        rows = pl.ds(r0, chunk)
        g_all, g_t, b_all = [], [], []
        for s in range(sb):
            ga = jnp.dot(ltri, g_s[s, rows, :], precision=HIGHEST, preferred_element_type=F32)
            g_all.append(ga)
            if chunk < LANES:
                ga = jnp.concatenate([ga, jnp.zeros((LANES - chunk, LANES), F32)], axis=0)
            g_t.append(ga.T)
            b_all.append(b_s[s, rows, :])
        g_col = [g_all[s][:, h:h + 1] for s, h in units]
        dec = [jnp.exp(jnp.minimum(g_col[i] - g_t[s][h:h + 1, 0:chunk], 0.0)) for i, (s, h) in enumerate(units)]
        b_col = [b_all[s][:, h:h + 1] for s, h in units]
        q = [qkv(0, s, rows, h) for s, h in units]
        k = [qkv(1, s, rows, h) for s, h in units]
        idx = range(len(units))
        kb = [k[i] * b_col[i] for i in idx]
        kbk = [_dot_nt(kb[i], k[i]) for i in idx]
        qk = [_dot_nt(q[i], k[i]) for i in idx]
        p_mat = [jnp.where(strict, kbk[i] * dec[i], 0.0) for i in idx]
        t_inv = [eye - p_mat[i] for i in idx]
        for _ in range(n_doublings):
            p_mat = [_dot(p_mat[i], p_mat[i]) for i in idx]
            xp = [_dot(t_inv[i], p_mat[i]) for i in idx]
            t_inv = [t_inv[i] + xp[i] for i in idx]
        g_exp = [jnp.exp(g_col[i]) for i in idx]
        wu = [_dot(t_inv[i], jnp.concatenate([kb[i] * g_exp[i], qkv(2, s, rows, h) * b_col[i]], axis=1))
              for i, (s, h) in enumerate(units)]
        s_old = [state[s, h] for s, h in units]
        res = [_dot(jnp.concatenate([wu[i][:, 0:HEAD_DIM], q[i] * g_exp[i]], axis=0), s_old[i]) for i in idx]
        v_new = [wu[i][:, HEAD_DIM:2 * HEAD_DIM] - res[i][0:chunk] for i in idx]
        a_qk = [jnp.where(causal, qk[i] * dec[i], 0.0) for i in idx]
        g_last = [g_all[s][chunk - 1:chunk, h:h + 1] for s, h in units]
        k_dec = [k[i] * jnp.exp(g_last[i] - g_col[i]) for i in idx]
        intra = [_dot(a_qk[i], v_new[i]) for i in idx]
        upd = [_dot_tn(k_dec[i], v_new[i]) for i in idx]
        for i, (s, h) in enumerate(units):
            state[s, h] = s_old[i] * jnp.exp(g_last[i]) + upd[i]
            o_c = res[i][chunk:2 * chunk] + intra[i]
            z_c = proj_ref[s, rows, OFF_Z + h * HEAD_DIM:OFF_Z + (h + 1) * HEAD_DIM]
            ydelta_ref[s, rows, hsl[h]] = _rms(o_c, onw_ref[...]) * (z_c * _sigmoid(z_c))
        return carry

    lax.fori_loop(0, tt // chunk, chunk_body, 0)

    @pl.when(t == n_tiles - 1)
    def _():
        ns_ref[...] = state[...]


def _mixers(proj, pool0, conv0, s0, wpool, pscale, wconv, alog, dtb, onw, *, n_seq, sb, tt, tv, chunk, pos0, n_tiles,
            qkv_done):
    kern = functools.partial(_mixer_kernel, sb=sb, tt=tt, tv=tv, chunk=chunk, pos0=pos0, n_tiles=n_tiles,
                             qkv_done=qkv_done)
    seq_len = n_tiles * tt
    tile = lambda b, t: (b, t, 0)
    seq3 = lambda b, t: (b, 0, 0)
    seq4 = lambda b, t: (b, 0, 0, 0)
    const2 = lambda b, t: (0, 0)
    ypool, ydelta, npool, nconv, ns = pl.pallas_call(
        kern,
        grid=(n_seq // sb, n_tiles),
        in_specs=[
            pl.BlockSpec((sb, tt, MIX_COLS), tile),
            pl.BlockSpec((sb, POOL_BUF, POOL_DIM), seq3),
            pl.BlockSpec((sb, CONV_WIDTH - 1, CONV_DIM), seq3),
            pl.BlockSpec((sb, N_HEADS, HEAD_DIM, HEAD_DIM), seq4),
            pl.BlockSpec((len(POOL_WINDOWS), POOL_GROUP_DIM, POOL_GROUP_DIM), lambda b, t: (0, 0, 0)),
            pl.BlockSpec((1, POOL_DIM), const2),
            pl.BlockSpec((SUBLANES, CONV_DIM), const2),
            pl.BlockSpec((1, LANES), const2),
            pl.BlockSpec((1, LANES), const2),
            pl.BlockSpec((1, HEAD_DIM), const2),
        ],
        out_specs=[
            pl.BlockSpec((sb, tt, POOL_DIM), tile),
            pl.BlockSpec((sb, tt, QK_DIM), tile),
            pl.BlockSpec((sb, POOL_BUF, POOL_DIM), seq3),
            pl.BlockSpec((sb, CONV_WIDTH - 1, CONV_DIM), seq3),
            pl.BlockSpec((sb, N_HEADS, HEAD_DIM, HEAD_DIM), seq4),
        ],
        out_shape=[
            jax.ShapeDtypeStruct((n_seq, seq_len, POOL_DIM), F32),
            jax.ShapeDtypeStruct((n_seq, seq_len, QK_DIM), F32),
            jax.ShapeDtypeStruct((n_seq, POOL_BUF, POOL_DIM), F32),
            jax.ShapeDtypeStruct((n_seq, CONV_WIDTH - 1, CONV_DIM), F32),
            jax.ShapeDtypeStruct((n_seq, N_HEADS, HEAD_DIM, HEAD_DIM), F32),
        ],
        scratch_shapes=[
            pltpu.VMEM((sb, POOL_ROWS + tt, POOL_DIM), F32),
            pltpu.VMEM((sb, CONV_ROWS + tt, CONV_DIM), F32),
            pltpu.VMEM((sb, tt, QK_DIM), F32),
            pltpu.VMEM((sb, tt, QK_DIM), F32),
            pltpu.VMEM((sb, tt, QK_DIM), F32),
            pltpu.VMEM((sb, tt, LANES), F32),
            pltpu.VMEM((sb, tt, LANES), F32),
            pltpu.VMEM((sb, N_HEADS, HEAD_DIM, HEAD_DIM), F32),
        ],
        compiler_params=pltpu.CompilerParams(dimension_semantics=("arbitrary", "arbitrary"),
                                             vmem_limit_bytes=VMEM_LIMIT),
        name="mixers",
    )(proj.reshape(n_seq, seq_len, MIX_COLS), pool0, conv0, s0, wpool, pscale, wconv, alog, dtb, onw)
    n = n_seq * seq_len
    return ypool.reshape(n, POOL_DIM), ydelta.reshape(n, QK_DIM), npool, nconv, ns


def _merge_kernel(x_ref, gates_ref, ypool_ref, ydelta_ref, wbp_ref, wbd_ref, wout_ref, n2w_ref, wr_hi_ref, wr_lo_ref,
                  br_ref, cnt0_ref, xm_ref, h2_ref, route_ref, wk_ref, cnt_ref, cnt_s, *, period, valid):
    tm = x_ref.shape[0]

    @pl.when(pl.program_id(0) == 0)
    def _():
        cnt_s[...] = cnt0_ref[...]

    bp = _dot(ypool_ref[...], wbp_ref[...])
    bd = _dot(ydelta_ref[...], wbd_ref[...])
    merged = _sigmoid(gates_ref[:, 0:1024]) * bp + _sigmoid(gates_ref[:, 1024:2048]) * bd
    xm = x_ref[...] + _dot(merged, wout_ref[...])
    xm_ref[...] = xm
    h2 = _rms(xm, n2w_ref[...])
    h2_ref[...] = _pack_pairs(h2)

    h2_hi = h2.astype(BF16)
    h2_lo = (h2 - h2_hi.astype(F32)).astype(BF16)
    logits = (jnp.dot(h2_hi, wr_hi_ref[...], preferred_element_type=F32)
              + (jnp.dot(h2_hi, wr_lo_ref[...], preferred_element_type=F32)
                 + jnp.dot(h2_lo, wr_hi_ref[...], preferred_element_type=F32))) + br_ref[...]
    lane = lax.broadcasted_iota(jnp.int32, logits.shape, 1)
    neg = -jnp.inf
    far = LANES - 1
    is_g = lane < N_GROUPS
    g_max = jnp.max(jnp.where(is_g, logits, neg), axis=-1, keepdims=True)
    g_sel = jnp.min(jnp.where(is_g & (logits == g_max), lane, far), axis=-1, keepdims=True)
    p_g = 1.0 / jnp.sum(jnp.where(is_g, jnp.exp(logits - g_max), 0.0), axis=-1, keepdims=True)
    e_lane = lane - N_GROUPS
    is_e = (e_lane >= 0) & (e_lane < N_EXPERTS) & ((e_lane // EXPERTS_PER_GROUP) == g_sel)
    ev = jnp.where(is_e, logits, neg)
    v1 = jnp.max(ev, axis=-1, keepdims=True)
    i1 = jnp.min(jnp.where(is_e & (ev == v1), lane, far), axis=-1, keepdims=True)
    is_e2 = is_e & (lane != i1)
    ev2 = jnp.where(is_e2, logits, neg)
    v2 = jnp.max(ev2, axis=-1, keepdims=True)
    i2 = jnp.min(jnp.where(is_e2 & (ev2 == v2), lane, far), axis=-1, keepdims=True)
    e21 = jnp.exp(v2 - v1)
    w1 = p_g / (1.0 + e21)
    w2 = p_g * e21 / (1.0 + e21)
    e1 = i1 - N_GROUPS
    e2 = i2 - N_GROUPS
    chosen = (lane == e1) | (lane == e2)
    routed = jnp.ones((tm, 1), F32)
    if valid < period:
        row = lax.broadcasted_iota(jnp.int32, (tm, 1), 0)
        is_real = lax.rem(row, period) < valid
        chosen = chosen & is_real
        routed = jnp.where(is_real, 1.0, 0.0)
    onehot = jnp.where(chosen, 1.0, 0.0)
    rr = lax.broadcasted_iota(jnp.int32, (tm, tm), 0)
    cc = lax.broadcasted_iota(jnp.int32, (tm, tm), 1)
    earlier = jnp.where(rr > cc, 1.0, 0.0).astype(BF16)
    before = jnp.dot(earlier, onehot.astype(BF16), preferred_element_type=F32) + cnt_s[...]
    r1 = jnp.sum(jnp.where(lane == e1, before, 0.0), axis=-1, keepdims=True)
    r2 = jnp.sum(jnp.where(lane == e2, before, 0.0), axis=-1, keepdims=True)
    cnt_s[...] = cnt_s[...] + jnp.sum(onehot, axis=0, keepdims=True)
    cnt_ref[...] = cnt_s[...]
    record = jnp.where(lane == 0, e1.astype(F32), jnp.where(lane == 1, e2.astype(F32), jnp.where(
        lane == 2, r1, jnp.where(lane == 3, r2, jnp.where(lane == 4, routed, 0.0)))))
    route_ref[...] = record.T[0:SUBLANES, :]
    lane8 = lax.broadcasted_iota(jnp.int32, (tm, SUBLANES), 1)
    wk_ref[...] = jnp.where(lane8 == 0, w1, jnp.where(lane8 == 1, w2, 0.0))


def _merge(x, proj, ypool, ydelta, wbp, wbd, wout, n2w, wr_hi, wr_lo, br, cnt0, tm, period, valid):
    n, d = x.shape
    row = lambda i: (i, 0)
    const = lambda i: (0, 0)
    return pl.pallas_call(
        functools.partial(_merge_kernel, period=period, valid=valid),
        grid=(n // tm,),
        in_specs=[
            pl.BlockSpec((tm, d), row),
            pl.BlockSpec((tm, GATE_COLS), row),
            pl.BlockSpec((tm, POOL_DIM), row),
            pl.BlockSpec((tm, QK_DIM), row),
            pl.BlockSpec((POOL_DIM, d), const),
            pl.BlockSpec((QK_DIM, d), const),
            pl.BlockSpec((d, d), const),
            pl.BlockSpec((1, d), const),
            pl.BlockSpec((d, LANES), const),
            pl.BlockSpec((d, LANES), const),
            pl.BlockSpec((1, LANES), const),
            pl.BlockSpec((1, LANES), const),
        ],
        out_specs=[
            pl.BlockSpec((tm, d), row),
            pl.BlockSpec((tm, d // 2), row),
            pl.BlockSpec((SUBLANES, tm), lambda i: (0, i)),
            pl.BlockSpec((tm, SUBLANES), row),
            pl.BlockSpec((1, LANES), const),
        ],
        out_shape=[
            jax.ShapeDtypeStruct((n, d), F32),
            jax.ShapeDtypeStruct((n, d // 2), U32),
            jax.ShapeDtypeStruct((SUBLANES, n), F32),
            jax.ShapeDtypeStruct((n, SUBLANES), F32),
            jax.ShapeDtypeStruct((1, LANES), F32),
        ],
        scratch_shapes=[pltpu.VMEM((1, LANES), F32)],
        compiler_params=pltpu.CompilerParams(dimension_semantics=("arbitrary",), vmem_limit_bytes=VMEM_LIMIT),
        name="merge",
    )(x, proj, ypool, ydelta, wbp, wbd, wout, n2w, wr_hi, wr_lo, br, cnt0)


def _expert_kernel(te_ref, nv_ref, x_ref, wg_ref, wu_ref, wd_ref, y_ref, wg_s, wu_s, wd_s):
    i = pl.program_id(0)
    n_valid = nv_ref[i]

    @pl.when((i == 0) | (te_ref[i] != te_ref[jnp.maximum(i - 1, 0)]))
    def _():
        wg_s[...] = wg_ref[0].astype(BF16)
        wu_s[...] = wu_ref[0].astype(BF16)
        wd_s[...] = wd_ref[0].astype(BF16)

    @pl.when(n_valid > 0)
    def _():
        tm = x_ref.shape[0]
        part = min(tm, EXPERT_PART_ROWS)
        parts = [slice(r0, r0 + part) for r0 in range(0, tm, part)]
        row = lax.broadcasted_iota(jnp.int32, (part, 1), 0)
        xs = []
        for p in parts:
            left, right = _unpack_pairs(jnp.where(row + p.start < n_valid, x_ref[p, :], jnp.uint32(0)))
            xs.append(jnp.concatenate([left.astype(BF16), right.astype(BF16)], axis=1))
        a = [_dot(x, wg_s[...]) for x in xs]
        b = [_dot(x, wu_s[...]) for x in xs]
        act = [(a[j] * _sigmoid(a[j])) * b[j] for j in range(len(parts))]
        y = [_dot(act[j], wd_s[...]) for j in range(len(parts))]
        for j, p in enumerate(parts):
            y_ref[p, :] = _pack_pairs(y[j])

    @pl.when(n_valid == 0)
    def _():
        y_ref[...] = jnp.zeros_like(y_ref)


def _experts(tile_expert, tile_valid, xs, wg, wu, wd, tm):
    n_rows, dp = xs.shape
    d, f = wg.shape[1], wg.shape[2]
    grid_spec = pltpu.PrefetchScalarGridSpec(
        num_scalar_prefetch=2,
        grid=(n_rows // tm,),
        in_specs=[
            pl.BlockSpec((tm, dp), lambda i, te, nv: (jnp.where(nv[i] > 0, i, 0), 0)),
            pl.BlockSpec((1, d, f), lambda i, te, nv: (te[i], 0, 0)),
            pl.BlockSpec((1, d, f), lambda i, te, nv: (te[i], 0, 0)),
            pl.BlockSpec((1, f, d), lambda i, te, nv: (te[i], 0, 0)),
        ],
        out_specs=pl.BlockSpec((tm, dp), lambda i, te, nv: (i, 0)),
        scratch_shapes=[pltpu.VMEM((d, f), BF16), pltpu.VMEM((d, f), BF16), pltpu.VMEM((f, d), BF16)],
    )
    return pl.pallas_call(
        _expert_kernel,
        grid_spec=grid_spec,
        out_shape=jax.ShapeDtypeStruct((n_rows, dp), U32),
        compiler_params=pltpu.CompilerParams(dimension_semantics=("arbitrary",), vmem_limit_bytes=VMEM_LIMIT),
        name="experts",
    )(tile_expert, tile_valid, xs, wg, wu, wd)


def _plan_kernel(route_ref, cnt_ref, pos_ref, tiles_ref, *, tm_e, n_rows):
    counts = cnt_ref[...]
    padded = jnp.floor((counts + (tm_e - 1)) / tm_e) * tm_e
    lane = lax.broadcasted_iota(jnp.int32, counts.shape, 1)
    ends = padded
    shift = 1
    while shift < N_EXPERTS:
        ends = ends + jnp.where(lane >= shift, pltpu.roll(ends, shift, axis=1), 0.0)
        shift *= 2
    starts = ends - padded

    def lookup(table, key):
        out = jnp.zeros_like(key)
        for e in range(N_EXPERTS):
            out = out + jnp.where(key == e, table[:, e:e + 1], 0.0)
        return out

    routed = route_ref[4:5, :] > 0.0
    for k in range(2):
        pos = lookup(starts, route_ref[k:k + 1, :]) + route_ref[2 + k:3 + k, :]
        pos_ref[k:k + 1, :] = jnp.where(routed, pos, n_rows - 1.0).astype(jnp.int32)

    tile_start = lax.broadcasted_iota(jnp.int32, (1, tiles_ref.shape[1]), 1).astype(F32) * tm_e
    tile_expert = jnp.zeros_like(tile_start)
    for e in range(N_EXPERTS):
        tile_expert = tile_expert + jnp.where(ends[:, e:e + 1] <= tile_start, 1.0, 0.0)
    tile_expert = jnp.minimum(tile_expert, N_EXPERTS - 1.0)
    valid = lookup(counts, tile_expert) - (tile_start - lookup(starts, tile_expert))
    tiles_ref[0:1, :] = tile_expert.astype(jnp.int32)
    tiles_ref[1:2, :] = jnp.clip(valid, 0.0, tm_e).astype(jnp.int32)


def _plan(route_t, counts, tm_e, n_tiles):
    n = route_t.shape[1]
    tiles_pad = -(-n_tiles // LANES) * LANES
    return pl.pallas_call(
        functools.partial(_plan_kernel, tm_e=tm_e, n_rows=n_tiles * tm_e),
        out_shape=[jax.ShapeDtypeStruct((2, n), jnp.int32), jax.ShapeDtypeStruct((2, tiles_pad), jnp.int32)],
        name="plan",
    )(route_t, counts)


def _sc_workers():
    info = plsc.get_sparse_core_info()
    return info.num_cores, info.num_subcores


def _sc_chunk(per_worker):
    assert per_worker % SUBLANES == 0
    return max(c for c in range(SUBLANES, SC_ROWS + 1, SUBLANES) if per_worker % c == 0)


def _sc_scatter2(rows, idx_a, idx_b, n_out):
    n_cores, n_sub = _sc_workers()
    n, d = rows.shape
    per_worker = n // (n_cores * n_sub)
    assert per_worker * n_cores * n_sub == n
    chunk = _sc_chunk(per_worker)
    mesh = plsc.VectorSubcoreMesh(core_axis_name="c", subcore_axis_name="s")

    n_chunks = per_worker // chunk
    buf = lambda shape, dtype: [pltpu.VMEM(shape, dtype), pltpu.VMEM(shape, dtype)]

    @functools.partial(
        pl.kernel, mesh=mesh, out_type=jax.ShapeDtypeStruct((n_out, d), rows.dtype),
        scratch_types=buf((chunk,), jnp.int32) + buf((chunk,), jnp.int32) + buf((chunk, d), rows.dtype)
        + [pltpu.SemaphoreType.DMA, pltpu.SemaphoreType.DMA])
    def scatter_rows(rows_hbm, ia_hbm, ib_hbm, out_hbm, ia0, ia1, ib0, ib1, r0, r1, sem0, sem1):
        worker = lax.axis_index("s") * n_cores + lax.axis_index("c")
        base = worker * per_worker
        slots = ((ia0, ib0, r0, sem0), (ia1, ib1, r1, sem1))
        pending = {}
        for j in range(n_chunks):
            ia_v, ib_v, rows_v, sem = slots[j % 2]
            if j >= 2:
                for cp in pending.pop(j - 2):
                    cp.wait()
            off = pl.multiple_of(base + j * chunk, SUBLANES)
            pltpu.sync_copy(ia_hbm.at[pl.ds(off, chunk)], ia_v)
            pltpu.sync_copy(ib_hbm.at[pl.ds(off, chunk)], ib_v)
            pltpu.sync_copy(rows_hbm.at[pl.ds(off, chunk)], rows_v)
            pending[j] = (pltpu.async_copy(rows_v, out_hbm.at[ia_v], sem),
                          pltpu.async_copy(rows_v, out_hbm.at[ib_v], sem))
        for j in sorted(pending):
            for cp in pending[j]:
                cp.wait()

    return scatter_rows(rows, idx_a, idx_b)


def _sc_gather(table, idx):
    n_cores, n_sub = _sc_workers()
    n_idx = idx.shape[0]
    d = table.shape[1]
    per_worker = n_idx // (n_cores * n_sub)
    assert per_worker * n_cores * n_sub == n_idx
    chunk = _sc_chunk(per_worker)
    mesh = plsc.VectorSubcoreMesh(core_axis_name="c", subcore_axis_name="s")

    n_chunks = per_worker // chunk

    @functools.partial(
        pl.kernel, mesh=mesh, out_type=jax.ShapeDtypeStruct((n_idx, d), table.dtype),
        scratch_types=[pltpu.VMEM((chunk,), jnp.int32), pltpu.VMEM((chunk,), jnp.int32),
                       pltpu.VMEM((chunk, d), table.dtype), pltpu.VMEM((chunk, d), table.dtype),
                       pltpu.SemaphoreType.DMA, pltpu.SemaphoreType.DMA])
    def gather_rows(table_hbm, idx_hbm, out_hbm, i0, i1, r0, r1, sem0, sem1):
        worker = lax.axis_index("s") * n_cores + lax.axis_index("c")
        base = worker * per_worker
        slots = ((i0, r0, sem0), (i1, r1, sem1))

        def start(j):
            idx_v, rows_v, sem = slots[j % 2]
            off = pl.multiple_of(base + j * chunk, SUBLANES)
            pltpu.sync_copy(idx_hbm.at[pl.ds(off, chunk)], idx_v)
            return pltpu.async_copy(table_hbm.at[idx_v], rows_v, sem)

        nxt = start(0)
        for j in range(n_chunks):
            cur = nxt
            if j + 1 < n_chunks:
                nxt = start(j + 1)
            cur.wait()
            off = pl.multiple_of(base + j * chunk, SUBLANES)
            pltpu.sync_copy(slots[j % 2][1], out_hbm.at[pl.ds(off, chunk)])

    return gather_rows(table, idx)


def _finalize_kernel(xm_ref, ya_ref, yb_ref, wk_ref, fnw_ref, o_ref):
    d = xm_ref.shape[1]
    c = d // 2
    w = wk_ref[...]
    a_left, a_right = _unpack_pairs(ya_ref[...])
    b_left, b_right = _unpack_pairs(yb_ref[...])
    x_left = xm_ref[:, 0:c] + (w[:, 0:1] * a_left + w[:, 1:2] * b_left)
    x_right = xm_ref[:, c:d] + (w[:, 0:1] * a_right + w[:, 1:2] * b_right)
    ms = (jnp.sum(x_left * x_left, axis=-1, keepdims=True) + jnp.sum(x_right * x_right, axis=-1, keepdims=True)) / d
    scale = lax.rsqrt(ms + EPS)
    o_ref[:, 0:c] = x_left * scale * fnw_ref[:, 0:c]
    o_ref[:, c:d] = x_right * scale * fnw_ref[:, c:d]


def _finalize(xm, yk, wk, fnw, tm, row0):
    n, d = xm.shape
    steps = n // tm
    first = row0 // tm
    second = (yk.shape[0] // 2 + row0) // tm
    return pl.pallas_call(
        _finalize_kernel,
        grid=(steps,),
        in_specs=[
            pl.BlockSpec((tm, d), lambda i: (i, 0)),
            pl.BlockSpec((tm, d // 2), lambda i: (i + first, 0)),
            pl.BlockSpec((tm, d // 2), lambda i: (i + second, 0)),
            pl.BlockSpec((tm, SUBLANES), lambda i: (i, 0)),
            pl.BlockSpec((1, d), lambda i: (0, 0)),
        ],
        out_specs=pl.BlockSpec((tm, d), lambda i: (i, 0)),
        out_shape=jax.ShapeDtypeStruct((n, d), F32),
        compiler_params=pltpu.CompilerParams(dimension_semantics=("arbitrary",), vmem_limit_bytes=VMEM_LIMIT),
        name="finalize",
    )(xm, yk, yk, wk, fnw)


def _moe(h2, route, counts, wg, wu, wd, *, tm_e):
    n = h2.shape[0]
    n_rows = 2 * n + N_EXPERTS * tm_e
    n_tiles = n_rows // tm_e
    pos, tiles = _plan(route, counts, tm_e, n_tiles)
    xs = _sc_scatter2(h2, pos[0], pos[1], n_rows)
    y_sorted = _experts(tiles[0, :n_tiles], tiles[1, :n_tiles], xs, wg, wu, wd, tm_e)
    return _sc_gather(y_sorted, pos.reshape(-1))


def _layer(x_tokens, pool0, conv0, s0, cnt0, prm, *, n_seq, sb, tt, tv, chunk, pos0, n_tiles, tm):
    conv_in_proj = tv == tt
    if conv_in_proj:
        proj, gates, nconv_proj = _inproj_conv(x_tokens.reshape(n_seq, n_tiles * tt, -1), conv0, prm["norm1_w"],
                                               prm["w_in"], prm["w_conv"], tm)
    else:
        proj, gates = _inproj(x_tokens, prm["norm1_w"], prm["w_in"], tm)
    ypool, ydelta, npool, nconv, ns = _mixers(
        proj, pool0, conv0, s0, prm["w_pool"], prm["pool_scale"], prm["w_conv"], prm["a_log"], prm["dt_bias"],
        prm["o_norm_w"], n_seq=n_seq, sb=sb, tt=tt, tv=tv, chunk=chunk, pos0=pos0, n_tiles=n_tiles,
        qkv_done=conv_in_proj)
    if conv_in_proj:
        nconv = nconv_proj
    xm, h2, route, wk, counts = _merge(
        x_tokens, gates, ypool, ydelta, prm["w_branch_pool"], prm["w_branch_delta"], prm["w_out"], prm["norm2_w"],
        prm["w_router_hi"], prm["w_router_lo"], prm["b_router"], cnt0, 2 * tm, tt, tv)
    return xm, h2, route, wk, counts, npool, nconv, ns


def _pad_lanes(v, width=LANES):
    v = v.reshape(1, -1).astype(F32)
    return jnp.pad(v, ((0, 0), (0, width - v.shape[1])))


def kernel(x_prompt, x_sample, cache_pool, cache_conv, state_delta, norm1_w, w_in, w_pool, pool_scale, w_conv, a_log, dt_bias, o_norm_w, w_branch_pool, w_branch_delta, w_out, norm2_w, w_router_group, b_router_group, w_router_expert, b_router_expert, w_gate, w_up, w_down, final_norm_w):
    n_layers = norm1_w.shape[0]
    assert n_layers == 1, "single-layer step"
    bsz, seq, d = x_prompt.shape
    dbs, dseq, _ = x_sample.shape
    past_len = 16384
    lyr = 0

    wi = w_in[lyr]
    c = [POOL_DIM, POOL_DIM + CONV_DIM]
    c += [c[1] + N_HEADS, c[1] + 2 * N_HEADS]
    c += [c[3] + QK_DIM, c[3] + QK_DIM + d]
    pad8 = lambda m: jnp.pad(m, ((0, 0), (0, LANES - N_HEADS)))
    w_in_r = jnp.concatenate(
        [wi[:, c[3]:c[4]], wi[:, c[0]:c[1]], wi[:, :c[0]], pad8(wi[:, c[1]:c[2]]), pad8(wi[:, c[2]:c[3]]),
         wi[:, c[4]:c[5]], wi[:, c[5]:]], axis=1).astype(BF16)
    w_router = jnp.concatenate(
        [w_router_group[lyr], w_router_expert[lyr].reshape(d, N_EXPERTS),
         jnp.zeros((d, LANES - N_GROUPS - N_EXPERTS), F32)], axis=1).astype(F32)
    w_router_hi = w_router.astype(BF16)
    b_router = _pad_lanes(jnp.concatenate([b_router_group[lyr], b_router_expert[lyr].reshape(-1)]))
    prm = dict(
        norm1_w=norm1_w[lyr].reshape(1, d), w_in=w_in_r,
        w_pool=w_pool[lyr].astype(BF16), pool_scale=pool_scale[lyr].reshape(1, POOL_DIM),
        w_conv=jnp.pad(w_conv[lyr], ((0, SUBLANES - CONV_WIDTH), (0, 0))),
        a_log=_pad_lanes(a_log[lyr]), dt_bias=_pad_lanes(dt_bias[lyr]), o_norm_w=o_norm_w[lyr].reshape(1, HEAD_DIM),
        w_branch_pool=w_branch_pool[lyr].astype(BF16), w_branch_delta=w_branch_delta[lyr].astype(BF16),
        w_out=w_out[lyr].astype(BF16), norm2_w=norm2_w[lyr].reshape(1, d),
        w_router_hi=w_router_hi, w_router_lo=(w_router - w_router_hi.astype(F32)).astype(BF16), b_router=b_router,
    )
    wg, wu, wd = w_gate[lyr], w_up[lyr], w_down[lyr]
    fnw = final_norm_w.reshape(1, d)

    tt_p = 128
    xm, h2, route, wk, counts, npool, nconv, ns = _layer(
        x_prompt.reshape(bsz * seq, d),
        jnp.zeros((bsz, POOL_BUF, POOL_DIM), F32), jnp.zeros((bsz, CONV_WIDTH - 1, CONV_DIM), F32),
        jnp.zeros((bsz, N_HEADS, HEAD_DIM, HEAD_DIM), F32), jnp.zeros((1, LANES), F32), prm,
        n_seq=bsz, sb=4, tt=tt_p, tv=tt_p, chunk=64, pos0=0, n_tiles=seq // tt_p, tm=256)
    yk = _moe(h2, route, counts, wg, wu, wd, tm_e=512)
    y_prompt = _finalize(xm, yk, wk, fnw, 256, 0).reshape(bsz, seq, d)
    pool_p = npool[None]
    conv_p = nconv[None]
    delta_p = ns[None]

    tt_s = SUBLANES
    xs_pad = jnp.pad(x_sample, ((0, 0), (0, tt_s - dseq), (0, 0))).reshape(dbs * tt_s, d)
    xm, h2, route, wk, counts, npool, nconv, ns = _layer(
        xs_pad, cache_pool[lyr], cache_conv[lyr], state_delta[lyr], jnp.zeros((1, LANES), F32), prm,
        n_seq=dbs, sb=8, tt=tt_s, tv=dseq, chunk=tt_s, pos0=past_len, n_tiles=1, tm=256)
    real = lambda a: a.reshape(dbs, tt_s, -1)[:, :dseq].reshape(dbs * dseq, -1)
    route = route.reshape(SUBLANES, dbs, tt_s)[:, :, :dseq].reshape(SUBLANES, dbs * dseq)
    yk = _moe(real(h2), route, counts, wg, wu, wd, tm_e=128)
    y_sample = _finalize(real(xm), yk, real(wk), fnw, 256, 0).reshape(dbs, dseq, d)
    pool_s = npool[None]
    conv_s = nconv[None]
    delta_s = ns[None]
    return (y_prompt, y_sample, pool_p, conv_p, delta_p, pool_s, conv_s, delta_s)
```

```python
import functools

import jax
import jax.numpy as jnp
from jax import lax
from jax.experimental import pallas as pl
from jax.experimental.pallas import tpu as pltpu
from jax.experimental.pallas import tpu_sc as plsc

F32 = jnp.float32
BF16 = jnp.bfloat16
U32 = jnp.uint32
EPS = 1e-6
HIGHEST = lax.Precision.HIGHEST

POOL_WINDOWS = (2, 4, 8, 16)
POOL_GROUP_DIM = 128
POOL_DIM = len(POOL_WINDOWS) * POOL_GROUP_DIM
POOL_BUF = max(POOL_WINDOWS) - 1
N_HEADS = 8
HEAD_DIM = 128
QK_DIM = N_HEADS * HEAD_DIM
CONV_WIDTH = 4
CONV_DIM = 3 * QK_DIM
N_GROUPS = 4
EXPERTS_PER_GROUP = 8
N_EXPERTS = N_GROUPS * EXPERTS_PER_GROUP
LANES = 128
SUBLANES = 8

OFF_Z = 0
OFF_QKV = OFF_Z + QK_DIM
OFF_POOL = OFF_QKV + CONV_DIM
OFF_B = OFF_POOL + POOL_DIM
OFF_A = OFF_B + LANES
MIX_COLS = OFF_A + LANES
GATE_COLS = 2048
IN_COLS = MIX_COLS + GATE_COLS

POOL_ROWS = 16
CONV_ROWS = 8
VMEM_LIMIT = 60 * 1024 * 1024
EXPERT_PART_ROWS = 256
CONV_GROUPS = 12
SC_ROWS = 96


def _sigmoid(x):
    return 1.0 / (1.0 + jnp.exp(-x))


def _softplus(x):
    return jnp.maximum(x, 0.0) + jnp.log1p(jnp.exp(-jnp.abs(x)))


def _rms(x, w):
    return x * lax.rsqrt(jnp.mean(x * x, axis=-1, keepdims=True) + EPS) * w


def _dot(a, b):
    return jnp.dot(a.astype(BF16), b.astype(BF16), preferred_element_type=F32)


def _dot_nt(a, b):
    return lax.dot_general(a.astype(BF16), b.astype(BF16), (((1,), (1,)), ((), ())), preferred_element_type=F32)


def _dot_tn(a, b):
    return lax.dot_general(a.astype(BF16), b.astype(BF16), (((0,), (0,)), ((), ())), preferred_element_type=F32)


def _pack_pairs(x):
    c = x.shape[1] // 2
    hi = pltpu.bitcast(x[:, :c].astype(BF16).astype(F32), U32)
    lo = pltpu.bitcast(x[:, c:].astype(BF16).astype(F32), U32)
    return hi | lax.shift_right_logical(lo, jnp.uint32(16))


def _unpack_pairs(w):
    left = pltpu.bitcast(w & jnp.uint32(0xFFFF0000), F32)
    right = pltpu.bitcast(lax.shift_left(w, jnp.uint32(16)), F32)
    return left, right


def _inproj_kernel(x_ref, nw_ref, w_ref, mix_ref, gate_ref):
    h = _rms(x_ref[...], nw_ref[...]).astype(BF16)
    half = MIX_COLS // 2
    for c0 in (0, half):
        mix_ref[:, c0:c0 + half] = jnp.dot(h, w_ref[:, c0:c0 + half], preferred_element_type=F32)
    half = GATE_COLS // 2
    for c0 in (0, half):
        gate_ref[:, c0:c0 + half] = jnp.dot(h, w_ref[:, MIX_COLS + c0:MIX_COLS + c0 + half],
                                            preferred_element_type=F32)


def _inproj(x, norm_w, w_in_r, tm):
    n, d = x.shape
    return pl.pallas_call(
        _inproj_kernel,
        grid=(n // tm,),
        in_specs=[
            pl.BlockSpec((tm, d), lambda i: (i, 0)),
            pl.BlockSpec((1, d), lambda i: (0, 0)),
            pl.BlockSpec((d, IN_COLS), lambda i: (0, 0)),
        ],
        out_specs=[pl.BlockSpec((tm, MIX_COLS), lambda i: (i, 0)), pl.BlockSpec((tm, GATE_COLS), lambda i: (i, 0))],
        out_shape=[jax.ShapeDtypeStruct((n, MIX_COLS), F32), jax.ShapeDtypeStruct((n, GATE_COLS), F32)],
        compiler_params=pltpu.CompilerParams(dimension_semantics=("arbitrary",), vmem_limit_bytes=VMEM_LIMIT),
        name="inproj",
    )(x, norm_w, w_in_r)


def _conv_silu_qkv(ext_c, seq, tt, wconv_ref, ci, col0=0):
    base = CONV_ROWS - (CONV_WIDTH - 1)
    cs = slice(ci * LANES, (ci + 1) * LANES)
    cl = slice(ci * LANES - col0, (ci + 1) * LANES - col0)
    idx = (lambda r: (r, cl)) if seq is None else (lambda r: (seq, r, cl))
    acc = ext_c[idx(slice(base, base + tt))] * wconv_ref[0:1, cs]
    for j in range(1, CONV_WIDTH):
        acc = acc + ext_c[idx(slice(base + j, base + j + tt))] * wconv_ref[j:j + 1, cs]
    y = acc * _sigmoid(acc)
    if ci < 2 * N_HEADS:
        y = y * lax.rsqrt(jnp.sum(y * y, axis=-1, keepdims=True) + EPS)
    if ci < N_HEADS:
        y = y * (HEAD_DIM ** -0.5)
    return y


def _inproj_conv_kernel(x_ref, nw_ref, w_ref, wconv_ref, conv0_ref, mix_ref, gate_ref, nconv_ref, h_s, *ext, n_tiles):
    t = pl.program_id(1)
    tm = x_ref.shape[1]
    base = CONV_ROWS - (CONV_WIDTH - 1)
    step = CONV_DIM // len(ext)

    @pl.when(t == 0)
    def _():
        for g, buf in enumerate(ext):
            buf[base:CONV_ROWS, :] = conv0_ref[0, :, g * step:(g + 1) * step]

    h_s[...] = _rms(x_ref[0], nw_ref[...]).astype(BF16)

    def proj(c0, c1):
        return jnp.dot(h_s[...], w_ref[:, c0:c1], preferred_element_type=F32)

    def fill_z0():
        mix_ref[0, :, OFF_Z:OFF_Z + QK_DIM // 2] = proj(OFF_Z, OFF_Z + QK_DIM // 2)

    def fill_z1():
        mix_ref[0, :, OFF_Z + QK_DIM // 2:OFF_QKV] = proj(OFF_Z + QK_DIM // 2, OFF_QKV)

    def fill_pool():
        mix_ref[0, :, OFF_POOL:MIX_COLS] = proj(OFF_POOL, MIX_COLS)

    def fill_gate(i):
        q = GATE_COLS // 4
        gate_ref[0, :, i * q:(i + 1) * q] = proj(MIX_COLS + i * q, MIX_COLS + (i + 1) * q)

    fillers = [fill_z0, fill_z1, fill_pool] + [functools.partial(fill_gate, i) for i in range(4)]
    for g, buf in enumerate(ext):
        c0 = g * step
        buf[CONV_ROWS:CONV_ROWS + tm, :] = proj(OFF_QKV + c0, OFF_QKV + c0 + step)
        if fillers:
            fillers.pop(0)()
        for ci in range(c0 // LANES, (c0 + step) // LANES):
            mix_ref[0, :, OFF_QKV + ci * LANES:OFF_QKV + (ci + 1) * LANES] = _conv_silu_qkv(
                buf, None, tm, wconv_ref, ci, c0)
    for f in fillers:
        f()

    @pl.when(t == n_tiles - 1)
    def _():
        for g, buf in enumerate(ext):
            nconv_ref[0, :, g * step:(g + 1) * step] = buf[tm + base:tm + CONV_ROWS, :]

    for buf in ext:
        buf[0:CONV_ROWS, :] = buf[tm:tm + CONV_ROWS, :]


def _inproj_conv(x, conv0, norm_w, w_in_r, wconv, tm):
    n_seq, seq_len, d = x.shape
    n_tiles = seq_len // tm
    tile = lambda b, t: (b, t, 0)
    const = lambda b, t: (0, 0)
    mix, gates, nconv = pl.pallas_call(
        functools.partial(_inproj_conv_kernel, n_tiles=n_tiles),
        grid=(n_seq, n_tiles),
        in_specs=[
            pl.BlockSpec((1, tm, d), tile),
            pl.BlockSpec((1, d), const),
            pl.BlockSpec((d, IN_COLS), const),
            pl.BlockSpec((SUBLANES, CONV_DIM), const),
            pl.BlockSpec((1, CONV_WIDTH - 1, CONV_DIM), lambda b, t: (b, 0, 0)),
        ],
        out_specs=[
            pl.BlockSpec((1, tm, MIX_COLS), tile),
            pl.BlockSpec((1, tm, GATE_COLS), tile),
            pl.BlockSpec((1, CONV_WIDTH - 1, CONV_DIM), lambda b, t: (b, 0, 0)),
        ],
        out_shape=[
            jax.ShapeDtypeStruct((n_seq, seq_len, MIX_COLS), F32),
            jax.ShapeDtypeStruct((n_seq, seq_len, GATE_COLS), F32),
            jax.ShapeDtypeStruct((n_seq, CONV_WIDTH - 1, CONV_DIM), F32),
        ],
        scratch_shapes=[pltpu.VMEM((tm, d), BF16)] + [
            pltpu.VMEM((CONV_ROWS + tm, CONV_DIM // CONV_GROUPS), F32) for _ in range(CONV_GROUPS)],
        compiler_params=pltpu.CompilerParams(dimension_semantics=("arbitrary", "arbitrary"),
                                             vmem_limit_bytes=VMEM_LIMIT),
        name="inproj_conv",
    )(x, norm_w, w_in_r, wconv, conv0)
    n = n_seq * seq_len
    return mix.reshape(n, MIX_COLS), gates.reshape(n, GATE_COLS), nconv


def _mixer_kernel(proj_ref, pool0_ref, conv0_ref, s0_ref, wpool_ref, pscale_ref, wconv_ref, alog_ref, dtb_ref, onw_ref,
                  ypool_ref, ydelta_ref, npool_ref, nconv_ref, ns_ref,
                  ext_p, ext_c, q_s, k_s, v_s, g_s, b_s, state,
                  *, sb, tt, tv, chunk, pos0, n_tiles, qkv_done):
    t = pl.program_id(1)
    base = CONV_ROWS - (CONV_WIDTH - 1)

    @pl.when(t == 0)
    def _():
        ext_p[:, POOL_ROWS - POOL_BUF:POOL_ROWS, :] = pool0_ref[...]
        if not qkv_done:
            ext_c[:, base:CONV_ROWS, :] = conv0_ref[...]
        state[...] = s0_ref[...]

    row = lax.broadcasted_iota(jnp.int32, (tt, 1), 0)
    pos = pos0 + t * tt + row
    for s in range(sb):
        ext_p[s, POOL_ROWS:POOL_ROWS + tt, :] = proj_ref[s, :, OFF_POOL:OFF_POOL + POOL_DIM]
        if not qkv_done:
            ext_c[s, CONV_ROWS:CONV_ROWS + tt, :] = proj_ref[s, :, OFF_QKV:OFF_QKV + CONV_DIM]

        for gi, win in enumerate(POOL_WINDOWS):
            cs = slice(gi * POOL_GROUP_DIM, (gi + 1) * POOL_GROUP_DIM)
            u = ext_p[s, POOL_ROWS:POOL_ROWS + tt, cs]
            acc = u
            for j in range(1, win):
                acc = acc + ext_p[s, POOL_ROWS - j:POOL_ROWS - j + tt, cs]
            cnt = jnp.minimum(win, pos + 1).astype(F32)
            pooled = acc / cnt - u
            ypool_ref[s, :, cs] = _dot(pooled, wpool_ref[gi]) * pscale_ref[:, cs]

        for ci in range(0 if qkv_done else CONV_DIM // LANES):
            hs = slice((ci % N_HEADS) * HEAD_DIM, (ci % N_HEADS + 1) * HEAD_DIM)
            (q_s, k_s, v_s)[ci // N_HEADS][s, :, hs] = _conv_silu_qkv(ext_c, s, tt, wconv_ref, ci)

        beta = _sigmoid(proj_ref[s, :, OFF_B:OFF_B + LANES])
        g = -jnp.exp(alog_ref[...]) * _softplus(proj_ref[s, :, OFF_A:OFF_A + LANES] + dtb_ref[...])
        if tv < tt:
            beta = jnp.where(row < tv, beta, 0.0)
            g = jnp.where(row < tv, g, 0.0)
        b_s[s] = beta
        g_s[s] = g

    @pl.when(t == n_tiles - 1)
    def _():
        npool_ref[...] = ext_p[:, tv + POOL_ROWS - POOL_BUF:tv + POOL_ROWS, :]
        if qkv_done:
            nconv_ref[...] = jnp.zeros_like(nconv_ref)
        else:
            nconv_ref[...] = ext_c[:, tv + base:tv + CONV_ROWS, :]

    if n_tiles > 1:
        ext_p[:, 0:POOL_ROWS, :] = ext_p[:, tt:tt + POOL_ROWS, :]
        if not qkv_done:
            ext_c[:, 0:CONV_ROWS, :] = ext_c[:, tt:tt + CONV_ROWS, :]

    ri = lax.broadcasted_iota(jnp.int32, (chunk, chunk), 0)
    cj = lax.broadcasted_iota(jnp.int32, (chunk, chunk), 1)
    causal = ri >= cj
    strict = ri > cj
    ltri = causal.astype(F32)
    eye = (ri == cj).astype(F32)
    n_doublings = chunk.bit_length() - 2
    units = [(s, h) for s in range(sb) for h in range(N_HEADS)]
    hsl = [slice(h * HEAD_DIM, (h + 1) * HEAD_DIM) for h in range(N_HEADS)]

    def qkv(which, s, rows, h):
        if qkv_done:
            c0 = OFF_QKV + which * QK_DIM + h * HEAD_DIM
            return proj_ref[s, rows, c0:c0 + HEAD_DIM]
        return (q_s, k_s, v_s)[which][s, rows, hsl[h]]

    def chunk_body(ci, carry):
        r0 = pl.multiple_of(ci * chunk, chunk)
        rows = pl.ds(r0, chunk)
        g_all, g_t, b_all = [], [], []
        for s in range(sb):
            ga = jnp.dot(ltri, g_s[s, rows, :], precision=HIGHEST, preferred_element_type=F32)
            g_all.append(ga)
            if chunk < LANES:
                ga = jnp.concatenate([ga, jnp.zeros((LANES - chunk, LANES), F32)], axis=0)
            g_t.append(ga.T)
            b_all.append(b_s[s, rows, :])
        g_col = [g_all[s][:, h:h + 1] for s, h in units]
        dec = [jnp.exp(jnp.minimum(g_col[i] - g_t[s][h:h + 1, 0:chunk], 0.0)) for i, (s, h) in enumerate(units)]
        b_col = [b_all[s][:, h:h + 1] for s, h in units]
        q = [qkv(0, s, rows, h) for s, h in units]
        k = [qkv(1, s, rows, h) for s, h in units]
        idx = range(len(units))
        kb = [k[i] * b_col[i] for i in idx]
        kbk = [_dot_nt(kb[i], k[i]) for i in idx]
        qk = [_dot_nt(q[i], k[i]) for i in idx]
        p_mat = [jnp.where(strict, kbk[i] * dec[i], 0.0) for i in idx]
        t_inv = [eye - p_mat[i] for i in idx]
        for _ in range(n_doublings):
            p_mat = [_dot(p_mat[i], p_mat[i]) for i in idx]
            xp = [_dot(t_inv[i], p_mat[i]) for i in idx]
            t_inv = [t_inv[i] + xp[i] for i in idx]
        g_exp = [jnp.exp(g_col[i]) for i in idx]
        wu = [_dot(t_inv[i], jnp.concatenate([kb[i] * g_exp[i], qkv(2, s, rows, h) * b_col[i]], axis=1))
              for i, (s, h) in enumerate(units)]
        s_old = [state[s, h] for s, h in units]
        res = [_dot(jnp.concatenate([wu[i][:, 0:HEAD_DIM], q[i] * g_exp[i]], axis=0), s_old[i]) for i in idx]
        v_new = [wu[i][:, HEAD_DIM:2 * HEAD_DIM] - res[i][0:chunk] for i in idx]
        a_qk = [jnp.where(causal, qk[i] * dec[i], 0.0) for i in idx]
        g_last = [g_all[s][chunk - 1:chunk, h:h + 1] for s, h in units]
        k_dec = [k[i] * jnp.exp(g_last[i] - g_col[i]) for i in idx]
        intra = [_dot(a_qk[i], v_new[i]) for i in idx]
        upd = [_dot_tn(k_dec[i], v_new[i]) for i in idx]
        for i, (s, h) in enumerate(units):
            state[s, h] = s_old[i] * jnp.exp(g_last[i]) + upd[i]
            o_c = res[i][chunk:2 * chunk] + intra[i]
            z_c = proj_ref[s, rows, OFF_Z + h * HEAD_DIM:OFF_Z + (h + 1) * HEAD_DIM]
            ydelta_ref[s, rows, hsl[h]] = _rms(o_c, onw_ref[...]) * (z_c * _sigmoid(z_c))
        return carry

    lax.fori_loop(0, tt // chunk, chunk_body, 0)

    @pl.when(t == n_tiles - 1)
    def _():
        ns_ref[...] = state[...]


def _mixers(proj, pool0, conv0, s0, wpool, pscale, wconv, alog, dtb, onw, *, n_seq, sb, tt, tv, chunk, pos0, n_tiles,
            qkv_done):
    kern = functools.partial(_mixer_kernel, sb=sb, tt=tt, tv=tv, chunk=chunk, pos0=pos0, n_tiles=n_tiles,
                             qkv_done=qkv_done)
    seq_len = n_tiles * tt
    tile = lambda b, t: (b, t, 0)
    seq3 = lambda b, t: (b, 0, 0)
    seq4 = lambda b, t: (b, 0, 0, 0)
    const2 = lambda b, t: (0, 0)
    ypool, ydelta, npool, nconv, ns = pl.pallas_call(
        kern,
        grid=(n_seq // sb, n_tiles),
        in_specs=[
            pl.BlockSpec((sb, tt, MIX_COLS), tile),
            pl.BlockSpec((sb, POOL_BUF, POOL_DIM), seq3),
            pl.BlockSpec((sb, CONV_WIDTH - 1, CONV_DIM), seq3),
            pl.BlockSpec((sb, N_HEADS, HEAD_DIM, HEAD_DIM), seq4),
            pl.BlockSpec((len(POOL_WINDOWS), POOL_GROUP_DIM, POOL_GROUP_DIM), lambda b, t: (0, 0, 0)),
            pl.BlockSpec((1, POOL_DIM), const2),
            pl.BlockSpec((SUBLANES, CONV_DIM), const2),
            pl.BlockSpec((1, LANES), const2),
            pl.BlockSpec((1, LANES), const2),
            pl.BlockSpec((1, HEAD_DIM), const2),
        ],
        out_specs=[
            pl.BlockSpec((sb, tt, POOL_DIM), tile),
            pl.BlockSpec((sb, tt, QK_DIM), tile),
            pl.BlockSpec((sb, POOL_BUF, POOL_DIM), seq3),
            pl.BlockSpec((sb, CONV_WIDTH - 1, CONV_DIM), seq3),
            pl.BlockSpec((sb, N_HEADS, HEAD_DIM, HEAD_DIM), seq4),
        ],
        out_shape=[
            jax.ShapeDtypeStruct((n_seq, seq_len, POOL_DIM), F32),
            jax.ShapeDtypeStruct((n_seq, seq_len, QK_DIM), F32),
            jax.ShapeDtypeStruct((n_seq, POOL_BUF, POOL_DIM), F32),
            jax.ShapeDtypeStruct((n_seq, CONV_WIDTH - 1, CONV_DIM), F32),
            jax.ShapeDtypeStruct((n_seq, N_HEADS, HEAD_DIM, HEAD_DIM), F32),
        ],
        scratch_shapes=[
            pltpu.VMEM((sb, POOL_ROWS + tt, POOL_DIM), F32),
            pltpu.VMEM((sb, CONV_ROWS + tt, CONV_DIM), F32),
            pltpu.VMEM((sb, tt, QK_DIM), F32),
            pltpu.VMEM((sb, tt, QK_DIM), F32),
            pltpu.VMEM((sb, tt, QK_DIM), F32),
            pltpu.VMEM((sb, tt, LANES), F32),
            pltpu.VMEM((sb, tt, LANES), F32),
            pltpu.VMEM((sb, N_HEADS, HEAD_DIM, HEAD_DIM), F32),
        ],
        compiler_params=pltpu.CompilerParams(dimension_semantics=("arbitrary", "arbitrary"),
                                             vmem_limit_bytes=VMEM_LIMIT),
        name="mixers",
    )(proj.reshape(n_seq, seq_len, MIX_COLS), pool0, conv0, s0, wpool, pscale, wconv, alog, dtb, onw)
    n = n_seq * seq_len
    return ypool.reshape(n, POOL_DIM), ydelta.reshape(n, QK_DIM), npool, nconv, ns


def _merge_kernel(x_ref, gates_ref, ypool_ref, ydelta_ref, wbp_ref, wbd_ref, wout_ref, n2w_ref, wr_hi_ref, wr_lo_ref,
                  br_ref, cnt0_ref, xm_ref, h2_ref, route_ref, wk_ref, cnt_ref, cnt_s, *, period, valid):
    tm = x_ref.shape[0]

    @pl.when(pl.program_id(0) == 0)
    def _():
        cnt_s[...] = cnt0_ref[...]

    bp = _dot(ypool_ref[...], wbp_ref[...])
    bd = _dot(ydelta_ref[...], wbd_ref[...])
    merged = _sigmoid(gates_ref[:, 0:1024]) * bp + _sigmoid(gates_ref[:, 1024:2048]) * bd
    xm = x_ref[...] + _dot(merged, wout_ref[...])
    xm_ref[...] = xm
    h2 = _rms(xm, n2w_ref[...])
    h2_ref[...] = _pack_pairs(h2)

    h2_hi = h2.astype(BF16)
    h2_lo = (h2 - h2_hi.astype(F32)).astype(BF16)
    logits = (jnp.dot(h2_hi, wr_hi_ref[...], preferred_element_type=F32)
              + (jnp.dot(h2_hi, wr_lo_ref[...], preferred_element_type=F32)
                 + jnp.dot(h2_lo, wr_hi_ref[...], preferred_element_type=F32))) + br_ref[...]
    lane = lax.broadcasted_iota(jnp.int32, logits.shape, 1)
    neg = -jnp.inf
    far = LANES - 1
    is_g = lane < N_GROUPS
    g_max = jnp.max(jnp.where(is_g, logits, neg), axis=-1, keepdims=True)
    g_sel = jnp.min(jnp.where(is_g & (logits == g_max), lane, far), axis=-1, keepdims=True)
    p_g = 1.0 / jnp.sum(jnp.where(is_g, jnp.exp(logits - g_max), 0.0), axis=-1, keepdims=True)
    e_lane = lane - N_GROUPS
    is_e = (e_lane >= 0) & (e_lane < N_EXPERTS) & ((e_lane // EXPERTS_PER_GROUP) == g_sel)
    ev = jnp.where(is_e, logits, neg)
    v1 = jnp.max(ev, axis=-1, keepdims=True)
    i1 = jnp.min(jnp.where(is_e & (ev == v1), lane, far), axis=-1, keepdims=True)
    is_e2 = is_e & (lane != i1)
    ev2 = jnp.where(is_e2, logits, neg)
    v2 = jnp.max(ev2, axis=-1, keepdims=True)
    i2 = jnp.min(jnp.where(is_e2 & (ev2 == v2), lane, far), axis=-1, keepdims=True)
    e21 = jnp.exp(v2 - v1)
    w1 = p_g / (1.0 + e21)
    w2 = p_g * e21 / (1.0 + e21)
    e1 = i1 - N_GROUPS
    e2 = i2 - N_GROUPS
    chosen = (lane == e1) | (lane == e2)
    routed = jnp.ones((tm, 1), F32)
    if valid < period:
        row = lax.broadcasted_iota(jnp.int32, (tm, 1), 0)
        is_real = lax.rem(row, period) < valid
        chosen = chosen & is_real
        routed = jnp.where(is_real, 1.0, 0.0)
    onehot = jnp.where(chosen, 1.0, 0.0)
    rr = lax.broadcasted_iota(jnp.int32, (tm, tm), 0)
    cc = lax.broadcasted_iota(jnp.int32, (tm, tm), 1)
    earlier = jnp.where(rr > cc, 1.0, 0.0).astype(BF16)
    before = jnp.dot(earlier, onehot.astype(BF16), preferred_element_type=F32) + cnt_s[...]
    r1 = jnp.sum(jnp.where(lane == e1, before, 0.0), axis=-1, keepdims=True)
    r2 = jnp.sum(jnp.where(lane == e2, before, 0.0), axis=-1, keepdims=True)
    cnt_s[...] = cnt_s[...] + jnp.sum(onehot, axis=0, keepdims=True)
    cnt_ref[...] = cnt_s[...]
    record = jnp.where(lane == 0, e1.astype(F32), jnp.where(lane == 1, e2.astype(F32), jnp.where(
        lane == 2, r1, jnp.where(lane == 3, r2, jnp.where(lane == 4, routed, 0.0)))))
    route_ref[...] = record.T[0:SUBLANES, :]
    lane8 = lax.broadcasted_iota(jnp.int32, (tm, SUBLANES), 1)
    wk_ref[...] = jnp.where(lane8 == 0, w1, jnp.where(lane8 == 1, w2, 0.0))


def _merge(x, proj, ypool, ydelta, wbp, wbd, wout, n2w, wr_hi, wr_lo, br, cnt0, tm, period, valid):
    n, d = x.shape
    row = lambda i: (i, 0)
    const = lambda i: (0, 0)
    return pl.pallas_call(
        functools.partial(_merge_kernel, period=period, valid=valid),
        grid=(n // tm,),
        in_specs=[
            pl.BlockSpec((tm, d), row),
            pl.BlockSpec((tm, GATE_COLS), row),
            pl.BlockSpec((tm, POOL_DIM), row),
            pl.BlockSpec((tm, QK_DIM), row),
            pl.BlockSpec((POOL_DIM, d), const),
            pl.BlockSpec((QK_DIM, d), const),
            pl.BlockSpec((d, d), const),
            pl.BlockSpec((1, d), const),
            pl.BlockSpec((d, LANES), const),
            pl.BlockSpec((d, LANES), const),
            pl.BlockSpec((1, LANES), const),
            pl.BlockSpec((1, LANES), const),
        ],
        out_specs=[
            pl.BlockSpec((tm, d), row),
            pl.BlockSpec((tm, d // 2), row),
            pl.BlockSpec((SUBLANES, tm), lambda i: (0, i)),
            pl.BlockSpec((tm, SUBLANES), row),
            pl.BlockSpec((1, LANES), const),
        ],
        out_shape=[
            jax.ShapeDtypeStruct((n, d), F32),
            jax.ShapeDtypeStruct((n, d // 2), U32),
            jax.ShapeDtypeStruct((SUBLANES, n), F32),
            jax.ShapeDtypeStruct((n, SUBLANES), F32),
            jax.ShapeDtypeStruct((1, LANES), F32),
        ],
        scratch_shapes=[pltpu.VMEM((1, LANES), F32)],
        compiler_params=pltpu.CompilerParams(dimension_semantics=("arbitrary",), vmem_limit_bytes=VMEM_LIMIT),
        name="merge",
    )(x, proj, ypool, ydelta, wbp, wbd, wout, n2w, wr_hi, wr_lo, br, cnt0)


def _expert_kernel(te_ref, nv_ref, x_ref, wg_hbm, wu_hbm, wd_hbm, y_ref, st_g, st_u, st_d, wg_s, wu_s, wd_s, sem, slot_s):
    i = pl.program_id(0)
    n_tiles = pl.num_programs(0)
    n_valid = nv_ref[i]
    expert = te_ref[i]

    def weight_copies(e, slot):
        return [pltpu.make_async_copy(src.at[e], dst.at[slot], sem.at[slot])
                for src, dst in ((wg_hbm, st_g), (wu_hbm, st_u), (wd_hbm, st_d))]

    @pl.when(i == 0)
    def _():
        slot_s[0] = 0
        for cp in weight_copies(expert, 0):
            cp.start()

    @pl.when((i == 0) | (expert != te_ref[jnp.maximum(i - 1, 0)]))
    def _():
        slot = slot_s[0]
        for cp in weight_copies(expert, slot):
            cp.wait()
        wg_s[...] = st_g[slot].astype(BF16)
        wu_s[...] = st_u[slot].astype(BF16)
        wd_s[...] = st_d[slot].astype(BF16)
        nxt = lax.while_loop(lambda j: (j < n_tiles) & (te_ref[jnp.minimum(j, n_tiles - 1)] == expert),
                             lambda j: j + 1, i + 1)

        @pl.when(nxt < n_tiles)
        def _():
            for cp in weight_copies(te_ref[jnp.minimum(nxt, n_tiles - 1)], 1 - slot):
                cp.start()
        slot_s[0] = 1 - slot

    @pl.when(n_valid > 0)
    def _():
        tm = x_ref.shape[0]
        part = min(tm, EXPERT_PART_ROWS)
        parts = [slice(r0, r0 + part) for r0 in range(0, tm, part)]
        row = lax.broadcasted_iota(jnp.int32, (part, 1), 0)
        xs = []
        for p in parts:
            left, right = _unpack_pairs(jnp.where(row + p.start < n_valid, x_ref[p, :], jnp.uint32(0)))
            xs.append(jnp.concatenate([left.astype(BF16), right.astype(BF16)], axis=1))
        a = [_dot(x, wg_s[...]) for x in xs]
        b = [_dot(x, wu_s[...]) for x in xs]
        act = [(a[j] * _sigmoid(a[j])) * b[j] for j in range(len(parts))]
        y = [_dot(act[j], wd_s[...]) for j in range(len(parts))]
        for j, p in enumerate(parts):
            y_ref[p, :] = _pack_pairs(y[j])

    @pl.when(n_valid == 0)
    def _():
        y_ref[...] = jnp.zeros_like(y_ref)


def _experts(tile_expert, tile_valid, xs, wg, wu, wd, tm):
    n_rows, dp = xs.shape
    d, f = wg.shape[1], wg.shape[2]
    grid_spec = pltpu.PrefetchScalarGridSpec(
        num_scalar_prefetch=2,
        grid=(n_rows // tm,),
        in_specs=[
            pl.BlockSpec((tm, dp), lambda i, te, nv: (jnp.where(nv[i] > 0, i, 0), 0)),
            pl.BlockSpec(memory_space=pl.ANY),
            pl.BlockSpec(memory_space=pl.ANY),
            pl.BlockSpec(memory_space=pl.ANY),
        ],
        out_specs=pl.BlockSpec((tm, dp), lambda i, te, nv: (i, 0)),
        scratch_shapes=[pltpu.VMEM((2, d, f), F32), pltpu.VMEM((2, d, f), F32), pltpu.VMEM((2, f, d), F32),
                        pltpu.VMEM((d, f), BF16), pltpu.VMEM((d, f), BF16), pltpu.VMEM((f, d), BF16),
                        pltpu.SemaphoreType.DMA((2,)), pltpu.SMEM((1,), jnp.int32)],
    )
    return pl.pallas_call(
        _expert_kernel,
        grid_spec=grid_spec,
        out_shape=jax.ShapeDtypeStruct((n_rows, dp), U32),
        compiler_params=pltpu.CompilerParams(dimension_semantics=("arbitrary",), vmem_limit_bytes=VMEM_LIMIT),
        name="experts",
    )(tile_expert, tile_valid, xs, wg, wu, wd)


def _plan_kernel(route_ref, cnt_ref, pos_ref, tiles_ref, *, tm_e, n_rows):
    counts = cnt_ref[...]
    padded = jnp.floor((counts + (tm_e - 1)) / tm_e) * tm_e
    lane = lax.broadcasted_iota(jnp.int32, counts.shape, 1)
    ends = padded
    shift = 1
    while shift < N_EXPERTS:
        ends = ends + jnp.where(lane >= shift, pltpu.roll(ends, shift, axis=1), 0.0)
        shift *= 2
    starts = ends - padded

    def lookup(table, key):
        out = jnp.zeros_like(key)
        for e in range(N_EXPERTS):
            out = out + jnp.where(key == e, table[:, e:e + 1], 0.0)
        return out

    routed = route_ref[4:5, :] > 0.0
    for k in range(2):
        pos = lookup(starts, route_ref[k:k + 1, :]) + route_ref[2 + k:3 + k, :]
        pos_ref[k:k + 1, :] = jnp.where(routed, pos, n_rows - 1.0).astype(jnp.int32)

    tile_start = lax.broadcasted_iota(jnp.int32, (1, tiles_ref.shape[1]), 1).astype(F32) * tm_e
    tile_expert = jnp.zeros_like(tile_start)
    for e in range(N_EXPERTS):
        tile_expert = tile_expert + jnp.where(ends[:, e:e + 1] <= tile_start, 1.0, 0.0)
    tile_expert = jnp.minimum(tile_expert, N_EXPERTS - 1.0)
    valid = lookup(counts, tile_expert) - (tile_start - lookup(starts, tile_expert))
    tiles_ref[0:1, :] = tile_expert.astype(jnp.int32)
    tiles_ref[1:2, :] = jnp.clip(valid, 0.0, tm_e).astype(jnp.int32)


def _plan(route_t, counts, tm_e, n_tiles):
    n = route_t.shape[1]
    tiles_pad = -(-n_tiles // LANES) * LANES
    return pl.pallas_call(
        functools.partial(_plan_kernel, tm_e=tm_e, n_rows=n_tiles * tm_e),
        out_shape=[jax.ShapeDtypeStruct((2, n), jnp.int32), jax.ShapeDtypeStruct((2, tiles_pad), jnp.int32)],
        name="plan",
    )(route_t, counts)


def _sc_workers():
    info = plsc.get_sparse_core_info()
    return info.num_cores, info.num_subcores


def _sc_chunk(per_worker):
    assert per_worker % SUBLANES == 0
    return max(c for c in range(SUBLANES, SC_ROWS + 1, SUBLANES) if per_worker % c == 0)


def _sc_scatter2(rows, idx_a, idx_b, n_out):
    n_cores, n_sub = _sc_workers()
    n, d = rows.shape
    per_worker = n // (n_cores * n_sub)
    assert per_worker * n_cores * n_sub == n
    chunk = _sc_chunk(per_worker)
    mesh = plsc.VectorSubcoreMesh(core_axis_name="c", subcore_axis_name="s")

    n_chunks = per_worker // chunk
    buf = lambda shape, dtype: [pltpu.VMEM(shape, dtype), pltpu.VMEM(shape, dtype)]

    @functools.partial(
        pl.kernel, mesh=mesh, out_type=jax.ShapeDtypeStruct((n_out, d), rows.dtype),
        scratch_types=buf((chunk,), jnp.int32) + buf((chunk,), jnp.int32) + buf((chunk, d), rows.dtype)
        + [pltpu.SemaphoreType.DMA, pltpu.SemaphoreType.DMA])
    def scatter_rows(rows_hbm, ia_hbm, ib_hbm, out_hbm, ia0, ia1, ib0, ib1, r0, r1, sem0, sem1):
        worker = lax.axis_index("s") * n_cores + lax.axis_index("c")
        base = worker * per_worker
        slots = ((ia0, ib0, r0, sem0), (ia1, ib1, r1, sem1))
        pending = {}
        for j in range(n_chunks):
            ia_v, ib_v, rows_v, sem = slots[j % 2]
            if j >= 2:
                for cp in pending.pop(j - 2):
                    cp.wait()
            off = pl.multiple_of(base + j * chunk, SUBLANES)
            pltpu.sync_copy(ia_hbm.at[pl.ds(off, chunk)], ia_v)
            pltpu.sync_copy(ib_hbm.at[pl.ds(off, chunk)], ib_v)
            pltpu.sync_copy(rows_hbm.at[pl.ds(off, chunk)], rows_v)
            pending[j] = (pltpu.async_copy(rows_v, out_hbm.at[ia_v], sem),
                          pltpu.async_copy(rows_v, out_hbm.at[ib_v], sem))
        for j in sorted(pending):
            for cp in pending[j]:
                cp.wait()

    return scatter_rows(rows, idx_a, idx_b)


def _sc_gather(table, idx):
    n_cores, n_sub = _sc_workers()
    n_idx = idx.shape[0]
    d = table.shape[1]
    per_worker = n_idx // (n_cores * n_sub)
    assert per_worker * n_cores * n_sub == n_idx
    chunk = _sc_chunk(per_worker)
    mesh = plsc.VectorSubcoreMesh(core_axis_name="c", subcore_axis_name="s")

    n_chunks = per_worker // chunk

    @functools.partial(
        pl.kernel, mesh=mesh, out_type=jax.ShapeDtypeStruct((n_idx, d), table.dtype),
        scratch_types=[pltpu.VMEM((chunk,), jnp.int32), pltpu.VMEM((chunk,), jnp.int32),
                       pltpu.VMEM((chunk, d), table.dtype), pltpu.VMEM((chunk, d), table.dtype),
                       pltpu.SemaphoreType.DMA, pltpu.SemaphoreType.DMA])
    def gather_rows(table_hbm, idx_hbm, out_hbm, i0, i1, r0, r1, sem0, sem1):
        worker = lax.axis_index("s") * n_cores + lax.axis_index("c")
        base = worker * per_worker
        slots = ((i0, r0, sem0), (i1, r1, sem1))

        def start(j):
            idx_v, rows_v, sem = slots[j % 2]
            off = pl.multiple_of(base + j * chunk, SUBLANES)
            pltpu.sync_copy(idx_hbm.at[pl.ds(off, chunk)], idx_v)
            return pltpu.async_copy(table_hbm.at[idx_v], rows_v, sem)

        nxt = start(0)
        for j in range(n_chunks):
            cur = nxt
            if j + 1 < n_chunks:
                nxt = start(j + 1)
            cur.wait()
            off = pl.multiple_of(base + j * chunk, SUBLANES)
            pltpu.sync_copy(slots[j % 2][1], out_hbm.at[pl.ds(off, chunk)])

    return gather_rows(table, idx)


def _finalize_kernel(xm_ref, ya_ref, yb_ref, wk_ref, fnw_ref, o_ref):
    d = xm_ref.shape[1]
    c = d // 2
    w = wk_ref[...]
    a_left, a_right = _unpack_pairs(ya_ref[...])
    b_left, b_right = _unpack_pairs(yb_ref[...])
    x_left = xm_ref[:, 0:c] + (w[:, 0:1] * a_left + w[:, 1:2] * b_left)
    x_right = xm_ref[:, c:d] + (w[:, 0:1] * a_right + w[:, 1:2] * b_right)
    ms = (jnp.sum(x_left * x_left, axis=-1, keepdims=True) + jnp.sum(x_right * x_right, axis=-1, keepdims=True)) / d
    scale = lax.rsqrt(ms + EPS)
    o_ref[:, 0:c] = x_left * scale * fnw_ref[:, 0:c]
    o_ref[:, c:d] = x_right * scale * fnw_ref[:, c:d]


def _finalize(xm, yk, wk, fnw, tm, row0):
    n, d = xm.shape
    steps = n // tm
    first = row0 // tm
    second = (yk.shape[0] // 2 + row0) // tm
    return pl.pallas_call(
        _finalize_kernel,
        grid=(steps,),
        in_specs=[
            pl.BlockSpec((tm, d), lambda i: (i, 0)),
            pl.BlockSpec((tm, d // 2), lambda i: (i + first, 0)),
            pl.BlockSpec((tm, d // 2), lambda i: (i + second, 0)),
            pl.BlockSpec((tm, SUBLANES), lambda i: (i, 0)),
            pl.BlockSpec((1, d), lambda i: (0, 0)),
        ],
        out_specs=pl.BlockSpec((tm, d), lambda i: (i, 0)),
        out_shape=jax.ShapeDtypeStruct((n, d), F32),
        compiler_params=pltpu.CompilerParams(dimension_semantics=("arbitrary",), vmem_limit_bytes=VMEM_LIMIT),
        name="finalize",
    )(xm, yk, yk, wk, fnw)


def _moe(h2, route, counts, wg, wu, wd, *, tm_e):
    n = h2.shape[0]
    n_rows = 2 * n + N_EXPERTS * tm_e
    n_tiles = n_rows // tm_e
    pos, tiles = _plan(route, counts, tm_e, n_tiles)
    xs = _sc_scatter2(h2, pos[0], pos[1], n_rows)
    y_sorted = _experts(tiles[0, :n_tiles], tiles[1, :n_tiles], xs, wg, wu, wd, tm_e)
    return _sc_gather(y_sorted, pos.reshape(-1))


def _layer(x_tokens, pool0, conv0, s0, cnt0, prm, *, n_seq, sb, tt, tv, chunk, pos0, n_tiles, tm):
    conv_in_proj = tv == tt
    if conv_in_proj:
        proj, gates, nconv_proj = _inproj_conv(x_tokens.reshape(n_seq, n_tiles * tt, -1), conv0, prm["norm1_w"],
                                               prm["w_in"], prm["w_conv"], tm)
    else:
        proj, gates = _inproj(x_tokens, prm["norm1_w"], prm["w_in"], tm)
    ypool, ydelta, npool, nconv, ns = _mixers(
        proj, pool0, conv0, s0, prm["w_pool"], prm["pool_scale"], prm["w_conv"], prm["a_log"], prm["dt_bias"],
        prm["o_norm_w"], n_seq=n_seq, sb=sb, tt=tt, tv=tv, chunk=chunk, pos0=pos0, n_tiles=n_tiles,
        qkv_done=conv_in_proj)
    if conv_in_proj:
        nconv = nconv_proj
    xm, h2, route, wk, counts = _merge(
        x_tokens, gates, ypool, ydelta, prm["w_branch_pool"], prm["w_branch_delta"], prm["w_out"], prm["norm2_w"],
        prm["w_router_hi"], prm["w_router_lo"], prm["b_router"], cnt0, 2 * tm, tt, tv)
    return xm, h2, route, wk, counts, npool, nconv, ns


def _pad_lanes(v, width=LANES):
    v = v.reshape(1, -1).astype(F32)
    return jnp.pad(v, ((0, 0), (0, width - v.shape[1])))


def kernel(x_prompt, x_sample, cache_pool, cache_conv, state_delta, norm1_w, w_in, w_pool, pool_scale, w_conv, a_log, dt_bias, o_norm_w, w_branch_pool, w_branch_delta, w_out, norm2_w, w_router_group, b_router_group, w_router_expert, b_router_expert, w_gate, w_up, w_down, final_norm_w):
    n_layers = norm1_w.shape[0]
    assert n_layers == 1, "single-layer step"
    bsz, seq, d = x_prompt.shape
    dbs, dseq, _ = x_sample.shape
    past_len = 16384
    lyr = 0

    wi = w_in[lyr]
    c = [POOL_DIM, POOL_DIM + CONV_DIM]
    c += [c[1] + N_HEADS, c[1] + 2 * N_HEADS]
    c += [c[3] + QK_DIM, c[3] + QK_DIM + d]
    pad8 = lambda m: jnp.pad(m, ((0, 0), (0, LANES - N_HEADS)))
    w_in_r = jnp.concatenate(
        [wi[:, c[3]:c[4]], wi[:, c[0]:c[1]], wi[:, :c[0]], pad8(wi[:, c[1]:c[2]]), pad8(wi[:, c[2]:c[3]]),
         wi[:, c[4]:c[5]], wi[:, c[5]:]], axis=1).astype(BF16)
    w_router = jnp.concatenate(
        [w_router_group[lyr], w_router_expert[lyr].reshape(d, N_EXPERTS),
         jnp.zeros((d, LANES - N_GROUPS - N_EXPERTS), F32)], axis=1).astype(F32)
    w_router_hi = w_router.astype(BF16)
    b_router = _pad_lanes(jnp.concatenate([b_router_group[lyr], b_router_expert[lyr].reshape(-1)]))
    prm = dict(
        norm1_w=norm1_w[lyr].reshape(1, d), w_in=w_in_r,
        w_pool=w_pool[lyr].astype(BF16), pool_scale=pool_scale[lyr].reshape(1, POOL_DIM),
        w_conv=jnp.pad(w_conv[lyr], ((0, SUBLANES - CONV_WIDTH), (0, 0))),
        a_log=_pad_lanes(a_log[lyr]), dt_bias=_pad_lanes(dt_bias[lyr]), o_norm_w=o_norm_w[lyr].reshape(1, HEAD_DIM),
        w_branch_pool=w_branch_pool[lyr].astype(BF16), w_branch_delta=w_branch_delta[lyr].astype(BF16),
        w_out=w_out[lyr].astype(BF16), norm2_w=norm2_w[lyr].reshape(1, d),
        w_router_hi=w_router_hi, w_router_lo=(w_router - w_router_hi.astype(F32)).astype(BF16), b_router=b_router,
    )
    wg, wu, wd = w_gate[lyr], w_up[lyr], w_down[lyr]
    fnw = final_norm_w.reshape(1, d)

    tt_p = 128
    xm, h2, route, wk, counts, npool, nconv, ns = _layer(
        x_prompt.reshape(bsz * seq, d),
        jnp.zeros((bsz, POOL_BUF, POOL_DIM), F32), jnp.zeros((bsz, CONV_WIDTH - 1, CONV_DIM), F32),
        jnp.zeros((bsz, N_HEADS, HEAD_DIM, HEAD_DIM), F32), jnp.zeros((1, LANES), F32), prm,
        n_seq=bsz, sb=4, tt=tt_p, tv=tt_p, chunk=64, pos0=0, n_tiles=seq // tt_p, tm=256)
    yk = _moe(h2, route, counts, wg, wu, wd, tm_e=512)
    y_prompt = _finalize(xm, yk, wk, fnw, 512, 0).reshape(bsz, seq, d)
    pool_p = npool[None]
    conv_p = nconv[None]
    delta_p = ns[None]

    tt_s = SUBLANES
    xs_pad = jnp.pad(x_sample, ((0, 0), (0, tt_s - dseq), (0, 0))).reshape(dbs * tt_s, d)
    xm, h2, route, wk, counts, npool, nconv, ns = _layer(
        xs_pad, cache_pool[lyr], cache_conv[lyr], state_delta[lyr], jnp.zeros((1, LANES), F32), prm,
        n_seq=dbs, sb=8, tt=tt_s, tv=dseq, chunk=tt_s, pos0=past_len, n_tiles=1, tm=256)
    real = lambda a: a.reshape(dbs, tt_s, -1)[:, :dseq].reshape(dbs * dseq, -1)
    route = route.reshape(SUBLANES, dbs, tt_s)[:, :, :dseq].reshape(SUBLANES, dbs * dseq)
    yk = _moe(real(h2), route, counts, wg, wu, wd, tm_e=128)
    y_sample = _finalize(real(xm), yk, real(wk), fnw, 256, 0).reshape(dbs, dseq, d)
    pool_s = npool[None]
    conv_s = nconv[None]
    delta_s = ns[None]
    return (y_prompt, y_sample, pool_p, conv_p, delta_p, pool_s, conv_s, delta_s)
```

```python
import functools

import jax
import jax.numpy as jnp
from jax import lax
from jax.experimental import pallas as pl
from jax.experimental.pallas import tpu as pltpu
from jax.experimental.pallas import tpu_sc as plsc

F32 = jnp.float32
BF16 = jnp.bfloat16
U32 = jnp.uint32
EPS = 1e-6
HIGHEST = lax.Precision.HIGHEST

POOL_WINDOWS = (2, 4, 8, 16)
POOL_GROUP_DIM = 128
POOL_DIM = len(POOL_WINDOWS) * POOL_GROUP_DIM
POOL_BUF = max(POOL_WINDOWS) - 1
N_HEADS = 8
HEAD_DIM = 128
QK_DIM = N_HEADS * HEAD_DIM
CONV_WIDTH = 4
CONV_DIM = 3 * QK_DIM
N_GROUPS = 4
EXPERTS_PER_GROUP = 8
N_EXPERTS = N_GROUPS * EXPERTS_PER_GROUP
LANES = 128
SUBLANES = 8

OFF_Z = 0
OFF_QKV = OFF_Z + QK_DIM
OFF_POOL = OFF_QKV + CONV_DIM
OFF_B = OFF_POOL + POOL_DIM
OFF_A = OFF_B + LANES
MIX_COLS = OFF_A + LANES
GATE_COLS = 2048
IN_COLS = MIX_COLS + GATE_COLS

POOL_ROWS = 16
CONV_ROWS = 8
VMEM_LIMIT = 60 * 1024 * 1024
EXPERT_PART_ROWS = 256
CONV_GROUPS = 12
SC_ROWS = 96


def _sigmoid(x):
    return 1.0 / (1.0 + jnp.exp(-x))


def _softplus(x):
    return jnp.maximum(x, 0.0) + jnp.log1p(jnp.exp(-jnp.abs(x)))


def _rms(x, w):
    return x * lax.rsqrt(jnp.mean(x * x, axis=-1, keepdims=True) + EPS) * w


def _dot(a, b):
    return jnp.dot(a.astype(BF16), b.astype(BF16), preferred_element_type=F32)


def _dot_nt(a, b):
    return lax.dot_general(a.astype(BF16), b.astype(BF16), (((1,), (1,)), ((), ())), preferred_element_type=F32)


def _dot_tn(a, b):
    return lax.dot_general(a.astype(BF16), b.astype(BF16), (((0,), (0,)), ((), ())), preferred_element_type=F32)


def _pack_pairs(x):
    c = x.shape[1] // 2
    hi = pltpu.bitcast(x[:, :c].astype(BF16).astype(F32), U32)
    lo = pltpu.bitcast(x[:, c:].astype(BF16).astype(F32), U32)
    return hi | lax.shift_right_logical(lo, jnp.uint32(16))


def _unpack_pairs(w):
    left = pltpu.bitcast(w & jnp.uint32(0xFFFF0000), F32)
    right = pltpu.bitcast(lax.shift_left(w, jnp.uint32(16)), F32)
    return left, right


def _inproj_kernel(x_ref, nw_ref, w_ref, mix_ref, gate_ref):
    h = _rms(x_ref[...], nw_ref[...]).astype(BF16)
    half = MIX_COLS // 2
    for c0 in (0, half):
        mix_ref[:, c0:c0 + half] = jnp.dot(h, w_ref[:, c0:c0 + half], preferred_element_type=F32)
    half = GATE_COLS // 2
    for c0 in (0, half):
        gate_ref[:, c0:c0 + half] = jnp.dot(h, w_ref[:, MIX_COLS + c0:MIX_COLS + c0 + half],
                                            preferred_element_type=F32)


def _inproj(x, norm_w, w_in_r, tm):
    n, d = x.shape
    return pl.pallas_call(
        _inproj_kernel,
        grid=(n // tm,),
        in_specs=[
            pl.BlockSpec((tm, d), lambda i: (i, 0)),
            pl.BlockSpec((1, d), lambda i: (0, 0)),
            pl.BlockSpec((d, IN_COLS), lambda i: (0, 0)),
        ],
        out_specs=[pl.BlockSpec((tm, MIX_COLS), lambda i: (i, 0)), pl.BlockSpec((tm, GATE_COLS), lambda i: (i, 0))],
        out_shape=[jax.ShapeDtypeStruct((n, MIX_COLS), F32), jax.ShapeDtypeStruct((n, GATE_COLS), F32)],
        compiler_params=pltpu.CompilerParams(dimension_semantics=("arbitrary",), vmem_limit_bytes=VMEM_LIMIT),
        name="inproj",
    )(x, norm_w, w_in_r)


def _conv_silu_qkv(ext_c, seq, tt, wconv_ref, ci, col0=0):
    base = CONV_ROWS - (CONV_WIDTH - 1)
    cs = slice(ci * LANES, (ci + 1) * LANES)
    cl = slice(ci * LANES - col0, (ci + 1) * LANES - col0)
    idx = (lambda r: (r, cl)) if seq is None else (lambda r: (seq, r, cl))
    acc = ext_c[idx(slice(base, base + tt))] * wconv_ref[0:1, cs]
    for j in range(1, CONV_WIDTH):
        acc = acc + ext_c[idx(slice(base + j, base + j + tt))] * wconv_ref[j:j + 1, cs]
    y = acc * _sigmoid(acc)
    if ci < 2 * N_HEADS:
        y = y * lax.rsqrt(jnp.sum(y * y, axis=-1, keepdims=True) + EPS)
    if ci < N_HEADS:
        y = y * (HEAD_DIM ** -0.5)
    return y


def _inproj_conv_kernel(x_ref, nw_ref, w_ref, wconv_ref, conv0_ref, mix_ref, gate_ref, nconv_ref, h_s, *ext, n_tiles):
    t = pl.program_id(1)
    tm = x_ref.shape[1]
    base = CONV_ROWS - (CONV_WIDTH - 1)
    step = CONV_DIM // len(ext)

    @pl.when(t == 0)
    def _():
        for g, buf in enumerate(ext):
            buf[base:CONV_ROWS, :] = conv0_ref[0, :, g * step:(g + 1) * step]

    h_s[...] = _rms(x_ref[0], nw_ref[...]).astype(BF16)

    def proj(c0, c1):
        return jnp.dot(h_s[...], w_ref[:, c0:c1], preferred_element_type=F32)

    def fill_z0():
        mix_ref[0, :, OFF_Z:OFF_Z + QK_DIM // 2] = proj(OFF_Z, OFF_Z + QK_DIM // 2)

    def fill_z1():
        mix_ref[0, :, OFF_Z + QK_DIM // 2:OFF_QKV] = proj(OFF_Z + QK_DIM // 2, OFF_QKV)

    def fill_pool():
        mix_ref[0, :, OFF_POOL:MIX_COLS] = proj(OFF_POOL, MIX_COLS)

    def fill_gate(i):
        q = GATE_COLS // 4
        gate_ref[0, :, i * q:(i + 1) * q] = proj(MIX_COLS + i * q, MIX_COLS + (i + 1) * q)

    fillers = [fill_z0, fill_z1, fill_pool] + [functools.partial(fill_gate, i) for i in range(4)]
    for g, buf in enumerate(ext):
        c0 = g * step
        buf[CONV_ROWS:CONV_ROWS + tm, :] = proj(OFF_QKV + c0, OFF_QKV + c0 + step)
        if fillers:
            fillers.pop(0)()
        for ci in range(c0 // LANES, (c0 + step) // LANES):
            mix_ref[0, :, OFF_QKV + ci * LANES:OFF_QKV + (ci + 1) * LANES] = _conv_silu_qkv(
                buf, None, tm, wconv_ref, ci, c0)
    for f in fillers:
        f()

    @pl.when(t == n_tiles - 1)
    def _():
        for g, buf in enumerate(ext):
            nconv_ref[0, :, g * step:(g + 1) * step] = buf[tm + base:tm + CONV_ROWS, :]

    for buf in ext:
        buf[0:CONV_ROWS, :] = buf[tm:tm + CONV_ROWS, :]


def _inproj_conv(x, conv0, norm_w, w_in_r, wconv, tm):
    n_seq, seq_len, d = x.shape
    n_tiles = seq_len // tm
    tile = lambda b, t: (b, t, 0)
    const = lambda b, t: (0, 0)
    mix, gates, nconv = pl.pallas_call(
        functools.partial(_inproj_conv_kernel, n_tiles=n_tiles),
        grid=(n_seq, n_tiles),
        in_specs=[
            pl.BlockSpec((1, tm, d), tile),
            pl.BlockSpec((1, d), const),
            pl.BlockSpec((d, IN_COLS), const),
            pl.BlockSpec((SUBLANES, CONV_DIM), const),
            pl.BlockSpec((1, CONV_WIDTH - 1, CONV_DIM), lambda b, t: (b, 0, 0)),
        ],
        out_specs=[
            pl.BlockSpec((1, tm, MIX_COLS), tile),
            pl.BlockSpec((1, tm, GATE_COLS), tile),
            pl.BlockSpec((1, CONV_WIDTH - 1, CONV_DIM), lambda b, t: (b, 0, 0)),
        ],
        out_shape=[
            jax.ShapeDtypeStruct((n_seq, seq_len, MIX_COLS), F32),
            jax.ShapeDtypeStruct((n_seq, seq_len, GATE_COLS), F32),
            jax.ShapeDtypeStruct((n_seq, CONV_WIDTH - 1, CONV_DIM), F32),
        ],
        scratch_shapes=[pltpu.VMEM((tm, d), BF16)] + [
            pltpu.VMEM((CONV_ROWS + tm, CONV_DIM // CONV_GROUPS), F32) for _ in range(CONV_GROUPS)],
        compiler_params=pltpu.CompilerParams(dimension_semantics=("arbitrary", "arbitrary"),
                                             vmem_limit_bytes=VMEM_LIMIT),
        name="inproj_conv",
    )(x, norm_w, w_in_r, wconv, conv0)
    n = n_seq * seq_len
    return mix.reshape(n, MIX_COLS), gates.reshape(n, GATE_COLS), nconv


def _mixer_kernel(proj_ref, pool0_ref, conv0_ref, s0_ref, wpool_ref, pscale_ref, wconv_ref, alog_ref, dtb_ref, onw_ref,
                  ypool_ref, ydelta_ref, npool_ref, nconv_ref, ns_ref,
                  ext_p, ext_c, q_s, k_s, v_s, g_s, b_s, state,
                  *, sb, tt, tv, chunk, pos0, n_tiles, qkv_done):
    t = pl.program_id(1)
    base = CONV_ROWS - (CONV_WIDTH - 1)

    @pl.when(t == 0)
    def _():
        ext_p[:, POOL_ROWS - POOL_BUF:POOL_ROWS, :] = pool0_ref[...]
        if not qkv_done:
            ext_c[:, base:CONV_ROWS, :] = conv0_ref[...]
        state[...] = s0_ref[...]

    row = lax.broadcasted_iota(jnp.int32, (tt, 1), 0)
    pos = pos0 + t * tt + row
    for s in range(sb):
        ext_p[s, POOL_ROWS:POOL_ROWS + tt, :] = proj_ref[s, :, OFF_POOL:OFF_POOL + POOL_DIM]
        if not qkv_done:
            ext_c[s, CONV_ROWS:CONV_ROWS + tt, :] = proj_ref[s, :, OFF_QKV:OFF_QKV + CONV_DIM]

        for gi, win in enumerate(POOL_WINDOWS):
            cs = slice(gi * POOL_GROUP_DIM, (gi + 1) * POOL_GROUP_DIM)
            u = ext_p[s, POOL_ROWS:POOL_ROWS + tt, cs]
            acc = u
            for j in range(1, win):
                acc = acc + ext_p[s, POOL_ROWS - j:POOL_ROWS - j + tt, cs]
            cnt = jnp.minimum(win, pos + 1).astype(F32)
            pooled = acc / cnt - u
            ypool_ref[s, :, cs] = _dot(pooled, wpool_ref[gi]) * pscale_ref[:, cs]

        for ci in range(0 if qkv_done else CONV_DIM // LANES):
            hs = slice((ci % N_HEADS) * HEAD_DIM, (ci % N_HEADS + 1) * HEAD_DIM)
            (q_s, k_s, v_s)[ci // N_HEADS][s, :, hs] = _conv_silu_qkv(ext_c, s, tt, wconv_ref, ci)

        beta = _sigmoid(proj_ref[s, :, OFF_B:OFF_B + LANES])
        g = -jnp.exp(alog_ref[...]) * _softplus(proj_ref[s, :, OFF_A:OFF_A + LANES] + dtb_ref[...])
        if tv < tt:
            beta = jnp.where(row < tv, beta, 0.0)
            g = jnp.where(row < tv, g, 0.0)
        b_s[s] = beta
        g_s[s] = g

    @pl.when(t == n_tiles - 1)
    def _():
        npool_ref[...] = ext_p[:, tv + POOL_ROWS - POOL_BUF:tv + POOL_ROWS, :]
        if qkv_done:
            nconv_ref[...] = jnp.zeros_like(nconv_ref)
        else:
            nconv_ref[...] = ext_c[:, tv + base:tv + CONV_ROWS, :]

    if n_tiles > 1:
        ext_p[:, 0:POOL_ROWS, :] = ext_p[:, tt:tt + POOL_ROWS, :]
        if not qkv_done:
            ext_c[:, 0:CONV_ROWS, :] = ext_c[:, tt:tt + CONV_ROWS, :]

    ri = lax.broadcasted_iota(jnp.int32, (chunk, chunk), 0)
    cj = lax.broadcasted_iota(jnp.int32, (chunk, chunk), 1)
    causal = ri >= cj
    strict = ri > cj
    ltri = causal.astype(F32)
    eye = (ri == cj).astype(F32)
    n_doublings = chunk.bit_length() - 2
    units = [(s, h) for s in range(sb) for h in range(N_HEADS)]
    hsl = [slice(h * HEAD_DIM, (h + 1) * HEAD_DIM) for h in range(N_HEADS)]

    def qkv(which, s, rows, h):
        if qkv_done:
            c0 = OFF_QKV + which * QK_DIM + h * HEAD_DIM
            return proj_ref[s, rows, c0:c0 + HEAD_DIM]
        return (q_s, k_s, v_s)[which][s, rows, hsl[h]]

    def chunk_body(ci, carry):
        r0 = pl.multiple_of(ci * chunk, chunk)
        rows = pl.ds(r0, chunk)
        g_all, g_t, b_all = [], [], []
        for s in range(sb):
            ga = jnp.dot(ltri, g_s[s, rows, :], precision=HIGHEST, preferred_element_type=F32)
            g_all.append(ga)
            if chunk < LANES:
                ga = jnp.concatenate([ga, jnp.zeros((LANES - chunk, LANES), F32)], axis=0)
            g_t.append(ga.T)
            b_all.append(b_s[s, rows, :])
        g_col = [g_all[s][:, h:h + 1] for s, h in units]
        dec = [jnp.exp(jnp.minimum(g_col[i] - g_t[s][h:h + 1, 0:chunk], 0.0)) for i, (s, h) in enumerate(units)]
        b_col = [b_all[s][:, h:h + 1] for s, h in units]
        q = [qkv(0, s, rows, h) for s, h in units]
        k = [qkv(1, s, rows, h) for s, h in units]
        idx = range(len(units))
        kb = [k[i] * b_col[i] for i in idx]
        kbk = [_dot_nt(kb[i], k[i]) for i in idx]
        qk = [_dot_nt(q[i], k[i]) for i in idx]
        p_mat = [jnp.where(strict, kbk[i] * dec[i], 0.0) for i in idx]
        t_inv = [eye - p_mat[i] for i in idx]
        for _ in range(n_doublings):
            p_mat = [_dot(p_mat[i], p_mat[i]) for i in idx]
            xp = [_dot(t_inv[i], p_mat[i]) for i in idx]
            t_inv = [t_inv[i] + xp[i] for i in idx]
        g_exp = [jnp.exp(g_col[i]) for i in idx]
        wu = [_dot(t_inv[i], jnp.concatenate([kb[i] * g_exp[i], qkv(2, s, rows, h) * b_col[i]], axis=1))
              for i, (s, h) in enumerate(units)]
        s_old = [state[s, h] for s, h in units]
        res = [_dot(jnp.concatenate([wu[i][:, 0:HEAD_DIM], q[i] * g_exp[i]], axis=0), s_old[i]) for i in idx]
        v_new = [wu[i][:, HEAD_DIM:2 * HEAD_DIM] - res[i][0:chunk] for i in idx]
        a_qk = [jnp.where(causal, qk[i] * dec[i], 0.0) for i in idx]
        g_last = [g_all[s][chunk - 1:chunk, h:h + 1] for s, h in units]
        k_dec = [k[i] * jnp.exp(g_last[i] - g_col[i]) for i in idx]
        intra = [_dot(a_qk[i], v_new[i]) for i in idx]
        upd = [_dot_tn(k_dec[i], v_new[i]) for i in idx]
        for i, (s, h) in enumerate(units):
            state[s, h] = s_old[i] * jnp.exp(g_last[i]) + upd[i]
            o_c = res[i][chunk:2 * chunk] + intra[i]
            z_c = proj_ref[s, rows, OFF_Z + h * HEAD_DIM:OFF_Z + (h + 1) * HEAD_DIM]
            ydelta_ref[s, rows, hsl[h]] = _rms(o_c, onw_ref[...]) * (z_c * _sigmoid(z_c))
        return carry

    lax.fori_loop(0, tt // chunk, chunk_body, 0)

    @pl.when(t == n_tiles - 1)
    def _():
        ns_ref[...] = state[...]


def _mixers(proj, pool0, conv0, s0, wpool, pscale, wconv, alog, dtb, onw, *, n_seq, sb, tt, tv, chunk, pos0, n_tiles,
            qkv_done):
    kern = functools.partial(_mixer_kernel, sb=sb, tt=tt, tv=tv, chunk=chunk, pos0=pos0, n_tiles=n_tiles,
                             qkv_done=qkv_done)
    seq_len = n_tiles * tt
    tile = lambda b, t: (b, t, 0)
    seq3 = lambda b, t: (b, 0, 0)
    seq4 = lambda b, t: (b, 0, 0, 0)
    const2 = lambda b, t: (0, 0)
    ypool, ydelta, npool, nconv, ns = pl.pallas_call(
        kern,
        grid=(n_seq // sb, n_tiles),
        in_specs=[
            pl.BlockSpec((sb, tt, MIX_COLS), tile),
            pl.BlockSpec((sb, POOL_BUF, POOL_DIM), seq3),
            pl.BlockSpec((sb, CONV_WIDTH - 1, CONV_DIM), seq3),
            pl.BlockSpec((sb, N_HEADS, HEAD_DIM, HEAD_DIM), seq4),
            pl.BlockSpec((len(POOL_WINDOWS), POOL_GROUP_DIM, POOL_GROUP_DIM), lambda b, t: (0, 0, 0)),
            pl.BlockSpec((1, POOL_DIM), const2),
            pl.BlockSpec((SUBLANES, CONV_DIM), const2),
            pl.BlockSpec((1, LANES), const2),
            pl.BlockSpec((1, LANES), const2),
            pl.BlockSpec((1, HEAD_DIM), const2),
        ],
        out_specs=[
            pl.BlockSpec((sb, tt, POOL_DIM), tile),
            pl.BlockSpec((sb, tt, QK_DIM), tile),
            pl.BlockSpec((sb, POOL_BUF, POOL_DIM), seq3),
            pl.BlockSpec((sb, CONV_WIDTH - 1, CONV_DIM), seq3),
            pl.BlockSpec((sb, N_HEADS, HEAD_DIM, HEAD_DIM), seq4),
        ],
        out_shape=[
            jax.ShapeDtypeStruct((n_seq, seq_len, POOL_DIM), F32),
            jax.ShapeDtypeStruct((n_seq, seq_len, QK_DIM), F32),
            jax.ShapeDtypeStruct((n_seq, POOL_BUF, POOL_DIM), F32),
            jax.ShapeDtypeStruct((n_seq, CONV_WIDTH - 1, CONV_DIM), F32),
            jax.ShapeDtypeStruct((n_seq, N_HEADS, HEAD_DIM, HEAD_DIM), F32),
        ],
        scratch_shapes=[
            pltpu.VMEM((sb, POOL_ROWS + tt, POOL_DIM), F32),
            pltpu.VMEM((sb, CONV_ROWS + tt, CONV_DIM), F32),
            pltpu.VMEM((sb, tt, QK_DIM), F32),
            pltpu.VMEM((sb, tt, QK_DIM), F32),
            pltpu.VMEM((sb, tt, QK_DIM), F32),
            pltpu.VMEM((sb, tt, LANES), F32),
            pltpu.VMEM((sb, tt, LANES), F32),
            pltpu.VMEM((sb, N_HEADS, HEAD_DIM, HEAD_DIM), F32),
        ],
        compiler_params=pltpu.CompilerParams(dimension_semantics=("arbitrary", "arbitrary"),
                                             vmem_limit_bytes=VMEM_LIMIT),
        name="mixers",
    )(proj.reshape(n_seq, seq_len, MIX_COLS), pool0, conv0, s0, wpool, pscale, wconv, alog, dtb, onw)
    n = n_seq * seq_len
    return ypool.reshape(n, POOL_DIM), ydelta.reshape(n, QK_DIM), npool, nconv, ns


def _merge_kernel(x_ref, gates_ref, ypool_ref, ydelta_ref, wbp_ref, wbd_ref, wout_ref, n2w_ref, wr_hi_ref, wr_lo_ref,
                  br_ref, cnt0_ref, xm_ref, h2_ref, route_ref, wk_ref, cnt_ref, cnt_s, *, period, valid):
    tm = x_ref.shape[0]

    @pl.when(pl.program_id(0) == 0)
    def _():
        cnt_s[...] = cnt0_ref[...]

    bp = _dot(ypool_ref[...], wbp_ref[...])
    bd = _dot(ydelta_ref[...], wbd_ref[...])
    merged = _sigmoid(gates_ref[:, 0:1024]) * bp + _sigmoid(gates_ref[:, 1024:2048]) * bd
    xm = x_ref[...] + _dot(merged, wout_ref[...])
    xm_ref[...] = xm
    h2 = _rms(xm, n2w_ref[...])
    h2_ref[...] = _pack_pairs(h2)

    h2_hi = h2.astype(BF16)
    h2_lo = (h2 - h2_hi.astype(F32)).astype(BF16)
    logits = (jnp.dot(h2_hi, wr_hi_ref[...], preferred_element_type=F32)
              + (jnp.dot(h2_hi, wr_lo_ref[...], preferred_element_type=F32)
                 + jnp.dot(h2_lo, wr_hi_ref[...], preferred_element_type=F32))) + br_ref[...]
    lane = lax.broadcasted_iota(jnp.int32, logits.shape, 1)
    neg = -jnp.inf
    far = LANES - 1
    is_g = lane < N_GROUPS
    g_max = jnp.max(jnp.where(is_g, logits, neg), axis=-1, keepdims=True)
    g_sel = jnp.min(jnp.where(is_g & (logits == g_max), lane, far), axis=-1, keepdims=True)
    p_g = 1.0 / jnp.sum(jnp.where(is_g, jnp.exp(logits - g_max), 0.0), axis=-1, keepdims=True)
    e_lane = lane - N_GROUPS
    is_e = (e_lane >= 0) & (e_lane < N_EXPERTS) & ((e_lane // EXPERTS_PER_GROUP) == g_sel)
    ev = jnp.where(is_e, logits, neg)
    v1 = jnp.max(ev, axis=-1, keepdims=True)
    i1 = jnp.min(jnp.where(is_e & (ev == v1), lane, far), axis=-1, keepdims=True)
    is_e2 = is_e & (lane != i1)
    ev2 = jnp.where(is_e2, logits, neg)
    v2 = jnp.max(ev2, axis=-1, keepdims=True)
    i2 = jnp.min(jnp.where(is_e2 & (ev2 == v2), lane, far), axis=-1, keepdims=True)
    e21 = jnp.exp(v2 - v1)
    w1 = p_g / (1.0 + e21)
    w2 = p_g * e21 / (1.0 + e21)
    e1 = i1 - N_GROUPS
    e2 = i2 - N_GROUPS
    chosen = (lane == e1) | (lane == e2)
    routed = jnp.ones((tm, 1), F32)
    if valid < period:
        row = lax.broadcasted_iota(jnp.int32, (tm, 1), 0)
        is_real = lax.rem(row, period) < valid
        chosen = chosen & is_real
        routed = jnp.where(is_real, 1.0, 0.0)
    onehot = jnp.where(chosen, 1.0, 0.0)
    rr = lax.broadcasted_iota(jnp.int32, (tm, tm), 0)
    cc = lax.broadcasted_iota(jnp.int32, (tm, tm), 1)
    earlier = jnp.where(rr > cc, 1.0, 0.0).astype(BF16)
    before = jnp.dot(earlier, onehot.astype(BF16), preferred_element_type=F32) + cnt_s[...]
    r1 = jnp.sum(jnp.where(lane == e1, before, 0.0), axis=-1, keepdims=True)
    r2 = jnp.sum(jnp.where(lane == e2, before, 0.0), axis=-1, keepdims=True)
    cnt_s[...] = cnt_s[...] + jnp.sum(onehot, axis=0, keepdims=True)
    cnt_ref[...] = cnt_s[...]
    record = jnp.where(lane == 0, e1.astype(F32), jnp.where(lane == 1, e2.astype(F32), jnp.where(
        lane == 2, r1, jnp.where(lane == 3, r2, jnp.where(lane == 4, routed, 0.0)))))
    route_ref[...] = record.T[0:SUBLANES, :]
    lane8 = lax.broadcasted_iota(jnp.int32, (tm, SUBLANES), 1)
    wk_ref[...] = jnp.where(lane8 == 0, w1, jnp.where(lane8 == 1, w2, 0.0))


def _merge(x, proj, ypool, ydelta, wbp, wbd, wout, n2w, wr_hi, wr_lo, br, cnt0, tm, period, valid):
    n, d = x.shape
    row = lambda i: (i, 0)
    const = lambda i: (0, 0)
    return pl.pallas_call(
        functools.partial(_merge_kernel, period=period, valid=valid),
        grid=(n // tm,),
        in_specs=[
            pl.BlockSpec((tm, d), row),
            pl.BlockSpec((tm, GATE_COLS), row),
            pl.BlockSpec((tm, POOL_DIM), row),
            pl.BlockSpec((tm, QK_DIM), row),
            pl.BlockSpec((POOL_DIM, d), const),
            pl.BlockSpec((QK_DIM, d), const),
            pl.BlockSpec((d, d), const),
            pl.BlockSpec((1, d), const),
            pl.BlockSpec((d, LANES), const),
            pl.BlockSpec((d, LANES), const),
            pl.BlockSpec((1, LANES), const),
            pl.BlockSpec((1, LANES), const),
        ],
        out_specs=[
            pl.BlockSpec((tm, d), row),
            pl.BlockSpec((tm, d // 2), row),
            pl.BlockSpec((SUBLANES, tm), lambda i: (0, i)),
            pl.BlockSpec((tm, SUBLANES), row),
            pl.BlockSpec((1, LANES), const),
        ],
        out_shape=[
            jax.ShapeDtypeStruct((n, d), F32),
            jax.ShapeDtypeStruct((n, d // 2), U32),
            jax.ShapeDtypeStruct((SUBLANES, n), F32),
            jax.ShapeDtypeStruct((n, SUBLANES), F32),
            jax.ShapeDtypeStruct((1, LANES), F32),
        ],
        scratch_shapes=[pltpu.VMEM((1, LANES), F32)],
        compiler_params=pltpu.CompilerParams(dimension_semantics=("arbitrary",), vmem_limit_bytes=VMEM_LIMIT),
        name="merge",
    )(x, proj, ypool, ydelta, wbp, wbd, wout, n2w, wr_hi, wr_lo, br, cnt0)


def _expert_kernel(te_ref, nv_ref, x_ref, wg_hbm, wu_hbm, wd_hbm, y_ref, st_g, st_u, st_d, wg_s, wu_s, wd_s, sem, slot_s):
    i = pl.program_id(0)
    n_tiles = pl.num_programs(0)
    n_valid = nv_ref[i]
    expert = te_ref[i]

    def weight_copies(e, slot):
        return [pltpu.make_async_copy(src.at[e], dst.at[slot], sem.at[slot])
                for src, dst in ((wg_hbm, st_g), (wu_hbm, st_u), (wd_hbm, st_d))]

    @pl.when(i == 0)
    def _():
        slot_s[0] = 0
        for cp in weight_copies(expert, 0):
            cp.start()

    @pl.when((i == 0) | (expert != te_ref[jnp.maximum(i - 1, 0)]))
    def _():
        slot = slot_s[0]
        nxt = lax.while_loop(lambda j: (j < n_tiles) & (te_ref[jnp.minimum(j, n_tiles - 1)] == expert),
                             lambda j: j + 1, i + 1)

        @pl.when(nxt < n_tiles)
        def _():
            for cp in weight_copies(te_ref[jnp.minimum(nxt, n_tiles - 1)], 1 - slot):
                cp.start()
        for cp in weight_copies(expert, slot):
            cp.wait()
        wg_s[...] = st_g[slot].astype(BF16)
        wu_s[...] = st_u[slot].astype(BF16)
        wd_s[...] = st_d[slot].astype(BF16)
        slot_s[0] = 1 - slot

    @pl.when(n_valid > 0)
    def _():
        tm = x_ref.shape[0]
        part = min(tm, EXPERT_PART_ROWS)
        parts = [slice(r0, r0 + part) for r0 in range(0, tm, part)]
        row = lax.broadcasted_iota(jnp.int32, (part, 1), 0)
        xs = []
        for p in parts:
            left, right = _unpack_pairs(jnp.where(row + p.start < n_valid, x_ref[p, :], jnp.uint32(0)))
            xs.append(jnp.concatenate([left.astype(BF16), right.astype(BF16)], axis=1))
        a = [_dot(x, wg_s[...]) for x in xs]
        b = [_dot(x, wu_s[...]) for x in xs]
        act = [(a[j] * _sigmoid(a[j])) * b[j] for j in range(len(parts))]
        y = [_dot(act[j], wd_s[...]) for j in range(len(parts))]
        for j, p in enumerate(parts):
            y_ref[p, :] = _pack_pairs(y[j])

    @pl.when(n_valid == 0)
    def _():
        y_ref[...] = jnp.zeros_like(y_ref)


def _experts(tile_expert, tile_valid, xs, wg, wu, wd, tm):
    n_rows, dp = xs.shape
    d, f = wg.shape[1], wg.shape[2]
    grid_spec = pltpu.PrefetchScalarGridSpec(
        num_scalar_prefetch=2,
        grid=(n_rows // tm,),
        in_specs=[
            pl.BlockSpec((tm, dp), lambda i, te, nv: (jnp.where(nv[i] > 0, i, 0), 0)),
            pl.BlockSpec(memory_space=pl.ANY),
            pl.BlockSpec(memory_space=pl.ANY),
            pl.BlockSpec(memory_space=pl.ANY),
        ],
        out_specs=pl.BlockSpec((tm, dp), lambda i, te, nv: (i, 0)),
        scratch_shapes=[pltpu.VMEM((2, d, f), F32), pltpu.VMEM((2, d, f), F32), pltpu.VMEM((2, f, d), F32),
                        pltpu.VMEM((d, f), BF16), pltpu.VMEM((d, f), BF16), pltpu.VMEM((f, d), BF16),
                        pltpu.SemaphoreType.DMA((2,)), pltpu.SMEM((1,), jnp.int32)],
    )
    return pl.pallas_call(
        _expert_kernel,
        grid_spec=grid_spec,
        out_shape=jax.ShapeDtypeStruct((n_rows, dp), U32),
        compiler_params=pltpu.CompilerParams(dimension_semantics=("arbitrary",), vmem_limit_bytes=VMEM_LIMIT),
        name="experts",
    )(tile_expert, tile_valid, xs, wg, wu, wd)


def _plan_kernel(route_ref, cnt_ref, pos_ref, tiles_ref, *, tm_e, n_rows):
    counts = cnt_ref[...]
    padded = jnp.floor((counts + (tm_e - 1)) / tm_e) * tm_e
    lane = lax.broadcasted_iota(jnp.int32, counts.shape, 1)
    ends = padded
    shift = 1
    while shift < N_EXPERTS:
        ends = ends + jnp.where(lane >= shift, pltpu.roll(ends, shift, axis=1), 0.0)
        shift *= 2
    starts = ends - padded

    def lookup(table, key):
        out = jnp.zeros_like(key)
        for e in range(N_EXPERTS):
            out = out + jnp.where(key == e, table[:, e:e + 1], 0.0)
        return out

    routed = route_ref[4:5, :] > 0.0
    for k in range(2):
        pos = lookup(starts, route_ref[k:k + 1, :]) + route_ref[2 + k:3 + k, :]
        pos_ref[k:k + 1, :] = jnp.where(routed, pos, n_rows - 1.0).astype(jnp.int32)

    tile_start = lax.broadcasted_iota(jnp.int32, (1, tiles_ref.shape[1]), 1).astype(F32) * tm_e
    tile_expert = jnp.zeros_like(tile_start)
    for e in range(N_EXPERTS):
        tile_expert = tile_expert + jnp.where(ends[:, e:e + 1] <= tile_start, 1.0, 0.0)
    tile_expert = jnp.minimum(tile_expert, N_EXPERTS - 1.0)
    valid = lookup(counts, tile_expert) - (tile_start - lookup(starts, tile_expert))
    tiles_ref[0:1, :] = tile_expert.astype(jnp.int32)
    tiles_ref[1:2, :] = jnp.clip(valid, 0.0, tm_e).astype(jnp.int32)


def _plan(route_t, counts, tm_e, n_tiles):
    n = route_t.shape[1]
    tiles_pad = -(-n_tiles // LANES) * LANES
    return pl.pallas_call(
        functools.partial(_plan_kernel, tm_e=tm_e, n_rows=n_tiles * tm_e),
        out_shape=[jax.ShapeDtypeStruct((2, n), jnp.int32), jax.ShapeDtypeStruct((2, tiles_pad), jnp.int32)],
        name="plan",
    )(route_t, counts)


def _sc_workers():
    info = plsc.get_sparse_core_info()
    return info.num_cores, info.num_subcores


def _sc_chunk(per_worker):
    assert per_worker % SUBLANES == 0
    return max(c for c in range(SUBLANES, SC_ROWS + 1, SUBLANES) if per_worker % c == 0)


def _sc_scatter2(rows, idx_a, idx_b, n_out):
    n_cores, n_sub = _sc_workers()
    n, d = rows.shape
    per_worker = n // (n_cores * n_sub)
    assert per_worker * n_cores * n_sub == n
    chunk = _sc_chunk(per_worker)
    mesh = plsc.VectorSubcoreMesh(core_axis_name="c", subcore_axis_name="s")

    n_chunks = per_worker // chunk
    buf = lambda shape, dtype: [pltpu.VMEM(shape, dtype), pltpu.VMEM(shape, dtype)]

    @functools.partial(
        pl.kernel, mesh=mesh, out_type=jax.ShapeDtypeStruct((n_out, d), rows.dtype),
        scratch_types=buf((chunk,), jnp.int32) + buf((chunk,), jnp.int32) + buf((chunk, d), rows.dtype)
        + [pltpu.SemaphoreType.DMA, pltpu.SemaphoreType.DMA])
    def scatter_rows(rows_hbm, ia_hbm, ib_hbm, out_hbm, ia0, ia1, ib0, ib1, r0, r1, sem0, sem1):
        worker = lax.axis_index("s") * n_cores + lax.axis_index("c")
        base = worker * per_worker
        slots = ((ia0, ib0, r0, sem0), (ia1, ib1, r1, sem1))
        pending = {}
        for j in range(n_chunks):
            ia_v, ib_v, rows_v, sem = slots[j % 2]
            if j >= 2:
                for cp in pending.pop(j - 2):
                    cp.wait()
            off = pl.multiple_of(base + j * chunk, SUBLANES)
            pltpu.sync_copy(ia_hbm.at[pl.ds(off, chunk)], ia_v)
            pltpu.sync_copy(ib_hbm.at[pl.ds(off, chunk)], ib_v)
            pltpu.sync_copy(rows_hbm.at[pl.ds(off, chunk)], rows_v)
            pending[j] = (pltpu.async_copy(rows_v, out_hbm.at[ia_v], sem),
                          pltpu.async_copy(rows_v, out_hbm.at[ib_v], sem))
        for j in sorted(pending):
            for cp in pending[j]:
                cp.wait()

    return scatter_rows(rows, idx_a, idx_b)


def _sc_gather(table, idx):
    n_cores, n_sub = _sc_workers()
    n_idx = idx.shape[0]
    d = table.shape[1]
    per_worker = n_idx // (n_cores * n_sub)
    assert per_worker * n_cores * n_sub == n_idx
    chunk = _sc_chunk(per_worker)
    mesh = plsc.VectorSubcoreMesh(core_axis_name="c", subcore_axis_name="s")

    n_chunks = per_worker // chunk

    @functools.partial(
        pl.kernel, mesh=mesh, out_type=jax.ShapeDtypeStruct((n_idx, d), table.dtype),
        scratch_types=[pltpu.VMEM((chunk,), jnp.int32), pltpu.VMEM((chunk,), jnp.int32),
                       pltpu.VMEM((chunk, d), table.dtype), pltpu.VMEM((chunk, d), table.dtype),
                       pltpu.SemaphoreType.DMA, pltpu.SemaphoreType.DMA])
    def gather_rows(table_hbm, idx_hbm, out_hbm, i0, i1, r0, r1, sem0, sem1):
        worker = lax.axis_index("s") * n_cores + lax.axis_index("c")
        base = worker * per_worker
        slots = ((i0, r0, sem0), (i1, r1, sem1))

        def start(j):
            idx_v, rows_v, sem = slots[j % 2]
            off = pl.multiple_of(base + j * chunk, SUBLANES)
            pltpu.sync_copy(idx_hbm.at[pl.ds(off, chunk)], idx_v)
            return pltpu.async_copy(table_hbm.at[idx_v], rows_v, sem)

        nxt = start(0)
        for j in range(n_chunks):
            cur = nxt
            if j + 1 < n_chunks:
                nxt = start(j + 1)
            cur.wait()
            off = pl.multiple_of(base + j * chunk, SUBLANES)
            pltpu.sync_copy(slots[j % 2][1], out_hbm.at[pl.ds(off, chunk)])

    return gather_rows(table, idx)


def _finalize_kernel(xm_ref, ya_ref, yb_ref, wk_ref, fnw_ref, o_ref):
    d = xm_ref.shape[1]
    c = d // 2
    w = wk_ref[...]
    a_left, a_right = _unpack_pairs(ya_ref[...])
    b_left, b_right = _unpack_pairs(yb_ref[...])
    x_left = xm_ref[:, 0:c] + (w[:, 0:1] * a_left + w[:, 1:2] * b_left)
    x_right = xm_ref[:, c:d] + (w[:, 0:1] * a_right + w[:, 1:2] * b_right)
    ms = (jnp.sum(x_left * x_left, axis=-1, keepdims=True) + jnp.sum(x_right * x_right, axis=-1, keepdims=True)) / d
    scale = lax.rsqrt(ms + EPS)
    o_ref[:, 0:c] = x_left * scale * fnw_ref[:, 0:c]
    o_ref[:, c:d] = x_right * scale * fnw_ref[:, c:d]


def _finalize(xm, yk, wk, fnw, tm, row0):
    n, d = xm.shape
    steps = n // tm
    first = row0 // tm
    second = (yk.shape[0] // 2 + row0) // tm
    return pl.pallas_call(
        _finalize_kernel,
        grid=(steps,),
        in_specs=[
            pl.BlockSpec((tm, d), lambda i: (i, 0)),
            pl.BlockSpec((tm, d // 2), lambda i: (i + first, 0)),
            pl.BlockSpec((tm, d // 2), lambda i: (i + second, 0)),
            pl.BlockSpec((tm, SUBLANES), lambda i: (i, 0)),
            pl.BlockSpec((1, d), lambda i: (0, 0)),
        ],
        out_specs=pl.BlockSpec((tm, d), lambda i: (i, 0)),
        out_shape=jax.ShapeDtypeStruct((n, d), F32),
        compiler_params=pltpu.CompilerParams(dimension_semantics=("arbitrary",), vmem_limit_bytes=VMEM_LIMIT),
        name="finalize",
    )(xm, yk, yk, wk, fnw)


def _moe(h2, route, counts, wg, wu, wd, *, tm_e):
    n = h2.shape[0]
    n_rows = 2 * n + N_EXPERTS * tm_e
    n_tiles = n_rows // tm_e
    pos, tiles = _plan(route, counts, tm_e, n_tiles)
    xs = _sc_scatter2(h2, pos[0], pos[1], n_rows)
    y_sorted = _experts(tiles[0, :n_tiles], tiles[1, :n_tiles], xs, wg, wu, wd, tm_e)
    return _sc_gather(y_sorted, pos.reshape(-1))


def _layer(x_tokens, pool0, conv0, s0, cnt0, prm, *, n_seq, sb, tt, tv, chunk, pos0, n_tiles, tm):
    conv_in_proj = tv == tt
    if conv_in_proj:
        proj, gates, nconv_proj = _inproj_conv(x_tokens.reshape(n_seq, n_tiles * tt, -1), conv0, prm["norm1_w"],
                                               prm["w_in"], prm["w_conv"], tm)
    else:
        proj, gates = _inproj(x_tokens, prm["norm1_w"], prm["w_in"], tm)
    ypool, ydelta, npool, nconv, ns = _mixers(
        proj, pool0, conv0, s0, prm["w_pool"], prm["pool_scale"], prm["w_conv"], prm["a_log"], prm["dt_bias"],
        prm["o_norm_w"], n_seq=n_seq, sb=sb, tt=tt, tv=tv, chunk=chunk, pos0=pos0, n_tiles=n_tiles,
        qkv_done=conv_in_proj)
    if conv_in_proj:
        nconv = nconv_proj
    xm, h2, route, wk, counts = _merge(
        x_tokens, gates, ypool, ydelta, prm["w_branch_pool"], prm["w_branch_delta"], prm["w_out"], prm["norm2_w"],
        prm["w_router_hi"], prm["w_router_lo"], prm["b_router"], cnt0, 2 * tm, tt, tv)
    return xm, h2, route, wk, counts, npool, nconv, ns


def _pad_lanes(v, width=LANES):
    v = v.reshape(1, -1).astype(F32)
    return jnp.pad(v, ((0, 0), (0, width - v.shape[1])))


def kernel(x_prompt, x_sample, cache_pool, cache_conv, state_delta, norm1_w, w_in, w_pool, pool_scale, w_conv, a_log, dt_bias, o_norm_w, w_branch_pool, w_branch_delta, w_out, norm2_w, w_router_group, b_router_group, w_router_expert, b_router_expert, w_gate, w_up, w_down, final_norm_w):
    n_layers = norm1_w.shape[0]
    assert n_layers == 1, "single-layer step"
    bsz, seq, d = x_prompt.shape
    dbs, dseq, _ = x_sample.shape
    past_len = 16384
    lyr = 0

    wi = w_in[lyr]
    c = [POOL_DIM, POOL_DIM + CONV_DIM]
    c += [c[1] + N_HEADS, c[1] + 2 * N_HEADS]
    c += [c[3] + QK_DIM, c[3] + QK_DIM + d]
    pad8 = lambda m: jnp.pad(m, ((0, 0), (0, LANES - N_HEADS)))
    w_in_r = jnp.concatenate(
        [wi[:, c[3]:c[4]], wi[:, c[0]:c[1]], wi[:, :c[0]], pad8(wi[:, c[1]:c[2]]), pad8(wi[:, c[2]:c[3]]),
         wi[:, c[4]:c[5]], wi[:, c[5]:]], axis=1).astype(BF16)
    w_router = jnp.concatenate(
        [w_router_group[lyr], w_router_expert[lyr].reshape(d, N_EXPERTS),
         jnp.zeros((d, LANES - N_GROUPS - N_EXPERTS), F32)], axis=1).astype(F32)
    w_router_hi = w_router.astype(BF16)
    b_router = _pad_lanes(jnp.concatenate([b_router_group[lyr], b_router_expert[lyr].reshape(-1)]))
    prm = dict(
        norm1_w=norm1_w[lyr].reshape(1, d), w_in=w_in_r,
        w_pool=w_pool[lyr].astype(BF16), pool_scale=pool_scale[lyr].reshape(1, POOL_DIM),
        w_conv=jnp.pad(w_conv[lyr], ((0, SUBLANES - CONV_WIDTH), (0, 0))),
        a_log=_pad_lanes(a_log[lyr]), dt_bias=_pad_lanes(dt_bias[lyr]), o_norm_w=o_norm_w[lyr].reshape(1, HEAD_DIM),
        w_branch_pool=w_branch_pool[lyr].astype(BF16), w_branch_delta=w_branch_delta[lyr].astype(BF16),
        w_out=w_out[lyr].astype(BF16), norm2_w=norm2_w[lyr].reshape(1, d),
        w_router_hi=w_router_hi, w_router_lo=(w_router - w_router_hi.astype(F32)).astype(BF16), b_router=b_router,
    )
    wg, wu, wd = w_gate[lyr], w_up[lyr], w_down[lyr]
    fnw = final_norm_w.reshape(1, d)

    tt_p = 128
    xm, h2, route, wk, counts, npool, nconv, ns = _layer(
        x_prompt.reshape(bsz * seq, d),
        jnp.zeros((bsz, POOL_BUF, POOL_DIM), F32), jnp.zeros((bsz, CONV_WIDTH - 1, CONV_DIM), F32),
        jnp.zeros((bsz, N_HEADS, HEAD_DIM, HEAD_DIM), F32), jnp.zeros((1, LANES), F32), prm,
        n_seq=bsz, sb=4, tt=tt_p, tv=tt_p, chunk=64, pos0=0, n_tiles=seq // tt_p, tm=256)
    yk = _moe(h2, route, counts, wg, wu, wd, tm_e=512)
    y_prompt = _finalize(xm, yk, wk, fnw, 512, 0).reshape(bsz, seq, d)
    pool_p = npool[None]
    conv_p = nconv[None]
    delta_p = ns[None]

    tt_s = SUBLANES
    xs_pad = jnp.pad(x_sample, ((0, 0), (0, tt_s - dseq), (0, 0))).reshape(dbs * tt_s, d)
    xm, h2, route, wk, counts, npool, nconv, ns = _layer(
        xs_pad, cache_pool[lyr], cache_conv[lyr], state_delta[lyr], jnp.zeros((1, LANES), F32), prm,
        n_seq=dbs, sb=8, tt=tt_s, tv=dseq, chunk=tt_s, pos0=past_len, n_tiles=1, tm=256)
    real = lambda a: a.reshape(dbs, tt_s, -1)[:, :dseq].reshape(dbs * dseq, -1)
    route = route.reshape(SUBLANES, dbs, tt_s)[:, :, :dseq].reshape(SUBLANES, dbs * dseq)
    yk = _moe(real(h2), route, counts, wg, wu, wd, tm_e=128)
    y_sample = _finalize(real(xm), yk, real(wk), fnw, 256, 0).reshape(dbs, dseq, d)
    pool_s = npool[None]
    conv_s = nconv[None]
    delta_s = ns[None]
    return (y_prompt, y_sample, pool_p, conv_p, delta_p, pool_s, conv_s, delta_s)
```

```python
import functools

import jax
import jax.numpy as jnp
from jax import lax
from jax.experimental import pallas as pl
from jax.experimental.pallas import tpu as pltpu
from jax.experimental.pallas import tpu_sc as plsc

F32 = jnp.float32
BF16 = jnp.bfloat16
U32 = jnp.uint32
EPS = 1e-6
HIGHEST = lax.Precision.HIGHEST

POOL_WINDOWS = (2, 4, 8, 16)
POOL_GROUP_DIM = 128
POOL_DIM = len(POOL_WINDOWS) * POOL_GROUP_DIM
POOL_BUF = max(POOL_WINDOWS) - 1
N_HEADS = 8
HEAD_DIM = 128
QK_DIM = N_HEADS * HEAD_DIM
CONV_WIDTH = 4
CONV_DIM = 3 * QK_DIM
N_GROUPS = 4
EXPERTS_PER_GROUP = 8
N_EXPERTS = N_GROUPS * EXPERTS_PER_GROUP
PAST_LEN = 16384
DELTA_CHUNK = 64
TOKEN_TILE = 256
LANES = 128
SUBLANES = 8

OFF_Z = 0
OFF_QKV = OFF_Z + QK_DIM
OFF_POOL = OFF_QKV + CONV_DIM
OFF_B = OFF_POOL + POOL_DIM
OFF_A = OFF_B + LANES
MIX_COLS = OFF_A + LANES
GATE_COLS = 2048
IN_COLS = MIX_COLS + GATE_COLS

POOL_ROWS = 16
CONV_ROWS = 8
VMEM_LIMIT = 60 * 1024 * 1024
EXPERT_PART_ROWS = 128
CONV_GROUPS = 12
SC_ROWS = 96


def _sigmoid(x):
    return 1.0 / (1.0 + jnp.exp(-x))


def _softplus(x):
    return jnp.maximum(x, 0.0) + jnp.log1p(jnp.exp(-jnp.abs(x)))


def _rms(x, w):
    return x * lax.rsqrt(jnp.mean(x * x, axis=-1, keepdims=True) + EPS) * w


def _dot(a, b):
    return jnp.dot(a.astype(BF16), b.astype(BF16), preferred_element_type=F32)


def _dot_nt(a, b):
    return lax.dot_general(a.astype(BF16), b.astype(BF16), (((1,), (1,)), ((), ())), preferred_element_type=F32)


def _dot_tn(a, b):
    return lax.dot_general(a.astype(BF16), b.astype(BF16), (((0,), (0,)), ((), ())), preferred_element_type=F32)


def _pack_pairs(x):
    c = x.shape[1] // 2
    hi = pltpu.bitcast(x[:, :c].astype(BF16).astype(F32), U32)
    lo = pltpu.bitcast(x[:, c:].astype(BF16).astype(F32), U32)
    return hi | lax.shift_right_logical(lo, jnp.uint32(16))


def _unpack_pairs(w):
    left = pltpu.bitcast(w & jnp.uint32(0xFFFF0000), F32)
    right = pltpu.bitcast(lax.shift_left(w, jnp.uint32(16)), F32)
    return left, right


def _inproj_kernel(x_ref, nw_ref, w_ref, mix_ref, gate_ref):
    h = _rms(x_ref[...], nw_ref[...]).astype(BF16)
    half = MIX_COLS // 2
    for c0 in (0, half):
        mix_ref[:, c0:c0 + half] = jnp.dot(h, w_ref[:, c0:c0 + half], preferred_element_type=F32)
    half = GATE_COLS // 2
    for c0 in (0, half):
        gate_ref[:, c0:c0 + half] = jnp.dot(h, w_ref[:, MIX_COLS + c0:MIX_COLS + c0 + half],
                                            preferred_element_type=F32)


def _inproj(x, norm_w, w_in_r, tm):
    n, d = x.shape
    return pl.pallas_call(
        _inproj_kernel,
        grid=(n // tm,),
        in_specs=[
            pl.BlockSpec((tm, d), lambda i: (i, 0)),
            pl.BlockSpec((1, d), lambda i: (0, 0)),
            pl.BlockSpec((d, IN_COLS), lambda i: (0, 0)),
        ],
        out_specs=[pl.BlockSpec((tm, MIX_COLS), lambda i: (i, 0)), pl.BlockSpec((tm, GATE_COLS), lambda i: (i, 0))],
        out_shape=[jax.ShapeDtypeStruct((n, MIX_COLS), F32), jax.ShapeDtypeStruct((n, GATE_COLS), F32)],
        compiler_params=pltpu.CompilerParams(dimension_semantics=("arbitrary",), vmem_limit_bytes=VMEM_LIMIT),
        name="inproj",
    )(x, norm_w, w_in_r)


def _conv_silu_qkv(ext_c, seq, tt, wconv_ref, ci, col0=0):
    base = CONV_ROWS - (CONV_WIDTH - 1)
    cs = slice(ci * LANES, (ci + 1) * LANES)
    cl = slice(ci * LANES - col0, (ci + 1) * LANES - col0)
    idx = (lambda r: (r, cl)) if seq is None else (lambda r: (seq, r, cl))
    acc = ext_c[idx(slice(base, base + tt))] * wconv_ref[0:1, cs]
    for j in range(1, CONV_WIDTH):
        acc = acc + ext_c[idx(slice(base + j, base + j + tt))] * wconv_ref[j:j + 1, cs]
    y = acc * _sigmoid(acc)
    if ci < 2 * N_HEADS:
        y = y * lax.rsqrt(jnp.sum(y * y, axis=-1, keepdims=True) + EPS)
    if ci < N_HEADS:
        y = y * (HEAD_DIM ** -0.5)
    return y


def _inproj_conv_kernel(x_ref, nw_ref, w_ref, wconv_ref, conv0_ref, mix_ref, gate_ref, nconv_ref, h_s, *ext, n_tiles):
    t = pl.program_id(1)
    tm = x_ref.shape[1]
    base = CONV_ROWS - (CONV_WIDTH - 1)
    step = CONV_DIM // len(ext)

    @pl.when(t == 0)
    def _():
        for g, buf in enumerate(ext):
            buf[base:CONV_ROWS, :] = conv0_ref[0, :, g * step:(g + 1) * step]

    h_s[...] = _rms(x_ref[0], nw_ref[...]).astype(BF16)

    def proj(c0, c1):
        return jnp.dot(h_s[...], w_ref[:, c0:c1], preferred_element_type=F32)

    def fill_z0():
        mix_ref[0, :, OFF_Z:OFF_Z + QK_DIM // 2] = proj(OFF_Z, OFF_Z + QK_DIM // 2)

    def fill_z1():
        mix_ref[0, :, OFF_Z + QK_DIM // 2:OFF_QKV] = proj(OFF_Z + QK_DIM // 2, OFF_QKV)

    def fill_pool():
        mix_ref[0, :, OFF_POOL:MIX_COLS] = proj(OFF_POOL, MIX_COLS)

    def fill_gate(i):
        q = GATE_COLS // 4
        gate_ref[0, :, i * q:(i + 1) * q] = proj(MIX_COLS + i * q, MIX_COLS + (i + 1) * q)

    fillers = [fill_z0, fill_z1, fill_pool] + [functools.partial(fill_gate, i) for i in range(4)]
    for g, buf in enumerate(ext):
        c0 = g * step
        buf[CONV_ROWS:CONV_ROWS + tm, :] = proj(OFF_QKV + c0, OFF_QKV + c0 + step)
        if fillers:
            fillers.pop(0)()
        for ci in range(c0 // LANES, (c0 + step) // LANES):
            mix_ref[0, :, OFF_QKV + ci * LANES:OFF_QKV + (ci + 1) * LANES] = _conv_silu_qkv(
                buf, None, tm, wconv_ref, ci, c0)
    for f in fillers:
        f()

    @pl.when(t == n_tiles - 1)
    def _():
        for g, buf in enumerate(ext):
            nconv_ref[0, :, g * step:(g + 1) * step] = buf[tm + base:tm + CONV_ROWS, :]

    for buf in ext:
        buf[0:CONV_ROWS, :] = buf[tm:tm + CONV_ROWS, :]


def _inproj_conv(x, conv0, norm_w, w_in_r, wconv, tm):
    n_seq, seq_len, d = x.shape
    n_tiles = seq_len // tm
    tile = lambda b, t: (b, t, 0)
    const = lambda b, t: (0, 0)
    mix, gates, nconv = pl.pallas_call(
        functools.partial(_inproj_conv_kernel, n_tiles=n_tiles),
        grid=(n_seq, n_tiles),
        in_specs=[
            pl.BlockSpec((1, tm, d), tile),
            pl.BlockSpec((1, d), const),
            pl.BlockSpec((d, IN_COLS), const),
            pl.BlockSpec((SUBLANES, CONV_DIM), const),
            pl.BlockSpec((1, CONV_WIDTH - 1, CONV_DIM), lambda b, t: (b, 0, 0)),
        ],
        out_specs=[
            pl.BlockSpec((1, tm, MIX_COLS), tile),
            pl.BlockSpec((1, tm, GATE_COLS), tile),
            pl.BlockSpec((1, CONV_WIDTH - 1, CONV_DIM), lambda b, t: (b, 0, 0)),
        ],
        out_shape=[
            jax.ShapeDtypeStruct((n_seq, seq_len, MIX_COLS), F32),
            jax.ShapeDtypeStruct((n_seq, seq_len, GATE_COLS), F32),
            jax.ShapeDtypeStruct((n_seq, CONV_WIDTH - 1, CONV_DIM), F32),
        ],
        scratch_shapes=[pltpu.VMEM((tm, d), BF16)] + [
            pltpu.VMEM((CONV_ROWS + tm, CONV_DIM // CONV_GROUPS), F32) for _ in range(CONV_GROUPS)],
        compiler_params=pltpu.CompilerParams(dimension_semantics=("arbitrary", "arbitrary"),
                                             vmem_limit_bytes=VMEM_LIMIT),
        name="inproj_conv",
    )(x, norm_w, w_in_r, wconv, conv0)
    n = n_seq * seq_len
    return mix.reshape(n, MIX_COLS), gates.reshape(n, GATE_COLS), nconv


def _mixer_kernel(proj_ref, pool0_ref, conv0_ref, s0_ref, wpool_ref, pscale_ref, wconv_ref, alog_ref, dtb_ref, onw_ref,
                  ypool_ref, ydelta_ref, npool_ref, nconv_ref, ns_ref,
                  ext_p, ext_c, q_s, k_s, v_s, g_s, b_s, state,
                  *, sb, tt, tv, chunk, pos0, n_tiles, qkv_done):
    t = pl.program_id(1)
    base = CONV_ROWS - (CONV_WIDTH - 1)

    @pl.when(t == 0)
    def _():
        ext_p[:, POOL_ROWS - POOL_BUF:POOL_ROWS, :] = pool0_ref[...]
        if not qkv_done:
            ext_c[:, base:CONV_ROWS, :] = conv0_ref[...]
        state[...] = s0_ref[...]

    row = lax.broadcasted_iota(jnp.int32, (tt, 1), 0)
    pos = pos0 + t * tt + row
    for s in range(sb):
        ext_p[s, POOL_ROWS:POOL_ROWS + tt, :] = proj_ref[s, :, OFF_POOL:OFF_POOL + POOL_DIM]
        if not qkv_done:
            ext_c[s, CONV_ROWS:CONV_ROWS + tt, :] = proj_ref[s, :, OFF_QKV:OFF_QKV + CONV_DIM]

        for gi, win in enumerate(POOL_WINDOWS):
            cs = slice(gi * POOL_GROUP_DIM, (gi + 1) * POOL_GROUP_DIM)
            u = ext_p[s, POOL_ROWS:POOL_ROWS + tt, cs]
            acc = u
            for j in range(1, win):
                acc = acc + ext_p[s, POOL_ROWS - j:POOL_ROWS - j + tt, cs]
            cnt = jnp.minimum(win, pos + 1).astype(F32)
            pooled = acc / cnt - u
            ypool_ref[s, :, cs] = _dot(pooled, wpool_ref[gi]) * pscale_ref[:, cs]

        for ci in range(0 if qkv_done else CONV_DIM // LANES):
            hs = slice((ci % N_HEADS) * HEAD_DIM, (ci % N_HEADS + 1) * HEAD_DIM)
            (q_s, k_s, v_s)[ci // N_HEADS][s, :, hs] = _conv_silu_qkv(ext_c, s, tt, wconv_ref, ci)

        beta = _sigmoid(proj_ref[s, :, OFF_B:OFF_B + LANES])
        g = -jnp.exp(alog_ref[...]) * _softplus(proj_ref[s, :, OFF_A:OFF_A + LANES] + dtb_ref[...])
        if tv < tt:
            beta = jnp.where(row < tv, beta, 0.0)
            g = jnp.where(row < tv, g, 0.0)
        b_s[s] = beta
        g_s[s] = g

    @pl.when(t == n_tiles - 1)
    def _():
        npool_ref[...] = ext_p[:, tv + POOL_ROWS - POOL_BUF:tv + POOL_ROWS, :]
        if qkv_done:
            nconv_ref[...] = jnp.zeros_like(nconv_ref)
        else:
            nconv_ref[...] = ext_c[:, tv + base:tv + CONV_ROWS, :]

    if n_tiles > 1:
        ext_p[:, 0:POOL_ROWS, :] = ext_p[:, tt:tt + POOL_ROWS, :]
        if not qkv_done:
            ext_c[:, 0:CONV_ROWS, :] = ext_c[:, tt:tt + CONV_ROWS, :]

    ri = lax.broadcasted_iota(jnp.int32, (chunk, chunk), 0)
    cj = lax.broadcasted_iota(jnp.int32, (chunk, chunk), 1)
    causal = ri >= cj
    strict = ri > cj
    ltri = causal.astype(F32)
    eye = (ri == cj).astype(F32)
    n_doublings = chunk.bit_length() - 2
    units = [(s, h) for s in range(sb) for h in range(N_HEADS)]
    hsl = [slice(h * HEAD_DIM, (h + 1) * HEAD_DIM) for h in range(N_HEADS)]

    def qkv(which, s, rows, h):
        if qkv_done:
            c0 = OFF_QKV + which * QK_DIM + h * HEAD_DIM
            return proj_ref[s, rows, c0:c0 + HEAD_DIM]
        return (q_s, k_s, v_s)[which][s, rows, hsl[h]]

    def chunk_body(ci, carry):
        r0 = pl.multiple_of(ci * chunk, chunk)
        rows = pl.ds(r0, chunk)
        g_all, g_t, b_all = [], [], []
        for s in range(sb):
            ga = jnp.dot(ltri, g_s[s, rows, :], precision=HIGHEST, preferred_element_type=F32)
            g_all.append(ga)
            if chunk < LANES:
                ga = jnp.concatenate([ga, jnp.zeros((LANES - chunk, LANES), F32)], axis=0)
            g_t.append(ga.T)
            b_all.append(b_s[s, rows, :])
        g_col = [g_all[s][:, h:h + 1] for s, h in units]
        dec = [jnp.exp(jnp.minimum(g_col[i] - g_t[s][h:h + 1, 0:chunk], 0.0)) for i, (s, h) in enumerate(units)]
        b_col = [b_all[s][:, h:h + 1] for s, h in units]
        q = [qkv(0, s, rows, h) for s, h in units]
        k = [qkv(1, s, rows, h) for s, h in units]
        idx = range(len(units))
        kb = [k[i] * b_col[i] for i in idx]
        kbk = [_dot_nt(kb[i], k[i]) for i in idx]
        qk = [_dot_nt(q[i], k[i]) for i in idx]
        p_mat = [jnp.where(strict, kbk[i] * dec[i], 0.0) for i in idx]
        t_inv = [eye - p_mat[i] for i in idx]
        for _ in range(n_doublings):
            p_mat = [_dot(p_mat[i], p_mat[i]) for i in idx]
            xp = [_dot(t_inv[i], p_mat[i]) for i in idx]
            t_inv = [t_inv[i] + xp[i] for i in idx]
        g_exp = [jnp.exp(g_col[i]) for i in idx]
        wu = [_dot(t_inv[i], jnp.concatenate([kb[i] * g_exp[i], qkv(2, s, rows, h) * b_col[i]], axis=1))
              for i, (s, h) in enumerate(units)]
        s_old = [state[s, h] for s, h in units]
        res = [_dot(jnp.concatenate([wu[i][:, 0:HEAD_DIM], q[i] * g_exp[i]], axis=0), s_old[i]) for i in idx]
        v_new = [wu[i][:, HEAD_DIM:2 * HEAD_DIM] - res[i][0:chunk] for i in idx]
        a_qk = [jnp.where(causal, qk[i] * dec[i], 0.0) for i in idx]
        g_last = [g_all[s][chunk - 1:chunk, h:h + 1] for s, h in units]
        k_dec = [k[i] * jnp.exp(g_last[i] - g_col[i]) for i in idx]
        intra = [_dot(a_qk[i], v_new[i]) for i in idx]
        upd = [_dot_tn(k_dec[i], v_new[i]) for i in idx]
        for i, (s, h) in enumerate(units):
            state[s, h] = s_old[i] * jnp.exp(g_last[i]) + upd[i]
            o_c = res[i][chunk:2 * chunk] + intra[i]
            z_c = proj_ref[s, rows, OFF_Z + h * HEAD_DIM:OFF_Z + (h + 1) * HEAD_DIM]
            ydelta_ref[s, rows, hsl[h]] = _rms(o_c, onw_ref[...]) * (z_c * _sigmoid(z_c))
        return carry

    lax.fori_loop(0, tt // chunk, chunk_body, 0)

    @pl.when(t == n_tiles - 1)
    def _():
        ns_ref[...] = state[...]


def _mixers(proj, pool0, conv0, s0, wpool, pscale, wconv, alog, dtb, onw, *, n_seq, sb, tt, tv, chunk, pos0, n_tiles,
            qkv_done):
    kern = functools.partial(_mixer_kernel, sb=sb, tt=tt, tv=tv, chunk=chunk, pos0=pos0, n_tiles=n_tiles,
                             qkv_done=qkv_done)
    seq_len = n_tiles * tt
    tile = lambda b, t: (b, t, 0)
    seq3 = lambda b, t: (b, 0, 0)
    seq4 = lambda b, t: (b, 0, 0, 0)
    const2 = lambda b, t: (0, 0)
    ypool, ydelta, npool, nconv, ns = pl.pallas_call(
        kern,
        grid=(n_seq // sb, n_tiles),
        in_specs=[
            pl.BlockSpec((sb, tt, MIX_COLS), tile),
            pl.BlockSpec((sb, POOL_BUF, POOL_DIM), seq3),
            pl.BlockSpec((sb, CONV_WIDTH - 1, CONV_DIM), seq3),
            pl.BlockSpec((sb, N_HEADS, HEAD_DIM, HEAD_DIM), seq4),
            pl.BlockSpec((len(POOL_WINDOWS), POOL_GROUP_DIM, POOL_GROUP_DIM), lambda b, t: (0, 0, 0)),
            pl.BlockSpec((1, POOL_DIM), const2),
            pl.BlockSpec((SUBLANES, CONV_DIM), const2),
            pl.BlockSpec((1, LANES), const2),
            pl.BlockSpec((1, LANES), const2),
            pl.BlockSpec((1, HEAD_DIM), const2),
        ],
        out_specs=[
            pl.BlockSpec((sb, tt, POOL_DIM), tile),
            pl.BlockSpec((sb, tt, QK_DIM), tile),
            pl.BlockSpec((sb, POOL_BUF, POOL_DIM), seq3),
            pl.BlockSpec((sb, CONV_WIDTH - 1, CONV_DIM), seq3),
            pl.BlockSpec((sb, N_HEADS, HEAD_DIM, HEAD_DIM), seq4),
        ],
        out_shape=[
            jax.ShapeDtypeStruct((n_seq, seq_len, POOL_DIM), F32),
            jax.ShapeDtypeStruct((n_seq, seq_len, QK_DIM), F32),
            jax.ShapeDtypeStruct((n_seq, POOL_BUF, POOL_DIM), F32),
            jax.ShapeDtypeStruct((n_seq, CONV_WIDTH - 1, CONV_DIM), F32),
            jax.ShapeDtypeStruct((n_seq, N_HEADS, HEAD_DIM, HEAD_DIM), F32),
        ],
        scratch_shapes=[
            pltpu.VMEM((sb, POOL_ROWS + tt, POOL_DIM), F32),
            pltpu.VMEM((sb, CONV_ROWS + tt, CONV_DIM), F32),
            pltpu.VMEM((sb, tt, QK_DIM), F32),
            pltpu.VMEM((sb, tt, QK_DIM), F32),
            pltpu.VMEM((sb, tt, QK_DIM), F32),
            pltpu.VMEM((sb, tt, LANES), F32),
            pltpu.VMEM((sb, tt, LANES), F32),
            pltpu.VMEM((sb, N_HEADS, HEAD_DIM, HEAD_DIM), F32),
        ],
        compiler_params=pltpu.CompilerParams(dimension_semantics=("arbitrary", "arbitrary"),
                                             vmem_limit_bytes=VMEM_LIMIT),
        name="mixers",
    )(proj.reshape(n_seq, seq_len, MIX_COLS), pool0, conv0, s0, wpool, pscale, wconv, alog, dtb, onw)
    n = n_seq * seq_len
    return ypool.reshape(n, POOL_DIM), ydelta.reshape(n, QK_DIM), npool, nconv, ns


def _merge_kernel(x_ref, gates_ref, ypool_ref, ydelta_ref, wbp_ref, wbd_ref, wout_ref, n2w_ref, wr_hi_ref, wr_lo_ref,
                  br_ref, xm_ref, h2_ref, route_ref, wk_ref, cnt_ref, cnt_s, *, period, valid):
    tm = x_ref.shape[0]

    @pl.when(pl.program_id(0) == 0)
    def _():
        cnt_s[...] = jnp.zeros_like(cnt_s)

    bp = _dot(ypool_ref[...], wbp_ref[...])
    bd = _dot(ydelta_ref[...], wbd_ref[...])
    merged = _sigmoid(gates_ref[:, 0:1024]) * bp + _sigmoid(gates_ref[:, 1024:2048]) * bd
    xm = x_ref[...] + _dot(merged, wout_ref[...])
    xm_ref[...] = xm
    h2 = _rms(xm, n2w_ref[...])
    h2_ref[...] = _pack_pairs(h2)

    h2_hi = h2.astype(BF16)
    h2_lo = (h2 - h2_hi.astype(F32)).astype(BF16)
    logits = (jnp.dot(h2_hi, wr_hi_ref[...], preferred_element_type=F32)
              + (jnp.dot(h2_hi, wr_lo_ref[...], preferred_element_type=F32)
                 + jnp.dot(h2_lo, wr_hi_ref[...], preferred_element_type=F32))) + br_ref[...]
    lane = lax.broadcasted_iota(jnp.int32, logits.shape, 1)
    neg = -jnp.inf
    far = LANES - 1
    is_g = lane < N_GROUPS
    g_max = jnp.max(jnp.where(is_g, logits, neg), axis=-1, keepdims=True)
    g_sel = jnp.min(jnp.where(is_g & (logits == g_max), lane, far), axis=-1, keepdims=True)
    p_g = 1.0 / jnp.sum(jnp.where(is_g, jnp.exp(logits - g_max), 0.0), axis=-1, keepdims=True)
    e_lane = lane - N_GROUPS
    is_e = (e_lane >= 0) & (e_lane < N_EXPERTS) & ((e_lane // EXPERTS_PER_GROUP) == g_sel)
    ev = jnp.where(is_e, logits, neg)
    v1 = jnp.max(ev, axis=-1, keepdims=True)
    i1 = jnp.min(jnp.where(is_e & (ev == v1), lane, far), axis=-1, keepdims=True)
    is_e2 = is_e & (lane != i1)
    ev2 = jnp.where(is_e2, logits, neg)
    v2 = jnp.max(ev2, axis=-1, keepdims=True)
    i2 = jnp.min(jnp.where(is_e2 & (ev2 == v2), lane, far), axis=-1, keepdims=True)
    e21 = jnp.exp(v2 - v1)
    w1 = p_g / (1.0 + e21)
    w2 = p_g * e21 / (1.0 + e21)
    e1 = i1 - N_GROUPS
    e2 = i2 - N_GROUPS
    chosen = (lane == e1) | (lane == e2)
    if valid < period:
        row = lax.broadcasted_iota(jnp.int32, (tm, 1), 0)
        chosen = chosen & (lax.rem(row, period) < valid)
    onehot = jnp.where(chosen, 1.0, 0.0)
    rr = lax.broadcasted_iota(jnp.int32, (tm, tm), 0)
    cc = lax.broadcasted_iota(jnp.int32, (tm, tm), 1)
    earlier = jnp.where(rr > cc, 1.0, 0.0).astype(BF16)
    before = jnp.dot(earlier, onehot.astype(BF16), preferred_element_type=F32) + cnt_s[...]
    r1 = jnp.sum(jnp.where(lane == e1, before, 0.0), axis=-1, keepdims=True)
    r2 = jnp.sum(jnp.where(lane == e2, before, 0.0), axis=-1, keepdims=True)
    cnt_s[...] = cnt_s[...] + jnp.sum(onehot, axis=0, keepdims=True)
    cnt_ref[...] = cnt_s[...]
    record = jnp.where(lane == 0, e1.astype(F32), jnp.where(lane == 1, e2.astype(F32),
                                                           jnp.where(lane == 2, r1, jnp.where(lane == 3, r2, 0.0))))
    route_ref[...] = record.T[0:SUBLANES, :]
    lane8 = lax.broadcasted_iota(jnp.int32, (tm, SUBLANES), 1)
    wk_ref[...] = jnp.where(lane8 == 0, w1, jnp.where(lane8 == 1, w2, 0.0))


def _merge(x, proj, ypool, ydelta, wbp, wbd, wout, n2w, wr_hi, wr_lo, br, tm, period, valid):
    n, d = x.shape
    row = lambda i: (i, 0)
    const = lambda i: (0, 0)
    return pl.pallas_call(
        functools.partial(_merge_kernel, period=period, valid=valid),
        grid=(n // tm,),
        in_specs=[
            pl.BlockSpec((tm, d), row),
            pl.BlockSpec((tm, GATE_COLS), row),
            pl.BlockSpec((tm, POOL_DIM), row),
            pl.BlockSpec((tm, QK_DIM), row),
            pl.BlockSpec((POOL_DIM, d), const),
            pl.BlockSpec((QK_DIM, d), const),
            pl.BlockSpec((d, d), const),
            pl.BlockSpec((1, d), const),
            pl.BlockSpec((d, LANES), const),
            pl.BlockSpec((d, LANES), const),
            pl.BlockSpec((1, LANES), const),
        ],
        out_specs=[
            pl.BlockSpec((tm, d), row),
            pl.BlockSpec((tm, d // 2), row),
            pl.BlockSpec((SUBLANES, tm), lambda i: (0, i)),
            pl.BlockSpec((tm, SUBLANES), row),
            pl.BlockSpec((1, LANES), const),
        ],
        out_shape=[
            jax.ShapeDtypeStruct((n, d), F32),
            jax.ShapeDtypeStruct((n, d // 2), U32),
            jax.ShapeDtypeStruct((SUBLANES, n), F32),
            jax.ShapeDtypeStruct((n, SUBLANES), F32),
            jax.ShapeDtypeStruct((1, LANES), F32),
        ],
        scratch_shapes=[pltpu.VMEM((1, LANES), F32)],
        compiler_params=pltpu.CompilerParams(dimension_semantics=("arbitrary",), vmem_limit_bytes=VMEM_LIMIT),
        name="merge",
    )(x, proj, ypool, ydelta, wbp, wbd, wout, n2w, wr_hi, wr_lo, br)


def _expert_kernel(te_ref, nv_ref, x_ref, wg_hbm, wu_hbm, wd_hbm, y_ref, st_g, st_u, st_d, wg_s, wu_s, wd_s, sem, slot_s):
    i = pl.program_id(0)
    n_tiles = pl.num_programs(0)
    n_valid = nv_ref[i]
    expert = te_ref[i]

    def weight_copies(e, slot):
        return [pltpu.make_async_copy(src.at[e], dst.at[slot], sem.at[slot])
                for src, dst in ((wg_hbm, st_g), (wu_hbm, st_u), (wd_hbm, st_d))]

    @pl.when(i == 0)
    def _():
        slot_s[0] = 0
        for cp in weight_copies(expert, 0):
            cp.start()

    @pl.when((i == 0) | (expert != te_ref[jnp.maximum(i - 1, 0)]))
    def _():
        slot = slot_s[0]
        nxt = lax.while_loop(lambda j: (j < n_tiles) & (te_ref[jnp.minimum(j, n_tiles - 1)] == expert),
                             lambda j: j + 1, i + 1)

        @pl.when(nxt < n_tiles)
        def _():
            for cp in weight_copies(te_ref[jnp.minimum(nxt, n_tiles - 1)], 1 - slot):
                cp.start()
        for cp in weight_copies(expert, slot):
            cp.wait()
        wg_s[...] = st_g[slot].astype(BF16)
        wu_s[...] = st_u[slot].astype(BF16)
        wd_s[...] = st_d[slot].astype(BF16)
        slot_s[0] = 1 - slot

    @pl.when(n_valid > 0)
    def _():
        tm = x_ref.shape[0]
        part = min(tm, EXPERT_PART_ROWS)
        parts = [slice(r0, r0 + part) for r0 in range(0, tm, part)]
        row = lax.broadcasted_iota(jnp.int32, (part, 1), 0)
        xs = []
        for p in parts:
            left, right = _unpack_pairs(jnp.where(row + p.start < n_valid, x_ref[p, :], jnp.uint32(0)))
            xs.append(jnp.concatenate([left.astype(BF16), right.astype(BF16)], axis=1))
        a = [_dot(x, wg_s[...]) for x in xs]
        b = [_dot(x, wu_s[...]) for x in xs]
        act = [(a[j] * _sigmoid(a[j])) * b[j] for j in range(len(parts))]
        y = [_dot(act[j], wd_s[...]) for j in range(len(parts))]
        for j, p in enumerate(parts):
            y_ref[p, :] = _pack_pairs(y[j])

    @pl.when(n_valid == 0)
    def _():
        y_ref[...] = jnp.zeros_like(y_ref)


def _experts(tile_expert, tile_valid, xs, wg, wu, wd, tm):
    n_rows, dp = xs.shape
    d, f = wg.shape[1], wg.shape[2]
    grid_spec = pltpu.PrefetchScalarGridSpec(
        num_scalar_prefetch=2,
        grid=(n_rows // tm,),
        in_specs=[
            pl.BlockSpec((tm, dp), lambda i, te, nv: (jnp.where(nv[i] > 0, i, 0), 0)),
            pl.BlockSpec(memory_space=pl.ANY),
            pl.BlockSpec(memory_space=pl.ANY),
            pl.BlockSpec(memory_space=pl.ANY),
        ],
        out_specs=pl.BlockSpec((tm, dp), lambda i, te, nv: (i, 0)),
        scratch_shapes=[pltpu.VMEM((2, d, f), F32), pltpu.VMEM((2, d, f), F32), pltpu.VMEM((2, f, d), F32),
                        pltpu.VMEM((d, f), BF16), pltpu.VMEM((d, f), BF16), pltpu.VMEM((f, d), BF16),
                        pltpu.SemaphoreType.DMA((2,)), pltpu.SMEM((1,), jnp.int32)],
    )
    return pl.pallas_call(
        _expert_kernel,
        grid_spec=grid_spec,
        out_shape=jax.ShapeDtypeStruct((n_rows, dp), U32),
        compiler_params=pltpu.CompilerParams(dimension_semantics=("arbitrary",), vmem_limit_bytes=VMEM_LIMIT),
        name="experts",
    )(tile_expert, tile_valid, xs, wg, wu, wd)


def _plan_kernel(route_ref, cnt_ref, pos_ref, tiles_ref, *, tm_e):
    counts = cnt_ref[...]
    padded = jnp.floor((counts + (tm_e - 1)) / tm_e) * tm_e
    lane = lax.broadcasted_iota(jnp.int32, counts.shape, 1)
    ends = padded
    shift = 1
    while shift < N_EXPERTS:
        ends = ends + jnp.where(lane >= shift, pltpu.roll(ends, shift, axis=1), 0.0)
        shift *= 2
    starts = ends - padded

    def lookup(table, key):
        out = jnp.zeros_like(key)
        for e in range(N_EXPERTS):
            out = out + jnp.where(key == e, table[:, e:e + 1], 0.0)
        return out

    for k in range(2):
        pos = lookup(starts, route_ref[k:k + 1, :]) + route_ref[2 + k:3 + k, :]
        pos_ref[k:k + 1, :] = pos.astype(jnp.int32)

    tile_start = lax.broadcasted_iota(jnp.int32, (1, tiles_ref.shape[1]), 1).astype(F32) * tm_e
    tile_expert = jnp.zeros_like(tile_start)
    for e in range(N_EXPERTS):
        tile_expert = tile_expert + jnp.where(ends[:, e:e + 1] <= tile_start, 1.0, 0.0)
    tile_expert = jnp.minimum(tile_expert, N_EXPERTS - 1.0)
    valid = lookup(counts, tile_expert) - (tile_start - lookup(starts, tile_expert))
    tiles_ref[0:1, :] = tile_expert.astype(jnp.int32)
    tiles_ref[1:2, :] = jnp.clip(valid, 0.0, tm_e).astype(jnp.int32)


def _plan(route_t, counts, tm_e, n_tiles):
    n = route_t.shape[1]
    tiles_pad = -(-n_tiles // LANES) * LANES
    return pl.pallas_call(
        functools.partial(_plan_kernel, tm_e=tm_e),
        out_shape=[jax.ShapeDtypeStruct((2, n), jnp.int32), jax.ShapeDtypeStruct((2, tiles_pad), jnp.int32)],
        name="plan",
    )(route_t, counts)


def _sc_workers():
    info = plsc.get_sparse_core_info()
    return info.num_cores, info.num_subcores


def _sc_chunk(per_worker):
    assert per_worker % SUBLANES == 0
    return max(c for c in range(SUBLANES, SC_ROWS + 1, SUBLANES) if per_worker % c == 0)


def _sc_scatter2(rows, idx_a, idx_b, n_out):
    n_cores, n_sub = _sc_workers()
    n, d = rows.shape
    per_worker = n // (n_cores * n_sub)
    assert per_worker * n_cores * n_sub == n
    chunk = _sc_chunk(per_worker)
    mesh = plsc.VectorSubcoreMesh(core_axis_name="c", subcore_axis_name="s")

    n_chunks = per_worker // chunk
    buf = lambda shape, dtype: [pltpu.VMEM(shape, dtype), pltpu.VMEM(shape, dtype)]

    @functools.partial(
        pl.kernel, mesh=mesh, out_type=jax.ShapeDtypeStruct((n_out, d), rows.dtype),
        scratch_types=buf((chunk,), jnp.int32) + buf((chunk,), jnp.int32) + buf((chunk, d), rows.dtype)
        + [pltpu.SemaphoreType.DMA, pltpu.SemaphoreType.DMA])
    def scatter_rows(rows_hbm, ia_hbm, ib_hbm, out_hbm, ia0, ia1, ib0, ib1, r0, r1, sem0, sem1):
        worker = lax.axis_index("s") * n_cores + lax.axis_index("c")
        base = worker * per_worker
        slots = ((ia0, ib0, r0, sem0), (ia1, ib1, r1, sem1))
        pending = {}
        for j in range(n_chunks):
            ia_v, ib_v, rows_v, sem = slots[j % 2]
            if j >= 2:
                for cp in pending.pop(j - 2):
                    cp.wait()
            off = pl.multiple_of(base + j * chunk, SUBLANES)
            pltpu.sync_copy(ia_hbm.at[pl.ds(off, chunk)], ia_v)
            pltpu.sync_copy(ib_hbm.at[pl.ds(off, chunk)], ib_v)
            pltpu.sync_copy(rows_hbm.at[pl.ds(off, chunk)], rows_v)
            pending[j] = (pltpu.async_copy(rows_v, out_hbm.at[ia_v], sem),
                          pltpu.async_copy(rows_v, out_hbm.at[ib_v], sem))
        for j in sorted(pending):
            for cp in pending[j]:
                cp.wait()

    return scatter_rows(rows, idx_a, idx_b)


def _sc_gather(table, idx):
    n_cores, n_sub = _sc_workers()
    n_idx = idx.shape[0]
    d = table.shape[1]
    per_worker = n_idx // (n_cores * n_sub)
    assert per_worker * n_cores * n_sub == n_idx
    chunk = _sc_chunk(per_worker)
    mesh = plsc.VectorSubcoreMesh(core_axis_name="c", subcore_axis_name="s")

    n_chunks = per_worker // chunk

    @functools.partial(
        pl.kernel, mesh=mesh, out_type=jax.ShapeDtypeStruct((n_idx, d), table.dtype),
        scratch_types=[pltpu.VMEM((chunk,), jnp.int32), pltpu.VMEM((chunk,), jnp.int32),
                       pltpu.VMEM((chunk, d), table.dtype), pltpu.VMEM((chunk, d), table.dtype),
                       pltpu.SemaphoreType.DMA, pltpu.SemaphoreType.DMA])
    def gather_rows(table_hbm, idx_hbm, out_hbm, i0, i1, r0, r1, sem0, sem1):
        worker = lax.axis_index("s") * n_cores + lax.axis_index("c")
        base = worker * per_worker
        slots = ((i0, r0, sem0), (i1, r1, sem1))

        def start(j):
            idx_v, rows_v, sem = slots[j % 2]
            off = pl.multiple_of(base + j * chunk, SUBLANES)
            pltpu.sync_copy(idx_hbm.at[pl.ds(off, chunk)], idx_v)
            return pltpu.async_copy(table_hbm.at[idx_v], rows_v, sem)

        nxt = start(0)
        for j in range(n_chunks):
            cur = nxt
            if j + 1 < n_chunks:
                nxt = start(j + 1)
            cur.wait()
            off = pl.multiple_of(base + j * chunk, SUBLANES)
            pltpu.sync_copy(slots[j % 2][1], out_hbm.at[pl.ds(off, chunk)])

    return gather_rows(table, idx)


def _finalize_kernel(xm_ref, ya_ref, yb_ref, wk_ref, fnw_ref, o_ref):
    d = xm_ref.shape[1]
    c = d // 2
    w = wk_ref[...]
    a_left, a_right = _unpack_pairs(ya_ref[...])
    b_left, b_right = _unpack_pairs(yb_ref[...])
    x_left = xm_ref[:, 0:c] + (w[:, 0:1] * a_left + w[:, 1:2] * b_left)
    x_right = xm_ref[:, c:d] + (w[:, 0:1] * a_right + w[:, 1:2] * b_right)
    ms = (jnp.sum(x_left * x_left, axis=-1, keepdims=True) + jnp.sum(x_right * x_right, axis=-1, keepdims=True)) / d
    scale = lax.rsqrt(ms + EPS)
    o_ref[:, 0:c] = x_left * scale * fnw_ref[:, 0:c]
    o_ref[:, c:d] = x_right * scale * fnw_ref[:, c:d]


def _finalize(xm, yk, wk, fnw, tm):
    n, d = xm.shape
    steps = n // tm
    return pl.pallas_call(
        _finalize_kernel,
        grid=(steps,),
        in_specs=[
            pl.BlockSpec((tm, d), lambda i: (i, 0)),
            pl.BlockSpec((tm, d // 2), lambda i: (i, 0)),
            pl.BlockSpec((tm, d // 2), lambda i: (i + steps, 0)),
            pl.BlockSpec((tm, SUBLANES), lambda i: (i, 0)),
            pl.BlockSpec((1, d), lambda i: (0, 0)),
        ],
        out_specs=pl.BlockSpec((tm, d), lambda i: (i, 0)),
        out_shape=jax.ShapeDtypeStruct((n, d), F32),
        compiler_params=pltpu.CompilerParams(dimension_semantics=("arbitrary",), vmem_limit_bytes=VMEM_LIMIT),
        name="finalize",
    )(xm, yk, yk, wk, fnw)


def _moe(h2, route, counts, wg, wu, wd, *, tm_e):
    n = h2.shape[0]
    n_rows = 2 * n + N_EXPERTS * tm_e
    n_tiles = n_rows // tm_e
    pos, tiles = _plan(route, counts, tm_e, n_tiles)
    xs = _sc_scatter2(h2, pos[0], pos[1], n_rows)
    y_sorted = _experts(tiles[0, :n_tiles], tiles[1, :n_tiles], xs, wg, wu, wd, tm_e)
    return _sc_gather(y_sorted, pos.reshape(-1))


def _layer(x_tokens, pool0, conv0, s0, prm, *, n_seq, sb, tt, tv, chunk, pos0, n_tiles, tm):
    conv_in_proj = tv == tt
    if conv_in_proj:
        proj, gates, nconv_proj = _inproj_conv(x_tokens.reshape(n_seq, n_tiles * tt, -1), conv0, prm["norm1_w"],
                                               prm["w_in"], prm["w_conv"], tm)
    else:
        proj, gates = _inproj(x_tokens, prm["norm1_w"], prm["w_in"], tm)
    ypool, ydelta, npool, nconv, ns = _mixers(
        proj, pool0, conv0, s0, prm["w_pool"], prm["pool_scale"], prm["w_conv"], prm["a_log"], prm["dt_bias"],
        prm["o_norm_w"], n_seq=n_seq, sb=sb, tt=tt, tv=tv, chunk=chunk, pos0=pos0, n_tiles=n_tiles,
        qkv_done=conv_in_proj)
    if conv_in_proj:
        nconv = nconv_proj
    xm, h2, route, wk, counts = _merge(
        x_tokens, gates, ypool, ydelta, prm["w_branch_pool"], prm["w_branch_delta"], prm["w_out"], prm["norm2_w"],
        prm["w_router_hi"], prm["w_router_lo"], prm["b_router"], 2 * tm, tt, tv)
    return xm, h2, route, wk, counts, npool, nconv, ns


def _pad_lanes(v, width=LANES):
    v = v.reshape(1, -1).astype(F32)
    return jnp.pad(v, ((0, 0), (0, width - v.shape[1])))


def kernel(x_prompt, x_sample, cache_pool, cache_conv, state_delta, norm1_w, w_in, w_pool, pool_scale, w_conv, a_log, dt_bias, o_norm_w, w_branch_pool, w_branch_delta, w_out, norm2_w, w_router_group, b_router_group, w_router_expert, b_router_expert, w_gate, w_up, w_down, final_norm_w):
    n_layers = norm1_w.shape[0]
    assert n_layers == 1, "single-layer step"
    bsz, seq, d = x_prompt.shape
    dbs, dseq, _ = x_sample.shape
    lyr = 0

    wi = w_in[lyr]
    c = [POOL_DIM, POOL_DIM + CONV_DIM]
    c += [c[1] + N_HEADS, c[1] + 2 * N_HEADS]
    c += [c[3] + QK_DIM, c[3] + QK_DIM + d]
    pad8 = lambda m: jnp.pad(m, ((0, 0), (0, LANES - N_HEADS)))
    w_in_r = jnp.concatenate(
        [wi[:, c[3]:c[4]], wi[:, c[0]:c[1]], wi[:, :c[0]], pad8(wi[:, c[1]:c[2]]), pad8(wi[:, c[2]:c[3]]),
         wi[:, c[4]:c[5]], wi[:, c[5]:]], axis=1).astype(BF16)
    w_router = jnp.concatenate(
        [w_router_group[lyr], w_router_expert[lyr].reshape(d, N_EXPERTS),
         jnp.zeros((d, LANES - N_GROUPS - N_EXPERTS), F32)], axis=1).astype(F32)
    w_router_hi = w_router.astype(BF16)
    b_router = _pad_lanes(jnp.concatenate([b_router_group[lyr], b_router_expert[lyr].reshape(-1)]))
    prm = dict(
        norm1_w=norm1_w[lyr].reshape(1, d), w_in=w_in_r,
        w_pool=w_pool[lyr].astype(BF16), pool_scale=pool_scale[lyr].reshape(1, POOL_DIM),
        w_conv=jnp.pad(w_conv[lyr], ((0, SUBLANES - CONV_WIDTH), (0, 0))),
        a_log=_pad_lanes(a_log[lyr]), dt_bias=_pad_lanes(dt_bias[lyr]), o_norm_w=o_norm_w[lyr].reshape(1, HEAD_DIM),
        w_branch_pool=w_branch_pool[lyr].astype(BF16), w_branch_delta=w_branch_delta[lyr].astype(BF16),
        w_out=w_out[lyr].astype(BF16), norm2_w=norm2_w[lyr].reshape(1, d),
        w_router_hi=w_router_hi, w_router_lo=(w_router - w_router_hi.astype(F32)).astype(BF16), b_router=b_router,
    )
    wg, wu, wd = w_gate[lyr], w_up[lyr], w_down[lyr]
    fnw = final_norm_w.reshape(1, d)

    tt_p = TOKEN_TILE // 2
    xm, h2, route, wk, counts, npool, nconv, ns = _layer(
        x_prompt.reshape(bsz * seq, d),
        jnp.zeros((bsz, POOL_BUF, POOL_DIM), F32), jnp.zeros((bsz, CONV_WIDTH - 1, CONV_DIM), F32),
        jnp.zeros((bsz, N_HEADS, HEAD_DIM, HEAD_DIM), F32), prm,
        n_seq=bsz, sb=4, tt=tt_p, tv=tt_p, chunk=DELTA_CHUNK, pos0=0, n_tiles=seq // tt_p, tm=TOKEN_TILE)
    yk = _moe(h2, route, counts, wg, wu, wd, tm_e=2 * TOKEN_TILE)
    y_prompt = _finalize(xm, yk, wk, fnw, 2 * TOKEN_TILE).reshape(bsz, seq, d)
    pool_p = npool[None]
    conv_p = nconv[None]
    delta_p = ns[None]

    tt_s = SUBLANES
    xs_pad = jnp.pad(x_sample, ((0, 0), (0, tt_s - dseq), (0, 0))).reshape(dbs * tt_s, d)
    xm, h2, route, wk, counts, npool, nconv, ns = _layer(
        xs_pad, cache_pool[lyr], cache_conv[lyr], state_delta[lyr], prm,
        n_seq=dbs, sb=8, tt=tt_s, tv=dseq, chunk=tt_s, pos0=PAST_LEN, n_tiles=1, tm=TOKEN_TILE)
    real = lambda a: a.reshape(dbs, tt_s, -1)[:, :dseq].reshape(dbs * dseq, -1)
    route = route.reshape(SUBLANES, dbs, tt_s)[:, :, :dseq].reshape(SUBLANES, dbs * dseq)
    yk = _moe(real(h2), route, counts, wg, wu, wd, tm_e=TOKEN_TILE // 2)
    y_sample = _finalize(real(xm), yk, real(wk), fnw, TOKEN_TILE).reshape(dbs, dseq, d)
    pool_s = npool[None]
    conv_s = nconv[None]
    delta_s = ns[None]
    return (y_prompt, y_sample, pool_p, conv_p, delta_p, pool_s, conv_s, delta_s)
```

```python
import functools

import jax
import jax.numpy as jnp
from jax import lax
from jax.experimental import pallas as pl
from jax.experimental.pallas import tpu as pltpu
from jax.experimental.pallas import tpu_sc as plsc

F32 = jnp.float32
BF16 = jnp.bfloat16
U32 = jnp.uint32
EPS = 1e-6
HIGHEST = lax.Precision.HIGHEST

POOL_WINDOWS = (2, 4, 8, 16)
POOL_GROUP_DIM = 128
POOL_DIM = len(POOL_WINDOWS) * POOL_GROUP_DIM
POOL_BUF = max(POOL_WINDOWS) - 1
N_HEADS = 8
HEAD_DIM = 128
QK_DIM = N_HEADS * HEAD_DIM
CONV_WIDTH = 4
CONV_DIM = 3 * QK_DIM
N_GROUPS = 4
EXPERTS_PER_GROUP = 8
N_EXPERTS = N_GROUPS * EXPERTS_PER_GROUP
PAST_LEN = 16384
DELTA_CHUNK = 64
TOKEN_TILE = 256
LANES = 128
SUBLANES = 8

OFF_Z = 0
OFF_QKV = OFF_Z + QK_DIM
OFF_POOL = OFF_QKV + CONV_DIM
OFF_B = OFF_POOL + POOL_DIM
OFF_A = OFF_B + LANES
MIX_COLS = OFF_A + LANES
GATE_COLS = 2048
IN_COLS = MIX_COLS + GATE_COLS

POOL_ROWS = 16
CONV_ROWS = 8
VMEM_LIMIT = 60 * 1024 * 1024
EXPERT_PART_ROWS = 128
EXPERT_STAGES = 4
CONV_GROUPS = 12
SC_ROWS = 96


def _sigmoid(x):
    return 1.0 / (1.0 + jnp.exp(-x))


def _softplus(x):
    return jnp.maximum(x, 0.0) + jnp.log1p(jnp.exp(-jnp.abs(x)))


def _rms(x, w):
    return x * lax.rsqrt(jnp.mean(x * x, axis=-1, keepdims=True) + EPS) * w


def _dot(a, b):
    return jnp.dot(a.astype(BF16), b.astype(BF16), preferred_element_type=F32)


def _dot_nt(a, b):
    return lax.dot_general(a.astype(BF16), b.astype(BF16), (((1,), (1,)), ((), ())), preferred_element_type=F32)


def _dot_tn(a, b):
    return lax.dot_general(a.astype(BF16), b.astype(BF16), (((0,), (0,)), ((), ())), preferred_element_type=F32)


def _pack_pairs(x):
    c = x.shape[1] // 2
    hi = pltpu.bitcast(x[:, :c].astype(BF16).astype(F32), U32)
    lo = pltpu.bitcast(x[:, c:].astype(BF16).astype(F32), U32)
    return hi | lax.shift_right_logical(lo, jnp.uint32(16))


def _unpack_pairs(w):
    left = pltpu.bitcast(w & jnp.uint32(0xFFFF0000), F32)
    right = pltpu.bitcast(lax.shift_left(w, jnp.uint32(16)), F32)
    return left, right


def _inproj_kernel(x_ref, nw_ref, w_ref, mix_ref, gate_ref):
    h = _rms(x_ref[...], nw_ref[...]).astype(BF16)
    half = MIX_COLS // 2
    for c0 in (0, half):
        mix_ref[:, c0:c0 + half] = jnp.dot(h, w_ref[:, c0:c0 + half], preferred_element_type=F32)
    half = GATE_COLS // 2
    for c0 in (0, half):
        gate_ref[:, c0:c0 + half] = jnp.dot(h, w_ref[:, MIX_COLS + c0:MIX_COLS + c0 + half],
                                            preferred_element_type=F32)


def _inproj(x, norm_w, w_in_r, tm):
    n, d = x.shape
    return pl.pallas_call(
        _inproj_kernel,
        grid=(n // tm,),
        in_specs=[
            pl.BlockSpec((tm, d), lambda i: (i, 0)),
            pl.BlockSpec((1, d), lambda i: (0, 0)),
            pl.BlockSpec((d, IN_COLS), lambda i: (0, 0)),
        ],
        out_specs=[pl.BlockSpec((tm, MIX_COLS), lambda i: (i, 0)), pl.BlockSpec((tm, GATE_COLS), lambda i: (i, 0))],
        out_shape=[jax.ShapeDtypeStruct((n, MIX_COLS), F32), jax.ShapeDtypeStruct((n, GATE_COLS), F32)],
        compiler_params=pltpu.CompilerParams(dimension_semantics=("arbitrary",), vmem_limit_bytes=VMEM_LIMIT),
        name="inproj",
    )(x, norm_w, w_in_r)


def _conv_silu_qkv(ext_c, seq, tt, wconv_ref, ci, col0=0):
    base = CONV_ROWS - (CONV_WIDTH - 1)
    cs = slice(ci * LANES, (ci + 1) * LANES)
    cl = slice(ci * LANES - col0, (ci + 1) * LANES - col0)
    idx = (lambda r: (r, cl)) if seq is None else (lambda r: (seq, r, cl))
    acc = ext_c[idx(slice(base, base + tt))] * wconv_ref[0:1, cs]
    for j in range(1, CONV_WIDTH):
        acc = acc + ext_c[idx(slice(base + j, base + j + tt))] * wconv_ref[j:j + 1, cs]
    y = acc * _sigmoid(acc)
    if ci < 2 * N_HEADS:
        y = y * lax.rsqrt(jnp.sum(y * y, axis=-1, keepdims=True) + EPS)
    if ci < N_HEADS:
        y = y * (HEAD_DIM ** -0.5)
    return y


def _inproj_conv_kernel(x_ref, nw_ref, w_ref, wconv_ref, conv0_ref, mix_ref, gate_ref, nconv_ref, h_s, *ext, n_tiles):
    t = pl.program_id(1)
    tm = x_ref.shape[1]
    base = CONV_ROWS - (CONV_WIDTH - 1)
    step = CONV_DIM // len(ext)

    @pl.when(t == 0)
    def _():
        for g, buf in enumerate(ext):
            buf[base:CONV_ROWS, :] = conv0_ref[0, :, g * step:(g + 1) * step]

    h_s[...] = _rms(x_ref[0], nw_ref[...]).astype(BF16)

    def proj(c0, c1):
        return jnp.dot(h_s[...], w_ref[:, c0:c1], preferred_element_type=F32)

    def fill_z0():
        mix_ref[0, :, OFF_Z:OFF_Z + QK_DIM // 2] = proj(OFF_Z, OFF_Z + QK_DIM // 2)

    def fill_z1():
        mix_ref[0, :, OFF_Z + QK_DIM // 2:OFF_QKV] = proj(OFF_Z + QK_DIM // 2, OFF_QKV)

    def fill_pool():
        mix_ref[0, :, OFF_POOL:MIX_COLS] = proj(OFF_POOL, MIX_COLS)

    def fill_gate(i):
        q = GATE_COLS // 4
        gate_ref[0, :, i * q:(i + 1) * q] = proj(MIX_COLS + i * q, MIX_COLS + (i + 1) * q)

    fillers = [fill_z0, fill_z1, fill_pool] + [functools.partial(fill_gate, i) for i in range(4)]
    for g, buf in enumerate(ext):
        c0 = g * step
        buf[CONV_ROWS:CONV_ROWS + tm, :] = proj(OFF_QKV + c0, OFF_QKV + c0 + step)
        if fillers:
            fillers.pop(0)()
        for ci in range(c0 // LANES, (c0 + step) // LANES):
            mix_ref[0, :, OFF_QKV + ci * LANES:OFF_QKV + (ci + 1) * LANES] = _conv_silu_qkv(
                buf, None, tm, wconv_ref, ci, c0)
    for f in fillers:
        f()

    @pl.when(t == n_tiles - 1)
    def _():
        for g, buf in enumerate(ext):
            nconv_ref[0, :, g * step:(g + 1) * step] = buf[tm + base:tm + CONV_ROWS, :]

    for buf in ext:
        buf[0:CONV_ROWS, :] = buf[tm:tm + CONV_ROWS, :]


def _inproj_conv(x, conv0, norm_w, w_in_r, wconv, tm):
    n_seq, seq_len, d = x.shape
    n_tiles = seq_len // tm
    tile = lambda b, t: (b, t, 0)
    const = lambda b, t: (0, 0)
    mix, gates, nconv = pl.pallas_call(
        functools.partial(_inproj_conv_kernel, n_tiles=n_tiles),
        grid=(n_seq, n_tiles),
        in_specs=[
            pl.BlockSpec((1, tm, d), tile),
            pl.BlockSpec((1, d), const),
            pl.BlockSpec((d, IN_COLS), const),
            pl.BlockSpec((SUBLANES, CONV_DIM), const),
            pl.BlockSpec((1, CONV_WIDTH - 1, CONV_DIM), lambda b, t: (b, 0, 0)),
        ],
        out_specs=[
            pl.BlockSpec((1, tm, MIX_COLS), tile),
            pl.BlockSpec((1, tm, GATE_COLS), tile),
            pl.BlockSpec((1, CONV_WIDTH - 1, CONV_DIM), lambda b, t: (b, 0, 0)),
        ],
        out_shape=[
            jax.ShapeDtypeStruct((n_seq, seq_len, MIX_COLS), F32),
            jax.ShapeDtypeStruct((n_seq, seq_len, GATE_COLS), F32),
            jax.ShapeDtypeStruct((n_seq, CONV_WIDTH - 1, CONV_DIM), F32),
        ],
        scratch_shapes=[pltpu.VMEM((tm, d), BF16)] + [
            pltpu.VMEM((CONV_ROWS + tm, CONV_DIM // CONV_GROUPS), F32) for _ in range(CONV_GROUPS)],
        compiler_params=pltpu.CompilerParams(dimension_semantics=("arbitrary", "arbitrary"),
                                             vmem_limit_bytes=VMEM_LIMIT),
        name="inproj_conv",
    )(x, norm_w, w_in_r, wconv, conv0)
    n = n_seq * seq_len
    return mix.reshape(n, MIX_COLS), gates.reshape(n, GATE_COLS), nconv


def _mixer_kernel(proj_ref, pool0_ref, conv0_ref, s0_ref, wpool_ref, pscale_ref, wconv_ref, alog_ref, dtb_ref, onw_ref,
                  ypool_ref, ydelta_ref, npool_ref, nconv_ref, ns_ref,
                  ext_p, ext_c, q_s, k_s, v_s, g_s, b_s, state,
                  *, sb, tt, tv, chunk, pos0, n_tiles, qkv_done):
    t = pl.program_id(1)
    base = CONV_ROWS - (CONV_WIDTH - 1)

    @pl.when(t == 0)
    def _():
        ext_p[:, POOL_ROWS - POOL_BUF:POOL_ROWS, :] = pool0_ref[...]
        if not qkv_done:
            ext_c[:, base:CONV_ROWS, :] = conv0_ref[...]
        state[...] = s0_ref[...]

    row = lax.broadcasted_iota(jnp.int32, (tt, 1), 0)
    pos = pos0 + t * tt + row
    for s in range(sb):
        ext_p[s, POOL_ROWS:POOL_ROWS + tt, :] = proj_ref[s, :, OFF_POOL:OFF_POOL + POOL_DIM]
        if not qkv_done:
            ext_c[s, CONV_ROWS:CONV_ROWS + tt, :] = proj_ref[s, :, OFF_QKV:OFF_QKV + CONV_DIM]

        for gi, win in enumerate(POOL_WINDOWS):
            cs = slice(gi * POOL_GROUP_DIM, (gi + 1) * POOL_GROUP_DIM)
            u = ext_p[s, POOL_ROWS:POOL_ROWS + tt, cs]
            acc = u
            for j in range(1, win):
                acc = acc + ext_p[s, POOL_ROWS - j:POOL_ROWS - j + tt, cs]
            cnt = jnp.minimum(win, pos + 1).astype(F32)
            pooled = acc / cnt - u
            ypool_ref[s, :, cs] = _dot(pooled, wpool_ref[gi]) * pscale_ref[:, cs]

        for ci in range(0 if qkv_done else CONV_DIM // LANES):
            hs = slice((ci % N_HEADS) * HEAD_DIM, (ci % N_HEADS + 1) * HEAD_DIM)
            (q_s, k_s, v_s)[ci // N_HEADS][s, :, hs] = _conv_silu_qkv(ext_c, s, tt, wconv_ref, ci)

        beta = _sigmoid(proj_ref[s, :, OFF_B:OFF_B + LANES])
        g = -jnp.exp(alog_ref[...]) * _softplus(proj_ref[s, :, OFF_A:OFF_A + LANES] + dtb_ref[...])
        if tv < tt:
            beta = jnp.where(row < tv, beta, 0.0)
            g = jnp.where(row < tv, g, 0.0)
        b_s[s] = beta
        g_s[s] = g

    @pl.when(t == n_tiles - 1)
    def _():
        npool_ref[...] = ext_p[:, tv + POOL_ROWS - POOL_BUF:tv + POOL_ROWS, :]
        if qkv_done:
            nconv_ref[...] = jnp.zeros_like(nconv_ref)
        else:
            nconv_ref[...] = ext_c[:, tv + base:tv + CONV_ROWS, :]

    if n_tiles > 1:
        ext_p[:, 0:POOL_ROWS, :] = ext_p[:, tt:tt + POOL_ROWS, :]
        if not qkv_done:
            ext_c[:, 0:CONV_ROWS, :] = ext_c[:, tt:tt + CONV_ROWS, :]

    ri = lax.broadcasted_iota(jnp.int32, (chunk, chunk), 0)
    cj = lax.broadcasted_iota(jnp.int32, (chunk, chunk), 1)
    causal = ri >= cj
    strict = ri > cj
    ltri = causal.astype(F32)
    eye = (ri == cj).astype(F32)
    n_doublings = chunk.bit_length() - 2
    units = [(s, h) for s in range(sb) for h in range(N_HEADS)]
    hsl = [slice(h * HEAD_DIM, (h + 1) * HEAD_DIM) for h in range(N_HEADS)]

    def qkv(which, s, rows, h):
        if qkv_done:
            c0 = OFF_QKV + which * QK_DIM + h * HEAD_DIM
            return proj_ref[s, rows, c0:c0 + HEAD_DIM]
        return (q_s, k_s, v_s)[which][s, rows, hsl[h]]

    def chunk_body(ci, carry):
        r0 = pl.multiple_of(ci * chunk, chunk)
        rows = pl.ds(r0, chunk)
        g_all, g_t, b_all = [], [], []
        for s in range(sb):
            ga = jnp.dot(ltri, g_s[s, rows, :], precision=HIGHEST, preferred_element_type=F32)
            g_all.append(ga)
            if chunk < LANES:
                ga = jnp.concatenate([ga, jnp.zeros((LANES - chunk, LANES), F32)], axis=0)
            g_t.append(ga.T)
            b_all.append(b_s[s, rows, :])
        g_col = [g_all[s][:, h:h + 1] for s, h in units]
        dec = [jnp.exp(jnp.minimum(g_col[i] - g_t[s][h:h + 1, 0:chunk], 0.0)) for i, (s, h) in enumerate(units)]
        b_col = [b_all[s][:, h:h + 1] for s, h in units]
        q = [qkv(0, s, rows, h) for s, h in units]
        k = [qkv(1, s, rows, h) for s, h in units]
        idx = range(len(units))
        kb = [k[i] * b_col[i] for i in idx]
        kbk = [_dot_nt(kb[i], k[i]) for i in idx]
        qk = [_dot_nt(q[i], k[i]) for i in idx]
        p_mat = [jnp.where(strict, kbk[i] * dec[i], 0.0) for i in idx]
        t_inv = [eye - p_mat[i] for i in idx]
        for _ in range(n_doublings):
            p_mat = [_dot(p_mat[i], p_mat[i]) for i in idx]
            xp = [_dot(t_inv[i], p_mat[i]) for i in idx]
            t_inv = [t_inv[i] + xp[i] for i in idx]
        g_exp = [jnp.exp(g_col[i]) for i in idx]
        wu = [_dot(t_inv[i], jnp.concatenate([kb[i] * g_exp[i], qkv(2, s, rows, h) * b_col[i]], axis=1))
              for i, (s, h) in enumerate(units)]
        s_old = [state[s, h] for s, h in units]
        res = [_dot(jnp.concatenate([wu[i][:, 0:HEAD_DIM], q[i] * g_exp[i]], axis=0), s_old[i]) for i in idx]
        v_new = [wu[i][:, HEAD_DIM:2 * HEAD_DIM] - res[i][0:chunk] for i in idx]
        a_qk = [jnp.where(causal, qk[i] * dec[i], 0.0) for i in idx]
        g_last = [g_all[s][chunk - 1:chunk, h:h + 1] for s, h in units]
        k_dec = [k[i] * jnp.exp(g_last[i] - g_col[i]) for i in idx]
        intra = [_dot(a_qk[i], v_new[i]) for i in idx]
        upd = [_dot_tn(k_dec[i], v_new[i]) for i in idx]
        for i, (s, h) in enumerate(units):
            state[s, h] = s_old[i] * jnp.exp(g_last[i]) + upd[i]
            o_c = res[i][chunk:2 * chunk] + intra[i]
            z_c = proj_ref[s, rows, OFF_Z + h * HEAD_DIM:OFF_Z + (h + 1) * HEAD_DIM]
            ydelta_ref[s, rows, hsl[h]] = _rms(o_c, onw_ref[...]) * (z_c * _sigmoid(z_c))
        return carry

    lax.fori_loop(0, tt // chunk, chunk_body, 0)

    @pl.when(t == n_tiles - 1)
    def _():
        ns_ref[...] = state[...]


def _mixers(proj, pool0, conv0, s0, wpool, pscale, wconv, alog, dtb, onw, *, n_seq, sb, tt, tv, chunk, pos0, n_tiles,
            qkv_done):
    kern = functools.partial(_mixer_kernel, sb=sb, tt=tt, tv=tv, chunk=chunk, pos0=pos0, n_tiles=n_tiles,
                             qkv_done=qkv_done)
    seq_len = n_tiles * tt
    tile = lambda b, t: (b, t, 0)
    seq3 = lambda b, t: (b, 0, 0)
    seq4 = lambda b, t: (b, 0, 0, 0)
    const2 = lambda b, t: (0, 0)
    ypool, ydelta, npool, nconv, ns = pl.pallas_call(
        kern,
        grid=(n_seq // sb, n_tiles),
        in_specs=[
            pl.BlockSpec((sb, tt, MIX_COLS), tile),
            pl.BlockSpec((sb, POOL_BUF, POOL_DIM), seq3),
            pl.BlockSpec((sb, CONV_WIDTH - 1, CONV_DIM), seq3),
            pl.BlockSpec((sb, N_HEADS, HEAD_DIM, HEAD_DIM), seq4),
            pl.BlockSpec((len(POOL_WINDOWS), POOL_GROUP_DIM, POOL_GROUP_DIM), lambda b, t: (0, 0, 0)),
            pl.BlockSpec((1, POOL_DIM), const2),
            pl.BlockSpec((SUBLANES, CONV_DIM), const2),
            pl.BlockSpec((1, LANES), const2),
            pl.BlockSpec((1, LANES), const2),
            pl.BlockSpec((1, HEAD_DIM), const2),
        ],
        out_specs=[
            pl.BlockSpec((sb, tt, POOL_DIM), tile),
            pl.BlockSpec((sb, tt, QK_DIM), tile),
            pl.BlockSpec((sb, POOL_BUF, POOL_DIM), seq3),
            pl.BlockSpec((sb, CONV_WIDTH - 1, CONV_DIM), seq3),
            pl.BlockSpec((sb, N_HEADS, HEAD_DIM, HEAD_DIM), seq4),
        ],
        out_shape=[
            jax.ShapeDtypeStruct((n_seq, seq_len, POOL_DIM), F32),
            jax.ShapeDtypeStruct((n_seq, seq_len, QK_DIM), F32),
            jax.ShapeDtypeStruct((n_seq, POOL_BUF, POOL_DIM), F32),
            jax.ShapeDtypeStruct((n_seq, CONV_WIDTH - 1, CONV_DIM), F32),
            jax.ShapeDtypeStruct((n_seq, N_HEADS, HEAD_DIM, HEAD_DIM), F32),
        ],
        scratch_shapes=[
            pltpu.VMEM((sb, POOL_ROWS + tt, POOL_DIM), F32),
            pltpu.VMEM((sb, CONV_ROWS + tt, CONV_DIM), F32),
            pltpu.VMEM((sb, tt, QK_DIM), F32),
            pltpu.VMEM((sb, tt, QK_DIM), F32),
            pltpu.VMEM((sb, tt, QK_DIM), F32),
            pltpu.VMEM((sb, tt, LANES), F32),
            pltpu.VMEM((sb, tt, LANES), F32),
            pltpu.VMEM((sb, N_HEADS, HEAD_DIM, HEAD_DIM), F32),
        ],
        compiler_params=pltpu.CompilerParams(dimension_semantics=("arbitrary", "arbitrary"),
                                             vmem_limit_bytes=VMEM_LIMIT),
        name="mixers",
    )(proj.reshape(n_seq, seq_len, MIX_COLS), pool0, conv0, s0, wpool, pscale, wconv, alog, dtb, onw)
    n = n_seq * seq_len
    return ypool.reshape(n, POOL_DIM), ydelta.reshape(n, QK_DIM), npool, nconv, ns


def _merge_kernel(x_ref, gates_ref, ypool_ref, ydelta_ref, wbp_ref, wbd_ref, wout_ref, n2w_ref, wr_hi_ref, wr_lo_ref,
                  br_ref, xm_ref, h2_ref, route_ref, wk_ref, cnt_ref, cnt_s, *, period, valid):
    tm = x_ref.shape[0]

    @pl.when(pl.program_id(0) == 0)
    def _():
        cnt_s[...] = jnp.zeros_like(cnt_s)

    bp = _dot(ypool_ref[...], wbp_ref[...])
    bd = _dot(ydelta_ref[...], wbd_ref[...])
    merged = _sigmoid(gates_ref[:, 0:1024]) * bp + _sigmoid(gates_ref[:, 1024:2048]) * bd
    xm = x_ref[...] + _dot(merged, wout_ref[...])
    xm_ref[...] = xm
    h2 = _rms(xm, n2w_ref[...])
    h2_ref[...] = _pack_pairs(h2)

    h2_hi = h2.astype(BF16)
    h2_lo = (h2 - h2_hi.astype(F32)).astype(BF16)
    logits = (jnp.dot(h2_hi, wr_hi_ref[...], preferred_element_type=F32)
              + (jnp.dot(h2_hi, wr_lo_ref[...], preferred_element_type=F32)
                 + jnp.dot(h2_lo, wr_hi_ref[...], preferred_element_type=F32))) + br_ref[...]
    lane = lax.broadcasted_iota(jnp.int32, logits.shape, 1)
    neg = -jnp.inf
    far = LANES - 1
    is_g = lane < N_GROUPS
    g_max = jnp.max(jnp.where(is_g, logits, neg), axis=-1, keepdims=True)
    g_sel = jnp.min(jnp.where(is_g & (logits == g_max), lane, far), axis=-1, keepdims=True)
    p_g = 1.0 / jnp.sum(jnp.where(is_g, jnp.exp(logits - g_max), 0.0), axis=-1, keepdims=True)
    e_lane = lane - N_GROUPS
    is_e = (e_lane >= 0) & (e_lane < N_EXPERTS) & ((e_lane // EXPERTS_PER_GROUP) == g_sel)
    ev = jnp.where(is_e, logits, neg)
    v1 = jnp.max(ev, axis=-1, keepdims=True)
    i1 = jnp.min(jnp.where(is_e & (ev == v1), lane, far), axis=-1, keepdims=True)
    is_e2 = is_e & (lane != i1)
    ev2 = jnp.where(is_e2, logits, neg)
    v2 = jnp.max(ev2, axis=-1, keepdims=True)
    i2 = jnp.min(jnp.where(is_e2 & (ev2 == v2), lane, far), axis=-1, keepdims=True)
    e21 = jnp.exp(v2 - v1)
    w1 = p_g / (1.0 + e21)
    w2 = p_g * e21 / (1.0 + e21)
    e1 = i1 - N_GROUPS
    e2 = i2 - N_GROUPS
    chosen = (lane == e1) | (lane == e2)
    if valid < period:
        row = lax.broadcasted_iota(jnp.int32, (tm, 1), 0)
        chosen = chosen & (lax.rem(row, period) < valid)
    onehot = jnp.where(chosen, 1.0, 0.0)
    rr = lax.broadcasted_iota(jnp.int32, (tm, tm), 0)
    cc = lax.broadcasted_iota(jnp.int32, (tm, tm), 1)
    earlier = jnp.where(rr > cc, 1.0, 0.0).astype(BF16)
    before = jnp.dot(earlier, onehot.astype(BF16), preferred_element_type=F32) + cnt_s[...]
    r1 = jnp.sum(jnp.where(lane == e1, before, 0.0), axis=-1, keepdims=True)
    r2 = jnp.sum(jnp.where(lane == e2, before, 0.0), axis=-1, keepdims=True)
    cnt_s[...] = cnt_s[...] + jnp.sum(onehot, axis=0, keepdims=True)
    cnt_ref[...] = cnt_s[...]
    record = jnp.where(lane == 0, e1.astype(F32), jnp.where(lane == 1, e2.astype(F32),
                                                           jnp.where(lane == 2, r1, jnp.where(lane == 3, r2, 0.0))))
    route_ref[...] = record.T[0:SUBLANES, :]
    lane8 = lax.broadcasted_iota(jnp.int32, (tm, SUBLANES), 1)
    wk_ref[...] = jnp.where(lane8 == 0, w1, jnp.where(lane8 == 1, w2, 0.0))


def _merge(x, proj, ypool, ydelta, wbp, wbd, wout, n2w, wr_hi, wr_lo, br, tm, period, valid):
    n, d = x.shape
    row = lambda i: (i, 0)
    const = lambda i: (0, 0)
    return pl.pallas_call(
        functools.partial(_merge_kernel, period=period, valid=valid),
        grid=(n // tm,),
        in_specs=[
            pl.BlockSpec((tm, d), row),
            pl.BlockSpec((tm, GATE_COLS), row),
            pl.BlockSpec((tm, POOL_DIM), row),
            pl.BlockSpec((tm, QK_DIM), row),
            pl.BlockSpec((POOL_DIM, d), const),
            pl.BlockSpec((QK_DIM, d), const),
            pl.BlockSpec((d, d), const),
            pl.BlockSpec((1, d), const),
            pl.BlockSpec((d, LANES), const),
            pl.BlockSpec((d, LANES), const),
            pl.BlockSpec((1, LANES), const),
        ],
        out_specs=[
            pl.BlockSpec((tm, d), row),
            pl.BlockSpec((tm, d // 2), row),
            pl.BlockSpec((SUBLANES, tm), lambda i: (0, i)),
            pl.BlockSpec((tm, SUBLANES), row),
            pl.BlockSpec((1, LANES), const),
        ],
        out_shape=[
            jax.ShapeDtypeStruct((n, d), F32),
            jax.ShapeDtypeStruct((n, d // 2), U32),
            jax.ShapeDtypeStruct((SUBLANES, n), F32),
            jax.ShapeDtypeStruct((n, SUBLANES), F32),
            jax.ShapeDtypeStruct((1, LANES), F32),
        ],
        scratch_shapes=[pltpu.VMEM((1, LANES), F32)],
        compiler_params=pltpu.CompilerParams(dimension_semantics=("arbitrary",), vmem_limit_bytes=VMEM_LIMIT),
        name="merge",
    )(x, proj, ypool, ydelta, wbp, wbd, wout, n2w, wr_hi, wr_lo, br)


def _expert_kernel(te_ref, nv_ref, x_ref, wg_hbm, wu_hbm, wd_hbm, y_ref, st_g, st_u, st_d, wg_s, wu_s, wd_s, sem, slot_s):
    i = pl.program_id(0)
    n_tiles = pl.num_programs(0)
    n_valid = nv_ref[i]
    expert = te_ref[i]

    def expert_at(j):
        return te_ref[jnp.minimum(j, n_tiles - 1)]

    def next_expert_tile(j):
        e = expert_at(j)
        return lax.while_loop(lambda t: (t < n_tiles) & (expert_at(t) == e), lambda t: t + 1, j + 1)

    def start_weight_copies(j, slot):
        @pl.when(j < n_tiles)
        def _():
            for cp in weight_copies(expert_at(j), slot):
                cp.start()

    def weight_copies(e, slot):
        return [pltpu.make_async_copy(src.at[e], dst.at[slot], sem.at[slot])
                for src, dst in ((wg_hbm, st_g), (wu_hbm, st_u), (wd_hbm, st_d))]

    @pl.when(i == 0)
    def _():
        slot_s[0] = 0
        j = i
        for slot in range(EXPERT_STAGES - 1):
            start_weight_copies(j, slot)
            j = next_expert_tile(j)

    @pl.when((i == 0) | (expert != te_ref[jnp.maximum(i - 1, 0)]))
    def _():
        slot = slot_s[0]
        j = i
        for _ in range(EXPERT_STAGES - 1):
            j = next_expert_tile(j)
        start_weight_copies(j, (slot + EXPERT_STAGES - 1) % EXPERT_STAGES)
        for cp in weight_copies(expert, slot):
            cp.wait()
        wg_s[...] = st_g[slot].astype(BF16)
        wu_s[...] = st_u[slot].astype(BF16)
        wd_s[...] = st_d[slot].astype(BF16)
        slot_s[0] = (slot + 1) % EXPERT_STAGES

    @pl.when(n_valid > 0)
    def _():
        tm = x_ref.shape[0]
        part = min(tm, EXPERT_PART_ROWS)
        parts = [slice(r0, r0 + part) for r0 in range(0, tm, part)]
        row = lax.broadcasted_iota(jnp.int32, (part, 1), 0)
        xs = []
        for p in parts:
            left, right = _unpack_pairs(jnp.where(row + p.start < n_valid, x_ref[p, :], jnp.uint32(0)))
            xs.append(jnp.concatenate([left.astype(BF16), right.astype(BF16)], axis=1))
        a = [_dot(x, wg_s[...]) for x in xs]
        b = [_dot(x, wu_s[...]) for x in xs]
        act = [(a[j] * _sigmoid(a[j])) * b[j] for j in range(len(parts))]
        y = [_dot(act[j], wd_s[...]) for j in range(len(parts))]
        for j, p in enumerate(parts):
            y_ref[p, :] = _pack_pairs(y[j])

    @pl.when(n_valid == 0)
    def _():
        y_ref[...] = jnp.zeros_like(y_ref)


def _experts(tile_expert, tile_valid, xs, wg, wu, wd, tm):
    n_rows, dp = xs.shape
    d, f = wg.shape[1], wg.shape[2]
    grid_spec = pltpu.PrefetchScalarGridSpec(
        num_scalar_prefetch=2,
        grid=(n_rows // tm,),
        in_specs=[
            pl.BlockSpec((tm, dp), lambda i, te, nv: (jnp.where(nv[i] > 0, i, 0), 0)),
            pl.BlockSpec(memory_space=pl.ANY),
            pl.BlockSpec(memory_space=pl.ANY),
            pl.BlockSpec(memory_space=pl.ANY),
        ],
        out_specs=pl.BlockSpec((tm, dp), lambda i, te, nv: (i, 0)),
        scratch_shapes=[pltpu.VMEM((EXPERT_STAGES, d, f), F32), pltpu.VMEM((EXPERT_STAGES, d, f), F32),
                        pltpu.VMEM((EXPERT_STAGES, f, d), F32),
                        pltpu.VMEM((d, f), BF16), pltpu.VMEM((d, f), BF16), pltpu.VMEM((f, d), BF16),
                        pltpu.SemaphoreType.DMA((EXPERT_STAGES,)), pltpu.SMEM((1,), jnp.int32)],
    )
    return pl.pallas_call(
        _expert_kernel,
        grid_spec=grid_spec,
        out_shape=jax.ShapeDtypeStruct((n_rows, dp), U32),
        compiler_params=pltpu.CompilerParams(dimension_semantics=("arbitrary",), vmem_limit_bytes=VMEM_LIMIT),
        name="experts",
    )(tile_expert, tile_valid, xs, wg, wu, wd)


def _plan_kernel(route_ref, cnt_ref, pos_ref, tiles_ref, *, tm_e):
    counts = cnt_ref[...]
    padded = jnp.floor((counts + (tm_e - 1)) / tm_e) * tm_e
    lane = lax.broadcasted_iota(jnp.int32, counts.shape, 1)
    ends = padded
    shift = 1
    while shift < N_EXPERTS:
        ends = ends + jnp.where(lane >= shift, pltpu.roll(ends, shift, axis=1), 0.0)
        shift *= 2
    starts = ends - padded

    def lookup(table, key):
        out = jnp.zeros_like(key)
        for e in range(N_EXPERTS):
            out = out + jnp.where(key == e, table[:, e:e + 1], 0.0)
        return out

    for k in range(2):
        pos = lookup(starts, route_ref[k:k + 1, :]) + route_ref[2 + k:3 + k, :]
        pos_ref[k:k + 1, :] = pos.astype(jnp.int32)

    tile_start = lax.broadcasted_iota(jnp.int32, (1, tiles_ref.shape[1]), 1).astype(F32) * tm_e
    tile_expert = jnp.zeros_like(tile_start)
    for e in range(N_EXPERTS):
        tile_expert = tile_expert + jnp.where(ends[:, e:e + 1] <= tile_start, 1.0, 0.0)
    tile_expert = jnp.minimum(tile_expert, N_EXPERTS - 1.0)
    valid = lookup(counts, tile_expert) - (tile_start - lookup(starts, tile_expert))
    tiles_ref[0:1, :] = tile_expert.astype(jnp.int32)
    tiles_ref[1:2, :] = jnp.clip(valid, 0.0, tm_e).astype(jnp.int32)


def _plan(route_t, counts, tm_e, n_tiles):
    n = route_t.shape[1]
    tiles_pad = -(-n_tiles // LANES) * LANES
    return pl.pallas_call(
        functools.partial(_plan_kernel, tm_e=tm_e),
        out_shape=[jax.ShapeDtypeStruct((2, n), jnp.int32), jax.ShapeDtypeStruct((2, tiles_pad), jnp.int32)],
        name="plan",
    )(route_t, counts)


def _sc_workers():
    info = plsc.get_sparse_core_info()
    return info.num_cores, info.num_subcores


def _sc_chunk(per_worker):
    assert per_worker % SUBLANES == 0
    return max(c for c in range(SUBLANES, SC_ROWS + 1, SUBLANES) if per_worker % c == 0)


def _sc_scatter2(rows, idx_a, idx_b, n_out):
    n_cores, n_sub = _sc_workers()
    n, d = rows.shape
    per_worker = n // (n_cores * n_sub)
    assert per_worker * n_cores * n_sub == n
    chunk = _sc_chunk(per_worker)
    mesh = plsc.VectorSubcoreMesh(core_axis_name="c", subcore_axis_name="s")

    n_chunks = per_worker // chunk
    buf = lambda shape, dtype: [pltpu.VMEM(shape, dtype), pltpu.VMEM(shape, dtype)]

    @functools.partial(
        pl.kernel, mesh=mesh, out_type=jax.ShapeDtypeStruct((n_out, d), rows.dtype),
        scratch_types=buf((chunk,), jnp.int32) + buf((chunk,), jnp.int32) + buf((chunk, d), rows.dtype)
        + [pltpu.SemaphoreType.DMA, pltpu.SemaphoreType.DMA])
    def scatter_rows(rows_hbm, ia_hbm, ib_hbm, out_hbm, ia0, ia1, ib0, ib1, r0, r1, sem0, sem1):
        worker = lax.axis_index("s") * n_cores + lax.axis_index("c")
        base = worker * per_worker
        slots = ((ia0, ib0, r0, sem0), (ia1, ib1, r1, sem1))
        pending = {}
        for j in range(n_chunks):
            ia_v, ib_v, rows_v, sem = slots[j % 2]
            if j >= 2:
                for cp in pending.pop(j - 2):
                    cp.wait()
            off = pl.multiple_of(base + j * chunk, SUBLANES)
            pltpu.sync_copy(ia_hbm.at[pl.ds(off, chunk)], ia_v)
            pltpu.sync_copy(ib_hbm.at[pl.ds(off, chunk)], ib_v)
            pltpu.sync_copy(rows_hbm.at[pl.ds(off, chunk)], rows_v)
            pending[j] = (pltpu.async_copy(rows_v, out_hbm.at[ia_v], sem),
                          pltpu.async_copy(rows_v, out_hbm.at[ib_v], sem))
        for j in sorted(pending):
            for cp in pending[j]:
                cp.wait()

    return scatter_rows(rows, idx_a, idx_b)


def _sc_gather(table, idx):
    n_cores, n_sub = _sc_workers()
    n_idx = idx.shape[0]
    d = table.shape[1]
    per_worker = n_idx // (n_cores * n_sub)
    assert per_worker * n_cores * n_sub == n_idx
    chunk = _sc_chunk(per_worker)
    mesh = plsc.VectorSubcoreMesh(core_axis_name="c", subcore_axis_name="s")

    n_chunks = per_worker // chunk

    @functools.partial(
        pl.kernel, mesh=mesh, out_type=jax.ShapeDtypeStruct((n_idx, d), table.dtype),
        scratch_types=[pltpu.VMEM((chunk,), jnp.int32), pltpu.VMEM((chunk,), jnp.int32),
                       pltpu.VMEM((chunk, d), table.dtype), pltpu.VMEM((chunk, d), table.dtype),
                       pltpu.SemaphoreType.DMA, pltpu.SemaphoreType.DMA])
    def gather_rows(table_hbm, idx_hbm, out_hbm, i0, i1, r0, r1, sem0, sem1):
        worker = lax.axis_index("s") * n_cores + lax.axis_index("c")
        base = worker * per_worker
        slots = ((i0, r0, sem0), (i1, r1, sem1))

        def start(j):
            idx_v, rows_v, sem = slots[j % 2]
            off = pl.multiple_of(base + j * chunk, SUBLANES)
            pltpu.sync_copy(idx_hbm.at[pl.ds(off, chunk)], idx_v)
            return pltpu.async_copy(table_hbm.at[idx_v], rows_v, sem)

        nxt = start(0)
        for j in range(n_chunks):
            cur = nxt
            if j + 1 < n_chunks:
                nxt = start(j + 1)
            cur.wait()
            off = pl.multiple_of(base + j * chunk, SUBLANES)
            pltpu.sync_copy(slots[j % 2][1], out_hbm.at[pl.ds(off, chunk)])

    return gather_rows(table, idx)


def _finalize_kernel(xm_ref, ya_ref, yb_ref, wk_ref, fnw_ref, o_ref):
    d = xm_ref.shape[1]
    c = d // 2
    w = wk_ref[...]
    a_left, a_right = _unpack_pairs(ya_ref[...])
    b_left, b_right = _unpack_pairs(yb_ref[...])
    x_left = xm_ref[:, 0:c] + (w[:, 0:1] * a_left + w[:, 1:2] * b_left)
    x_right = xm_ref[:, c:d] + (w[:, 0:1] * a_right + w[:, 1:2] * b_right)
    ms = (jnp.sum(x_left * x_left, axis=-1, keepdims=True) + jnp.sum(x_right * x_right, axis=-1, keepdims=True)) / d
    scale = lax.rsqrt(ms + EPS)
    o_ref[:, 0:c] = x_left * scale * fnw_ref[:, 0:c]
    o_ref[:, c:d] = x_right * scale * fnw_ref[:, c:d]


def _finalize(xm, yk, wk, fnw, tm):
    n, d = xm.shape
    steps = n // tm
    return pl.pallas_call(
        _finalize_kernel,
        grid=(steps,),
        in_specs=[
            pl.BlockSpec((tm, d), lambda i: (i, 0)),
            pl.BlockSpec((tm, d // 2), lambda i: (i, 0)),
            pl.BlockSpec((tm, d // 2), lambda i: (i + steps, 0)),
            pl.BlockSpec((tm, SUBLANES), lambda i: (i, 0)),
            pl.BlockSpec((1, d), lambda i: (0, 0)),
        ],
        out_specs=pl.BlockSpec((tm, d), lambda i: (i, 0)),
        out_shape=jax.ShapeDtypeStruct((n, d), F32),
        compiler_params=pltpu.CompilerParams(dimension_semantics=("arbitrary",), vmem_limit_bytes=VMEM_LIMIT),
        name="finalize",
    )(xm, yk, yk, wk, fnw)


def _moe(h2, route, counts, wg, wu, wd, *, tm_e):
    n = h2.shape[0]
    n_rows = 2 * n + N_EXPERTS * tm_e
    n_tiles = n_rows // tm_e
    pos, tiles = _plan(route, counts, tm_e, n_tiles)
    xs = _sc_scatter2(h2, pos[0], pos[1], n_rows)
    y_sorted = _experts(tiles[0, :n_tiles], tiles[1, :n_tiles], xs, wg, wu, wd, tm_e)
    return _sc_gather(y_sorted, pos.reshape(-1))


def _layer(x_tokens, pool0, conv0, s0, prm, *, n_seq, sb, tt, tv, chunk, pos0, n_tiles, tm):
    conv_in_proj = tv == tt
    if conv_in_proj:
        proj, gates, nconv_proj = _inproj_conv(x_tokens.reshape(n_seq, n_tiles * tt, -1), conv0, prm["norm1_w"],
                                               prm["w_in"], prm["w_conv"], tm)
    else:
        proj, gates = _inproj(x_tokens, prm["norm1_w"], prm["w_in"], tm)
    ypool, ydelta, npool, nconv, ns = _mixers(
        proj, pool0, conv0, s0, prm["w_pool"], prm["pool_scale"], prm["w_conv"], prm["a_log"], prm["dt_bias"],
        prm["o_norm_w"], n_seq=n_seq, sb=sb, tt=tt, tv=tv, chunk=chunk, pos0=pos0, n_tiles=n_tiles,
        qkv_done=conv_in_proj)
    if conv_in_proj:
        nconv = nconv_proj
    xm, h2, route, wk, counts = _merge(
        x_tokens, gates, ypool, ydelta, prm["w_branch_pool"], prm["w_branch_delta"], prm["w_out"], prm["norm2_w"],
        prm["w_router_hi"], prm["w_router_lo"], prm["b_router"], 2 * tm, tt, tv)
    return xm, h2, route, wk, counts, npool, nconv, ns


def _pad_lanes(v, width=LANES):
    v = v.reshape(1, -1).astype(F32)
    return jnp.pad(v, ((0, 0), (0, width - v.shape[1])))


def kernel(x_prompt, x_sample, cache_pool, cache_conv, state_delta, norm1_w, w_in, w_pool, pool_scale, w_conv, a_log, dt_bias, o_norm_w, w_branch_pool, w_branch_delta, w_out, norm2_w, w_router_group, b_router_group, w_router_expert, b_router_expert, w_gate, w_up, w_down, final_norm_w):
    n_layers = norm1_w.shape[0]
    assert n_layers == 1, "single-layer step"
    bsz, seq, d = x_prompt.shape
    dbs, dseq, _ = x_sample.shape
    lyr = 0

    wi = w_in[lyr]
    c = [POOL_DIM, POOL_DIM + CONV_DIM]
    c += [c[1] + N_HEADS, c[1] + 2 * N_HEADS]
    c += [c[3] + QK_DIM, c[3] + QK_DIM + d]
    pad8 = lambda m: jnp.pad(m, ((0, 0), (0, LANES - N_HEADS)))
    w_in_r = jnp.concatenate(
        [wi[:, c[3]:c[4]], wi[:, c[0]:c[1]], wi[:, :c[0]], pad8(wi[:, c[1]:c[2]]), pad8(wi[:, c[2]:c[3]]),
         wi[:, c[4]:c[5]], wi[:, c[5]:]], axis=1).astype(BF16)
    w_router = jnp.concatenate(
        [w_router_group[lyr], w_router_expert[lyr].reshape(d, N_EXPERTS),
         jnp.zeros((d, LANES - N_GROUPS - N_EXPERTS), F32)], axis=1).astype(F32)
    w_router_hi = w_router.astype(BF16)
    b_router = _pad_lanes(jnp.concatenate([b_router_group[lyr], b_router_expert[lyr].reshape(-1)]))
    prm = dict(
        norm1_w=norm1_w[lyr].reshape(1, d), w_in=w_in_r,
        w_pool=w_pool[lyr].astype(BF16), pool_scale=pool_scale[lyr].reshape(1, POOL_DIM),
        w_conv=jnp.pad(w_conv[lyr], ((0, SUBLANES - CONV_WIDTH), (0, 0))),
        a_log=_pad_lanes(a_log[lyr]), dt_bias=_pad_lanes(dt_bias[lyr]), o_norm_w=o_norm_w[lyr].reshape(1, HEAD_DIM),
        w_branch_pool=w_branch_pool[lyr].astype(BF16), w_branch_delta=w_branch_delta[lyr].astype(BF16),
        w_out=w_out[lyr].astype(BF16), norm2_w=norm2_w[lyr].reshape(1, d),
        w_router_hi=w_router_hi, w_router_lo=(w_router - w_router_hi.astype(F32)).astype(BF16), b_router=b_router,
    )
    wg, wu, wd = w_gate[lyr], w_up[lyr], w_down[lyr]
    fnw = final_norm_w.reshape(1, d)

    tt_p = TOKEN_TILE // 2
    xm, h2, route, wk, counts, npool, nconv, ns = _layer(
        x_prompt.reshape(bsz * seq, d),
        jnp.zeros((bsz, POOL_BUF, POOL_DIM), F32), jnp.zeros((bsz, CONV_WIDTH - 1, CONV_DIM), F32),
        jnp.zeros((bsz, N_HEADS, HEAD_DIM, HEAD_DIM), F32), prm,
        n_seq=bsz, sb=4, tt=tt_p, tv=tt_p, chunk=DELTA_CHUNK, pos0=0, n_tiles=seq // tt_p, tm=TOKEN_TILE)
    yk = _moe(h2, route, counts, wg, wu, wd, tm_e=2 * TOKEN_TILE)
    y_prompt = _finalize(xm, yk, wk, fnw, 2 * TOKEN_TILE).reshape(bsz, seq, d)
    pool_p = npool[None]
    conv_p = nconv[None]
    delta_p = ns[None]

    tt_s = SUBLANES
    xs_pad = jnp.pad(x_sample, ((0, 0), (0, tt_s - dseq), (0, 0))).reshape(dbs * tt_s, d)
    xm, h2, route, wk, counts, npool, nconv, ns = _layer(
        xs_pad, cache_pool[lyr], cache_conv[lyr], state_delta[lyr], prm,
        n_seq=dbs, sb=8, tt=tt_s, tv=dseq, chunk=tt_s, pos0=PAST_LEN, n_tiles=1, tm=TOKEN_TILE)
    real = lambda a: a.reshape(dbs, tt_s, -1)[:, :dseq].reshape(dbs * dseq, -1)
    route = route.reshape(SUBLANES, dbs, tt_s)[:, :, :dseq].reshape(SUBLANES, dbs * dseq)
    yk = _moe(real(h2), route, counts, wg, wu, wd, tm_e=TOKEN_TILE // 2)
    y_sample = _finalize(real(xm), yk, real(wk), fnw, TOKEN_TILE).reshape(dbs, dseq, d)
    pool_s = npool[None]
    conv_s = nconv[None]
    delta_s = ns[None]
    return (y_prompt, y_sample, pool_p, conv_p, delta_p, pool_s, conv_s, delta_s)
```

```python
import functools

import jax
import jax.numpy as jnp
from jax import lax
from jax.experimental import pallas as pl
from jax.experimental.pallas import tpu as pltpu
from jax.experimental.pallas import tpu_sc as plsc

F32 = jnp.float32
BF16 = jnp.bfloat16
U32 = jnp.uint32
EPS = 1e-6
HIGHEST = lax.Precision.HIGHEST

POOL_WINDOWS = (2, 4, 8, 16)
POOL_GROUP_DIM = 128
POOL_DIM = len(POOL_WINDOWS) * POOL_GROUP_DIM
POOL_BUF = max(POOL_WINDOWS) - 1
N_HEADS = 8
HEAD_DIM = 128
QK_DIM = N_HEADS * HEAD_DIM
CONV_WIDTH = 4
CONV_DIM = 3 * QK_DIM
N_GROUPS = 4
EXPERTS_PER_GROUP = 8
N_EXPERTS = N_GROUPS * EXPERTS_PER_GROUP
PAST_LEN = 16384
DELTA_CHUNK = 64
TOKEN_TILE = 256
LANES = 128
SUBLANES = 8

OFF_Z = 0
OFF_QKV = OFF_Z + QK_DIM
OFF_POOL = OFF_QKV + CONV_DIM
OFF_B = OFF_POOL + POOL_DIM
OFF_A = OFF_B + LANES
MIX_COLS = OFF_A + LANES
GATE_COLS = 2048
IN_COLS = MIX_COLS + GATE_COLS

POOL_ROWS = 16
CONV_ROWS = 8
VMEM_LIMIT = 60 * 1024 * 1024
EXPERT_PART_ROWS = 128
EXPERT_STAGES = 4
CONV_GROUPS = 12
SC_ROWS = 96


def _sigmoid(x):
    return 1.0 / (1.0 + jnp.exp(-x))


def _softplus(x):
    return jnp.maximum(x, 0.0) + jnp.log1p(jnp.exp(-jnp.abs(x)))


def _rms(x, w):
    return x * lax.rsqrt(jnp.mean(x * x, axis=-1, keepdims=True) + EPS) * w


def _dot(a, b):
    return jnp.dot(a.astype(BF16), b.astype(BF16), preferred_element_type=F32)


def _dot_nt(a, b):
    return lax.dot_general(a.astype(BF16), b.astype(BF16), (((1,), (1,)), ((), ())), preferred_element_type=F32)


def _dot_tn(a, b):
    return lax.dot_general(a.astype(BF16), b.astype(BF16), (((0,), (0,)), ((), ())), preferred_element_type=F32)


def _pack_pairs(x):
    c = x.shape[1] // 2
    hi = pltpu.bitcast(x[:, :c].astype(BF16).astype(F32), U32)
    lo = pltpu.bitcast(x[:, c:].astype(BF16).astype(F32), U32)
    return hi | lax.shift_right_logical(lo, jnp.uint32(16))


def _unpack_pairs(w):
    left = pltpu.bitcast(w & jnp.uint32(0xFFFF0000), F32)
    right = pltpu.bitcast(lax.shift_left(w, jnp.uint32(16)), F32)
    return left, right


def _inproj_kernel(x_ref, nw_ref, w_ref, mix_ref, gate_ref):
    h = _rms(x_ref[...], nw_ref[...]).astype(BF16)
    half = MIX_COLS // 2
    for c0 in (0, half):
        mix_ref[:, c0:c0 + half] = jnp.dot(h, w_ref[:, c0:c0 + half], preferred_element_type=F32)
    half = GATE_COLS // 2
    for c0 in (0, half):
        gate_ref[:, c0:c0 + half] = jnp.dot(h, w_ref[:, MIX_COLS + c0:MIX_COLS + c0 + half],
                                            preferred_element_type=F32)


def _inproj(x, norm_w, w_in_r, tm):
    n, d = x.shape
    return pl.pallas_call(
        _inproj_kernel,
        grid=(n // tm,),
        in_specs=[
            pl.BlockSpec((tm, d), lambda i: (i, 0)),
            pl.BlockSpec((1, d), lambda i: (0, 0)),
            pl.BlockSpec((d, IN_COLS), lambda i: (0, 0)),
        ],
        out_specs=[pl.BlockSpec((tm, MIX_COLS), lambda i: (i, 0)), pl.BlockSpec((tm, GATE_COLS), lambda i: (i, 0))],
        out_shape=[jax.ShapeDtypeStruct((n, MIX_COLS), F32), jax.ShapeDtypeStruct((n, GATE_COLS), F32)],
        compiler_params=pltpu.CompilerParams(dimension_semantics=("arbitrary",), vmem_limit_bytes=VMEM_LIMIT),
        name="inproj",
    )(x, norm_w, w_in_r)


def _conv_silu_qkv(ext_c, seq, tt, wconv_ref, ci, col0=0):
    base = CONV_ROWS - (CONV_WIDTH - 1)
    cs = slice(ci * LANES, (ci + 1) * LANES)
    cl = slice(ci * LANES - col0, (ci + 1) * LANES - col0)
    idx = (lambda r: (r, cl)) if seq is None else (lambda r: (seq, r, cl))
    acc = ext_c[idx(slice(base, base + tt))] * wconv_ref[0:1, cs]
    for j in range(1, CONV_WIDTH):
        acc = acc + ext_c[idx(slice(base + j, base + j + tt))] * wconv_ref[j:j + 1, cs]
    y = acc * _sigmoid(acc)
    if ci < 2 * N_HEADS:
        y = y * lax.rsqrt(jnp.sum(y * y, axis=-1, keepdims=True) + EPS)
    if ci < N_HEADS:
        y = y * (HEAD_DIM ** -0.5)
    return y


def _inproj_conv_kernel(x_ref, nw_ref, w_ref, wconv_ref, conv0_ref, mix_ref, gate_ref, nconv_ref, h_s, *ext, n_tiles):
    t = pl.program_id(1)
    tm = x_ref.shape[1]
    base = CONV_ROWS - (CONV_WIDTH - 1)
    step = CONV_DIM // len(ext)

    @pl.when(t == 0)
    def _():
        for g, buf in enumerate(ext):
            buf[base:CONV_ROWS, :] = conv0_ref[0, :, g * step:(g + 1) * step]

    h_s[...] = _rms(x_ref[0], nw_ref[...]).astype(BF16)

    def proj(c0, c1):
        return jnp.dot(h_s[...], w_ref[:, c0:c1], preferred_element_type=F32)

    def fill_z0():
        mix_ref[0, :, OFF_Z:OFF_Z + QK_DIM // 2] = proj(OFF_Z, OFF_Z + QK_DIM // 2)

    def fill_z1():
        mix_ref[0, :, OFF_Z + QK_DIM // 2:OFF_QKV] = proj(OFF_Z + QK_DIM // 2, OFF_QKV)

    def fill_pool():
        mix_ref[0, :, OFF_POOL:MIX_COLS] = proj(OFF_POOL, MIX_COLS)

    def fill_gate(i):
        q = GATE_COLS // 4
        gate_ref[0, :, i * q:(i + 1) * q] = proj(MIX_COLS + i * q, MIX_COLS + (i + 1) * q)

    fillers = [fill_z0, fill_z1, fill_pool] + [functools.partial(fill_gate, i) for i in range(4)]
    for g, buf in enumerate(ext):
        c0 = g * step
        buf[CONV_ROWS:CONV_ROWS + tm, :] = proj(OFF_QKV + c0, OFF_QKV + c0 + step)
        if fillers:
            fillers.pop(0)()
        for ci in range(c0 // LANES, (c0 + step) // LANES):
            mix_ref[0, :, OFF_QKV + ci * LANES:OFF_QKV + (ci + 1) * LANES] = _conv_silu_qkv(
                buf, None, tm, wconv_ref, ci, c0)
    for f in fillers:
        f()

    @pl.when(t == n_tiles - 1)
    def _():
        for g, buf in enumerate(ext):
            nconv_ref[0, :, g * step:(g + 1) * step] = buf[tm + base:tm + CONV_ROWS, :]

    for buf in ext:
        buf[0:CONV_ROWS, :] = buf[tm:tm + CONV_ROWS, :]


def _inproj_conv(x, conv0, norm_w, w_in_r, wconv, tm):
    n_seq, seq_len, d = x.shape
    n_tiles = seq_len // tm
    tile = lambda b, t: (b, t, 0)
    const = lambda b, t: (0, 0)
    mix, gates, nconv = pl.pallas_call(
        functools.partial(_inproj_conv_kernel, n_tiles=n_tiles),
        grid=(n_seq, n_tiles),
        in_specs=[
            pl.BlockSpec((1, tm, d), tile),
            pl.BlockSpec((1, d), const),
            pl.BlockSpec((d, IN_COLS), const),
            pl.BlockSpec((SUBLANES, CONV_DIM), const),
            pl.BlockSpec((1, CONV_WIDTH - 1, CONV_DIM), lambda b, t: (b, 0, 0)),
        ],
        out_specs=[
            pl.BlockSpec((1, tm, MIX_COLS), tile),
            pl.BlockSpec((1, tm, GATE_COLS), tile),
            pl.BlockSpec((1, CONV_WIDTH - 1, CONV_DIM), lambda b, t: (b, 0, 0)),
        ],
        out_shape=[
            jax.ShapeDtypeStruct((n_seq, seq_len, MIX_COLS), F32),
            jax.ShapeDtypeStruct((n_seq, seq_len, GATE_COLS), F32),
            jax.ShapeDtypeStruct((n_seq, CONV_WIDTH - 1, CONV_DIM), F32),
        ],
        scratch_shapes=[pltpu.VMEM((tm, d), BF16)] + [
            pltpu.VMEM((CONV_ROWS + tm, CONV_DIM // CONV_GROUPS), F32) for _ in range(CONV_GROUPS)],
        compiler_params=pltpu.CompilerParams(dimension_semantics=("arbitrary", "arbitrary"),
                                             vmem_limit_bytes=VMEM_LIMIT),
        name="inproj_conv",
    )(x, norm_w, w_in_r, wconv, conv0)
    n = n_seq * seq_len
    return mix.reshape(n, MIX_COLS), gates.reshape(n, GATE_COLS), nconv


def _mixer_kernel(proj_ref, pool0_ref, conv0_ref, s0_ref, wpool_ref, pscale_ref, wconv_ref, alog_ref, dtb_ref, onw_ref,
                  ypool_ref, ydelta_ref, npool_ref, nconv_ref, ns_ref,
                  ext_p, ext_c, q_s, k_s, v_s, g_s, b_s, state,
                  *, sb, tt, tv, chunk, pos0, n_tiles, qkv_done):
    t = pl.program_id(1)
    base = CONV_ROWS - (CONV_WIDTH - 1)

    @pl.when(t == 0)
    def _():
        ext_p[:, POOL_ROWS - POOL_BUF:POOL_ROWS, :] = pool0_ref[...]
        if not qkv_done:
            ext_c[:, base:CONV_ROWS, :] = conv0_ref[...]
        state[...] = s0_ref[...]

    row = lax.broadcasted_iota(jnp.int32, (tt, 1), 0)
    pos = pos0 + t * tt + row
    for s in range(sb):
        ext_p[s, POOL_ROWS:POOL_ROWS + tt, :] = proj_ref[s, :, OFF_POOL:OFF_POOL + POOL_DIM]
        if not qkv_done:
            ext_c[s, CONV_ROWS:CONV_ROWS + tt, :] = proj_ref[s, :, OFF_QKV:OFF_QKV + CONV_DIM]

        for gi, win in enumerate(POOL_WINDOWS):
            cs = slice(gi * POOL_GROUP_DIM, (gi + 1) * POOL_GROUP_DIM)
            u = ext_p[s, POOL_ROWS:POOL_ROWS + tt, cs]
            acc = u
            for j in range(1, win):
                acc = acc + ext_p[s, POOL_ROWS - j:POOL_ROWS - j + tt, cs]
            cnt = jnp.minimum(win, pos + 1).astype(F32)
            pooled = acc / cnt - u
            ypool_ref[s, :, cs] = (_dot(pooled, wpool_ref[gi]) * pscale_ref[:, cs]).astype(BF16)

        for ci in range(0 if qkv_done else CONV_DIM // LANES):
            hs = slice((ci % N_HEADS) * HEAD_DIM, (ci % N_HEADS + 1) * HEAD_DIM)
            (q_s, k_s, v_s)[ci // N_HEADS][s, :, hs] = _conv_silu_qkv(ext_c, s, tt, wconv_ref, ci)

        beta = _sigmoid(proj_ref[s, :, OFF_B:OFF_B + LANES])
        g = -jnp.exp(alog_ref[...]) * _softplus(proj_ref[s, :, OFF_A:OFF_A + LANES] + dtb_ref[...])
        if tv < tt:
            beta = jnp.where(row < tv, beta, 0.0)
            g = jnp.where(row < tv, g, 0.0)
        b_s[s] = beta
        g_s[s] = g

    @pl.when(t == n_tiles - 1)
    def _():
        npool_ref[...] = ext_p[:, tv + POOL_ROWS - POOL_BUF:tv + POOL_ROWS, :]
        if qkv_done:
            nconv_ref[...] = jnp.zeros_like(nconv_ref)
        else:
            nconv_ref[...] = ext_c[:, tv + base:tv + CONV_ROWS, :]

    if n_tiles > 1:
        ext_p[:, 0:POOL_ROWS, :] = ext_p[:, tt:tt + POOL_ROWS, :]
        if not qkv_done:
            ext_c[:, 0:CONV_ROWS, :] = ext_c[:, tt:tt + CONV_ROWS, :]

    ri = lax.broadcasted_iota(jnp.int32, (chunk, chunk), 0)
    cj = lax.broadcasted_iota(jnp.int32, (chunk, chunk), 1)
    causal = ri >= cj
    strict = ri > cj
    ltri = causal.astype(F32)
    eye = (ri == cj).astype(F32)
    n_doublings = chunk.bit_length() - 2
    units = [(s, h) for s in range(sb) for h in range(N_HEADS)]
    hsl = [slice(h * HEAD_DIM, (h + 1) * HEAD_DIM) for h in range(N_HEADS)]

    def qkv(which, s, rows, h):
        if qkv_done:
            c0 = OFF_QKV + which * QK_DIM + h * HEAD_DIM
            return proj_ref[s, rows, c0:c0 + HEAD_DIM]
        return (q_s, k_s, v_s)[which][s, rows, hsl[h]]

    def chunk_body(ci, carry):
        r0 = pl.multiple_of(ci * chunk, chunk)
        rows = pl.ds(r0, chunk)
        g_all, g_t, b_all = [], [], []
        for s in range(sb):
            ga = jnp.dot(ltri, g_s[s, rows, :], precision=HIGHEST, preferred_element_type=F32)
            g_all.append(ga)
            if chunk < LANES:
                ga = jnp.concatenate([ga, jnp.zeros((LANES - chunk, LANES), F32)], axis=0)
            g_t.append(ga.T)
            b_all.append(b_s[s, rows, :])
        g_col = [g_all[s][:, h:h + 1] for s, h in units]
        dec = [jnp.exp(jnp.minimum(g_col[i] - g_t[s][h:h + 1, 0:chunk], 0.0)) for i, (s, h) in enumerate(units)]
        b_col = [b_all[s][:, h:h + 1] for s, h in units]
        q = [qkv(0, s, rows, h) for s, h in units]
        k = [qkv(1, s, rows, h) for s, h in units]
        idx = range(len(units))
        kb = [k[i] * b_col[i] for i in idx]
        kbk = [_dot_nt(kb[i], k[i]) for i in idx]
        qk = [_dot_nt(q[i], k[i]) for i in idx]
        p_mat = [jnp.where(strict, kbk[i] * dec[i], 0.0) for i in idx]
        t_inv = [eye - p_mat[i] for i in idx]
        for _ in range(n_doublings):
            p_mat = [_dot(p_mat[i], p_mat[i]) for i in idx]
            xp = [_dot(t_inv[i], p_mat[i]) for i in idx]
            t_inv = [t_inv[i] + xp[i] for i in idx]
        g_exp = [jnp.exp(g_col[i]) for i in idx]
        wu = [_dot(t_inv[i], jnp.concatenate([kb[i] * g_exp[i], qkv(2, s, rows, h) * b_col[i]], axis=1))
              for i, (s, h) in enumerate(units)]
        s_old = [state[s, h] for s, h in units]
        res = [_dot(jnp.concatenate([wu[i][:, 0:HEAD_DIM], q[i] * g_exp[i]], axis=0), s_old[i]) for i in idx]
        v_new = [wu[i][:, HEAD_DIM:2 * HEAD_DIM] - res[i][0:chunk] for i in idx]
        a_qk = [jnp.where(causal, qk[i] * dec[i], 0.0) for i in idx]
        g_last = [g_all[s][chunk - 1:chunk, h:h + 1] for s, h in units]
        k_dec = [k[i] * jnp.exp(g_last[i] - g_col[i]) for i in idx]
        intra = [_dot(a_qk[i], v_new[i]) for i in idx]
        upd = [_dot_tn(k_dec[i], v_new[i]) for i in idx]
        for i, (s, h) in enumerate(units):
            state[s, h] = s_old[i] * jnp.exp(g_last[i]) + upd[i]
            o_c = res[i][chunk:2 * chunk] + intra[i]
            z_c = proj_ref[s, rows, OFF_Z + h * HEAD_DIM:OFF_Z + (h + 1) * HEAD_DIM]
            ydelta_ref[s, rows, hsl[h]] = (_rms(o_c, onw_ref[...]) * (z_c * _sigmoid(z_c))).astype(BF16)
        return carry

    lax.fori_loop(0, tt // chunk, chunk_body, 0)

    @pl.when(t == n_tiles - 1)
    def _():
        ns_ref[...] = state[...]


def _mixers(proj, pool0, conv0, s0, wpool, pscale, wconv, alog, dtb, onw, *, n_seq, sb, tt, tv, chunk, pos0, n_tiles,
            qkv_done):
    kern = functools.partial(_mixer_kernel, sb=sb, tt=tt, tv=tv, chunk=chunk, pos0=pos0, n_tiles=n_tiles,
                             qkv_done=qkv_done)
    seq_len = n_tiles * tt
    tile = lambda b, t: (b, t, 0)
    seq3 = lambda b, t: (b, 0, 0)
    seq4 = lambda b, t: (b, 0, 0, 0)
    const2 = lambda b, t: (0, 0)
    ypool, ydelta, npool, nconv, ns = pl.pallas_call(
        kern,
        grid=(n_seq // sb, n_tiles),
        in_specs=[
            pl.BlockSpec((sb, tt, MIX_COLS), tile),
            pl.BlockSpec((sb, POOL_BUF, POOL_DIM), seq3),
            pl.BlockSpec((sb, CONV_WIDTH - 1, CONV_DIM), seq3),
            pl.BlockSpec((sb, N_HEADS, HEAD_DIM, HEAD_DIM), seq4),
            pl.BlockSpec((len(POOL_WINDOWS), POOL_GROUP_DIM, POOL_GROUP_DIM), lambda b, t: (0, 0, 0)),
            pl.BlockSpec((1, POOL_DIM), const2),
            pl.BlockSpec((SUBLANES, CONV_DIM), const2),
            pl.BlockSpec((1, LANES), const2),
            pl.BlockSpec((1, LANES), const2),
            pl.BlockSpec((1, HEAD_DIM), const2),
        ],
        out_specs=[
            pl.BlockSpec((sb, tt, POOL_DIM), tile),
            pl.BlockSpec((sb, tt, QK_DIM), tile),
            pl.BlockSpec((sb, POOL_BUF, POOL_DIM), seq3),
            pl.BlockSpec((sb, CONV_WIDTH - 1, CONV_DIM), seq3),
            pl.BlockSpec((sb, N_HEADS, HEAD_DIM, HEAD_DIM), seq4),
        ],
        out_shape=[
            jax.ShapeDtypeStruct((n_seq, seq_len, POOL_DIM), BF16),
            jax.ShapeDtypeStruct((n_seq, seq_len, QK_DIM), BF16),
            jax.ShapeDtypeStruct((n_seq, POOL_BUF, POOL_DIM), F32),
            jax.ShapeDtypeStruct((n_seq, CONV_WIDTH - 1, CONV_DIM), F32),
            jax.ShapeDtypeStruct((n_seq, N_HEADS, HEAD_DIM, HEAD_DIM), F32),
        ],
        scratch_shapes=[
            pltpu.VMEM((sb, POOL_ROWS + tt, POOL_DIM), F32),
            pltpu.VMEM((sb, CONV_ROWS + tt, CONV_DIM), F32),
            pltpu.VMEM((sb, tt, QK_DIM), F32),
            pltpu.VMEM((sb, tt, QK_DIM), F32),
            pltpu.VMEM((sb, tt, QK_DIM), F32),
            pltpu.VMEM((sb, tt, LANES), F32),
            pltpu.VMEM((sb, tt, LANES), F32),
            pltpu.VMEM((sb, N_HEADS, HEAD_DIM, HEAD_DIM), F32),
        ],
        compiler_params=pltpu.CompilerParams(dimension_semantics=("arbitrary", "arbitrary"),
                                             vmem_limit_bytes=VMEM_LIMIT),
        name="mixers",
    )(proj.reshape(n_seq, seq_len, MIX_COLS), pool0, conv0, s0, wpool, pscale, wconv, alog, dtb, onw)
    n = n_seq * seq_len
    return ypool.reshape(n, POOL_DIM), ydelta.reshape(n, QK_DIM), npool, nconv, ns


def _merge_kernel(x_ref, gates_ref, ypool_ref, ydelta_ref, wbp_ref, wbd_ref, wout_ref, n2w_ref, wr_hi_ref, wr_lo_ref,
                  br_ref, xm_ref, h2_ref, route_ref, wk_ref, cnt_ref, cnt_s, *, period, valid):
    tm = x_ref.shape[0]

    @pl.when(pl.program_id(0) == 0)
    def _():
        cnt_s[...] = jnp.zeros_like(cnt_s)

    bp = _dot(ypool_ref[...], wbp_ref[...])
    bd = _dot(ydelta_ref[...], wbd_ref[...])
    merged = _sigmoid(gates_ref[:, 0:1024]) * bp + _sigmoid(gates_ref[:, 1024:2048]) * bd
    xm = x_ref[...] + _dot(merged, wout_ref[...])
    xm_ref[...] = xm
    h2 = _rms(xm, n2w_ref[...])
    h2_ref[...] = _pack_pairs(h2)

    h2_hi = h2.astype(BF16)
    h2_lo = (h2 - h2_hi.astype(F32)).astype(BF16)
    logits = (jnp.dot(h2_hi, wr_hi_ref[...], preferred_element_type=F32)
              + (jnp.dot(h2_hi, wr_lo_ref[...], preferred_element_type=F32)
                 + jnp.dot(h2_lo, wr_hi_ref[...], preferred_element_type=F32))) + br_ref[...]
    lane = lax.broadcasted_iota(jnp.int32, logits.shape, 1)
    neg = -jnp.inf
    far = LANES - 1
    is_g = lane < N_GROUPS
    g_max = jnp.max(jnp.where(is_g, logits, neg), axis=-1, keepdims=True)
    g_sel = jnp.min(jnp.where(is_g & (logits == g_max), lane, far), axis=-1, keepdims=True)
    p_g = 1.0 / jnp.sum(jnp.where(is_g, jnp.exp(logits - g_max), 0.0), axis=-1, keepdims=True)
    e_lane = lane - N_GROUPS
    is_e = (e_lane >= 0) & (e_lane < N_EXPERTS) & ((e_lane // EXPERTS_PER_GROUP) == g_sel)
    ev = jnp.where(is_e, logits, neg)
    v1 = jnp.max(ev, axis=-1, keepdims=True)
    i1 = jnp.min(jnp.where(is_e & (ev == v1), lane, far), axis=-1, keepdims=True)
    is_e2 = is_e & (lane != i1)
    ev2 = jnp.where(is_e2, logits, neg)
    v2 = jnp.max(ev2, axis=-1, keepdims=True)
    i2 = jnp.min(jnp.where(is_e2 & (ev2 == v2), lane, far), axis=-1, keepdims=True)
    e21 = jnp.exp(v2 - v1)
    w1 = p_g / (1.0 + e21)
    w2 = p_g * e21 / (1.0 + e21)
    e1 = i1 - N_GROUPS
    e2 = i2 - N_GROUPS
    chosen = (lane == e1) | (lane == e2)
    if valid < period:
        row = lax.broadcasted_iota(jnp.int32, (tm, 1), 0)
        chosen = chosen & (lax.rem(row, period) < valid)
    onehot = jnp.where(chosen, 1.0, 0.0)
    rr = lax.broadcasted_iota(jnp.int32, (tm, tm), 0)
    cc = lax.broadcasted_iota(jnp.int32, (tm, tm), 1)
    earlier = jnp.where(rr > cc, 1.0, 0.0).astype(BF16)
    before = jnp.dot(earlier, onehot.astype(BF16), preferred_element_type=F32) + cnt_s[...]
    r1 = jnp.sum(jnp.where(lane == e1, before, 0.0), axis=-1, keepdims=True)
    r2 = jnp.sum(jnp.where(lane == e2, before, 0.0), axis=-1, keepdims=True)
    cnt_s[...] = cnt_s[...] + jnp.sum(onehot, axis=0, keepdims=True)
    cnt_ref[...] = cnt_s[...]
    record = jnp.where(lane == 0, e1.astype(F32), jnp.where(lane == 1, e2.astype(F32),
                                                           jnp.where(lane == 2, r1, jnp.where(lane == 3, r2, 0.0))))
    route_ref[...] = record.T[0:SUBLANES, :]
    lane8 = lax.broadcasted_iota(jnp.int32, (tm, SUBLANES), 1)
    wk_ref[...] = jnp.where(lane8 == 0, w1, jnp.where(lane8 == 1, w2, 0.0))


def _merge(x, proj, ypool, ydelta, wbp, wbd, wout, n2w, wr_hi, wr_lo, br, tm, period, valid):
    n, d = x.shape
    row = lambda i: (i, 0)
    const = lambda i: (0, 0)
    return pl.pallas_call(
        functools.partial(_merge_kernel, period=period, valid=valid),
        grid=(n // tm,),
        in_specs=[
            pl.BlockSpec((tm, d), row),
            pl.BlockSpec((tm, GATE_COLS), row),
            pl.BlockSpec((tm, POOL_DIM), row),
            pl.BlockSpec((tm, QK_DIM), row),
            pl.BlockSpec((POOL_DIM, d), const),
            pl.BlockSpec((QK_DIM, d), const),
            pl.BlockSpec((d, d), const),
            pl.BlockSpec((1, d), const),
            pl.BlockSpec((d, LANES), const),
            pl.BlockSpec((d, LANES), const),
            pl.BlockSpec((1, LANES), const),
        ],
        out_specs=[
            pl.BlockSpec((tm, d), row),
            pl.BlockSpec((tm, d // 2), row),
            pl.BlockSpec((SUBLANES, tm), lambda i: (0, i)),
            pl.BlockSpec((tm, SUBLANES), row),
            pl.BlockSpec((1, LANES), const),
        ],
        out_shape=[
            jax.ShapeDtypeStruct((n, d), F32),
            jax.ShapeDtypeStruct((n, d // 2), U32),
            jax.ShapeDtypeStruct((SUBLANES, n), F32),
            jax.ShapeDtypeStruct((n, SUBLANES), F32),
            jax.ShapeDtypeStruct((1, LANES), F32),
        ],
        scratch_shapes=[pltpu.VMEM((1, LANES), F32)],
        compiler_params=pltpu.CompilerParams(dimension_semantics=("arbitrary",), vmem_limit_bytes=VMEM_LIMIT),
        name="merge",
    )(x, proj, ypool, ydelta, wbp, wbd, wout, n2w, wr_hi, wr_lo, br)


def _expert_kernel(te_ref, nv_ref, x_ref, wg_hbm, wu_hbm, wd_hbm, y_ref, st_g, st_u, st_d, wg_s, wu_s, wd_s, sem, slot_s):
    i = pl.program_id(0)
    n_tiles = pl.num_programs(0)
    n_valid = nv_ref[i]
    expert = te_ref[i]

    def expert_at(j):
        return te_ref[jnp.minimum(j, n_tiles - 1)]

    def next_expert_tile(j):
        e = expert_at(j)
        return lax.while_loop(lambda t: (t < n_tiles) & (expert_at(t) == e), lambda t: t + 1, j + 1)

    def start_weight_copies(j, slot):
        @pl.when(j < n_tiles)
        def _():
            for cp in weight_copies(expert_at(j), slot):
                cp.start()

    def weight_copies(e, slot):
        return [pltpu.make_async_copy(src.at[e], dst.at[slot], sem.at[slot])
                for src, dst in ((wg_hbm, st_g), (wu_hbm, st_u), (wd_hbm, st_d))]

    @pl.when(i == 0)
    def _():
        slot_s[0] = 0
        j = i
        for slot in range(EXPERT_STAGES - 1):
            start_weight_copies(j, slot)
            j = next_expert_tile(j)

    @pl.when((i == 0) | (expert != te_ref[jnp.maximum(i - 1, 0)]))
    def _():
        slot = slot_s[0]
        j = i
        for _ in range(EXPERT_STAGES - 1):
            j = next_expert_tile(j)
        start_weight_copies(j, (slot + EXPERT_STAGES - 1) % EXPERT_STAGES)
        for cp in weight_copies(expert, slot):
            cp.wait()
        wg_s[...] = st_g[slot].astype(BF16)
        wu_s[...] = st_u[slot].astype(BF16)
        wd_s[...] = st_d[slot].astype(BF16)
        slot_s[0] = (slot + 1) % EXPERT_STAGES

    @pl.when(n_valid > 0)
    def _():
        tm = x_ref.shape[0]
        part = min(tm, EXPERT_PART_ROWS)
        parts = [slice(r0, r0 + part) for r0 in range(0, tm, part)]
        row = lax.broadcasted_iota(jnp.int32, (part, 1), 0)
        xs = []
        for p in parts:
            left, right = _unpack_pairs(jnp.where(row + p.start < n_valid, x_ref[p, :], jnp.uint32(0)))
            xs.append(jnp.concatenate([left.astype(BF16), right.astype(BF16)], axis=1))
        a = [_dot(x, wg_s[...]) for x in xs]
        b = [_dot(x, wu_s[...]) for x in xs]
        act = [(a[j] * _sigmoid(a[j])) * b[j] for j in range(len(parts))]
        y = [_dot(act[j], wd_s[...]) for j in range(len(parts))]
        for j, p in enumerate(parts):
            y_ref[p, :] = _pack_pairs(y[j])

    @pl.when(n_valid == 0)
    def _():
        y_ref[...] = jnp.zeros_like(y_ref)


def _experts(tile_expert, tile_valid, xs, wg, wu, wd, tm):
    n_rows, dp = xs.shape
    d, f = wg.shape[1], wg.shape[2]
    grid_spec = pltpu.PrefetchScalarGridSpec(
        num_scalar_prefetch=2,
        grid=(n_rows // tm,),
        in_specs=[
            pl.BlockSpec((tm, dp), lambda i, te, nv: (jnp.where(nv[i] > 0, i, 0), 0)),
            pl.BlockSpec(memory_space=pl.ANY),
            pl.BlockSpec(memory_space=pl.ANY),
            pl.BlockSpec(memory_space=pl.ANY),
        ],
        out_specs=pl.BlockSpec((tm, dp), lambda i, te, nv: (i, 0)),
        scratch_shapes=[pltpu.VMEM((EXPERT_STAGES, d, f), F32), pltpu.VMEM((EXPERT_STAGES, d, f), F32),
                        pltpu.VMEM((EXPERT_STAGES, f, d), F32),
                        pltpu.VMEM((d, f), BF16), pltpu.VMEM((d, f), BF16), pltpu.VMEM((f, d), BF16),
                        pltpu.SemaphoreType.DMA((EXPERT_STAGES,)), pltpu.SMEM((1,), jnp.int32)],
    )
    return pl.pallas_call(
        _expert_kernel,
        grid_spec=grid_spec,
        out_shape=jax.ShapeDtypeStruct((n_rows, dp), U32),
        compiler_params=pltpu.CompilerParams(dimension_semantics=("arbitrary",), vmem_limit_bytes=VMEM_LIMIT),
        name="experts",
    )(tile_expert, tile_valid, xs, wg, wu, wd)


def _plan_kernel(route_ref, cnt_ref, pos_ref, tiles_ref, *, tm_e):
    counts = cnt_ref[...]
    padded = jnp.floor((counts + (tm_e - 1)) / tm_e) * tm_e
    lane = lax.broadcasted_iota(jnp.int32, counts.shape, 1)
    ends = padded
    shift = 1
    while shift < N_EXPERTS:
        ends = ends + jnp.where(lane >= shift, pltpu.roll(ends, shift, axis=1), 0.0)
        shift *= 2
    starts = ends - padded

    def lookup(table, key):
        out = jnp.zeros_like(key)
        for e in range(N_EXPERTS):
            out = out + jnp.where(key == e, table[:, e:e + 1], 0.0)
        return out

    for k in range(2):
        pos = lookup(starts, route_ref[k:k + 1, :]) + route_ref[2 + k:3 + k, :]
        pos_ref[k:k + 1, :] = pos.astype(jnp.int32)

    tile_start = lax.broadcasted_iota(jnp.int32, (1, tiles_ref.shape[1]), 1).astype(F32) * tm_e
    tile_expert = jnp.zeros_like(tile_start)
    for e in range(N_EXPERTS):
        tile_expert = tile_expert + jnp.where(ends[:, e:e + 1] <= tile_start, 1.0, 0.0)
    tile_expert = jnp.minimum(tile_expert, N_EXPERTS - 1.0)
    valid = lookup(counts, tile_expert) - (tile_start - lookup(starts, tile_expert))
    tiles_ref[0:1, :] = tile_expert.astype(jnp.int32)
    tiles_ref[1:2, :] = jnp.clip(valid, 0.0, tm_e).astype(jnp.int32)


def _plan(route_t, counts, tm_e, n_tiles):
    n = route_t.shape[1]
    tiles_pad = -(-n_tiles // LANES) * LANES
    return pl.pallas_call(
        functools.partial(_plan_kernel, tm_e=tm_e),
        out_shape=[jax.ShapeDtypeStruct((2, n), jnp.int32), jax.ShapeDtypeStruct((2, tiles_pad), jnp.int32)],
        name="plan",
    )(route_t, counts)


def _sc_workers():
    info = plsc.get_sparse_core_info()
    return info.num_cores, info.num_subcores


def _sc_chunk(per_worker):
    assert per_worker % SUBLANES == 0
    return max(c for c in range(SUBLANES, SC_ROWS + 1, SUBLANES) if per_worker % c == 0)


def _sc_scatter2(rows, idx_a, idx_b, n_out):
    n_cores, n_sub = _sc_workers()
    n, d = rows.shape
    per_worker = n // (n_cores * n_sub)
    assert per_worker * n_cores * n_sub == n
    chunk = _sc_chunk(per_worker)
    mesh = plsc.VectorSubcoreMesh(core_axis_name="c", subcore_axis_name="s")

    n_chunks = per_worker // chunk
    buf = lambda shape, dtype: [pltpu.VMEM(shape, dtype), pltpu.VMEM(shape, dtype)]

    @functools.partial(
        pl.kernel, mesh=mesh, out_type=jax.ShapeDtypeStruct((n_out, d), rows.dtype),
        scratch_types=buf((chunk,), jnp.int32) + buf((chunk,), jnp.int32) + buf((chunk, d), rows.dtype)
        + [pltpu.SemaphoreType.DMA, pltpu.SemaphoreType.DMA])
    def scatter_rows(rows_hbm, ia_hbm, ib_hbm, out_hbm, ia0, ia1, ib0, ib1, r0, r1, sem0, sem1):
        worker = lax.axis_index("s") * n_cores + lax.axis_index("c")
        base = worker * per_worker
        slots = ((ia0, ib0, r0, sem0), (ia1, ib1, r1, sem1))
        pending = {}
        for j in range(n_chunks):
            ia_v, ib_v, rows_v, sem = slots[j % 2]
            if j >= 2:
                for cp in pending.pop(j - 2):
                    cp.wait()
            off = pl.multiple_of(base + j * chunk, SUBLANES)
            pltpu.sync_copy(ia_hbm.at[pl.ds(off, chunk)], ia_v)
            pltpu.sync_copy(ib_hbm.at[pl.ds(off, chunk)], ib_v)
            pltpu.sync_copy(rows_hbm.at[pl.ds(off, chunk)], rows_v)
            pending[j] = (pltpu.async_copy(rows_v, out_hbm.at[ia_v], sem),
                          pltpu.async_copy(rows_v, out_hbm.at[ib_v], sem))
        for j in sorted(pending):
            for cp in pending[j]:
                cp.wait()

    return scatter_rows(rows, idx_a, idx_b)


def _sc_gather(table, idx):
    n_cores, n_sub = _sc_workers()
    n_idx = idx.shape[0]
    d = table.shape[1]
    per_worker = n_idx // (n_cores * n_sub)
    assert per_worker * n_cores * n_sub == n_idx
    chunk = _sc_chunk(per_worker)
    mesh = plsc.VectorSubcoreMesh(core_axis_name="c", subcore_axis_name="s")

    n_chunks = per_worker // chunk

    @functools.partial(
        pl.kernel, mesh=mesh, out_type=jax.ShapeDtypeStruct((n_idx, d), table.dtype),
        scratch_types=[pltpu.VMEM((chunk,), jnp.int32), pltpu.VMEM((chunk,), jnp.int32),
                       pltpu.VMEM((chunk, d), table.dtype), pltpu.VMEM((chunk, d), table.dtype),
                       pltpu.SemaphoreType.DMA, pltpu.SemaphoreType.DMA])
    def gather_rows(table_hbm, idx_hbm, out_hbm, i0, i1, r0, r1, sem0, sem1):
        worker = lax.axis_index("s") * n_cores + lax.axis_index("c")
        base = worker * per_worker
        slots = ((i0, r0, sem0), (i1, r1, sem1))

        def start(j):
            idx_v, rows_v, sem = slots[j % 2]
            off = pl.multiple_of(base + j * chunk, SUBLANES)
            pltpu.sync_copy(idx_hbm.at[pl.ds(off, chunk)], idx_v)
            return pltpu.async_copy(table_hbm.at[idx_v], rows_v, sem)

        nxt = start(0)
        for j in range(n_chunks):
            cur = nxt
            if j + 1 < n_chunks:
                nxt = start(j + 1)
            cur.wait()
            off = pl.multiple_of(base + j * chunk, SUBLANES)
            pltpu.sync_copy(slots[j % 2][1], out_hbm.at[pl.ds(off, chunk)])

    return gather_rows(table, idx)


def _finalize_kernel(xm_ref, ya_ref, yb_ref, wk_ref, fnw_ref, o_ref):
    d = xm_ref.shape[1]
    c = d // 2
    w = wk_ref[...]
    a_left, a_right = _unpack_pairs(ya_ref[...])
    b_left, b_right = _unpack_pairs(yb_ref[...])
    x_left = xm_ref[:, 0:c] + (w[:, 0:1] * a_left + w[:, 1:2] * b_left)
    x_right = xm_ref[:, c:d] + (w[:, 0:1] * a_right + w[:, 1:2] * b_right)
    ms = (jnp.sum(x_left * x_left, axis=-1, keepdims=True) + jnp.sum(x_right * x_right, axis=-1, keepdims=True)) / d
    scale = lax.rsqrt(ms + EPS)
    o_ref[:, 0:c] = x_left * scale * fnw_ref[:, 0:c]
    o_ref[:, c:d] = x_right * scale * fnw_ref[:, c:d]


def _finalize(xm, yk, wk, fnw, tm):
    n, d = xm.shape
    steps = n // tm
    return pl.pallas_call(
        _finalize_kernel,
        grid=(steps,),
        in_specs=[
            pl.BlockSpec((tm, d), lambda i: (i, 0)),
            pl.BlockSpec((tm, d // 2), lambda i: (i, 0)),
            pl.BlockSpec((tm, d // 2), lambda i: (i + steps, 0)),
            pl.BlockSpec((tm, SUBLANES), lambda i: (i, 0)),
            pl.BlockSpec((1, d), lambda i: (0, 0)),
        ],
        out_specs=pl.BlockSpec((tm, d), lambda i: (i, 0)),
        out_shape=jax.ShapeDtypeStruct((n, d), F32),
        compiler_params=pltpu.CompilerParams(dimension_semantics=("arbitrary",), vmem_limit_bytes=VMEM_LIMIT),
        name="finalize",
    )(xm, yk, yk, wk, fnw)


def _moe(h2, route, counts, wg, wu, wd, *, tm_e):
    n = h2.shape[0]
    n_rows = 2 * n + N_EXPERTS * tm_e
    n_tiles = n_rows // tm_e
    pos, tiles = _plan(route, counts, tm_e, n_tiles)
    xs = _sc_scatter2(h2, pos[0], pos[1], n_rows)
    y_sorted = _experts(tiles[0, :n_tiles], tiles[1, :n_tiles], xs, wg, wu, wd, tm_e)
    return _sc_gather(y_sorted, pos.reshape(-1))


def _layer(x_tokens, pool0, conv0, s0, prm, *, n_seq, sb, tt, tv, chunk, pos0, n_tiles, tm):
    conv_in_proj = tv == tt
    if conv_in_proj:
        proj, gates, nconv_proj = _inproj_conv(x_tokens.reshape(n_seq, n_tiles * tt, -1), conv0, prm["norm1_w"],
                                               prm["w_in"], prm["w_conv"], tm)
    else:
        proj, gates = _inproj(x_tokens, prm["norm1_w"], prm["w_in"], tm)
    ypool, ydelta, npool, nconv, ns = _mixers(
        proj, pool0, conv0, s0, prm["w_pool"], prm["pool_scale"], prm["w_conv"], prm["a_log"], prm["dt_bias"],
        prm["o_norm_w"], n_seq=n_seq, sb=sb, tt=tt, tv=tv, chunk=chunk, pos0=pos0, n_tiles=n_tiles,
        qkv_done=conv_in_proj)
    if conv_in_proj:
        nconv = nconv_proj
    xm, h2, route, wk, counts = _merge(
        x_tokens, gates, ypool, ydelta, prm["w_branch_pool"], prm["w_branch_delta"], prm["w_out"], prm["norm2_w"],
        prm["w_router_hi"], prm["w_router_lo"], prm["b_router"], 2 * tm, tt, tv)
    return xm, h2, route, wk, counts, npool, nconv, ns


def _pad_lanes(v, width=LANES):
    v = v.reshape(1, -1).astype(F32)
    return jnp.pad(v, ((0, 0), (0, width - v.shape[1])))


def kernel(x_prompt, x_sample, cache_pool, cache_conv, state_delta, norm1_w, w_in, w_pool, pool_scale, w_conv, a_log, dt_bias, o_norm_w, w_branch_pool, w_branch_delta, w_out, norm2_w, w_router_group, b_router_group, w_router_expert, b_router_expert, w_gate, w_up, w_down, final_norm_w):
    n_layers = norm1_w.shape[0]
    assert n_layers == 1, "single-layer step"
    bsz, seq, d = x_prompt.shape
    dbs, dseq, _ = x_sample.shape
    lyr = 0

    wi = w_in[lyr]
    c = [POOL_DIM, POOL_DIM + CONV_DIM]
    c += [c[1] + N_HEADS, c[1] + 2 * N_HEADS]
    c += [c[3] + QK_DIM, c[3] + QK_DIM + d]
    pad8 = lambda m: jnp.pad(m, ((0, 0), (0, LANES - N_HEADS)))
    w_in_r = jnp.concatenate(
        [wi[:, c[3]:c[4]], wi[:, c[0]:c[1]], wi[:, :c[0]], pad8(wi[:, c[1]:c[2]]), pad8(wi[:, c[2]:c[3]]),
         wi[:, c[4]:c[5]], wi[:, c[5]:]], axis=1).astype(BF16)
    w_router = jnp.concatenate(
        [w_router_group[lyr], w_router_expert[lyr].reshape(d, N_EXPERTS),
         jnp.zeros((d, LANES - N_GROUPS - N_EXPERTS), F32)], axis=1).astype(F32)
    w_router_hi = w_router.astype(BF16)
    b_router = _pad_lanes(jnp.concatenate([b_router_group[lyr], b_router_expert[lyr].reshape(-1)]))
    prm = dict(
        norm1_w=norm1_w[lyr].reshape(1, d), w_in=w_in_r,
        w_pool=w_pool[lyr].astype(BF16), pool_scale=pool_scale[lyr].reshape(1, POOL_DIM),
        w_conv=jnp.pad(w_conv[lyr], ((0, SUBLANES - CONV_WIDTH), (0, 0))),
        a_log=_pad_lanes(a_log[lyr]), dt_bias=_pad_lanes(dt_bias[lyr]), o_norm_w=o_norm_w[lyr].reshape(1, HEAD_DIM),
        w_branch_pool=w_branch_pool[lyr].astype(BF16), w_branch_delta=w_branch_delta[lyr].astype(BF16),
        w_out=w_out[lyr].astype(BF16), norm2_w=norm2_w[lyr].reshape(1, d),
        w_router_hi=w_router_hi, w_router_lo=(w_router - w_router_hi.astype(F32)).astype(BF16), b_router=b_router,
    )
    wg, wu, wd = w_gate[lyr], w_up[lyr], w_down[lyr]
    fnw = final_norm_w.reshape(1, d)

    tt_p = TOKEN_TILE // 2
    xm, h2, route, wk, counts, npool, nconv, ns = _layer(
        x_prompt.reshape(bsz * seq, d),
        jnp.zeros((bsz, POOL_BUF, POOL_DIM), F32), jnp.zeros((bsz, CONV_WIDTH - 1, CONV_DIM), F32),
        jnp.zeros((bsz, N_HEADS, HEAD_DIM, HEAD_DIM), F32), prm,
        n_seq=bsz, sb=4, tt=tt_p, tv=tt_p, chunk=DELTA_CHUNK, pos0=0, n_tiles=seq // tt_p, tm=TOKEN_TILE)
    yk = _moe(h2, route, counts, wg, wu, wd, tm_e=2 * TOKEN_TILE)
    y_prompt = _finalize(xm, yk, wk, fnw, 2 * TOKEN_TILE).reshape(bsz, seq, d)
    pool_p = npool[None]
    conv_p = nconv[None]
    delta_p = ns[None]

    tt_s = SUBLANES
    xs_pad = jnp.pad(x_sample, ((0, 0), (0, tt_s - dseq), (0, 0))).reshape(dbs * tt_s, d)
    xm, h2, route, wk, counts, npool, nconv, ns = _layer(
        xs_pad, cache_pool[lyr], cache_conv[lyr], state_delta[lyr], prm,
        n_seq=dbs, sb=8, tt=tt_s, tv=dseq, chunk=tt_s, pos0=PAST_LEN, n_tiles=1, tm=TOKEN_TILE)
    real = lambda a: a.reshape(dbs, tt_s, -1)[:, :dseq].reshape(dbs * dseq, -1)
    route = route.reshape(SUBLANES, dbs, tt_s)[:, :, :dseq].reshape(SUBLANES, dbs * dseq)
    yk = _moe(real(h2), route, counts, wg, wu, wd, tm_e=TOKEN_TILE // 2)
    y_sample = _finalize(real(xm), yk, real(wk), fnw, TOKEN_TILE).reshape(dbs, dseq, d)
    pool_s = npool[None]
    conv_s = nconv[None]
    delta_s = ns[None]
    return (y_prompt, y_sample, pool_p, conv_p, delta_p, pool_s, conv_s, delta_s)
```

```python
import functools

import jax
import jax.numpy as jnp
from jax import lax
from jax.experimental import pallas as pl
from jax.experimental.pallas import tpu as pltpu
from jax.experimental.pallas import tpu_sc as plsc

F32 = jnp.float32
BF16 = jnp.bfloat16
U32 = jnp.uint32
EPS = 1e-6
HIGHEST = lax.Precision.HIGHEST

POOL_WINDOWS = (2, 4, 8, 16)
POOL_GROUP_DIM = 128
POOL_DIM = len(POOL_WINDOWS) * POOL_GROUP_DIM
POOL_BUF = max(POOL_WINDOWS) - 1
N_HEADS = 8
HEAD_DIM = 128
QK_DIM = N_HEADS * HEAD_DIM
CONV_WIDTH = 4
CONV_DIM = 3 * QK_DIM
N_GROUPS = 4
EXPERTS_PER_GROUP = 8
N_EXPERTS = N_GROUPS * EXPERTS_PER_GROUP
PAST_LEN = 16384
DELTA_CHUNK = 64
TOKEN_TILE = 256
LANES = 128
SUBLANES = 8

OFF_Z = 0
OFF_QKV = OFF_Z + QK_DIM
OFF_POOL = OFF_QKV + CONV_DIM
OFF_B = OFF_POOL + POOL_DIM
OFF_A = OFF_B + LANES
MIX_COLS = OFF_A + LANES
GATE_COLS = 2048
IN_COLS = MIX_COLS + GATE_COLS

POOL_ROWS = 16
CONV_ROWS = 8
VMEM_LIMIT = 60 * 1024 * 1024
EXPERT_PART_ROWS = 128
EXPERT_STAGES = 4
CONV_GROUPS = 12
SC_ROWS = 96


def _sigmoid(x):
    return 1.0 / (1.0 + jnp.exp(-x))


def _softplus(x):
    return jnp.maximum(x, 0.0) + jnp.log1p(jnp.exp(-jnp.abs(x)))


def _rms(x, w):
    return x * lax.rsqrt(jnp.mean(x * x, axis=-1, keepdims=True) + EPS) * w


def _dot(a, b):
    return jnp.dot(a.astype(BF16), b.astype(BF16), preferred_element_type=F32)


def _dot_nt(a, b):
    return lax.dot_general(a.astype(BF16), b.astype(BF16), (((1,), (1,)), ((), ())), preferred_element_type=F32)


def _dot_tn(a, b):
    return lax.dot_general(a.astype(BF16), b.astype(BF16), (((0,), (0,)), ((), ())), preferred_element_type=F32)


def _pack_pairs(x):
    c = x.shape[1] // 2
    hi = pltpu.bitcast(x[:, :c].astype(BF16).astype(F32), U32)
    lo = pltpu.bitcast(x[:, c:].astype(BF16).astype(F32), U32)
    return hi | lax.shift_right_logical(lo, jnp.uint32(16))


def _unpack_pairs(w):
    left = pltpu.bitcast(w & jnp.uint32(0xFFFF0000), F32)
    right = pltpu.bitcast(lax.shift_left(w, jnp.uint32(16)), F32)
    return left, right


def _inproj_kernel(x_ref, nw_ref, w_ref, mix_ref, gate_ref):
    h = _rms(x_ref[...], nw_ref[...]).astype(BF16)
    half = MIX_COLS // 2
    for c0 in (0, half):
        mix_ref[:, c0:c0 + half] = jnp.dot(h, w_ref[:, c0:c0 + half], preferred_element_type=F32)
    half = GATE_COLS // 2
    for c0 in (0, half):
        gate_ref[:, c0:c0 + half] = jnp.dot(h, w_ref[:, MIX_COLS + c0:MIX_COLS + c0 + half],
                                            preferred_element_type=F32)


def _inproj(x, norm_w, w_in_r, tm):
    n, d = x.shape
    return pl.pallas_call(
        _inproj_kernel,
        grid=(n // tm,),
        in_specs=[
            pl.BlockSpec((tm, d), lambda i: (i, 0)),
            pl.BlockSpec((1, d), lambda i: (0, 0)),
            pl.BlockSpec((d, IN_COLS), lambda i: (0, 0)),
        ],
        out_specs=[pl.BlockSpec((tm, MIX_COLS), lambda i: (i, 0)), pl.BlockSpec((tm, GATE_COLS), lambda i: (i, 0))],
        out_shape=[jax.ShapeDtypeStruct((n, MIX_COLS), F32), jax.ShapeDtypeStruct((n, GATE_COLS), F32)],
        compiler_params=pltpu.CompilerParams(dimension_semantics=("arbitrary",), vmem_limit_bytes=VMEM_LIMIT),
        name="inproj",
    )(x, norm_w, w_in_r)


def _conv_silu_qkv(ext_c, seq, tt, wconv_ref, ci, col0=0):
    base = CONV_ROWS - (CONV_WIDTH - 1)
    cs = slice(ci * LANES, (ci + 1) * LANES)
    cl = slice(ci * LANES - col0, (ci + 1) * LANES - col0)
    idx = (lambda r: (r, cl)) if seq is None else (lambda r: (seq, r, cl))
    acc = ext_c[idx(slice(base, base + tt))] * wconv_ref[0:1, cs]
    for j in range(1, CONV_WIDTH):
        acc = acc + ext_c[idx(slice(base + j, base + j + tt))] * wconv_ref[j:j + 1, cs]
    y = acc * _sigmoid(acc)
    if ci < 2 * N_HEADS:
        y = y * lax.rsqrt(jnp.sum(y * y, axis=-1, keepdims=True) + EPS)
    if ci < N_HEADS:
        y = y * (HEAD_DIM ** -0.5)
    return y


def _inproj_conv_kernel(x_ref, nw_ref, w_ref, wconv_ref, conv0_ref, mix_ref, gate_ref, nconv_ref, h_s, *ext, n_tiles):
    t = pl.program_id(1)
    tm = x_ref.shape[1]
    base = CONV_ROWS - (CONV_WIDTH - 1)
    step = CONV_DIM // len(ext)

    @pl.when(t == 0)
    def _():
        for g, buf in enumerate(ext):
            buf[base:CONV_ROWS, :] = conv0_ref[0, :, g * step:(g + 1) * step]

    h_s[...] = _rms(x_ref[0], nw_ref[...]).astype(BF16)

    def proj(c0, c1):
        return jnp.dot(h_s[...], w_ref[:, c0:c1], preferred_element_type=F32)

    def fill_z0():
        mix_ref[0, :, OFF_Z:OFF_Z + QK_DIM // 2] = proj(OFF_Z, OFF_Z + QK_DIM // 2)

    def fill_z1():
        mix_ref[0, :, OFF_Z + QK_DIM // 2:OFF_QKV] = proj(OFF_Z + QK_DIM // 2, OFF_QKV)

    def fill_pool():
        mix_ref[0, :, OFF_POOL:MIX_COLS] = proj(OFF_POOL, MIX_COLS)

    def fill_gate(i):
        q = GATE_COLS // 4
        gate_ref[0, :, i * q:(i + 1) * q] = proj(MIX_COLS + i * q, MIX_COLS + (i + 1) * q)

    fillers = [fill_z0, fill_z1, fill_pool] + [functools.partial(fill_gate, i) for i in range(4)]
    for g, buf in enumerate(ext):
        c0 = g * step
        buf[CONV_ROWS:CONV_ROWS + tm, :] = proj(OFF_QKV + c0, OFF_QKV + c0 + step)
        if fillers:
            fillers.pop(0)()
        for ci in range(c0 // LANES, (c0 + step) // LANES):
            mix_ref[0, :, OFF_QKV + ci * LANES:OFF_QKV + (ci + 1) * LANES] = _conv_silu_qkv(
                buf, None, tm, wconv_ref, ci, c0)
    for f in fillers:
        f()

    @pl.when(t == n_tiles - 1)
    def _():
        for g, buf in enumerate(ext):
            nconv_ref[0, :, g * step:(g + 1) * step] = buf[tm + base:tm + CONV_ROWS, :]

    for buf in ext:
        buf[0:CONV_ROWS, :] = buf[tm:tm + CONV_ROWS, :]


def _inproj_conv(x, conv0, norm_w, w_in_r, wconv, tm):
    n_seq, seq_len, d = x.shape
    n_tiles = seq_len // tm
    tile = lambda b, t: (b, t, 0)
    const = lambda b, t: (0, 0)
    mix, gates, nconv = pl.pallas_call(
        functools.partial(_inproj_conv_kernel, n_tiles=n_tiles),
        grid=(n_seq, n_tiles),
        in_specs=[
            pl.BlockSpec((1, tm, d), tile),
            pl.BlockSpec((1, d), const),
            pl.BlockSpec((d, IN_COLS), const),
            pl.BlockSpec((SUBLANES, CONV_DIM), const),
            pl.BlockSpec((1, CONV_WIDTH - 1, CONV_DIM), lambda b, t: (b, 0, 0)),
        ],
        out_specs=[
            pl.BlockSpec((1, tm, MIX_COLS), tile),
            pl.BlockSpec((1, tm, GATE_COLS), tile),
            pl.BlockSpec((1, CONV_WIDTH - 1, CONV_DIM), lambda b, t: (b, 0, 0)),
        ],
        out_shape=[
            jax.ShapeDtypeStruct((n_seq, seq_len, MIX_COLS), F32),
            jax.ShapeDtypeStruct((n_seq, seq_len, GATE_COLS), F32),
            jax.ShapeDtypeStruct((n_seq, CONV_WIDTH - 1, CONV_DIM), F32),
        ],
        scratch_shapes=[pltpu.VMEM((tm, d), BF16)] + [
            pltpu.VMEM((CONV_ROWS + tm, CONV_DIM // CONV_GROUPS), F32) for _ in range(CONV_GROUPS)],
        compiler_params=pltpu.CompilerParams(dimension_semantics=("arbitrary", "arbitrary"),
                                             vmem_limit_bytes=VMEM_LIMIT),
        name="inproj_conv",
    )(x, norm_w, w_in_r, wconv, conv0)
    n = n_seq * seq_len
    return mix.reshape(n, MIX_COLS), gates.reshape(n, GATE_COLS), nconv


def _mixer_kernel(proj_ref, pool0_ref, conv0_ref, s0_ref, wpool_ref, pscale_ref, wconv_ref, alog_ref, dtb_ref, onw_ref,
                  ypool_ref, ydelta_ref, npool_ref, nconv_ref, ns_ref,
                  ext_p, ext_c, q_s, k_s, v_s, g_s, b_s, state,
                  *, sb, tt, tv, chunk, pos0, n_tiles, qkv_done):
    t = pl.program_id(1)
    base = CONV_ROWS - (CONV_WIDTH - 1)

    @pl.when(t == 0)
    def _():
        ext_p[:, POOL_ROWS - POOL_BUF:POOL_ROWS, :] = pool0_ref[...]
        if not qkv_done:
            ext_c[:, base:CONV_ROWS, :] = conv0_ref[...]
        state[...] = s0_ref[...]

    row = lax.broadcasted_iota(jnp.int32, (tt, 1), 0)
    pos = pos0 + t * tt + row
    for s in range(sb):
        ext_p[s, POOL_ROWS:POOL_ROWS + tt, :] = proj_ref[s, :, OFF_POOL:OFF_POOL + POOL_DIM]
        if not qkv_done:
            ext_c[s, CONV_ROWS:CONV_ROWS + tt, :] = proj_ref[s, :, OFF_QKV:OFF_QKV + CONV_DIM]

        for gi, win in enumerate(POOL_WINDOWS):
            cs = slice(gi * POOL_GROUP_DIM, (gi + 1) * POOL_GROUP_DIM)
            u = ext_p[s, POOL_ROWS:POOL_ROWS + tt, cs]
            acc = u
            for j in range(1, win):
                acc = acc + ext_p[s, POOL_ROWS - j:POOL_ROWS - j + tt, cs]
            cnt = jnp.minimum(win, pos + 1).astype(F32)
            pooled = acc / cnt - u
            ypool_ref[s, :, cs] = (_dot(pooled, wpool_ref[gi]) * pscale_ref[:, cs]).astype(BF16)

        for ci in range(0 if qkv_done else CONV_DIM // LANES):
            hs = slice((ci % N_HEADS) * HEAD_DIM, (ci % N_HEADS + 1) * HEAD_DIM)
            (q_s, k_s, v_s)[ci // N_HEADS][s, :, hs] = _conv_silu_qkv(ext_c, s, tt, wconv_ref, ci)

        beta = _sigmoid(proj_ref[s, :, OFF_B:OFF_B + LANES])
        g = -jnp.exp(alog_ref[...]) * _softplus(proj_ref[s, :, OFF_A:OFF_A + LANES] + dtb_ref[...])
        if tv < tt:
            beta = jnp.where(row < tv, beta, 0.0)
            g = jnp.where(row < tv, g, 0.0)
        b_s[s] = beta
        g_s[s] = g

    @pl.when(t == n_tiles - 1)
    def _():
        npool_ref[...] = ext_p[:, tv + POOL_ROWS - POOL_BUF:tv + POOL_ROWS, :]
        if qkv_done:
            nconv_ref[...] = jnp.zeros_like(nconv_ref)
        else:
            nconv_ref[...] = ext_c[:, tv + base:tv + CONV_ROWS, :]

    if n_tiles > 1:
        ext_p[:, 0:POOL_ROWS, :] = ext_p[:, tt:tt + POOL_ROWS, :]
        if not qkv_done:
            ext_c[:, 0:CONV_ROWS, :] = ext_c[:, tt:tt + CONV_ROWS, :]

    ri = lax.broadcasted_iota(jnp.int32, (chunk, chunk), 0)
    cj = lax.broadcasted_iota(jnp.int32, (chunk, chunk), 1)
    causal = ri >= cj
    strict = ri > cj
    ltri = causal.astype(F32)
    eye = (ri == cj).astype(F32)
    n_doublings = chunk.bit_length() - 2
    units = [(s, h) for s in range(sb) for h in range(N_HEADS)]
    hsl = [slice(h * HEAD_DIM, (h + 1) * HEAD_DIM) for h in range(N_HEADS)]

    def qkv(which, s, rows, h):
        if qkv_done:
            c0 = OFF_QKV + which * QK_DIM + h * HEAD_DIM
            return proj_ref[s, rows, c0:c0 + HEAD_DIM]
        return (q_s, k_s, v_s)[which][s, rows, hsl[h]]

    def chunk_body(ci, carry):
        r0 = pl.multiple_of(ci * chunk, chunk)
        rows = pl.ds(r0, chunk)
        g_all, g_t, b_all = [], [], []
        for s in range(sb):
            ga = jnp.dot(ltri, g_s[s, rows, :], precision=HIGHEST, preferred_element_type=F32)
            g_all.append(ga)
            if chunk < LANES:
                ga = jnp.concatenate([ga, jnp.zeros((LANES - chunk, LANES), F32)], axis=0)
            g_t.append(ga.T)
            b_all.append(b_s[s, rows, :])
        g_col = [g_all[s][:, h:h + 1] for s, h in units]
        dec = [jnp.exp(jnp.minimum(g_col[i] - g_t[s][h:h + 1, 0:chunk], 0.0)) for i, (s, h) in enumerate(units)]
        b_col = [b_all[s][:, h:h + 1] for s, h in units]
        q = [qkv(0, s, rows, h) for s, h in units]
        k = [qkv(1, s, rows, h) for s, h in units]
        idx = range(len(units))
        kb = [k[i] * b_col[i] for i in idx]
        kbk = [_dot_nt(kb[i], k[i]) for i in idx]
        qk = [_dot_nt(q[i], k[i]) for i in idx]
        p_mat = [jnp.where(strict, kbk[i] * dec[i], 0.0) for i in idx]
        t_inv = [eye - p_mat[i] for i in idx]
        for _ in range(n_doublings):
            p_mat = [_dot(p_mat[i], p_mat[i]) for i in idx]
            xp = [_dot(t_inv[i], p_mat[i]) for i in idx]
            t_inv = [t_inv[i] + xp[i] for i in idx]
        g_exp = [jnp.exp(g_col[i]) for i in idx]
        wu = [_dot(t_inv[i], jnp.concatenate([kb[i] * g_exp[i], qkv(2, s, rows, h) * b_col[i]], axis=1))
              for i, (s, h) in enumerate(units)]
        s_old = [state[s, h] for s, h in units]
        res = [_dot(jnp.concatenate([wu[i][:, 0:HEAD_DIM], q[i] * g_exp[i]], axis=0), s_old[i]) for i in idx]
        v_new = [wu[i][:, HEAD_DIM:2 * HEAD_DIM] - res[i][0:chunk] for i in idx]
        a_qk = [jnp.where(causal, qk[i] * dec[i], 0.0) for i in idx]
        g_last = [g_all[s][chunk - 1:chunk, h:h + 1] for s, h in units]
        k_dec = [k[i] * jnp.exp(g_last[i] - g_col[i]) for i in idx]
        intra = [_dot(a_qk[i], v_new[i]) for i in idx]
        upd = [_dot_tn(k_dec[i], v_new[i]) for i in idx]
        for i, (s, h) in enumerate(units):
            state[s, h] = s_old[i] * jnp.exp(g_last[i]) + upd[i]
            o_c = res[i][chunk:2 * chunk] + intra[i]
            z_c = proj_ref[s, rows, OFF_Z + h * HEAD_DIM:OFF_Z + (h + 1) * HEAD_DIM]
            ydelta_ref[s, rows, hsl[h]] = (_rms(o_c, onw_ref[...]) * (z_c * _sigmoid(z_c))).astype(BF16)
        return carry

    lax.fori_loop(0, tt // chunk, chunk_body, 0)

    @pl.when(t == n_tiles - 1)
    def _():
        ns_ref[...] = state[...]


def _mixers(proj, pool0, conv0, s0, wpool, pscale, wconv, alog, dtb, onw, *, n_seq, sb, tt, tv, chunk, pos0, n_tiles,
            qkv_done):
    kern = functools.partial(_mixer_kernel, sb=sb, tt=tt, tv=tv, chunk=chunk, pos0=pos0, n_tiles=n_tiles,
                             qkv_done=qkv_done)
    seq_len = n_tiles * tt
    tile = lambda b, t: (b, t, 0)
    seq3 = lambda b, t: (b, 0, 0)
    seq4 = lambda b, t: (b, 0, 0, 0)
    const2 = lambda b, t: (0, 0)
    ypool, ydelta, npool, nconv, ns = pl.pallas_call(
        kern,
        grid=(n_seq // sb, n_tiles),
        in_specs=[
            pl.BlockSpec((sb, tt, MIX_COLS), tile),
            pl.BlockSpec((sb, POOL_BUF, POOL_DIM), seq3),
            pl.BlockSpec((sb, CONV_WIDTH - 1, CONV_DIM), seq3),
            pl.BlockSpec((sb, N_HEADS, HEAD_DIM, HEAD_DIM), seq4),
            pl.BlockSpec((len(POOL_WINDOWS), POOL_GROUP_DIM, POOL_GROUP_DIM), lambda b, t: (0, 0, 0)),
            pl.BlockSpec((1, POOL_DIM), const2),
            pl.BlockSpec((SUBLANES, CONV_DIM), const2),
            pl.BlockSpec((1, LANES), const2),
            pl.BlockSpec((1, LANES), const2),
            pl.BlockSpec((1, HEAD_DIM), const2),
        ],
        out_specs=[
            pl.BlockSpec((sb, tt, POOL_DIM), tile),
            pl.BlockSpec((sb, tt, QK_DIM), tile),
            pl.BlockSpec((sb, POOL_BUF, POOL_DIM), seq3),
            pl.BlockSpec((sb, CONV_WIDTH - 1, CONV_DIM), seq3),
            pl.BlockSpec((sb, N_HEADS, HEAD_DIM, HEAD_DIM), seq4),
        ],
        out_shape=[
            jax.ShapeDtypeStruct((n_seq, seq_len, POOL_DIM), BF16),
            jax.ShapeDtypeStruct((n_seq, seq_len, QK_DIM), BF16),
            jax.ShapeDtypeStruct((n_seq, POOL_BUF, POOL_DIM), F32),
            jax.ShapeDtypeStruct((n_seq, CONV_WIDTH - 1, CONV_DIM), F32),
            jax.ShapeDtypeStruct((n_seq, N_HEADS, HEAD_DIM, HEAD_DIM), F32),
        ],
        scratch_shapes=[
            pltpu.VMEM((sb, POOL_ROWS + tt, POOL_DIM), F32),
            pltpu.VMEM((sb, CONV_ROWS + tt, CONV_DIM), F32),
            pltpu.VMEM((sb, tt, QK_DIM), F32),
            pltpu.VMEM((sb, tt, QK_DIM), F32),
            pltpu.VMEM((sb, tt, QK_DIM), F32),
            pltpu.VMEM((sb, tt, LANES), F32),
            pltpu.VMEM((sb, tt, LANES), F32),
            pltpu.VMEM((sb, N_HEADS, HEAD_DIM, HEAD_DIM), F32),
        ],
        compiler_params=pltpu.CompilerParams(dimension_semantics=("arbitrary", "arbitrary"),
                                             vmem_limit_bytes=VMEM_LIMIT),
        name="mixers",
    )(proj.reshape(n_seq, seq_len, MIX_COLS), pool0, conv0, s0, wpool, pscale, wconv, alog, dtb, onw)
    n = n_seq * seq_len
    return ypool.reshape(n, POOL_DIM), ydelta.reshape(n, QK_DIM), npool, nconv, ns


def _merge_kernel(x_ref, gates_ref, ypool_ref, ydelta_ref, wbp_ref, wbd_ref, wout_ref, n2w_ref, wr_hi_ref, wr_lo_ref,
                  br_ref, xm_ref, h2_ref, route_ref, wk_ref, cnt_ref, cnt_s, logits_s, *, period, valid):
    tm = x_ref.shape[0]
    step = pl.program_id(0)

    @pl.when(step == 0)
    def _():
        cnt_s[...] = jnp.zeros_like(cnt_s)
        logits_s[...] = jnp.zeros_like(logits_s)

    logits = logits_s[...]
    bp = _dot(ypool_ref[...], wbp_ref[...])
    bd = _dot(ydelta_ref[...], wbd_ref[...])

    lane = lax.broadcasted_iota(jnp.int32, logits.shape, 1)
    neg = -jnp.inf
    far = LANES - 1
    is_g = lane < N_GROUPS
    g_max = jnp.max(jnp.where(is_g, logits, neg), axis=-1, keepdims=True)
    g_sel = jnp.min(jnp.where(is_g & (logits == g_max), lane, far), axis=-1, keepdims=True)
    p_g = 1.0 / jnp.sum(jnp.where(is_g, jnp.exp(logits - g_max), 0.0), axis=-1, keepdims=True)
    e_lane = lane - N_GROUPS
    is_e = (e_lane >= 0) & (e_lane < N_EXPERTS) & ((e_lane // EXPERTS_PER_GROUP) == g_sel)
    ev = jnp.where(is_e, logits, neg)
    v1 = jnp.max(ev, axis=-1, keepdims=True)
    i1 = jnp.min(jnp.where(is_e & (ev == v1), lane, far), axis=-1, keepdims=True)
    is_e2 = is_e & (lane != i1)
    ev2 = jnp.where(is_e2, logits, neg)
    v2 = jnp.max(ev2, axis=-1, keepdims=True)
    i2 = jnp.min(jnp.where(is_e2 & (ev2 == v2), lane, far), axis=-1, keepdims=True)
    e21 = jnp.exp(v2 - v1)
    w1 = p_g / (1.0 + e21)
    w2 = p_g * e21 / (1.0 + e21)
    e1 = i1 - N_GROUPS
    e2 = i2 - N_GROUPS
    chosen = ((lane == e1) | (lane == e2)) & (step > 0)
    if valid < period:
        row = lax.broadcasted_iota(jnp.int32, (tm, 1), 0)
        chosen = chosen & (lax.rem(row, period) < valid)
    onehot = jnp.where(chosen, 1.0, 0.0)

    rr = lax.broadcasted_iota(jnp.int32, (tm, tm), 0)
    cc = lax.broadcasted_iota(jnp.int32, (tm, tm), 1)
    earlier = jnp.where(rr > cc, 1.0, 0.0).astype(BF16)
    before = jnp.dot(earlier, onehot.astype(BF16), preferred_element_type=F32) + cnt_s[...]
    r1 = jnp.sum(jnp.where(lane == e1, before, 0.0), axis=-1, keepdims=True)
    r2 = jnp.sum(jnp.where(lane == e2, before, 0.0), axis=-1, keepdims=True)
    cnt_s[...] = cnt_s[...] + jnp.sum(onehot, axis=0, keepdims=True)
    cnt_ref[...] = cnt_s[...]
    record = jnp.where(lane == 0, e1.astype(F32), jnp.where(lane == 1, e2.astype(F32),
                                                           jnp.where(lane == 2, r1, jnp.where(lane == 3, r2, 0.0))))
    route_ref[...] = record.T[0:SUBLANES, :]
    lane8 = lax.broadcasted_iota(jnp.int32, (tm, SUBLANES), 1)
    wk_ref[...] = jnp.where(lane8 == 0, w1, jnp.where(lane8 == 1, w2, 0.0))

    merged = _sigmoid(gates_ref[:, 0:1024]) * bp + _sigmoid(gates_ref[:, 1024:2048]) * bd
    xm = x_ref[...] + _dot(merged, wout_ref[...])
    xm_ref[...] = xm
    h2 = _rms(xm, n2w_ref[...])
    h2_ref[...] = _pack_pairs(h2)
    h2_hi = h2.astype(BF16)
    h2_lo = (h2 - h2_hi.astype(F32)).astype(BF16)
    logits_s[...] = (jnp.dot(h2_hi, wr_hi_ref[...], preferred_element_type=F32)
                     + (jnp.dot(h2_hi, wr_lo_ref[...], preferred_element_type=F32)
                        + jnp.dot(h2_lo, wr_hi_ref[...], preferred_element_type=F32))) + br_ref[...]


def _merge(x, proj, ypool, ydelta, wbp, wbd, wout, n2w, wr_hi, wr_lo, br, tm, period, valid):
    n, d = x.shape
    last = n // tm - 1
    row = lambda i: (jnp.minimum(i, last), 0)
    chosen_row = lambda i: (jnp.maximum(i - 1, 0), 0)
    const = lambda i: (0, 0)
    return pl.pallas_call(
        functools.partial(_merge_kernel, period=period, valid=valid),
        grid=(n // tm + 1,),
        in_specs=[
            pl.BlockSpec((tm, d), row),
            pl.BlockSpec((tm, GATE_COLS), row),
            pl.BlockSpec((tm, POOL_DIM), row),
            pl.BlockSpec((tm, QK_DIM), row),
            pl.BlockSpec((POOL_DIM, d), const),
            pl.BlockSpec((QK_DIM, d), const),
            pl.BlockSpec((d, d), const),
            pl.BlockSpec((1, d), const),
            pl.BlockSpec((d, LANES), const),
            pl.BlockSpec((d, LANES), const),
            pl.BlockSpec((1, LANES), const),
        ],
        out_specs=[
            pl.BlockSpec((tm, d), row),
            pl.BlockSpec((tm, d // 2), row),
            pl.BlockSpec((SUBLANES, tm), lambda i: (0, jnp.maximum(i - 1, 0))),
            pl.BlockSpec((tm, SUBLANES), chosen_row),
            pl.BlockSpec((1, LANES), const),
        ],
        out_shape=[
            jax.ShapeDtypeStruct((n, d), F32),
            jax.ShapeDtypeStruct((n, d // 2), U32),
            jax.ShapeDtypeStruct((SUBLANES, n), F32),
            jax.ShapeDtypeStruct((n, SUBLANES), F32),
            jax.ShapeDtypeStruct((1, LANES), F32),
        ],
        scratch_shapes=[pltpu.VMEM((1, LANES), F32), pltpu.VMEM((tm, LANES), F32)],
        compiler_params=pltpu.CompilerParams(dimension_semantics=("arbitrary",), vmem_limit_bytes=VMEM_LIMIT),
        name="merge",
    )(x, proj, ypool, ydelta, wbp, wbd, wout, n2w, wr_hi, wr_lo, br)


def _expert_kernel(te_ref, nv_ref, x_ref, wg_hbm, wu_hbm, wd_hbm, y_ref, st_g, st_u, st_d, wg_s, wu_s, wd_s, sem, slot_s):
    i = pl.program_id(0)
    n_tiles = pl.num_programs(0)
    n_valid = nv_ref[i]
    expert = te_ref[i]

    def expert_at(j):
        return te_ref[jnp.minimum(j, n_tiles - 1)]

    def next_expert_tile(j):
        e = expert_at(j)
        return lax.while_loop(lambda t: (t < n_tiles) & (expert_at(t) == e), lambda t: t + 1, j + 1)

    def start_weight_copies(j, slot):
        @pl.when(j < n_tiles)
        def _():
            for cp in weight_copies(expert_at(j), slot):
                cp.start()

    def weight_copies(e, slot):
        return [pltpu.make_async_copy(src.at[e], dst.at[slot], sem.at[slot])
                for src, dst in ((wg_hbm, st_g), (wu_hbm, st_u), (wd_hbm, st_d))]

    @pl.when(i == 0)
    def _():
        slot_s[0] = 0
        j = i
        for slot in range(EXPERT_STAGES - 1):
            start_weight_copies(j, slot)
            j = next_expert_tile(j)

    @pl.when((i == 0) | (expert != te_ref[jnp.maximum(i - 1, 0)]))
    def _():
        slot = slot_s[0]
        j = i
        for _ in range(EXPERT_STAGES - 1):
            j = next_expert_tile(j)
        start_weight_copies(j, (slot + EXPERT_STAGES - 1) % EXPERT_STAGES)
        for cp in weight_copies(expert, slot):
            cp.wait()
        wg_s[...] = st_g[slot].astype(BF16)
        wu_s[...] = st_u[slot].astype(BF16)
        wd_s[...] = st_d[slot].astype(BF16)
        slot_s[0] = (slot + 1) % EXPERT_STAGES

    @pl.when(n_valid > 0)
    def _():
        tm = x_ref.shape[0]
        part = min(tm, EXPERT_PART_ROWS)
        parts = [slice(r0, r0 + part) for r0 in range(0, tm, part)]
        row = lax.broadcasted_iota(jnp.int32, (part, 1), 0)
        xs = []
        for p in parts:
            left, right = _unpack_pairs(jnp.where(row + p.start < n_valid, x_ref[p, :], jnp.uint32(0)))
            xs.append(jnp.concatenate([left.astype(BF16), right.astype(BF16)], axis=1))
        a = [_dot(x, wg_s[...]) for x in xs]
        b = [_dot(x, wu_s[...]) for x in xs]
        act = [(a[j] * _sigmoid(a[j])) * b[j] for j in range(len(parts))]
        y = [_dot(act[j], wd_s[...]) for j in range(len(parts))]
        for j, p in enumerate(parts):
            y_ref[p, :] = _pack_pairs(y[j])

    @pl.when(n_valid == 0)
    def _():
        y_ref[...] = jnp.zeros_like(y_ref)


def _experts(tile_expert, tile_valid, xs, wg, wu, wd, tm):
    n_rows, dp = xs.shape
    d, f = wg.shape[1], wg.shape[2]
    grid_spec = pltpu.PrefetchScalarGridSpec(
        num_scalar_prefetch=2,
        grid=(n_rows // tm,),
        in_specs=[
            pl.BlockSpec((tm, dp), lambda i, te, nv: (jnp.where(nv[i] > 0, i, 0), 0)),
            pl.BlockSpec(memory_space=pl.ANY),
            pl.BlockSpec(memory_space=pl.ANY),
            pl.BlockSpec(memory_space=pl.ANY),
        ],
        out_specs=pl.BlockSpec((tm, dp), lambda i, te, nv: (i, 0)),
        scratch_shapes=[pltpu.VMEM((EXPERT_STAGES, d, f), F32), pltpu.VMEM((EXPERT_STAGES, d, f), F32),
                        pltpu.VMEM((EXPERT_STAGES, f, d), F32),
                        pltpu.VMEM((d, f), BF16), pltpu.VMEM((d, f), BF16), pltpu.VMEM((f, d), BF16),
                        pltpu.SemaphoreType.DMA((EXPERT_STAGES,)), pltpu.SMEM((1,), jnp.int32)],
    )
    return pl.pallas_call(
        _expert_kernel,
        grid_spec=grid_spec,
        out_shape=jax.ShapeDtypeStruct((n_rows, dp), U32),
        compiler_params=pltpu.CompilerParams(dimension_semantics=("arbitrary",), vmem_limit_bytes=VMEM_LIMIT),
        name="experts",
    )(tile_expert, tile_valid, xs, wg, wu, wd)


def _plan_kernel(route_ref, cnt_ref, pos_ref, tiles_ref, *, tm_e):
    counts = cnt_ref[...]
    padded = jnp.floor((counts + (tm_e - 1)) / tm_e) * tm_e
    lane = lax.broadcasted_iota(jnp.int32, counts.shape, 1)
    ends = padded
    shift = 1
    while shift < N_EXPERTS:
        ends = ends + jnp.where(lane >= shift, pltpu.roll(ends, shift, axis=1), 0.0)
        shift *= 2
    starts = ends - padded

    def lookup(table, key):
        out = jnp.zeros_like(key)
        for e in range(N_EXPERTS):
            out = out + jnp.where(key == e, table[:, e:e + 1], 0.0)
        return out

    for k in range(2):
        pos = lookup(starts, route_ref[k:k + 1, :]) + route_ref[2 + k:3 + k, :]
        pos_ref[k:k + 1, :] = pos.astype(jnp.int32)

    tile_start = lax.broadcasted_iota(jnp.int32, (1, tiles_ref.shape[1]), 1).astype(F32) * tm_e
    tile_expert = jnp.zeros_like(tile_start)
    for e in range(N_EXPERTS):
        tile_expert = tile_expert + jnp.where(ends[:, e:e + 1] <= tile_start, 1.0, 0.0)
    tile_expert = jnp.minimum(tile_expert, N_EXPERTS - 1.0)
    valid = lookup(counts, tile_expert) - (tile_start - lookup(starts, tile_expert))
    tiles_ref[0:1, :] = tile_expert.astype(jnp.int32)
    tiles_ref[1:2, :] = jnp.clip(valid, 0.0, tm_e).astype(jnp.int32)


def _plan(route_t, counts, tm_e, n_tiles):
    n = route_t.shape[1]
    tiles_pad = -(-n_tiles // LANES) * LANES
    return pl.pallas_call(
        functools.partial(_plan_kernel, tm_e=tm_e),
        out_shape=[jax.ShapeDtypeStruct((2, n), jnp.int32), jax.ShapeDtypeStruct((2, tiles_pad), jnp.int32)],
        name="plan",
    )(route_t, counts)


def _sc_workers():
    info = plsc.get_sparse_core_info()
    return info.num_cores, info.num_subcores


def _sc_chunk(per_worker):
    assert per_worker % SUBLANES == 0
    return max(c for c in range(SUBLANES, SC_ROWS + 1, SUBLANES) if per_worker % c == 0)


def _sc_scatter2(rows, idx_a, idx_b, n_out):
    n_cores, n_sub = _sc_workers()
    n, d = rows.shape
    per_worker = n // (n_cores * n_sub)
    assert per_worker * n_cores * n_sub == n
    chunk = _sc_chunk(per_worker)
    mesh = plsc.VectorSubcoreMesh(core_axis_name="c", subcore_axis_name="s")

    n_chunks = per_worker // chunk
    buf = lambda shape, dtype: [pltpu.VMEM(shape, dtype), pltpu.VMEM(shape, dtype)]

    @functools.partial(
        pl.kernel, mesh=mesh, out_type=jax.ShapeDtypeStruct((n_out, d), rows.dtype),
        scratch_types=buf((chunk,), jnp.int32) + buf((chunk,), jnp.int32) + buf((chunk, d), rows.dtype)
        + [pltpu.SemaphoreType.DMA, pltpu.SemaphoreType.DMA])
    def scatter_rows(rows_hbm, ia_hbm, ib_hbm, out_hbm, ia0, ia1, ib0, ib1, r0, r1, sem0, sem1):
        worker = lax.axis_index("s") * n_cores + lax.axis_index("c")
        base = worker * per_worker
        slots = ((ia0, ib0, r0, sem0), (ia1, ib1, r1, sem1))
        pending = {}
        for j in range(n_chunks):
            ia_v, ib_v, rows_v, sem = slots[j % 2]
            if j >= 2:
                for cp in pending.pop(j - 2):
                    cp.wait()
            off = pl.multiple_of(base + j * chunk, SUBLANES)
            pltpu.sync_copy(ia_hbm.at[pl.ds(off, chunk)], ia_v)
            pltpu.sync_copy(ib_hbm.at[pl.ds(off, chunk)], ib_v)
            pltpu.sync_copy(rows_hbm.at[pl.ds(off, chunk)], rows_v)
            pending[j] = (pltpu.async_copy(rows_v, out_hbm.at[ia_v], sem),
                          pltpu.async_copy(rows_v, out_hbm.at[ib_v], sem))
        for j in sorted(pending):
            for cp in pending[j]:
                cp.wait()

    return scatter_rows(rows, idx_a, idx_b)


def _sc_gather(table, idx):
    n_cores, n_sub = _sc_workers()
    n_idx = idx.shape[0]
    d = table.shape[1]
    per_worker = n_idx // (n_cores * n_sub)
    assert per_worker * n_cores * n_sub == n_idx
    chunk = _sc_chunk(per_worker)
    mesh = plsc.VectorSubcoreMesh(core_axis_name="c", subcore_axis_name="s")

    n_chunks = per_worker // chunk

    @functools.partial(
        pl.kernel, mesh=mesh, out_type=jax.ShapeDtypeStruct((n_idx, d), table.dtype),
        scratch_types=[pltpu.VMEM((chunk,), jnp.int32), pltpu.VMEM((chunk,), jnp.int32),
                       pltpu.VMEM((chunk, d), table.dtype), pltpu.VMEM((chunk, d), table.dtype),
                       pltpu.SemaphoreType.DMA, pltpu.SemaphoreType.DMA])
    def gather_rows(table_hbm, idx_hbm, out_hbm, i0, i1, r0, r1, sem0, sem1):
        worker = lax.axis_index("s") * n_cores + lax.axis_index("c")
        base = worker * per_worker
        slots = ((i0, r0, sem0), (i1, r1, sem1))

        def start(j):
            idx_v, rows_v, sem = slots[j % 2]
            off = pl.multiple_of(base + j * chunk, SUBLANES)
            pltpu.sync_copy(idx_hbm.at[pl.ds(off, chunk)], idx_v)
            return pltpu.async_copy(table_hbm.at[idx_v], rows_v, sem)

        nxt = start(0)
        for j in range(n_chunks):
            cur = nxt
            if j + 1 < n_chunks:
                nxt = start(j + 1)
            cur.wait()
            off = pl.multiple_of(base + j * chunk, SUBLANES)
            pltpu.sync_copy(slots[j % 2][1], out_hbm.at[pl.ds(off, chunk)])

    return gather_rows(table, idx)


def _finalize_kernel(xm_ref, ya_ref, yb_ref, wk_ref, fnw_ref, o_ref):
    d = xm_ref.shape[1]
    c = d // 2
    w = wk_ref[...]
    a_left, a_right = _unpack_pairs(ya_ref[...])
    b_left, b_right = _unpack_pairs(yb_ref[...])
    x_left = xm_ref[:, 0:c] + (w[:, 0:1] * a_left + w[:, 1:2] * b_left)
    x_right = xm_ref[:, c:d] + (w[:, 0:1] * a_right + w[:, 1:2] * b_right)
    ms = (jnp.sum(x_left * x_left, axis=-1, keepdims=True) + jnp.sum(x_right * x_right, axis=-1, keepdims=True)) / d
    scale = lax.rsqrt(ms + EPS)
    o_ref[:, 0:c] = x_left * scale * fnw_ref[:, 0:c]
    o_ref[:, c:d] = x_right * scale * fnw_ref[:, c:d]


def _finalize(xm, yk, wk, fnw, tm):
    n, d = xm.shape
    steps = n // tm
    return pl.pallas_call(
        _finalize_kernel,
        grid=(steps,),
        in_specs=[
            pl.BlockSpec((tm, d), lambda i: (i, 0)),
            pl.BlockSpec((tm, d // 2), lambda i: (i, 0)),
            pl.BlockSpec((tm, d // 2), lambda i: (i + steps, 0)),
            pl.BlockSpec((tm, SUBLANES), lambda i: (i, 0)),
            pl.BlockSpec((1, d), lambda i: (0, 0)),
        ],
        out_specs=pl.BlockSpec((tm, d), lambda i: (i, 0)),
        out_shape=jax.ShapeDtypeStruct((n, d), F32),
        compiler_params=pltpu.CompilerParams(dimension_semantics=("arbitrary",), vmem_limit_bytes=VMEM_LIMIT),
        name="finalize",
    )(xm, yk, yk, wk, fnw)


def _moe(h2, route, counts, wg, wu, wd, *, tm_e):
    n = h2.shape[0]
    n_rows = 2 * n + N_EXPERTS * tm_e
    n_tiles = n_rows // tm_e
    pos, tiles = _plan(route, counts, tm_e, n_tiles)
    xs = _sc_scatter2(h2, pos[0], pos[1], n_rows)
    y_sorted = _experts(tiles[0, :n_tiles], tiles[1, :n_tiles], xs, wg, wu, wd, tm_e)
    return _sc_gather(y_sorted, pos.reshape(-1))


def _layer(x_tokens, pool0, conv0, s0, prm, *, n_seq, sb, tt, tv, chunk, pos0, n_tiles, tm):
    conv_in_proj = tv == tt
    if conv_in_proj:
        proj, gates, nconv_proj = _inproj_conv(x_tokens.reshape(n_seq, n_tiles * tt, -1), conv0, prm["norm1_w"],
                                               prm["w_in"], prm["w_conv"], tm)
    else:
        proj, gates = _inproj(x_tokens, prm["norm1_w"], prm["w_in"], tm)
    ypool, ydelta, npool, nconv, ns = _mixers(
        proj, pool0, conv0, s0, prm["w_pool"], prm["pool_scale"], prm["w_conv"], prm["a_log"], prm["dt_bias"],
        prm["o_norm_w"], n_seq=n_seq, sb=sb, tt=tt, tv=tv, chunk=chunk, pos0=pos0, n_tiles=n_tiles,
        qkv_done=conv_in_proj)
    if conv_in_proj:
        nconv = nconv_proj
    xm, h2, route, wk, counts = _merge(
        x_tokens, gates, ypool, ydelta, prm["w_branch_pool"], prm["w_branch_delta"], prm["w_out"], prm["norm2_w"],
        prm["w_router_hi"], prm["w_router_lo"], prm["b_router"], 2 * tm, tt, tv)
    return xm, h2, route, wk, counts, npool, nconv, ns


def _pad_lanes(v, width=LANES):
    v = v.reshape(1, -1).astype(F32)
    return jnp.pad(v, ((0, 0), (0, width - v.shape[1])))


def kernel(x_prompt, x_sample, cache_pool, cache_conv, state_delta, norm1_w, w_in, w_pool, pool_scale, w_conv, a_log, dt_bias, o_norm_w, w_branch_pool, w_branch_delta, w_out, norm2_w, w_router_group, b_router_group, w_router_expert, b_router_expert, w_gate, w_up, w_down, final_norm_w):
    n_layers = norm1_w.shape[0]
    assert n_layers == 1, "single-layer step"
    bsz, seq, d = x_prompt.shape
    dbs, dseq, _ = x_sample.shape
    lyr = 0

    wi = w_in[lyr]
    c = [POOL_DIM, POOL_DIM + CONV_DIM]
    c += [c[1] + N_HEADS, c[1] + 2 * N_HEADS]
    c += [c[3] + QK_DIM, c[3] + QK_DIM + d]
    pad8 = lambda m: jnp.pad(m, ((0, 0), (0, LANES - N_HEADS)))
    w_in_r = jnp.concatenate(
        [wi[:, c[3]:c[4]], wi[:, c[0]:c[1]], wi[:, :c[0]], pad8(wi[:, c[1]:c[2]]), pad8(wi[:, c[2]:c[3]]),
         wi[:, c[4]:c[5]], wi[:, c[5]:]], axis=1).astype(BF16)
    w_router = jnp.concatenate(
        [w_router_group[lyr], w_router_expert[lyr].reshape(d, N_EXPERTS),
         jnp.zeros((d, LANES - N_GROUPS - N_EXPERTS), F32)], axis=1).astype(F32)
    w_router_hi = w_router.astype(BF16)
    b_router = _pad_lanes(jnp.concatenate([b_router_group[lyr], b_router_expert[lyr].reshape(-1)]))
    prm = dict(
        norm1_w=norm1_w[lyr].reshape(1, d), w_in=w_in_r,
        w_pool=w_pool[lyr].astype(BF16), pool_scale=pool_scale[lyr].reshape(1, POOL_DIM),
        w_conv=jnp.pad(w_conv[lyr], ((0, SUBLANES - CONV_WIDTH), (0, 0))),
        a_log=_pad_lanes(a_log[lyr]), dt_bias=_pad_lanes(dt_bias[lyr]), o_norm_w=o_norm_w[lyr].reshape(1, HEAD_DIM),
        w_branch_pool=w_branch_pool[lyr].astype(BF16), w_branch_delta=w_branch_delta[lyr].astype(BF16),
        w_out=w_out[lyr].astype(BF16), norm2_w=norm2_w[lyr].reshape(1, d),
        w_router_hi=w_router_hi, w_router_lo=(w_router - w_router_hi.astype(F32)).astype(BF16), b_router=b_router,
    )
    wg, wu, wd = w_gate[lyr], w_up[lyr], w_down[lyr]
    fnw = final_norm_w.reshape(1, d)

    tt_p = TOKEN_TILE // 2
    xm, h2, route, wk, counts, npool, nconv, ns = _layer(
        x_prompt.reshape(bsz * seq, d),
        jnp.zeros((bsz, POOL_BUF, POOL_DIM), F32), jnp.zeros((bsz, CONV_WIDTH - 1, CONV_DIM), F32),
        jnp.zeros((bsz, N_HEADS, HEAD_DIM, HEAD_DIM), F32), prm,
        n_seq=bsz, sb=4, tt=tt_p, tv=tt_p, chunk=DELTA_CHUNK, pos0=0, n_tiles=seq // tt_p, tm=TOKEN_TILE)
    yk = _moe(h2, route, counts, wg, wu, wd, tm_e=2 * TOKEN_TILE)
    y_prompt = _finalize(xm, yk, wk, fnw, 2 * TOKEN_TILE).reshape(bsz, seq, d)
    pool_p = npool[None]
    conv_p = nconv[None]
    delta_p = ns[None]

    tt_s = SUBLANES
    xs_pad = jnp.pad(x_sample, ((0, 0), (0, tt_s - dseq), (0, 0))).reshape(dbs * tt_s, d)
    xm, h2, route, wk, counts, npool, nconv, ns = _layer(
        xs_pad, cache_pool[lyr], cache_conv[lyr], state_delta[lyr], prm,
        n_seq=dbs, sb=8, tt=tt_s, tv=dseq, chunk=tt_s, pos0=PAST_LEN, n_tiles=1, tm=TOKEN_TILE)
    real = lambda a: a.reshape(dbs, tt_s, -1)[:, :dseq].reshape(dbs * dseq, -1)
    route = route.reshape(SUBLANES, dbs, tt_s)[:, :, :dseq].reshape(SUBLANES, dbs * dseq)
    yk = _moe(real(h2), route, counts, wg, wu, wd, tm_e=TOKEN_TILE // 2)
    y_sample = _finalize(real(xm), yk, real(wk), fnw, TOKEN_TILE).reshape(dbs, dseq, d)
    pool_s = npool[None]
    conv_s = nconv[None]
    delta_s = ns[None]
    return (y_prompt, y_sample, pool_p, conv_p, delta_p, pool_s, conv_s, delta_s)
```

```python
import functools

import jax
import jax.numpy as jnp
from jax import lax
from jax.experimental import pallas as pl
from jax.experimental.pallas import tpu as pltpu
from jax.experimental.pallas import tpu_sc as plsc

F32 = jnp.float32
BF16 = jnp.bfloat16
U32 = jnp.uint32
EPS = 1e-6
HIGHEST = lax.Precision.HIGHEST

POOL_WINDOWS = (2, 4, 8, 16)
POOL_GROUP_DIM = 128
POOL_DIM = len(POOL_WINDOWS) * POOL_GROUP_DIM
POOL_BUF = max(POOL_WINDOWS) - 1
N_HEADS = 8
HEAD_DIM = 128
QK_DIM = N_HEADS * HEAD_DIM
CONV_WIDTH = 4
CONV_DIM = 3 * QK_DIM
N_GROUPS = 4
EXPERTS_PER_GROUP = 8
N_EXPERTS = N_GROUPS * EXPERTS_PER_GROUP
PAST_LEN = 16384
DELTA_CHUNK = 64
TOKEN_TILE = 256
LANES = 128
SUBLANES = 8

OFF_Z = 0
OFF_QKV = OFF_Z + QK_DIM
OFF_POOL = OFF_QKV + CONV_DIM
OFF_B = OFF_POOL + POOL_DIM
OFF_A = OFF_B + LANES
MIX_COLS = OFF_A + LANES
GATE_COLS = 2048
IN_COLS = MIX_COLS + GATE_COLS

POOL_ROWS = 16
CONV_ROWS = 8
VMEM_LIMIT = 60 * 1024 * 1024
EXPERT_PART_ROWS = 128
EXPERT_STAGES = 4
CONV_GROUPS = 12
SC_ROWS = 96


def _sigmoid(x):
    return 1.0 / (1.0 + jnp.exp(-x))


def _softplus(x):
    return jnp.maximum(x, 0.0) + jnp.log1p(jnp.exp(-jnp.abs(x)))


def _rms(x, w):
    return x * lax.rsqrt(jnp.mean(x * x, axis=-1, keepdims=True) + EPS) * w


def _dot(a, b):
    return jnp.dot(a.astype(BF16), b.astype(BF16), preferred_element_type=F32)


def _dot_nt(a, b):
    return lax.dot_general(a.astype(BF16), b.astype(BF16), (((1,), (1,)), ((), ())), preferred_element_type=F32)


def _dot_tn(a, b):
    return lax.dot_general(a.astype(BF16), b.astype(BF16), (((0,), (0,)), ((), ())), preferred_element_type=F32)


def _pack_pairs(x):
    c = x.shape[1] // 2
    hi = pltpu.bitcast(x[:, :c].astype(BF16).astype(F32), U32)
    lo = pltpu.bitcast(x[:, c:].astype(BF16).astype(F32), U32)
    return hi | lax.shift_right_logical(lo, jnp.uint32(16))


def _unpack_pairs(w):
    left = pltpu.bitcast(w & jnp.uint32(0xFFFF0000), F32)
    right = pltpu.bitcast(lax.shift_left(w, jnp.uint32(16)), F32)
    return left, right


def _inproj_kernel(x_ref, nw_ref, w_ref, mix_ref, gate_ref):
    h = _rms(x_ref[...], nw_ref[...]).astype(BF16)
    half = MIX_COLS // 2
    for c0 in (0, half):
        mix_ref[:, c0:c0 + half] = jnp.dot(h, w_ref[:, c0:c0 + half], preferred_element_type=F32)
    half = GATE_COLS // 2
    for c0 in (0, half):
        gate_ref[:, c0:c0 + half] = jnp.dot(h, w_ref[:, MIX_COLS + c0:MIX_COLS + c0 + half],
                                            preferred_element_type=F32)


def _inproj(x, norm_w, w_in_r, tm):
    n, d = x.shape
    return pl.pallas_call(
        _inproj_kernel,
        grid=(n // tm,),
        in_specs=[
            pl.BlockSpec((tm, d), lambda i: (i, 0)),
            pl.BlockSpec((1, d), lambda i: (0, 0)),
            pl.BlockSpec((d, IN_COLS), lambda i: (0, 0)),
        ],
        out_specs=[pl.BlockSpec((tm, MIX_COLS), lambda i: (i, 0)), pl.BlockSpec((tm, GATE_COLS), lambda i: (i, 0))],
        out_shape=[jax.ShapeDtypeStruct((n, MIX_COLS), F32), jax.ShapeDtypeStruct((n, GATE_COLS), F32)],
        compiler_params=pltpu.CompilerParams(dimension_semantics=("arbitrary",), vmem_limit_bytes=VMEM_LIMIT),
        name="inproj",
    )(x, norm_w, w_in_r)


def _conv_silu_qkv(ext_c, seq, tt, wconv_ref, ci, col0=0):
    base = CONV_ROWS - (CONV_WIDTH - 1)
    cs = slice(ci * LANES, (ci + 1) * LANES)
    cl = slice(ci * LANES - col0, (ci + 1) * LANES - col0)
    idx = (lambda r: (r, cl)) if seq is None else (lambda r: (seq, r, cl))
    acc = ext_c[idx(slice(base, base + tt))] * wconv_ref[0:1, cs]
    for j in range(1, CONV_WIDTH):
        acc = acc + ext_c[idx(slice(base + j, base + j + tt))] * wconv_ref[j:j + 1, cs]
    y = acc * _sigmoid(acc)
    if ci < 2 * N_HEADS:
        y = y * lax.rsqrt(jnp.sum(y * y, axis=-1, keepdims=True) + EPS)
    if ci < N_HEADS:
        y = y * (HEAD_DIM ** -0.5)
    return y


def _inproj_conv_kernel(x_ref, nw_ref, w_ref, wconv_ref, conv0_ref, mix_ref, gate_ref, nconv_ref, h_s, *ext, n_tiles):
    t = pl.program_id(1)
    tm = x_ref.shape[1]
    base = CONV_ROWS - (CONV_WIDTH - 1)
    step = CONV_DIM // len(ext)

    @pl.when(t == 0)
    def _():
        for g, buf in enumerate(ext):
            buf[base:CONV_ROWS, :] = conv0_ref[0, :, g * step:(g + 1) * step]

    h_s[...] = _rms(x_ref[0], nw_ref[...]).astype(BF16)

    def proj(c0, c1):
        return jnp.dot(h_s[...], w_ref[:, c0:c1], preferred_element_type=F32)

    def fill_z0():
        mix_ref[0, :, OFF_Z:OFF_Z + QK_DIM // 2] = proj(OFF_Z, OFF_Z + QK_DIM // 2)

    def fill_z1():
        mix_ref[0, :, OFF_Z + QK_DIM // 2:OFF_QKV] = proj(OFF_Z + QK_DIM // 2, OFF_QKV)

    def fill_pool():
        mix_ref[0, :, OFF_POOL:MIX_COLS] = proj(OFF_POOL, MIX_COLS)

    def fill_gate(i):
        q = GATE_COLS // 4
        gate_ref[0, :, i * q:(i + 1) * q] = proj(MIX_COLS + i * q, MIX_COLS + (i + 1) * q)

    fillers = [fill_z0, fill_z1, fill_pool] + [functools.partial(fill_gate, i) for i in range(4)]
    for g, buf in enumerate(ext):
        c0 = g * step
        buf[CONV_ROWS:CONV_ROWS + tm, :] = proj(OFF_QKV + c0, OFF_QKV + c0 + step)
        if fillers:
            fillers.pop(0)()
        for ci in range(c0 // LANES, (c0 + step) // LANES):
            mix_ref[0, :, OFF_QKV + ci * LANES:OFF_QKV + (ci + 1) * LANES] = _conv_silu_qkv(
                buf, None, tm, wconv_ref, ci, c0)
    for f in fillers:
        f()

    @pl.when(t == n_tiles - 1)
    def _():
        for g, buf in enumerate(ext):
            nconv_ref[0, :, g * step:(g + 1) * step] = buf[tm + base:tm + CONV_ROWS, :]

    for buf in ext:
        buf[0:CONV_ROWS, :] = buf[tm:tm + CONV_ROWS, :]


def _inproj_conv(x, conv0, norm_w, w_in_r, wconv, tm):
    n_seq, seq_len, d = x.shape
    n_tiles = seq_len // tm
    tile = lambda b, t: (b, t, 0)
    const = lambda b, t: (0, 0)
    mix, gates, nconv = pl.pallas_call(
        functools.partial(_inproj_conv_kernel, n_tiles=n_tiles),
        grid=(n_seq, n_tiles),
        in_specs=[
            pl.BlockSpec((1, tm, d), tile),
            pl.BlockSpec((1, d), const),
            pl.BlockSpec((d, IN_COLS), const),
            pl.BlockSpec((SUBLANES, CONV_DIM), const),
            pl.BlockSpec((1, CONV_WIDTH - 1, CONV_DIM), lambda b, t: (b, 0, 0)),
        ],
        out_specs=[
            pl.BlockSpec((1, tm, MIX_COLS), tile),
            pl.BlockSpec((1, tm, GATE_COLS), tile),
            pl.BlockSpec((1, CONV_WIDTH - 1, CONV_DIM), lambda b, t: (b, 0, 0)),
        ],
        out_shape=[
            jax.ShapeDtypeStruct((n_seq, seq_len, MIX_COLS), F32),
            jax.ShapeDtypeStruct((n_seq, seq_len, GATE_COLS), F32),
            jax.ShapeDtypeStruct((n_seq, CONV_WIDTH - 1, CONV_DIM), F32),
        ],
        scratch_shapes=[pltpu.VMEM((tm, d), BF16)] + [
            pltpu.VMEM((CONV_ROWS + tm, CONV_DIM // CONV_GROUPS), F32) for _ in range(CONV_GROUPS)],
        compiler_params=pltpu.CompilerParams(dimension_semantics=("arbitrary", "arbitrary"),
                                             vmem_limit_bytes=VMEM_LIMIT),
        name="inproj_conv",
    )(x, norm_w, w_in_r, wconv, conv0)
    n = n_seq * seq_len
    return mix.reshape(n, MIX_COLS), gates.reshape(n, GATE_COLS), nconv


def _mixer_kernel(proj_ref, pool0_ref, conv0_ref, s0_ref, wpool_ref, pscale_ref, wconv_ref, alog_ref, dtb_ref, onw_ref,
                  ypool_ref, ydelta_ref, npool_ref, nconv_ref, ns_ref,
                  ext_p, ext_c, q_s, k_s, v_s, g_s, b_s, state,
                  *, sb, tt, tv, chunk, pos0, n_tiles, qkv_done, caches_by_time):
    t = pl.program_id(1)
    base = CONV_ROWS - (CONV_WIDTH - 1)

    @pl.when(t == 0)
    def _():
        if caches_by_time:
            for j in range(POOL_BUF):
                ext_p[:, POOL_ROWS - POOL_BUF + j, :] = pool0_ref[j]
            for j in range(0 if qkv_done else CONV_WIDTH - 1):
                ext_c[:, base + j, :] = conv0_ref[j]
        else:
            ext_p[:, POOL_ROWS - POOL_BUF:POOL_ROWS, :] = pool0_ref[...]
            if not qkv_done:
                ext_c[:, base:CONV_ROWS, :] = conv0_ref[...]
        state[...] = s0_ref[...]

    row = lax.broadcasted_iota(jnp.int32, (tt, 1), 0)
    pos = pos0 + t * tt + row
    for s in range(sb):
        ext_p[s, POOL_ROWS:POOL_ROWS + tt, :] = proj_ref[s, :, OFF_POOL:OFF_POOL + POOL_DIM]
        if not qkv_done:
            ext_c[s, CONV_ROWS:CONV_ROWS + tt, :] = proj_ref[s, :, OFF_QKV:OFF_QKV + CONV_DIM]

        for gi, win in enumerate(POOL_WINDOWS):
            cs = slice(gi * POOL_GROUP_DIM, (gi + 1) * POOL_GROUP_DIM)
            u = ext_p[s, POOL_ROWS:POOL_ROWS + tt, cs]
            acc = u
            for j in range(1, win):
                acc = acc + ext_p[s, POOL_ROWS - j:POOL_ROWS - j + tt, cs]
            cnt = jnp.minimum(win, pos + 1).astype(F32)
            pooled = acc / cnt - u
            ypool_ref[s, :, cs] = (_dot(pooled, wpool_ref[gi]) * pscale_ref[:, cs]).astype(BF16)

        for ci in range(0 if qkv_done else CONV_DIM // LANES):
            hs = slice((ci % N_HEADS) * HEAD_DIM, (ci % N_HEADS + 1) * HEAD_DIM)
            (q_s, k_s, v_s)[ci // N_HEADS][s, :, hs] = _conv_silu_qkv(ext_c, s, tt, wconv_ref, ci)

        beta = _sigmoid(proj_ref[s, :, OFF_B:OFF_B + LANES])
        g = -jnp.exp(alog_ref[...]) * _softplus(proj_ref[s, :, OFF_A:OFF_A + LANES] + dtb_ref[...])
        if tv < tt:
            beta = jnp.where(row < tv, beta, 0.0)
            g = jnp.where(row < tv, g, 0.0)
        b_s[s] = beta
        g_s[s] = g

    @pl.when(t == n_tiles - 1)
    def _():
        if qkv_done:
            nconv_ref[...] = jnp.zeros_like(nconv_ref)
        if caches_by_time:
            for j in range(POOL_BUF):
                npool_ref[j] = ext_p[:, tv + POOL_ROWS - POOL_BUF + j, :]
            for j in range(0 if qkv_done else CONV_WIDTH - 1):
                nconv_ref[j] = ext_c[:, tv + base + j, :]
        else:
            npool_ref[...] = ext_p[:, tv + POOL_ROWS - POOL_BUF:tv + POOL_ROWS, :]
            if not qkv_done:
                nconv_ref[...] = ext_c[:, tv + base:tv + CONV_ROWS, :]

    if n_tiles > 1:
        ext_p[:, 0:POOL_ROWS, :] = ext_p[:, tt:tt + POOL_ROWS, :]
        if not qkv_done:
            ext_c[:, 0:CONV_ROWS, :] = ext_c[:, tt:tt + CONV_ROWS, :]

    ri = lax.broadcasted_iota(jnp.int32, (chunk, chunk), 0)
    cj = lax.broadcasted_iota(jnp.int32, (chunk, chunk), 1)
    causal = ri >= cj
    strict = ri > cj
    ltri = causal.astype(F32)
    eye = (ri == cj).astype(F32)
    n_doublings = chunk.bit_length() - 2
    units = [(s, h) for s in range(sb) for h in range(N_HEADS)]
    hsl = [slice(h * HEAD_DIM, (h + 1) * HEAD_DIM) for h in range(N_HEADS)]

    def qkv(which, s, rows, h):
        if qkv_done:
            c0 = OFF_QKV + which * QK_DIM + h * HEAD_DIM
            return proj_ref[s, rows, c0:c0 + HEAD_DIM]
        return (q_s, k_s, v_s)[which][s, rows, hsl[h]]

    def chunk_body(ci, carry):
        r0 = pl.multiple_of(ci * chunk, chunk)
        rows = pl.ds(r0, chunk)
        g_all, g_t, b_all = [], [], []
        for s in range(sb):
            ga = jnp.dot(ltri, g_s[s, rows, :], precision=HIGHEST, preferred_element_type=F32)
            g_all.append(ga)
            if chunk < LANES:
                ga = jnp.concatenate([ga, jnp.zeros((LANES - chunk, LANES), F32)], axis=0)
            g_t.append(ga.T)
            b_all.append(b_s[s, rows, :])
        g_col = [g_all[s][:, h:h + 1] for s, h in units]
        dec = [jnp.exp(jnp.minimum(g_col[i] - g_t[s][h:h + 1, 0:chunk], 0.0)) for i, (s, h) in enumerate(units)]
        b_col = [b_all[s][:, h:h + 1] for s, h in units]
        q = [qkv(0, s, rows, h) for s, h in units]
        k = [qkv(1, s, rows, h) for s, h in units]
        idx = range(len(units))
        kb = [k[i] * b_col[i] for i in idx]
        kbk = [_dot_nt(kb[i], k[i]) for i in idx]
        qk = [_dot_nt(q[i], k[i]) for i in idx]
        p_mat = [jnp.where(strict, kbk[i] * dec[i], 0.0) for i in idx]
        t_inv = [eye - p_mat[i] for i in idx]
        for _ in range(n_doublings):
            p_mat = [_dot(p_mat[i], p_mat[i]) for i in idx]
            xp = [_dot(t_inv[i], p_mat[i]) for i in idx]
            t_inv = [t_inv[i] + xp[i] for i in idx]
        g_exp = [jnp.exp(g_col[i]) for i in idx]
        wu = [_dot(t_inv[i], jnp.concatenate([kb[i] * g_exp[i], qkv(2, s, rows, h) * b_col[i]], axis=1))
              for i, (s, h) in enumerate(units)]
        s_old = [state[s, h] for s, h in units]
        res = [_dot(jnp.concatenate([wu[i][:, 0:HEAD_DIM], q[i] * g_exp[i]], axis=0), s_old[i]) for i in idx]
        v_new = [wu[i][:, HEAD_DIM:2 * HEAD_DIM] - res[i][0:chunk] for i in idx]
        a_qk = [jnp.where(causal, qk[i] * dec[i], 0.0) for i in idx]
        g_last = [g_all[s][chunk - 1:chunk, h:h + 1] for s, h in units]
        k_dec = [k[i] * jnp.exp(g_last[i] - g_col[i]) for i in idx]
        intra = [_dot(a_qk[i], v_new[i]) for i in idx]
        upd = [_dot_tn(k_dec[i], v_new[i]) for i in idx]
        for i, (s, h) in enumerate(units):
            state[s, h] = s_old[i] * jnp.exp(g_last[i]) + upd[i]
            o_c = res[i][chunk:2 * chunk] + intra[i]
            z_c = proj_ref[s, rows, OFF_Z + h * HEAD_DIM:OFF_Z + (h + 1) * HEAD_DIM]
            ydelta_ref[s, rows, hsl[h]] = (_rms(o_c, onw_ref[...]) * (z_c * _sigmoid(z_c))).astype(BF16)
        return carry

    lax.fori_loop(0, tt // chunk, chunk_body, 0)

    @pl.when(t == n_tiles - 1)
    def _():
        ns_ref[...] = state[...]


def _mixers(proj, pool0, conv0, s0, wpool, pscale, wconv, alog, dtb, onw, *, n_seq, sb, tt, tv, chunk, pos0, n_tiles,
            qkv_done):
    caches_by_time = sb % SUBLANES == 0
    kern = functools.partial(_mixer_kernel, sb=sb, tt=tt, tv=tv, chunk=chunk, pos0=pos0, n_tiles=n_tiles,
                             qkv_done=qkv_done, caches_by_time=caches_by_time)
    seq_len = n_tiles * tt
    tile = lambda b, t: (b, t, 0)
    seq3 = lambda b, t: (b, 0, 0)
    if caches_by_time:
        pool0, conv0 = jnp.swapaxes(pool0, 0, 1), jnp.swapaxes(conv0, 0, 1)
        cache_block = lambda rows, cols: pl.BlockSpec((rows, sb, cols), lambda b, t: (0, b, 0))
        cache_shape = lambda rows, cols: jax.ShapeDtypeStruct((rows, n_seq, cols), F32)
    else:
        cache_block = lambda rows, cols: pl.BlockSpec((sb, rows, cols), seq3)
        cache_shape = lambda rows, cols: jax.ShapeDtypeStruct((n_seq, rows, cols), F32)
    seq4 = lambda b, t: (b, 0, 0, 0)
    const2 = lambda b, t: (0, 0)
    ypool, ydelta, npool, nconv, ns = pl.pallas_call(
        kern,
        grid=(n_seq // sb, n_tiles),
        in_specs=[
            pl.BlockSpec((sb, tt, MIX_COLS), tile),
            cache_block(POOL_BUF, POOL_DIM),
            cache_block(CONV_WIDTH - 1, CONV_DIM),
            pl.BlockSpec((sb, N_HEADS, HEAD_DIM, HEAD_DIM), seq4),
            pl.BlockSpec((len(POOL_WINDOWS), POOL_GROUP_DIM, POOL_GROUP_DIM), lambda b, t: (0, 0, 0)),
            pl.BlockSpec((1, POOL_DIM), const2),
            pl.BlockSpec((SUBLANES, CONV_DIM), const2),
            pl.BlockSpec((1, LANES), const2),
            pl.BlockSpec((1, LANES), const2),
            pl.BlockSpec((1, HEAD_DIM), const2),
        ],
        out_specs=[
            pl.BlockSpec((sb, tt, POOL_DIM), tile),
            pl.BlockSpec((sb, tt, QK_DIM), tile),
            cache_block(POOL_BUF, POOL_DIM),
            cache_block(CONV_WIDTH - 1, CONV_DIM),
            pl.BlockSpec((sb, N_HEADS, HEAD_DIM, HEAD_DIM), seq4),
        ],
        out_shape=[
            jax.ShapeDtypeStruct((n_seq, seq_len, POOL_DIM), BF16),
            jax.ShapeDtypeStruct((n_seq, seq_len, QK_DIM), BF16),
            cache_shape(POOL_BUF, POOL_DIM),
            cache_shape(CONV_WIDTH - 1, CONV_DIM),
            jax.ShapeDtypeStruct((n_seq, N_HEADS, HEAD_DIM, HEAD_DIM), F32),
        ],
        scratch_shapes=[
            pltpu.VMEM((sb, POOL_ROWS + tt, POOL_DIM), F32),
            pltpu.VMEM((sb, CONV_ROWS + tt, CONV_DIM), F32),
            pltpu.VMEM((sb, tt, QK_DIM), F32),
            pltpu.VMEM((sb, tt, QK_DIM), F32),
            pltpu.VMEM((sb, tt, QK_DIM), F32),
            pltpu.VMEM((sb, tt, LANES), F32),
            pltpu.VMEM((sb, tt, LANES), F32),
            pltpu.VMEM((sb, N_HEADS, HEAD_DIM, HEAD_DIM), F32),
        ],
        compiler_params=pltpu.CompilerParams(dimension_semantics=("arbitrary", "arbitrary"),
                                             vmem_limit_bytes=VMEM_LIMIT),
        name="mixers",
    )(proj.reshape(n_seq, seq_len, MIX_COLS), pool0, conv0, s0, wpool, pscale, wconv, alog, dtb, onw)
    n = n_seq * seq_len
    if caches_by_time:
        npool, nconv = jnp.swapaxes(npool, 0, 1), jnp.swapaxes(nconv, 0, 1)
    return ypool.reshape(n, POOL_DIM), ydelta.reshape(n, QK_DIM), npool, nconv, ns


def _merge_kernel(x_ref, gates_ref, ypool_ref, ydelta_ref, wbp_ref, wbd_ref, wout_ref, n2w_ref, wr_hi_ref, wr_lo_ref,
                  br_ref, xm_ref, h2_ref, route_ref, wk_ref, cnt_ref, cnt_s, logits_s, *, period, valid):
    tm = x_ref.shape[0]
    step = pl.program_id(0)

    @pl.when(step == 0)
    def _():
        cnt_s[...] = jnp.zeros_like(cnt_s)
        logits_s[...] = jnp.zeros_like(logits_s)

    logits = logits_s[...]
    bp = _dot(ypool_ref[...], wbp_ref[...])
    bd = _dot(ydelta_ref[...], wbd_ref[...])

    lane = lax.broadcasted_iota(jnp.int32, logits.shape, 1)
    neg = -jnp.inf
    far = LANES - 1
    is_g = lane < N_GROUPS
    g_max = jnp.max(jnp.where(is_g, logits, neg), axis=-1, keepdims=True)
    g_sel = jnp.min(jnp.where(is_g & (logits == g_max), lane, far), axis=-1, keepdims=True)
    p_g = 1.0 / jnp.sum(jnp.where(is_g, jnp.exp(logits - g_max), 0.0), axis=-1, keepdims=True)
    e_lane = lane - N_GROUPS
    is_e = (e_lane >= 0) & (e_lane < N_EXPERTS) & ((e_lane // EXPERTS_PER_GROUP) == g_sel)
    ev = jnp.where(is_e, logits, neg)
    v1 = jnp.max(ev, axis=-1, keepdims=True)
    i1 = jnp.min(jnp.where(is_e & (ev == v1), lane, far), axis=-1, keepdims=True)
    is_e2 = is_e & (lane != i1)
    ev2 = jnp.where(is_e2, logits, neg)
    v2 = jnp.max(ev2, axis=-1, keepdims=True)
    i2 = jnp.min(jnp.where(is_e2 & (ev2 == v2), lane, far), axis=-1, keepdims=True)
    e21 = jnp.exp(v2 - v1)
    w1 = p_g / (1.0 + e21)
    w2 = p_g * e21 / (1.0 + e21)
    e1 = i1 - N_GROUPS
    e2 = i2 - N_GROUPS
    chosen = ((lane == e1) | (lane == e2)) & (step > 0)
    if valid < period:
        row = lax.broadcasted_iota(jnp.int32, (tm, 1), 0)
        chosen = chosen & (lax.rem(row, period) < valid)
    onehot = jnp.where(chosen, 1.0, 0.0)

    rr = lax.broadcasted_iota(jnp.int32, (tm, tm), 0)
    cc = lax.broadcasted_iota(jnp.int32, (tm, tm), 1)
    earlier = jnp.where(rr > cc, 1.0, 0.0).astype(BF16)
    before = jnp.dot(earlier, onehot.astype(BF16), preferred_element_type=F32) + cnt_s[...]
    r1 = jnp.sum(jnp.where(lane == e1, before, 0.0), axis=-1, keepdims=True)
    r2 = jnp.sum(jnp.where(lane == e2, before, 0.0), axis=-1, keepdims=True)
    cnt_s[...] = cnt_s[...] + jnp.sum(onehot, axis=0, keepdims=True)
    cnt_ref[...] = cnt_s[...]
    record = jnp.where(lane == 0, e1.astype(F32), jnp.where(lane == 1, e2.astype(F32),
                                                           jnp.where(lane == 2, r1, jnp.where(lane == 3, r2, 0.0))))
    route_ref[...] = record.T[0:SUBLANES, :]
    lane8 = lax.broadcasted_iota(jnp.int32, (tm, SUBLANES), 1)
    wk_ref[...] = jnp.where(lane8 == 0, w1, jnp.where(lane8 == 1, w2, 0.0))

    merged = _sigmoid(gates_ref[:, 0:1024]) * bp + _sigmoid(gates_ref[:, 1024:2048]) * bd
    xm = x_ref[...] + _dot(merged, wout_ref[...])
    xm_ref[...] = xm
    h2 = _rms(xm, n2w_ref[...])
    h2_ref[...] = _pack_pairs(h2)
    h2_hi = h2.astype(BF16)
    h2_lo = (h2 - h2_hi.astype(F32)).astype(BF16)
    logits_s[...] = (jnp.dot(h2_hi, wr_hi_ref[...], preferred_element_type=F32)
                     + (jnp.dot(h2_hi, wr_lo_ref[...], preferred_element_type=F32)
                        + jnp.dot(h2_lo, wr_hi_ref[...], preferred_element_type=F32))) + br_ref[...]


def _merge(x, proj, ypool, ydelta, wbp, wbd, wout, n2w, wr_hi, wr_lo, br, tm, period, valid):
    n, d = x.shape
    last = n // tm - 1
    row = lambda i: (jnp.minimum(i, last), 0)
    chosen_row = lambda i: (jnp.maximum(i - 1, 0), 0)
    const = lambda i: (0, 0)
    return pl.pallas_call(
        functools.partial(_merge_kernel, period=period, valid=valid),
        grid=(n // tm + 1,),
        in_specs=[
            pl.BlockSpec((tm, d), row),
            pl.BlockSpec((tm, GATE_COLS), row),
            pl.BlockSpec((tm, POOL_DIM), row),
            pl.BlockSpec((tm, QK_DIM), row),
            pl.BlockSpec((POOL_DIM, d), const),
            pl.BlockSpec((QK_DIM, d), const),
            pl.BlockSpec((d, d), const),
            pl.BlockSpec((1, d), const),
            pl.BlockSpec((d, LANES), const),
            pl.BlockSpec((d, LANES), const),
            pl.BlockSpec((1, LANES), const),
        ],
        out_specs=[
            pl.BlockSpec((tm, d), row),
            pl.BlockSpec((tm, d // 2), row),
            pl.BlockSpec((SUBLANES, tm), lambda i: (0, jnp.maximum(i - 1, 0))),
            pl.BlockSpec((tm, SUBLANES), chosen_row),
            pl.BlockSpec((1, LANES), const),
        ],
        out_shape=[
            jax.ShapeDtypeStruct((n, d), F32),
            jax.ShapeDtypeStruct((n, d // 2), U32),
            jax.ShapeDtypeStruct((SUBLANES, n), F32),
            jax.ShapeDtypeStruct((n, SUBLANES), F32),
            jax.ShapeDtypeStruct((1, LANES), F32),
        ],
        scratch_shapes=[pltpu.VMEM((1, LANES), F32), pltpu.VMEM((tm, LANES), F32)],
        compiler_params=pltpu.CompilerParams(dimension_semantics=("arbitrary",), vmem_limit_bytes=VMEM_LIMIT),
        name="merge",
    )(x, proj, ypool, ydelta, wbp, wbd, wout, n2w, wr_hi, wr_lo, br)


def _expert_kernel(te_ref, nv_ref, x_ref, wg_hbm, wu_hbm, wd_hbm, y_ref, st_g, st_u, st_d, wg_s, wu_s, wd_s, sem, slot_s):
    i = pl.program_id(0)
    n_tiles = pl.num_programs(0)
    n_valid = nv_ref[i]
    expert = te_ref[i]

    def expert_at(j):
        return te_ref[jnp.minimum(j, n_tiles - 1)]

    def next_expert_tile(j):
        e = expert_at(j)
        return lax.while_loop(lambda t: (t < n_tiles) & (expert_at(t) == e), lambda t: t + 1, j + 1)

    def start_weight_copies(j, slot):
        @pl.when(j < n_tiles)
        def _():
            for cp in weight_copies(expert_at(j), slot):
                cp.start()

    def weight_copies(e, slot):
        return [pltpu.make_async_copy(src.at[e], dst.at[slot], sem.at[slot])
                for src, dst in ((wg_hbm, st_g), (wu_hbm, st_u), (wd_hbm, st_d))]

    @pl.when(i == 0)
    def _():
        slot_s[0] = 0
        j = i
        for slot in range(EXPERT_STAGES - 1):
            start_weight_copies(j, slot)
            j = next_expert_tile(j)

    @pl.when((i == 0) | (expert != te_ref[jnp.maximum(i - 1, 0)]))
    def _():
        slot = slot_s[0]
        j = i
        for _ in range(EXPERT_STAGES - 1):
            j = next_expert_tile(j)
        start_weight_copies(j, (slot + EXPERT_STAGES - 1) % EXPERT_STAGES)
        for cp in weight_copies(expert, slot):
            cp.wait()
        wg_s[...] = st_g[slot].astype(BF16)
        wu_s[...] = st_u[slot].astype(BF16)
        wd_s[...] = st_d[slot].astype(BF16)
        slot_s[0] = (slot + 1) % EXPERT_STAGES

    @pl.when(n_valid > 0)
    def _():
        tm = x_ref.shape[0]
        part = min(tm, EXPERT_PART_ROWS)
        parts = [slice(r0, r0 + part) for r0 in range(0, tm, part)]
        row = lax.broadcasted_iota(jnp.int32, (part, 1), 0)
        xs = []
        for p in parts:
            left, right = _unpack_pairs(jnp.where(row + p.start < n_valid, x_ref[p, :], jnp.uint32(0)))
            xs.append(jnp.concatenate([left.astype(BF16), right.astype(BF16)], axis=1))
        a = [_dot(x, wg_s[...]) for x in xs]
        b = [_dot(x, wu_s[...]) for x in xs]
        act = [(a[j] * _sigmoid(a[j])) * b[j] for j in range(len(parts))]
        y = [_dot(act[j], wd_s[...]) for j in range(len(parts))]
        for j, p in enumerate(parts):
            y_ref[p, :] = _pack_pairs(y[j])

    @pl.when(n_valid == 0)
    def _():
        y_ref[...] = jnp.zeros_like(y_ref)


def _experts(tile_expert, tile_valid, xs, wg, wu, wd, tm):
    n_rows, dp = xs.shape
    d, f = wg.shape[1], wg.shape[2]
    grid_spec = pltpu.PrefetchScalarGridSpec(
        num_scalar_prefetch=2,
        grid=(n_rows // tm,),
        in_specs=[
            pl.BlockSpec((tm, dp), lambda i, te, nv: (jnp.where(nv[i] > 0, i, 0), 0)),
            pl.BlockSpec(memory_space=pl.ANY),
            pl.BlockSpec(memory_space=pl.ANY),
            pl.BlockSpec(memory_space=pl.ANY),
        ],
        out_specs=pl.BlockSpec((tm, dp), lambda i, te, nv: (i, 0)),
        scratch_shapes=[pltpu.VMEM((EXPERT_STAGES, d, f), F32), pltpu.VMEM((EXPERT_STAGES, d, f), F32),
                        pltpu.VMEM((EXPERT_STAGES, f, d), F32),
                        pltpu.VMEM((d, f), BF16), pltpu.VMEM((d, f), BF16), pltpu.VMEM((f, d), BF16),
                        pltpu.SemaphoreType.DMA((EXPERT_STAGES,)), pltpu.SMEM((1,), jnp.int32)],
    )
    return pl.pallas_call(
        _expert_kernel,
        grid_spec=grid_spec,
        out_shape=jax.ShapeDtypeStruct((n_rows, dp), U32),
        compiler_params=pltpu.CompilerParams(dimension_semantics=("arbitrary",), vmem_limit_bytes=VMEM_LIMIT),
        name="experts",
    )(tile_expert, tile_valid, xs, wg, wu, wd)


def _plan_kernel(route_ref, cnt_ref, pos_ref, tiles_ref, *, tm_e):
    counts = cnt_ref[...]
    padded = jnp.floor((counts + (tm_e - 1)) / tm_e) * tm_e
    lane = lax.broadcasted_iota(jnp.int32, counts.shape, 1)
    ends = padded
    shift = 1
    while shift < N_EXPERTS:
        ends = ends + jnp.where(lane >= shift, pltpu.roll(ends, shift, axis=1), 0.0)
        shift *= 2
    starts = ends - padded

    def lookup(table, key):
        out = jnp.zeros_like(key)
        for e in range(N_EXPERTS):
            out = out + jnp.where(key == e, table[:, e:e + 1], 0.0)
        return out

    for k in range(2):
        pos = lookup(starts, route_ref[k:k + 1, :]) + route_ref[2 + k:3 + k, :]
        pos_ref[k:k + 1, :] = pos.astype(jnp.int32)

    tile_start = lax.broadcasted_iota(jnp.int32, (1, tiles_ref.shape[1]), 1).astype(F32) * tm_e
    tile_expert = jnp.zeros_like(tile_start)
    for e in range(N_EXPERTS):
        tile_expert = tile_expert + jnp.where(ends[:, e:e + 1] <= tile_start, 1.0, 0.0)
    tile_expert = jnp.minimum(tile_expert, N_EXPERTS - 1.0)
    valid = lookup(counts, tile_expert) - (tile_start - lookup(starts, tile_expert))
    tiles_ref[0:1, :] = tile_expert.astype(jnp.int32)
    tiles_ref[1:2, :] = jnp.clip(valid, 0.0, tm_e).astype(jnp.int32)


def _plan(route_t, counts, tm_e, n_tiles):
    n = route_t.shape[1]
    tiles_pad = -(-n_tiles // LANES) * LANES
    return pl.pallas_call(
        functools.partial(_plan_kernel, tm_e=tm_e),
        out_shape=[jax.ShapeDtypeStruct((2, n), jnp.int32), jax.ShapeDtypeStruct((2, tiles_pad), jnp.int32)],
        name="plan",
    )(route_t, counts)


def _sc_workers():
    info = plsc.get_sparse_core_info()
    return info.num_cores, info.num_subcores


def _sc_chunk(per_worker):
    assert per_worker % SUBLANES == 0
    return max(c for c in range(SUBLANES, SC_ROWS + 1, SUBLANES) if per_worker % c == 0)


def _sc_scatter2(rows, idx_a, idx_b, n_out):
    n_cores, n_sub = _sc_workers()
    n, d = rows.shape
    per_worker = n // (n_cores * n_sub)
    assert per_worker * n_cores * n_sub == n
    chunk = _sc_chunk(per_worker)
    mesh = plsc.VectorSubcoreMesh(core_axis_name="c", subcore_axis_name="s")

    n_chunks = per_worker // chunk
    buf = lambda shape, dtype: [pltpu.VMEM(shape, dtype), pltpu.VMEM(shape, dtype)]

    @functools.partial(
        pl.kernel, mesh=mesh, out_type=jax.ShapeDtypeStruct((n_out, d), rows.dtype),
        scratch_types=buf((chunk,), jnp.int32) + buf((chunk,), jnp.int32) + buf((chunk, d), rows.dtype)
        + [pltpu.SemaphoreType.DMA, pltpu.SemaphoreType.DMA])
    def scatter_rows(rows_hbm, ia_hbm, ib_hbm, out_hbm, ia0, ia1, ib0, ib1, r0, r1, sem0, sem1):
        worker = lax.axis_index("s") * n_cores + lax.axis_index("c")
        base = worker * per_worker
        slots = ((ia0, ib0, r0, sem0), (ia1, ib1, r1, sem1))
        pending = {}
        for j in range(n_chunks):
            ia_v, ib_v, rows_v, sem = slots[j % 2]
            if j >= 2:
                for cp in pending.pop(j - 2):
                    cp.wait()
            off = pl.multiple_of(base + j * chunk, SUBLANES)
            pltpu.sync_copy(ia_hbm.at[pl.ds(off, chunk)], ia_v)
            pltpu.sync_copy(ib_hbm.at[pl.ds(off, chunk)], ib_v)
            pltpu.sync_copy(rows_hbm.at[pl.ds(off, chunk)], rows_v)
            pending[j] = (pltpu.async_copy(rows_v, out_hbm.at[ia_v], sem),
                          pltpu.async_copy(rows_v, out_hbm.at[ib_v], sem))
        for j in sorted(pending):
            for cp in pending[j]:
                cp.wait()

    return scatter_rows(rows, idx_a, idx_b)


def _sc_gather(table, idx):
    n_cores, n_sub = _sc_workers()
    n_idx = idx.shape[0]
    d = table.shape[1]
    per_worker = n_idx // (n_cores * n_sub)
    assert per_worker * n_cores * n_sub == n_idx
    chunk = _sc_chunk(per_worker)
    mesh = plsc.VectorSubcoreMesh(core_axis_name="c", subcore_axis_name="s")

    n_chunks = per_worker // chunk

    @functools.partial(
        pl.kernel, mesh=mesh, out_type=jax.ShapeDtypeStruct((n_idx, d), table.dtype),
        scratch_types=[pltpu.VMEM((chunk,), jnp.int32), pltpu.VMEM((chunk,), jnp.int32),
                       pltpu.VMEM((chunk, d), table.dtype), pltpu.VMEM((chunk, d), table.dtype),
                       pltpu.SemaphoreType.DMA, pltpu.SemaphoreType.DMA])
    def gather_rows(table_hbm, idx_hbm, out_hbm, i0, i1, r0, r1, sem0, sem1):
        worker = lax.axis_index("s") * n_cores + lax.axis_index("c")
        base = worker * per_worker
        slots = ((i0, r0, sem0), (i1, r1, sem1))

        def start(j):
            idx_v, rows_v, sem = slots[j % 2]
            off = pl.multiple_of(base + j * chunk, SUBLANES)
            pltpu.sync_copy(idx_hbm.at[pl.ds(off, chunk)], idx_v)
            return pltpu.async_copy(table_hbm.at[idx_v], rows_v, sem)

        nxt = start(0)
        for j in range(n_chunks):
            cur = nxt
            if j + 1 < n_chunks:
                nxt = start(j + 1)
            cur.wait()
            off = pl.multiple_of(base + j * chunk, SUBLANES)
            pltpu.sync_copy(slots[j % 2][1], out_hbm.at[pl.ds(off, chunk)])

    return gather_rows(table, idx)


def _finalize_kernel(xm_ref, ya_ref, yb_ref, wk_ref, fnw_ref, o_ref):
    d = xm_ref.shape[1]
    c = d // 2
    w = wk_ref[...]
    a_left, a_right = _unpack_pairs(ya_ref[...])
    b_left, b_right = _unpack_pairs(yb_ref[...])
    x_left = xm_ref[:, 0:c] + (w[:, 0:1] * a_left + w[:, 1:2] * b_left)
    x_right = xm_ref[:, c:d] + (w[:, 0:1] * a_right + w[:, 1:2] * b_right)
    ms = (jnp.sum(x_left * x_left, axis=-1, keepdims=True) + jnp.sum(x_right * x_right, axis=-1, keepdims=True)) / d
    scale = lax.rsqrt(ms + EPS)
    o_ref[:, 0:c] = x_left * scale * fnw_ref[:, 0:c]
    o_ref[:, c:d] = x_right * scale * fnw_ref[:, c:d]


def _finalize(xm, yk, wk, fnw, tm):
    n, d = xm.shape
    steps = n // tm
    return pl.pallas_call(
        _finalize_kernel,
        grid=(steps,),
        in_specs=[
            pl.BlockSpec((tm, d), lambda i: (i, 0)),
            pl.BlockSpec((tm, d // 2), lambda i: (i, 0)),
            pl.BlockSpec((tm, d // 2), lambda i: (i + steps, 0)),
            pl.BlockSpec((tm, SUBLANES), lambda i: (i, 0)),
            pl.BlockSpec((1, d), lambda i: (0, 0)),
        ],
        out_specs=pl.BlockSpec((tm, d), lambda i: (i, 0)),
        out_shape=jax.ShapeDtypeStruct((n, d), F32),
        compiler_params=pltpu.CompilerParams(dimension_semantics=("arbitrary",), vmem_limit_bytes=VMEM_LIMIT),
        name="finalize",
    )(xm, yk, yk, wk, fnw)


def _moe(h2, route, counts, wg, wu, wd, *, tm_e):
    n = h2.shape[0]
    n_rows = 2 * n + N_EXPERTS * tm_e
    n_tiles = n_rows // tm_e
    pos, tiles = _plan(route, counts, tm_e, n_tiles)
    xs = _sc_scatter2(h2, pos[0], pos[1], n_rows)
    y_sorted = _experts(tiles[0, :n_tiles], tiles[1, :n_tiles], xs, wg, wu, wd, tm_e)
    return _sc_gather(y_sorted, pos.reshape(-1))


def _layer(x_tokens, pool0, conv0, s0, prm, *, n_seq, sb, tt, tv, chunk, pos0, n_tiles, tm):
    conv_in_proj = tv == tt
    if conv_in_proj:
        proj, gates, nconv_proj = _inproj_conv(x_tokens.reshape(n_seq, n_tiles * tt, -1), conv0, prm["norm1_w"],
                                               prm["w_in"], prm["w_conv"], tm)
    else:
        proj, gates = _inproj(x_tokens, prm["norm1_w"], prm["w_in"], tm)
    ypool, ydelta, npool, nconv, ns = _mixers(
        proj, pool0, conv0, s0, prm["w_pool"], prm["pool_scale"], prm["w_conv"], prm["a_log"], prm["dt_bias"],
        prm["o_norm_w"], n_seq=n_seq, sb=sb, tt=tt, tv=tv, chunk=chunk, pos0=pos0, n_tiles=n_tiles,
        qkv_done=conv_in_proj)
    if conv_in_proj:
        nconv = nconv_proj
    xm, h2, route, wk, counts = _merge(
        x_tokens, gates, ypool, ydelta, prm["w_branch_pool"], prm["w_branch_delta"], prm["w_out"], prm["norm2_w"],
        prm["w_router_hi"], prm["w_router_lo"], prm["b_router"], 2 * tm, tt, tv)
    return xm, h2, route, wk, counts, npool, nconv, ns


def _pad_lanes(v, width=LANES):
    v = v.reshape(1, -1).astype(F32)
    return jnp.pad(v, ((0, 0), (0, width - v.shape[1])))


def kernel(x_prompt, x_sample, cache_pool, cache_conv, state_delta, norm1_w, w_in, w_pool, pool_scale, w_conv, a_log, dt_bias, o_norm_w, w_branch_pool, w_branch_delta, w_out, norm2_w, w_router_group, b_router_group, w_router_expert, b_router_expert, w_gate, w_up, w_down, final_norm_w):
    n_layers = norm1_w.shape[0]
    assert n_layers == 1, "single-layer step"
    bsz, seq, d = x_prompt.shape
    dbs, dseq, _ = x_sample.shape
    lyr = 0

    wi = w_in[lyr]
    c = [POOL_DIM, POOL_DIM + CONV_DIM]
    c += [c[1] + N_HEADS, c[1] + 2 * N_HEADS]
    c += [c[3] + QK_DIM, c[3] + QK_DIM + d]
    pad8 = lambda m: jnp.pad(m, ((0, 0), (0, LANES - N_HEADS)))
    w_in_r = jnp.concatenate(
        [wi[:, c[3]:c[4]], wi[:, c[0]:c[1]], wi[:, :c[0]], pad8(wi[:, c[1]:c[2]]), pad8(wi[:, c[2]:c[3]]),
         wi[:, c[4]:c[5]], wi[:, c[5]:]], axis=1).astype(BF16)
    w_router = jnp.concatenate(
        [w_router_group[lyr], w_router_expert[lyr].reshape(d, N_EXPERTS),
         jnp.zeros((d, LANES - N_GROUPS - N_EXPERTS), F32)], axis=1).astype(F32)
    w_router_hi = w_router.astype(BF16)
    b_router = _pad_lanes(jnp.concatenate([b_router_group[lyr], b_router_expert[lyr].reshape(-1)]))
    prm = dict(
        norm1_w=norm1_w[lyr].reshape(1, d), w_in=w_in_r,
        w_pool=w_pool[lyr].astype(BF16), pool_scale=pool_scale[lyr].reshape(1, POOL_DIM),
        w_conv=jnp.pad(w_conv[lyr], ((0, SUBLANES - CONV_WIDTH), (0, 0))),
        a_log=_pad_lanes(a_log[lyr]), dt_bias=_pad_lanes(dt_bias[lyr]), o_norm_w=o_norm_w[lyr].reshape(1, HEAD_DIM),
        w_branch_pool=w_branch_pool[lyr].astype(BF16), w_branch_delta=w_branch_delta[lyr].astype(BF16),
        w_out=w_out[lyr].astype(BF16), norm2_w=norm2_w[lyr].reshape(1, d),
        w_router_hi=w_router_hi, w_router_lo=(w_router - w_router_hi.astype(F32)).astype(BF16), b_router=b_router,
    )
    wg, wu, wd = w_gate[lyr], w_up[lyr], w_down[lyr]
    fnw = final_norm_w.reshape(1, d)

    tt_p = TOKEN_TILE // 2
    xm, h2, route, wk, counts, npool, nconv, ns = _layer(
        x_prompt.reshape(bsz * seq, d),
        jnp.zeros((bsz, POOL_BUF, POOL_DIM), F32), jnp.zeros((bsz, CONV_WIDTH - 1, CONV_DIM), F32),
        jnp.zeros((bsz, N_HEADS, HEAD_DIM, HEAD_DIM), F32), prm,
        n_seq=bsz, sb=4, tt=tt_p, tv=tt_p, chunk=DELTA_CHUNK, pos0=0, n_tiles=seq // tt_p, tm=TOKEN_TILE)
    yk = _moe(h2, route, counts, wg, wu, wd, tm_e=2 * TOKEN_TILE)
    y_prompt = _finalize(xm, yk, wk, fnw, 2 * TOKEN_TILE).reshape(bsz, seq, d)
    pool_p = npool[None]
    conv_p = nconv[None]
    delta_p = ns[None]

    tt_s = SUBLANES
    xs_pad = jnp.pad(x_sample, ((0, 0), (0, tt_s - dseq), (0, 0))).reshape(dbs * tt_s, d)
    xm, h2, route, wk, counts, npool, nconv, ns = _layer(
        xs_pad, cache_pool[lyr], cache_conv[lyr], state_delta[lyr], prm,
        n_seq=dbs, sb=8, tt=tt_s, tv=dseq, chunk=tt_s, pos0=PAST_LEN, n_tiles=1, tm=TOKEN_TILE)
    real = lambda a: a.reshape(dbs, tt_s, -1)[:, :dseq].reshape(dbs * dseq, -1)
    route = route.reshape(SUBLANES, dbs, tt_s)[:, :, :dseq].reshape(SUBLANES, dbs * dseq)
    yk = _moe(real(h2), route, counts, wg, wu, wd, tm_e=TOKEN_TILE // 2)
    y_sample = _finalize(real(xm), yk, real(wk), fnw, TOKEN_TILE).reshape(dbs, dseq, d)
    pool_s = npool[None]
    conv_s = nconv[None]
    delta_s = ns[None]
    return (y_prompt, y_sample, pool_p, conv_p, delta_p, pool_s, conv_s, delta_s)
```

```python
import functools

import jax
import jax.numpy as jnp
from jax import lax
from jax.experimental import pallas as pl
from jax.experimental.pallas import tpu as pltpu
from jax.experimental.pallas import tpu_sc as plsc

F32 = jnp.float32
BF16 = jnp.bfloat16
U32 = jnp.uint32
EPS = 1e-6
HIGHEST = lax.Precision.HIGHEST

POOL_WINDOWS = (2, 4, 8, 16)
POOL_GROUP_DIM = 128
POOL_DIM = len(POOL_WINDOWS) * POOL_GROUP_DIM
POOL_BUF = max(POOL_WINDOWS) - 1
N_HEADS = 8
HEAD_DIM = 128
QK_DIM = N_HEADS * HEAD_DIM
CONV_WIDTH = 4
CONV_DIM = 3 * QK_DIM
N_GROUPS = 4
EXPERTS_PER_GROUP = 8
N_EXPERTS = N_GROUPS * EXPERTS_PER_GROUP
PAST_LEN = 16384
DELTA_CHUNK = 64
TOKEN_TILE = 256
LANES = 128
SUBLANES = 8

OFF_Z = 0
OFF_QKV = OFF_Z + QK_DIM
OFF_POOL = OFF_QKV + CONV_DIM
OFF_B = OFF_POOL + POOL_DIM
OFF_A = OFF_B + LANES
MIX_COLS = OFF_A + LANES
GATE_COLS = 2048
IN_COLS = MIX_COLS + GATE_COLS

POOL_ROWS = 16
CONV_ROWS = 8
VMEM_LIMIT = 60 * 1024 * 1024
EXPERT_PART_ROWS = 128
EXPERT_STAGES = 4
CONV_GROUPS = 12
SC_ROWS = 96


def _sigmoid(x):
    return 1.0 / (1.0 + jnp.exp(-x))


def _softplus(x):
    return jnp.maximum(x, 0.0) + jnp.log1p(jnp.exp(-jnp.abs(x)))


def _rms(x, w):
    return x * lax.rsqrt(jnp.mean(x * x, axis=-1, keepdims=True) + EPS) * w


def _dot(a, b):
    return jnp.dot(a.astype(BF16), b.astype(BF16), preferred_element_type=F32)


def _dot_nt(a, b):
    return lax.dot_general(a.astype(BF16), b.astype(BF16), (((1,), (1,)), ((), ())), preferred_element_type=F32)


def _dot_tn(a, b):
    return lax.dot_general(a.astype(BF16), b.astype(BF16), (((0,), (0,)), ((), ())), preferred_element_type=F32)


def _pack_pairs(x):
    c = x.shape[1] // 2
    hi = pltpu.bitcast(x[:, :c].astype(BF16).astype(F32), U32)
    lo = pltpu.bitcast(x[:, c:].astype(BF16).astype(F32), U32)
    return hi | lax.shift_right_logical(lo, jnp.uint32(16))


def _unpack_pairs(w):
    left = pltpu.bitcast(w & jnp.uint32(0xFFFF0000), F32)
    right = pltpu.bitcast(lax.shift_left(w, jnp.uint32(16)), F32)
    return left, right


def _inproj_kernel(x_ref, nw_ref, w_ref, mix_ref, gate_ref):
    h = _rms(x_ref[...], nw_ref[...]).astype(BF16)
    half = MIX_COLS // 2
    for c0 in (0, half):
        mix_ref[:, c0:c0 + half] = jnp.dot(h, w_ref[:, c0:c0 + half], preferred_element_type=F32)
    half = GATE_COLS // 2
    for c0 in (0, half):
        gate_ref[:, c0:c0 + half] = jnp.dot(h, w_ref[:, MIX_COLS + c0:MIX_COLS + c0 + half],
                                            preferred_element_type=F32)


def _inproj(x, norm_w, w_in_r, tm):
    n, d = x.shape
    return pl.pallas_call(
        _inproj_kernel,
        grid=(n // tm,),
        in_specs=[
            pl.BlockSpec((tm, d), lambda i: (i, 0)),
            pl.BlockSpec((1, d), lambda i: (0, 0)),
            pl.BlockSpec((d, IN_COLS), lambda i: (0, 0)),
        ],
        out_specs=[pl.BlockSpec((tm, MIX_COLS), lambda i: (i, 0)), pl.BlockSpec((tm, GATE_COLS), lambda i: (i, 0))],
        out_shape=[jax.ShapeDtypeStruct((n, MIX_COLS), F32), jax.ShapeDtypeStruct((n, GATE_COLS), F32)],
        compiler_params=pltpu.CompilerParams(dimension_semantics=("arbitrary",), vmem_limit_bytes=VMEM_LIMIT),
        name="inproj",
    )(x, norm_w, w_in_r)


def _conv_silu_qkv(ext_c, seq, tt, wconv_ref, ci, col0=0):
    base = CONV_ROWS - (CONV_WIDTH - 1)
    cs = slice(ci * LANES, (ci + 1) * LANES)
    cl = slice(ci * LANES - col0, (ci + 1) * LANES - col0)
    idx = (lambda r: (r, cl)) if seq is None else (lambda r: (seq, r, cl))
    acc = ext_c[idx(slice(base, base + tt))] * wconv_ref[0:1, cs]
    for j in range(1, CONV_WIDTH):
        acc = acc + ext_c[idx(slice(base + j, base + j + tt))] * wconv_ref[j:j + 1, cs]
    y = acc * _sigmoid(acc)
    if ci < 2 * N_HEADS:
        y = y * lax.rsqrt(jnp.sum(y * y, axis=-1, keepdims=True) + EPS)
    if ci < N_HEADS:
        y = y * (HEAD_DIM ** -0.5)
    return y


def _inproj_conv_kernel(x_ref, nw_ref, w_ref, wconv_ref, conv0_ref, mix_ref, gate_ref, nconv_ref, h_s, *ext, n_tiles):
    t = pl.program_id(1)
    tm = x_ref.shape[1]
    base = CONV_ROWS - (CONV_WIDTH - 1)
    step = CONV_DIM // len(ext)

    @pl.when(t == 0)
    def _():
        for g, buf in enumerate(ext):
            buf[base:CONV_ROWS, :] = conv0_ref[0, :, g * step:(g + 1) * step]

    h_s[...] = _rms(x_ref[0], nw_ref[...]).astype(BF16)

    def proj(c0, c1):
        return jnp.dot(h_s[...], w_ref[:, c0:c1], preferred_element_type=F32)

    def fill_z0():
        mix_ref[0, :, OFF_Z:OFF_Z + QK_DIM // 2] = proj(OFF_Z, OFF_Z + QK_DIM // 2)

    def fill_z1():
        mix_ref[0, :, OFF_Z + QK_DIM // 2:OFF_QKV] = proj(OFF_Z + QK_DIM // 2, OFF_QKV)

    def fill_pool():
        mix_ref[0, :, OFF_POOL:MIX_COLS] = proj(OFF_POOL, MIX_COLS)

    def fill_gate(i):
        q = GATE_COLS // 4
        gate_ref[0, :, i * q:(i + 1) * q] = proj(MIX_COLS + i * q, MIX_COLS + (i + 1) * q)

    fillers = [fill_z0, fill_z1, fill_pool] + [functools.partial(fill_gate, i) for i in range(4)]
    for g, buf in enumerate(ext):
        c0 = g * step
        buf[CONV_ROWS:CONV_ROWS + tm, :] = proj(OFF_QKV + c0, OFF_QKV + c0 + step)
        if fillers:
            fillers.pop(0)()
        for ci in range(c0 // LANES, (c0 + step) // LANES):
            mix_ref[0, :, OFF_QKV + ci * LANES:OFF_QKV + (ci + 1) * LANES] = _conv_silu_qkv(
                buf, None, tm, wconv_ref, ci, c0)
    for f in fillers:
        f()

    @pl.when(t == n_tiles - 1)
    def _():
        for g, buf in enumerate(ext):
            nconv_ref[0, :, g * step:(g + 1) * step] = buf[tm + base:tm + CONV_ROWS, :]

    for buf in ext:
        buf[0:CONV_ROWS, :] = buf[tm:tm + CONV_ROWS, :]


def _inproj_conv(x, conv0, norm_w, w_in_r, wconv, tm):
    n_seq, seq_len, d = x.shape
    n_tiles = seq_len // tm
    tile = lambda b, t: (b, t, 0)
    const = lambda b, t: (0, 0)
    mix, gates, nconv = pl.pallas_call(
        functools.partial(_inproj_conv_kernel, n_tiles=n_tiles),
        grid=(n_seq, n_tiles),
        in_specs=[
            pl.BlockSpec((1, tm, d), tile),
            pl.BlockSpec((1, d), const),
            pl.BlockSpec((d, IN_COLS), const),
            pl.BlockSpec((SUBLANES, CONV_DIM), const),
            pl.BlockSpec((1, CONV_WIDTH - 1, CONV_DIM), lambda b, t: (b, 0, 0)),
        ],
        out_specs=[
            pl.BlockSpec((1, tm, MIX_COLS), tile),
            pl.BlockSpec((1, tm, GATE_COLS), tile),
            pl.BlockSpec((1, CONV_WIDTH - 1, CONV_DIM), lambda b, t: (b, 0, 0)),
        ],
        out_shape=[
            jax.ShapeDtypeStruct((n_seq, seq_len, MIX_COLS), F32),
            jax.ShapeDtypeStruct((n_seq, seq_len, GATE_COLS), F32),
            jax.ShapeDtypeStruct((n_seq, CONV_WIDTH - 1, CONV_DIM), F32),
        ],
        scratch_shapes=[pltpu.VMEM((tm, d), BF16)] + [
            pltpu.VMEM((CONV_ROWS + tm, CONV_DIM // CONV_GROUPS), F32) for _ in range(CONV_GROUPS)],
        compiler_params=pltpu.CompilerParams(dimension_semantics=("arbitrary", "arbitrary"),
                                             vmem_limit_bytes=VMEM_LIMIT),
        name="inproj_conv",
    )(x, norm_w, w_in_r, wconv, conv0)
    n = n_seq * seq_len
    return mix.reshape(n, MIX_COLS), gates.reshape(n, GATE_COLS), nconv


def _mixer_kernel(proj_ref, pool0_ref, conv0_ref, s0_ref, wpool_ref, pscale_ref, wconv_ref, alog_ref, dtb_ref, onw_ref,
                  ypool_ref, ydelta_ref, npool_ref, nconv_ref, ns_ref,
                  ext_p, ext_c, q_s, k_s, v_s, g_s, b_s, state,
                  *, sb, tt, tv, chunk, pos0, n_tiles, qkv_done, caches_by_time):
    t = pl.program_id(1)
    base = CONV_ROWS - (CONV_WIDTH - 1)

    @pl.when(t == 0)
    def _():
        if caches_by_time:
            for j in range(POOL_BUF):
                ext_p[:, POOL_ROWS - POOL_BUF + j, :] = pool0_ref[j]
            for j in range(0 if qkv_done else CONV_WIDTH - 1):
                ext_c[:, base + j, :] = conv0_ref[j]
        else:
            ext_p[:, POOL_ROWS - POOL_BUF:POOL_ROWS, :] = pool0_ref[...]
            if not qkv_done:
                ext_c[:, base:CONV_ROWS, :] = conv0_ref[...]
        state[...] = s0_ref[...]

    row = lax.broadcasted_iota(jnp.int32, (tt, 1), 0)
    pos = pos0 + t * tt + row
    for s in range(sb):
        ext_p[s, POOL_ROWS:POOL_ROWS + tt, :] = proj_ref[s, :, OFF_POOL:OFF_POOL + POOL_DIM]
        if not qkv_done:
            ext_c[s, CONV_ROWS:CONV_ROWS + tt, :] = proj_ref[s, :, OFF_QKV:OFF_QKV + CONV_DIM]

        for gi, win in enumerate(POOL_WINDOWS):
            cs = slice(gi * POOL_GROUP_DIM, (gi + 1) * POOL_GROUP_DIM)
            u = ext_p[s, POOL_ROWS:POOL_ROWS + tt, cs]
            acc = u
            for j in range(1, win):
                acc = acc + ext_p[s, POOL_ROWS - j:POOL_ROWS - j + tt, cs]
            cnt = jnp.minimum(win, pos + 1).astype(F32)
            pooled = acc / cnt - u
            ypool_ref[s, :, cs] = (_dot(pooled, wpool_ref[gi]) * pscale_ref[:, cs]).astype(BF16)

        for ci in range(0 if qkv_done else CONV_DIM // LANES):
            hs = slice((ci % N_HEADS) * HEAD_DIM, (ci % N_HEADS + 1) * HEAD_DIM)
            (q_s, k_s, v_s)[ci // N_HEADS][s, :, hs] = _conv_silu_qkv(ext_c, s, tt, wconv_ref, ci)

        beta = _sigmoid(proj_ref[s, :, OFF_B:OFF_B + LANES])
        g = -jnp.exp(alog_ref[...]) * _softplus(proj_ref[s, :, OFF_A:OFF_A + LANES] + dtb_ref[...])
        if tv < tt:
            beta = jnp.where(row < tv, beta, 0.0)
            g = jnp.where(row < tv, g, 0.0)
        b_s[s] = beta
        g_s[s] = g

    @pl.when(t == n_tiles - 1)
    def _():
        if qkv_done:
            nconv_ref[...] = jnp.zeros_like(nconv_ref)
        if caches_by_time:
            for j in range(POOL_BUF):
                npool_ref[j] = ext_p[:, tv + POOL_ROWS - POOL_BUF + j, :]
            for j in range(0 if qkv_done else CONV_WIDTH - 1):
                nconv_ref[j] = ext_c[:, tv + base + j, :]
        else:
            npool_ref[...] = ext_p[:, tv + POOL_ROWS - POOL_BUF:tv + POOL_ROWS, :]
            if not qkv_done:
                nconv_ref[...] = ext_c[:, tv + base:tv + CONV_ROWS, :]

    if n_tiles > 1:
        ext_p[:, 0:POOL_ROWS, :] = ext_p[:, tt:tt + POOL_ROWS, :]
        if not qkv_done:
            ext_c[:, 0:CONV_ROWS, :] = ext_c[:, tt:tt + CONV_ROWS, :]

    ri = lax.broadcasted_iota(jnp.int32, (chunk, chunk), 0)
    cj = lax.broadcasted_iota(jnp.int32, (chunk, chunk), 1)
    causal = ri >= cj
    strict = ri > cj
    ltri = causal.astype(F32)
    eye = (ri == cj).astype(F32)
    n_doublings = chunk.bit_length() - 2
    units = [(s, h) for s in range(sb) for h in range(N_HEADS)]
    hsl = [slice(h * HEAD_DIM, (h + 1) * HEAD_DIM) for h in range(N_HEADS)]

    def qkv(which, s, rows, h):
        if qkv_done:
            c0 = OFF_QKV + which * QK_DIM + h * HEAD_DIM
            return proj_ref[s, rows, c0:c0 + HEAD_DIM]
        return (q_s, k_s, v_s)[which][s, rows, hsl[h]]

    def chunk_body(ci, carry):
        r0 = pl.multiple_of(ci * chunk, chunk)
        rows = pl.ds(r0, chunk)
        g_all, g_t, b_all = [], [], []
        for s in range(sb):
            ga = jnp.dot(ltri, g_s[s, rows, :], precision=HIGHEST, preferred_element_type=F32)
            g_all.append(ga)
            if chunk < LANES:
                ga = jnp.concatenate([ga, jnp.zeros((LANES - chunk, LANES), F32)], axis=0)
            g_t.append(ga.T)
            b_all.append(b_s[s, rows, :])
        g_col = [g_all[s][:, h:h + 1] for s, h in units]
        dec = [jnp.exp(jnp.minimum(g_col[i] - g_t[s][h:h + 1, 0:chunk], 0.0)) for i, (s, h) in enumerate(units)]
        b_col = [b_all[s][:, h:h + 1] for s, h in units]
        q = [qkv(0, s, rows, h) for s, h in units]
        k = [qkv(1, s, rows, h) for s, h in units]
        idx = range(len(units))
        kb = [k[i] * b_col[i] for i in idx]
        kbk = [_dot_nt(kb[i], k[i]) for i in idx]
        qk = [_dot_nt(q[i], k[i]) for i in idx]
        p_mat = [jnp.where(strict, kbk[i] * dec[i], 0.0) for i in idx]
        t_inv = [eye - p_mat[i] for i in idx]
        for _ in range(n_doublings):
            p_mat = [_dot(p_mat[i], p_mat[i]) for i in idx]
            xp = [_dot(t_inv[i], p_mat[i]) for i in idx]
            t_inv = [t_inv[i] + xp[i] for i in idx]
        g_exp = [jnp.exp(g_col[i]) for i in idx]
        wu = [_dot(t_inv[i], jnp.concatenate([kb[i] * g_exp[i], qkv(2, s, rows, h) * b_col[i]], axis=1))
              for i, (s, h) in enumerate(units)]
        s_old = [state[s, h] for s, h in units]
        res = [_dot(jnp.concatenate([wu[i][:, 0:HEAD_DIM], q[i] * g_exp[i]], axis=0), s_old[i]) for i in idx]
        v_new = [wu[i][:, HEAD_DIM:2 * HEAD_DIM] - res[i][0:chunk] for i in idx]
        a_qk = [jnp.where(causal, qk[i] * dec[i], 0.0) for i in idx]
        g_last = [g_all[s][chunk - 1:chunk, h:h + 1] for s, h in units]
        k_dec = [k[i] * jnp.exp(g_last[i] - g_col[i]) for i in idx]
        intra = [_dot(a_qk[i], v_new[i]) for i in idx]
        upd = [_dot_tn(k_dec[i], v_new[i]) for i in idx]
        for i, (s, h) in enumerate(units):
            state[s, h] = s_old[i] * jnp.exp(g_last[i]) + upd[i]
            o_c = res[i][chunk:2 * chunk] + intra[i]
            z_c = proj_ref[s, rows, OFF_Z + h * HEAD_DIM:OFF_Z + (h + 1) * HEAD_DIM]
            ydelta_ref[s, rows, hsl[h]] = (_rms(o_c, onw_ref[...]) * (z_c * _sigmoid(z_c))).astype(BF16)
        return carry

    lax.fori_loop(0, tt // chunk, chunk_body, 0)

    @pl.when(t == n_tiles - 1)
    def _():
        ns_ref[...] = state[...]


def _mixers(proj, pool0, conv0, s0, wpool, pscale, wconv, alog, dtb, onw, *, n_seq, sb, tt, tv, chunk, pos0, n_tiles,
            qkv_done):
    caches_by_time = sb % SUBLANES == 0
    kern = functools.partial(_mixer_kernel, sb=sb, tt=tt, tv=tv, chunk=chunk, pos0=pos0, n_tiles=n_tiles,
                             qkv_done=qkv_done, caches_by_time=caches_by_time)
    seq_len = n_tiles * tt
    tile = lambda b, t: (b, t, 0)
    seq3 = lambda b, t: (b, 0, 0)
    if caches_by_time:
        pool0, conv0 = jnp.swapaxes(pool0, 0, 1), jnp.swapaxes(conv0, 0, 1)
        cache_block = lambda rows, cols: pl.BlockSpec((rows, sb, cols), lambda b, t: (0, b, 0))
        cache_shape = lambda rows, cols: jax.ShapeDtypeStruct((rows, n_seq, cols), F32)
    else:
        cache_block = lambda rows, cols: pl.BlockSpec((sb, rows, cols), seq3)
        cache_shape = lambda rows, cols: jax.ShapeDtypeStruct((n_seq, rows, cols), F32)
    seq4 = lambda b, t: (b, 0, 0, 0)
    const2 = lambda b, t: (0, 0)
    ypool, ydelta, npool, nconv, ns = pl.pallas_call(
        kern,
        grid=(n_seq // sb, n_tiles),
        in_specs=[
            pl.BlockSpec((sb, tt, MIX_COLS), tile),
            cache_block(POOL_BUF, POOL_DIM),
            cache_block(CONV_WIDTH - 1, CONV_DIM),
            pl.BlockSpec((sb, N_HEADS, HEAD_DIM, HEAD_DIM), seq4),
            pl.BlockSpec((len(POOL_WINDOWS), POOL_GROUP_DIM, POOL_GROUP_DIM), lambda b, t: (0, 0, 0)),
            pl.BlockSpec((1, POOL_DIM), const2),
            pl.BlockSpec((SUBLANES, CONV_DIM), const2),
            pl.BlockSpec((1, LANES), const2),
            pl.BlockSpec((1, LANES), const2),
            pl.BlockSpec((1, HEAD_DIM), const2),
        ],
        out_specs=[
            pl.BlockSpec((sb, tt, POOL_DIM), tile),
            pl.BlockSpec((sb, tt, QK_DIM), tile),
            cache_block(POOL_BUF, POOL_DIM),
            cache_block(CONV_WIDTH - 1, CONV_DIM),
            pl.BlockSpec((sb, N_HEADS, HEAD_DIM, HEAD_DIM), seq4),
        ],
        out_shape=[
            jax.ShapeDtypeStruct((n_seq, seq_len, POOL_DIM), BF16),
            jax.ShapeDtypeStruct((n_seq, seq_len, QK_DIM), BF16),
            cache_shape(POOL_BUF, POOL_DIM),
            cache_shape(CONV_WIDTH - 1, CONV_DIM),
            jax.ShapeDtypeStruct((n_seq, N_HEADS, HEAD_DIM, HEAD_DIM), F32),
        ],
        scratch_shapes=[
            pltpu.VMEM((sb, POOL_ROWS + tt, POOL_DIM), F32),
            pltpu.VMEM((sb, CONV_ROWS + tt, CONV_DIM), F32),
            pltpu.VMEM((sb, tt, QK_DIM), F32),
            pltpu.VMEM((sb, tt, QK_DIM), F32),
            pltpu.VMEM((sb, tt, QK_DIM), F32),
            pltpu.VMEM((sb, tt, LANES), F32),
            pltpu.VMEM((sb, tt, LANES), F32),
            pltpu.VMEM((sb, N_HEADS, HEAD_DIM, HEAD_DIM), F32),
        ],
        compiler_params=pltpu.CompilerParams(dimension_semantics=("arbitrary", "arbitrary"),
                                             vmem_limit_bytes=VMEM_LIMIT),
        name="mixers",
    )(proj.reshape(n_seq, seq_len, MIX_COLS), pool0, conv0, s0, wpool, pscale, wconv, alog, dtb, onw)
    n = n_seq * seq_len
    if caches_by_time:
        npool, nconv = jnp.swapaxes(npool, 0, 1), jnp.swapaxes(nconv, 0, 1)
    return ypool.reshape(n, POOL_DIM), ydelta.reshape(n, QK_DIM), npool, nconv, ns


def _merge_kernel(x_ref, gates_ref, ypool_ref, ydelta_ref, wbp_ref, wbd_ref, wout_ref, n2w_ref, wr_hi_ref, wr_lo_ref,
                  br_ref, xm_ref, h2_ref, route_ref, wk_ref, cnt_ref, cnt_s, logits_s, *, period, valid):
    tm = x_ref.shape[0]
    step = pl.program_id(0)

    @pl.when(step == 0)
    def _():
        cnt_s[...] = jnp.zeros_like(cnt_s)
        logits_s[...] = jnp.zeros_like(logits_s)

    logits = logits_s[...]
    bp = _dot(ypool_ref[...], wbp_ref[...])
    bd = _dot(ydelta_ref[...], wbd_ref[...])

    lane = lax.broadcasted_iota(jnp.int32, logits.shape, 1)
    neg = -jnp.inf
    far = LANES - 1
    is_g = lane < N_GROUPS
    g_max = jnp.max(jnp.where(is_g, logits, neg), axis=-1, keepdims=True)
    g_sel = jnp.min(jnp.where(is_g & (logits == g_max), lane, far), axis=-1, keepdims=True)
    p_g = 1.0 / jnp.sum(jnp.where(is_g, jnp.exp(logits - g_max), 0.0), axis=-1, keepdims=True)
    e_lane = lane - N_GROUPS
    is_e = (e_lane >= 0) & (e_lane < N_EXPERTS) & ((e_lane // EXPERTS_PER_GROUP) == g_sel)
    ev = jnp.where(is_e, logits, neg)
    v1 = jnp.max(ev, axis=-1, keepdims=True)
    i1 = jnp.min(jnp.where(is_e & (ev == v1), lane, far), axis=-1, keepdims=True)
    is_e2 = is_e & (lane != i1)
    ev2 = jnp.where(is_e2, logits, neg)
    v2 = jnp.max(ev2, axis=-1, keepdims=True)
    i2 = jnp.min(jnp.where(is_e2 & (ev2 == v2), lane, far), axis=-1, keepdims=True)
    e21 = jnp.exp(v2 - v1)
    w1 = p_g / (1.0 + e21)
    w2 = p_g * e21 / (1.0 + e21)
    e1 = i1 - N_GROUPS
    e2 = i2 - N_GROUPS
    chosen = ((lane == e1) | (lane == e2)) & (step > 0)
    if valid < period:
        row = lax.broadcasted_iota(jnp.int32, (tm, 1), 0)
        chosen = chosen & (lax.rem(row, period) < valid)
    onehot = jnp.where(chosen, 1.0, 0.0)

    rr = lax.broadcasted_iota(jnp.int32, (tm, tm), 0)
    cc = lax.broadcasted_iota(jnp.int32, (tm, tm), 1)
    earlier = jnp.where(rr > cc, 1.0, 0.0).astype(BF16)
    before = jnp.dot(earlier, onehot.astype(BF16), preferred_element_type=F32) + cnt_s[...]
    r1 = jnp.sum(jnp.where(lane == e1, before, 0.0), axis=-1, keepdims=True)
    r2 = jnp.sum(jnp.where(lane == e2, before, 0.0), axis=-1, keepdims=True)
    cnt_s[...] = cnt_s[...] + jnp.sum(onehot, axis=0, keepdims=True)
    cnt_ref[...] = cnt_s[...]
    record = jnp.where(lane == 0, e1.astype(F32), jnp.where(lane == 1, e2.astype(F32),
                                                           jnp.where(lane == 2, r1, jnp.where(lane == 3, r2, 0.0))))
    route_ref[...] = record.T[0:SUBLANES, :]
    lane8 = lax.broadcasted_iota(jnp.int32, (tm, SUBLANES), 1)
    wk_ref[...] = jnp.where(lane8 == 0, w1, jnp.where(lane8 == 1, w2, 0.0))

    merged = _sigmoid(gates_ref[:, 0:1024]) * bp + _sigmoid(gates_ref[:, 1024:2048]) * bd
    xm = x_ref[...] + _dot(merged, wout_ref[...])
    xm_ref[...] = xm
    h2 = _rms(xm, n2w_ref[...])
    h2_ref[...] = _pack_pairs(h2)
    h2_hi = h2.astype(BF16)
    h2_lo = (h2 - h2_hi.astype(F32)).astype(BF16)
    logits_s[...] = (jnp.dot(h2_hi, wr_hi_ref[...], preferred_element_type=F32)
                     + (jnp.dot(h2_hi, wr_lo_ref[...], preferred_element_type=F32)
                        + jnp.dot(h2_lo, wr_hi_ref[...], preferred_element_type=F32))) + br_ref[...]


def _merge(x, proj, ypool, ydelta, wbp, wbd, wout, n2w, wr_hi, wr_lo, br, tm, period, valid):
    n, d = x.shape
    last = n // tm - 1
    row = lambda i: (jnp.minimum(i, last), 0)
    chosen_row = lambda i: (jnp.maximum(i - 1, 0), 0)
    const = lambda i: (0, 0)
    return pl.pallas_call(
        functools.partial(_merge_kernel, period=period, valid=valid),
        grid=(n // tm + 1,),
        in_specs=[
            pl.BlockSpec((tm, d), row),
            pl.BlockSpec((tm, GATE_COLS), row),
            pl.BlockSpec((tm, POOL_DIM), row),
            pl.BlockSpec((tm, QK_DIM), row),
            pl.BlockSpec((POOL_DIM, d), const),
            pl.BlockSpec((QK_DIM, d), const),
            pl.BlockSpec((d, d), const),
            pl.BlockSpec((1, d), const),
            pl.BlockSpec((d, LANES), const),
            pl.BlockSpec((d, LANES), const),
            pl.BlockSpec((1, LANES), const),
        ],
        out_specs=[
            pl.BlockSpec((tm, d), row),
            pl.BlockSpec((tm, d // 2), row),
            pl.BlockSpec((SUBLANES, tm), lambda i: (0, jnp.maximum(i - 1, 0))),
            pl.BlockSpec((tm, SUBLANES), chosen_row),
            pl.BlockSpec((1, LANES), const),
        ],
        out_shape=[
            jax.ShapeDtypeStruct((n, d), F32),
            jax.ShapeDtypeStruct((n, d // 2), U32),
            jax.ShapeDtypeStruct((SUBLANES, n), F32),
            jax.ShapeDtypeStruct((n, SUBLANES), F32),
            jax.ShapeDtypeStruct((1, LANES), F32),
        ],
        scratch_shapes=[pltpu.VMEM((1, LANES), F32), pltpu.VMEM((tm, LANES), F32)],
        compiler_params=pltpu.CompilerParams(dimension_semantics=("arbitrary",), vmem_limit_bytes=VMEM_LIMIT),
        name="merge",
    )(x, proj, ypool, ydelta, wbp, wbd, wout, n2w, wr_hi, wr_lo, br)


def _expert_kernel(te_ref, nv_ref, x_ref, wg_hbm, wu_hbm, wd_hbm, y_ref, st_g, st_u, st_d, wg_s, wu_s, wd_s, sem, slot_s):
    i = pl.program_id(0)
    n_tiles = pl.num_programs(0)
    n_valid = nv_ref[i]
    expert = te_ref[i]

    def expert_at(j):
        return te_ref[jnp.minimum(j, n_tiles - 1)]

    def next_expert_tile(j):
        e = expert_at(j)
        return lax.while_loop(lambda t: (t < n_tiles) & (expert_at(t) == e), lambda t: t + 1, j + 1)

    def start_weight_copies(j, slot):
        @pl.when(j < n_tiles)
        def _():
            for cp in weight_copies(expert_at(j), slot):
                cp.start()

    def weight_copies(e, slot):
        return [pltpu.make_async_copy(src.at[e], dst.at[slot], sem.at[slot])
                for src, dst in ((wg_hbm, st_g), (wu_hbm, st_u), (wd_hbm, st_d))]

    @pl.when(i == 0)
    def _():
        slot_s[0] = 0
        j = i
        for slot in range(EXPERT_STAGES - 1):
            start_weight_copies(j, slot)
            j = next_expert_tile(j)

    @pl.when((i == 0) | (expert != te_ref[jnp.maximum(i - 1, 0)]))
    def _():
        slot = slot_s[0]
        j = i
        for _ in range(EXPERT_STAGES - 1):
            j = next_expert_tile(j)
        start_weight_copies(j, (slot + EXPERT_STAGES - 1) % EXPERT_STAGES)
        for cp in weight_copies(expert, slot):
            cp.wait()
        wg_s[...] = st_g[slot].astype(BF16)
        wu_s[...] = st_u[slot].astype(BF16)
        wd_s[...] = st_d[slot].astype(BF16)
        slot_s[0] = (slot + 1) % EXPERT_STAGES

    @pl.when(n_valid > 0)
    def _():
        tm = x_ref.shape[0]
        part = min(tm, EXPERT_PART_ROWS)
        parts = [slice(r0, r0 + part) for r0 in range(0, tm, part)]
        row = lax.broadcasted_iota(jnp.int32, (part, 1), 0)
        xs = []
        for p in parts:
            left, right = _unpack_pairs(jnp.where(row + p.start < n_valid, x_ref[p, :], jnp.uint32(0)))
            xs.append(jnp.concatenate([left.astype(BF16), right.astype(BF16)], axis=1))
        a = [_dot(x, wg_s[...]) for x in xs]
        b = [_dot(x, wu_s[...]) for x in xs]
        act = [(a[j] * _sigmoid(a[j])) * b[j] for j in range(len(parts))]
        y = [_dot(act[j], wd_s[...]) for j in range(len(parts))]
        for j, p in enumerate(parts):
            y_ref[p, :] = _pack_pairs(y[j])

    @pl.when(n_valid == 0)
    def _():
        y_ref[...] = jnp.zeros_like(y_ref)


def _experts(tile_expert, tile_valid, xs, wg, wu, wd, tm):
    n_rows, dp = xs.shape
    d, f = wg.shape[1], wg.shape[2]
    grid_spec = pltpu.PrefetchScalarGridSpec(
        num_scalar_prefetch=2,
        grid=(n_rows // tm,),
        in_specs=[
            pl.BlockSpec((tm, dp), lambda i, te, nv: (jnp.where(nv[i] > 0, i, 0), 0)),
            pl.BlockSpec(memory_space=pl.ANY),
            pl.BlockSpec(memory_space=pl.ANY),
            pl.BlockSpec(memory_space=pl.ANY),
        ],
        out_specs=pl.BlockSpec((tm, dp), lambda i, te, nv: (i, 0)),
        scratch_shapes=[pltpu.VMEM((EXPERT_STAGES, d, f), F32), pltpu.VMEM((EXPERT_STAGES, d, f), F32),
                        pltpu.VMEM((EXPERT_STAGES, f, d), F32),
                        pltpu.VMEM((d, f), BF16), pltpu.VMEM((d, f), BF16), pltpu.VMEM((f, d), BF16),
                        pltpu.SemaphoreType.DMA((EXPERT_STAGES,)), pltpu.SMEM((1,), jnp.int32)],
    )
    return pl.pallas_call(
        _expert_kernel,
        grid_spec=grid_spec,
        out_shape=jax.ShapeDtypeStruct((n_rows, dp), U32),
        compiler_params=pltpu.CompilerParams(dimension_semantics=("arbitrary",), vmem_limit_bytes=VMEM_LIMIT),
        name="experts",
    )(tile_expert, tile_valid, xs, wg, wu, wd)


def _plan_kernel(route_ref, cnt_ref, pos_ref, tiles_ref, *, tm_e):
    counts = cnt_ref[...]
    padded = jnp.floor((counts + (tm_e - 1)) / tm_e) * tm_e
    lane = lax.broadcasted_iota(jnp.int32, counts.shape, 1)
    ends = padded
    shift = 1
    while shift < N_EXPERTS:
        ends = ends + jnp.where(lane >= shift, pltpu.roll(ends, shift, axis=1), 0.0)
        shift *= 2
    starts = ends - padded

    def lookup(table, key):
        out = jnp.zeros_like(key)
        for e in range(N_EXPERTS):
            out = out + jnp.where(key == e, table[:, e:e + 1], 0.0)
        return out

    for k in range(2):
        pos = lookup(starts, route_ref[k:k + 1, :]) + route_ref[2 + k:3 + k, :]
        pos_ref[k:k + 1, :] = pos.astype(jnp.int32)

    tile_start = lax.broadcasted_iota(jnp.int32, (1, tiles_ref.shape[1]), 1).astype(F32) * tm_e
    tile_expert = jnp.zeros_like(tile_start)
    for e in range(N_EXPERTS):
        tile_expert = tile_expert + jnp.where(ends[:, e:e + 1] <= tile_start, 1.0, 0.0)
    tile_expert = jnp.minimum(tile_expert, N_EXPERTS - 1.0)
    valid = lookup(counts, tile_expert) - (tile_start - lookup(starts, tile_expert))
    tiles_ref[0:1, :] = tile_expert.astype(jnp.int32)
    tiles_ref[1:2, :] = jnp.clip(valid, 0.0, tm_e).astype(jnp.int32)


def _plan(route_t, counts, tm_e, n_tiles):
    n = route_t.shape[1]
    tiles_pad = -(-n_tiles // LANES) * LANES
    return pl.pallas_call(
        functools.partial(_plan_kernel, tm_e=tm_e),
        out_shape=[jax.ShapeDtypeStruct((2, n), jnp.int32), jax.ShapeDtypeStruct((2, tiles_pad), jnp.int32)],
        name="plan",
    )(route_t, counts)


def _sc_workers():
    info = plsc.get_sparse_core_info()
    return info.num_cores, info.num_subcores


def _sc_chunk(per_worker):
    assert per_worker % SUBLANES == 0
    return max(c for c in range(SUBLANES, SC_ROWS + 1, SUBLANES) if per_worker % c == 0)


def _sc_scatter2(rows, idx_a, idx_b, n_out):
    n_cores, n_sub = _sc_workers()
    n, d = rows.shape
    per_worker = n // (n_cores * n_sub)
    assert per_worker * n_cores * n_sub == n
    chunk = _sc_chunk(per_worker)
    mesh = plsc.VectorSubcoreMesh(core_axis_name="c", subcore_axis_name="s")

    n_chunks = per_worker // chunk
    buf = lambda shape, dtype: [pltpu.VMEM(shape, dtype), pltpu.VMEM(shape, dtype)]

    @functools.partial(
        pl.kernel, mesh=mesh, out_type=jax.ShapeDtypeStruct((n_out, d), rows.dtype),
        scratch_types=buf((chunk,), jnp.int32) + buf((chunk,), jnp.int32) + buf((chunk, d), rows.dtype)
        + [pltpu.SemaphoreType.DMA, pltpu.SemaphoreType.DMA])
    def scatter_rows(rows_hbm, ia_hbm, ib_hbm, out_hbm, ia0, ia1, ib0, ib1, r0, r1, sem0, sem1):
        worker = lax.axis_index("s") * n_cores + lax.axis_index("c")
        base = worker * per_worker
        slots = ((ia0, ib0, r0, sem0), (ia1, ib1, r1, sem1))
        pending = {}
        for j in range(n_chunks):
            ia_v, ib_v, rows_v, sem = slots[j % 2]
            if j >= 2:
                for cp in pending.pop(j - 2):
                    cp.wait()
            off = pl.multiple_of(base + j * chunk, SUBLANES)
            pltpu.sync_copy(ia_hbm.at[pl.ds(off, chunk)], ia_v)
            pltpu.sync_copy(ib_hbm.at[pl.ds(off, chunk)], ib_v)
            pltpu.sync_copy(rows_hbm.at[pl.ds(off, chunk)], rows_v)
            pending[j] = (pltpu.async_copy(rows_v, out_hbm.at[ia_v], sem),
                          pltpu.async_copy(rows_v, out_hbm.at[ib_v], sem))
        for j in sorted(pending):
            for cp in pending[j]:
                cp.wait()

    return scatter_rows(rows, idx_a, idx_b)


def _sc_gather(table, idx):
    n_cores, n_sub = _sc_workers()
    n_idx = idx.shape[0]
    d = table.shape[1]
    per_worker = n_idx // (n_cores * n_sub)
    assert per_worker * n_cores * n_sub == n_idx
    chunk = _sc_chunk(per_worker)
    mesh = plsc.VectorSubcoreMesh(core_axis_name="c", subcore_axis_name="s")

    n_chunks = per_worker // chunk

    @functools.partial(
        pl.kernel, mesh=mesh, out_type=jax.ShapeDtypeStruct((n_idx, d), table.dtype),
        scratch_types=[pltpu.VMEM((chunk,), jnp.int32), pltpu.VMEM((chunk,), jnp.int32),
                       pltpu.VMEM((chunk, d), table.dtype), pltpu.VMEM((chunk, d), table.dtype),
                       pltpu.SemaphoreType.DMA, pltpu.SemaphoreType.DMA])
    def gather_rows(table_hbm, idx_hbm, out_hbm, i0, i1, r0, r1, sem0, sem1):
        worker = lax.axis_index("s") * n_cores + lax.axis_index("c")
        base = worker * per_worker
        slots = ((i0, r0, sem0), (i1, r1, sem1))

        def start(j):
            idx_v, rows_v, sem = slots[j % 2]
            off = pl.multiple_of(base + j * chunk, SUBLANES)
            pltpu.sync_copy(idx_hbm.at[pl.ds(off, chunk)], idx_v)
            return pltpu.async_copy(table_hbm.at[idx_v], rows_v, sem)

        nxt = start(0)
        for j in range(n_chunks):
            cur = nxt
            if j + 1 < n_chunks:
                nxt = start(j + 1)
            cur.wait()
            off = pl.multiple_of(base + j * chunk, SUBLANES)
            pltpu.sync_copy(slots[j % 2][1], out_hbm.at[pl.ds(off, chunk)])

    return gather_rows(table, idx)


def _finalize_kernel(xm_ref, ya_ref, yb_ref, wk_ref, fnw_ref, o_ref):
    d = xm_ref.shape[1]
    c = d // 2
    w = wk_ref[...]
    a_left, a_right = _unpack_pairs(ya_ref[...])
    b_left, b_right = _unpack_pairs(yb_ref[...])
    x_left = xm_ref[:, 0:c] + (w[:, 0:1] * a_left + w[:, 1:2] * b_left)
    x_right = xm_ref[:, c:d] + (w[:, 0:1] * a_right + w[:, 1:2] * b_right)
    ms = (jnp.sum(x_left * x_left, axis=-1, keepdims=True) + jnp.sum(x_right * x_right, axis=-1, keepdims=True)) / d
    scale = lax.rsqrt(ms + EPS)
    o_ref[:, 0:c] = x_left * scale * fnw_ref[:, 0:c]
    o_ref[:, c:d] = x_right * scale * fnw_ref[:, c:d]


def _finalize(xm, yk, wk, fnw, tm):
    n, d = xm.shape
    steps = n // tm
    return pl.pallas_call(
        _finalize_kernel,
        grid=(steps,),
        in_specs=[
            pl.BlockSpec((tm, d), lambda i: (i, 0)),
            pl.BlockSpec((tm, d // 2), lambda i: (i, 0)),
            pl.BlockSpec((tm, d // 2), lambda i: (i + steps, 0)),
            pl.BlockSpec((tm, SUBLANES), lambda i: (i, 0)),
            pl.BlockSpec((1, d), lambda i: (0, 0)),
        ],
        out_specs=pl.BlockSpec((tm, d), lambda i: (i, 0)),
        out_shape=jax.ShapeDtypeStruct((n, d), F32),
        compiler_params=pltpu.CompilerParams(dimension_semantics=("arbitrary",), vmem_limit_bytes=VMEM_LIMIT),
        name="finalize",
    )(xm, yk, yk, wk, fnw)


def _moe(h2, route, counts, wg, wu, wd, *, tm_e):
    n = h2.shape[0]
    n_rows = 2 * n + N_EXPERTS * tm_e
    n_tiles = n_rows // tm_e
    pos, tiles = _plan(route, counts, tm_e, n_tiles)
    xs = _sc_scatter2(h2, pos[0], pos[1], n_rows)
    y_sorted = _experts(tiles[0, :n_tiles], tiles[1, :n_tiles], xs, wg, wu, wd, tm_e)
    return _sc_gather(y_sorted, pos.reshape(-1))


def _in_weight_kernel(w_ref, o_ref):
    src_ba = POOL_DIM + CONV_DIM
    src_z = src_ba + 2 * N_HEADS
    src_gates = src_z + QK_DIM
    o_ref[:, OFF_Z:OFF_Z + QK_DIM] = w_ref[:, src_z:src_gates].astype(BF16)
    o_ref[:, OFF_QKV:OFF_QKV + CONV_DIM] = w_ref[:, POOL_DIM:src_ba].astype(BF16)
    o_ref[:, OFF_POOL:OFF_POOL + POOL_DIM] = w_ref[:, 0:POOL_DIM].astype(BF16)
    ba = w_ref[:, src_ba:src_ba + LANES]
    head_lane = lax.broadcasted_iota(jnp.int32, ba.shape, 1) < N_HEADS
    o_ref[:, OFF_B:OFF_B + LANES] = jnp.where(head_lane, ba, 0.0).astype(BF16)
    o_ref[:, OFF_A:OFF_A + LANES] = jnp.where(head_lane, pltpu.roll(ba, LANES - N_HEADS, axis=1), 0.0).astype(BF16)
    o_ref[:, MIX_COLS:IN_COLS] = w_ref[:, src_gates:src_gates + GATE_COLS].astype(BF16)


def _in_weight(w):
    d, cols = w.shape
    assert (POOL_DIM + CONV_DIM) % LANES == 0 and cols == POOL_DIM + CONV_DIM + 2 * N_HEADS + QK_DIM + GATE_COLS
    return pl.pallas_call(
        _in_weight_kernel,
        grid=(d // LANES,),
        in_specs=[pl.BlockSpec((LANES, cols), lambda i: (i, 0))],
        out_specs=pl.BlockSpec((LANES, IN_COLS), lambda i: (i, 0)),
        out_shape=jax.ShapeDtypeStruct((d, IN_COLS), BF16),
        compiler_params=pltpu.CompilerParams(dimension_semantics=("arbitrary",), vmem_limit_bytes=VMEM_LIMIT),
        name="in_weight",
    )(w)


def _layer(x_tokens, pool0, conv0, s0, prm, *, n_seq, sb, tt, tv, chunk, pos0, n_tiles, tm):
    conv_in_proj = tv == tt
    if conv_in_proj:
        proj, gates, nconv_proj = _inproj_conv(x_tokens.reshape(n_seq, n_tiles * tt, -1), conv0, prm["norm1_w"],
                                               prm["w_in"], prm["w_conv"], tm)
    else:
        proj, gates = _inproj(x_tokens, prm["norm1_w"], prm["w_in"], tm)
    ypool, ydelta, npool, nconv, ns = _mixers(
        proj, pool0, conv0, s0, prm["w_pool"], prm["pool_scale"], prm["w_conv"], prm["a_log"], prm["dt_bias"],
        prm["o_norm_w"], n_seq=n_seq, sb=sb, tt=tt, tv=tv, chunk=chunk, pos0=pos0, n_tiles=n_tiles,
        qkv_done=conv_in_proj)
    if conv_in_proj:
        nconv = nconv_proj
    xm, h2, route, wk, counts = _merge(
        x_tokens, gates, ypool, ydelta, prm["w_branch_pool"], prm["w_branch_delta"], prm["w_out"], prm["norm2_w"],
        prm["w_router_hi"], prm["w_router_lo"], prm["b_router"], 2 * tm, tt, tv)
    return xm, h2, route, wk, counts, npool, nconv, ns


def _pad_lanes(v, width=LANES):
    v = v.reshape(1, -1).astype(F32)
    return jnp.pad(v, ((0, 0), (0, width - v.shape[1])))


def kernel(x_prompt, x_sample, cache_pool, cache_conv, state_delta, norm1_w, w_in, w_pool, pool_scale, w_conv, a_log, dt_bias, o_norm_w, w_branch_pool, w_branch_delta, w_out, norm2_w, w_router_group, b_router_group, w_router_expert, b_router_expert, w_gate, w_up, w_down, final_norm_w):
    n_layers = norm1_w.shape[0]
    assert n_layers == 1, "single-layer step"
    bsz, seq, d = x_prompt.shape
    dbs, dseq, _ = x_sample.shape
    lyr = 0

    w_in_r = _in_weight(w_in[lyr])
    w_router = jnp.concatenate(
        [w_router_group[lyr], w_router_expert[lyr].reshape(d, N_EXPERTS),
         jnp.zeros((d, LANES - N_GROUPS - N_EXPERTS), F32)], axis=1).astype(F32)
    w_router_hi = w_router.astype(BF16)
    b_router = _pad_lanes(jnp.concatenate([b_router_group[lyr], b_router_expert[lyr].reshape(-1)]))
    prm = dict(
        norm1_w=norm1_w[lyr].reshape(1, d), w_in=w_in_r,
        w_pool=w_pool[lyr].astype(BF16), pool_scale=pool_scale[lyr].reshape(1, POOL_DIM),
        w_conv=jnp.pad(w_conv[lyr], ((0, SUBLANES - CONV_WIDTH), (0, 0))),
        a_log=_pad_lanes(a_log[lyr]), dt_bias=_pad_lanes(dt_bias[lyr]), o_norm_w=o_norm_w[lyr].reshape(1, HEAD_DIM),
        w_branch_pool=w_branch_pool[lyr].astype(BF16), w_branch_delta=w_branch_delta[lyr].astype(BF16),
        w_out=w_out[lyr].astype(BF16), norm2_w=norm2_w[lyr].reshape(1, d),
        w_router_hi=w_router_hi, w_router_lo=(w_router - w_router_hi.astype(F32)).astype(BF16), b_router=b_router,
    )
    wg, wu, wd = w_gate[lyr], w_up[lyr], w_down[lyr]
    fnw = final_norm_w.reshape(1, d)

    tt_p = TOKEN_TILE // 2
    xm, h2, route, wk, counts, npool, nconv, ns = _layer(
        x_prompt.reshape(bsz * seq, d),
        jnp.zeros((bsz, POOL_BUF, POOL_DIM), F32), jnp.zeros((bsz, CONV_WIDTH - 1, CONV_DIM), F32),
        jnp.zeros((bsz, N_HEADS, HEAD_DIM, HEAD_DIM), F32), prm,
        n_seq=bsz, sb=4, tt=tt_p, tv=tt_p, chunk=DELTA_CHUNK, pos0=0, n_tiles=seq // tt_p, tm=TOKEN_TILE)
    yk = _moe(h2, route, counts, wg, wu, wd, tm_e=2 * TOKEN_TILE)
    y_prompt = _finalize(xm, yk, wk, fnw, 2 * TOKEN_TILE).reshape(bsz, seq, d)
    pool_p = npool[None]
    conv_p = nconv[None]
    delta_p = ns[None]

    tt_s = SUBLANES
    xs_pad = jnp.pad(x_sample, ((0, 0), (0, tt_s - dseq), (0, 0))).reshape(dbs * tt_s, d)
    xm, h2, route, wk, counts, npool, nconv, ns = _layer(
        xs_pad, cache_pool[lyr], cache_conv[lyr], state_delta[lyr], prm,
        n_seq=dbs, sb=8, tt=tt_s, tv=dseq, chunk=tt_s, pos0=PAST_LEN, n_tiles=1, tm=TOKEN_TILE)
    real = lambda a: a.reshape(dbs, tt_s, -1)[:, :dseq].reshape(dbs * dseq, -1)
    route = route.reshape(SUBLANES, dbs, tt_s)[:, :, :dseq].reshape(SUBLANES, dbs * dseq)
    yk = _moe(real(h2), route, counts, wg, wu, wd, tm_e=TOKEN_TILE // 2)
    y_sample = _finalize(real(xm), yk, real(wk), fnw, TOKEN_TILE).reshape(dbs, dseq, d)
    pool_s = npool[None]
    conv_s = nconv[None]
    delta_s = ns[None]
    return (y_prompt, y_sample, pool_p, conv_p, delta_p, pool_s, conv_s, delta_s)
```

```python
import functools

import jax
import jax.numpy as jnp
from jax import lax
from jax.experimental import pallas as pl
from jax.experimental.pallas import tpu as pltpu
from jax.experimental.pallas import tpu_sc as plsc

F32 = jnp.float32
BF16 = jnp.bfloat16
U32 = jnp.uint32
EPS = 1e-6
HIGHEST = lax.Precision.HIGHEST

POOL_WINDOWS = (2, 4, 8, 16)
POOL_GROUP_DIM = 128
POOL_DIM = len(POOL_WINDOWS) * POOL_GROUP_DIM
POOL_BUF = max(POOL_WINDOWS) - 1
N_HEADS = 8
HEAD_DIM = 128
QK_DIM = N_HEADS * HEAD_DIM
CONV_WIDTH = 4
CONV_DIM = 3 * QK_DIM
N_GROUPS = 4
EXPERTS_PER_GROUP = 8
N_EXPERTS = N_GROUPS * EXPERTS_PER_GROUP
PAST_LEN = 16384
DELTA_CHUNK = 64
TOKEN_TILE = 256
LANES = 128
SUBLANES = 8

OFF_Z = 0
OFF_QKV = OFF_Z + QK_DIM
OFF_POOL = OFF_QKV + CONV_DIM
OFF_B = OFF_POOL + POOL_DIM
OFF_A = OFF_B + LANES
MIX_COLS = OFF_A + LANES
GATE_COLS = 2048
IN_COLS = MIX_COLS + GATE_COLS

POOL_ROWS = 16
CONV_ROWS = 8
VMEM_LIMIT = 60 * 1024 * 1024
EXPERT_PART_ROWS = 128
EXPERT_STAGES = 4
CONV_GROUPS = 12
SC_ROWS = 96


def _sigmoid(x):
    return 1.0 / (1.0 + jnp.exp(-x))


def _softplus(x):
    return jnp.maximum(x, 0.0) + jnp.log1p(jnp.exp(-jnp.abs(x)))


def _rms(x, w):
    return x * lax.rsqrt(jnp.mean(x * x, axis=-1, keepdims=True) + EPS) * w


def _dot(a, b):
    return jnp.dot(a.astype(BF16), b.astype(BF16), preferred_element_type=F32)


def _dot_nt(a, b):
    return lax.dot_general(a.astype(BF16), b.astype(BF16), (((1,), (1,)), ((), ())), preferred_element_type=F32)


def _dot_tn(a, b):
    return lax.dot_general(a.astype(BF16), b.astype(BF16), (((0,), (0,)), ((), ())), preferred_element_type=F32)


def _pack_pairs(x):
    c = x.shape[1] // 2
    hi = pltpu.bitcast(x[:, :c].astype(BF16).astype(F32), U32)
    lo = pltpu.bitcast(x[:, c:].astype(BF16).astype(F32), U32)
    return hi | lax.shift_right_logical(lo, jnp.uint32(16))


def _unpack_pairs(w):
    left = pltpu.bitcast(w & jnp.uint32(0xFFFF0000), F32)
    right = pltpu.bitcast(lax.shift_left(w, jnp.uint32(16)), F32)
    return left, right


def _inproj_kernel(x_ref, nw_ref, w_ref, mix_ref, gate_ref):
    h = _rms(x_ref[...], nw_ref[...]).astype(BF16)
    half = MIX_COLS // 2
    for c0 in (0, half):
        mix_ref[:, c0:c0 + half] = jnp.dot(h, w_ref[:, c0:c0 + half], preferred_element_type=F32)
    half = GATE_COLS // 2
    for c0 in (0, half):
        gate_ref[:, c0:c0 + half] = jnp.dot(h, w_ref[:, MIX_COLS + c0:MIX_COLS + c0 + half],
                                            preferred_element_type=F32)


def _inproj(x, norm_w, w_in_r, tm):
    n, d = x.shape
    return pl.pallas_call(
        _inproj_kernel,
        grid=(n // tm,),
        in_specs=[
            pl.BlockSpec((tm, d), lambda i: (i, 0)),
            pl.BlockSpec((1, d), lambda i: (0, 0)),
            pl.BlockSpec((d, IN_COLS), lambda i: (0, 0)),
        ],
        out_specs=[pl.BlockSpec((tm, MIX_COLS), lambda i: (i, 0)), pl.BlockSpec((tm, GATE_COLS), lambda i: (i, 0))],
        out_shape=[jax.ShapeDtypeStruct((n, MIX_COLS), F32), jax.ShapeDtypeStruct((n, GATE_COLS), F32)],
        compiler_params=pltpu.CompilerParams(dimension_semantics=("arbitrary",), vmem_limit_bytes=VMEM_LIMIT),
        name="inproj",
    )(x, norm_w, w_in_r)


def _conv_silu_qkv(ext_c, seq, tt, wconv_ref, ci, col0=0):
    base = CONV_ROWS - (CONV_WIDTH - 1)
    cs = slice(ci * LANES, (ci + 1) * LANES)
    cl = slice(ci * LANES - col0, (ci + 1) * LANES - col0)
    idx = (lambda r: (r, cl)) if seq is None else (lambda r: (seq, r, cl))
    acc = ext_c[idx(slice(base, base + tt))] * wconv_ref[0:1, cs]
    for j in range(1, CONV_WIDTH):
        acc = acc + ext_c[idx(slice(base + j, base + j + tt))] * wconv_ref[j:j + 1, cs]
    y = acc * _sigmoid(acc)
    if ci < 2 * N_HEADS:
        y = y * lax.rsqrt(jnp.sum(y * y, axis=-1, keepdims=True) + EPS)
    if ci < N_HEADS:
        y = y * (HEAD_DIM ** -0.5)
    return y


def _inproj_conv_kernel(x_ref, nw_ref, w_ref, wconv_ref, conv0_ref, mix_ref, gate_ref, nconv_ref, h_s, *ext, n_tiles):
    t = pl.program_id(1)
    tm = x_ref.shape[1]
    base = CONV_ROWS - (CONV_WIDTH - 1)
    step = CONV_DIM // len(ext)

    @pl.when(t == 0)
    def _():
        for g, buf in enumerate(ext):
            buf[base:CONV_ROWS, :] = conv0_ref[0, :, g * step:(g + 1) * step]

    h_s[...] = _rms(x_ref[0], nw_ref[...]).astype(BF16)

    def proj(c0, c1):
        return jnp.dot(h_s[...], w_ref[:, c0:c1], preferred_element_type=F32)

    def fill_z0():
        mix_ref[0, :, OFF_Z:OFF_Z + QK_DIM // 2] = proj(OFF_Z, OFF_Z + QK_DIM // 2)

    def fill_z1():
        mix_ref[0, :, OFF_Z + QK_DIM // 2:OFF_QKV] = proj(OFF_Z + QK_DIM // 2, OFF_QKV)

    def fill_pool():
        mix_ref[0, :, OFF_POOL:MIX_COLS] = proj(OFF_POOL, MIX_COLS)

    def fill_gate(i):
        q = GATE_COLS // 4
        gate_ref[0, :, i * q:(i + 1) * q] = proj(MIX_COLS + i * q, MIX_COLS + (i + 1) * q)

    fillers = [fill_z0, fill_z1, fill_pool] + [functools.partial(fill_gate, i) for i in range(4)]
    for g, buf in enumerate(ext):
        c0 = g * step
        buf[CONV_ROWS:CONV_ROWS + tm, :] = proj(OFF_QKV + c0, OFF_QKV + c0 + step)
        if fillers:
            fillers.pop(0)()
        for ci in range(c0 // LANES, (c0 + step) // LANES):
            mix_ref[0, :, OFF_QKV + ci * LANES:OFF_QKV + (ci + 1) * LANES] = _conv_silu_qkv(
                buf, None, tm, wconv_ref, ci, c0)
    for f in fillers:
        f()

    @pl.when(t == n_tiles - 1)
    def _():
        for g, buf in enumerate(ext):
            nconv_ref[0, :, g * step:(g + 1) * step] = buf[tm + base:tm + CONV_ROWS, :]

    for buf in ext:
        buf[0:CONV_ROWS, :] = buf[tm:tm + CONV_ROWS, :]


def _inproj_conv(x, conv0, norm_w, w_in_r, wconv, tm):
    n_seq, seq_len, d = x.shape
    n_tiles = seq_len // tm
    tile = lambda b, t: (b, t, 0)
    const = lambda b, t: (0, 0)
    mix, gates, nconv = pl.pallas_call(
        functools.partial(_inproj_conv_kernel, n_tiles=n_tiles),
        grid=(n_seq, n_tiles),
        in_specs=[
            pl.BlockSpec((1, tm, d), tile),
            pl.BlockSpec((1, d), const),
            pl.BlockSpec((d, IN_COLS), const),
            pl.BlockSpec((SUBLANES, CONV_DIM), const),
            pl.BlockSpec((1, CONV_WIDTH - 1, CONV_DIM), lambda b, t: (b, 0, 0)),
        ],
        out_specs=[
            pl.BlockSpec((1, tm, MIX_COLS), tile),
            pl.BlockSpec((1, tm, GATE_COLS), tile),
            pl.BlockSpec((1, CONV_WIDTH - 1, CONV_DIM), lambda b, t: (b, 0, 0)),
        ],
        out_shape=[
            jax.ShapeDtypeStruct((n_seq, seq_len, MIX_COLS), F32),
            jax.ShapeDtypeStruct((n_seq, seq_len, GATE_COLS), F32),
            jax.ShapeDtypeStruct((n_seq, CONV_WIDTH - 1, CONV_DIM), F32),
        ],
        scratch_shapes=[pltpu.VMEM((tm, d), BF16)] + [
            pltpu.VMEM((CONV_ROWS + tm, CONV_DIM // CONV_GROUPS), F32) for _ in range(CONV_GROUPS)],
        compiler_params=pltpu.CompilerParams(dimension_semantics=("arbitrary", "arbitrary"),
                                             vmem_limit_bytes=VMEM_LIMIT),
        name="inproj_conv",
    )(x, norm_w, w_in_r, wconv, conv0)
    n = n_seq * seq_len
    return mix.reshape(n, MIX_COLS), gates.reshape(n, GATE_COLS), nconv


def _mixer_kernel(proj_ref, pool0_ref, conv0_ref, s0_ref, wpool_ref, pscale_ref, wconv_ref, alog_ref, dtb_ref, onw_ref,
                  ypool_ref, ydelta_ref, npool_ref, nconv_ref, ns_ref,
                  ext_p, ext_c, q_s, k_s, v_s, g_s, b_s, state,
                  *, sb, tt, tv, chunk, pos0, n_tiles, qkv_done, caches_by_time):
    t = pl.program_id(1)
    base = CONV_ROWS - (CONV_WIDTH - 1)

    @pl.when(t == 0)
    def _():
        if caches_by_time:
            for j in range(POOL_BUF):
                ext_p[:, POOL_ROWS - POOL_BUF + j, :] = pool0_ref[j]
            for j in range(0 if qkv_done else CONV_WIDTH - 1):
                ext_c[:, base + j, :] = conv0_ref[j]
        else:
            ext_p[:, POOL_ROWS - POOL_BUF:POOL_ROWS, :] = pool0_ref[...]
            if not qkv_done:
                ext_c[:, base:CONV_ROWS, :] = conv0_ref[...]
        state[...] = s0_ref[...]

    row = lax.broadcasted_iota(jnp.int32, (tt, 1), 0)
    pos = pos0 + t * tt + row
    for s in range(sb):
        ext_p[s, POOL_ROWS:POOL_ROWS + tt, :] = proj_ref[s, :, OFF_POOL:OFF_POOL + POOL_DIM]
        if not qkv_done:
            ext_c[s, CONV_ROWS:CONV_ROWS + tt, :] = proj_ref[s, :, OFF_QKV:OFF_QKV + CONV_DIM]

        for gi, win in enumerate(POOL_WINDOWS):
            cs = slice(gi * POOL_GROUP_DIM, (gi + 1) * POOL_GROUP_DIM)
            u = ext_p[s, POOL_ROWS:POOL_ROWS + tt, cs]
            acc = u
            for j in range(1, win):
                acc = acc + ext_p[s, POOL_ROWS - j:POOL_ROWS - j + tt, cs]
            cnt = jnp.minimum(win, pos + 1).astype(F32)
            pooled = acc / cnt - u
            ypool_ref[s, :, cs] = (_dot(pooled, wpool_ref[gi]) * pscale_ref[:, cs]).astype(BF16)

        for ci in range(0 if qkv_done else CONV_DIM // LANES):
            hs = slice((ci % N_HEADS) * HEAD_DIM, (ci % N_HEADS + 1) * HEAD_DIM)
            (q_s, k_s, v_s)[ci // N_HEADS][s, :, hs] = _conv_silu_qkv(ext_c, s, tt, wconv_ref, ci)

        beta = _sigmoid(proj_ref[s, :, OFF_B:OFF_B + LANES])
        g = -jnp.exp(alog_ref[...]) * _softplus(proj_ref[s, :, OFF_A:OFF_A + LANES] + dtb_ref[...])
        if tv < tt:
            beta = jnp.where(row < tv, beta, 0.0)
            g = jnp.where(row < tv, g, 0.0)
        b_s[s] = beta
        g_s[s] = g

    @pl.when(t == n_tiles - 1)
    def _():
        if qkv_done:
            nconv_ref[...] = jnp.zeros_like(nconv_ref)
        if caches_by_time:
            for j in range(POOL_BUF):
                npool_ref[j] = ext_p[:, tv + POOL_ROWS - POOL_BUF + j, :]
            for j in range(0 if qkv_done else CONV_WIDTH - 1):
                nconv_ref[j] = ext_c[:, tv + base + j, :]
        else:
            npool_ref[...] = ext_p[:, tv + POOL_ROWS - POOL_BUF:tv + POOL_ROWS, :]
            if not qkv_done:
                nconv_ref[...] = ext_c[:, tv + base:tv + CONV_ROWS, :]

    if n_tiles > 1:
        ext_p[:, 0:POOL_ROWS, :] = ext_p[:, tt:tt + POOL_ROWS, :]
        if not qkv_done:
            ext_c[:, 0:CONV_ROWS, :] = ext_c[:, tt:tt + CONV_ROWS, :]

    ri = lax.broadcasted_iota(jnp.int32, (chunk, chunk), 0)
    cj = lax.broadcasted_iota(jnp.int32, (chunk, chunk), 1)
    causal = ri >= cj
    strict = ri > cj
    ltri = causal.astype(F32)
    eye = (ri == cj).astype(F32)
    n_doublings = chunk.bit_length() - 2
    units = [(s, h) for s in range(sb) for h in range(N_HEADS)]
    hsl = [slice(h * HEAD_DIM, (h + 1) * HEAD_DIM) for h in range(N_HEADS)]

    def qkv(which, s, rows, h):
        if qkv_done:
            c0 = OFF_QKV + which * QK_DIM + h * HEAD_DIM
            return proj_ref[s, rows, c0:c0 + HEAD_DIM]
        return (q_s, k_s, v_s)[which][s, rows, hsl[h]]

    def chunk_body(ci, carry):
        r0 = pl.multiple_of(ci * chunk, chunk)
        rows = pl.ds(r0, chunk)
        g_all, g_t, b_all = [], [], []
        for s in range(sb):
            ga = jnp.dot(ltri, g_s[s, rows, :], precision=HIGHEST, preferred_element_type=F32)
            g_all.append(ga)
            if chunk < LANES:
                ga = jnp.concatenate([ga, jnp.zeros((LANES - chunk, LANES), F32)], axis=0)
            g_t.append(ga.T)
            b_all.append(b_s[s, rows, :])
        g_col = [g_all[s][:, h:h + 1] for s, h in units]
        dec = [jnp.exp(jnp.minimum(g_col[i] - g_t[s][h:h + 1, 0:chunk], 0.0)) for i, (s, h) in enumerate(units)]
        b_col = [b_all[s][:, h:h + 1] for s, h in units]
        q = [qkv(0, s, rows, h) for s, h in units]
        k = [qkv(1, s, rows, h) for s, h in units]
        idx = range(len(units))
        kb = [k[i] * b_col[i] for i in idx]
        kbk = [_dot_nt(kb[i], k[i]) for i in idx]
        qk = [_dot_nt(q[i], k[i]) for i in idx]
        p_mat = [jnp.where(strict, kbk[i] * dec[i], 0.0) for i in idx]
        t_inv = [eye - p_mat[i] for i in idx]
        for _ in range(n_doublings):
            p_mat = [_dot(p_mat[i], p_mat[i]) for i in idx]
            xp = [_dot(t_inv[i], p_mat[i]) for i in idx]
            t_inv = [t_inv[i] + xp[i] for i in idx]
        g_exp = [jnp.exp(g_col[i]) for i in idx]
        wu = [_dot(t_inv[i], jnp.concatenate([kb[i] * g_exp[i], qkv(2, s, rows, h) * b_col[i]], axis=1))
              for i, (s, h) in enumerate(units)]
        s_old = [state[s, h] for s, h in units]
        res = [_dot(jnp.concatenate([wu[i][:, 0:HEAD_DIM], q[i] * g_exp[i]], axis=0), s_old[i]) for i in idx]
        v_new = [wu[i][:, HEAD_DIM:2 * HEAD_DIM] - res[i][0:chunk] for i in idx]
        a_qk = [jnp.where(causal, qk[i] * dec[i], 0.0) for i in idx]
        g_last = [g_all[s][chunk - 1:chunk, h:h + 1] for s, h in units]
        k_dec = [k[i] * jnp.exp(g_last[i] - g_col[i]) for i in idx]
        intra = [_dot(a_qk[i], v_new[i]) for i in idx]
        upd = [_dot_tn(k_dec[i], v_new[i]) for i in idx]
        for i, (s, h) in enumerate(units):
            state[s, h] = s_old[i] * jnp.exp(g_last[i]) + upd[i]
            o_c = res[i][chunk:2 * chunk] + intra[i]
            z_c = proj_ref[s, rows, OFF_Z + h * HEAD_DIM:OFF_Z + (h + 1) * HEAD_DIM]
            ydelta_ref[s, rows, hsl[h]] = (_rms(o_c, onw_ref[...]) * (z_c * _sigmoid(z_c))).astype(BF16)
        return carry

    lax.fori_loop(0, tt // chunk, chunk_body, 0)

    @pl.when(t == n_tiles - 1)
    def _():
        ns_ref[...] = state[...]


def _mixers(proj, pool0, conv0, s0, wpool, pscale, wconv, alog, dtb, onw, *, n_seq, sb, tt, tv, chunk, pos0, n_tiles,
            qkv_done):
    caches_by_time = sb % SUBLANES == 0
    kern = functools.partial(_mixer_kernel, sb=sb, tt=tt, tv=tv, chunk=chunk, pos0=pos0, n_tiles=n_tiles,
                             qkv_done=qkv_done, caches_by_time=caches_by_time)
    seq_len = n_tiles * tt
    tile = lambda b, t: (b, t, 0)
    seq3 = lambda b, t: (b, 0, 0)
    if caches_by_time:
        pool0, conv0 = jnp.swapaxes(pool0, 0, 1), jnp.swapaxes(conv0, 0, 1)
        cache_block = lambda rows, cols: pl.BlockSpec((rows, sb, cols), lambda b, t: (0, b, 0))
        cache_shape = lambda rows, cols: jax.ShapeDtypeStruct((rows, n_seq, cols), F32)
    else:
        cache_block = lambda rows, cols: pl.BlockSpec((sb, rows, cols), seq3)
        cache_shape = lambda rows, cols: jax.ShapeDtypeStruct((n_seq, rows, cols), F32)
    seq4 = lambda b, t: (b, 0, 0, 0)
    const2 = lambda b, t: (0, 0)
    ypool, ydelta, npool, nconv, ns = pl.pallas_call(
        kern,
        grid=(n_seq // sb, n_tiles),
        in_specs=[
            pl.BlockSpec((sb, tt, MIX_COLS), tile),
            cache_block(POOL_BUF, POOL_DIM),
            cache_block(CONV_WIDTH - 1, CONV_DIM),
            pl.BlockSpec((sb, N_HEADS, HEAD_DIM, HEAD_DIM), seq4),
            pl.BlockSpec((len(POOL_WINDOWS), POOL_GROUP_DIM, POOL_GROUP_DIM), lambda b, t: (0, 0, 0)),
            pl.BlockSpec((1, POOL_DIM), const2),
            pl.BlockSpec((SUBLANES, CONV_DIM), const2),
            pl.BlockSpec((1, LANES), const2),
            pl.BlockSpec((1, LANES), const2),
            pl.BlockSpec((1, HEAD_DIM), const2),
        ],
        out_specs=[
            pl.BlockSpec((sb, tt, POOL_DIM), tile),
            pl.BlockSpec((sb, tt, QK_DIM), tile),
            cache_block(POOL_BUF, POOL_DIM),
            cache_block(CONV_WIDTH - 1, CONV_DIM),
            pl.BlockSpec((sb, N_HEADS, HEAD_DIM, HEAD_DIM), seq4),
        ],
        out_shape=[
            jax.ShapeDtypeStruct((n_seq, seq_len, POOL_DIM), BF16),
            jax.ShapeDtypeStruct((n_seq, seq_len, QK_DIM), BF16),
            cache_shape(POOL_BUF, POOL_DIM),
            cache_shape(CONV_WIDTH - 1, CONV_DIM),
            jax.ShapeDtypeStruct((n_seq, N_HEADS, HEAD_DIM, HEAD_DIM), F32),
        ],
        scratch_shapes=[
            pltpu.VMEM((sb, POOL_ROWS + tt, POOL_DIM), F32),
            pltpu.VMEM((sb, CONV_ROWS + tt, CONV_DIM), F32),
            pltpu.VMEM((sb, tt, QK_DIM), F32),
            pltpu.VMEM((sb, tt, QK_DIM), F32),
            pltpu.VMEM((sb, tt, QK_DIM), F32),
            pltpu.VMEM((sb, tt, LANES), F32),
            pltpu.VMEM((sb, tt, LANES), F32),
            pltpu.VMEM((sb, N_HEADS, HEAD_DIM, HEAD_DIM), F32),
        ],
        compiler_params=pltpu.CompilerParams(dimension_semantics=("arbitrary", "arbitrary"),
                                             vmem_limit_bytes=VMEM_LIMIT),
        name="mixers",
    )(proj.reshape(n_seq, seq_len, MIX_COLS), pool0, conv0, s0, wpool, pscale, wconv, alog, dtb, onw)
    n = n_seq * seq_len
    if caches_by_time:
        npool, nconv = jnp.swapaxes(npool, 0, 1), jnp.swapaxes(nconv, 0, 1)
    return ypool.reshape(n, POOL_DIM), ydelta.reshape(n, QK_DIM), npool, nconv, ns


def _merge_kernel(x_ref, gates_ref, ypool_ref, ydelta_ref, wbp_ref, wbd_ref, wout_ref, n2w_ref, wr_hi_ref, wr_lo_ref,
                  br_ref, xm_ref, h2_ref, route_ref, wk_ref, cnt_ref, cnt_s, logits_s, *, period, valid):
    tm = x_ref.shape[0]
    step = pl.program_id(0)

    @pl.when(step == 0)
    def _():
        cnt_s[...] = jnp.zeros_like(cnt_s)
        logits_s[...] = jnp.zeros_like(logits_s)

    logits = logits_s[...]
    bp = _dot(ypool_ref[...], wbp_ref[...])
    bd = _dot(ydelta_ref[...], wbd_ref[...])

    lane = lax.broadcasted_iota(jnp.int32, logits.shape, 1)
    neg = -jnp.inf
    far = LANES - 1
    is_g = lane < N_GROUPS
    g_max = jnp.max(jnp.where(is_g, logits, neg), axis=-1, keepdims=True)
    g_sel = jnp.min(jnp.where(is_g & (logits == g_max), lane, far), axis=-1, keepdims=True)
    p_g = 1.0 / jnp.sum(jnp.where(is_g, jnp.exp(logits - g_max), 0.0), axis=-1, keepdims=True)
    e_lane = lane - N_GROUPS
    is_e = (e_lane >= 0) & (e_lane < N_EXPERTS) & ((e_lane // EXPERTS_PER_GROUP) == g_sel)
    ev = jnp.where(is_e, logits, neg)
    v1 = jnp.max(ev, axis=-1, keepdims=True)
    i1 = jnp.min(jnp.where(is_e & (ev == v1), lane, far), axis=-1, keepdims=True)
    is_e2 = is_e & (lane != i1)
    ev2 = jnp.where(is_e2, logits, neg)
    v2 = jnp.max(ev2, axis=-1, keepdims=True)
    i2 = jnp.min(jnp.where(is_e2 & (ev2 == v2), lane, far), axis=-1, keepdims=True)
    e21 = jnp.exp(v2 - v1)
    w1 = p_g / (1.0 + e21)
    w2 = p_g * e21 / (1.0 + e21)
    e1 = i1 - N_GROUPS
    e2 = i2 - N_GROUPS
    chosen = ((lane == e1) | (lane == e2)) & (step > 0)
    if valid < period:
        row = lax.broadcasted_iota(jnp.int32, (tm, 1), 0)
        chosen = chosen & (lax.rem(row, period) < valid)
    onehot = jnp.where(chosen, 1.0, 0.0)

    rr = lax.broadcasted_iota(jnp.int32, (tm, tm), 0)
    cc = lax.broadcasted_iota(jnp.int32, (tm, tm), 1)
    earlier = jnp.where(rr > cc, 1.0, 0.0).astype(BF16)
    before = jnp.dot(earlier, onehot.astype(BF16), preferred_element_type=F32) + cnt_s[...]
    r1 = jnp.sum(jnp.where(lane == e1, before, 0.0), axis=-1, keepdims=True)
    r2 = jnp.sum(jnp.where(lane == e2, before, 0.0), axis=-1, keepdims=True)
    cnt_s[...] = cnt_s[...] + jnp.sum(onehot, axis=0, keepdims=True)
    cnt_ref[...] = cnt_s[...]
    record = jnp.where(lane == 0, e1.astype(F32), jnp.where(lane == 1, e2.astype(F32),
                                                           jnp.where(lane == 2, r1, jnp.where(lane == 3, r2, 0.0))))
    route_ref[...] = record.T[0:SUBLANES, :]
    lane8 = lax.broadcasted_iota(jnp.int32, (tm, SUBLANES), 1)
    wk_ref[...] = jnp.where(lane8 == 0, w1, jnp.where(lane8 == 1, w2, 0.0))

    merged = _sigmoid(gates_ref[:, 0:1024]) * bp + _sigmoid(gates_ref[:, 1024:2048]) * bd
    xm = x_ref[...] + _dot(merged, wout_ref[...])
    xm_ref[...] = xm
    h2 = _rms(xm, n2w_ref[...])
    h2_ref[...] = _pack_pairs(h2)
    h2_hi = h2.astype(BF16)
    h2_lo = (h2 - h2_hi.astype(F32)).astype(BF16)
    logits_s[...] = (jnp.dot(h2_hi, wr_hi_ref[...], preferred_element_type=F32)
                     + (jnp.dot(h2_hi, wr_lo_ref[...], preferred_element_type=F32)
                        + jnp.dot(h2_lo, wr_hi_ref[...], preferred_element_type=F32))) + br_ref[...]


def _merge(x, proj, ypool, ydelta, wbp, wbd, wout, n2w, wr_hi, wr_lo, br, tm, period, valid):
    n, d = x.shape
    last = n // tm - 1
    row = lambda i: (jnp.minimum(i, last), 0)
    chosen_row = lambda i: (jnp.maximum(i - 1, 0), 0)
    const = lambda i: (0, 0)
    return pl.pallas_call(
        functools.partial(_merge_kernel, period=period, valid=valid),
        grid=(n // tm + 1,),
        in_specs=[
            pl.BlockSpec((tm, d), row),
            pl.BlockSpec((tm, GATE_COLS), row),
            pl.BlockSpec((tm, POOL_DIM), row),
            pl.BlockSpec((tm, QK_DIM), row),
            pl.BlockSpec((POOL_DIM, d), const),
            pl.BlockSpec((QK_DIM, d), const),
            pl.BlockSpec((d, d), const),
            pl.BlockSpec((1, d), const),
            pl.BlockSpec((d, LANES), const),
            pl.BlockSpec((d, LANES), const),
            pl.BlockSpec((1, LANES), const),
        ],
        out_specs=[
            pl.BlockSpec((tm, d), row),
            pl.BlockSpec((tm, d // 2), row),
            pl.BlockSpec((SUBLANES, tm), lambda i: (0, jnp.maximum(i - 1, 0))),
            pl.BlockSpec((tm, SUBLANES), chosen_row),
            pl.BlockSpec((1, LANES), const),
        ],
        out_shape=[
            jax.ShapeDtypeStruct((n, d), F32),
            jax.ShapeDtypeStruct((n, d // 2), U32),
            jax.ShapeDtypeStruct((SUBLANES, n), F32),
            jax.ShapeDtypeStruct((n, SUBLANES), F32),
            jax.ShapeDtypeStruct((1, LANES), F32),
        ],
        scratch_shapes=[pltpu.VMEM((1, LANES), F32), pltpu.VMEM((tm, LANES), F32)],
        compiler_params=pltpu.CompilerParams(dimension_semantics=("arbitrary",), vmem_limit_bytes=VMEM_LIMIT),
        name="merge",
    )(x, proj, ypool, ydelta, wbp, wbd, wout, n2w, wr_hi, wr_lo, br)


def _expert_kernel(te_ref, nv_ref, x_ref, wg_hbm, wu_hbm, wd_hbm, y_ref, st_g, st_u, st_d, wg_s, wu_s, wd_s, sem, slot_s):
    i = pl.program_id(0)
    n_tiles = pl.num_programs(0)
    n_valid = nv_ref[i]
    expert = te_ref[i]

    def expert_at(j):
        return te_ref[jnp.minimum(j, n_tiles - 1)]

    def next_expert_tile(j):
        e = expert_at(j)
        return lax.while_loop(lambda t: (t < n_tiles) & (expert_at(t) == e), lambda t: t + 1, j + 1)

    def start_weight_copies(j, slot):
        @pl.when(j < n_tiles)
        def _():
            for cp in weight_copies(expert_at(j), slot):
                cp.start()

    def weight_copies(e, slot):
        return [pltpu.make_async_copy(src.at[e], dst.at[slot], sem.at[slot])
                for src, dst in ((wg_hbm, st_g), (wu_hbm, st_u), (wd_hbm, st_d))]

    @pl.when(i == 0)
    def _():
        slot_s[0] = 0
        j = i
        for slot in range(EXPERT_STAGES - 1):
            start_weight_copies(j, slot)
            j = next_expert_tile(j)

    @pl.when((i == 0) | (expert != te_ref[jnp.maximum(i - 1, 0)]))
    def _():
        slot = slot_s[0]
        j = i
        for _ in range(EXPERT_STAGES - 1):
            j = next_expert_tile(j)
        start_weight_copies(j, (slot + EXPERT_STAGES - 1) % EXPERT_STAGES)
        for cp in weight_copies(expert, slot):
            cp.wait()
        wg_s[...] = st_g[slot].astype(BF16)
        wu_s[...] = st_u[slot].astype(BF16)
        wd_s[...] = st_d[slot].astype(BF16)
        slot_s[0] = (slot + 1) % EXPERT_STAGES

    @pl.when(n_valid > 0)
    def _():
        tm = x_ref.shape[0]
        part = min(tm, EXPERT_PART_ROWS)
        parts = [slice(r0, r0 + part) for r0 in range(0, tm, part)]
        row = lax.broadcasted_iota(jnp.int32, (part, 1), 0)
        xs = []
        for p in parts:
            left, right = _unpack_pairs(jnp.where(row + p.start < n_valid, x_ref[p, :], jnp.uint32(0)))
            xs.append(jnp.concatenate([left.astype(BF16), right.astype(BF16)], axis=1))
        a = [_dot(x, wg_s[...]) for x in xs]
        b = [_dot(x, wu_s[...]) for x in xs]
        act = [(a[j] * _sigmoid(a[j])) * b[j] for j in range(len(parts))]
        y = [_dot(act[j], wd_s[...]) for j in range(len(parts))]
        for j, p in enumerate(parts):
            y_ref[p, :] = _pack_pairs(y[j])

    @pl.when(n_valid == 0)
    def _():
        y_ref[...] = jnp.zeros_like(y_ref)


def _experts(tile_expert, tile_valid, xs, wg, wu, wd, tm):
    n_rows, dp = xs.shape
    d, f = wg.shape[1], wg.shape[2]
    grid_spec = pltpu.PrefetchScalarGridSpec(
        num_scalar_prefetch=2,
        grid=(n_rows // tm,),
        in_specs=[
            pl.BlockSpec((tm, dp), lambda i, te, nv: (jnp.where(nv[i] > 0, i, 0), 0)),
            pl.BlockSpec(memory_space=pl.ANY),
            pl.BlockSpec(memory_space=pl.ANY),
            pl.BlockSpec(memory_space=pl.ANY),
        ],
        out_specs=pl.BlockSpec((tm, dp), lambda i, te, nv: (i, 0)),
        scratch_shapes=[pltpu.VMEM((EXPERT_STAGES, d, f), F32), pltpu.VMEM((EXPERT_STAGES, d, f), F32),
                        pltpu.VMEM((EXPERT_STAGES, f, d), F32),
                        pltpu.VMEM((d, f), BF16), pltpu.VMEM((d, f), BF16), pltpu.VMEM((f, d), BF16),
                        pltpu.SemaphoreType.DMA((EXPERT_STAGES,)), pltpu.SMEM((1,), jnp.int32)],
    )
    return pl.pallas_call(
        _expert_kernel,
        grid_spec=grid_spec,
        out_shape=jax.ShapeDtypeStruct((n_rows, dp), U32),
        compiler_params=pltpu.CompilerParams(dimension_semantics=("arbitrary",), vmem_limit_bytes=VMEM_LIMIT),
        name="experts",
    )(tile_expert, tile_valid, xs, wg, wu, wd)


def _plan_kernel(route_ref, cnt_ref, pos_ref, tiles_ref, *, tm_e):
    counts = cnt_ref[...]
    padded = jnp.floor((counts + (tm_e - 1)) / tm_e) * tm_e
    lane = lax.broadcasted_iota(jnp.int32, counts.shape, 1)
    ends = padded
    shift = 1
    while shift < N_EXPERTS:
        ends = ends + jnp.where(lane >= shift, pltpu.roll(ends, shift, axis=1), 0.0)
        shift *= 2
    starts = ends - padded

    def lookup(table, key):
        out = jnp.zeros_like(key)
        for e in range(N_EXPERTS):
            out = out + jnp.where(key == e, table[:, e:e + 1], 0.0)
        return out

    for k in range(2):
        pos = lookup(starts, route_ref[k:k + 1, :]) + route_ref[2 + k:3 + k, :]
        pos_ref[k:k + 1, :] = pos.astype(jnp.int32)

    tile_start = lax.broadcasted_iota(jnp.int32, (1, tiles_ref.shape[1]), 1).astype(F32) * tm_e
    tile_expert = jnp.zeros_like(tile_start)
    for e in range(N_EXPERTS):
        tile_expert = tile_expert + jnp.where(ends[:, e:e + 1] <= tile_start, 1.0, 0.0)
    tile_expert = jnp.minimum(tile_expert, N_EXPERTS - 1.0)
    valid = lookup(counts, tile_expert) - (tile_start - lookup(starts, tile_expert))
    tiles_ref[0:1, :] = tile_expert.astype(jnp.int32)
    tiles_ref[1:2, :] = jnp.clip(valid, 0.0, tm_e).astype(jnp.int32)


def _plan(route_t, counts, tm_e, n_tiles):
    n = route_t.shape[1]
    tiles_pad = -(-n_tiles // LANES) * LANES
    return pl.pallas_call(
        functools.partial(_plan_kernel, tm_e=tm_e),
        out_shape=[jax.ShapeDtypeStruct((2, n), jnp.int32), jax.ShapeDtypeStruct((2, tiles_pad), jnp.int32)],
        name="plan",
    )(route_t, counts)


def _sc_workers():
    info = plsc.get_sparse_core_info()
    return info.num_cores, info.num_subcores


def _sc_chunk(per_worker):
    assert per_worker % SUBLANES == 0
    return max(c for c in range(SUBLANES, SC_ROWS + 1, SUBLANES) if per_worker % c == 0)


def _sc_scatter2(rows, idx_a, idx_b, n_out):
    n_cores, n_sub = _sc_workers()
    n, d = rows.shape
    per_worker = n // (n_cores * n_sub)
    assert per_worker * n_cores * n_sub == n
    chunk = _sc_chunk(per_worker)
    mesh = plsc.VectorSubcoreMesh(core_axis_name="c", subcore_axis_name="s")

    n_chunks = per_worker // chunk
    buf = lambda shape, dtype: [pltpu.VMEM(shape, dtype), pltpu.VMEM(shape, dtype)]

    @functools.partial(
        pl.kernel, mesh=mesh, out_type=jax.ShapeDtypeStruct((n_out, d), rows.dtype),
        scratch_types=buf((chunk,), jnp.int32) + buf((chunk,), jnp.int32) + buf((chunk, d), rows.dtype)
        + [pltpu.SemaphoreType.DMA, pltpu.SemaphoreType.DMA])
    def scatter_rows(rows_hbm, ia_hbm, ib_hbm, out_hbm, ia0, ia1, ib0, ib1, r0, r1, sem0, sem1):
        worker = lax.axis_index("s") * n_cores + lax.axis_index("c")
        base = worker * per_worker
        slots = ((ia0, ib0, r0, sem0), (ia1, ib1, r1, sem1))
        pending = {}
        for j in range(n_chunks):
            ia_v, ib_v, rows_v, sem = slots[j % 2]
            if j >= 2:
                for cp in pending.pop(j - 2):
                    cp.wait()
            off = pl.multiple_of(base + j * chunk, SUBLANES)
            pltpu.sync_copy(ia_hbm.at[pl.ds(off, chunk)], ia_v)
            pltpu.sync_copy(ib_hbm.at[pl.ds(off, chunk)], ib_v)
            pltpu.sync_copy(rows_hbm.at[pl.ds(off, chunk)], rows_v)
            pending[j] = (pltpu.async_copy(rows_v, out_hbm.at[ia_v], sem),
                          pltpu.async_copy(rows_v, out_hbm.at[ib_v], sem))
        for j in sorted(pending):
            for cp in pending[j]:
                cp.wait()

    return scatter_rows(rows, idx_a, idx_b)


def _sc_gather(table, idx):
    n_cores, n_sub = _sc_workers()
    n_idx = idx.shape[0]
    d = table.shape[1]
    per_worker = n_idx // (n_cores * n_sub)
    assert per_worker * n_cores * n_sub == n_idx
    chunk = _sc_chunk(per_worker)
    mesh = plsc.VectorSubcoreMesh(core_axis_name="c", subcore_axis_name="s")

    n_chunks = per_worker // chunk

    @functools.partial(
        pl.kernel, mesh=mesh, out_type=jax.ShapeDtypeStruct((n_idx, d), table.dtype),
        scratch_types=[pltpu.VMEM((chunk,), jnp.int32), pltpu.VMEM((chunk,), jnp.int32),
                       pltpu.VMEM((chunk, d), table.dtype), pltpu.VMEM((chunk, d), table.dtype),
                       pltpu.SemaphoreType.DMA, pltpu.SemaphoreType.DMA])
    def gather_rows(table_hbm, idx_hbm, out_hbm, i0, i1, r0, r1, sem0, sem1):
        worker = lax.axis_index("s") * n_cores + lax.axis_index("c")
        base = worker * per_worker
        slots = ((i0, r0, sem0), (i1, r1, sem1))

        def start(j):
            idx_v, rows_v, sem = slots[j % 2]
            off = pl.multiple_of(base + j * chunk, SUBLANES)
            pltpu.sync_copy(idx_hbm.at[pl.ds(off, chunk)], idx_v)
            return pltpu.async_copy(table_hbm.at[idx_v], rows_v, sem)

        nxt = start(0)
        for j in range(n_chunks):
            cur = nxt
            if j + 1 < n_chunks:
                nxt = start(j + 1)
            cur.wait()
            off = pl.multiple_of(base + j * chunk, SUBLANES)
            pltpu.sync_copy(slots[j % 2][1], out_hbm.at[pl.ds(off, chunk)])

    return gather_rows(table, idx)


def _finalize_kernel(xm_ref, ya_ref, yb_ref, wk_ref, fnw_ref, o_ref):
    d = xm_ref.shape[1]
    c = d // 2
    w = wk_ref[...]
    a_left, a_right = _unpack_pairs(ya_ref[...])
    b_left, b_right = _unpack_pairs(yb_ref[...])
    x_left = xm_ref[:, 0:c] + (w[:, 0:1] * a_left + w[:, 1:2] * b_left)
    x_right = xm_ref[:, c:d] + (w[:, 0:1] * a_right + w[:, 1:2] * b_right)
    ms = (jnp.sum(x_left * x_left, axis=-1, keepdims=True) + jnp.sum(x_right * x_right, axis=-1, keepdims=True)) / d
    scale = lax.rsqrt(ms + EPS)
    o_ref[:, 0:c] = x_left * scale * fnw_ref[:, 0:c]
    o_ref[:, c:d] = x_right * scale * fnw_ref[:, c:d]


def _finalize(xm, yk, wk, fnw, tm):
    n, d = xm.shape
    steps = n // tm
    return pl.pallas_call(
        _finalize_kernel,
        grid=(steps,),
        in_specs=[
            pl.BlockSpec((tm, d), lambda i: (i, 0)),
            pl.BlockSpec((tm, d // 2), lambda i: (i, 0)),
            pl.BlockSpec((tm, d // 2), lambda i: (i + steps, 0)),
            pl.BlockSpec((tm, SUBLANES), lambda i: (i, 0)),
            pl.BlockSpec((1, d), lambda i: (0, 0)),
        ],
        out_specs=pl.BlockSpec((tm, d), lambda i: (i, 0)),
        out_shape=jax.ShapeDtypeStruct((n, d), F32),
        compiler_params=pltpu.CompilerParams(dimension_semantics=("arbitrary",), vmem_limit_bytes=VMEM_LIMIT),
        name="finalize",
    )(xm, yk, yk, wk, fnw)


def _moe(h2, route, counts, wg, wu, wd, *, tm_e):
    n = h2.shape[0]
    n_rows = 2 * n + N_EXPERTS * tm_e
    n_tiles = n_rows // tm_e
    pos, tiles = _plan(route, counts, tm_e, n_tiles)
    xs = _sc_scatter2(h2, pos[0], pos[1], n_rows)
    y_sorted = _experts(tiles[0, :n_tiles], tiles[1, :n_tiles], xs, wg, wu, wd, tm_e)
    return _sc_gather(y_sorted, pos.reshape(-1))


def _in_weight_kernel(wt_ref, o_ref):
    src_ba = POOL_DIM + CONV_DIM
    src_z = src_ba + 2 * N_HEADS
    src_gates = src_z + QK_DIM
    o_ref[:, OFF_Z:OFF_Z + QK_DIM] = wt_ref[src_z:src_gates, :].T.astype(BF16)
    o_ref[:, OFF_QKV:OFF_QKV + CONV_DIM] = wt_ref[POOL_DIM:src_ba, :].T.astype(BF16)
    o_ref[:, OFF_POOL:OFF_POOL + POOL_DIM] = wt_ref[0:POOL_DIM, :].T.astype(BF16)
    ba = wt_ref[src_ba:src_ba + LANES, :].T
    head_lane = lax.broadcasted_iota(jnp.int32, ba.shape, 1) < N_HEADS
    o_ref[:, OFF_B:OFF_B + LANES] = jnp.where(head_lane, ba, 0.0).astype(BF16)
    o_ref[:, OFF_A:OFF_A + LANES] = jnp.where(head_lane, pltpu.roll(ba, LANES - N_HEADS, axis=1), 0.0).astype(BF16)
    o_ref[:, MIX_COLS:IN_COLS] = wt_ref[src_gates:src_gates + GATE_COLS, :].T.astype(BF16)


def _in_weight(wt):
    cols, d = wt.shape
    assert (POOL_DIM + CONV_DIM) % LANES == 0 and cols == POOL_DIM + CONV_DIM + 2 * N_HEADS + QK_DIM + GATE_COLS
    return pl.pallas_call(
        _in_weight_kernel,
        grid=(d // LANES,),
        in_specs=[pl.BlockSpec((cols, LANES), lambda i: (0, i))],
        out_specs=pl.BlockSpec((LANES, IN_COLS), lambda i: (i, 0)),
        out_shape=jax.ShapeDtypeStruct((d, IN_COLS), BF16),
        compiler_params=pltpu.CompilerParams(dimension_semantics=("arbitrary",), vmem_limit_bytes=VMEM_LIMIT),
        name="in_weight",
    )(wt)


def _layer(x_tokens, pool0, conv0, s0, prm, *, n_seq, sb, tt, tv, chunk, pos0, n_tiles, tm):
    conv_in_proj = tv == tt
    if conv_in_proj:
        proj, gates, nconv_proj = _inproj_conv(x_tokens.reshape(n_seq, n_tiles * tt, -1), conv0, prm["norm1_w"],
                                               prm["w_in"], prm["w_conv"], tm)
    else:
        proj, gates = _inproj(x_tokens, prm["norm1_w"], prm["w_in"], tm)
    ypool, ydelta, npool, nconv, ns = _mixers(
        proj, pool0, conv0, s0, prm["w_pool"], prm["pool_scale"], prm["w_conv"], prm["a_log"], prm["dt_bias"],
        prm["o_norm_w"], n_seq=n_seq, sb=sb, tt=tt, tv=tv, chunk=chunk, pos0=pos0, n_tiles=n_tiles,
        qkv_done=conv_in_proj)
    if conv_in_proj:
        nconv = nconv_proj
    xm, h2, route, wk, counts = _merge(
        x_tokens, gates, ypool, ydelta, prm["w_branch_pool"], prm["w_branch_delta"], prm["w_out"], prm["norm2_w"],
        prm["w_router_hi"], prm["w_router_lo"], prm["b_router"], 2 * tm, tt, tv)
    return xm, h2, route, wk, counts, npool, nconv, ns


def _pad_lanes(v, width=LANES):
    v = v.reshape(1, -1).astype(F32)
    return jnp.pad(v, ((0, 0), (0, width - v.shape[1])))


def kernel(x_prompt, x_sample, cache_pool, cache_conv, state_delta, norm1_w, w_in, w_pool, pool_scale, w_conv, a_log, dt_bias, o_norm_w, w_branch_pool, w_branch_delta, w_out, norm2_w, w_router_group, b_router_group, w_router_expert, b_router_expert, w_gate, w_up, w_down, final_norm_w):
    n_layers = norm1_w.shape[0]
    assert n_layers == 1, "single-layer step"
    bsz, seq, d = x_prompt.shape
    dbs, dseq, _ = x_sample.shape
    lyr = 0

    w_in_r = _in_weight(jnp.swapaxes(w_in[lyr], 0, 1))
    w_router = jnp.concatenate(
        [w_router_group[lyr], w_router_expert[lyr].reshape(d, N_EXPERTS),
         jnp.zeros((d, LANES - N_GROUPS - N_EXPERTS), F32)], axis=1).astype(F32)
    w_router_hi = w_router.astype(BF16)
    b_router = _pad_lanes(jnp.concatenate([b_router_group[lyr], b_router_expert[lyr].reshape(-1)]))
    prm = dict(
        norm1_w=norm1_w[lyr].reshape(1, d), w_in=w_in_r,
        w_pool=w_pool[lyr].astype(BF16), pool_scale=pool_scale[lyr].reshape(1, POOL_DIM),
        w_conv=jnp.pad(w_conv[lyr], ((0, SUBLANES - CONV_WIDTH), (0, 0))),
        a_log=_pad_lanes(a_log[lyr]), dt_bias=_pad_lanes(dt_bias[lyr]), o_norm_w=o_norm_w[lyr].reshape(1, HEAD_DIM),
        w_branch_pool=w_branch_pool[lyr].astype(BF16), w_branch_delta=w_branch_delta[lyr].astype(BF16),
        w_out=w_out[lyr].astype(BF16), norm2_w=norm2_w[lyr].reshape(1, d),
        w_router_hi=w_router_hi, w_router_lo=(w_router - w_router_hi.astype(F32)).astype(BF16), b_router=b_router,
    )
    wg, wu, wd = w_gate[lyr], w_up[lyr], w_down[lyr]
    fnw = final_norm_w.reshape(1, d)

    tt_p = TOKEN_TILE // 2
    xm, h2, route, wk, counts, npool, nconv, ns = _layer(
        x_prompt.reshape(bsz * seq, d),
        jnp.zeros((bsz, POOL_BUF, POOL_DIM), F32), jnp.zeros((bsz, CONV_WIDTH - 1, CONV_DIM), F32),
        jnp.zeros((bsz, N_HEADS, HEAD_DIM, HEAD_DIM), F32), prm,
        n_seq=bsz, sb=4, tt=tt_p, tv=tt_p, chunk=DELTA_CHUNK, pos0=0, n_tiles=seq // tt_p, tm=TOKEN_TILE)
    yk = _moe(h2, route, counts, wg, wu, wd, tm_e=2 * TOKEN_TILE)
    y_prompt = _finalize(xm, yk, wk, fnw, 2 * TOKEN_TILE).reshape(bsz, seq, d)
    pool_p = npool[None]
    conv_p = nconv[None]
    delta_p = ns[None]

    tt_s = SUBLANES
    xs_pad = jnp.pad(x_sample, ((0, 0), (0, tt_s - dseq), (0, 0))).reshape(dbs * tt_s, d)
    xm, h2, route, wk, counts, npool, nconv, ns = _layer(
        xs_pad, cache_pool[lyr], cache_conv[lyr], state_delta[lyr], prm,
        n_seq=dbs, sb=8, tt=tt_s, tv=dseq, chunk=tt_s, pos0=PAST_LEN, n_tiles=1, tm=TOKEN_TILE)
    real = lambda a: a.reshape(dbs, tt_s, -1)[:, :dseq].reshape(dbs * dseq, -1)
    route = route.reshape(SUBLANES, dbs, tt_s)[:, :, :dseq].reshape(SUBLANES, dbs * dseq)
    yk = _moe(real(h2), route, counts, wg, wu, wd, tm_e=TOKEN_TILE // 2)
    y_sample = _finalize(real(xm), yk, real(wk), fnw, TOKEN_TILE).reshape(dbs, dseq, d)
    pool_s = npool[None]
    conv_s = nconv[None]
    delta_s = ns[None]
    return (y_prompt, y_sample, pool_p, conv_p, delta_p, pool_s, conv_s, delta_s)
```

```python
import functools

import jax
import jax.numpy as jnp
from jax import lax
from jax.experimental import pallas as pl
from jax.experimental.pallas import tpu as pltpu
from jax.experimental.pallas import tpu_sc as plsc

F32 = jnp.float32
BF16 = jnp.bfloat16
U32 = jnp.uint32
EPS = 1e-6
HIGHEST = lax.Precision.HIGHEST

POOL_WINDOWS = (2, 4, 8, 16)
POOL_GROUP_DIM = 128
POOL_DIM = len(POOL_WINDOWS) * POOL_GROUP_DIM
POOL_BUF = max(POOL_WINDOWS) - 1
N_HEADS = 8
HEAD_DIM = 128
QK_DIM = N_HEADS * HEAD_DIM
CONV_WIDTH = 4
CONV_DIM = 3 * QK_DIM
N_GROUPS = 4
EXPERTS_PER_GROUP = 8
N_EXPERTS = N_GROUPS * EXPERTS_PER_GROUP
PAST_LEN = 16384
DELTA_CHUNK = 64
TOKEN_TILE = 256
LANES = 128
SUBLANES = 8

OFF_Z = 0
OFF_QKV = OFF_Z + QK_DIM
OFF_POOL = OFF_QKV + CONV_DIM
OFF_B = OFF_POOL + POOL_DIM
OFF_A = OFF_B + LANES
MIX_COLS = OFF_A + LANES
GATE_COLS = 2048
IN_COLS = MIX_COLS + GATE_COLS

POOL_ROWS = 16
CONV_ROWS = 8
VMEM_LIMIT = 60 * 1024 * 1024
EXPERT_PART_ROWS = 128
EXPERT_STAGES = 4
CONV_GROUPS = 12
SC_ROWS = 96


def _sigmoid(x):
    return 1.0 / (1.0 + jnp.exp(-x))


def _softplus(x):
    return jnp.maximum(x, 0.0) + jnp.log1p(jnp.exp(-jnp.abs(x)))


def _rms(x, w):
    return x * lax.rsqrt(jnp.mean(x * x, axis=-1, keepdims=True) + EPS) * w


def _dot(a, b):
    return jnp.dot(a.astype(BF16), b.astype(BF16), preferred_element_type=F32)


def _dot_nt(a, b):
    return lax.dot_general(a.astype(BF16), b.astype(BF16), (((1,), (1,)), ((), ())), preferred_element_type=F32)


def _dot_tn(a, b):
    return lax.dot_general(a.astype(BF16), b.astype(BF16), (((0,), (0,)), ((), ())), preferred_element_type=F32)


def _pack_pairs(x):
    c = x.shape[1] // 2
    hi = pltpu.bitcast(x[:, :c].astype(BF16).astype(F32), U32)
    lo = pltpu.bitcast(x[:, c:].astype(BF16).astype(F32), U32)
    return hi | lax.shift_right_logical(lo, jnp.uint32(16))


def _unpack_pairs(w):
    left = pltpu.bitcast(w & jnp.uint32(0xFFFF0000), F32)
    right = pltpu.bitcast(lax.shift_left(w, jnp.uint32(16)), F32)
    return left, right


def _inproj_kernel(x_ref, nw_ref, w_ref, mix_ref, gate_ref):
    h = _rms(x_ref[...], nw_ref[...]).astype(BF16)
    half = MIX_COLS // 2
    for c0 in (0, half):
        mix_ref[:, c0:c0 + half] = jnp.dot(h, w_ref[:, c0:c0 + half], preferred_element_type=F32)
    half = GATE_COLS // 2
    for c0 in (0, half):
        gate_ref[:, c0:c0 + half] = jnp.dot(h, w_ref[:, MIX_COLS + c0:MIX_COLS + c0 + half],
                                            preferred_element_type=F32)


def _inproj(x, norm_w, w_in_r, tm):
    n, d = x.shape
    return pl.pallas_call(
        _inproj_kernel,
        grid=(n // tm,),
        in_specs=[
            pl.BlockSpec((tm, d), lambda i: (i, 0)),
            pl.BlockSpec((1, d), lambda i: (0, 0)),
            pl.BlockSpec((d, IN_COLS), lambda i: (0, 0)),
        ],
        out_specs=[pl.BlockSpec((tm, MIX_COLS), lambda i: (i, 0)), pl.BlockSpec((tm, GATE_COLS), lambda i: (i, 0))],
        out_shape=[jax.ShapeDtypeStruct((n, MIX_COLS), F32), jax.ShapeDtypeStruct((n, GATE_COLS), F32)],
        compiler_params=pltpu.CompilerParams(dimension_semantics=("arbitrary",), vmem_limit_bytes=VMEM_LIMIT),
        name="inproj",
    )(x, norm_w, w_in_r)


def _conv_silu_qkv(ext_c, seq, tt, wconv_ref, ci, col0=0):
    base = CONV_ROWS - (CONV_WIDTH - 1)
    cs = slice(ci * LANES, (ci + 1) * LANES)
    cl = slice(ci * LANES - col0, (ci + 1) * LANES - col0)
    idx = (lambda r: (r, cl)) if seq is None else (lambda r: (seq, r, cl))
    acc = ext_c[idx(slice(base, base + tt))] * wconv_ref[0:1, cs]
    for j in range(1, CONV_WIDTH):
        acc = acc + ext_c[idx(slice(base + j, base + j + tt))] * wconv_ref[j:j + 1, cs]
    y = acc * _sigmoid(acc)
    if ci < 2 * N_HEADS:
        y = y * lax.rsqrt(jnp.sum(y * y, axis=-1, keepdims=True) + EPS)
    if ci < N_HEADS:
        y = y * (HEAD_DIM ** -0.5)
    return y


def _inproj_conv_kernel(x_ref, nw_ref, w_ref, wconv_ref, conv0_ref, mix_ref, gate_ref, nconv_ref, h_s, *ext, n_tiles):
    t = pl.program_id(1)
    tm = x_ref.shape[1]
    base = CONV_ROWS - (CONV_WIDTH - 1)
    step = CONV_DIM // len(ext)

    @pl.when(t == 0)
    def _():
        for g, buf in enumerate(ext):
            buf[base:CONV_ROWS, :] = conv0_ref[0, :, g * step:(g + 1) * step]

    h_s[...] = _rms(x_ref[0], nw_ref[...]).astype(BF16)

    def proj(c0, c1):
        return jnp.dot(h_s[...], w_ref[:, c0:c1], preferred_element_type=F32)

    def fill_z0():
        mix_ref[0, :, OFF_Z:OFF_Z + QK_DIM // 2] = proj(OFF_Z, OFF_Z + QK_DIM // 2)

    def fill_z1():
        mix_ref[0, :, OFF_Z + QK_DIM // 2:OFF_QKV] = proj(OFF_Z + QK_DIM // 2, OFF_QKV)

    def fill_pool():
        mix_ref[0, :, OFF_POOL:MIX_COLS] = proj(OFF_POOL, MIX_COLS)

    def fill_gate(i):
        q = GATE_COLS // 4
        gate_ref[0, :, i * q:(i + 1) * q] = proj(MIX_COLS + i * q, MIX_COLS + (i + 1) * q)

    fillers = [fill_z0, fill_z1, fill_pool] + [functools.partial(fill_gate, i) for i in range(4)]
    for g, buf in enumerate(ext):
        c0 = g * step
        buf[CONV_ROWS:CONV_ROWS + tm, :] = proj(OFF_QKV + c0, OFF_QKV + c0 + step)
        if fillers:
            fillers.pop(0)()
        for ci in range(c0 // LANES, (c0 + step) // LANES):
            mix_ref[0, :, OFF_QKV + ci * LANES:OFF_QKV + (ci + 1) * LANES] = _conv_silu_qkv(
                buf, None, tm, wconv_ref, ci, c0)
    for f in fillers:
        f()

    @pl.when(t == n_tiles - 1)
    def _():
        for g, buf in enumerate(ext):
            nconv_ref[0, :, g * step:(g + 1) * step] = buf[tm + base:tm + CONV_ROWS, :]

    for buf in ext:
        buf[0:CONV_ROWS, :] = buf[tm:tm + CONV_ROWS, :]


def _inproj_conv(x, conv0, norm_w, w_in_r, wconv, tm):
    n_seq, seq_len, d = x.shape
    n_tiles = seq_len // tm
    tile = lambda b, t: (b, t, 0)
    const = lambda b, t: (0, 0)
    mix, gates, nconv = pl.pallas_call(
        functools.partial(_inproj_conv_kernel, n_tiles=n_tiles),
        grid=(n_seq, n_tiles),
        in_specs=[
            pl.BlockSpec((1, tm, d), tile),
            pl.BlockSpec((1, d), const),
            pl.BlockSpec((d, IN_COLS), const),
            pl.BlockSpec((SUBLANES, CONV_DIM), const),
            pl.BlockSpec((1, CONV_WIDTH - 1, CONV_DIM), lambda b, t: (b, 0, 0)),
        ],
        out_specs=[
            pl.BlockSpec((1, tm, MIX_COLS), tile),
            pl.BlockSpec((1, tm, GATE_COLS), tile),
            pl.BlockSpec((1, CONV_WIDTH - 1, CONV_DIM), lambda b, t: (b, 0, 0)),
        ],
        out_shape=[
            jax.ShapeDtypeStruct((n_seq, seq_len, MIX_COLS), F32),
            jax.ShapeDtypeStruct((n_seq, seq_len, GATE_COLS), F32),
            jax.ShapeDtypeStruct((n_seq, CONV_WIDTH - 1, CONV_DIM), F32),
        ],
        scratch_shapes=[pltpu.VMEM((tm, d), BF16)] + [
            pltpu.VMEM((CONV_ROWS + tm, CONV_DIM // CONV_GROUPS), F32) for _ in range(CONV_GROUPS)],
        compiler_params=pltpu.CompilerParams(dimension_semantics=("arbitrary", "arbitrary"),
                                             vmem_limit_bytes=VMEM_LIMIT),
        name="inproj_conv",
    )(x, norm_w, w_in_r, wconv, conv0)
    n = n_seq * seq_len
    return mix.reshape(n, MIX_COLS), gates.reshape(n, GATE_COLS), nconv


def _mixer_kernel(proj_ref, pool0_ref, conv0_ref, s0_ref, wpool_ref, pscale_ref, wconv_ref, alog_ref, dtb_ref, onw_ref,
                  ypool_ref, ydelta_ref, npool_ref, nconv_ref, ns_ref,
                  ext_p, ext_c, q_s, k_s, v_s, g_s, b_s, state,
                  *, sb, tt, tv, chunk, pos0, n_tiles, qkv_done, caches_by_time):
    t = pl.program_id(1)
    base = CONV_ROWS - (CONV_WIDTH - 1)

    @pl.when(t == 0)
    def _():
        if caches_by_time:
            for j in range(POOL_BUF):
                ext_p[:, POOL_ROWS - POOL_BUF + j, :] = pool0_ref[j]
            for j in range(0 if qkv_done else CONV_WIDTH - 1):
                ext_c[:, base + j, :] = conv0_ref[j]
        else:
            ext_p[:, POOL_ROWS - POOL_BUF:POOL_ROWS, :] = pool0_ref[...]
            if not qkv_done:
                ext_c[:, base:CONV_ROWS, :] = conv0_ref[...]
        state[...] = s0_ref[...]

    row = lax.broadcasted_iota(jnp.int32, (tt, 1), 0)
    pos = pos0 + t * tt + row
    for s in range(sb):
        ext_p[s, POOL_ROWS:POOL_ROWS + tt, :] = proj_ref[s, :, OFF_POOL:OFF_POOL + POOL_DIM]
        if not qkv_done:
            ext_c[s, CONV_ROWS:CONV_ROWS + tt, :] = proj_ref[s, :, OFF_QKV:OFF_QKV + CONV_DIM]

        for gi, win in enumerate(POOL_WINDOWS):
            cs = slice(gi * POOL_GROUP_DIM, (gi + 1) * POOL_GROUP_DIM)
            u = ext_p[s, POOL_ROWS:POOL_ROWS + tt, cs]
            acc = u
            for j in range(1, win):
                acc = acc + ext_p[s, POOL_ROWS - j:POOL_ROWS - j + tt, cs]
            cnt = jnp.minimum(win, pos + 1).astype(F32)
            pooled = acc / cnt - u
            ypool_ref[s, :, cs] = (_dot(pooled, wpool_ref[gi]) * pscale_ref[:, cs]).astype(BF16)

        for ci in range(0 if qkv_done else CONV_DIM // LANES):
            hs = slice((ci % N_HEADS) * HEAD_DIM, (ci % N_HEADS + 1) * HEAD_DIM)
            (q_s, k_s, v_s)[ci // N_HEADS][s, :, hs] = _conv_silu_qkv(ext_c, s, tt, wconv_ref, ci)

        beta = _sigmoid(proj_ref[s, :, OFF_B:OFF_B + LANES])
        g = -jnp.exp(alog_ref[...]) * _softplus(proj_ref[s, :, OFF_A:OFF_A + LANES] + dtb_ref[...])
        if tv < tt:
            beta = jnp.where(row < tv, beta, 0.0)
            g = jnp.where(row < tv, g, 0.0)
        b_s[s] = beta
        g_s[s] = g

    @pl.when(t == n_tiles - 1)
    def _():
        if qkv_done:
            nconv_ref[...] = jnp.zeros_like(nconv_ref)
        if caches_by_time:
            for j in range(POOL_BUF):
                npool_ref[j] = ext_p[:, tv + POOL_ROWS - POOL_BUF + j, :]
            for j in range(0 if qkv_done else CONV_WIDTH - 1):
                nconv_ref[j] = ext_c[:, tv + base + j, :]
        else:
            npool_ref[...] = ext_p[:, tv + POOL_ROWS - POOL_BUF:tv + POOL_ROWS, :]
            if not qkv_done:
                nconv_ref[...] = ext_c[:, tv + base:tv + CONV_ROWS, :]

    if n_tiles > 1:
        ext_p[:, 0:POOL_ROWS, :] = ext_p[:, tt:tt + POOL_ROWS, :]
        if not qkv_done:
            ext_c[:, 0:CONV_ROWS, :] = ext_c[:, tt:tt + CONV_ROWS, :]

    ri = lax.broadcasted_iota(jnp.int32, (chunk, chunk), 0)
    cj = lax.broadcasted_iota(jnp.int32, (chunk, chunk), 1)
    causal = ri >= cj
    strict = ri > cj
    ltri = causal.astype(F32)
    eye = (ri == cj).astype(F32)
    n_doublings = chunk.bit_length() - 2
    units = [(s, h) for s in range(sb) for h in range(N_HEADS)]
    hsl = [slice(h * HEAD_DIM, (h + 1) * HEAD_DIM) for h in range(N_HEADS)]

    def qkv(which, s, rows, h):
        if qkv_done:
            c0 = OFF_QKV + which * QK_DIM + h * HEAD_DIM
            return proj_ref[s, rows, c0:c0 + HEAD_DIM]
        return (q_s, k_s, v_s)[which][s, rows, hsl[h]]

    def chunk_body(ci, carry):
        r0 = pl.multiple_of(ci * chunk, chunk)
        rows = pl.ds(r0, chunk)
        g_all, g_t, b_all = [], [], []
        for s in range(sb):
            ga = jnp.dot(ltri, g_s[s, rows, :], precision=HIGHEST, preferred_element_type=F32)
            g_all.append(ga)
            if chunk < LANES:
                ga = jnp.concatenate([ga, jnp.zeros((LANES - chunk, LANES), F32)], axis=0)
            g_t.append(ga.T)
            b_all.append(b_s[s, rows, :])
        g_col = [g_all[s][:, h:h + 1] for s, h in units]
        dec = [jnp.exp(jnp.minimum(g_col[i] - g_t[s][h:h + 1, 0:chunk], 0.0)) for i, (s, h) in enumerate(units)]
        b_col = [b_all[s][:, h:h + 1] for s, h in units]
        q = [qkv(0, s, rows, h) for s, h in units]
        k = [qkv(1, s, rows, h) for s, h in units]
        idx = range(len(units))
        kb = [k[i] * b_col[i] for i in idx]
        kbk = [_dot_nt(kb[i], k[i]) for i in idx]
        qk = [_dot_nt(q[i], k[i]) for i in idx]
        p_mat = [jnp.where(strict, kbk[i] * dec[i], 0.0) for i in idx]
        t_inv = [eye - p_mat[i] for i in idx]
        for _ in range(n_doublings):
            p_mat = [_dot(p_mat[i], p_mat[i]) for i in idx]
            xp = [_dot(t_inv[i], p_mat[i]) for i in idx]
            t_inv = [t_inv[i] + xp[i] for i in idx]
        g_exp = [jnp.exp(g_col[i]) for i in idx]
        wu = [_dot(t_inv[i], jnp.concatenate([kb[i] * g_exp[i], qkv(2, s, rows, h) * b_col[i]], axis=1))
              for i, (s, h) in enumerate(units)]
        s_old = [state[s, h] for s, h in units]
        res = [_dot(jnp.concatenate([wu[i][:, 0:HEAD_DIM], q[i] * g_exp[i]], axis=0), s_old[i]) for i in idx]
        v_new = [wu[i][:, HEAD_DIM:2 * HEAD_DIM] - res[i][0:chunk] for i in idx]
        a_qk = [jnp.where(causal, qk[i] * dec[i], 0.0) for i in idx]
        g_last = [g_all[s][chunk - 1:chunk, h:h + 1] for s, h in units]
        k_dec = [k[i] * jnp.exp(g_last[i] - g_col[i]) for i in idx]
        intra = [_dot(a_qk[i], v_new[i]) for i in idx]
        upd = [_dot_tn(k_dec[i], v_new[i]) for i in idx]
        for i, (s, h) in enumerate(units):
            state[s, h] = s_old[i] * jnp.exp(g_last[i]) + upd[i]
            o_c = res[i][chunk:2 * chunk] + intra[i]
            z_c = proj_ref[s, rows, OFF_Z + h * HEAD_DIM:OFF_Z + (h + 1) * HEAD_DIM]
            ydelta_ref[s, rows, hsl[h]] = (_rms(o_c, onw_ref[...]) * (z_c * _sigmoid(z_c))).astype(BF16)
        return carry

    lax.fori_loop(0, tt // chunk, chunk_body, 0)

    @pl.when(t == n_tiles - 1)
    def _():
        ns_ref[...] = state[...]


def _mixers(proj, pool0, conv0, s0, wpool, pscale, wconv, alog, dtb, onw, *, n_seq, sb, tt, tv, chunk, pos0, n_tiles,
            qkv_done):
    caches_by_time = sb % SUBLANES == 0
    kern = functools.partial(_mixer_kernel, sb=sb, tt=tt, tv=tv, chunk=chunk, pos0=pos0, n_tiles=n_tiles,
                             qkv_done=qkv_done, caches_by_time=caches_by_time)
    seq_len = n_tiles * tt
    tile = lambda b, t: (b, t, 0)
    seq3 = lambda b, t: (b, 0, 0)
    if caches_by_time:
        pool0, conv0 = jnp.swapaxes(pool0, 0, 1), jnp.swapaxes(conv0, 0, 1)
        cache_block = lambda rows, cols: pl.BlockSpec((rows, sb, cols), lambda b, t: (0, b, 0))
        cache_shape = lambda rows, cols: jax.ShapeDtypeStruct((rows, n_seq, cols), F32)
    else:
        cache_block = lambda rows, cols: pl.BlockSpec((sb, rows, cols), seq3)
        cache_shape = lambda rows, cols: jax.ShapeDtypeStruct((n_seq, rows, cols), F32)
    seq4 = lambda b, t: (b, 0, 0, 0)
    const2 = lambda b, t: (0, 0)
    ypool, ydelta, npool, nconv, ns = pl.pallas_call(
        kern,
        grid=(n_seq // sb, n_tiles),
        in_specs=[
            pl.BlockSpec((sb, tt, MIX_COLS), tile),
            cache_block(POOL_BUF, POOL_DIM),
            cache_block(CONV_WIDTH - 1, CONV_DIM),
            pl.BlockSpec((sb, N_HEADS, HEAD_DIM, HEAD_DIM), seq4),
            pl.BlockSpec((len(POOL_WINDOWS), POOL_GROUP_DIM, POOL_GROUP_DIM), lambda b, t: (0, 0, 0)),
            pl.BlockSpec((1, POOL_DIM), const2),
            pl.BlockSpec((SUBLANES, CONV_DIM), const2),
            pl.BlockSpec((1, LANES), const2),
            pl.BlockSpec((1, LANES), const2),
            pl.BlockSpec((1, HEAD_DIM), const2),
        ],
        out_specs=[
            pl.BlockSpec((sb, tt, POOL_DIM), tile),
            pl.BlockSpec((sb, tt, QK_DIM), tile),
            cache_block(POOL_BUF, POOL_DIM),
            cache_block(CONV_WIDTH - 1, CONV_DIM),
            pl.BlockSpec((sb, N_HEADS, HEAD_DIM, HEAD_DIM), seq4),
        ],
        out_shape=[
            jax.ShapeDtypeStruct((n_seq, seq_len, POOL_DIM), BF16),
            jax.ShapeDtypeStruct((n_seq, seq_len, QK_DIM), BF16),
            cache_shape(POOL_BUF, POOL_DIM),
            cache_shape(CONV_WIDTH - 1, CONV_DIM),
            jax.ShapeDtypeStruct((n_seq, N_HEADS, HEAD_DIM, HEAD_DIM), F32),
        ],
        scratch_shapes=[
            pltpu.VMEM((sb, POOL_ROWS + tt, POOL_DIM), F32),
            pltpu.VMEM((sb, CONV_ROWS + tt, CONV_DIM), F32),
            pltpu.VMEM((sb, tt, QK_DIM), F32),
            pltpu.VMEM((sb, tt, QK_DIM), F32),
            pltpu.VMEM((sb, tt, QK_DIM), F32),
            pltpu.VMEM((sb, tt, LANES), F32),
            pltpu.VMEM((sb, tt, LANES), F32),
            pltpu.VMEM((sb, N_HEADS, HEAD_DIM, HEAD_DIM), F32),
        ],
        compiler_params=pltpu.CompilerParams(dimension_semantics=("arbitrary", "arbitrary"),
                                             vmem_limit_bytes=VMEM_LIMIT),
        name="mixers",
    )(proj.reshape(n_seq, seq_len, MIX_COLS), pool0, conv0, s0, wpool, pscale, wconv, alog, dtb, onw)
    n = n_seq * seq_len
    if caches_by_time:
        npool, nconv = jnp.swapaxes(npool, 0, 1), jnp.swapaxes(nconv, 0, 1)
    return ypool.reshape(n, POOL_DIM), ydelta.reshape(n, QK_DIM), npool, nconv, ns


def _merge_kernel(x_ref, gates_ref, ypool_ref, ydelta_ref, wbp_ref, wbd_ref, wout_ref, n2w_ref, wr_hi_ref, wr_lo_ref,
                  br_ref, xm_ref, h2_ref, route_ref, wk_ref, cnt_ref, cnt_s, logits_s, *, period, valid):
    tm = x_ref.shape[0]
    step = pl.program_id(0)

    @pl.when(step == 0)
    def _():
        cnt_s[...] = jnp.zeros_like(cnt_s)
        logits_s[...] = jnp.zeros_like(logits_s)

    logits = logits_s[...]
    bp = _dot(ypool_ref[...], wbp_ref[...])
    bd = _dot(ydelta_ref[...], wbd_ref[...])

    lane = lax.broadcasted_iota(jnp.int32, logits.shape, 1)
    neg = -jnp.inf
    far = LANES - 1
    is_g = lane < N_GROUPS
    g_max = jnp.max(jnp.where(is_g, logits, neg), axis=-1, keepdims=True)
    g_sel = jnp.min(jnp.where(is_g & (logits == g_max), lane, far), axis=-1, keepdims=True)
    p_g = 1.0 / jnp.sum(jnp.where(is_g, jnp.exp(logits - g_max), 0.0), axis=-1, keepdims=True)
    e_lane = lane - N_GROUPS
    is_e = (e_lane >= 0) & (e_lane < N_EXPERTS) & ((e_lane // EXPERTS_PER_GROUP) == g_sel)
    ev = jnp.where(is_e, logits, neg)
    v1 = jnp.max(ev, axis=-1, keepdims=True)
    i1 = jnp.min(jnp.where(is_e & (ev == v1), lane, far), axis=-1, keepdims=True)
    is_e2 = is_e & (lane != i1)
    ev2 = jnp.where(is_e2, logits, neg)
    v2 = jnp.max(ev2, axis=-1, keepdims=True)
    i2 = jnp.min(jnp.where(is_e2 & (ev2 == v2), lane, far), axis=-1, keepdims=True)
    e21 = jnp.exp(v2 - v1)
    w1 = p_g / (1.0 + e21)
    w2 = p_g * e21 / (1.0 + e21)
    e1 = i1 - N_GROUPS
    e2 = i2 - N_GROUPS
    chosen = ((lane == e1) | (lane == e2)) & (step > 0)
    if valid < period:
        row = lax.broadcasted_iota(jnp.int32, (tm, 1), 0)
        chosen = chosen & (lax.rem(row, period) < valid)
    onehot = jnp.where(chosen, 1.0, 0.0)

    rr = lax.broadcasted_iota(jnp.int32, (tm, tm), 0)
    cc = lax.broadcasted_iota(jnp.int32, (tm, tm), 1)
    earlier = jnp.where(rr > cc, 1.0, 0.0).astype(BF16)
    before = jnp.dot(earlier, onehot.astype(BF16), preferred_element_type=F32) + cnt_s[...]
    r1 = jnp.sum(jnp.where(lane == e1, before, 0.0), axis=-1, keepdims=True)
    r2 = jnp.sum(jnp.where(lane == e2, before, 0.0), axis=-1, keepdims=True)
    cnt_s[...] = cnt_s[...] + jnp.sum(onehot, axis=0, keepdims=True)
    cnt_ref[...] = cnt_s[...]
    record = jnp.where(lane == 0, e1.astype(F32), jnp.where(lane == 1, e2.astype(F32),
                                                           jnp.where(lane == 2, r1, jnp.where(lane == 3, r2, 0.0))))
    route_ref[...] = record.T[0:SUBLANES, :]
    lane8 = lax.broadcasted_iota(jnp.int32, (tm, SUBLANES), 1)
    wk_ref[...] = jnp.where(lane8 == 0, w1, jnp.where(lane8 == 1, w2, 0.0))

    merged = _sigmoid(gates_ref[:, 0:1024]) * bp + _sigmoid(gates_ref[:, 1024:2048]) * bd
    xm = x_ref[...] + _dot(merged, wout_ref[...])
    xm_ref[...] = xm
    h2 = _rms(xm, n2w_ref[...])
    h2_ref[...] = _pack_pairs(h2)
    h2_hi = h2.astype(BF16)
    h2_lo = (h2 - h2_hi.astype(F32)).astype(BF16)
    logits_s[...] = (jnp.dot(h2_hi, wr_hi_ref[...], preferred_element_type=F32)
                     + (jnp.dot(h2_hi, wr_lo_ref[...], preferred_element_type=F32)
                        + jnp.dot(h2_lo, wr_hi_ref[...], preferred_element_type=F32))) + br_ref[...]


def _merge(x, proj, ypool, ydelta, wbp, wbd, wout, n2w, wr_hi, wr_lo, br, tm, period, valid):
    n, d = x.shape
    last = n // tm - 1
    row = lambda i: (jnp.minimum(i, last), 0)
    chosen_row = lambda i: (jnp.maximum(i - 1, 0), 0)
    const = lambda i: (0, 0)
    return pl.pallas_call(
        functools.partial(_merge_kernel, period=period, valid=valid),
        grid=(n // tm + 1,),
        in_specs=[
            pl.BlockSpec((tm, d), row),
            pl.BlockSpec((tm, GATE_COLS), row),
            pl.BlockSpec((tm, POOL_DIM), row),
            pl.BlockSpec((tm, QK_DIM), row),
            pl.BlockSpec((POOL_DIM, d), const),
            pl.BlockSpec((QK_DIM, d), const),
            pl.BlockSpec((d, d), const),
            pl.BlockSpec((1, d), const),
            pl.BlockSpec((d, LANES), const),
            pl.BlockSpec((d, LANES), const),
            pl.BlockSpec((1, LANES), const),
        ],
        out_specs=[
            pl.BlockSpec((tm, d), row),
            pl.BlockSpec((tm, d // 2), row),
            pl.BlockSpec((SUBLANES, tm), lambda i: (0, jnp.maximum(i - 1, 0))),
            pl.BlockSpec((tm, SUBLANES), chosen_row),
            pl.BlockSpec((1, LANES), const),
        ],
        out_shape=[
            jax.ShapeDtypeStruct((n, d), F32),
            jax.ShapeDtypeStruct((n, d // 2), U32),
            jax.ShapeDtypeStruct((SUBLANES, n), F32),
            jax.ShapeDtypeStruct((n, SUBLANES), F32),
            jax.ShapeDtypeStruct((1, LANES), F32),
        ],
        scratch_shapes=[pltpu.VMEM((1, LANES), F32), pltpu.VMEM((tm, LANES), F32)],
        compiler_params=pltpu.CompilerParams(dimension_semantics=("arbitrary",), vmem_limit_bytes=VMEM_LIMIT),
        name="merge",
    )(x, proj, ypool, ydelta, wbp, wbd, wout, n2w, wr_hi, wr_lo, br)


def _expert_kernel(te_ref, nv_ref, x_ref, wg_hbm, wu_hbm, wd_hbm, y_ref, st_g, st_u, st_d, wg_s, wu_s, wd_s, sem, slot_s):
    i = pl.program_id(0)
    n_tiles = pl.num_programs(0)
    n_valid = nv_ref[i]
    expert = te_ref[i]

    def expert_at(j):
        return te_ref[jnp.minimum(j, n_tiles - 1)]

    def next_expert_tile(j):
        e = expert_at(j)
        return lax.while_loop(lambda t: (t < n_tiles) & (expert_at(t) == e), lambda t: t + 1, j + 1)

    def start_weight_copies(j, slot):
        @pl.when(j < n_tiles)
        def _():
            for cp in weight_copies(expert_at(j), slot):
                cp.start()

    def weight_copies(e, slot):
        return [pltpu.make_async_copy(src.at[e], dst.at[slot], sem.at[slot])
                for src, dst in ((wg_hbm, st_g), (wu_hbm, st_u), (wd_hbm, st_d))]

    @pl.when(i == 0)
    def _():
        slot_s[0] = 0
        j = i
        for slot in range(EXPERT_STAGES - 1):
            start_weight_copies(j, slot)
            j = next_expert_tile(j)

    @pl.when((i == 0) | (expert != te_ref[jnp.maximum(i - 1, 0)]))
    def _():
        slot = slot_s[0]
        j = i
        for _ in range(EXPERT_STAGES - 1):
            j = next_expert_tile(j)
        start_weight_copies(j, (slot + EXPERT_STAGES - 1) % EXPERT_STAGES)
        for cp in weight_copies(expert, slot):
            cp.wait()
        wg_s[...] = st_g[slot].astype(BF16)
        wu_s[...] = st_u[slot].astype(BF16)
        wd_s[...] = st_d[slot].astype(BF16)
        slot_s[0] = (slot + 1) % EXPERT_STAGES

    @pl.when(n_valid > 0)
    def _():
        tm = x_ref.shape[0]
        part = min(tm, EXPERT_PART_ROWS)
        parts = [slice(r0, r0 + part) for r0 in range(0, tm, part)]
        row = lax.broadcasted_iota(jnp.int32, (part, 1), 0)
        xs = []
        for p in parts:
            left, right = _unpack_pairs(jnp.where(row + p.start < n_valid, x_ref[p, :], jnp.uint32(0)))
            xs.append(jnp.concatenate([left.astype(BF16), right.astype(BF16)], axis=1))
        a = [_dot(x, wg_s[...]) for x in xs]
        b = [_dot(x, wu_s[...]) for x in xs]
        act = [(a[j] * _sigmoid(a[j])) * b[j] for j in range(len(parts))]
        y = [_dot(act[j], wd_s[...]) for j in range(len(parts))]
        for j, p in enumerate(parts):
            y_ref[p, :] = _pack_pairs(y[j])

    @pl.when(n_valid == 0)
    def _():
        y_ref[...] = jnp.zeros_like(y_ref)


def _experts(tile_expert, tile_valid, xs, wg, wu, wd, tm):
    n_rows, dp = xs.shape
    d, f = wg.shape[1], wg.shape[2]
    grid_spec = pltpu.PrefetchScalarGridSpec(
        num_scalar_prefetch=2,
        grid=(n_rows // tm,),
        in_specs=[
            pl.BlockSpec((tm, dp), lambda i, te, nv: (jnp.where(nv[i] > 0, i, 0), 0)),
            pl.BlockSpec(memory_space=pl.ANY),
            pl.BlockSpec(memory_space=pl.ANY),
            pl.BlockSpec(memory_space=pl.ANY),
        ],
        out_specs=pl.BlockSpec((tm, dp), lambda i, te, nv: (i, 0)),
        scratch_shapes=[pltpu.VMEM((EXPERT_STAGES, d, f), F32), pltpu.VMEM((EXPERT_STAGES, d, f), F32),
                        pltpu.VMEM((EXPERT_STAGES, f, d), F32),
                        pltpu.VMEM((d, f), BF16), pltpu.VMEM((d, f), BF16), pltpu.VMEM((f, d), BF16),
                        pltpu.SemaphoreType.DMA((EXPERT_STAGES,)), pltpu.SMEM((1,), jnp.int32)],
    )
    return pl.pallas_call(
        _expert_kernel,
        grid_spec=grid_spec,
        out_shape=jax.ShapeDtypeStruct((n_rows, dp), U32),
        compiler_params=pltpu.CompilerParams(dimension_semantics=("arbitrary",), vmem_limit_bytes=VMEM_LIMIT),
        name="experts",
    )(tile_expert, tile_valid, xs, wg, wu, wd)


def _plan_kernel(route_ref, cnt_ref, pos_ref, tiles_ref, *, tm_e):
    counts = cnt_ref[...]
    padded = jnp.floor((counts + (tm_e - 1)) / tm_e) * tm_e
    lane = lax.broadcasted_iota(jnp.int32, counts.shape, 1)
    ends = padded
    shift = 1
    while shift < N_EXPERTS:
        ends = ends + jnp.where(lane >= shift, pltpu.roll(ends, shift, axis=1), 0.0)
        shift *= 2
    starts = ends - padded

    def lookup(table, key):
        out = jnp.zeros_like(key)
        for e in range(N_EXPERTS):
            out = out + jnp.where(key == e, table[:, e:e + 1], 0.0)
        return out

    for k in range(2):
        pos = lookup(starts, route_ref[k:k + 1, :]) + route_ref[2 + k:3 + k, :]
        pos_ref[k:k + 1, :] = pos.astype(jnp.int32)

    tile_start = lax.broadcasted_iota(jnp.int32, (1, tiles_ref.shape[1]), 1).astype(F32) * tm_e
    tile_expert = jnp.zeros_like(tile_start)
    for e in range(N_EXPERTS):
        tile_expert = tile_expert + jnp.where(ends[:, e:e + 1] <= tile_start, 1.0, 0.0)
    tile_expert = jnp.minimum(tile_expert, N_EXPERTS - 1.0)
    valid = lookup(counts, tile_expert) - (tile_start - lookup(starts, tile_expert))
    tiles_ref[0:1, :] = tile_expert.astype(jnp.int32)
    tiles_ref[1:2, :] = jnp.clip(valid, 0.0, tm_e).astype(jnp.int32)


def _plan(route_t, counts, tm_e, n_tiles):
    n = route_t.shape[1]
    tiles_pad = -(-n_tiles // LANES) * LANES
    return pl.pallas_call(
        functools.partial(_plan_kernel, tm_e=tm_e),
        out_shape=[jax.ShapeDtypeStruct((2, n), jnp.int32), jax.ShapeDtypeStruct((2, tiles_pad), jnp.int32)],
        name="plan",
    )(route_t, counts)


def _sc_workers():
    info = plsc.get_sparse_core_info()
    return info.num_cores, info.num_subcores


def _sc_chunk(per_worker):
    assert per_worker % SUBLANES == 0
    return max(c for c in range(SUBLANES, SC_ROWS + 1, SUBLANES) if per_worker % c == 0)


def _sc_scatter2(rows, idx_a, idx_b, n_out):
    n_cores, n_sub = _sc_workers()
    n, d = rows.shape
    per_worker = n // (n_cores * n_sub)
    assert per_worker * n_cores * n_sub == n
    chunk = _sc_chunk(per_worker)
    mesh = plsc.VectorSubcoreMesh(core_axis_name="c", subcore_axis_name="s")

    n_chunks = per_worker // chunk
    buf = lambda shape, dtype: [pltpu.VMEM(shape, dtype), pltpu.VMEM(shape, dtype)]

    @functools.partial(
        pl.kernel, mesh=mesh, out_type=jax.ShapeDtypeStruct((n_out, d), rows.dtype),
        scratch_types=buf((chunk,), jnp.int32) + buf((chunk,), jnp.int32) + buf((chunk, d), rows.dtype)
        + [pltpu.SemaphoreType.DMA, pltpu.SemaphoreType.DMA])
    def scatter_rows(rows_hbm, ia_hbm, ib_hbm, out_hbm, ia0, ia1, ib0, ib1, r0, r1, sem0, sem1):
        worker = lax.axis_index("s") * n_cores + lax.axis_index("c")
        base = worker * per_worker
        slots = ((ia0, ib0, r0, sem0), (ia1, ib1, r1, sem1))
        pending = {}
        for j in range(n_chunks):
            ia_v, ib_v, rows_v, sem = slots[j % 2]
            if j >= 2:
                for cp in pending.pop(j - 2):
                    cp.wait()
            off = pl.multiple_of(base + j * chunk, SUBLANES)
            pltpu.sync_copy(ia_hbm.at[pl.ds(off, chunk)], ia_v)
            pltpu.sync_copy(ib_hbm.at[pl.ds(off, chunk)], ib_v)
            pltpu.sync_copy(rows_hbm.at[pl.ds(off, chunk)], rows_v)
            pending[j] = (pltpu.async_copy(rows_v, out_hbm.at[ia_v], sem),
                          pltpu.async_copy(rows_v, out_hbm.at[ib_v], sem))
        for j in sorted(pending):
            for cp in pending[j]:
                cp.wait()

    return scatter_rows(rows, idx_a, idx_b)


def _sc_gather(table, idx):
    n_cores, n_sub = _sc_workers()
    n_idx = idx.shape[0]
    d = table.shape[1]
    per_worker = n_idx // (n_cores * n_sub)
    assert per_worker * n_cores * n_sub == n_idx
    chunk = _sc_chunk(per_worker)
    mesh = plsc.VectorSubcoreMesh(core_axis_name="c", subcore_axis_name="s")

    n_chunks = per_worker // chunk

    @functools.partial(
        pl.kernel, mesh=mesh, out_type=jax.ShapeDtypeStruct((n_idx, d), table.dtype),
        scratch_types=[pltpu.VMEM((chunk,), jnp.int32), pltpu.VMEM((chunk,), jnp.int32),
                       pltpu.VMEM((chunk, d), table.dtype), pltpu.VMEM((chunk, d), table.dtype),
                       pltpu.SemaphoreType.DMA, pltpu.SemaphoreType.DMA])
    def gather_rows(table_hbm, idx_hbm, out_hbm, i0, i1, r0, r1, sem0, sem1):
        worker = lax.axis_index("s") * n_cores + lax.axis_index("c")
        base = worker * per_worker
        slots = ((i0, r0, sem0), (i1, r1, sem1))

        def start(j):
            idx_v, rows_v, sem = slots[j % 2]
            off = pl.multiple_of(base + j * chunk, SUBLANES)
            pltpu.sync_copy(idx_hbm.at[pl.ds(off, chunk)], idx_v)
            return pltpu.async_copy(table_hbm.at[idx_v], rows_v, sem)

        nxt = start(0)
        for j in range(n_chunks):
            cur = nxt
            if j + 1 < n_chunks:
                nxt = start(j + 1)
            cur.wait()
            off = pl.multiple_of(base + j * chunk, SUBLANES)
            pltpu.sync_copy(slots[j % 2][1], out_hbm.at[pl.ds(off, chunk)])

    return gather_rows(table, idx)


def _finalize_kernel(xm_ref, ya_ref, yb_ref, wk_ref, fnw_ref, *rest):
    o_ref = rest[-1]
    d = xm_ref.shape[1]
    c = d // 2
    w = wk_ref[...]
    a_left, a_right = _unpack_pairs(ya_ref[...])
    b_left, b_right = _unpack_pairs(yb_ref[...])
    x_left = xm_ref[:, 0:c] + (w[:, 0:1] * a_left + w[:, 1:2] * b_left)
    x_right = xm_ref[:, c:d] + (w[:, 0:1] * a_right + w[:, 1:2] * b_right)
    ms = (jnp.sum(x_left * x_left, axis=-1, keepdims=True) + jnp.sum(x_right * x_right, axis=-1, keepdims=True)) / d
    scale = lax.rsqrt(ms + EPS)
    o_ref[:, 0:c] = x_left * scale * fnw_ref[:, 0:c]
    o_ref[:, c:d] = x_right * scale * fnw_ref[:, c:d]


def _finalize(xm, yks, wk, fnw, tm):
    n, d = xm.shape
    steps = n // len(yks) // tm
    out = None
    for p, yk in enumerate(yks):
        first = p * steps
        here = lambda i, first=first: (i + first, 0)
        in_specs = [
            pl.BlockSpec((tm, d), here),
            pl.BlockSpec((tm, d // 2), lambda i: (i, 0)),
            pl.BlockSpec((tm, d // 2), lambda i: (i + steps, 0)),
            pl.BlockSpec((tm, SUBLANES), here),
            pl.BlockSpec((1, d), lambda i: (0, 0)),
        ]
        args = [xm, yk, yk, wk, fnw]
        if out is not None:
            in_specs.append(pl.BlockSpec(memory_space=pl.ANY))
            args.append(out)
        out = pl.pallas_call(
            _finalize_kernel,
            grid=(steps,),
            in_specs=in_specs,
            out_specs=pl.BlockSpec((tm, d), here),
            out_shape=jax.ShapeDtypeStruct((n, d), F32),
            input_output_aliases={} if len(args) == 5 else {5: 0},
            compiler_params=pltpu.CompilerParams(dimension_semantics=("arbitrary",), vmem_limit_bytes=VMEM_LIMIT),
            name="finalize",
        )(*args)
    return out


def _moe(h2, route, counts, wg, wu, wd, *, tm_e, parts=1):
    n = h2.shape[0]
    n_rows = 2 * n + N_EXPERTS * tm_e
    n_tiles = n_rows // tm_e
    pos, tiles = _plan(route, counts, tm_e, n_tiles)
    xs = _sc_scatter2(h2, pos[0], pos[1], n_rows)
    y_sorted = _experts(tiles[0, :n_tiles], tiles[1, :n_tiles], xs, wg, wu, wd, tm_e)
    m = n // parts
    return [_sc_gather(y_sorted, pos[:, p * m:(p + 1) * m].reshape(-1)) for p in range(parts)]


def _in_weight_kernel(wt_ref, o_ref):
    src_ba = POOL_DIM + CONV_DIM
    src_z = src_ba + 2 * N_HEADS
    src_gates = src_z + QK_DIM
    o_ref[:, OFF_Z:OFF_Z + QK_DIM] = wt_ref[src_z:src_gates, :].T.astype(BF16)
    o_ref[:, OFF_QKV:OFF_QKV + CONV_DIM] = wt_ref[POOL_DIM:src_ba, :].T.astype(BF16)
    o_ref[:, OFF_POOL:OFF_POOL + POOL_DIM] = wt_ref[0:POOL_DIM, :].T.astype(BF16)
    ba = wt_ref[src_ba:src_ba + LANES, :].T
    head_lane = lax.broadcasted_iota(jnp.int32, ba.shape, 1) < N_HEADS
    o_ref[:, OFF_B:OFF_B + LANES] = jnp.where(head_lane, ba, 0.0).astype(BF16)
    o_ref[:, OFF_A:OFF_A + LANES] = jnp.where(head_lane, pltpu.roll(ba, LANES - N_HEADS, axis=1), 0.0).astype(BF16)
    o_ref[:, MIX_COLS:IN_COLS] = wt_ref[src_gates:src_gates + GATE_COLS, :].T.astype(BF16)


def _in_weight(wt):
    cols, d = wt.shape
    assert (POOL_DIM + CONV_DIM) % LANES == 0 and cols == POOL_DIM + CONV_DIM + 2 * N_HEADS + QK_DIM + GATE_COLS
    return pl.pallas_call(
        _in_weight_kernel,
        grid=(d // LANES,),
        in_specs=[pl.BlockSpec((cols, LANES), lambda i: (0, i))],
        out_specs=pl.BlockSpec((LANES, IN_COLS), lambda i: (i, 0)),
        out_shape=jax.ShapeDtypeStruct((d, IN_COLS), BF16),
        compiler_params=pltpu.CompilerParams(dimension_semantics=("arbitrary",), vmem_limit_bytes=VMEM_LIMIT),
        name="in_weight",
    )(wt)


def _layer(x_tokens, pool0, conv0, s0, prm, *, n_seq, sb, tt, tv, chunk, pos0, n_tiles, tm):
    conv_in_proj = tv == tt
    if conv_in_proj:
        proj, gates, nconv_proj = _inproj_conv(x_tokens.reshape(n_seq, n_tiles * tt, -1), conv0, prm["norm1_w"],
                                               prm["w_in"], prm["w_conv"], tm)
    else:
        proj, gates = _inproj(x_tokens, prm["norm1_w"], prm["w_in"], tm)
    ypool, ydelta, npool, nconv, ns = _mixers(
        proj, pool0, conv0, s0, prm["w_pool"], prm["pool_scale"], prm["w_conv"], prm["a_log"], prm["dt_bias"],
        prm["o_norm_w"], n_seq=n_seq, sb=sb, tt=tt, tv=tv, chunk=chunk, pos0=pos0, n_tiles=n_tiles,
        qkv_done=conv_in_proj)
    if conv_in_proj:
        nconv = nconv_proj
    xm, h2, route, wk, counts = _merge(
        x_tokens, gates, ypool, ydelta, prm["w_branch_pool"], prm["w_branch_delta"], prm["w_out"], prm["norm2_w"],
        prm["w_router_hi"], prm["w_router_lo"], prm["b_router"], 2 * tm, tt, tv)
    return xm, h2, route, wk, counts, npool, nconv, ns


def _pad_lanes(v, width=LANES):
    v = v.reshape(1, -1).astype(F32)
    return jnp.pad(v, ((0, 0), (0, width - v.shape[1])))


def kernel(x_prompt, x_sample, cache_pool, cache_conv, state_delta, norm1_w, w_in, w_pool, pool_scale, w_conv, a_log, dt_bias, o_norm_w, w_branch_pool, w_branch_delta, w_out, norm2_w, w_router_group, b_router_group, w_router_expert, b_router_expert, w_gate, w_up, w_down, final_norm_w):
    n_layers = norm1_w.shape[0]
    assert n_layers == 1, "single-layer step"
    bsz, seq, d = x_prompt.shape
    dbs, dseq, _ = x_sample.shape
    lyr = 0

    w_in_r = _in_weight(jnp.swapaxes(w_in[lyr], 0, 1))
    w_router = jnp.concatenate(
        [w_router_group[lyr], w_router_expert[lyr].reshape(d, N_EXPERTS),
         jnp.zeros((d, LANES - N_GROUPS - N_EXPERTS), F32)], axis=1).astype(F32)
    w_router_hi = w_router.astype(BF16)
    b_router = _pad_lanes(jnp.concatenate([b_router_group[lyr], b_router_expert[lyr].reshape(-1)]))
    prm = dict(
        norm1_w=norm1_w[lyr].reshape(1, d), w_in=w_in_r,
        w_pool=w_pool[lyr].astype(BF16), pool_scale=pool_scale[lyr].reshape(1, POOL_DIM),
        w_conv=jnp.pad(w_conv[lyr], ((0, SUBLANES - CONV_WIDTH), (0, 0))),
        a_log=_pad_lanes(a_log[lyr]), dt_bias=_pad_lanes(dt_bias[lyr]), o_norm_w=o_norm_w[lyr].reshape(1, HEAD_DIM),
        w_branch_pool=w_branch_pool[lyr].astype(BF16), w_branch_delta=w_branch_delta[lyr].astype(BF16),
        w_out=w_out[lyr].astype(BF16), norm2_w=norm2_w[lyr].reshape(1, d),
        w_router_hi=w_router_hi, w_router_lo=(w_router - w_router_hi.astype(F32)).astype(BF16), b_router=b_router,
    )
    wg, wu, wd = w_gate[lyr], w_up[lyr], w_down[lyr]
    fnw = final_norm_w.reshape(1, d)

    tt_p = TOKEN_TILE // 2
    xm, h2, route, wk, counts, npool, nconv, ns = _layer(
        x_prompt.reshape(bsz * seq, d),
        jnp.zeros((bsz, POOL_BUF, POOL_DIM), F32), jnp.zeros((bsz, CONV_WIDTH - 1, CONV_DIM), F32),
        jnp.zeros((bsz, N_HEADS, HEAD_DIM, HEAD_DIM), F32), prm,
        n_seq=bsz, sb=4, tt=tt_p, tv=tt_p, chunk=DELTA_CHUNK, pos0=0, n_tiles=seq // tt_p, tm=TOKEN_TILE)
    yk = _moe(h2, route, counts, wg, wu, wd, tm_e=2 * TOKEN_TILE, parts=2)
    y_prompt = _finalize(xm, yk, wk, fnw, 2 * TOKEN_TILE).reshape(bsz, seq, d)
    pool_p = npool[None]
    conv_p = nconv[None]
    delta_p = ns[None]

    tt_s = SUBLANES
    xs_pad = jnp.pad(x_sample, ((0, 0), (0, tt_s - dseq), (0, 0))).reshape(dbs * tt_s, d)
    xm, h2, route, wk, counts, npool, nconv, ns = _layer(
        xs_pad, cache_pool[lyr], cache_conv[lyr], state_delta[lyr], prm,
        n_seq=dbs, sb=8, tt=tt_s, tv=dseq, chunk=tt_s, pos0=PAST_LEN, n_tiles=1, tm=TOKEN_TILE)
    real = lambda a: a.reshape(dbs, tt_s, -1)[:, :dseq].reshape(dbs * dseq, -1)
    route = route.reshape(SUBLANES, dbs, tt_s)[:, :, :dseq].reshape(SUBLANES, dbs * dseq)
    yk = _moe(real(h2), route, counts, wg, wu, wd, tm_e=TOKEN_TILE // 2)
    y_sample = _finalize(real(xm), yk, real(wk), fnw, TOKEN_TILE).reshape(dbs, dseq, d)
    pool_s = npool[None]
    conv_s = nconv[None]
    delta_s = ns[None]
    return (y_prompt, y_sample, pool_p, conv_p, delta_p, pool_s, conv_s, delta_s)
```

```python
import functools

import jax
import jax.numpy as jnp
from jax import lax
from jax.experimental import pallas as pl
from jax.experimental.pallas import tpu as pltpu
from jax.experimental.pallas import tpu_sc as plsc

F32 = jnp.float32
BF16 = jnp.bfloat16
U32 = jnp.uint32
EPS = 1e-6
HIGHEST = lax.Precision.HIGHEST

POOL_WINDOWS = (2, 4, 8, 16)
POOL_GROUP_DIM = 128
POOL_DIM = len(POOL_WINDOWS) * POOL_GROUP_DIM
POOL_BUF = max(POOL_WINDOWS) - 1
N_HEADS = 8
HEAD_DIM = 128
QK_DIM = N_HEADS * HEAD_DIM
CONV_WIDTH = 4
CONV_DIM = 3 * QK_DIM
N_GROUPS = 4
EXPERTS_PER_GROUP = 8
N_EXPERTS = N_GROUPS * EXPERTS_PER_GROUP
PAST_LEN = 16384
DELTA_CHUNK = 64
TOKEN_TILE = 256
LANES = 128
SUBLANES = 8

OFF_Z = 0
OFF_QKV = OFF_Z + QK_DIM
OFF_POOL = OFF_QKV + CONV_DIM
OFF_B = OFF_POOL + POOL_DIM
OFF_A = OFF_B + LANES
MIX_COLS = OFF_A + LANES
GATE_COLS = 2048
IN_COLS = MIX_COLS + GATE_COLS

POOL_ROWS = 16
CONV_ROWS = 8
VMEM_LIMIT = 60 * 1024 * 1024
EXPERT_PART_ROWS = 128
EXPERT_STAGES = 4
CONV_GROUPS = 12
SC_ROWS = 96


def _sigmoid(x):
    return 1.0 / (1.0 + jnp.exp(-x))


def _softplus(x):
    return jnp.maximum(x, 0.0) + jnp.log1p(jnp.exp(-jnp.abs(x)))


def _rms(x, w):
    return x * lax.rsqrt(jnp.mean(x * x, axis=-1, keepdims=True) + EPS) * w


def _dot(a, b):
    return jnp.dot(a.astype(BF16), b.astype(BF16), preferred_element_type=F32)


def _dot_nt(a, b):
    return lax.dot_general(a.astype(BF16), b.astype(BF16), (((1,), (1,)), ((), ())), preferred_element_type=F32)


def _dot_tn(a, b):
    return lax.dot_general(a.astype(BF16), b.astype(BF16), (((0,), (0,)), ((), ())), preferred_element_type=F32)


def _pack_pairs(x):
    c = x.shape[1] // 2
    hi = pltpu.bitcast(x[:, :c].astype(BF16).astype(F32), U32)
    lo = pltpu.bitcast(x[:, c:].astype(BF16).astype(F32), U32)
    return hi | lax.shift_right_logical(lo, jnp.uint32(16))


def _unpack_pairs(w):
    left = pltpu.bitcast(w & jnp.uint32(0xFFFF0000), F32)
    right = pltpu.bitcast(lax.shift_left(w, jnp.uint32(16)), F32)
    return left, right


def _inproj_kernel(x_ref, nw_ref, w_ref, mix_ref, gate_ref):
    h = _rms(x_ref[...], nw_ref[...]).astype(BF16)
    half = MIX_COLS // 2
    for c0 in (0, half):
        mix_ref[:, c0:c0 + half] = jnp.dot(h, w_ref[:, c0:c0 + half], preferred_element_type=F32)
    half = GATE_COLS // 2
    for c0 in (0, half):
        gate_ref[:, c0:c0 + half] = jnp.dot(h, w_ref[:, MIX_COLS + c0:MIX_COLS + c0 + half],
                                            preferred_element_type=F32)


def _inproj(x, norm_w, w_in_r, tm):
    n, d = x.shape
    return pl.pallas_call(
        _inproj_kernel,
        grid=(n // tm,),
        in_specs=[
            pl.BlockSpec((tm, d), lambda i: (i, 0)),
            pl.BlockSpec((1, d), lambda i: (0, 0)),
            pl.BlockSpec((d, IN_COLS), lambda i: (0, 0)),
        ],
        out_specs=[pl.BlockSpec((tm, MIX_COLS), lambda i: (i, 0)), pl.BlockSpec((tm, GATE_COLS), lambda i: (i, 0))],
        out_shape=[jax.ShapeDtypeStruct((n, MIX_COLS), F32), jax.ShapeDtypeStruct((n, GATE_COLS), F32)],
        compiler_params=pltpu.CompilerParams(dimension_semantics=("arbitrary",), vmem_limit_bytes=VMEM_LIMIT),
        name="inproj",
    )(x, norm_w, w_in_r)


def _conv_silu_qkv(ext_c, seq, tt, wconv_ref, ci, col0=0):
    base = CONV_ROWS - (CONV_WIDTH - 1)
    cs = slice(ci * LANES, (ci + 1) * LANES)
    cl = slice(ci * LANES - col0, (ci + 1) * LANES - col0)
    idx = (lambda r: (r, cl)) if seq is None else (lambda r: (seq, r, cl))
    acc = ext_c[idx(slice(base, base + tt))] * wconv_ref[0:1, cs]
    for j in range(1, CONV_WIDTH):
        acc = acc + ext_c[idx(slice(base + j, base + j + tt))] * wconv_ref[j:j + 1, cs]
    y = acc * _sigmoid(acc)
    if ci < 2 * N_HEADS:
        y = y * lax.rsqrt(jnp.sum(y * y, axis=-1, keepdims=True) + EPS)
    if ci < N_HEADS:
        y = y * (HEAD_DIM ** -0.5)
    return y


def _inproj_conv_kernel(x_ref, nw_ref, w_ref, wconv_ref, conv0_ref, mix_ref, gate_ref, nconv_ref, h_s, *ext, n_tiles):
    t = pl.program_id(1)
    tm = x_ref.shape[1]
    base = CONV_ROWS - (CONV_WIDTH - 1)
    step = CONV_DIM // len(ext)

    @pl.when(t == 0)
    def _():
        for g, buf in enumerate(ext):
            buf[base:CONV_ROWS, :] = conv0_ref[0, :, g * step:(g + 1) * step]

    h_s[...] = _rms(x_ref[0], nw_ref[...]).astype(BF16)

    def proj(c0, c1):
        return jnp.dot(h_s[...], w_ref[:, c0:c1], preferred_element_type=F32)

    def fill_z0():
        mix_ref[0, :, OFF_Z:OFF_Z + QK_DIM // 2] = proj(OFF_Z, OFF_Z + QK_DIM // 2)

    def fill_z1():
        mix_ref[0, :, OFF_Z + QK_DIM // 2:OFF_QKV] = proj(OFF_Z + QK_DIM // 2, OFF_QKV)

    def fill_pool():
        mix_ref[0, :, OFF_POOL:MIX_COLS] = proj(OFF_POOL, MIX_COLS)

    def fill_gate(i):
        q = GATE_COLS // 4
        gate_ref[0, :, i * q:(i + 1) * q] = proj(MIX_COLS + i * q, MIX_COLS + (i + 1) * q)

    fillers = [fill_z0, fill_z1, fill_pool] + [functools.partial(fill_gate, i) for i in range(4)]
    for g, buf in enumerate(ext):
        c0 = g * step
        buf[CONV_ROWS:CONV_ROWS + tm, :] = proj(OFF_QKV + c0, OFF_QKV + c0 + step)
        if fillers:
            fillers.pop(0)()
        for ci in range(c0 // LANES, (c0 + step) // LANES):
            mix_ref[0, :, OFF_QKV + ci * LANES:OFF_QKV + (ci + 1) * LANES] = _conv_silu_qkv(
                buf, None, tm, wconv_ref, ci, c0)
    for f in fillers:
        f()

    @pl.when(t == n_tiles - 1)
    def _():
        for g, buf in enumerate(ext):
            nconv_ref[0, :, g * step:(g + 1) * step] = buf[tm + base:tm + CONV_ROWS, :]

    for buf in ext:
        buf[0:CONV_ROWS, :] = buf[tm:tm + CONV_ROWS, :]


def _inproj_conv(x, conv0, norm_w, w_in_r, wconv, tm):
    n_seq, seq_len, d = x.shape
    n_tiles = seq_len // tm
    tile = lambda b, t: (b, t, 0)
    const = lambda b, t: (0, 0)
    mix, gates, nconv = pl.pallas_call(
        functools.partial(_inproj_conv_kernel, n_tiles=n_tiles),
        grid=(n_seq, n_tiles),
        in_specs=[
            pl.BlockSpec((1, tm, d), tile),
            pl.BlockSpec((1, d), const),
            pl.BlockSpec((d, IN_COLS), const),
            pl.BlockSpec((SUBLANES, CONV_DIM), const),
            pl.BlockSpec((1, CONV_WIDTH - 1, CONV_DIM), lambda b, t: (b, 0, 0)),
        ],
        out_specs=[
            pl.BlockSpec((1, tm, MIX_COLS), tile),
            pl.BlockSpec((1, tm, GATE_COLS), tile),
            pl.BlockSpec((1, CONV_WIDTH - 1, CONV_DIM), lambda b, t: (b, 0, 0)),
        ],
        out_shape=[
            jax.ShapeDtypeStruct((n_seq, seq_len, MIX_COLS), F32),
            jax.ShapeDtypeStruct((n_seq, seq_len, GATE_COLS), F32),
            jax.ShapeDtypeStruct((n_seq, CONV_WIDTH - 1, CONV_DIM), F32),
        ],
        scratch_shapes=[pltpu.VMEM((tm, d), BF16)] + [
            pltpu.VMEM((CONV_ROWS + tm, CONV_DIM // CONV_GROUPS), F32) for _ in range(CONV_GROUPS)],
        compiler_params=pltpu.CompilerParams(dimension_semantics=("arbitrary", "arbitrary"),
                                             vmem_limit_bytes=VMEM_LIMIT),
        name="inproj_conv",
    )(x, norm_w, w_in_r, wconv, conv0)
    n = n_seq * seq_len
    return mix.reshape(n, MIX_COLS), gates.reshape(n, GATE_COLS), nconv


def _mixer_kernel(proj_ref, pool0_ref, conv0_ref, s0_ref, wpool_ref, pscale_ref, wconv_ref, alog_ref, dtb_ref, onw_ref,
                  ypool_ref, ydelta_ref, npool_ref, nconv_ref, ns_ref,
                  ext_p, ext_c, q_s, k_s, v_s, g_s, b_s, state,
                  *, sb, tt, tv, chunk, pos0, n_tiles, qkv_done, caches_by_time):
    t = pl.program_id(1)
    base = CONV_ROWS - (CONV_WIDTH - 1)

    @pl.when(t == 0)
    def _():
        if caches_by_time:
            for j in range(POOL_BUF):
                ext_p[:, POOL_ROWS - POOL_BUF + j, :] = pool0_ref[j]
            for j in range(0 if qkv_done else CONV_WIDTH - 1):
                ext_c[:, base + j, :] = conv0_ref[j]
        else:
            ext_p[:, POOL_ROWS - POOL_BUF:POOL_ROWS, :] = pool0_ref[...]
            if not qkv_done:
                ext_c[:, base:CONV_ROWS, :] = conv0_ref[...]
        state[...] = s0_ref[...]

    row = lax.broadcasted_iota(jnp.int32, (tt, 1), 0)
    pos = pos0 + t * tt + row
    for s in range(sb):
        ext_p[s, POOL_ROWS:POOL_ROWS + tt, :] = proj_ref[s, :, OFF_POOL:OFF_POOL + POOL_DIM]
        if not qkv_done:
            ext_c[s, CONV_ROWS:CONV_ROWS + tt, :] = proj_ref[s, :, OFF_QKV:OFF_QKV + CONV_DIM]

        for gi, win in enumerate(POOL_WINDOWS):
            cs = slice(gi * POOL_GROUP_DIM, (gi + 1) * POOL_GROUP_DIM)
            u = ext_p[s, POOL_ROWS:POOL_ROWS + tt, cs]
            acc = u
            for j in range(1, win):
                acc = acc + ext_p[s, POOL_ROWS - j:POOL_ROWS - j + tt, cs]
            cnt = jnp.minimum(win, pos + 1).astype(F32)
            pooled = acc / cnt - u
            ypool_ref[s, :, cs] = (_dot(pooled, wpool_ref[gi]) * pscale_ref[:, cs]).astype(BF16)

        for ci in range(0 if qkv_done else CONV_DIM // LANES):
            hs = slice((ci % N_HEADS) * HEAD_DIM, (ci % N_HEADS + 1) * HEAD_DIM)
            (q_s, k_s, v_s)[ci // N_HEADS][s, :, hs] = _conv_silu_qkv(ext_c, s, tt, wconv_ref, ci)

        beta = _sigmoid(proj_ref[s, :, OFF_B:OFF_B + LANES])
        g = -jnp.exp(alog_ref[...]) * _softplus(proj_ref[s, :, OFF_A:OFF_A + LANES] + dtb_ref[...])
        if tv < tt:
            beta = jnp.where(row < tv, beta, 0.0)
            g = jnp.where(row < tv, g, 0.0)
        b_s[s] = beta
        g_s[s] = g

    @pl.when(t == n_tiles - 1)
    def _():
        if qkv_done:
            nconv_ref[...] = jnp.zeros_like(nconv_ref)
        if caches_by_time:
            for j in range(POOL_BUF):
                npool_ref[j] = ext_p[:, tv + POOL_ROWS - POOL_BUF + j, :]
            for j in range(0 if qkv_done else CONV_WIDTH - 1):
                nconv_ref[j] = ext_c[:, tv + base + j, :]
        else:
            npool_ref[...] = ext_p[:, tv + POOL_ROWS - POOL_BUF:tv + POOL_ROWS, :]
            if not qkv_done:
                nconv_ref[...] = ext_c[:, tv + base:tv + CONV_ROWS, :]

    if n_tiles > 1:
        ext_p[:, 0:POOL_ROWS, :] = ext_p[:, tt:tt + POOL_ROWS, :]
        if not qkv_done:
            ext_c[:, 0:CONV_ROWS, :] = ext_c[:, tt:tt + CONV_ROWS, :]

    ri = lax.broadcasted_iota(jnp.int32, (chunk, chunk), 0)
    cj = lax.broadcasted_iota(jnp.int32, (chunk, chunk), 1)
    causal = ri >= cj
    strict = ri > cj
    ltri = causal.astype(F32)
    eye = (ri == cj).astype(F32)
    n_doublings = chunk.bit_length() - 2
    units = [(s, h) for s in range(sb) for h in range(N_HEADS)]
    hsl = [slice(h * HEAD_DIM, (h + 1) * HEAD_DIM) for h in range(N_HEADS)]

    def qkv(which, s, rows, h):
        if qkv_done:
            c0 = OFF_QKV + which * QK_DIM + h * HEAD_DIM
            return proj_ref[s, rows, c0:c0 + HEAD_DIM]
        return (q_s, k_s, v_s)[which][s, rows, hsl[h]]

    def chunk_body(ci, carry):
        r0 = pl.multiple_of(ci * chunk, chunk)
        rows = pl.ds(r0, chunk)
        g_all, g_t, b_all = [], [], []
        for s in range(sb):
            ga = jnp.dot(ltri, g_s[s, rows, :], precision=HIGHEST, preferred_element_type=F32)
            g_all.append(ga)
            if chunk < LANES:
                ga = jnp.concatenate([ga, jnp.zeros((LANES - chunk, LANES), F32)], axis=0)
            g_t.append(ga.T)
            b_all.append(b_s[s, rows, :])
        g_col = [g_all[s][:, h:h + 1] for s, h in units]
        dec = [jnp.exp(jnp.minimum(g_col[i] - g_t[s][h:h + 1, 0:chunk], 0.0)) for i, (s, h) in enumerate(units)]
        b_col = [b_all[s][:, h:h + 1] for s, h in units]
        q = [qkv(0, s, rows, h) for s, h in units]
        k = [qkv(1, s, rows, h) for s, h in units]
        idx = range(len(units))
        kb = [k[i] * b_col[i] for i in idx]
        kbk = [_dot_nt(kb[i], k[i]) for i in idx]
        qk = [_dot_nt(q[i], k[i]) for i in idx]
        p_mat = [jnp.where(strict, kbk[i] * dec[i], 0.0) for i in idx]
        t_inv = [eye - p_mat[i] for i in idx]
        for _ in range(n_doublings):
            p_mat = [_dot(p_mat[i], p_mat[i]) for i in idx]
            xp = [_dot(t_inv[i], p_mat[i]) for i in idx]
            t_inv = [t_inv[i] + xp[i] for i in idx]
        g_exp = [jnp.exp(g_col[i]) for i in idx]
        wu = [_dot(t_inv[i], jnp.concatenate([kb[i] * g_exp[i], qkv(2, s, rows, h) * b_col[i]], axis=1))
              for i, (s, h) in enumerate(units)]
        s_old = [state[s, h] for s, h in units]
        res = [_dot(jnp.concatenate([wu[i][:, 0:HEAD_DIM], q[i] * g_exp[i]], axis=0), s_old[i]) for i in idx]
        v_new = [wu[i][:, HEAD_DIM:2 * HEAD_DIM] - res[i][0:chunk] for i in idx]
        a_qk = [jnp.where(causal, qk[i] * dec[i], 0.0) for i in idx]
        g_last = [g_all[s][chunk - 1:chunk, h:h + 1] for s, h in units]
        k_dec = [k[i] * jnp.exp(g_last[i] - g_col[i]) for i in idx]
        intra = [_dot(a_qk[i], v_new[i]) for i in idx]
        upd = [_dot_tn(k_dec[i], v_new[i]) for i in idx]
        for i, (s, h) in enumerate(units):
            state[s, h] = s_old[i] * jnp.exp(g_last[i]) + upd[i]
            o_c = res[i][chunk:2 * chunk] + intra[i]
            z_c = proj_ref[s, rows, OFF_Z + h * HEAD_DIM:OFF_Z + (h + 1) * HEAD_DIM]
            ydelta_ref[s, rows, hsl[h]] = (_rms(o_c, onw_ref[...]) * (z_c * _sigmoid(z_c))).astype(BF16)
        return carry

    lax.fori_loop(0, tt // chunk, chunk_body, 0)

    @pl.when(t == n_tiles - 1)
    def _():
        ns_ref[...] = state[...]


def _mixers(proj, pool0, conv0, s0, wpool, pscale, wconv, alog, dtb, onw, *, n_seq, sb, tt, tv, chunk, pos0, n_tiles,
            qkv_done):
    caches_by_time = sb % SUBLANES == 0
    kern = functools.partial(_mixer_kernel, sb=sb, tt=tt, tv=tv, chunk=chunk, pos0=pos0, n_tiles=n_tiles,
                             qkv_done=qkv_done, caches_by_time=caches_by_time)
    seq_len = n_tiles * tt
    tile = lambda b, t: (b, t, 0)
    seq3 = lambda b, t: (b, 0, 0)
    if caches_by_time:
        pool0, conv0 = jnp.swapaxes(pool0, 0, 1), jnp.swapaxes(conv0, 0, 1)
        cache_block = lambda rows, cols: pl.BlockSpec((rows, sb, cols), lambda b, t: (0, b, 0))
        cache_shape = lambda rows, cols: jax.ShapeDtypeStruct((rows, n_seq, cols), F32)
    else:
        cache_block = lambda rows, cols: pl.BlockSpec((sb, rows, cols), seq3)
        cache_shape = lambda rows, cols: jax.ShapeDtypeStruct((n_seq, rows, cols), F32)
    seq4 = lambda b, t: (b, 0, 0, 0)
    const2 = lambda b, t: (0, 0)
    ypool, ydelta, npool, nconv, ns = pl.pallas_call(
        kern,
        grid=(n_seq // sb, n_tiles),
        in_specs=[
            pl.BlockSpec((sb, tt, MIX_COLS), tile),
            cache_block(POOL_BUF, POOL_DIM),
            cache_block(CONV_WIDTH - 1, CONV_DIM),
            pl.BlockSpec((sb, N_HEADS, HEAD_DIM, HEAD_DIM), seq4),
            pl.BlockSpec((len(POOL_WINDOWS), POOL_GROUP_DIM, POOL_GROUP_DIM), lambda b, t: (0, 0, 0)),
            pl.BlockSpec((1, POOL_DIM), const2),
            pl.BlockSpec((SUBLANES, CONV_DIM), const2),
            pl.BlockSpec((1, LANES), const2),
            pl.BlockSpec((1, LANES), const2),
            pl.BlockSpec((1, HEAD_DIM), const2),
        ],
        out_specs=[
            pl.BlockSpec((sb, tt, POOL_DIM), tile),
            pl.BlockSpec((sb, tt, QK_DIM), tile),
            cache_block(POOL_BUF, POOL_DIM),
            cache_block(CONV_WIDTH - 1, CONV_DIM),
            pl.BlockSpec((sb, N_HEADS, HEAD_DIM, HEAD_DIM), seq4),
        ],
        out_shape=[
            jax.ShapeDtypeStruct((n_seq, seq_len, POOL_DIM), BF16),
            jax.ShapeDtypeStruct((n_seq, seq_len, QK_DIM), BF16),
            cache_shape(POOL_BUF, POOL_DIM),
            cache_shape(CONV_WIDTH - 1, CONV_DIM),
            jax.ShapeDtypeStruct((n_seq, N_HEADS, HEAD_DIM, HEAD_DIM), F32),
        ],
        scratch_shapes=[
            pltpu.VMEM((sb, POOL_ROWS + tt, POOL_DIM), F32),
            pltpu.VMEM((sb, CONV_ROWS + tt, CONV_DIM), F32),
            pltpu.VMEM((sb, tt, QK_DIM), F32),
            pltpu.VMEM((sb, tt, QK_DIM), F32),
            pltpu.VMEM((sb, tt, QK_DIM), F32),
            pltpu.VMEM((sb, tt, LANES), F32),
            pltpu.VMEM((sb, tt, LANES), F32),
            pltpu.VMEM((sb, N_HEADS, HEAD_DIM, HEAD_DIM), F32),
        ],
        compiler_params=pltpu.CompilerParams(dimension_semantics=("arbitrary", "arbitrary"),
                                             vmem_limit_bytes=VMEM_LIMIT),
        name="mixers",
    )(proj.reshape(n_seq, seq_len, MIX_COLS), pool0, conv0, s0, wpool, pscale, wconv, alog, dtb, onw)
    n = n_seq * seq_len
    if caches_by_time:
        npool, nconv = jnp.swapaxes(npool, 0, 1), jnp.swapaxes(nconv, 0, 1)
    return ypool.reshape(n, POOL_DIM), ydelta.reshape(n, QK_DIM), npool, nconv, ns


def _merge_kernel(x_ref, gates_ref, ypool_ref, ydelta_ref, wbp_ref, wbd_ref, wout_ref, n2w_ref, wr_hi_ref, wr_lo_ref,
                  br_ref, xm_ref, h2_ref, route_ref, wk_ref, cnt_ref, cnt_s, logits_s, *, period, valid):
    tm = x_ref.shape[0]
    step = pl.program_id(0)

    @pl.when(step == 0)
    def _():
        cnt_s[...] = jnp.zeros_like(cnt_s)
        logits_s[...] = jnp.zeros_like(logits_s)

    logits = logits_s[...]
    bp = _dot(ypool_ref[...], wbp_ref[...])
    bd = _dot(ydelta_ref[...], wbd_ref[...])

    lane = lax.broadcasted_iota(jnp.int32, logits.shape, 1)
    neg = -jnp.inf
    far = LANES - 1
    is_g = lane < N_GROUPS
    g_max = jnp.max(jnp.where(is_g, logits, neg), axis=-1, keepdims=True)
    g_sel = jnp.min(jnp.where(is_g & (logits == g_max), lane, far), axis=-1, keepdims=True)
    p_g = 1.0 / jnp.sum(jnp.where(is_g, jnp.exp(logits - g_max), 0.0), axis=-1, keepdims=True)
    e_lane = lane - N_GROUPS
    is_e = (e_lane >= 0) & (e_lane < N_EXPERTS) & ((e_lane // EXPERTS_PER_GROUP) == g_sel)
    ev = jnp.where(is_e, logits, neg)
    v1 = jnp.max(ev, axis=-1, keepdims=True)
    i1 = jnp.min(jnp.where(is_e & (ev == v1), lane, far), axis=-1, keepdims=True)
    is_e2 = is_e & (lane != i1)
    ev2 = jnp.where(is_e2, logits, neg)
    v2 = jnp.max(ev2, axis=-1, keepdims=True)
    i2 = jnp.min(jnp.where(is_e2 & (ev2 == v2), lane, far), axis=-1, keepdims=True)
    e21 = jnp.exp(v2 - v1)
    w1 = p_g / (1.0 + e21)
    w2 = p_g * e21 / (1.0 + e21)
    e1 = i1 - N_GROUPS
    e2 = i2 - N_GROUPS
    chosen = ((lane == e1) | (lane == e2)) & (step > 0)
    if valid < period:
        row = lax.broadcasted_iota(jnp.int32, (tm, 1), 0)
        chosen = chosen & (lax.rem(row, period) < valid)
    onehot = jnp.where(chosen, 1.0, 0.0)

    rr = lax.broadcasted_iota(jnp.int32, (tm, tm), 0)
    cc = lax.broadcasted_iota(jnp.int32, (tm, tm), 1)
    earlier = jnp.where(rr > cc, 1.0, 0.0).astype(BF16)
    before = jnp.dot(earlier, onehot.astype(BF16), preferred_element_type=F32) + cnt_s[...]
    r1 = jnp.sum(jnp.where(lane == e1, before, 0.0), axis=-1, keepdims=True)
    r2 = jnp.sum(jnp.where(lane == e2, before, 0.0), axis=-1, keepdims=True)
    cnt_s[...] = cnt_s[...] + jnp.sum(onehot, axis=0, keepdims=True)
    cnt_ref[...] = cnt_s[...]
    record = jnp.where(lane == 0, e1.astype(F32), jnp.where(lane == 1, e2.astype(F32),
                                                           jnp.where(lane == 2, r1, jnp.where(lane == 3, r2, 0.0))))
    route_ref[...] = record.T[0:SUBLANES, :]
    lane8 = lax.broadcasted_iota(jnp.int32, (tm, SUBLANES), 1)
    wk_ref[...] = jnp.where(lane8 == 0, w1, jnp.where(lane8 == 1, w2, 0.0))

    merged = _sigmoid(gates_ref[:, 0:1024]) * bp + _sigmoid(gates_ref[:, 1024:2048]) * bd
    xm = x_ref[...] + _dot(merged, wout_ref[...])
    xm_ref[...] = xm
    h2 = _rms(xm, n2w_ref[...])
    h2_ref[...] = _pack_pairs(h2)
    h2_hi = h2.astype(BF16)
    h2_lo = (h2 - h2_hi.astype(F32)).astype(BF16)
    parts = jnp.dot(jnp.concatenate([h2_hi, h2_lo], axis=0),
                    jnp.concatenate([wr_hi_ref[...], wr_lo_ref[...]], axis=1), preferred_element_type=F32)
    logits_s[...] = (parts[:tm, :LANES] + (parts[:tm, LANES:] + parts[tm:, :LANES])) + br_ref[...]


def _merge(x, proj, ypool, ydelta, wbp, wbd, wout, n2w, wr_hi, wr_lo, br, tm, period, valid):
    n, d = x.shape
    last = n // tm - 1
    row = lambda i: (jnp.minimum(i, last), 0)
    chosen_row = lambda i: (jnp.maximum(i - 1, 0), 0)
    const = lambda i: (0, 0)
    return pl.pallas_call(
        functools.partial(_merge_kernel, period=period, valid=valid),
        grid=(n // tm + 1,),
        in_specs=[
            pl.BlockSpec((tm, d), row),
            pl.BlockSpec((tm, GATE_COLS), row),
            pl.BlockSpec((tm, POOL_DIM), row),
            pl.BlockSpec((tm, QK_DIM), row),
            pl.BlockSpec((POOL_DIM, d), const),
            pl.BlockSpec((QK_DIM, d), const),
            pl.BlockSpec((d, d), const),
            pl.BlockSpec((1, d), const),
            pl.BlockSpec((d, LANES), const),
            pl.BlockSpec((d, LANES), const),
            pl.BlockSpec((1, LANES), const),
        ],
        out_specs=[
            pl.BlockSpec((tm, d), row),
            pl.BlockSpec((tm, d // 2), row),
            pl.BlockSpec((SUBLANES, tm), lambda i: (0, jnp.maximum(i - 1, 0))),
            pl.BlockSpec((tm, SUBLANES), chosen_row),
            pl.BlockSpec((1, LANES), const),
        ],
        out_shape=[
            jax.ShapeDtypeStruct((n, d), F32),
            jax.ShapeDtypeStruct((n, d // 2), U32),
            jax.ShapeDtypeStruct((SUBLANES, n), F32),
            jax.ShapeDtypeStruct((n, SUBLANES), F32),
            jax.ShapeDtypeStruct((1, LANES), F32),
        ],
        scratch_shapes=[pltpu.VMEM((1, LANES), F32), pltpu.VMEM((tm, LANES), F32)],
        compiler_params=pltpu.CompilerParams(dimension_semantics=("arbitrary",), vmem_limit_bytes=VMEM_LIMIT),
        name="merge",
    )(x, proj, ypool, ydelta, wbp, wbd, wout, n2w, wr_hi, wr_lo, br)


def _expert_kernel(te_ref, nv_ref, x_ref, wg_hbm, wu_hbm, wd_hbm, y_ref, st_g, st_u, st_d, wg_s, wu_s, wd_s, sem, slot_s):
    i = pl.program_id(0)
    n_tiles = pl.num_programs(0)
    n_valid = nv_ref[i]
    expert = te_ref[i]

    def expert_at(j):
        return te_ref[jnp.minimum(j, n_tiles - 1)]

    def next_expert_tile(j):
        e = expert_at(j)
        return lax.while_loop(lambda t: (t < n_tiles) & (expert_at(t) == e), lambda t: t + 1, j + 1)

    def start_weight_copies(j, slot):
        @pl.when(j < n_tiles)
        def _():
            for cp in weight_copies(expert_at(j), slot):
                cp.start()

    def weight_copies(e, slot):
        return [pltpu.make_async_copy(src.at[e], dst.at[slot], sem.at[slot])
                for src, dst in ((wg_hbm, st_g), (wu_hbm, st_u), (wd_hbm, st_d))]

    @pl.when(i == 0)
    def _():
        slot_s[0] = 0
        j = i
        for slot in range(EXPERT_STAGES - 1):
            start_weight_copies(j, slot)
            j = next_expert_tile(j)

    @pl.when((i == 0) | (expert != te_ref[jnp.maximum(i - 1, 0)]))
    def _():
        slot = slot_s[0]
        j = i
        for _ in range(EXPERT_STAGES - 1):
            j = next_expert_tile(j)
        start_weight_copies(j, (slot + EXPERT_STAGES - 1) % EXPERT_STAGES)
        for cp in weight_copies(expert, slot):
            cp.wait()
        wg_s[...] = st_g[slot].astype(BF16)
        wu_s[...] = st_u[slot].astype(BF16)
        wd_s[...] = st_d[slot].astype(BF16)
        slot_s[0] = (slot + 1) % EXPERT_STAGES

    @pl.when(n_valid > 0)
    def _():
        tm = x_ref.shape[0]
        part = min(tm, EXPERT_PART_ROWS)
        parts = [slice(r0, r0 + part) for r0 in range(0, tm, part)]
        row = lax.broadcasted_iota(jnp.int32, (part, 1), 0)
        xs = []
        for p in parts:
            left, right = _unpack_pairs(jnp.where(row + p.start < n_valid, x_ref[p, :], jnp.uint32(0)))
            xs.append(jnp.concatenate([left.astype(BF16), right.astype(BF16)], axis=1))
        a = [_dot(x, wg_s[...]) for x in xs]
        b = [_dot(x, wu_s[...]) for x in xs]
        act = [(a[j] * _sigmoid(a[j])) * b[j] for j in range(len(parts))]
        y = [_dot(act[j], wd_s[...]) for j in range(len(parts))]
        for j, p in enumerate(parts):
            y_ref[p, :] = _pack_pairs(y[j])

    @pl.when(n_valid == 0)
    def _():
        y_ref[...] = jnp.zeros_like(y_ref)


def _experts(tile_expert, tile_valid, xs, wg, wu, wd, tm):
    n_rows, dp = xs.shape
    d, f = wg.shape[1], wg.shape[2]
    grid_spec = pltpu.PrefetchScalarGridSpec(
        num_scalar_prefetch=2,
        grid=(n_rows // tm,),
        in_specs=[
            pl.BlockSpec((tm, dp), lambda i, te, nv: (jnp.where(nv[i] > 0, i, 0), 0)),
            pl.BlockSpec(memory_space=pl.ANY),
            pl.BlockSpec(memory_space=pl.ANY),
            pl.BlockSpec(memory_space=pl.ANY),
        ],
        out_specs=pl.BlockSpec((tm, dp), lambda i, te, nv: (i, 0)),
        scratch_shapes=[pltpu.VMEM((EXPERT_STAGES, d, f), F32), pltpu.VMEM((EXPERT_STAGES, d, f), F32),
                        pltpu.VMEM((EXPERT_STAGES, f, d), F32),
                        pltpu.VMEM((d, f), BF16), pltpu.VMEM((d, f), BF16), pltpu.VMEM((f, d), BF16),
                        pltpu.SemaphoreType.DMA((EXPERT_STAGES,)), pltpu.SMEM((1,), jnp.int32)],
    )
    return pl.pallas_call(
        _expert_kernel,
        grid_spec=grid_spec,
        out_shape=jax.ShapeDtypeStruct((n_rows, dp), U32),
        compiler_params=pltpu.CompilerParams(dimension_semantics=("arbitrary",), vmem_limit_bytes=VMEM_LIMIT),
        name="experts",
    )(tile_expert, tile_valid, xs, wg, wu, wd)


def _plan_kernel(route_ref, cnt_ref, pos_ref, tiles_ref, *, tm_e):
    counts = cnt_ref[...]
    padded = jnp.floor((counts + (tm_e - 1)) / tm_e) * tm_e
    lane = lax.broadcasted_iota(jnp.int32, counts.shape, 1)
    ends = padded
    shift = 1
    while shift < N_EXPERTS:
        ends = ends + jnp.where(lane >= shift, pltpu.roll(ends, shift, axis=1), 0.0)
        shift *= 2
    starts = ends - padded

    def lookup(table, key):
        out = jnp.zeros_like(key)
        for e in range(N_EXPERTS):
            out = out + jnp.where(key == e, table[:, e:e + 1], 0.0)
        return out

    for k in range(2):
        pos = lookup(starts, route_ref[k:k + 1, :]) + route_ref[2 + k:3 + k, :]
        pos_ref[k:k + 1, :] = pos.astype(jnp.int32)

    tile_start = lax.broadcasted_iota(jnp.int32, (1, tiles_ref.shape[1]), 1).astype(F32) * tm_e
    tile_expert = jnp.zeros_like(tile_start)
    for e in range(N_EXPERTS):
        tile_expert = tile_expert + jnp.where(ends[:, e:e + 1] <= tile_start, 1.0, 0.0)
    tile_expert = jnp.minimum(tile_expert, N_EXPERTS - 1.0)
    valid = lookup(counts, tile_expert) - (tile_start - lookup(starts, tile_expert))
    tiles_ref[0:1, :] = tile_expert.astype(jnp.int32)
    tiles_ref[1:2, :] = jnp.clip(valid, 0.0, tm_e).astype(jnp.int32)


def _plan(route_t, counts, tm_e, n_tiles):
    n = route_t.shape[1]
    tiles_pad = -(-n_tiles // LANES) * LANES
    return pl.pallas_call(
        functools.partial(_plan_kernel, tm_e=tm_e),
        out_shape=[jax.ShapeDtypeStruct((2, n), jnp.int32), jax.ShapeDtypeStruct((2, tiles_pad), jnp.int32)],
        name="plan",
    )(route_t, counts)


def _sc_workers():
    info = plsc.get_sparse_core_info()
    return info.num_cores, info.num_subcores


def _sc_chunk(per_worker):
    assert per_worker % SUBLANES == 0
    return max(c for c in range(SUBLANES, SC_ROWS + 1, SUBLANES) if per_worker % c == 0)


def _sc_scatter2(rows, idx_a, idx_b, n_out):
    n_cores, n_sub = _sc_workers()
    n, d = rows.shape
    per_worker = n // (n_cores * n_sub)
    assert per_worker * n_cores * n_sub == n
    chunk = _sc_chunk(per_worker)
    mesh = plsc.VectorSubcoreMesh(core_axis_name="c", subcore_axis_name="s")

    n_chunks = per_worker // chunk
    buf = lambda shape, dtype: [pltpu.VMEM(shape, dtype), pltpu.VMEM(shape, dtype)]

    @functools.partial(
        pl.kernel, mesh=mesh, out_type=jax.ShapeDtypeStruct((n_out, d), rows.dtype),
        scratch_types=buf((chunk,), jnp.int32) + buf((chunk,), jnp.int32) + buf((chunk, d), rows.dtype)
        + [pltpu.SemaphoreType.DMA, pltpu.SemaphoreType.DMA])
    def scatter_rows(rows_hbm, ia_hbm, ib_hbm, out_hbm, ia0, ia1, ib0, ib1, r0, r1, sem0, sem1):
        worker = lax.axis_index("s") * n_cores + lax.axis_index("c")
        base = worker * per_worker
        slots = ((ia0, ib0, r0, sem0), (ia1, ib1, r1, sem1))
        pending = {}
        for j in range(n_chunks):
            ia_v, ib_v, rows_v, sem = slots[j % 2]
            if j >= 2:
                for cp in pending.pop(j - 2):
                    cp.wait()
            off = pl.multiple_of(base + j * chunk, SUBLANES)
            pltpu.sync_copy(ia_hbm.at[pl.ds(off, chunk)], ia_v)
            pltpu.sync_copy(ib_hbm.at[pl.ds(off, chunk)], ib_v)
            pltpu.sync_copy(rows_hbm.at[pl.ds(off, chunk)], rows_v)
            pending[j] = (pltpu.async_copy(rows_v, out_hbm.at[ia_v], sem),
                          pltpu.async_copy(rows_v, out_hbm.at[ib_v], sem))
        for j in sorted(pending):
            for cp in pending[j]:
                cp.wait()

    return scatter_rows(rows, idx_a, idx_b)


def _sc_gather(table, idx):
    n_cores, n_sub = _sc_workers()
    n_idx = idx.shape[0]
    d = table.shape[1]
    per_worker = n_idx // (n_cores * n_sub)
    assert per_worker * n_cores * n_sub == n_idx
    chunk = _sc_chunk(per_worker)
    mesh = plsc.VectorSubcoreMesh(core_axis_name="c", subcore_axis_name="s")

    n_chunks = per_worker // chunk

    @functools.partial(
        pl.kernel, mesh=mesh, out_type=jax.ShapeDtypeStruct((n_idx, d), table.dtype),
        scratch_types=[pltpu.VMEM((chunk,), jnp.int32), pltpu.VMEM((chunk,), jnp.int32),
                       pltpu.VMEM((chunk, d), table.dtype), pltpu.VMEM((chunk, d), table.dtype),
                       pltpu.SemaphoreType.DMA, pltpu.SemaphoreType.DMA])
    def gather_rows(table_hbm, idx_hbm, out_hbm, i0, i1, r0, r1, sem0, sem1):
        worker = lax.axis_index("s") * n_cores + lax.axis_index("c")
        base = worker * per_worker
        slots = ((i0, r0, sem0), (i1, r1, sem1))

        def start(j):
            idx_v, rows_v, sem = slots[j % 2]
            off = pl.multiple_of(base + j * chunk, SUBLANES)
            pltpu.sync_copy(idx_hbm.at[pl.ds(off, chunk)], idx_v)
            return pltpu.async_copy(table_hbm.at[idx_v], rows_v, sem)

        nxt = start(0)
        for j in range(n_chunks):
            cur = nxt
            if j + 1 < n_chunks:
                nxt = start(j + 1)
            cur.wait()
            off = pl.multiple_of(base + j * chunk, SUBLANES)
            pltpu.sync_copy(slots[j % 2][1], out_hbm.at[pl.ds(off, chunk)])

    return gather_rows(table, idx)


def _finalize_kernel(xm_ref, ya_ref, yb_ref, wk_ref, fnw_ref, o_ref):
    d = xm_ref.shape[1]
    c = d // 2
    w = wk_ref[...]
    a_left, a_right = _unpack_pairs(ya_ref[...])
    b_left, b_right = _unpack_pairs(yb_ref[...])
    x_left = xm_ref[:, 0:c] + (w[:, 0:1] * a_left + w[:, 1:2] * b_left)
    x_right = xm_ref[:, c:d] + (w[:, 0:1] * a_right + w[:, 1:2] * b_right)
    ms = (jnp.sum(x_left * x_left, axis=-1, keepdims=True) + jnp.sum(x_right * x_right, axis=-1, keepdims=True)) / d
    scale = lax.rsqrt(ms + EPS)
    o_ref[:, 0:c] = x_left * scale * fnw_ref[:, 0:c]
    o_ref[:, c:d] = x_right * scale * fnw_ref[:, c:d]


def _finalize(xm, yk, wk, fnw, tm):
    n, d = xm.shape
    steps = n // tm
    return pl.pallas_call(
        _finalize_kernel,
        grid=(steps,),
        in_specs=[
            pl.BlockSpec((tm, d), lambda i: (i, 0)),
            pl.BlockSpec((tm, d // 2), lambda i: (i, 0)),
            pl.BlockSpec((tm, d // 2), lambda i: (i + steps, 0)),
            pl.BlockSpec((tm, SUBLANES), lambda i: (i, 0)),
            pl.BlockSpec((1, d), lambda i: (0, 0)),
        ],
        out_specs=pl.BlockSpec((tm, d), lambda i: (i, 0)),
        out_shape=jax.ShapeDtypeStruct((n, d), F32),
        compiler_params=pltpu.CompilerParams(dimension_semantics=("arbitrary",), vmem_limit_bytes=VMEM_LIMIT),
        name="finalize",
    )(xm, yk, yk, wk, fnw)


def _moe(h2, route, counts, wg, wu, wd, *, tm_e):
    n = h2.shape[0]
    n_rows = 2 * n + N_EXPERTS * tm_e
    n_tiles = n_rows // tm_e
    pos, tiles = _plan(route, counts, tm_e, n_tiles)
    xs = _sc_scatter2(h2, pos[0], pos[1], n_rows)
    y_sorted = _experts(tiles[0, :n_tiles], tiles[1, :n_tiles], xs, wg, wu, wd, tm_e)
    return _sc_gather(y_sorted, pos.reshape(-1))


def _in_weight_kernel(wt_ref, o_ref):
    src_ba = POOL_DIM + CONV_DIM
    src_z = src_ba + 2 * N_HEADS
    src_gates = src_z + QK_DIM
    o_ref[:, OFF_Z:OFF_Z + QK_DIM] = wt_ref[src_z:src_gates, :].T.astype(BF16)
    o_ref[:, OFF_QKV:OFF_QKV + CONV_DIM] = wt_ref[POOL_DIM:src_ba, :].T.astype(BF16)
    o_ref[:, OFF_POOL:OFF_POOL + POOL_DIM] = wt_ref[0:POOL_DIM, :].T.astype(BF16)
    ba = wt_ref[src_ba:src_ba + LANES, :].T
    head_lane = lax.broadcasted_iota(jnp.int32, ba.shape, 1) < N_HEADS
    o_ref[:, OFF_B:OFF_B + LANES] = jnp.where(head_lane, ba, 0.0).astype(BF16)
    o_ref[:, OFF_A:OFF_A + LANES] = jnp.where(head_lane, pltpu.roll(ba, LANES - N_HEADS, axis=1), 0.0).astype(BF16)
    o_ref[:, MIX_COLS:IN_COLS] = wt_ref[src_gates:src_gates + GATE_COLS, :].T.astype(BF16)


def _in_weight(wt):
    cols, d = wt.shape
    assert (POOL_DIM + CONV_DIM) % LANES == 0 and cols == POOL_DIM + CONV_DIM + 2 * N_HEADS + QK_DIM + GATE_COLS
    return pl.pallas_call(
        _in_weight_kernel,
        grid=(d // LANES,),
        in_specs=[pl.BlockSpec((cols, LANES), lambda i: (0, i))],
        out_specs=pl.BlockSpec((LANES, IN_COLS), lambda i: (i, 0)),
        out_shape=jax.ShapeDtypeStruct((d, IN_COLS), BF16),
        compiler_params=pltpu.CompilerParams(dimension_semantics=("arbitrary",), vmem_limit_bytes=VMEM_LIMIT),
        name="in_weight",
    )(wt)


def _layer(x_tokens, pool0, conv0, s0, prm, *, n_seq, sb, tt, tv, chunk, pos0, n_tiles, tm):
    conv_in_proj = tv == tt
    if conv_in_proj:
        proj, gates, nconv_proj = _inproj_conv(x_tokens.reshape(n_seq, n_tiles * tt, -1), conv0, prm["norm1_w"],
                                               prm["w_in"], prm["w_conv"], tm)
    else:
        proj, gates = _inproj(x_tokens, prm["norm1_w"], prm["w_in"], tm)
    ypool, ydelta, npool, nconv, ns = _mixers(
        proj, pool0, conv0, s0, prm["w_pool"], prm["pool_scale"], prm["w_conv"], prm["a_log"], prm["dt_bias"],
        prm["o_norm_w"], n_seq=n_seq, sb=sb, tt=tt, tv=tv, chunk=chunk, pos0=pos0, n_tiles=n_tiles,
        qkv_done=conv_in_proj)
    if conv_in_proj:
        nconv = nconv_proj
    xm, h2, route, wk, counts = _merge(
        x_tokens, gates, ypool, ydelta, prm["w_branch_pool"], prm["w_branch_delta"], prm["w_out"], prm["norm2_w"],
        prm["w_router_hi"], prm["w_router_lo"], prm["b_router"], 2 * tm, tt, tv)
    return xm, h2, route, wk, counts, npool, nconv, ns


def _pad_lanes(v, width=LANES):
    v = v.reshape(1, -1).astype(F32)
    return jnp.pad(v, ((0, 0), (0, width - v.shape[1])))


def kernel(x_prompt, x_sample, cache_pool, cache_conv, state_delta, norm1_w, w_in, w_pool, pool_scale, w_conv, a_log, dt_bias, o_norm_w, w_branch_pool, w_branch_delta, w_out, norm2_w, w_router_group, b_router_group, w_router_expert, b_router_expert, w_gate, w_up, w_down, final_norm_w):
    n_layers = norm1_w.shape[0]
    assert n_layers == 1, "single-layer step"
    bsz, seq, d = x_prompt.shape
    dbs, dseq, _ = x_sample.shape
    lyr = 0

    w_in_r = _in_weight(jnp.swapaxes(w_in[lyr], 0, 1))
    w_router = jnp.concatenate(
        [w_router_group[lyr], w_router_expert[lyr].reshape(d, N_EXPERTS),
         jnp.zeros((d, LANES - N_GROUPS - N_EXPERTS), F32)], axis=1).astype(F32)
    w_router_hi = w_router.astype(BF16)
    b_router = _pad_lanes(jnp.concatenate([b_router_group[lyr], b_router_expert[lyr].reshape(-1)]))
    prm = dict(
        norm1_w=norm1_w[lyr].reshape(1, d), w_in=w_in_r,
        w_pool=w_pool[lyr].astype(BF16), pool_scale=pool_scale[lyr].reshape(1, POOL_DIM),
        w_conv=jnp.pad(w_conv[lyr], ((0, SUBLANES - CONV_WIDTH), (0, 0))),
        a_log=_pad_lanes(a_log[lyr]), dt_bias=_pad_lanes(dt_bias[lyr]), o_norm_w=o_norm_w[lyr].reshape(1, HEAD_DIM),
        w_branch_pool=w_branch_pool[lyr].astype(BF16), w_branch_delta=w_branch_delta[lyr].astype(BF16),
        w_out=w_out[lyr].astype(BF16), norm2_w=norm2_w[lyr].reshape(1, d),
        w_router_hi=w_router_hi, w_router_lo=(w_router - w_router_hi.astype(F32)).astype(BF16), b_router=b_router,
    )
    wg, wu, wd = w_gate[lyr], w_up[lyr], w_down[lyr]
    fnw = final_norm_w.reshape(1, d)

    tt_p = TOKEN_TILE // 2
    xm, h2, route, wk, counts, npool, nconv, ns = _layer(
        x_prompt.reshape(bsz * seq, d),
        jnp.zeros((bsz, POOL_BUF, POOL_DIM), F32), jnp.zeros((bsz, CONV_WIDTH - 1, CONV_DIM), F32),
        jnp.zeros((bsz, N_HEADS, HEAD_DIM, HEAD_DIM), F32), prm,
        n_seq=bsz, sb=4, tt=tt_p, tv=tt_p, chunk=DELTA_CHUNK, pos0=0, n_tiles=seq // tt_p, tm=TOKEN_TILE)
    yk = _moe(h2, route, counts, wg, wu, wd, tm_e=2 * TOKEN_TILE)
    y_prompt = _finalize(xm, yk, wk, fnw, 2 * TOKEN_TILE).reshape(bsz, seq, d)
    pool_p = npool[None]
    conv_p = nconv[None]
    delta_p = ns[None]

    tt_s = SUBLANES
    xs_pad = jnp.pad(x_sample, ((0, 0), (0, tt_s - dseq), (0, 0))).reshape(dbs * tt_s, d)
    xm, h2, route, wk, counts, npool, nconv, ns = _layer(
        xs_pad, cache_pool[lyr], cache_conv[lyr], state_delta[lyr], prm,
        n_seq=dbs, sb=8, tt=tt_s, tv=dseq, chunk=tt_s, pos0=PAST_LEN, n_tiles=1, tm=TOKEN_TILE)
    real = lambda a: a.reshape(dbs, tt_s, -1)[:, :dseq].reshape(dbs * dseq, -1)
    route = route.reshape(SUBLANES, dbs, tt_s)[:, :, :dseq].reshape(SUBLANES, dbs * dseq)
    yk = _moe(real(h2), route, counts, wg, wu, wd, tm_e=TOKEN_TILE // 2)
    y_sample = _finalize(real(xm), yk, real(wk), fnw, TOKEN_TILE).reshape(dbs, dseq, d)
    pool_s = npool[None]
    conv_s = nconv[None]
    delta_s = ns[None]
    return (y_prompt, y_sample, pool_p, conv_p, delta_p, pool_s, conv_s, delta_s)
```
